```python
import jax, jax.numpy as jnp
from jax import lax
import numpy as np

D_MODEL = 1024
BATCH = 32
SEQ = 2048
DEPTH = 2

HEAD_DIM = 64
N_ATTN_HEADS = 8
N_KV_HEADS = 2
GQA_GROUP = N_ATTN_HEADS // N_KV_HEADS
ATTN_WIDTH = N_ATTN_HEADS * HEAD_DIM
KV_WIDTH = N_KV_HEADS * HEAD_DIM
ROPE_DIM = HEAD_DIM // 4
ROPE_THETA = 500000.0
DILATED_PATTERNS = ((128, 1), (512, 4), (2048, 16))
ATTN_BLOCK = 128

SSM_HEAD_DIM = 64
SSM_HEADS = 16
SSM_INNER = SSM_HEADS * SSM_HEAD_DIM
SSM_GROUPS = 2
D_STATE = 128
CONV_WIDTH = 4
CHUNK = 128
CONV_CH = SSM_INNER + 2 * SSM_GROUPS * D_STATE

MIX_WIDTH = ATTN_WIDTH + SSM_INNER
Q_END = ATTN_WIDTH
K_END = Q_END + KV_WIDTH
V_END = K_END + KV_WIDTH
Z_END = V_END + SSM_INNER
XBC_END = Z_END + CONV_CH
IN_PROJ = XBC_END + SSM_HEADS

FFN_HIDDEN = ((8 * D_MODEL + 3 * 256 - 1) // (3 * 256)) * 256
EPS = 1e-5

kernel_name = "hybrid_dilated_attn_mamba2_block"


def rmsnorm(x, w):
    xf = x.astype(jnp.float32)
    y = xf * lax.rsqrt(jnp.mean(xf * xf, axis=-1, keepdims=True) + EPS)
    return (y * w.astype(jnp.float32)).astype(x.dtype)


def rotary_tables(positions, dtype):
    inv_freq = ROPE_THETA ** (-jnp.arange(0, ROPE_DIM, 2, dtype=jnp.float32) / ROPE_DIM)
    ang = positions.astype(jnp.float32)[..., None] * inv_freq
    return jnp.cos(ang)[:, :, None, :].astype(dtype), jnp.sin(ang)[:, :, None, :].astype(dtype)


def partial_rotary(t, cos, sin):
    half = ROPE_DIM // 2
    t1, t2, rest = t[..., :half], t[..., half:ROPE_DIM], t[..., ROPE_DIM:]
    return jnp.concatenate([t1 * cos - t2 * sin, t2 * cos + t1 * sin, rest], axis=-1)


def dilated_window_branch(q, k, v, window, dilation):
    bsz, s = q.shape[0], q.shape[1]
    length = s // dilation
    w_d = window // dilation
    nb = -(-length // ATTN_BLOCK)
    lp = nb * ATTN_BLOCK
    qd = q.reshape((bsz, length, dilation) + q.shape[2:])
    kd = k.reshape((bsz, length, dilation) + k.shape[2:])
    vd = v.reshape((bsz, length, dilation) + v.shape[2:])
    qd = jnp.pad(qd, [(0, 0), (0, lp - length)] + [(0, 0)] * (qd.ndim - 2))
    kv_pad = [(0, 0), (ATTN_BLOCK, lp - length)] + [(0, 0)] * (kd.ndim - 2)
    kd = jnp.pad(kd, kv_pad)
    vd = jnp.pad(vd, kv_pad)
    qb = qd.reshape((bsz, nb, ATTN_BLOCK) + qd.shape[2:])
    kb = kd.reshape((bsz, nb + 1, ATTN_BLOCK) + kd.shape[2:])
    vb = vd.reshape((bsz, nb + 1, ATTN_BLOCK) + vd.shape[2:])
    kb = jnp.concatenate([kb[:, :-1], kb[:, 1:]], axis=2)
    vb = jnp.concatenate([vb[:, :-1], vb[:, 1:]], axis=2)
    scores = jnp.einsum('bnqrhgc,bnkrhc->bnrhgqk', qb, kb).astype(jnp.float32)
    qi = jnp.arange(ATTN_BLOCK)[:, None]
    ki = jnp.arange(2 * ATTN_BLOCK)[None, :]
    delta = qi + ATTN_BLOCK - ki
    kpos = jnp.arange(nb)[:, None, None] * ATTN_BLOCK - ATTN_BLOCK + ki[None]
    valid = (delta >= 0)[None] & (delta <= w_d)[None] & (kpos >= 0)
    scores = jnp.where(valid[None, :, None, None, None], scores, -jnp.inf)
    m = jnp.max(scores, axis=-1, keepdims=True)
    p = jnp.exp(scores - m)
    den = jnp.sum(p, axis=-1, keepdims=True)
    o = jnp.einsum('bnrhgqk,bnkrhc->bnrhgqc', p, vb.astype(jnp.float32)) / den
    lse = (m + jnp.log(den))[..., 0]
    o = jnp.transpose(o, (0, 1, 5, 2, 3, 4, 6)).reshape((bsz, lp, dilation) + q.shape[2:])
    lse = jnp.transpose(lse, (0, 1, 5, 2, 3, 4)).reshape((bsz, lp, dilation) + q.shape[2:4])
    o = o[:, :length].reshape(q.shape)
    lse = lse[:, :length].reshape(q.shape[:4])
    return o, lse


def dilated_attention(q, k, v):
    outs, lses = [], []
    for window, dilation in DILATED_PATTERNS:
        o, l = dilated_window_branch(q, k, v, window, dilation)
        outs.append(o)
        lses.append(l)
    wts = jax.nn.softmax(jnp.stack(lses, axis=0), axis=0)
    return jnp.einsum('ibshg,ibshgc->bshgc', wts, jnp.stack(outs, axis=0))


def causal_depthwise_conv(u, w, b):
    y = lax.conv_general_dilated(u, w[:, None, :], window_strides=(1,),
                                 padding=[(CONV_WIDTH - 1, 0)],
                                 dimension_numbers=('NWC', 'WIO', 'NWC'),
                                 feature_group_count=u.shape[-1])
    return y + b


def segsum_exp(a):
    cs = jnp.cumsum(a, axis=-1)
    diff = cs[..., :, None] - cs[..., None, :]
    t = a.shape[-1]
    mask = jnp.tril(jnp.ones((t, t), dtype=bool))
    return jnp.exp(jnp.where(mask, diff, -jnp.inf))


def ssd_chunked(xs, dt, a_neg, bm, cm):
    bsz, s, nh, hp = xs.shape
    nc = s // CHUNK
    e = nh // SSM_GROUPS
    xg = (xs.astype(jnp.float32) * dt[..., None]).reshape(bsz, nc, CHUNK, SSM_GROUPS, e, hp)
    a = jnp.transpose((dt * a_neg).reshape(bsz, nc, CHUNK, SSM_GROUPS, e), (0, 1, 3, 4, 2))
    bc = bm.astype(jnp.float32).reshape(bsz, nc, CHUNK, SSM_GROUPS, D_STATE)
    cc = cm.astype(jnp.float32).reshape(bsz, nc, CHUNK, SSM_GROUPS, D_STATE)
    a_cs = jnp.cumsum(a, axis=-1)
    cb = jnp.einsum('bclgn,bcsgn->bcgls', cc, bc)
    m_mat = cb[:, :, :, None] * segsum_exp(a)
    y_diag = jnp.einsum('bcgels,bcsgep->bclgep', m_mat, xg)
    decay_states = jnp.exp(a_cs[..., -1:] - a_cs)
    states = jnp.einsum('bclgn,bcgel,bclgep->bcgepn', bc, decay_states, xg)
    chunk_decay = jnp.exp(a_cs[..., -1])

    def step(h, inp):
        dec, st = inp
        return h * dec[..., None, None] + st, h

    h0 = jnp.zeros((bsz, SSM_GROUPS, e, hp, D_STATE), jnp.float32)
    _, prev = lax.scan(step, h0, (jnp.moveaxis(chunk_decay, 1, 0), jnp.moveaxis(states, 1, 0)))
    prev_states = jnp.moveaxis(prev, 0, 1)
    y_off = jnp.einsum('bclgn,bcgepn,bcgel->bclgep', cc, prev_states, jnp.exp(a_cs))
    return (y_diag + y_off).reshape(bsz, s, nh, hp)


def hybrid_mixer(h, w_in, conv_w, conv_b, dt_bias, a_log, d_skip, ssm_norm, w_out, cos, sin):
    bsz, s, _ = h.shape
    proj = h @ w_in
    q, k, v, z, xbc, dt = jnp.split(proj, [Q_END, K_END, V_END, Z_END, XBC_END], axis=-1)
    q = partial_rotary(q.reshape(bsz, s, N_ATTN_HEADS, HEAD_DIM), cos, sin)
    k = partial_rotary(k.reshape(bsz, s, N_KV_HEADS, HEAD_DIM), cos, sin)
    v = v.reshape(bsz, s, N_KV_HEADS, HEAD_DIM)
    q = (q * (HEAD_DIM ** -0.5)).reshape(bsz, s, N_KV_HEADS, GQA_GROUP, HEAD_DIM)
    attn = dilated_attention(q, k, v).reshape(bsz, s, ATTN_WIDTH).astype(h.dtype)
    xbc = jax.nn.silu(causal_depthwise_conv(xbc, conv_w, conv_b))
    xs, bm, cm = jnp.split(xbc, [SSM_INNER, SSM_INNER + SSM_GROUPS * D_STATE], axis=-1)
    xs = xs.reshape(bsz, s, SSM_HEADS, SSM_HEAD_DIM)
    bm = bm.reshape(bsz, s, SSM_GROUPS, D_STATE)
    cm = cm.reshape(bsz, s, SSM_GROUPS, D_STATE)
    dt = jax.nn.softplus(dt.astype(jnp.float32) + dt_bias.astype(jnp.float32))
    a_neg = -jnp.exp(a_log.astype(jnp.float32))
    y = ssd_chunked(xs, dt, a_neg, bm, cm) + d_skip.astype(jnp.float32)[:, None] * xs.astype(jnp.float32)
    y = y.reshape(bsz, s, SSM_INNER).astype(h.dtype) * jax.nn.silu(z)
    gsize = SSM_INNER // SSM_GROUPS
    y = rmsnorm(y.reshape(bsz, s, SSM_GROUPS, gsize), ssm_norm.reshape(SSM_GROUPS, gsize))
    y = y.reshape(bsz, s, SSM_INNER)
    return jnp.concatenate([attn, y], axis=-1) @ w_out


def swiglu(h, w_gate, w_up, w_down):
    return (jax.nn.silu(h @ w_gate) * (h @ w_up)) @ w_down


def _fwd_setup_inputs(seed: int = 0) -> dict:
    key = jax.random.key(seed)
    ks = jax.random.split(key, 16)
    f32 = jnp.float32
    x = jax.random.normal(ks[0], (BATCH, SEQ, D_MODEL), f32)
    positions = jnp.broadcast_to(jnp.arange(SEQ, dtype=jnp.int32), (BATCH, SEQ))
    norm_mix = 1.0 + 0.02 * jax.random.normal(ks[1], (DEPTH, D_MODEL), f32)
    w_in = jax.random.normal(ks[2], (DEPTH, D_MODEL, IN_PROJ), f32) * D_MODEL ** -0.5
    conv_w = jax.random.normal(ks[3], (DEPTH, CONV_WIDTH, CONV_CH), f32) * CONV_WIDTH ** -0.5
    conv_b = 0.01 * jax.random.normal(ks[4], (DEPTH, CONV_CH), f32)
    dt0 = jnp.exp(jax.random.uniform(ks[5], (DEPTH, SSM_HEADS), f32, np.log(1e-3), np.log(1e-1)))
    dt_bias = dt0 + jnp.log(-jnp.expm1(-dt0))
    a_log = jnp.log(jax.random.uniform(ks[6], (DEPTH, SSM_HEADS), f32, 1.0, 16.0))
    d_skip = 1.0 + 0.1 * jax.random.normal(ks[7], (DEPTH, SSM_HEADS), f32)
    ssm_norm = 1.0 + 0.02 * jax.random.normal(ks[8], (DEPTH, SSM_INNER), f32)
    w_out = jax.random.normal(ks[9], (DEPTH, MIX_WIDTH, D_MODEL), f32) * MIX_WIDTH ** -0.5
    norm_ffn = 1.0 + 0.02 * jax.random.normal(ks[10], (DEPTH, D_MODEL), f32)
    w_gate = jax.random.normal(ks[11], (DEPTH, D_MODEL, FFN_HIDDEN), f32) * D_MODEL ** -0.5
    w_up = jax.random.normal(ks[12], (DEPTH, D_MODEL, FFN_HIDDEN), f32) * D_MODEL ** -0.5
    w_down = jax.random.normal(ks[13], (DEPTH, FFN_HIDDEN, D_MODEL), f32) * FFN_HIDDEN ** -0.5
    final_norm = 1.0 + 0.02 * jax.random.normal(ks[14], (D_MODEL,), f32)
    return {"x": x, "positions": positions, "norm_mix": norm_mix, "w_in": w_in,
            "conv_w": conv_w, "conv_b": conv_b, "dt_bias": dt_bias, "a_log": a_log,
            "d_skip": d_skip, "ssm_norm": ssm_norm, "w_out": w_out, "norm_ffn": norm_ffn,
            "w_gate": w_gate, "w_up": w_up, "w_down": w_down, "final_norm": final_norm}


def _fwd_reference(x, positions, norm_mix, w_in, conv_w, conv_b, dt_bias, a_log, d_skip, ssm_norm,
              w_out, norm_ffn, w_gate, w_up, w_down, final_norm):
    cos, sin = rotary_tables(positions, x.dtype)
    h = x
    for layer in range(DEPTH):
        h = h + hybrid_mixer(rmsnorm(h, norm_mix[layer]), w_in[layer], conv_w[layer],
                             conv_b[layer], dt_bias[layer], a_log[layer], d_skip[layer],
                             ssm_norm[layer], w_out[layer], cos, sin)
        h = h + swiglu(rmsnorm(h, norm_ffn[layer]), w_gate[layer], w_up[layer], w_down[layer])
    return rmsnorm(h, final_norm)


import jax as _jax
import jax.numpy as _jnp

TWIN_FORMAT = 'train_step'
FWD_PARAMS = ['x', 'positions', 'norm_mix', 'w_in', 'conv_w', 'conv_b', 'dt_bias', 'a_log', 'd_skip', 'ssm_norm', 'w_out', 'norm_ffn', 'w_gate', 'w_up', 'w_down', 'final_norm']
TWIN_WEIGHTS = ['norm_mix', 'w_in', 'conv_w', 'conv_b', 'dt_bias', 'a_log', 'd_skip', 'ssm_norm', 'w_out', 'norm_ffn', 'w_gate', 'w_up', 'w_down', 'final_norm']
TWIN_DIFF_INPUT = 'x'
TWIN_INPUTS = ['x', 'positions', 'norm_mix', 'w_in', 'conv_w', 'conv_b', 'dt_bias', 'a_log', 'd_skip', 'ssm_norm', 'w_out', 'norm_ffn', 'w_gate', 'w_up', 'w_down', 'final_norm', 'loss_target', 'm_norm_mix', 'm_w_in', 'm_conv_w', 'm_conv_b', 'm_dt_bias', 'm_a_log', 'm_d_skip', 'm_ssm_norm', 'm_w_out', 'm_norm_ffn', 'm_w_gate', 'm_w_up', 'm_w_down', 'm_final_norm', 'v_norm_mix', 'v_w_in', 'v_conv_w', 'v_conv_b', 'v_dt_bias', 'v_a_log', 'v_d_skip', 'v_ssm_norm', 'v_w_out', 'v_norm_ffn', 'v_w_gate', 'v_w_up', 'v_w_down', 'v_final_norm']
TWIN_OUTPUTS = ['loss', 'grad_x', 'grad_norm_mix', 'grad_w_in', 'grad_conv_w', 'grad_conv_b', 'grad_dt_bias', 'grad_a_log', 'grad_d_skip', 'grad_ssm_norm', 'grad_w_out', 'grad_norm_ffn', 'grad_w_gate', 'grad_w_up', 'grad_w_down', 'grad_final_norm', 'delta_norm_mix', 'delta_w_in', 'delta_conv_w', 'delta_conv_b', 'delta_dt_bias', 'delta_a_log', 'delta_d_skip', 'delta_ssm_norm', 'delta_w_out', 'delta_norm_ffn', 'delta_w_gate', 'delta_w_up', 'delta_w_down', 'delta_final_norm', 'new_m_norm_mix', 'new_m_w_in', 'new_m_conv_w', 'new_m_conv_b', 'new_m_dt_bias', 'new_m_a_log', 'new_m_d_skip', 'new_m_ssm_norm', 'new_m_w_out', 'new_m_norm_ffn', 'new_m_w_gate', 'new_m_w_up', 'new_m_w_down', 'new_m_final_norm', 'new_v_norm_mix', 'new_v_w_in', 'new_v_conv_w', 'new_v_conv_b', 'new_v_dt_bias', 'new_v_a_log', 'new_v_d_skip', 'new_v_ssm_norm', 'new_v_w_out', 'new_v_norm_ffn', 'new_v_w_gate', 'new_v_w_up', 'new_v_w_down', 'new_v_final_norm']
TWIN_LEAF_KINDS = {'loss': 'loss', 'grad_x': 'grad_x', 'grad_norm_mix': 'grad_w', 'grad_w_in': 'grad_w', 'grad_conv_w': 'grad_w', 'grad_conv_b': 'grad_w', 'grad_dt_bias': 'grad_w', 'grad_a_log': 'grad_w', 'grad_d_skip': 'grad_w', 'grad_ssm_norm': 'grad_w', 'grad_w_out': 'grad_w', 'grad_norm_ffn': 'grad_w', 'grad_w_gate': 'grad_w', 'grad_w_up': 'grad_w', 'grad_w_down': 'grad_w', 'grad_final_norm': 'grad_w', 'delta_norm_mix': 'delta_w', 'delta_w_in': 'delta_w', 'delta_conv_w': 'delta_w', 'delta_conv_b': 'delta_w', 'delta_dt_bias': 'delta_w', 'delta_a_log': 'delta_w', 'delta_d_skip': 'delta_w', 'delta_ssm_norm': 'delta_w', 'delta_w_out': 'delta_w', 'delta_norm_ffn': 'delta_w', 'delta_w_gate': 'delta_w', 'delta_w_up': 'delta_w', 'delta_w_down': 'delta_w', 'delta_final_norm': 'delta_w', 'new_m_norm_mix': 'new_m', 'new_m_w_in': 'new_m', 'new_m_conv_w': 'new_m', 'new_m_conv_b': 'new_m', 'new_m_dt_bias': 'new_m', 'new_m_a_log': 'new_m', 'new_m_d_skip': 'new_m', 'new_m_ssm_norm': 'new_m', 'new_m_w_out': 'new_m', 'new_m_norm_ffn': 'new_m', 'new_m_w_gate': 'new_m', 'new_m_w_up': 'new_m', 'new_m_w_down': 'new_m', 'new_m_final_norm': 'new_m', 'new_v_norm_mix': 'new_v', 'new_v_w_in': 'new_v', 'new_v_conv_w': 'new_v', 'new_v_conv_b': 'new_v', 'new_v_dt_bias': 'new_v', 'new_v_a_log': 'new_v', 'new_v_d_skip': 'new_v', 'new_v_ssm_norm': 'new_v', 'new_v_w_out': 'new_v', 'new_v_norm_ffn': 'new_v', 'new_v_w_gate': 'new_v', 'new_v_w_up': 'new_v', 'new_v_w_down': 'new_v', 'new_v_final_norm': 'new_v'}


def _forward(args):
    return _fwd_reference(*[args[k] for k in FWD_PARAMS])


def _output_shape():
    out = _jax.eval_shape(lambda: _forward(_fwd_setup_inputs(0)))
    return out.shape, out.dtype

N_MICROBATCH = 1
ADAM_LR = 0.001
ADAM_B1 = 0.9
ADAM_B2 = 0.999
ADAM_EPS = 1e-08
ADAM_WD = 0.01
ADAM_STEP = 10
PER_EXAMPLE_BATCH_AXIS = {'x': 0, 'positions': 0, 'loss_target': 0}
SHARED_INPUTS = []
_WEIGHT_DTYPES = {'norm_mix': _jnp.float32, 'w_in': _jnp.float32, 'conv_w': _jnp.float32, 'conv_b': _jnp.float32, 'dt_bias': _jnp.float32, 'a_log': _jnp.float32, 'd_skip': _jnp.float32, 'ssm_norm': _jnp.float32, 'w_out': _jnp.float32, 'norm_ffn': _jnp.float32, 'w_gate': _jnp.float32, 'w_up': _jnp.float32, 'w_down': _jnp.float32, 'final_norm': _jnp.float32}
MOMENT_SCALE = {'norm_mix': 2.760249e-01, 'w_in': 1.473368e-01, 'conv_w': 1.584071e-01, 'conv_b': 2.224589e-01, 'dt_bias': 3.948716e-01, 'a_log': 5.005316e-01, 'd_skip': 8.159636e-01, 'ssm_norm': 1.735080e-01, 'w_out': 1.765994e-01, 'norm_ffn': 1.545493e-01, 'w_gate': 6.712799e-02, 'w_up': 6.518062e-02, 'w_down': 1.077152e-01, 'final_norm': 6.392534e+01}


def _to_microbatches(a, axis):
    t = _jnp.moveaxis(a, axis, 0)
    t = t.reshape((N_MICROBATCH, t.shape[0] // N_MICROBATCH) + t.shape[1:])
    return _jnp.moveaxis(t, 1, axis + 1)


def setup_inputs(seed: int = 0) -> dict:
    inp = _fwd_setup_inputs(seed)
    key = _jax.random.fold_in(_jax.random.key(seed), 7919)
    shape, _ = _output_shape()
    out = dict(inp)
    out["loss_target"] = _jax.random.normal(_jax.random.fold_in(key, 0), shape, _jnp.float32)
    for i, name in enumerate(TWIN_WEIGHTS):
        w = inp[name].astype(_jnp.float32)
        if MOMENT_SCALE is None:
            s = _jnp.sqrt(_jnp.mean(_jnp.square(w)) + 1e-30)
        else:
            s = MOMENT_SCALE[name]
        km, kv = _jax.random.split(_jax.random.fold_in(key, i + 1))
        out[name] = w
        out["m_" + name] = s * _jax.random.normal(km, w.shape, _jnp.float32)
        out["v_" + name] = (s * s) * _jax.random.uniform(kv, w.shape, _jnp.float32, 0.5, 1.5)
    if N_MICROBATCH > 1:
        for name, axis in PER_EXAMPLE_BATCH_AXIS.items():
            out[name] = _to_microbatches(out[name], axis)
    return {'x': out['x'], 'positions': out['positions'], 'norm_mix': out['norm_mix'], 'w_in': out['w_in'], 'conv_w': out['conv_w'], 'conv_b': out['conv_b'], 'dt_bias': out['dt_bias'], 'a_log': out['a_log'], 'd_skip': out['d_skip'], 'ssm_norm': out['ssm_norm'], 'w_out': out['w_out'], 'norm_ffn': out['norm_ffn'], 'w_gate': out['w_gate'], 'w_up': out['w_up'], 'w_down': out['w_down'], 'final_norm': out['final_norm'], 'loss_target': out['loss_target'], 'm_norm_mix': out['m_norm_mix'], 'm_w_in': out['m_w_in'], 'm_conv_w': out['m_conv_w'], 'm_conv_b': out['m_conv_b'], 'm_dt_bias': out['m_dt_bias'], 'm_a_log': out['m_a_log'], 'm_d_skip': out['m_d_skip'], 'm_ssm_norm': out['m_ssm_norm'], 'm_w_out': out['m_w_out'], 'm_norm_ffn': out['m_norm_ffn'], 'm_w_gate': out['m_w_gate'], 'm_w_up': out['m_w_up'], 'm_w_down': out['m_w_down'], 'm_final_norm': out['m_final_norm'], 'v_norm_mix': out['v_norm_mix'], 'v_w_in': out['v_w_in'], 'v_conv_w': out['v_conv_w'], 'v_conv_b': out['v_conv_b'], 'v_dt_bias': out['v_dt_bias'], 'v_a_log': out['v_a_log'], 'v_d_skip': out['v_d_skip'], 'v_ssm_norm': out['v_ssm_norm'], 'v_w_out': out['v_w_out'], 'v_norm_ffn': out['v_norm_ffn'], 'v_w_gate': out['v_w_gate'], 'v_w_up': out['v_w_up'], 'v_w_down': out['v_w_down'], 'v_final_norm': out['v_final_norm']}


def _loss(weights, diff, rest, loss_target):
    with _jax.named_scope("forward"):
        args = {**rest, TWIN_DIFF_INPUT: diff, **{k: w.astype(_WEIGHT_DTYPES[k]) for k, w in weights.items()}}
        y = _forward(args)
    with _jax.named_scope("loss_head"):
        err = _jnp.square(y.astype(_jnp.float32) - loss_target)
        return 0.5 * _jnp.sum(_jnp.mean(err, axis=-1)) if err.ndim else 0.5 * err


def _adamw(w, g, m, v):
    m = ADAM_B1 * m + (1.0 - ADAM_B1) * g
    v = ADAM_B2 * v + (1.0 - ADAM_B2) * _jnp.square(g)
    m_hat = m / (1.0 - ADAM_B1 ** ADAM_STEP)
    v_hat = v / (1.0 - ADAM_B2 ** ADAM_STEP)
    delta = -ADAM_LR * (m_hat / (_jnp.sqrt(v_hat) + ADAM_EPS) + ADAM_WD * w)
    return delta, m, v


def reference(x, positions, norm_mix, w_in, conv_w, conv_b, dt_bias, a_log, d_skip, ssm_norm, w_out, norm_ffn, w_gate, w_up, w_down, final_norm, loss_target, m_norm_mix, m_w_in, m_conv_w, m_conv_b, m_dt_bias, m_a_log, m_d_skip, m_ssm_norm, m_w_out, m_norm_ffn, m_w_gate, m_w_up, m_w_down, m_final_norm, v_norm_mix, v_w_in, v_conv_w, v_conv_b, v_dt_bias, v_a_log, v_d_skip, v_ssm_norm, v_w_out, v_norm_ffn, v_w_gate, v_w_up, v_w_down, v_final_norm):
    given = dict(x=x, positions=positions, norm_mix=norm_mix, w_in=w_in, conv_w=conv_w, conv_b=conv_b, dt_bias=dt_bias, a_log=a_log, d_skip=d_skip, ssm_norm=ssm_norm, w_out=w_out, norm_ffn=norm_ffn, w_gate=w_gate, w_up=w_up, w_down=w_down, final_norm=final_norm, loss_target=loss_target, m_norm_mix=m_norm_mix, m_w_in=m_w_in, m_conv_w=m_conv_w, m_conv_b=m_conv_b, m_dt_bias=m_dt_bias, m_a_log=m_a_log, m_d_skip=m_d_skip, m_ssm_norm=m_ssm_norm, m_w_out=m_w_out, m_norm_ffn=m_norm_ffn, m_w_gate=m_w_gate, m_w_up=m_w_up, m_w_down=m_w_down, m_final_norm=m_final_norm, v_norm_mix=v_norm_mix, v_w_in=v_w_in, v_conv_w=v_conv_w, v_conv_b=v_conv_b, v_dt_bias=v_dt_bias, v_a_log=v_a_log, v_d_skip=v_d_skip, v_ssm_norm=v_ssm_norm, v_w_out=v_w_out, v_norm_ffn=v_norm_ffn, v_w_gate=v_w_gate, v_w_up=v_w_up, v_w_down=v_w_down, v_final_norm=v_final_norm)
    weights = {n: given[n] for n in TWIN_WEIGHTS}
    shared = {n: given[n] for n in SHARED_INPUTS}
    per_example = {n: given[n] for n in ['x', 'positions']}
    grad_fn = _jax.value_and_grad(_loss, argnums=(0, 1))

    def one_microbatch(ex, loss_target):
        ex = dict(ex)
        diff = ex.pop(TWIN_DIFF_INPUT)
        return grad_fn(weights, diff, {**shared, **ex}, loss_target)

    if N_MICROBATCH == 1:
        loss, (grad_w, grad_x) = one_microbatch(per_example, given["loss_target"])
    else:
        def body(carry, xs):
            loss_sum, grad_sum = carry
            l_k, (gw_k, gx_k) = one_microbatch(xs[0], xs[1])
            with _jax.named_scope("update"):
                return (loss_sum + l_k, _jax.tree.map(_jnp.add, grad_sum, gw_k)), gx_k

        init = (_jnp.zeros((), _jnp.float32), _jax.tree.map(_jnp.zeros_like, weights))
        (loss, grad_w), grad_x = _jax.lax.scan(body, init, (per_example, given["loss_target"]))
    with _jax.named_scope("update"):
        delta_w, new_m, new_v = {}, {}, {}
        for n in TWIN_WEIGHTS:
            delta_w[n], new_m[n], new_v[n] = _adamw(weights[n], grad_w[n], given["m_" + n], given["v_" + n])
    return (loss, grad_x, *[grad_w[n] for n in TWIN_WEIGHTS], *[delta_w[n] for n in TWIN_WEIGHTS],
            *[new_m[n] for n in TWIN_WEIGHTS], *[new_v[n] for n in TWIN_WEIGHTS])
```

```python
import functools

import jax
import jax.numpy as jnp
from jax import lax
from jax.experimental import pallas as pl
from jax.experimental.pallas import tpu as pltpu

F32 = jnp.float32
BF16 = jnp.bfloat16
MESH = pl.DeviceIdType.MESH

D_MODEL = 1024
DEPTH = 2
HEAD_DIM = 64
N_Q_HEADS = 8
N_KV_HEADS = 2
GQA = N_Q_HEADS // N_KV_HEADS
ATTN_WIDTH = N_Q_HEADS * HEAD_DIM
ROPE_DIM = HEAD_DIM // 4
ROPE_HALF = ROPE_DIM // 2
ROPE_THETA = 500000.0
DILATIONS = (1, 4, 16)
ATTN_BLOCK = 128
SSM_P = 64
SSM_HEADS = 16
SSM_INNER = SSM_HEADS * SSM_P
SSM_GROUPS = 2
HEADS_PER_GROUP = SSM_HEADS // SSM_GROUPS
D_STATE = 128
CONV_WIDTH = 4
CHUNK = 128
CONV_CH = SSM_INNER + 2 * SSM_GROUPS * D_STATE
MIX_WIDTH = ATTN_WIDTH + SSM_INNER
Q_END = ATTN_WIDTH
K_END = Q_END + N_KV_HEADS * HEAD_DIM
V_END = K_END + N_KV_HEADS * HEAD_DIM
Z_END = V_END + SSM_INNER
XBC_END = Z_END + CONV_CH
IN_PROJ = XBC_END + SSM_HEADS
LANE = 128
IN_PAD = XBC_END + LANE
FFN_HIDDEN = 2816
EPS = 1e-5
ADAM_LR, ADAM_B1, ADAM_B2, ADAM_EPS, ADAM_WD, ADAM_STEP = 0.001, 0.9, 0.999, 1e-8, 0.01, 10
N_CHIPS = 4
N_DEV = 8
VMEM_LIMIT = 48 * 1024 * 1024
NEG_BIG = -1e30


def _params(sem=None):
    return pltpu.CompilerParams(dimension_semantics=sem, vmem_limit_bytes=VMEM_LIMIT)


def _pick(n, prefs):
    for p in prefs:
        if n % p == 0:
            return p
    return n


def matmul(a, b, *, name, ta=False, tb=False, out_dtype=F32, residual=None):
    if ta:
        kdim, m = a.shape
    else:
        m, kdim = a.shape
    n = b.shape[0] if tb else b.shape[1]
    tm = _pick(m, (512, 384, 256, 128))
    tn = _pick(n, (512, 384, 256, 128))
    tk = _pick(kdim, (512, 384, 256, 128))
    nk = kdim // tk
    a_spec = pl.BlockSpec((tk, tm), lambda i, j, k: (k, i)) if ta else pl.BlockSpec((tm, tk), lambda i, j, k: (i, k))
    b_spec = pl.BlockSpec((tn, tk), lambda i, j, k: (j, k)) if tb else pl.BlockSpec((tk, tn), lambda i, j, k: (k, j))
    o_spec = pl.BlockSpec((tm, tn), lambda i, j, k: (i, j))
    dims = (((0 if ta else 1,), (1 if tb else 0,)), ((), ()))
    has_res = residual is not None

    def body(*refs):
        if has_res:
            a_ref, b_ref, r_ref, o_ref, acc = refs
        else:
            a_ref, b_ref, o_ref, acc = refs
        k = pl.program_id(2)

        @pl.when(k == 0)
        def _():
            acc[...] = jnp.zeros_like(acc)

        acc[...] += lax.dot_general(a_ref[...].astype(BF16), b_ref[...].astype(BF16), dims,
                                    preferred_element_type=F32)

        @pl.when(k == nk - 1)
        def _():
            r = acc[...]
            if has_res:
                r = r + r_ref[...]
            o_ref[...] = r.astype(out_dtype)

    in_specs = [a_spec, b_spec] + ([o_spec] if has_res else [])
    args = (a, b) + ((residual,) if has_res else ())
    return pl.pallas_call(
        body, name=name, grid=(m // tm, n // tn, nk), in_specs=in_specs, out_specs=o_spec,
        out_shape=jax.ShapeDtypeStruct((m, n), out_dtype),
        scratch_shapes=[pltpu.VMEM((tm, tn), F32)],
        compiler_params=_params(("parallel", "parallel", "arbitrary")),
    )(*args)


ROW_BLOCK_BYTES = 8 * 1024 * 1024


def _row_tile(t, tr, arrays, groups, n_copies):
    lanes = sum(-(-(a.shape[1] // groups) // LANE) * LANE for a in arrays) * n_copies
    tr = min(tr, t)
    while tr > 8 and tr * lanes * 4 > ROW_BLOCK_BYTES:
        tr //= 2
    return tr


def rowwise_fwd(fn, rows, params, out_dtypes, *, name, tr=512, groups=1):
    t = rows[0].shape[0]
    tr = _row_tile(t, tr, rows, groups, 2)
    row_spec = lambda a: pl.BlockSpec((tr, a.shape[1] // groups), lambda g, i: (i, g))
    par_spec = lambda p: pl.BlockSpec((1, p.shape[1] // groups), lambda g, i: (0, g))
    n_in = len(rows) + len(params)
    out_cols = [o.shape[1] for o in jax.eval_shape(
        fn, *[jax.ShapeDtypeStruct((tr, a.shape[1] // groups), F32) for a in rows],
        *[jax.ShapeDtypeStruct((1, p.shape[1] // groups), F32) for p in params])]

    def body(*refs):
        vals = [r[...].astype(F32) for r in refs[:n_in]]
        outs = fn(*vals)
        for o_ref, o in zip(refs[n_in:], outs):
            o_ref[...] = o.astype(o_ref.dtype)

    return pl.pallas_call(
        body, name=name, grid=(groups, t // tr),
        in_specs=[row_spec(a) for a in rows] + [par_spec(p) for p in params],
        out_specs=[pl.BlockSpec((tr, c), lambda g, i: (i, g)) for c in out_cols],
        out_shape=[jax.ShapeDtypeStruct((t, c * groups), d) for c, d in zip(out_cols, out_dtypes)],
        compiler_params=_params(("arbitrary", "arbitrary")),
    )(*rows, *params)


def rowwise_bwd(fn, rows, params, cts, drow_dtypes, *, name, tr=512, groups=1, add_to_first=None):
    t = rows[0].shape[0]
    tr = _row_tile(t, tr, list(rows) + list(cts), groups, 2)
    row_spec = lambda a: pl.BlockSpec((tr, a.shape[1] // groups), lambda g, i: (i, g))
    par_spec = lambda p: pl.BlockSpec((1, p.shape[1] // groups), lambda g, i: (0, g))
    n_rows, n_par, n_ct = len(rows), len(params), len(cts)
    has_add = add_to_first is not None
    n_in = n_rows + n_par + n_ct + (1 if has_add else 0)

    def body(*refs):
        i = pl.program_id(1)
        vals = [r[...].astype(F32) for r in refs[:n_rows + n_par]]
        ct_vals = tuple(r[...].astype(F32) for r in refs[n_rows + n_par:n_rows + n_par + n_ct])
        _, vjp = jax.vjp(fn, *vals)
        grads = vjp(ct_vals)
        out_refs = refs[n_in:]
        for idx in range(n_rows):
            g = grads[idx]
            if idx == 0 and has_add:
                g = g + refs[n_in - 1][...]
            out_refs[idx][...] = g.astype(out_refs[idx].dtype)
        for idx in range(n_par):
            p_ref = out_refs[n_rows + idx]

            @pl.when(i == 0)
            def _():
                p_ref[...] = jnp.zeros_like(p_ref)

            p_ref[...] += grads[n_rows + idx]

    ins = list(rows) + list(params) + list(cts) + ([add_to_first] if has_add else [])
    in_specs = ([row_spec(a) for a in rows] + [par_spec(p) for p in params] + [row_spec(a) for a in cts]
                + ([row_spec(add_to_first)] if has_add else []))
    return pl.pallas_call(
        body, name=name, grid=(groups, t // tr), in_specs=in_specs,
        out_specs=[row_spec(a) for a in rows] + [par_spec(p) for p in params],
        out_shape=[jax.ShapeDtypeStruct(a.shape, d) for a, d in zip(rows, drow_dtypes)]
        + [jax.ShapeDtypeStruct(p.shape, F32) for p in params],
        compiler_params=_params(("arbitrary", "arbitrary")),
    )(*ins)


def rms_fn(x, w):
    return (x * lax.rsqrt(jnp.mean(x * x, axis=-1, keepdims=True) + EPS) * w,)


def swiglu_fn(g, u):
    return (g * jax.nn.sigmoid(g) * u,)


def gated_norm_fn(y, z, w):
    v = y * (z * jax.nn.sigmoid(z))
    return (v * lax.rsqrt(jnp.mean(v * v, axis=-1, keepdims=True) + EPS) * w,)


def combine_fn(o1, o2, o3, l1, l2, l3):
    m = jnp.maximum(jnp.maximum(l1, l2), l3)
    e1, e2, e3 = jnp.exp(l1 - m), jnp.exp(l2 - m), jnp.exp(l3 - m)
    inv = 1.0 / (e1 + e2 + e3)
    return ((e1 * inv) * o1 + (e2 * inv) * o2 + (e3 * inv) * o3,)


def loss_and_grad(h, target, w, *, tr=512):
    t, d = h.shape

    def loss_fn(hv, wv, tv):
        err = rms_fn(hv, wv)[0] - tv
        per_row = jnp.mean(err * err, axis=-1, keepdims=True)
        return 0.5 * jnp.sum(per_row, axis=0, keepdims=True)

    def body(h_ref, t_ref, w_ref, dh_ref, dw_ref, loss_ref):
        i = pl.program_id(0)

        @pl.when(i == 0)
        def _():
            dw_ref[...] = jnp.zeros_like(dw_ref)
            loss_ref[...] = jnp.zeros_like(loss_ref)

        tv = t_ref[...]
        val, vjp = jax.vjp(lambda hv, wv: loss_fn(hv, wv, tv), h_ref[...], w_ref[...])
        dh, dw = vjp(jnp.ones((1, 1), F32))
        dh_ref[...] = dh
        dw_ref[...] += dw
        loss_ref[...] += jnp.broadcast_to(val, loss_ref.shape)

    row = pl.BlockSpec((tr, d), lambda i: (i, 0))
    par = pl.BlockSpec((1, d), lambda i: (0, 0))
    return pl.pallas_call(
        body, name="loss_and_grad", grid=(t // tr,), in_specs=[row, row, par],
        out_specs=[row, par, pl.BlockSpec((1, LANE), lambda i: (0, 0))],
        out_shape=[jax.ShapeDtypeStruct((t, d), F32), jax.ShapeDtypeStruct((1, d), F32),
                   jax.ShapeDtypeStruct((1, LANE), F32)],
        compiler_params=_params(("arbitrary",)),
    )(h, target, w)


def _split3(x):
    hi = x.astype(BF16)
    r1 = x - hi.astype(F32)
    mid = r1.astype(BF16)
    lo = (r1 - mid.astype(F32)).astype(BF16)
    return hi, mid, lo


def _dot01_left(m01, x):
    return sum(jnp.dot(m01, p, preferred_element_type=F32) for p in _split3(x))


def _dot01_right(x, m01):
    return sum(jnp.dot(p, m01, preferred_element_type=F32) for p in _split3(x))


def rotary(xs_list, cosf, sinf, scale, *, adjoint, name, ts=512):
    b, h, s, c = xs_list[0].shape
    n_x = len(xs_list)

    def body(*refs):
        x = refs[0][0, 0]
        for r in refs[1:n_x]:
            x = x + r[0, 0]
        cos_v, sin_v = refs[n_x][0], refs[n_x + 1][0]
        o_ref = refs[n_x + 2]
        ci = lax.broadcasted_iota(jnp.int32, (c, c), 0)
        cj = lax.broadcasted_iota(jnp.int32, (c, c), 1)
        swap = ((cj == ci + ROPE_HALF) & (ci < ROPE_HALF)) | ((cj == ci - ROPE_HALF) & (ci >= ROPE_HALF) & (ci < ROPE_DIM))
        swap = swap.astype(BF16)
        if adjoint:
            out = x * cos_v + _dot01_right(x * sin_v, swap)
        else:
            out = x * cos_v + _dot01_right(x, swap) * sin_v
        o_ref[0, 0] = out * scale

    x_spec = pl.BlockSpec((1, 1, ts, c), lambda bi, hi, si: (bi, hi, si, 0))
    t_spec = pl.BlockSpec((1, ts, c), lambda bi, hi, si: (bi, si, 0))
    return pl.pallas_call(
        body, name=name, grid=(b, h, s // ts), in_specs=[x_spec] * n_x + [t_spec, t_spec], out_specs=x_spec,
        out_shape=jax.ShapeDtypeStruct((b, h, s, c), F32),
        compiler_params=_params(("parallel", "parallel", "parallel")),
    )(*xs_list, cosf, sinf)


def add3(a, b, c, *, name, tr=1024):
    def fn(x, y, z):
        return (x + y + z,)
    return rowwise_fwd(fn, [a, b, c], [], [F32], name=name, tr=tr)[0]


def _attn_mask(n):
    rows = GQA * ATTN_BLOCK
    qi = lax.broadcasted_iota(jnp.int32, (rows, 2 * ATTN_BLOCK), 0) % ATTN_BLOCK
    ki = lax.broadcasted_iota(jnp.int32, (rows, 2 * ATTN_BLOCK), 1)
    delta = qi + ATTN_BLOCK - ki
    return (delta >= 0) & (delta <= ATTN_BLOCK) & ((n - 1) * ATTN_BLOCK + ki >= 0)


def _attn_specs(l):
    q_spec = pl.BlockSpec((1, GQA, ATTN_BLOCK, HEAD_DIM), lambda p, n: (p, 0, n, 0))
    l_spec = pl.BlockSpec((1, GQA, ATTN_BLOCK, 1), lambda p, n: (p, 0, n, 0))
    kprev = pl.BlockSpec((1, ATTN_BLOCK, HEAD_DIM), lambda p, n: (p, jnp.maximum(n - 1, 0), 0))
    kcur = pl.BlockSpec((1, ATTN_BLOCK, HEAD_DIM), lambda p, n: (p, n, 0))
    kfull = pl.BlockSpec((1, l, HEAD_DIM), lambda p, n: (p, 0, 0))
    return q_spec, l_spec, kprev, kcur, kfull


def attn_branch_fwd(q, k, v, *, name):
    p_cnt, _, l, _ = q.shape
    rows = GQA * ATTN_BLOCK
    q_spec, l_spec, kprev, kcur, _ = _attn_specs(l)

    def body(q_ref, kp_ref, kc_ref, vp_ref, vc_ref, o_ref, lse_ref):
        n = pl.program_id(1)
        qv = q_ref[0].reshape(rows, HEAD_DIM).astype(BF16)
        kk = jnp.concatenate([kp_ref[0], kc_ref[0]], axis=0).astype(BF16)
        vv = jnp.concatenate([vp_ref[0], vc_ref[0]], axis=0).astype(BF16)
        s = lax.dot_general(qv, kk, (((1,), (1,)), ((), ())), preferred_element_type=F32)
        s = jnp.where(_attn_mask(n), s, NEG_BIG)
        m = jnp.max(s, axis=-1, keepdims=True)
        pr = jnp.exp(s - m)
        den = jnp.sum(pr, axis=-1, keepdims=True)
        o = jnp.dot(pr.astype(BF16), vv, preferred_element_type=F32) / den
        o_ref[0] = o.reshape(GQA, ATTN_BLOCK, HEAD_DIM)
        lse_ref[0] = (m + jnp.log(den)).reshape(GQA, ATTN_BLOCK, 1)

    return pl.pallas_call(
        body, name=name, grid=(p_cnt, l // ATTN_BLOCK), in_specs=[q_spec, kprev, kcur, kprev, kcur],
        out_specs=[q_spec, l_spec],
        out_shape=[jax.ShapeDtypeStruct(q.shape, F32), jax.ShapeDtypeStruct(q.shape[:3] + (1,), F32)],
        compiler_params=_params(("parallel", "arbitrary")),
    )(q, k, k, v, v)


def attn_branch_bwd(q, k, v, o, lse, do, dlse, *, name):
    p_cnt, _, l, _ = q.shape
    rows = GQA * ATTN_BLOCK
    q_spec, l_spec, kprev, kcur, kfull = _attn_specs(l)

    def body(q_ref, kp_ref, kc_ref, vp_ref, vc_ref, o_ref, lse_ref, do_ref, dlse_ref, dq_ref, dk_ref, dv_ref):
        n = pl.program_id(1)

        @pl.when(n == 0)
        def _():
            dk_ref[...] = jnp.zeros_like(dk_ref)
            dv_ref[...] = jnp.zeros_like(dv_ref)

        qv = q_ref[0].reshape(rows, HEAD_DIM).astype(BF16)
        kk = jnp.concatenate([kp_ref[0], kc_ref[0]], axis=0).astype(BF16)
        vv = jnp.concatenate([vp_ref[0], vc_ref[0]], axis=0).astype(BF16)
        ov = o_ref[0].reshape(rows, HEAD_DIM)
        dov = do_ref[0].reshape(rows, HEAD_DIM)
        lsev = lse_ref[0].reshape(rows, 1)
        dlsev = dlse_ref[0].reshape(rows, 1)
        s = lax.dot_general(qv, kk, (((1,), (1,)), ((), ())), preferred_element_type=F32)
        pr = jnp.where(_attn_mask(n), jnp.exp(s - lsev), 0.0)
        do16 = dov.astype(BF16)
        dv = lax.dot_general(pr.astype(BF16), do16, (((0,), (0,)), ((), ())), preferred_element_type=F32)
        dp = lax.dot_general(do16, vv, (((1,), (1,)), ((), ())), preferred_element_type=F32)
        delta = jnp.sum(dov * ov, axis=-1, keepdims=True)
        ds = (pr * (dp - delta + dlsev)).astype(BF16)
        dq = jnp.dot(ds, kk, preferred_element_type=F32)
        dk = lax.dot_general(ds, qv, (((0,), (0,)), ((), ())), preferred_element_type=F32)
        dq_ref[0] = dq.reshape(GQA, ATTN_BLOCK, HEAD_DIM)
        cur = pl.ds(pl.multiple_of(n * ATTN_BLOCK, ATTN_BLOCK), ATTN_BLOCK)
        dk_ref[0, cur, :] += dk[ATTN_BLOCK:]
        dv_ref[0, cur, :] += dv[ATTN_BLOCK:]

        @pl.when(n > 0)
        def _():
            prev = pl.ds(pl.multiple_of((n - 1) * ATTN_BLOCK, ATTN_BLOCK), ATTN_BLOCK)
            dk_ref[0, prev, :] += dk[:ATTN_BLOCK]
            dv_ref[0, prev, :] += dv[:ATTN_BLOCK]

    return pl.pallas_call(
        body, name=name, grid=(p_cnt, l // ATTN_BLOCK),
        in_specs=[q_spec, kprev, kcur, kprev, kcur, q_spec, l_spec, q_spec, l_spec],
        out_specs=[q_spec, kfull, kfull],
        out_shape=[jax.ShapeDtypeStruct(q.shape, F32), jax.ShapeDtypeStruct(k.shape, F32),
                   jax.ShapeDtypeStruct(v.shape, F32)],
        compiler_params=_params(("parallel", "arbitrary")),
    )(q, k, k, v, v, o, lse, do, dlse)


CONV_TC = 256
CONV_COL0 = Z_END // CONV_TC


def _shift_down(u, s):
    if s == 0:
        return u
    rows = lax.broadcasted_iota(jnp.int32, u.shape, 0)
    return jnp.where(rows >= s, pltpu.roll(u, s, 0), 0.0)


def _shift_up(u, s):
    if s == 0:
        return u
    n = u.shape[0]
    rows = lax.broadcasted_iota(jnp.int32, u.shape, 0)
    return jnp.where(rows < n - s, pltpu.roll(u, n - s, 0), 0.0)


def conv_silu_fwd(proj3, w, bias, *, name):
    b, s, _ = proj3.shape
    u_spec = pl.BlockSpec((1, s, CONV_TC), lambda j, bi: (bi, 0, CONV_COL0 + j))
    o_spec = pl.BlockSpec((1, s, CONV_TC), lambda j, bi: (bi, 0, j))
    w_spec = pl.BlockSpec((CONV_WIDTH, CONV_TC), lambda j, bi: (0, j))
    b_spec = pl.BlockSpec((1, CONV_TC), lambda j, bi: (0, j))

    def body(u_ref, w_ref, b_ref, o_ref):
        u = u_ref[0]
        y = jnp.broadcast_to(b_ref[...], u.shape)
        for k in range(CONV_WIDTH):
            y = y + w_ref[k:k + 1, :] * _shift_down(u, CONV_WIDTH - 1 - k)
        o_ref[0] = y * jax.nn.sigmoid(y)

    return pl.pallas_call(
        body, name=name, grid=(CONV_CH // CONV_TC, b), in_specs=[u_spec, w_spec, b_spec], out_specs=o_spec,
        out_shape=jax.ShapeDtypeStruct((b, s, CONV_CH), F32),
        compiler_params=_params(("parallel", "arbitrary")),
    )(proj3, w, bias)


def conv_silu_bwd(proj3, w, bias, dact, *, name):
    b, s, _ = proj3.shape
    u_spec = pl.BlockSpec((1, s, CONV_TC), lambda j, bi: (bi, 0, CONV_COL0 + j))
    o_spec = pl.BlockSpec((1, s, CONV_TC), lambda j, bi: (bi, 0, j))
    w_spec = pl.BlockSpec((CONV_WIDTH, CONV_TC), lambda j, bi: (0, j))
    b_spec = pl.BlockSpec((1, CONV_TC), lambda j, bi: (0, j))

    def body(u_ref, w_ref, b_ref, g_ref, du_ref, dw_ref, db_ref):
        bi = pl.program_id(1)

        @pl.when(bi == 0)
        def _():
            dw_ref[...] = jnp.zeros_like(dw_ref)
            db_ref[...] = jnp.zeros_like(db_ref)

        u = u_ref[0]
        y = jnp.broadcast_to(b_ref[...], u.shape)
        shifted = [_shift_down(u, CONV_WIDTH - 1 - k) for k in range(CONV_WIDTH)]
        for k in range(CONV_WIDTH):
            y = y + w_ref[k:k + 1, :] * shifted[k]
        sig = jax.nn.sigmoid(y)
        dy = g_ref[0] * (sig * (1.0 + y * (1.0 - sig)))
        du = jnp.zeros_like(u)
        for k in range(CONV_WIDTH):
            du = du + w_ref[k:k + 1, :] * _shift_up(dy, CONV_WIDTH - 1 - k)
            dw_ref[k:k + 1, :] += jnp.sum(dy * shifted[k], axis=0, keepdims=True)
        du_ref[0] = du
        db_ref[...] += jnp.sum(dy, axis=0, keepdims=True)

    return pl.pallas_call(
        body, name=name, grid=(CONV_CH // CONV_TC, b), in_specs=[u_spec, w_spec, b_spec, o_spec],
        out_specs=[o_spec, w_spec, b_spec],
        out_shape=[jax.ShapeDtypeStruct((b, s, CONV_CH), F32), jax.ShapeDtypeStruct((CONV_WIDTH, CONV_CH), F32),
                   jax.ShapeDtypeStruct((1, CONV_CH), F32)],
        compiler_params=_params(("parallel", "arbitrary")),
    )(proj3, w, bias, dact)


def _softplus(z):
    e = jnp.exp(-jnp.abs(z))
    u = 1.0 + e
    log1p = jnp.where(u == 1.0, e, jnp.log(u) * e / jnp.where(u == 1.0, 1.0, u - 1.0))
    return jnp.maximum(z, 0.0) + log1p


def _tri(lower):
    r = lax.broadcasted_iota(jnp.int32, (CHUNK, CHUNK), 0)
    c = lax.broadcasted_iota(jnp.int32, (CHUNK, CHUNK), 1)
    return (r >= c) if lower else (r <= c)


def _ssd_common(dtr_ref, dtb_ref, alog_ref):
    z = dtr_ref[0] + dtb_ref[...]
    dt = _softplus(z)
    aneg = -jnp.exp(alog_ref[...])
    acs = _dot01_left(_tri(True).astype(BF16), dt * aneg)
    return z, dt, aneg, acs


def _col(mat, onehot):
    return jnp.sum(mat * onehot, axis=1, keepdims=True)


def _ssd_head(x, dt_j, acs_j, cb, tri_mask, last_row):
    acs_last = jnp.sum(acs_j * last_row, axis=0, keepdims=True)
    xg = x * dt_j
    bc = jnp.broadcast_to(acs_j, (CHUNK, CHUNK))
    dm = bc - bc.T
    lm = jnp.where(tri_mask, jnp.exp(jnp.where(tri_mask, dm, 0.0)), 0.0)
    mm = cb * lm
    decay_s = jnp.exp(acs_last - acs_j)
    return acs_last, xg, lm, mm, decay_s


def _ssd_specs(nc, reverse):
    cidx = (lambda c: nc - 1 - c) if reverse else (lambda c: c)
    x_spec = pl.BlockSpec((1, HEADS_PER_GROUP, CHUNK, SSM_P), lambda b, c, g: (b, g, cidx(c), 0))
    bc_spec = pl.BlockSpec((1, 1, CHUNK, D_STATE), lambda b, c, g: (b, g, cidx(c), 0))
    dt_spec = pl.BlockSpec((1, CHUNK, LANE), lambda b, c, g: (b, cidx(c), 0))
    par_spec = pl.BlockSpec((1, LANE), lambda b, c, g: (0, 0))
    h_spec = pl.BlockSpec((1, HEADS_PER_GROUP, 1, SSM_P, D_STATE), lambda b, c, g: (b, g, cidx(c), 0, 0))
    return x_spec, bc_spec, dt_spec, par_spec, h_spec


def ssd_fwd(xs, bm, cm, dtr, dtb, alog, dsk, *, name):
    b, _, s, _ = xs.shape
    nc = s // CHUNK
    x_spec, bc_spec, dt_spec, par_spec, h_spec = _ssd_specs(nc, False)

    def body(x_ref, b_ref, c_ref, dtr_ref, dtb_ref, alog_ref, dsk_ref, y_ref, hp_ref, state):
        c, g = pl.program_id(1), pl.program_id(2)

        @pl.when(c == 0)
        def _():
            state[pl.ds(g * HEADS_PER_GROUP, HEADS_PER_GROUP)] = jnp.zeros((HEADS_PER_GROUP, SSM_P, D_STATE), F32)

        _, dt, _, acs = _ssd_common(dtr_ref, dtb_ref, alog_ref)
        b16, c16 = b_ref[0, 0].astype(BF16), c_ref[0, 0].astype(BF16)
        cb = lax.dot_general(c16, b16, (((1,), (1,)), ((), ())), preferred_element_type=F32)
        tri_mask = _tri(True)
        last_row = (lax.broadcasted_iota(jnp.int32, (CHUNK, 1), 0) == CHUNK - 1).astype(F32)
        lanes = lax.broadcasted_iota(jnp.int32, (1, LANE), 1)
        for j in range(HEADS_PER_GROUP):
            hidx = g * HEADS_PER_GROUP + j
            onehot = (lanes == hidx).astype(F32)
            x = x_ref[0, j]
            dt_j, acs_j = _col(dt, onehot), _col(acs, onehot)
            acs_last, xg, _, mm, decay_s = _ssd_head(x, dt_j, acs_j, cb, tri_mask, last_row)
            xg16 = xg.astype(BF16)
            y_diag = jnp.dot(mm.astype(BF16), xg16, preferred_element_type=F32)
            st = lax.dot_general((xg * decay_s).astype(BF16), b16, (((0,), (0,)), ((), ())), preferred_element_type=F32)
            hp = state[hidx]
            hp_ref[0, j, 0] = hp
            y_off = lax.dot_general(c16, hp.astype(BF16), (((1,), (1,)), ((), ())), preferred_element_type=F32)
            d_j = jnp.sum(dsk_ref[...] * onehot, axis=1, keepdims=True)
            y_ref[0, j] = y_diag + y_off * jnp.exp(acs_j) + d_j * x
            state[hidx] = hp * jnp.exp(acs_last) + st

    return pl.pallas_call(
        body, name=name, grid=(b, nc, SSM_GROUPS),
        in_specs=[x_spec, bc_spec, bc_spec, dt_spec, par_spec, par_spec, par_spec],
        out_specs=[x_spec, h_spec],
        out_shape=[jax.ShapeDtypeStruct(xs.shape, F32),
                   jax.ShapeDtypeStruct((b, SSM_HEADS, nc, SSM_P, D_STATE), F32)],
        scratch_shapes=[pltpu.VMEM((SSM_HEADS, SSM_P, D_STATE), F32)],
        compiler_params=_params(("arbitrary", "arbitrary", "arbitrary")),
    )(xs, bm, cm, dtr, dtb, alog, dsk)


def ssd_bwd(xs, bm, cm, dtr, dtb, alog, dsk, hprev, dy, *, name):
    b, _, s, _ = xs.shape
    nc = s // CHUNK
    x_spec, bc_spec, dt_spec, par_spec, h_spec = _ssd_specs(nc, True)
    dpar_spec = pl.BlockSpec((8, LANE), lambda bi, c, g: (0, 0))

    def body(x_ref, b_ref, c_ref, dtr_ref, dtb_ref, alog_ref, dsk_ref, hp_ref, dy_ref,
             dx_ref, db_ref, dc_ref, ddtr_ref, dpar_ref, dstate):
        bi, c, g = pl.program_id(0), pl.program_id(1), pl.program_id(2)

        @pl.when(c == 0)
        def _():
            dstate[pl.ds(g * HEADS_PER_GROUP, HEADS_PER_GROUP)] = jnp.zeros((HEADS_PER_GROUP, SSM_P, D_STATE), F32)

        @pl.when((bi == 0) & (c == 0) & (g == 0))
        def _():
            dpar_ref[...] = jnp.zeros_like(dpar_ref)

        z, dt, aneg, acs = _ssd_common(dtr_ref, dtb_ref, alog_ref)
        bv, cv = b_ref[0, 0], c_ref[0, 0]
        b16, c16 = bv.astype(BF16), cv.astype(BF16)
        cb = lax.dot_general(c16, b16, (((1,), (1,)), ((), ())), preferred_element_type=F32)
        tri_mask = _tri(True)
        last_row = (lax.broadcasted_iota(jnp.int32, (CHUNK, 1), 0) == CHUNK - 1).astype(F32)
        lanes = lax.broadcasted_iota(jnp.int32, (1, LANE), 1)
        dcb = jnp.zeros((CHUNK, CHUNK), F32)
        db_acc = jnp.zeros((CHUNK, D_STATE), F32)
        dc_acc = jnp.zeros((CHUNK, D_STATE), F32)
        ddt_mat = jnp.zeros((CHUNK, LANE), F32)
        dacs_mat = jnp.zeros((CHUNK, LANE), F32)
        ddsk_row = jnp.zeros((1, LANE), F32)
        for j in range(HEADS_PER_GROUP):
            hidx = g * HEADS_PER_GROUP + j
            onehot = (lanes == hidx).astype(F32)
            x = x_ref[0, j]
            dt_j, acs_j = _col(dt, onehot), _col(acs, onehot)
            acs_last, xg, lm, mm, decay_s = _ssd_head(x, dt_j, acs_j, cb, tri_mask, last_row)
            ea = jnp.exp(acs_j)
            cd = jnp.exp(acs_last)
            d_j = jnp.sum(dsk_ref[...] * onehot, axis=1, keepdims=True)
            hp = hp_ref[0, j, 0]
            hp16 = hp.astype(BF16)
            g_y = dy_ref[0, j]
            g_y16 = g_y.astype(BF16)
            g_hn = dstate[hidx]
            g_hn16 = g_hn.astype(BF16)
            xg16 = xg.astype(BF16)
            ddsk_row = ddsk_row + jnp.sum(jnp.sum(g_y * x, axis=1, keepdims=True), axis=0, keepdims=True) * onehot
            d_mm = lax.dot_general(g_y16, xg16, (((1,), (1,)), ((), ())), preferred_element_type=F32)
            d_xg = lax.dot_general(mm.astype(BF16), g_y16, (((0,), (0,)), ((), ())), preferred_element_type=F32)
            dcb = dcb + d_mm * lm
            d_dm = d_mm * mm
            d_acs = jnp.sum(d_dm, axis=1, keepdims=True) - jnp.sum(d_dm.T, axis=1, keepdims=True)
            t_off = lax.dot_general(c16, hp16, (((1,), (1,)), ((), ())), preferred_element_type=F32)
            d_t16 = (g_y * ea).astype(BF16)
            d_acs = d_acs + jnp.sum(g_y * t_off, axis=1, keepdims=True) * ea
            dc_acc = dc_acc + jnp.dot(d_t16, hp16, preferred_element_type=F32)
            d_hp = lax.dot_general(d_t16, c16, (((0,), (0,)), ((), ())), preferred_element_type=F32) + g_hn * cd
            d_last = jnp.sum(jnp.sum(g_hn * hp, axis=1, keepdims=True), axis=0, keepdims=True) * cd
            d_w = lax.dot_general(b16, g_hn16, (((1,), (1,)), ((), ())), preferred_element_type=F32)
            db_acc = db_acc + jnp.dot((xg * decay_s).astype(BF16), g_hn16, preferred_element_type=F32)
            d_xg = d_xg + d_w * decay_s
            d_ds = jnp.sum(d_w * xg, axis=1, keepdims=True) * decay_s
            d_last = d_last + jnp.sum(d_ds, axis=0, keepdims=True)
            d_acs = d_acs - d_ds + d_last * last_row
            dx_ref[0, j] = d_j * g_y + d_xg * dt_j
            ddt_mat = ddt_mat + jnp.sum(d_xg * x, axis=1, keepdims=True) * onehot
            dacs_mat = dacs_mat + d_acs * onehot
            dstate[hidx] = d_hp
        dcb16 = dcb.astype(BF16)
        dc_ref[0, 0] = dc_acc + jnp.dot(dcb16, b16, preferred_element_type=F32)
        db_ref[0, 0] = db_acc + lax.dot_general(dcb16, c16, (((0,), (0,)), ((), ())), preferred_element_type=F32)
        d_a = _dot01_left(_tri(False).astype(BF16), dacs_mat)
        ddt_mat = ddt_mat + d_a * aneg
        d_aneg = jnp.sum(d_a * dt, axis=0, keepdims=True)
        d_raw = ddt_mat * jax.nn.sigmoid(z)

        @pl.when(g == 0)
        def _():
            ddtr_ref[0] = d_raw

        @pl.when(g != 0)
        def _():
            ddtr_ref[0] += d_raw

        dpar_ref[0:1, :] += jnp.sum(d_raw, axis=0, keepdims=True)
        dpar_ref[1:2, :] += d_aneg * aneg
        dpar_ref[2:3, :] += ddsk_row

    return pl.pallas_call(
        body, name=name, grid=(b, nc, SSM_GROUPS),
        in_specs=[x_spec, bc_spec, bc_spec, dt_spec, par_spec, par_spec, par_spec, h_spec, x_spec],
        out_specs=[x_spec, bc_spec, bc_spec, dt_spec, dpar_spec],
        out_shape=[jax.ShapeDtypeStruct(xs.shape, F32), jax.ShapeDtypeStruct(bm.shape, F32),
                   jax.ShapeDtypeStruct(cm.shape, F32), jax.ShapeDtypeStruct(dtr.shape, F32),
                   jax.ShapeDtypeStruct((8, LANE), F32)],
        scratch_shapes=[pltpu.VMEM((SSM_HEADS, SSM_P, D_STATE), F32)],
        compiler_params=_params(("arbitrary", "arbitrary", "arbitrary")),
    )(xs, bm, cm, dtr, dtb, alog, dsk, hprev, dy)


def to_heads(x, b, s, h):
    return x.reshape(b, s, h, -1).transpose(0, 2, 1, 3)


def from_heads(x):
    b, h, s, c = x.shape
    return x.transpose(0, 2, 1, 3).reshape(b * s, h * c)


def dilate_q(q, d):
    b, _, s, c = q.shape
    x = q.reshape(b, N_KV_HEADS, GQA, s // d, d, c).transpose(0, 1, 4, 2, 3, 5)
    return x.reshape(b * N_KV_HEADS * d, GQA, s // d, c)


def undilate_q(x, b, d):
    _, _, l, c = x.shape
    y = x.reshape(b, N_KV_HEADS, d, GQA, l, c).transpose(0, 1, 3, 4, 2, 5)
    return y.reshape(b, N_Q_HEADS, l * d, c)


def dilate_kv(k, d):
    b, h, s, c = k.shape
    return k.reshape(b, h, s // d, d, c).transpose(0, 1, 3, 2, 4).reshape(b * h * d, s // d, c)


def undilate_kv(x, b, d):
    _, l, c = x.shape
    return x.reshape(b, N_KV_HEADS, d, l, c).transpose(0, 1, 3, 2, 4).reshape(b, N_KV_HEADS, l * d, c)


def rotary_tables(positions):
    inv_freq = ROPE_THETA ** (-jnp.arange(0, ROPE_DIM, 2, dtype=F32) / ROPE_DIM)
    ang = positions.astype(F32)[..., None] * inv_freq
    cos, sin = jnp.cos(ang), jnp.sin(ang)
    rest = HEAD_DIM - ROPE_DIM
    cosf = jnp.concatenate([cos, cos, jnp.ones(cos.shape[:2] + (rest,), F32)], axis=-1)
    sinf = jnp.concatenate([-sin, sin, jnp.zeros(sin.shape[:2] + (rest,), F32)], axis=-1)
    return cosf, sinf


def lane_pad(v):
    return jnp.pad(v.reshape(1, -1), ((0, 0), (0, LANE - v.shape[-1])))


def layer_fwd(h, wts, small, cosf, sinf, b, s, tag):
    w_in, w_out, w_gate, w_up, w_down = wts
    t = b * s
    sv = {"h": h}
    hn = rowwise_fwd(rms_fn, [h], [small["norm_mix"]], [BF16], name=f"rms_mix_{tag}")[0]
    proj = matmul(hn, w_in, name=f"in_proj_{tag}")
    sv["hn"], sv["proj"] = hn, proj
    qh = rotary([to_heads(proj[:, :Q_END], b, s, N_Q_HEADS)], cosf, sinf, HEAD_DIM ** -0.5, adjoint=False, name=f"rope_q_{tag}")
    kh = rotary([to_heads(proj[:, Q_END:K_END], b, s, N_KV_HEADS)], cosf, sinf, 1.0, adjoint=False, name=f"rope_k_{tag}")
    vh = to_heads(proj[:, K_END:V_END], b, s, N_KV_HEADS)
    sv["qh"], sv["kh"], sv["vh"] = qh, kh, vh
    outs, lses = [], []
    for d in DILATIONS:
        o, lse = attn_branch_fwd(dilate_q(qh, d), dilate_kv(kh, d), dilate_kv(vh, d), name=f"attn_d{d}_{tag}")
        outs.append(undilate_q(o, b, d).reshape(b * N_Q_HEADS * s, HEAD_DIM))
        lses.append(undilate_q(lse, b, d).reshape(b * N_Q_HEADS * s, 1))
    sv["attn_o"], sv["attn_lse"] = outs, lses
    attn = rowwise_fwd(combine_fn, outs + lses, [], [F32], name=f"attn_combine_{tag}", tr=2048)[0]
    attn = from_heads(attn.reshape(b, N_Q_HEADS, s, HEAD_DIM))
    proj3 = proj.reshape(b, s, IN_PAD)
    act = conv_silu_fwd(proj3, small["conv_w"], small["conv_b"], name=f"conv_{tag}").reshape(t, CONV_CH)
    xs = to_heads(act[:, :SSM_INNER], b, s, SSM_HEADS)
    bm = to_heads(act[:, SSM_INNER:SSM_INNER + SSM_GROUPS * D_STATE], b, s, SSM_GROUPS)
    cm = to_heads(act[:, SSM_INNER + SSM_GROUPS * D_STATE:], b, s, SSM_GROUPS)
    dtr = proj3[:, :, XBC_END:]
    sv["xs"], sv["bm"], sv["cm"], sv["dtr"] = xs, bm, cm, dtr
    y_hm, hprev = ssd_fwd(xs, bm, cm, dtr, small["dt_bias"], small["a_log"], small["d_skip"], name=f"ssd_{tag}")
    y = from_heads(y_hm)
    z = proj[:, V_END:Z_END]
    sv["hprev"], sv["y"], sv["z"] = hprev, y, z
    gn = rowwise_fwd(gated_norm_fn, [y, z], [small["ssm_norm"]], [F32], name=f"gated_norm_{tag}", groups=SSM_GROUPS)[0]
    cat = jnp.concatenate([attn, gn], axis=1).astype(BF16)
    sv["cat"] = cat
    h1 = matmul(cat, w_out, name=f"out_proj_{tag}", residual=h)
    sv["h1"] = h1
    hn2 = rowwise_fwd(rms_fn, [h1], [small["norm_ffn"]], [BF16], name=f"rms_ffn_{tag}")[0]
    gate = matmul(hn2, w_gate, name=f"ffn_gate_{tag}")
    up = matmul(hn2, w_up, name=f"ffn_up_{tag}")
    act2 = rowwise_fwd(swiglu_fn, [gate, up], [], [BF16], name=f"swiglu_{tag}")[0]
    sv["hn2"], sv["gate"], sv["up"], sv["act2"] = hn2, gate, up, act2
    h2 = matmul(act2, w_down, name=f"ffn_down_{tag}", residual=h1)
    return h2, sv


def layer_bwd(dh2, sv, wts, small, cosf, sinf, b, s, tag):
    w_in, w_out, w_gate, w_up, w_down = wts
    t = b * s
    gr = {}
    dh2_16 = dh2.astype(BF16)
    d_act2 = matmul(dh2_16, w_down, tb=True, name=f"ffn_down_dx_{tag}")
    gr["w_down"] = matmul(sv["act2"], dh2_16, ta=True, out_dtype=BF16, name=f"ffn_down_dw_{tag}")
    d_gate, d_up = rowwise_bwd(swiglu_fn, [sv["gate"], sv["up"]], [], [d_act2], [BF16, BF16], name=f"swiglu_bwd_{tag}")
    gr["w_gate"] = matmul(sv["hn2"], d_gate, ta=True, out_dtype=BF16, name=f"ffn_gate_dw_{tag}")
    gr["w_up"] = matmul(sv["hn2"], d_up, ta=True, out_dtype=BF16, name=f"ffn_up_dw_{tag}")
    d_hn2 = matmul(d_gate, w_gate, tb=True, name=f"ffn_gate_dx_{tag}")
    d_hn2 = matmul(d_up, w_up, tb=True, residual=d_hn2, name=f"ffn_up_dx_{tag}")
    dh1, gr["norm_ffn"] = rowwise_bwd(rms_fn, [sv["h1"]], [small["norm_ffn"]], [d_hn2], [F32],
                                      name=f"rms_ffn_bwd_{tag}", add_to_first=dh2)
    dh1_16 = dh1.astype(BF16)
    d_cat = matmul(dh1_16, w_out, tb=True, name=f"out_proj_dx_{tag}")
    gr["w_out"] = matmul(sv["cat"], dh1_16, ta=True, out_dtype=BF16, name=f"out_proj_dw_{tag}")
    d_attn, d_gn = d_cat[:, :ATTN_WIDTH], d_cat[:, ATTN_WIDTH:]
    d_y, d_z, gr["ssm_norm"] = rowwise_bwd(gated_norm_fn, [sv["y"], sv["z"]], [small["ssm_norm"]], [d_gn], [F32, F32],
                                           name=f"gated_norm_bwd_{tag}", groups=SSM_GROUPS)
    d_xs, d_bm, d_cm, d_dtr, d_par = ssd_bwd(sv["xs"], sv["bm"], sv["cm"], sv["dtr"], small["dt_bias"], small["a_log"],
                                             small["d_skip"], sv["hprev"], to_heads(d_y, b, s, SSM_HEADS),
                                             name=f"ssd_bwd_{tag}")
    gr["dt_bias"], gr["a_log"], gr["d_skip"] = d_par[0, :SSM_HEADS], d_par[1, :SSM_HEADS], d_par[2, :SSM_HEADS]
    d_act = jnp.concatenate([from_heads(d_xs), from_heads(d_bm), from_heads(d_cm)], axis=1).reshape(b, s, CONV_CH)
    d_xbc, gr["conv_w"], gr["conv_b"] = conv_silu_bwd(sv["proj"].reshape(b, s, IN_PAD), small["conv_w"], small["conv_b"],
                                                      d_act, name=f"conv_bwd_{tag}")
    d_attn_h = to_heads(d_attn, b, s, N_Q_HEADS).reshape(b * N_Q_HEADS * s, HEAD_DIM)
    comb = rowwise_bwd(combine_fn, sv["attn_o"] + sv["attn_lse"], [], [d_attn_h], [F32] * 6,
                       name=f"attn_combine_bwd_{tag}", tr=2048)
    dqs, dks, dvs = [], [], []
    for i, d in enumerate(DILATIONS):
        as_q = lambda a, c: dilate_q(a.reshape(b, N_Q_HEADS, s, c), d)
        dq, dk, dv = attn_branch_bwd(dilate_q(sv["qh"], d), dilate_kv(sv["kh"], d), dilate_kv(sv["vh"], d),
                                     as_q(sv["attn_o"][i], HEAD_DIM), as_q(sv["attn_lse"][i], 1),
                                     as_q(comb[i], HEAD_DIM), as_q(comb[3 + i], 1), name=f"attn_d{d}_bwd_{tag}")
        dqs.append(undilate_q(dq, b, d))
        dks.append(undilate_kv(dk, b, d))
        dvs.append(undilate_kv(dv, b, d))
    d_q = rotary(dqs, cosf, sinf, HEAD_DIM ** -0.5, adjoint=True, name=f"rope_q_bwd_{tag}")
    d_k = rotary(dks, cosf, sinf, 1.0, adjoint=True, name=f"rope_k_bwd_{tag}")
    d_v = add3(*[from_heads(a) for a in dvs], name=f"dv_sum_{tag}")
    d_proj = jnp.concatenate([from_heads(d_q), from_heads(d_k), d_v, d_z, d_xbc.reshape(t, CONV_CH),
                              d_dtr.reshape(t, LANE)], axis=1).astype(BF16)
    d_hn = matmul(d_proj, w_in, tb=True, name=f"in_proj_dx_{tag}")
    gr["w_in"] = matmul(sv["hn"], d_proj, ta=True, out_dtype=BF16, name=f"in_proj_dw_{tag}")[:, :IN_PROJ]
    dh, gr["norm_mix"] = rowwise_bwd(rms_fn, [sv["h"]], [small["norm_mix"]], [d_hn], [F32],
                                     name=f"rms_mix_bwd_{tag}", add_to_first=dh1)
    return dh, gr


def local_step(x, positions, big, small_all, final_norm, loss_target):
    b, s, _ = x.shape
    t = b * s
    cosf, sinf = rotary_tables(positions)
    h = x.reshape(t, D_MODEL)
    saved = []
    for l in range(DEPTH):
        h, sv = layer_fwd(h, big[l], small_all[l], cosf, sinf, b, s, f"l{l}")
        saved.append(sv)
    dh, d_final, loss = loss_and_grad(h, loss_target.reshape(t, D_MODEL), final_norm.reshape(1, D_MODEL))
    grads = [None] * DEPTH
    for l in reversed(range(DEPTH)):
        dh, grads[l] = layer_bwd(dh, saved[l], big[l], small_all[l], cosf, sinf, b, s, f"l{l}")
    return loss, dh.reshape(b, s, D_MODEL), grads, d_final


def _slab_rows(r):
    return r if r <= 512 else _pick(r, (512, 256))


def cast_bf16(x, *, name):
    def fn(v):
        return (v,)
    return rowwise_fwd(fn, [x], [], [BF16], name=name, tr=_slab_rows(x.shape[0]))[0]


def sum_slots(x, *, name):
    n, r, c = x.shape
    tr = _slab_rows(r)

    def body(x_ref, o_ref):
        acc = x_ref[0].astype(F32)
        for i in range(1, n):
            acc = acc + x_ref[i].astype(F32)
        o_ref[...] = acc

    return pl.pallas_call(
        body, name=name, grid=(r // tr,), in_specs=[pl.BlockSpec((n, tr, c), lambda i: (0, i, 0))],
        out_specs=pl.BlockSpec((tr, c), lambda i: (i, 0)), out_shape=jax.ShapeDtypeStruct((r, c), F32),
        compiler_params=_params(("parallel",)),
    )(x)


def adamw(g_parts, w, m, v, *, name):
    r, c = w.shape
    tr = _slab_rows(r)
    n_g = len(g_parts)
    bc1 = 1.0 / (1.0 - ADAM_B1 ** ADAM_STEP)
    bc2 = 1.0 / (1.0 - ADAM_B2 ** ADAM_STEP)

    def body(*refs):
        g = refs[0][...]
        for r_ in refs[1:n_g]:
            g = g + r_[...]
        w_ref, m_ref, v_ref, g_out, d_out, m_out, v_out = refs[n_g:]
        m_new = ADAM_B1 * m_ref[...] + (1.0 - ADAM_B1) * g
        v_new = ADAM_B2 * v_ref[...] + (1.0 - ADAM_B2) * (g * g)
        g_out[...] = g
        m_out[...] = m_new
        v_out[...] = v_new
        d_out[...] = -ADAM_LR * ((m_new * bc1) / (jnp.sqrt(v_new * bc2) + ADAM_EPS) + ADAM_WD * w_ref[...])

    spec = pl.BlockSpec((tr, c), lambda i: (i, 0))
    return pl.pallas_call(
        body, name=name, grid=(r // tr,), in_specs=[spec] * (n_g + 3), out_specs=[spec] * 4,
        out_shape=[jax.ShapeDtypeStruct((r, c), F32)] * 4, compiler_params=_params(("parallel",)),
    )(*g_parts, w, m, v)


def _other_chips(x, y):
    return [(1 - x, y), (x, 1 - y), (1 - x, 1 - y)]


def allgather_chips(shards):
    n_arr = len(shards)

    def body(*refs):
        in_refs, out_refs = refs[:n_arr], refs[n_arr:2 * n_arr]
        send_sems, recv_sems, local_sems = refs[2 * n_arr:]
        x, y, c = lax.axis_index("x"), lax.axis_index("y"), lax.axis_index("c")
        chip = 2 * x + y
        started = []
        for a, (in_ref, out_ref) in enumerate(zip(in_refs, out_refs)):
            mine = pltpu.make_async_copy(in_ref, out_ref.at[chip], local_sems.at[a])
            mine.start()
            started.append(mine.wait)
            for k, (px, py) in enumerate(_other_chips(x, y)):
                cp = pltpu.make_async_remote_copy(src_ref=in_ref, dst_ref=out_ref.at[chip], send_sem=send_sems.at[3 * a + k],
                                                  recv_sem=recv_sems.at[3 * a + k], device_id=(px, py, c), device_id_type=MESH)
                cp.start()
                started.append(cp.wait_send)
        for a, (in_ref, out_ref) in enumerate(zip(in_refs, out_refs)):
            for k, (px, py) in enumerate(_other_chips(x, y)):
                pltpu.make_async_remote_copy(src_ref=in_ref, dst_ref=out_ref.at[2 * px + py], send_sem=send_sems.at[3 * a + k],
                                             recv_sem=recv_sems.at[3 * a + k], device_id=(px, py, c),
                                             device_id_type=MESH).wait_recv()
        for wait in started:
            wait()

    hbm = pl.BlockSpec(memory_space=pltpu.HBM)
    return pl.pallas_call(
        body, name="allgather_weights", in_specs=[hbm] * n_arr, out_specs=[hbm] * n_arr,
        out_shape=[jax.ShapeDtypeStruct((N_CHIPS,) + s.shape, s.dtype) for s in shards],
        scratch_shapes=[pltpu.SemaphoreType.DMA((3 * n_arr,)), pltpu.SemaphoreType.DMA((3 * n_arr,)),
                        pltpu.SemaphoreType.DMA((n_arr,))],
    )(*shards)


def exchange_grads(big, small):
    def body(big_ref, small_ref, big_out, small_out, send_sems, recv_sems, local_sems):
        x, y, c = lax.axis_index("x"), lax.axis_index("y"), lax.axis_index("c")
        chip = 2 * x + y
        dev = 4 * x + 2 * y + c
        own_big = pltpu.make_async_copy(big_ref.at[chip], big_out.at[chip], local_sems.at[0])
        own_small = pltpu.make_async_copy(small_ref, small_out.at[dev], local_sems.at[1])
        own_big.start()
        own_small.start()
        sends = []
        for k, (px, py) in enumerate(_other_chips(x, y)):
            cp = pltpu.make_async_remote_copy(src_ref=big_ref.at[2 * px + py], dst_ref=big_out.at[chip],
                                              send_sem=send_sems.at[k], recv_sem=recv_sems.at[k],
                                              device_id=(px, py, c), device_id_type=MESH)
            cp.start()
            sends.append(cp)
        peers = []
        for r in range(1, N_DEV):
            fx, fy, fc = (r >> 2) & 1, (r >> 1) & 1, r & 1
            px, py, pc = (x + fx) % 2, (y + fy) % 2, (c + fc) % 2
            peers.append((px, py, pc))
            cp = pltpu.make_async_remote_copy(src_ref=small_ref, dst_ref=small_out.at[dev], send_sem=send_sems.at[2 + r],
                                              recv_sem=recv_sems.at[2 + r], device_id=(px, py, pc), device_id_type=MESH)
            cp.start()
            sends.append(cp)
        for k, (px, py) in enumerate(_other_chips(x, y)):
            pltpu.make_async_remote_copy(src_ref=big_ref.at[chip], dst_ref=big_out.at[2 * px + py],
                                         send_sem=send_sems.at[k], recv_sem=recv_sems.at[k],
                                         device_id=(px, py, c), device_id_type=MESH).wait_recv()
        for r, (px, py, pc) in zip(range(1, N_DEV), peers):
            pltpu.make_async_remote_copy(src_ref=small_ref, dst_ref=small_out.at[4 * px + 2 * py + pc],
                                         send_sem=send_sems.at[2 + r], recv_sem=recv_sems.at[2 + r],
                                         device_id=(px, py, pc), device_id_type=MESH).wait_recv()
        for cp in sends:
            cp.wait_send()
        own_big.wait()
        own_small.wait()

    hbm = pl.BlockSpec(memory_space=pltpu.HBM)
    n_sem = 3 + N_DEV - 1
    return pl.pallas_call(
        body, name="exchange_grads", in_specs=[hbm, hbm], out_specs=[hbm, hbm],
        out_shape=[jax.ShapeDtypeStruct(big.shape, big.dtype), jax.ShapeDtypeStruct((N_DEV,) + small.shape, small.dtype)],
        scratch_shapes=[pltpu.SemaphoreType.DMA((n_sem,)), pltpu.SemaphoreType.DMA((n_sem,)), pltpu.SemaphoreType.DMA((2,))],
    )(big, small)


def swap_cores(mine):
    def body(in_ref, out_ref, send_sem, recv_sem, local_sem):
        x, y, c = lax.axis_index("x"), lax.axis_index("y"), lax.axis_index("c")
        own = pltpu.make_async_copy(in_ref, out_ref.at[c], local_sem)
        own.start()
        cp = pltpu.make_async_remote_copy(src_ref=in_ref, dst_ref=out_ref.at[c], send_sem=send_sem, recv_sem=recv_sem,
                                          device_id=(x, y, 1 - c), device_id_type=MESH)
        cp.start()
        pltpu.make_async_remote_copy(src_ref=in_ref, dst_ref=out_ref.at[1 - c], send_sem=send_sem, recv_sem=recv_sem,
                                     device_id=(x, y, 1 - c), device_id_type=MESH).wait_recv()
        cp.wait_send()
        own.wait()

    hbm = pl.BlockSpec(memory_space=pltpu.HBM)
    return pl.pallas_call(
        body, name="swap_cores", in_specs=[hbm], out_specs=hbm,
        out_shape=jax.ShapeDtypeStruct((2,) + mine.shape, mine.dtype),
        scratch_shapes=[pltpu.SemaphoreType.DMA, pltpu.SemaphoreType.DMA, pltpu.SemaphoreType.DMA],
    )(mine)


BIG_NAMES = ("w_in", "w_out", "w_gate", "w_up", "w_down")
BIG_SHARD_AXIS = {"w_in": 1, "w_out": 0, "w_gate": 1, "w_up": 1, "w_down": 0}
PACK_COLS = 1024
SMALL_NAMES = ("norm_mix", "conv_w", "conv_b", "dt_bias", "a_log", "d_skip", "ssm_norm", "norm_ffn")


PACK_ROW_TILE = 256


def pack_big(shards):
    flat = jnp.concatenate([shards[n].reshape(-1) for n in BIG_NAMES])
    unit = PACK_ROW_TILE * PACK_COLS
    total = -(-flat.size // unit) * unit
    return jnp.pad(flat, (0, total - flat.size)).reshape(-1, PACK_COLS)


def unpack_big(packed, like):
    out, off = {}, 0
    flat = packed.reshape(-1)
    for n in BIG_NAMES:
        size = like[n].size
        out[n] = flat[off:off + size].reshape(like[n].shape)
        off += size
    return out


def pack_small(parts):
    flat = jnp.concatenate([p.reshape(-1).astype(F32) for p in parts])
    rows = -(-flat.size // LANE)
    rows = -(-rows // 8) * 8
    return jnp.pad(flat, (0, rows * LANE - flat.size)).reshape(rows, LANE)


def unpack_small(packed, like):
    out, off = [], 0
    flat = packed.reshape(-1)
    for a in like:
        out.append(flat[off:off + a.size].reshape(a.shape))
        off += a.size
    return out


def kernel(x, positions, norm_mix, w_in, conv_w, conv_b, dt_bias, a_log, d_skip, ssm_norm, w_out, norm_ffn, w_gate, w_up, w_down, final_norm, loss_target, m_norm_mix, m_w_in, m_conv_w, m_conv_b, m_dt_bias, m_a_log, m_d_skip, m_ssm_norm, m_w_out, m_norm_ffn, m_w_gate, m_w_up, m_w_down, m_final_norm, v_norm_mix, v_w_in, v_conv_w, v_conv_b, v_dt_bias, v_a_log, v_d_skip, v_ssm_norm, v_w_out, v_norm_ffn, v_w_gate, v_w_up, v_w_down, v_final_norm):
    chip = 2 * lax.axis_index("x") + lax.axis_index("y")
    w_sh = {"w_in": w_in, "w_out": w_out, "w_gate": w_gate, "w_up": w_up, "w_down": w_down}
    m_sh = {"w_in": m_w_in, "w_out": m_w_out, "w_gate": m_w_gate, "w_up": m_w_up, "w_down": m_w_down}
    v_sh = {"w_in": v_w_in, "w_out": v_w_out, "w_gate": v_w_gate, "w_up": v_w_up, "w_down": v_w_down}

    w_packed = pack_big(w_sh)
    conv_cols = CONV_CH // N_CHIPS
    gathered, conv_g = allgather_chips([cast_bf16(w_packed, name="cast_weights"), conv_w.reshape(-1, LANE)])
    pieces = [unpack_big(gathered[j], w_sh) for j in range(N_CHIPS)]
    full = {n: jnp.concatenate([p[n] for p in pieces], axis=BIG_SHARD_AXIS[n] + 1) for n in BIG_NAMES}
    big = []
    for l in range(DEPTH):
        w_in_pad = jnp.pad(full["w_in"][l], ((0, 0), (0, IN_PAD - IN_PROJ)))
        big.append((w_in_pad, full["w_out"][l], full["w_gate"][l], full["w_up"][l], full["w_down"][l]))
    conv_w_full = jnp.concatenate([conv_g[j].reshape(DEPTH, CONV_WIDTH, conv_cols) for j in range(N_CHIPS)], axis=2)

    small_all = []
    for l in range(DEPTH):
        small_all.append({
            "norm_mix": norm_mix[l].reshape(1, -1), "conv_w": conv_w_full[l], "conv_b": conv_b[l].reshape(1, -1),
            "dt_bias": lane_pad(dt_bias[l]), "a_log": lane_pad(a_log[l]), "d_skip": lane_pad(d_skip[l]),
            "ssm_norm": ssm_norm[l].reshape(1, -1), "norm_ffn": norm_ffn[l].reshape(1, -1)})

    loss_part, grad_x, grads, d_final = local_step(x, positions, big, small_all, final_norm, loss_target)

    def shard_of(name, g, j):
        n = g.shape[BIG_SHARD_AXIS[name]] // N_CHIPS
        return lax.slice_in_dim(g, j * n, (j + 1) * n, axis=BIG_SHARD_AXIS[name])

    to_chip = []
    for j in range(N_CHIPS):
        to_chip.append(pack_big({n: jnp.stack([shard_of(n, grads[l][n], j) for l in range(DEPTH)]) for n in BIG_NAMES}))
    small_parts = [jnp.stack([grads[l][n].reshape(-1) for l in range(DEPTH)]) for n in SMALL_NAMES]
    small_parts += [d_final.reshape(-1), loss_part.reshape(-1)]
    recv_big, recv_small = exchange_grads(jnp.stack(to_chip), pack_small(small_parts))
    plane_sum = sum_slots(recv_big, name="sum_chip_partials")
    both = swap_cores(plane_sum)

    g_big, d_big, m_big, v_big = adamw([both[0], both[1]], w_packed, pack_big(m_sh), pack_big(v_sh), name="adamw_big")
    g_big, d_big, m_big, v_big = (unpack_big(a, w_sh) for a in (g_big, d_big, m_big, v_big))

    small_sum = sum_slots(recv_small, name="sum_small")
    like = [norm_mix, conv_w_full, conv_b, dt_bias, a_log, d_skip, ssm_norm, norm_ffn, final_norm, loss_part.reshape(-1)]
    g_small = unpack_small(small_sum, like)
    loss = g_small[-1][0]
    g_small = dict(zip(SMALL_NAMES + ("final_norm",), g_small[:-1]))
    g_small["conv_w"] = lax.dynamic_slice_in_dim(g_small["conv_w"], chip * conv_cols, conv_cols, axis=2)
    w_small = {"norm_mix": norm_mix, "conv_w": conv_w, "conv_b": conv_b, "dt_bias": dt_bias, "a_log": a_log, "d_skip": d_skip,
               "ssm_norm": ssm_norm, "norm_ffn": norm_ffn, "final_norm": final_norm}
    m_small = {"norm_mix": m_norm_mix, "conv_w": m_conv_w, "conv_b": m_conv_b, "dt_bias": m_dt_bias, "a_log": m_a_log,
               "d_skip": m_d_skip, "ssm_norm": m_ssm_norm, "norm_ffn": m_norm_ffn, "final_norm": m_final_norm}
    v_small = {"norm_mix": v_norm_mix, "conv_w": v_conv_w, "conv_b": v_conv_b, "dt_bias": v_dt_bias, "a_log": v_a_log,
               "d_skip": v_d_skip, "ssm_norm": v_ssm_norm, "norm_ffn": v_norm_ffn, "final_norm": v_final_norm}
    names = SMALL_NAMES + ("final_norm",)
    order = [w_small[n] for n in names]
    res = adamw([pack_small([g_small[n] for n in names])], pack_small(order), pack_small([m_small[n] for n in names]),
                pack_small([v_small[n] for n in names]), name="adamw_small")
    g_s, d_s, m_s, v_s = (dict(zip(names, unpack_small(a, order))) for a in res)

    all_names = ("norm_mix", "w_in", "conv_w", "conv_b", "dt_bias", "a_log", "d_skip", "ssm_norm", "w_out", "norm_ffn",
                 "w_gate", "w_up", "w_down", "final_norm")
    outs = [loss, grad_x]
    for src_big, src_small in ((g_big, g_s), (d_big, d_s), (m_big, m_s), (v_big, v_s)):
        outs += [src_big[n] if n in BIG_NAMES else src_small[n] for n in all_names]
    return tuple(outs)
```

```python
import functools

import jax
import jax.numpy as jnp
from jax import lax
from jax.experimental import pallas as pl
from jax.experimental.pallas import tpu as pltpu

F32 = jnp.float32
BF16 = jnp.bfloat16
MESH = pl.DeviceIdType.MESH

D_MODEL = 1024
DEPTH = 2
HEAD_DIM = 64
N_Q_HEADS = 8
N_KV_HEADS = 2
GQA = N_Q_HEADS // N_KV_HEADS
ATTN_WIDTH = N_Q_HEADS * HEAD_DIM
ROPE_DIM = HEAD_DIM // 4
ROPE_HALF = ROPE_DIM // 2
ROPE_THETA = 500000.0
DILATIONS = (1, 4, 16)
ATTN_BLOCK = 128
SSM_P = 64
SSM_HEADS = 16
SSM_INNER = SSM_HEADS * SSM_P
SSM_GROUPS = 2
HEADS_PER_GROUP = SSM_HEADS // SSM_GROUPS
D_STATE = 128
CONV_WIDTH = 4
CHUNK = 128
CONV_CH = SSM_INNER + 2 * SSM_GROUPS * D_STATE
MIX_WIDTH = ATTN_WIDTH + SSM_INNER
Q_END = ATTN_WIDTH
K_END = Q_END + N_KV_HEADS * HEAD_DIM
V_END = K_END + N_KV_HEADS * HEAD_DIM
Z_END = V_END + SSM_INNER
XBC_END = Z_END + CONV_CH
IN_PROJ = XBC_END + SSM_HEADS
LANE = 128
IN_PAD = XBC_END + LANE
Q_COL, Z_COL, XBC_COL, K_COL, V_COL, DT_COL = 0, 512, 1536, 3072, 3200, 3328
FFN_HIDDEN = 2816
EPS = 1e-5
ADAM_LR, ADAM_B1, ADAM_B2, ADAM_EPS, ADAM_WD, ADAM_STEP = 0.001, 0.9, 0.999, 1e-8, 0.01, 10
N_CHIPS = 4
N_DEV = 8
VMEM_LIMIT = 48 * 1024 * 1024
NEG_BIG = -1e30


def _params(sem=None):
    return pltpu.CompilerParams(dimension_semantics=sem, vmem_limit_bytes=VMEM_LIMIT)


def _pick(n, prefs):
    for p in prefs:
        if n % p == 0:
            return p
    return n


def matmul(a, b, *, name, ta=False, tb=False, out_dtype=F32, residual=None):
    if ta:
        assert not tb and residual is None
        return _matmul_over_rows(a, b, name=name, out_dtype=out_dtype)
    return _matmul_full_k(a, b, name=name, tb=tb, out_dtype=out_dtype, residual=residual)


def _matmul_full_k(a, b, *, name, tb, out_dtype, residual):
    m, kdim = a.shape
    n = b.shape[0] if tb else b.shape[1]
    tm = _pick(m, (1024, 512, 256)) if kdim <= 1536 else _pick(m, (512, 256))
    tn = _pick(n, (1152, 1408, 1536, 1024, 768, 512, 384, 256, 128))
    b_spec = pl.BlockSpec((tn, kdim), lambda i, j: (j, 0)) if tb else pl.BlockSpec((kdim, tn), lambda i, j: (0, j))
    o_spec = pl.BlockSpec((tm, tn), lambda i, j: (i, j))
    dims = (((1,), (1 if tb else 0,)), ((), ()))
    has_res = residual is not None

    def body(*refs):
        a_ref, b_ref = refs[:2]
        o_ref = refs[-1]
        r = lax.dot_general(a_ref[...].astype(BF16), b_ref[...].astype(BF16), dims, preferred_element_type=F32)
        if has_res:
            r = r + refs[2][...]
        o_ref[...] = r.astype(out_dtype)

    in_specs = [pl.BlockSpec((tm, kdim), lambda i, j: (i, 0)), b_spec] + ([o_spec] if has_res else [])
    args = (a, b) + ((residual,) if has_res else ())
    return pl.pallas_call(
        body, name=name, grid=(m // tm, n // tn), in_specs=in_specs, out_specs=o_spec,
        out_shape=jax.ShapeDtypeStruct((m, n), out_dtype),
        compiler_params=_params(("parallel", "parallel")),
    )(*args)


def _matmul_over_rows(a, b, *, name, out_dtype):
    t, m = a.shape
    n = b.shape[1]
    tm = _pick(m, (1024, 1408, 768, 512, 256, 128))
    tn = _pick(n, (1152, 1408, 1024, 768, 512, 256, 128))
    tk = _pick(t, (1024, 512, 256, 128))
    nk = t // tk

    def body(a_ref, b_ref, o_ref, acc):
        k = pl.program_id(2)
        part = lax.dot_general(a_ref[...].astype(BF16), b_ref[...].astype(BF16), (((0,), (0,)), ((), ())),
                               preferred_element_type=F32)

        @pl.when(k == 0)
        def _():
            acc[...] = part

        @pl.when(k > 0)
        def _():
            acc[...] += part

        @pl.when(k == nk - 1)
        def _():
            o_ref[...] = acc[...].astype(out_dtype)

    return pl.pallas_call(
        body, name=name, grid=(m // tm, n // tn, nk),
        in_specs=[pl.BlockSpec((tk, tm), lambda i, j, k: (k, i)), pl.BlockSpec((tk, tn), lambda i, j, k: (k, j))],
        out_specs=pl.BlockSpec((tm, tn), lambda i, j, k: (i, j)),
        out_shape=jax.ShapeDtypeStruct((m, n), out_dtype),
        scratch_shapes=[pltpu.VMEM((tm, tn), F32)],
        compiler_params=_params(("parallel", "parallel", "arbitrary")),
    )(a, b)


ROW_BLOCK_BYTES = 8 * 1024 * 1024


def _row_tile(t, tr, widths, n_copies):
    lanes = sum(-(-wd // LANE) * LANE for wd in widths) * n_copies
    tr = min(tr, t)
    while tr > 8 and tr * lanes * 4 > ROW_BLOCK_BYTES:
        tr //= 2
    return tr


def _row_widths(rows, groups, windows):
    windows = windows or [None] * len(rows)
    widths = [(w[1] if w else a.shape[1]) // groups for a, w in zip(rows, windows)]
    assert all(w is None or w[0] % wd == 0 for w, wd in zip(windows, widths))
    return widths, [(w[0] // wd if w else 0) for w, wd in zip(windows, widths)]


def _row_specs(tr, widths, offs):
    return [pl.BlockSpec((tr, wd), functools.partial(lambda g, i, off: (i, g + off), off=off)) for wd, off in zip(widths, offs)]


def rowwise_fwd(fn, rows, params, out_dtypes, *, name, tr=512, groups=1, windows=None):
    t = rows[0].shape[0]
    widths, offs = _row_widths(rows, groups, windows)
    tr = _row_tile(t, tr, widths, 2)
    row_specs = _row_specs(tr, widths, offs)
    par_spec = lambda p: pl.BlockSpec((1, p.shape[1] // groups), lambda g, i: (0, g))
    n_in = len(rows) + len(params)
    out_cols = [o.shape[1] for o in jax.eval_shape(
        fn, *[jax.ShapeDtypeStruct((tr, wd), F32) for wd in widths],
        *[jax.ShapeDtypeStruct((1, p.shape[1] // groups), F32) for p in params])]

    def body(*refs):
        vals = [r[...].astype(F32) for r in refs[:n_in]]
        outs = fn(*vals)
        for o_ref, o in zip(refs[n_in:], outs):
            o_ref[...] = o.astype(o_ref.dtype)

    return pl.pallas_call(
        body, name=name, grid=(groups, t // tr),
        in_specs=row_specs + [par_spec(p) for p in params],
        out_specs=[pl.BlockSpec((tr, c), lambda g, i: (i, g)) for c in out_cols],
        out_shape=[jax.ShapeDtypeStruct((t, c * groups), d) for c, d in zip(out_cols, out_dtypes)],
        compiler_params=_params(("arbitrary", "arbitrary")),
    )(*rows, *params)


def rowwise_bwd(fn, rows, params, cts, drow_dtypes, *, name, tr=512, groups=1, add_to_first=None, windows=None):
    t = rows[0].shape[0]
    widths, offs = _row_widths(rows, groups, windows)
    tr = _row_tile(t, tr, widths + [a.shape[1] // groups for a in cts], 2)
    row_spec = lambda a: pl.BlockSpec((tr, a.shape[1] // groups), lambda g, i: (i, g))
    row_specs = _row_specs(tr, widths, offs)
    par_spec = lambda p: pl.BlockSpec((1, p.shape[1] // groups), lambda g, i: (0, g))
    n_rows, n_par, n_ct = len(rows), len(params), len(cts)
    has_add = add_to_first is not None
    n_in = n_rows + n_par + n_ct + (1 if has_add else 0)

    def body(*refs):
        i = pl.program_id(1)
        vals = [r[...].astype(F32) for r in refs[:n_rows + n_par]]
        ct_vals = tuple(r[...].astype(F32) for r in refs[n_rows + n_par:n_rows + n_par + n_ct])
        _, vjp = jax.vjp(fn, *vals)
        grads = vjp(ct_vals)
        out_refs = refs[n_in:]
        for idx in range(n_rows):
            g = grads[idx]
            if idx == 0 and has_add:
                g = g + refs[n_in - 1][...]
            out_refs[idx][...] = g.astype(out_refs[idx].dtype)
        for idx in range(n_par):
            p_ref = out_refs[n_rows + idx]

            @pl.when(i == 0)
            def _():
                p_ref[...] = jnp.zeros_like(p_ref)

            p_ref[...] += grads[n_rows + idx]

    ins = list(rows) + list(params) + list(cts) + ([add_to_first] if has_add else [])
    in_specs = (row_specs + [par_spec(p) for p in params] + [row_spec(a) for a in cts]
                + ([row_spec(add_to_first)] if has_add else []))
    return pl.pallas_call(
        body, name=name, grid=(groups, t // tr), in_specs=in_specs,
        out_specs=[pl.BlockSpec((tr, wd), lambda g, i: (i, g)) for wd in widths] + [par_spec(p) for p in params],
        out_shape=[jax.ShapeDtypeStruct((t, wd * groups), d) for wd, d in zip(widths, drow_dtypes)]
        + [jax.ShapeDtypeStruct(p.shape, F32) for p in params],
        compiler_params=_params(("arbitrary", "arbitrary")),
    )(*ins)


def rms_fn(x, w):
    return (x * lax.rsqrt(jnp.mean(x * x, axis=-1, keepdims=True) + EPS) * w,)


def swiglu_fn(g, u):
    return (g * jax.nn.sigmoid(g) * u,)


def gated_norm_fn(y, z, w):
    v = y * (z * jax.nn.sigmoid(z))
    return (v * lax.rsqrt(jnp.mean(v * v, axis=-1, keepdims=True) + EPS) * w,)


def combine_fn(o1, o2, o3, l1, l2, l3):
    m = jnp.maximum(jnp.maximum(l1, l2), l3)
    e1, e2, e3 = jnp.exp(l1 - m), jnp.exp(l2 - m), jnp.exp(l3 - m)
    inv = 1.0 / (e1 + e2 + e3)
    return ((e1 * inv) * o1 + (e2 * inv) * o2 + (e3 * inv) * o3,)


def loss_and_grad(h, target, w, *, tr=512):
    t, d = h.shape

    def loss_fn(hv, wv, tv):
        err = rms_fn(hv, wv)[0] - tv
        per_row = jnp.mean(err * err, axis=-1, keepdims=True)
        return 0.5 * jnp.sum(per_row, axis=0, keepdims=True)

    def body(h_ref, t_ref, w_ref, dh_ref, dw_ref, loss_ref):
        i = pl.program_id(0)

        @pl.when(i == 0)
        def _():
            dw_ref[...] = jnp.zeros_like(dw_ref)
            loss_ref[...] = jnp.zeros_like(loss_ref)

        tv = t_ref[...]
        val, vjp = jax.vjp(lambda hv, wv: loss_fn(hv, wv, tv), h_ref[...], w_ref[...])
        dh, dw = vjp(jnp.ones((1, 1), F32))
        dh_ref[...] = dh
        dw_ref[...] += dw
        loss_ref[...] += jnp.broadcast_to(val, loss_ref.shape)

    row = pl.BlockSpec((tr, d), lambda i: (i, 0))
    par = pl.BlockSpec((1, d), lambda i: (0, 0))
    return pl.pallas_call(
        body, name="loss_and_grad", grid=(t // tr,), in_specs=[row, row, par],
        out_specs=[row, par, pl.BlockSpec((1, LANE), lambda i: (0, 0))],
        out_shape=[jax.ShapeDtypeStruct((t, d), F32), jax.ShapeDtypeStruct((1, d), F32),
                   jax.ShapeDtypeStruct((1, LANE), F32)],
        compiler_params=_params(("arbitrary",)),
    )(h, target, w)


def _split3(x):
    hi = x.astype(BF16)
    r1 = x - hi.astype(F32)
    mid = r1.astype(BF16)
    lo = (r1 - mid.astype(F32)).astype(BF16)
    return hi, mid, lo


def _dot01_left(m01, x):
    return sum(jnp.dot(m01, p, preferred_element_type=F32) for p in _split3(x))


def _dot01_right(x, m01):
    return sum(jnp.dot(p, m01, preferred_element_type=F32) for p in _split3(x))


def rotary(xs_list, cosf, sinf, scale, *, adjoint, name, ts=512):
    b, h, s, c = xs_list[0].shape
    n_x = len(xs_list)

    def body(*refs):
        x = refs[0][0, 0]
        for r in refs[1:n_x]:
            x = x + r[0, 0]
        cos_v, sin_v = refs[n_x][0], refs[n_x + 1][0]
        o_ref = refs[n_x + 2]
        ci = lax.broadcasted_iota(jnp.int32, (c, c), 0)
        cj = lax.broadcasted_iota(jnp.int32, (c, c), 1)
        swap = ((cj == ci + ROPE_HALF) & (ci < ROPE_HALF)) | ((cj == ci - ROPE_HALF) & (ci >= ROPE_HALF) & (ci < ROPE_DIM))
        swap = swap.astype(BF16)
        if adjoint:
            out = x * cos_v + _dot01_right(x * sin_v, swap)
        else:
            out = x * cos_v + _dot01_right(x, swap) * sin_v
        o_ref[0, 0] = out * scale

    x_spec = pl.BlockSpec((1, 1, ts, c), lambda bi, hi, si: (bi, hi, si, 0))
    t_spec = pl.BlockSpec((1, ts, c), lambda bi, hi, si: (bi, si, 0))
    return pl.pallas_call(
        body, name=name, grid=(b, h, s // ts), in_specs=[x_spec] * n_x + [t_spec, t_spec], out_specs=x_spec,
        out_shape=jax.ShapeDtypeStruct((b, h, s, c), F32),
        compiler_params=_params(("parallel", "parallel", "parallel")),
    )(*xs_list, cosf, sinf)


def add3(a, b, c, *, name, tr=1024):
    def fn(x, y, z):
        return (x + y + z,)
    return rowwise_fwd(fn, [a, b, c], [], [F32], name=name, tr=tr)[0]


def _attn_mask(n):
    rows = GQA * ATTN_BLOCK
    qi = lax.broadcasted_iota(jnp.int32, (rows, 2 * ATTN_BLOCK), 0) % ATTN_BLOCK
    ki = lax.broadcasted_iota(jnp.int32, (rows, 2 * ATTN_BLOCK), 1)
    delta = qi + ATTN_BLOCK - ki
    return (delta >= 0) & (delta <= ATTN_BLOCK) & ((n - 1) * ATTN_BLOCK + ki >= 0)


def _attn_specs(l):
    q_spec = pl.BlockSpec((1, GQA, ATTN_BLOCK, HEAD_DIM), lambda p, n: (p, 0, n, 0))
    l_spec = pl.BlockSpec((1, GQA, ATTN_BLOCK, 1), lambda p, n: (p, 0, n, 0))
    kprev = pl.BlockSpec((1, ATTN_BLOCK, HEAD_DIM), lambda p, n: (p, jnp.maximum(n - 1, 0), 0))
    kcur = pl.BlockSpec((1, ATTN_BLOCK, HEAD_DIM), lambda p, n: (p, n, 0))
    kfull = pl.BlockSpec((1, l, HEAD_DIM), lambda p, n: (p, 0, 0))
    return q_spec, l_spec, kprev, kcur, kfull


def attn_branch_fwd(q, k, v, *, name):
    p_cnt, _, l, _ = q.shape
    rows = GQA * ATTN_BLOCK
    q_spec, l_spec, kprev, kcur, _ = _attn_specs(l)

    def body(q_ref, kp_ref, kc_ref, vp_ref, vc_ref, o_ref, lse_ref):
        n = pl.program_id(1)
        qv = q_ref[0].reshape(rows, HEAD_DIM).astype(BF16)
        kk = jnp.concatenate([kp_ref[0], kc_ref[0]], axis=0).astype(BF16)
        vv = jnp.concatenate([vp_ref[0], vc_ref[0]], axis=0).astype(BF16)
        s = lax.dot_general(qv, kk, (((1,), (1,)), ((), ())), preferred_element_type=F32)
        s = jnp.where(_attn_mask(n), s, NEG_BIG)
        m = jnp.max(s, axis=-1, keepdims=True)
        pr = jnp.exp(s - m)
        den = jnp.sum(pr, axis=-1, keepdims=True)
        o = jnp.dot(pr.astype(BF16), vv, preferred_element_type=F32) / den
        o_ref[0] = o.reshape(GQA, ATTN_BLOCK, HEAD_DIM)
        lse_ref[0] = (m + jnp.log(den)).reshape(GQA, ATTN_BLOCK, 1)

    return pl.pallas_call(
        body, name=name, grid=(p_cnt, l // ATTN_BLOCK), in_specs=[q_spec, kprev, kcur, kprev, kcur],
        out_specs=[q_spec, l_spec],
        out_shape=[jax.ShapeDtypeStruct(q.shape, F32), jax.ShapeDtypeStruct(q.shape[:3] + (1,), F32)],
        compiler_params=_params(("parallel", "arbitrary")),
    )(q, k, k, v, v)


def attn_branch_bwd(q, k, v, o, lse, do, dlse, *, name):
    p_cnt, _, l, _ = q.shape
    rows = GQA * ATTN_BLOCK
    q_spec, l_spec, kprev, kcur, kfull = _attn_specs(l)

    def body(q_ref, kp_ref, kc_ref, vp_ref, vc_ref, o_ref, lse_ref, do_ref, dlse_ref, dq_ref, dk_ref, dv_ref):
        n = pl.program_id(1)

        @pl.when(n == 0)
        def _():
            dk_ref[...] = jnp.zeros_like(dk_ref)
            dv_ref[...] = jnp.zeros_like(dv_ref)

        qv = q_ref[0].reshape(rows, HEAD_DIM).astype(BF16)
        kk = jnp.concatenate([kp_ref[0], kc_ref[0]], axis=0).astype(BF16)
        vv = jnp.concatenate([vp_ref[0], vc_ref[0]], axis=0).astype(BF16)
        ov = o_ref[0].reshape(rows, HEAD_DIM)
        dov = do_ref[0].reshape(rows, HEAD_DIM)
        lsev = lse_ref[0].reshape(rows, 1)
        dlsev = dlse_ref[0].reshape(rows, 1)
        s = lax.dot_general(qv, kk, (((1,), (1,)), ((), ())), preferred_element_type=F32)
        pr = jnp.where(_attn_mask(n), jnp.exp(s - lsev), 0.0)
        do16 = dov.astype(BF16)
        dv = lax.dot_general(pr.astype(BF16), do16, (((0,), (0,)), ((), ())), preferred_element_type=F32)
        dp = lax.dot_general(do16, vv, (((1,), (1,)), ((), ())), preferred_element_type=F32)
        delta = jnp.sum(dov * ov, axis=-1, keepdims=True)
        ds = (pr * (dp - delta + dlsev)).astype(BF16)
        dq = jnp.dot(ds, kk, preferred_element_type=F32)
        dk = lax.dot_general(ds, qv, (((0,), (0,)), ((), ())), preferred_element_type=F32)
        dq_ref[0] = dq.reshape(GQA, ATTN_BLOCK, HEAD_DIM)
        cur = pl.ds(pl.multiple_of(n * ATTN_BLOCK, ATTN_BLOCK), ATTN_BLOCK)
        dk_ref[0, cur, :] += dk[ATTN_BLOCK:]
        dv_ref[0, cur, :] += dv[ATTN_BLOCK:]

        @pl.when(n > 0)
        def _():
            prev = pl.ds(pl.multiple_of((n - 1) * ATTN_BLOCK, ATTN_BLOCK), ATTN_BLOCK)
            dk_ref[0, prev, :] += dk[:ATTN_BLOCK]
            dv_ref[0, prev, :] += dv[:ATTN_BLOCK]

    return pl.pallas_call(
        body, name=name, grid=(p_cnt, l // ATTN_BLOCK),
        in_specs=[q_spec, kprev, kcur, kprev, kcur, q_spec, l_spec, q_spec, l_spec],
        out_specs=[q_spec, kfull, kfull],
        out_shape=[jax.ShapeDtypeStruct(q.shape, F32), jax.ShapeDtypeStruct(k.shape, F32),
                   jax.ShapeDtypeStruct(v.shape, F32)],
        compiler_params=_params(("parallel", "arbitrary")),
    )(q, k, k, v, v, o, lse, do, dlse)


CONV_TC = 256
CONV_COL0 = XBC_COL // CONV_TC


def _shift_down(u, s):
    if s == 0:
        return u
    rows = lax.broadcasted_iota(jnp.int32, u.shape, 0)
    return jnp.where(rows >= s, pltpu.roll(u, s, 0), 0.0)


def _shift_up(u, s):
    if s == 0:
        return u
    n = u.shape[0]
    rows = lax.broadcasted_iota(jnp.int32, u.shape, 0)
    return jnp.where(rows < n - s, pltpu.roll(u, n - s, 0), 0.0)


def conv_silu_fwd(proj3, w, bias, *, name):
    b, s, _ = proj3.shape
    u_spec = pl.BlockSpec((1, s, CONV_TC), lambda j, bi: (bi, 0, CONV_COL0 + j))
    o_spec = pl.BlockSpec((1, s, CONV_TC), lambda j, bi: (bi, 0, j))
    w_spec = pl.BlockSpec((CONV_WIDTH, CONV_TC), lambda j, bi: (0, j))
    b_spec = pl.BlockSpec((1, CONV_TC), lambda j, bi: (0, j))

    def body(u_ref, w_ref, b_ref, o_ref):
        u = u_ref[0]
        y = jnp.broadcast_to(b_ref[...], u.shape)
        for k in range(CONV_WIDTH):
            y = y + w_ref[k:k + 1, :] * _shift_down(u, CONV_WIDTH - 1 - k)
        o_ref[0] = y * jax.nn.sigmoid(y)

    return pl.pallas_call(
        body, name=name, grid=(CONV_CH // CONV_TC, b), in_specs=[u_spec, w_spec, b_spec], out_specs=o_spec,
        out_shape=jax.ShapeDtypeStruct((b, s, CONV_CH), F32),
        compiler_params=_params(("parallel", "arbitrary")),
    )(proj3, w, bias)


def conv_silu_bwd(proj3, w, bias, dact, *, name):
    b, s, _ = proj3.shape
    u_spec = pl.BlockSpec((1, s, CONV_TC), lambda j, bi: (bi, 0, CONV_COL0 + j))
    o_spec = pl.BlockSpec((1, s, CONV_TC), lambda j, bi: (bi, 0, j))
    w_spec = pl.BlockSpec((CONV_WIDTH, CONV_TC), lambda j, bi: (0, j))
    b_spec = pl.BlockSpec((1, CONV_TC), lambda j, bi: (0, j))

    def body(u_ref, w_ref, b_ref, g_ref, du_ref, dw_ref, db_ref):
        bi = pl.program_id(1)

        @pl.when(bi == 0)
        def _():
            dw_ref[...] = jnp.zeros_like(dw_ref)
            db_ref[...] = jnp.zeros_like(db_ref)

        u = u_ref[0]
        y = jnp.broadcast_to(b_ref[...], u.shape)
        shifted = [_shift_down(u, CONV_WIDTH - 1 - k) for k in range(CONV_WIDTH)]
        for k in range(CONV_WIDTH):
            y = y + w_ref[k:k + 1, :] * shifted[k]
        sig = jax.nn.sigmoid(y)
        dy = g_ref[0] * (sig * (1.0 + y * (1.0 - sig)))
        du = jnp.zeros_like(u)
        for k in range(CONV_WIDTH):
            du = du + w_ref[k:k + 1, :] * _shift_up(dy, CONV_WIDTH - 1 - k)
            dw_ref[k:k + 1, :] += jnp.sum(dy * shifted[k], axis=0, keepdims=True)
        du_ref[0] = du
        db_ref[...] += jnp.sum(dy, axis=0, keepdims=True)

    return pl.pallas_call(
        body, name=name, grid=(CONV_CH // CONV_TC, b), in_specs=[u_spec, w_spec, b_spec, o_spec],
        out_specs=[o_spec, w_spec, b_spec],
        out_shape=[jax.ShapeDtypeStruct((b, s, CONV_CH), F32), jax.ShapeDtypeStruct((CONV_WIDTH, CONV_CH), F32),
                   jax.ShapeDtypeStruct((1, CONV_CH), F32)],
        compiler_params=_params(("parallel", "arbitrary")),
    )(proj3, w, bias, dact)


def _softplus(z):
    e = jnp.exp(-jnp.abs(z))
    u = 1.0 + e
    log1p = jnp.where(u == 1.0, e, jnp.log(u) * e / jnp.where(u == 1.0, 1.0, u - 1.0))
    return jnp.maximum(z, 0.0) + log1p


def _tri(lower):
    r = lax.broadcasted_iota(jnp.int32, (CHUNK, CHUNK), 0)
    c = lax.broadcasted_iota(jnp.int32, (CHUNK, CHUNK), 1)
    return (r >= c) if lower else (r <= c)


def _ssd_common(dtr_ref, dtb_ref, alog_ref):
    z = dtr_ref[0] + dtb_ref[...]
    dt = _softplus(z)
    aneg = -jnp.exp(alog_ref[...])
    acs = _dot01_left(_tri(True).astype(BF16), dt * aneg)
    return z, dt, aneg, acs


def _col(mat, onehot):
    return jnp.sum(mat * onehot, axis=1, keepdims=True)


def _ssd_head(x, dt_j, acs_j, cb, tri_mask, last_row):
    acs_last = jnp.sum(acs_j * last_row, axis=0, keepdims=True)
    xg = x * dt_j
    bc = jnp.broadcast_to(acs_j, (CHUNK, CHUNK))
    dm = bc - bc.T
    lm = jnp.where(tri_mask, jnp.exp(jnp.where(tri_mask, dm, 0.0)), 0.0)
    mm = cb * lm
    decay_s = jnp.exp(acs_last - acs_j)
    return acs_last, xg, lm, mm, decay_s


def _ssd_specs(nc, reverse):
    cidx = (lambda c: nc - 1 - c) if reverse else (lambda c: c)
    act_spec = pl.BlockSpec((1, CHUNK, CONV_CH), lambda b, c: (b, cidx(c), 0))
    y_spec = pl.BlockSpec((1, CHUNK, SSM_INNER), lambda b, c: (b, cidx(c), 0))
    dt_in_spec = pl.BlockSpec((1, CHUNK, LANE), lambda b, c: (b, cidx(c), DT_COL // LANE))
    dt_out_spec = pl.BlockSpec((1, CHUNK, LANE), lambda b, c: (b, cidx(c), 0))
    par_spec = pl.BlockSpec((1, LANE), lambda b, c: (0, 0))
    h_spec = pl.BlockSpec((1, SSM_HEADS, 1, SSM_P, D_STATE), lambda b, c: (b, 0, cidx(c), 0, 0))
    return act_spec, y_spec, dt_in_spec, dt_out_spec, par_spec, h_spec


def _head_cols(h):
    return slice(h * SSM_P, (h + 1) * SSM_P)


def _group_cols(g, which):
    start = SSM_INNER + which * SSM_GROUPS * D_STATE + g * D_STATE
    return slice(start, start + D_STATE)


def ssd_fwd(act3, proj3, dtb, alog, dsk, *, name):
    b, s, _ = act3.shape
    nc = s // CHUNK
    act_spec, y_spec, dt_in_spec, _, par_spec, h_spec = _ssd_specs(nc, False)

    def body(act_ref, dtr_ref, dtb_ref, alog_ref, dsk_ref, y_ref, hp_ref, state):
        c = pl.program_id(1)

        @pl.when(c == 0)
        def _():
            state[...] = jnp.zeros_like(state)

        _, dt, _, acs = _ssd_common(dtr_ref, dtb_ref, alog_ref)
        tri_mask = _tri(True)
        last_row = (lax.broadcasted_iota(jnp.int32, (CHUNK, 1), 0) == CHUNK - 1).astype(F32)
        lanes = lax.broadcasted_iota(jnp.int32, (1, LANE), 1)
        for g in range(SSM_GROUPS):
            b16 = act_ref[0, :, _group_cols(g, 0)].astype(BF16)
            c16 = act_ref[0, :, _group_cols(g, 1)].astype(BF16)
            cb = lax.dot_general(c16, b16, (((1,), (1,)), ((), ())), preferred_element_type=F32)
            for j in range(HEADS_PER_GROUP):
                hidx = g * HEADS_PER_GROUP + j
                onehot = (lanes == hidx).astype(F32)
                x = act_ref[0, :, _head_cols(hidx)]
                dt_j, acs_j = _col(dt, onehot), _col(acs, onehot)
                acs_last, xg, _, mm, decay_s = _ssd_head(x, dt_j, acs_j, cb, tri_mask, last_row)
                y_diag = jnp.dot(mm.astype(BF16), xg.astype(BF16), preferred_element_type=F32)
                st = lax.dot_general((xg * decay_s).astype(BF16), b16, (((0,), (0,)), ((), ())), preferred_element_type=F32)
                hp = state[hidx]
                hp_ref[0, hidx, 0] = hp
                y_off = lax.dot_general(c16, hp.astype(BF16), (((1,), (1,)), ((), ())), preferred_element_type=F32)
                d_j = jnp.sum(dsk_ref[...] * onehot, axis=1, keepdims=True)
                y_ref[0, :, _head_cols(hidx)] = y_diag + y_off * jnp.exp(acs_j) + d_j * x
                state[hidx] = hp * jnp.exp(acs_last) + st

    return pl.pallas_call(
        body, name=name, grid=(b, nc),
        in_specs=[act_spec, dt_in_spec, par_spec, par_spec, par_spec],
        out_specs=[y_spec, h_spec],
        out_shape=[jax.ShapeDtypeStruct((b, s, SSM_INNER), F32),
                   jax.ShapeDtypeStruct((b, SSM_HEADS, nc, SSM_P, D_STATE), F32)],
        scratch_shapes=[pltpu.VMEM((SSM_HEADS, SSM_P, D_STATE), F32)],
        compiler_params=_params(("arbitrary", "arbitrary")),
    )(act3, proj3, dtb, alog, dsk)


def ssd_bwd(act3, proj3, dtb, alog, dsk, hprev, dy3, *, name):
    b, s, _ = act3.shape
    nc = s // CHUNK
    act_spec, y_spec, dt_in_spec, dt_out_spec, par_spec, h_spec = _ssd_specs(nc, True)
    dpar_spec = pl.BlockSpec((8, LANE), lambda bi, c: (0, 0))

    def body(act_ref, dtr_ref, dtb_ref, alog_ref, dsk_ref, hp_ref, dy_ref, dact_ref, ddtr_ref, dpar_ref, dstate):
        bi, c = pl.program_id(0), pl.program_id(1)

        @pl.when(c == 0)
        def _():
            dstate[...] = jnp.zeros_like(dstate)

        @pl.when((bi == 0) & (c == 0))
        def _():
            dpar_ref[...] = jnp.zeros_like(dpar_ref)

        z, dt, aneg, acs = _ssd_common(dtr_ref, dtb_ref, alog_ref)
        tri_mask = _tri(True)
        last_row = (lax.broadcasted_iota(jnp.int32, (CHUNK, 1), 0) == CHUNK - 1).astype(F32)
        lanes = lax.broadcasted_iota(jnp.int32, (1, LANE), 1)
        ddt_mat = jnp.zeros((CHUNK, LANE), F32)
        dacs_mat = jnp.zeros((CHUNK, LANE), F32)
        ddsk_row = jnp.zeros((1, LANE), F32)
        for g in range(SSM_GROUPS):
            b16 = act_ref[0, :, _group_cols(g, 0)].astype(BF16)
            c16 = act_ref[0, :, _group_cols(g, 1)].astype(BF16)
            cb = lax.dot_general(c16, b16, (((1,), (1,)), ((), ())), preferred_element_type=F32)
            dcb = jnp.zeros((CHUNK, CHUNK), F32)
            db_acc = jnp.zeros((CHUNK, D_STATE), F32)
            dc_acc = jnp.zeros((CHUNK, D_STATE), F32)
            for j in range(HEADS_PER_GROUP):
                hidx = g * HEADS_PER_GROUP + j
                onehot = (lanes == hidx).astype(F32)
                x = act_ref[0, :, _head_cols(hidx)]
                dt_j, acs_j = _col(dt, onehot), _col(acs, onehot)
                acs_last, xg, lm, mm, decay_s = _ssd_head(x, dt_j, acs_j, cb, tri_mask, last_row)
                ea = jnp.exp(acs_j)
                cd = jnp.exp(acs_last)
                d_j = jnp.sum(dsk_ref[...] * onehot, axis=1, keepdims=True)
                hp = hp_ref[0, hidx, 0]
                hp16 = hp.astype(BF16)
                g_y = dy_ref[0, :, _head_cols(hidx)]
                g_y16 = g_y.astype(BF16)
                g_hn = dstate[hidx]
                g_hn16 = g_hn.astype(BF16)
                xg16 = xg.astype(BF16)
                ddsk_row = ddsk_row + jnp.sum(jnp.sum(g_y * x, axis=1, keepdims=True), axis=0, keepdims=True) * onehot
                d_mm = lax.dot_general(g_y16, xg16, (((1,), (1,)), ((), ())), preferred_element_type=F32)
                d_xg = lax.dot_general(mm.astype(BF16), g_y16, (((0,), (0,)), ((), ())), preferred_element_type=F32)
                dcb = dcb + d_mm * lm
                d_dm = d_mm * mm
                d_acs = jnp.sum(d_dm, axis=1, keepdims=True) - jnp.sum(d_dm.T, axis=1, keepdims=True)
                t_off = lax.dot_general(c16, hp16, (((1,), (1,)), ((), ())), preferred_element_type=F32)
                d_t16 = (g_y * ea).astype(BF16)
                d_acs = d_acs + jnp.sum(g_y * t_off, axis=1, keepdims=True) * ea
                dc_acc = dc_acc + jnp.dot(d_t16, hp16, preferred_element_type=F32)
                d_hp = lax.dot_general(d_t16, c16, (((0,), (0,)), ((), ())), preferred_element_type=F32) + g_hn * cd
                d_last = jnp.sum(jnp.sum(g_hn * hp, axis=1, keepdims=True), axis=0, keepdims=True) * cd
                d_w = lax.dot_general(b16, g_hn16, (((1,), (1,)), ((), ())), preferred_element_type=F32)
                db_acc = db_acc + jnp.dot((xg * decay_s).astype(BF16), g_hn16, preferred_element_type=F32)
                d_xg = d_xg + d_w * decay_s
                d_ds = jnp.sum(d_w * xg, axis=1, keepdims=True) * decay_s
                d_last = d_last + jnp.sum(d_ds, axis=0, keepdims=True)
                d_acs = d_acs - d_ds + d_last * last_row
                dact_ref[0, :, _head_cols(hidx)] = d_j * g_y + d_xg * dt_j
                ddt_mat = ddt_mat + jnp.sum(d_xg * x, axis=1, keepdims=True) * onehot
                dacs_mat = dacs_mat + d_acs * onehot
                dstate[hidx] = d_hp
            dcb16 = dcb.astype(BF16)
            dact_ref[0, :, _group_cols(g, 1)] = dc_acc + jnp.dot(dcb16, b16, preferred_element_type=F32)
            dact_ref[0, :, _group_cols(g, 0)] = db_acc + lax.dot_general(dcb16, c16, (((0,), (0,)), ((), ())),
                                                                         preferred_element_type=F32)
        d_a = _dot01_left(_tri(False).astype(BF16), dacs_mat)
        ddt_mat = ddt_mat + d_a * aneg
        d_raw = ddt_mat * jax.nn.sigmoid(z)
        ddtr_ref[0] = d_raw
        dpar_ref[0:1, :] += jnp.sum(d_raw, axis=0, keepdims=True)
        dpar_ref[1:2, :] += jnp.sum(d_a * dt, axis=0, keepdims=True) * aneg
        dpar_ref[2:3, :] += ddsk_row

    return pl.pallas_call(
        body, name=name, grid=(b, nc),
        in_specs=[act_spec, dt_in_spec, par_spec, par_spec, par_spec, h_spec, y_spec],
        out_specs=[act_spec, dt_out_spec, dpar_spec],
        out_shape=[jax.ShapeDtypeStruct(act3.shape, F32), jax.ShapeDtypeStruct((b, s, LANE), F32),
                   jax.ShapeDtypeStruct((8, LANE), F32)],
        scratch_shapes=[pltpu.VMEM((SSM_HEADS, SSM_P, D_STATE), F32)],
        compiler_params=_params(("arbitrary", "arbitrary")),
    )(act3, proj3, dtb, alog, dsk, hprev, dy3)


def _unused_ssd_specs(nc, reverse):
    cidx = (lambda c: nc - 1 - c) if reverse else (lambda c: c)
    x_spec = pl.BlockSpec((1, HEADS_PER_GROUP, CHUNK, SSM_P), lambda b, c, g: (b, g, cidx(c), 0))
    bc_spec = pl.BlockSpec((1, 1, CHUNK, D_STATE), lambda b, c, g: (b, g, cidx(c), 0))
    dt_spec = pl.BlockSpec((1, CHUNK, LANE), lambda b, c, g: (b, cidx(c), 0))
    par_spec = pl.BlockSpec((1, LANE), lambda b, c, g: (0, 0))
    h_spec = pl.BlockSpec((1, HEADS_PER_GROUP, 1, SSM_P, D_STATE), lambda b, c, g: (b, g, cidx(c), 0, 0))
    return x_spec, bc_spec, dt_spec, par_spec, h_spec


def _unused_ssd_fwd(xs, bm, cm, dtr, dtb, alog, dsk, *, name):
    b, _, s, _ = xs.shape
    nc = s // CHUNK
    x_spec, bc_spec, dt_spec, par_spec, h_spec = _ssd_specs(nc, False)

    def body(x_ref, b_ref, c_ref, dtr_ref, dtb_ref, alog_ref, dsk_ref, y_ref, hp_ref, state):
        c, g = pl.program_id(1), pl.program_id(2)

        @pl.when(c == 0)
        def _():
            state[pl.ds(g * HEADS_PER_GROUP, HEADS_PER_GROUP)] = jnp.zeros((HEADS_PER_GROUP, SSM_P, D_STATE), F32)

        _, dt, _, acs = _ssd_common(dtr_ref, dtb_ref, alog_ref)
        b16, c16 = b_ref[0, 0].astype(BF16), c_ref[0, 0].astype(BF16)
        cb = lax.dot_general(c16, b16, (((1,), (1,)), ((), ())), preferred_element_type=F32)
        tri_mask = _tri(True)
        last_row = (lax.broadcasted_iota(jnp.int32, (CHUNK, 1), 0) == CHUNK - 1).astype(F32)
        lanes = lax.broadcasted_iota(jnp.int32, (1, LANE), 1)
        for j in range(HEADS_PER_GROUP):
            hidx = g * HEADS_PER_GROUP + j
            onehot = (lanes == hidx).astype(F32)
            x = x_ref[0, j]
            dt_j, acs_j = _col(dt, onehot), _col(acs, onehot)
            acs_last, xg, _, mm, decay_s = _ssd_head(x, dt_j, acs_j, cb, tri_mask, last_row)
            xg16 = xg.astype(BF16)
            y_diag = jnp.dot(mm.astype(BF16), xg16, preferred_element_type=F32)
            st = lax.dot_general((xg * decay_s).astype(BF16), b16, (((0,), (0,)), ((), ())), preferred_element_type=F32)
            hp = state[hidx]
            hp_ref[0, j, 0] = hp
            y_off = lax.dot_general(c16, hp.astype(BF16), (((1,), (1,)), ((), ())), preferred_element_type=F32)
            d_j = jnp.sum(dsk_ref[...] * onehot, axis=1, keepdims=True)
            y_ref[0, j] = y_diag + y_off * jnp.exp(acs_j) + d_j * x
            state[hidx] = hp * jnp.exp(acs_last) + st

    return pl.pallas_call(
        body, name=name, grid=(b, nc, SSM_GROUPS),
        in_specs=[x_spec, bc_spec, bc_spec, dt_spec, par_spec, par_spec, par_spec],
        out_specs=[x_spec, h_spec],
        out_shape=[jax.ShapeDtypeStruct(xs.shape, F32),
                   jax.ShapeDtypeStruct((b, SSM_HEADS, nc, SSM_P, D_STATE), F32)],
        scratch_shapes=[pltpu.VMEM((SSM_HEADS, SSM_P, D_STATE), F32)],
        compiler_params=_params(("arbitrary", "arbitrary", "arbitrary")),
    )(xs, bm, cm, dtr, dtb, alog, dsk)


def _unused_ssd_bwd(xs, bm, cm, dtr, dtb, alog, dsk, hprev, dy, *, name):
    b, _, s, _ = xs.shape
    nc = s // CHUNK
    x_spec, bc_spec, dt_spec, par_spec, h_spec = _ssd_specs(nc, True)
    dpar_spec = pl.BlockSpec((8, LANE), lambda bi, c, g: (0, 0))

    def body(x_ref, b_ref, c_ref, dtr_ref, dtb_ref, alog_ref, dsk_ref, hp_ref, dy_ref,
             dx_ref, db_ref, dc_ref, ddtr_ref, dpar_ref, dstate):
        bi, c, g = pl.program_id(0), pl.program_id(1), pl.program_id(2)

        @pl.when(c == 0)
        def _():
            dstate[pl.ds(g * HEADS_PER_GROUP, HEADS_PER_GROUP)] = jnp.zeros((HEADS_PER_GROUP, SSM_P, D_STATE), F32)

        @pl.when((bi == 0) & (c == 0) & (g == 0))
        def _():
            dpar_ref[...] = jnp.zeros_like(dpar_ref)

        z, dt, aneg, acs = _ssd_common(dtr_ref, dtb_ref, alog_ref)
        bv, cv = b_ref[0, 0], c_ref[0, 0]
        b16, c16 = bv.astype(BF16), cv.astype(BF16)
        cb = lax.dot_general(c16, b16, (((1,), (1,)), ((), ())), preferred_element_type=F32)
        tri_mask = _tri(True)
        last_row = (lax.broadcasted_iota(jnp.int32, (CHUNK, 1), 0) == CHUNK - 1).astype(F32)
        lanes = lax.broadcasted_iota(jnp.int32, (1, LANE), 1)
        dcb = jnp.zeros((CHUNK, CHUNK), F32)
        db_acc = jnp.zeros((CHUNK, D_STATE), F32)
        dc_acc = jnp.zeros((CHUNK, D_STATE), F32)
        ddt_mat = jnp.zeros((CHUNK, LANE), F32)
        dacs_mat = jnp.zeros((CHUNK, LANE), F32)
        ddsk_row = jnp.zeros((1, LANE), F32)
        for j in range(HEADS_PER_GROUP):
            hidx = g * HEADS_PER_GROUP + j
            onehot = (lanes == hidx).astype(F32)
            x = x_ref[0, j]
            dt_j, acs_j = _col(dt, onehot), _col(acs, onehot)
            acs_last, xg, lm, mm, decay_s = _ssd_head(x, dt_j, acs_j, cb, tri_mask, last_row)
            ea = jnp.exp(acs_j)
            cd = jnp.exp(acs_last)
            d_j = jnp.sum(dsk_ref[...] * onehot, axis=1, keepdims=True)
            hp = hp_ref[0, j, 0]
            hp16 = hp.astype(BF16)
            g_y = dy_ref[0, j]
            g_y16 = g_y.astype(BF16)
            g_hn = dstate[hidx]
            g_hn16 = g_hn.astype(BF16)
            xg16 = xg.astype(BF16)
            ddsk_row = ddsk_row + jnp.sum(jnp.sum(g_y * x, axis=1, keepdims=True), axis=0, keepdims=True) * onehot
            d_mm = lax.dot_general(g_y16, xg16, (((1,), (1,)), ((), ())), preferred_element_type=F32)
            d_xg = lax.dot_general(mm.astype(BF16), g_y16, (((0,), (0,)), ((), ())), preferred_element_type=F32)
            dcb = dcb + d_mm * lm
            d_dm = d_mm * mm
            d_acs = jnp.sum(d_dm, axis=1, keepdims=True) - jnp.sum(d_dm.T, axis=1, keepdims=True)
            t_off = lax.dot_general(c16, hp16, (((1,), (1,)), ((), ())), preferred_element_type=F32)
            d_t16 = (g_y * ea).astype(BF16)
            d_acs = d_acs + jnp.sum(g_y * t_off, axis=1, keepdims=True) * ea
            dc_acc = dc_acc + jnp.dot(d_t16, hp16, preferred_element_type=F32)
            d_hp = lax.dot_general(d_t16, c16, (((0,), (0,)), ((), ())), preferred_element_type=F32) + g_hn * cd
            d_last = jnp.sum(jnp.sum(g_hn * hp, axis=1, keepdims=True), axis=0, keepdims=True) * cd
            d_w = lax.dot_general(b16, g_hn16, (((1,), (1,)), ((), ())), preferred_element_type=F32)
            db_acc = db_acc + jnp.dot((xg * decay_s).astype(BF16), g_hn16, preferred_element_type=F32)
            d_xg = d_xg + d_w * decay_s
            d_ds = jnp.sum(d_w * xg, axis=1, keepdims=True) * decay_s
            d_last = d_last + jnp.sum(d_ds, axis=0, keepdims=True)
            d_acs = d_acs - d_ds + d_last * last_row
            dx_ref[0, j] = d_j * g_y + d_xg * dt_j
            ddt_mat = ddt_mat + jnp.sum(d_xg * x, axis=1, keepdims=True) * onehot
            dacs_mat = dacs_mat + d_acs * onehot
            dstate[hidx] = d_hp
        dcb16 = dcb.astype(BF16)
        dc_ref[0, 0] = dc_acc + jnp.dot(dcb16, b16, preferred_element_type=F32)
        db_ref[0, 0] = db_acc + lax.dot_general(dcb16, c16, (((0,), (0,)), ((), ())), preferred_element_type=F32)
        d_a = _dot01_left(_tri(False).astype(BF16), dacs_mat)
        ddt_mat = ddt_mat + d_a * aneg
        d_aneg = jnp.sum(d_a * dt, axis=0, keepdims=True)
        d_raw = ddt_mat * jax.nn.sigmoid(z)

        @pl.when(g == 0)
        def _():
            ddtr_ref[0] = d_raw

        @pl.when(g != 0)
        def _():
            ddtr_ref[0] += d_raw

        dpar_ref[0:1, :] += jnp.sum(d_raw, axis=0, keepdims=True)
        dpar_ref[1:2, :] += d_aneg * aneg
        dpar_ref[2:3, :] += ddsk_row

    return pl.pallas_call(
        body, name=name, grid=(b, nc, SSM_GROUPS),
        in_specs=[x_spec, bc_spec, bc_spec, dt_spec, par_spec, par_spec, par_spec, h_spec, x_spec],
        out_specs=[x_spec, bc_spec, bc_spec, dt_spec, dpar_spec],
        out_shape=[jax.ShapeDtypeStruct(xs.shape, F32), jax.ShapeDtypeStruct(bm.shape, F32),
                   jax.ShapeDtypeStruct(cm.shape, F32), jax.ShapeDtypeStruct(dtr.shape, F32),
                   jax.ShapeDtypeStruct((8, LANE), F32)],
        scratch_shapes=[pltpu.VMEM((SSM_HEADS, SSM_P, D_STATE), F32)],
        compiler_params=_params(("arbitrary", "arbitrary", "arbitrary")),
    )(xs, bm, cm, dtr, dtb, alog, dsk, hprev, dy)


def to_heads(x, b, s, h):
    return x.reshape(b, s, h, -1).transpose(0, 2, 1, 3)


def from_heads(x):
    b, h, s, c = x.shape
    return x.transpose(0, 2, 1, 3).reshape(b * s, h * c)


def dilate_q(q, d):
    b, _, s, c = q.shape
    x = q.reshape(b, N_KV_HEADS, GQA, s // d, d, c).transpose(0, 1, 4, 2, 3, 5)
    return x.reshape(b * N_KV_HEADS * d, GQA, s // d, c)


def undilate_q(x, b, d):
    _, _, l, c = x.shape
    y = x.reshape(b, N_KV_HEADS, d, GQA, l, c).transpose(0, 1, 3, 4, 2, 5)
    return y.reshape(b, N_Q_HEADS, l * d, c)


def dilate_kv(k, d):
    b, h, s, c = k.shape
    return k.reshape(b, h, s // d, d, c).transpose(0, 1, 3, 2, 4).reshape(b * h * d, s // d, c)


def undilate_kv(x, b, d):
    _, l, c = x.shape
    return x.reshape(b, N_KV_HEADS, d, l, c).transpose(0, 1, 3, 2, 4).reshape(b, N_KV_HEADS, l * d, c)


def rotary_tables(positions):
    inv_freq = ROPE_THETA ** (-jnp.arange(0, ROPE_DIM, 2, dtype=F32) / ROPE_DIM)
    ang = positions.astype(F32)[..., None] * inv_freq
    cos, sin = jnp.cos(ang), jnp.sin(ang)
    rest = HEAD_DIM - ROPE_DIM
    cosf = jnp.concatenate([cos, cos, jnp.ones(cos.shape[:2] + (rest,), F32)], axis=-1)
    sinf = jnp.concatenate([-sin, sin, jnp.zeros(sin.shape[:2] + (rest,), F32)], axis=-1)
    return cosf, sinf


def w_in_columns(w):
    pad = jnp.zeros((w.shape[0], IN_PAD - IN_PROJ), w.dtype)
    return jnp.concatenate([w[:, :Q_END], w[:, V_END:XBC_END], w[:, Q_END:V_END], w[:, XBC_END:], pad], axis=1)


def w_in_grad_columns(g):
    return jnp.concatenate([g[:, :Z_COL], g[:, K_COL:DT_COL], g[:, Z_COL:K_COL], g[:, DT_COL:DT_COL + SSM_HEADS]], axis=1)


def lane_pad(v):
    return jnp.pad(v.reshape(1, -1), ((0, 0), (0, LANE - v.shape[-1])))


def layer_fwd(h, wts, small, cosf, sinf, b, s, tag):
    w_in, w_out, w_gate, w_up, w_down = wts
    t = b * s
    sv = {"h": h}
    hn = rowwise_fwd(rms_fn, [h], [small["norm_mix"]], [BF16], name=f"rms_mix_{tag}")[0]
    proj = matmul(hn, w_in, name=f"in_proj_{tag}")
    sv["hn"], sv["proj"] = hn, proj
    qh = rotary([to_heads(proj[:, :Z_COL], b, s, N_Q_HEADS)], cosf, sinf, HEAD_DIM ** -0.5, adjoint=False, name=f"rope_q_{tag}")
    kh = rotary([to_heads(proj[:, K_COL:V_COL], b, s, N_KV_HEADS)], cosf, sinf, 1.0, adjoint=False, name=f"rope_k_{tag}")
    vh = to_heads(proj[:, V_COL:DT_COL], b, s, N_KV_HEADS)
    sv["qh"], sv["kh"], sv["vh"] = qh, kh, vh
    outs, lses = [], []
    for d in DILATIONS:
        o, lse = attn_branch_fwd(dilate_q(qh, d), dilate_kv(kh, d), dilate_kv(vh, d), name=f"attn_d{d}_{tag}")
        outs.append(undilate_q(o, b, d).reshape(b * N_Q_HEADS * s, HEAD_DIM))
        lses.append(undilate_q(lse, b, d).reshape(b * N_Q_HEADS * s, 1))
    sv["attn_o"], sv["attn_lse"] = outs, lses
    attn = rowwise_fwd(combine_fn, outs + lses, [], [F32], name=f"attn_combine_{tag}", tr=2048)[0]
    attn = from_heads(attn.reshape(b, N_Q_HEADS, s, HEAD_DIM))
    proj3 = proj.reshape(b, s, IN_PAD)
    act3 = conv_silu_fwd(proj3, small["conv_w"], small["conv_b"], name=f"conv_{tag}")
    y3, hprev = ssd_fwd(act3, proj3, small["dt_bias"], small["a_log"], small["d_skip"], name=f"ssd_{tag}")
    y = y3.reshape(t, SSM_INNER)
    sv["act3"], sv["hprev"], sv["y"] = act3, hprev, y
    gn = rowwise_fwd(gated_norm_fn, [y, proj], [small["ssm_norm"]], [F32], name=f"gated_norm_{tag}", groups=SSM_GROUPS,
                     windows=[None, (Z_COL, SSM_INNER)])[0]
    cat = jnp.concatenate([attn, gn], axis=1).astype(BF16)
    sv["cat"] = cat
    h1 = matmul(cat, w_out, name=f"out_proj_{tag}", residual=h)
    sv["h1"] = h1
    hn2 = rowwise_fwd(rms_fn, [h1], [small["norm_ffn"]], [BF16], name=f"rms_ffn_{tag}")[0]
    gate = matmul(hn2, w_gate, name=f"ffn_gate_{tag}")
    up = matmul(hn2, w_up, name=f"ffn_up_{tag}")
    act2 = rowwise_fwd(swiglu_fn, [gate, up], [], [BF16], name=f"swiglu_{tag}")[0]
    sv["hn2"], sv["gate"], sv["up"], sv["act2"] = hn2, gate, up, act2
    h2 = matmul(act2, w_down, name=f"ffn_down_{tag}", residual=h1)
    return h2, sv


def layer_bwd(dh2, sv, wts, small, cosf, sinf, b, s, tag):
    w_in, w_out, w_gate, w_up, w_down = wts
    t = b * s
    gr = {}
    dh2_16 = dh2.astype(BF16)
    d_act2 = matmul(dh2_16, w_down, tb=True, name=f"ffn_down_dx_{tag}")
    gr["w_down"] = matmul(sv["act2"], dh2_16, ta=True, out_dtype=BF16, name=f"ffn_down_dw_{tag}")
    d_gate, d_up = rowwise_bwd(swiglu_fn, [sv["gate"], sv["up"]], [], [d_act2], [BF16, BF16], name=f"swiglu_bwd_{tag}")
    gr["w_gate"] = matmul(sv["hn2"], d_gate, ta=True, out_dtype=BF16, name=f"ffn_gate_dw_{tag}")
    gr["w_up"] = matmul(sv["hn2"], d_up, ta=True, out_dtype=BF16, name=f"ffn_up_dw_{tag}")
    d_hn2 = matmul(d_gate, w_gate, tb=True, name=f"ffn_gate_dx_{tag}")
    d_hn2 = matmul(d_up, w_up, tb=True, residual=d_hn2, name=f"ffn_up_dx_{tag}")
    dh1, gr["norm_ffn"] = rowwise_bwd(rms_fn, [sv["h1"]], [small["norm_ffn"]], [d_hn2], [F32],
                                      name=f"rms_ffn_bwd_{tag}", add_to_first=dh2)
    dh1_16 = dh1.astype(BF16)
    d_cat = matmul(dh1_16, w_out, tb=True, name=f"out_proj_dx_{tag}")
    gr["w_out"] = matmul(sv["cat"], dh1_16, ta=True, out_dtype=BF16, name=f"out_proj_dw_{tag}")
    d_attn, d_gn = d_cat[:, :ATTN_WIDTH], d_cat[:, ATTN_WIDTH:]
    d_y, d_z, gr["ssm_norm"] = rowwise_bwd(gated_norm_fn, [sv["y"], sv["proj"]], [small["ssm_norm"]], [d_gn], [F32, F32],
                                           name=f"gated_norm_bwd_{tag}", groups=SSM_GROUPS,
                                           windows=[None, (Z_COL, SSM_INNER)])
    proj3 = sv["proj"].reshape(b, s, IN_PAD)
    d_act3, d_dtr, d_par = ssd_bwd(sv["act3"], proj3, small["dt_bias"], small["a_log"], small["d_skip"], sv["hprev"],
                                   d_y.reshape(b, s, SSM_INNER), name=f"ssd_bwd_{tag}")
    gr["dt_bias"], gr["a_log"], gr["d_skip"] = d_par[0, :SSM_HEADS], d_par[1, :SSM_HEADS], d_par[2, :SSM_HEADS]
    d_xbc, gr["conv_w"], gr["conv_b"] = conv_silu_bwd(proj3, small["conv_w"], small["conv_b"], d_act3,
                                                      name=f"conv_bwd_{tag}")
    d_attn_h = to_heads(d_attn, b, s, N_Q_HEADS).reshape(b * N_Q_HEADS * s, HEAD_DIM)
    comb = rowwise_bwd(combine_fn, sv["attn_o"] + sv["attn_lse"], [], [d_attn_h], [F32] * 6,
                       name=f"attn_combine_bwd_{tag}", tr=2048)
    dqs, dks, dvs = [], [], []
    for i, d in enumerate(DILATIONS):
        as_q = lambda a, c: dilate_q(a.reshape(b, N_Q_HEADS, s, c), d)
        dq, dk, dv = attn_branch_bwd(dilate_q(sv["qh"], d), dilate_kv(sv["kh"], d), dilate_kv(sv["vh"], d),
                                     as_q(sv["attn_o"][i], HEAD_DIM), as_q(sv["attn_lse"][i], 1),
                                     as_q(comb[i], HEAD_DIM), as_q(comb[3 + i], 1), name=f"attn_d{d}_bwd_{tag}")
        dqs.append(undilate_q(dq, b, d))
        dks.append(undilate_kv(dk, b, d))
        dvs.append(undilate_kv(dv, b, d))
    d_q = rotary(dqs, cosf, sinf, HEAD_DIM ** -0.5, adjoint=True, name=f"rope_q_bwd_{tag}")
    d_k = rotary(dks, cosf, sinf, 1.0, adjoint=True, name=f"rope_k_bwd_{tag}")
    d_v = add3(*[from_heads(a) for a in dvs], name=f"dv_sum_{tag}")
    d_proj = jnp.concatenate([from_heads(d_q), d_z, d_xbc.reshape(t, CONV_CH), from_heads(d_k), d_v,
                              d_dtr.reshape(t, LANE)], axis=1).astype(BF16)
    d_hn = matmul(d_proj, w_in, tb=True, name=f"in_proj_dx_{tag}")
    gr["w_in"] = w_in_grad_columns(matmul(sv["hn"], d_proj, ta=True, out_dtype=BF16, name=f"in_proj_dw_{tag}"))
    dh, gr["norm_mix"] = rowwise_bwd(rms_fn, [sv["h"]], [small["norm_mix"]], [d_hn], [F32],
                                     name=f"rms_mix_bwd_{tag}", add_to_first=dh1)
    return dh, gr


def local_step(x, positions, big, small_all, final_norm, loss_target):
    b, s, _ = x.shape
    t = b * s
    cosf, sinf = rotary_tables(positions)
    h = x.reshape(t, D_MODEL)
    saved = []
    for l in range(DEPTH):
        h, sv = layer_fwd(h, big[l], small_all[l], cosf, sinf, b, s, f"l{l}")
        saved.append(sv)
    dh, d_final, loss = loss_and_grad(h, loss_target.reshape(t, D_MODEL), final_norm.reshape(1, D_MODEL))
    grads = [None] * DEPTH
    for l in reversed(range(DEPTH)):
        dh, grads[l] = layer_bwd(dh, saved[l], big[l], small_all[l], cosf, sinf, b, s, f"l{l}")
    return loss, dh.reshape(b, s, D_MODEL), grads, d_final


def _slab_rows(r):
    return r if r <= 512 else _pick(r, (512, 256))


def cast_bf16(x, *, name):
    def fn(v):
        return (v,)
    return rowwise_fwd(fn, [x], [], [BF16], name=name, tr=_slab_rows(x.shape[0]))[0]


def sum_slots(x, *, name):
    n, r, c = x.shape
    tr = _slab_rows(r)

    def body(x_ref, o_ref):
        acc = x_ref[0].astype(F32)
        for i in range(1, n):
            acc = acc + x_ref[i].astype(F32)
        o_ref[...] = acc

    return pl.pallas_call(
        body, name=name, grid=(r // tr,), in_specs=[pl.BlockSpec((n, tr, c), lambda i: (0, i, 0))],
        out_specs=pl.BlockSpec((tr, c), lambda i: (i, 0)), out_shape=jax.ShapeDtypeStruct((r, c), F32),
        compiler_params=_params(("parallel",)),
    )(x)


def adamw(g_parts, w, m, v, *, name):
    r, c = w.shape
    tr = _slab_rows(r)
    n_g = len(g_parts)
    bc1 = 1.0 / (1.0 - ADAM_B1 ** ADAM_STEP)
    bc2 = 1.0 / (1.0 - ADAM_B2 ** ADAM_STEP)

    def body(*refs):
        g = refs[0][...]
        for r_ in refs[1:n_g]:
            g = g + r_[...]
        w_ref, m_ref, v_ref, g_out, d_out, m_out, v_out = refs[n_g:]
        m_new = ADAM_B1 * m_ref[...] + (1.0 - ADAM_B1) * g
        v_new = ADAM_B2 * v_ref[...] + (1.0 - ADAM_B2) * (g * g)
        g_out[...] = g
        m_out[...] = m_new
        v_out[...] = v_new
        d_out[...] = -ADAM_LR * ((m_new * bc1) / (jnp.sqrt(v_new * bc2) + ADAM_EPS) + ADAM_WD * w_ref[...])

    spec = pl.BlockSpec((tr, c), lambda i: (i, 0))
    return pl.pallas_call(
        body, name=name, grid=(r // tr,), in_specs=[spec] * (n_g + 3), out_specs=[spec] * 4,
        out_shape=[jax.ShapeDtypeStruct((r, c), F32)] * 4, compiler_params=_params(("parallel",)),
    )(*g_parts, w, m, v)


def _other_chips(x, y):
    return [(1 - x, y), (x, 1 - y), (1 - x, 1 - y)]


def allgather_chips(shards):
    n_arr = len(shards)

    def body(*refs):
        in_refs, out_refs = refs[:n_arr], refs[n_arr:2 * n_arr]
        send_sems, recv_sems, local_sems = refs[2 * n_arr:]
        x, y, c = lax.axis_index("x"), lax.axis_index("y"), lax.axis_index("c")
        chip = 2 * x + y
        started = []
        for a, (in_ref, out_ref) in enumerate(zip(in_refs, out_refs)):
            mine = pltpu.make_async_copy(in_ref, out_ref.at[chip], local_sems.at[a])
            mine.start()
            started.append(mine.wait)
            for k, (px, py) in enumerate(_other_chips(x, y)):
                cp = pltpu.make_async_remote_copy(src_ref=in_ref, dst_ref=out_ref.at[chip], send_sem=send_sems.at[3 * a + k],
                                                  recv_sem=recv_sems.at[3 * a + k], device_id=(px, py, c), device_id_type=MESH)
                cp.start()
                started.append(cp.wait_send)
        for a, (in_ref, out_ref) in enumerate(zip(in_refs, out_refs)):
            for k, (px, py) in enumerate(_other_chips(x, y)):
                pltpu.make_async_remote_copy(src_ref=in_ref, dst_ref=out_ref.at[2 * px + py], send_sem=send_sems.at[3 * a + k],
                                             recv_sem=recv_sems.at[3 * a + k], device_id=(px, py, c),
                                             device_id_type=MESH).wait_recv()
        for wait in started:
            wait()

    hbm = pl.BlockSpec(memory_space=pltpu.HBM)
    return pl.pallas_call(
        body, name="allgather_weights", in_specs=[hbm] * n_arr, out_specs=[hbm] * n_arr,
        out_shape=[jax.ShapeDtypeStruct((N_CHIPS,) + s.shape, s.dtype) for s in shards],
        scratch_shapes=[pltpu.SemaphoreType.DMA((3 * n_arr,)), pltpu.SemaphoreType.DMA((3 * n_arr,)),
                        pltpu.SemaphoreType.DMA((n_arr,))],
    )(*shards)


def exchange_grads(big, small):
    def body(big_ref, small_ref, big_out, small_out, send_sems, recv_sems, local_sems):
        x, y, c = lax.axis_index("x"), lax.axis_index("y"), lax.axis_index("c")
        chip = 2 * x + y
        dev = 4 * x + 2 * y + c
        own_big = pltpu.make_async_copy(big_ref.at[chip], big_out.at[chip], local_sems.at[0])
        own_small = pltpu.make_async_copy(small_ref, small_out.at[dev], local_sems.at[1])
        own_big.start()
        own_small.start()
        sends = []
        for k, (px, py) in enumerate(_other_chips(x, y)):
            cp = pltpu.make_async_remote_copy(src_ref=big_ref.at[2 * px + py], dst_ref=big_out.at[chip],
                                              send_sem=send_sems.at[k], recv_sem=recv_sems.at[k],
                                              device_id=(px, py, c), device_id_type=MESH)
            cp.start()
            sends.append(cp)
        peers = []
        for r in range(1, N_DEV):
            fx, fy, fc = (r >> 2) & 1, (r >> 1) & 1, r & 1
            px, py, pc = (x + fx) % 2, (y + fy) % 2, (c + fc) % 2
            peers.append((px, py, pc))
            cp = pltpu.make_async_remote_copy(src_ref=small_ref, dst_ref=small_out.at[dev], send_sem=send_sems.at[2 + r],
                                              recv_sem=recv_sems.at[2 + r], device_id=(px, py, pc), device_id_type=MESH)
            cp.start()
            sends.append(cp)
        for k, (px, py) in enumerate(_other_chips(x, y)):
            pltpu.make_async_remote_copy(src_ref=big_ref.at[chip], dst_ref=big_out.at[2 * px + py],
                                         send_sem=send_sems.at[k], recv_sem=recv_sems.at[k],
                                         device_id=(px, py, c), device_id_type=MESH).wait_recv()
        for r, (px, py, pc) in zip(range(1, N_DEV), peers):
            pltpu.make_async_remote_copy(src_ref=small_ref, dst_ref=small_out.at[4 * px + 2 * py + pc],
                                         send_sem=send_sems.at[2 + r], recv_sem=recv_sems.at[2 + r],
                                         device_id=(px, py, pc), device_id_type=MESH).wait_recv()
        for cp in sends:
            cp.wait_send()
        own_big.wait()
        own_small.wait()

    hbm = pl.BlockSpec(memory_space=pltpu.HBM)
    n_sem = 3 + N_DEV - 1
    return pl.pallas_call(
        body, name="exchange_grads", in_specs=[hbm, hbm], out_specs=[hbm, hbm],
        out_shape=[jax.ShapeDtypeStruct(big.shape, big.dtype), jax.ShapeDtypeStruct((N_DEV,) + small.shape, small.dtype)],
        scratch_shapes=[pltpu.SemaphoreType.DMA((n_sem,)), pltpu.SemaphoreType.DMA((n_sem,)), pltpu.SemaphoreType.DMA((2,))],
    )(big, small)


SWAP_CHUNKS = 27


def swap_cores(mine):
    rows = mine.shape[0] // SWAP_CHUNKS
    assert rows * SWAP_CHUNKS == mine.shape[0] and rows % 8 == 0

    def body(in_ref, out_ref, send_sems, recv_sems, local_sem):
        x, y, c = lax.axis_index("x"), lax.axis_index("y"), lax.axis_index("c")
        own = pltpu.make_async_copy(in_ref, out_ref.at[c], local_sem)
        own.start()

        def chunk(k, slot):
            part = pl.ds(k * rows, rows)
            return pltpu.make_async_remote_copy(src_ref=in_ref.at[part], dst_ref=out_ref.at[slot, part],
                                                send_sem=send_sems.at[k], recv_sem=recv_sems.at[k],
                                                device_id=(x, y, 1 - c), device_id_type=MESH)

        for k in range(SWAP_CHUNKS):
            chunk(k, c).start()
        for k in range(SWAP_CHUNKS):
            chunk(k, 1 - c).wait_recv()
        for k in range(SWAP_CHUNKS):
            chunk(k, c).wait_send()
        own.wait()

    hbm = pl.BlockSpec(memory_space=pltpu.HBM)
    return pl.pallas_call(
        body, name="swap_cores", in_specs=[hbm], out_specs=hbm,
        out_shape=jax.ShapeDtypeStruct((2,) + mine.shape, mine.dtype),
        scratch_shapes=[pltpu.SemaphoreType.DMA((SWAP_CHUNKS,)), pltpu.SemaphoreType.DMA((SWAP_CHUNKS,)),
                        pltpu.SemaphoreType.DMA],
    )(mine)


BIG_NAMES = ("w_in", "w_out", "w_gate", "w_up", "w_down")
BIG_SHARD_AXIS = {"w_in": 1, "w_out": 0, "w_gate": 1, "w_up": 1, "w_down": 0}
PACK_COLS = 1024
SMALL_NAMES = ("norm_mix", "conv_w", "conv_b", "dt_bias", "a_log", "d_skip", "ssm_norm", "norm_ffn")


PACK_ROW_TILE = 256


def pack_big(shards):
    flat = jnp.concatenate([shards[n].reshape(-1) for n in BIG_NAMES])
    unit = PACK_ROW_TILE * PACK_COLS
    total = -(-flat.size // unit) * unit
    return jnp.pad(flat, (0, total - flat.size)).reshape(-1, PACK_COLS)


def unpack_big(packed, like):
    out, off = {}, 0
    flat = packed.reshape(-1)
    for n in BIG_NAMES:
        size = like[n].size
        out[n] = flat[off:off + size].reshape(like[n].shape)
        off += size
    return out


def pack_small(parts):
    flat = jnp.concatenate([p.reshape(-1).astype(F32) for p in parts])
    rows = -(-flat.size // LANE)
    rows = -(-rows // 8) * 8
    return jnp.pad(flat, (0, rows * LANE - flat.size)).reshape(rows, LANE)


def unpack_small(packed, like):
    out, off = [], 0
    flat = packed.reshape(-1)
    for a in like:
        out.append(flat[off:off + a.size].reshape(a.shape))
        off += a.size
    return out


def kernel(x, positions, norm_mix, w_in, conv_w, conv_b, dt_bias, a_log, d_skip, ssm_norm, w_out, norm_ffn, w_gate, w_up, w_down, final_norm, loss_target, m_norm_mix, m_w_in, m_conv_w, m_conv_b, m_dt_bias, m_a_log, m_d_skip, m_ssm_norm, m_w_out, m_norm_ffn, m_w_gate, m_w_up, m_w_down, m_final_norm, v_norm_mix, v_w_in, v_conv_w, v_conv_b, v_dt_bias, v_a_log, v_d_skip, v_ssm_norm, v_w_out, v_norm_ffn, v_w_gate, v_w_up, v_w_down, v_final_norm):
    chip = 2 * lax.axis_index("x") + lax.axis_index("y")
    w_sh = {"w_in": w_in, "w_out": w_out, "w_gate": w_gate, "w_up": w_up, "w_down": w_down}
    m_sh = {"w_in": m_w_in, "w_out": m_w_out, "w_gate": m_w_gate, "w_up": m_w_up, "w_down": m_w_down}
    v_sh = {"w_in": v_w_in, "w_out": v_w_out, "w_gate": v_w_gate, "w_up": v_w_up, "w_down": v_w_down}

    w_packed = pack_big(w_sh)
    conv_cols = CONV_CH // N_CHIPS
    gathered, conv_g = allgather_chips([cast_bf16(w_packed, name="cast_weights"), conv_w.reshape(-1, LANE)])
    pieces = [unpack_big(gathered[j], w_sh) for j in range(N_CHIPS)]
    full = {n: jnp.concatenate([p[n] for p in pieces], axis=BIG_SHARD_AXIS[n] + 1) for n in BIG_NAMES}
    big = []
    for l in range(DEPTH):
        big.append((w_in_columns(full["w_in"][l]), full["w_out"][l], full["w_gate"][l], full["w_up"][l], full["w_down"][l]))
    conv_w_full = jnp.concatenate([conv_g[j].reshape(DEPTH, CONV_WIDTH, conv_cols) for j in range(N_CHIPS)], axis=2)

    small_all = []
    for l in range(DEPTH):
        small_all.append({
            "norm_mix": norm_mix[l].reshape(1, -1), "conv_w": conv_w_full[l], "conv_b": conv_b[l].reshape(1, -1),
            "dt_bias": lane_pad(dt_bias[l]), "a_log": lane_pad(a_log[l]), "d_skip": lane_pad(d_skip[l]),
            "ssm_norm": ssm_norm[l].reshape(1, -1), "norm_ffn": norm_ffn[l].reshape(1, -1)})

    loss_part, grad_x, grads, d_final = local_step(x, positions, big, small_all, final_norm, loss_target)

    def shard_of(name, g, j):
        n = g.shape[BIG_SHARD_AXIS[name]] // N_CHIPS
        return lax.slice_in_dim(g, j * n, (j + 1) * n, axis=BIG_SHARD_AXIS[name])

    to_chip = []
    for j in range(N_CHIPS):
        to_chip.append(pack_big({n: jnp.stack([shard_of(n, grads[l][n], j) for l in range(DEPTH)]) for n in BIG_NAMES}))
    small_parts = [jnp.stack([grads[l][n].reshape(-1) for l in range(DEPTH)]) for n in SMALL_NAMES]
    small_parts += [d_final.reshape(-1), loss_part.reshape(-1)]
    recv_big, recv_small = exchange_grads(jnp.stack(to_chip), pack_small(small_parts))
    plane_sum = sum_slots(recv_big, name="sum_chip_partials")
    both = swap_cores(plane_sum)

    g_big, d_big, m_big, v_big = adamw([both[0], both[1]], w_packed, pack_big(m_sh), pack_big(v_sh), name="adamw_big")
    g_big, d_big, m_big, v_big = (unpack_big(a, w_sh) for a in (g_big, d_big, m_big, v_big))

    small_sum = sum_slots(recv_small, name="sum_small")
    like = [norm_mix, conv_w_full, conv_b, dt_bias, a_log, d_skip, ssm_norm, norm_ffn, final_norm, loss_part.reshape(-1)]
    g_small = unpack_small(small_sum, like)
    loss = g_small[-1][0]
    g_small = dict(zip(SMALL_NAMES + ("final_norm",), g_small[:-1]))
    g_small["conv_w"] = lax.dynamic_slice_in_dim(g_small["conv_w"], chip * conv_cols, conv_cols, axis=2)
    w_small = {"norm_mix": norm_mix, "conv_w": conv_w, "conv_b": conv_b, "dt_bias": dt_bias, "a_log": a_log, "d_skip": d_skip,
               "ssm_norm": ssm_norm, "norm_ffn": norm_ffn, "final_norm": final_norm}
    m_small = {"norm_mix": m_norm_mix, "conv_w": m_conv_w, "conv_b": m_conv_b, "dt_bias": m_dt_bias, "a_log": m_a_log,
               "d_skip": m_d_skip, "ssm_norm": m_ssm_norm, "norm_ffn": m_norm_ffn, "final_norm": m_final_norm}
    v_small = {"norm_mix": v_norm_mix, "conv_w": v_conv_w, "conv_b": v_conv_b, "dt_bias": v_dt_bias, "a_log": v_a_log,
               "d_skip": v_d_skip, "ssm_norm": v_ssm_norm, "norm_ffn": v_norm_ffn, "final_norm": v_final_norm}
    names = SMALL_NAMES + ("final_norm",)
    order = [w_small[n] for n in names]
    res = adamw([pack_small([g_small[n] for n in names])], pack_small(order), pack_small([m_small[n] for n in names]),
                pack_small([v_small[n] for n in names]), name="adamw_small")
    g_s, d_s, m_s, v_s = (dict(zip(names, unpack_small(a, order))) for a in res)

    all_names = ("norm_mix", "w_in", "conv_w", "conv_b", "dt_bias", "a_log", "d_skip", "ssm_norm", "w_out", "norm_ffn",
                 "w_gate", "w_up", "w_down", "final_norm")
    outs = [loss, grad_x]
    for src_big, src_small in ((g_big, g_s), (d_big, d_s), (m_big, m_s), (v_big, v_s)):
        outs += [src_big[n] if n in BIG_NAMES else src_small[n] for n in all_names]
    return tuple(outs)
```

```python
import functools

import jax
import jax.numpy as jnp
from jax import lax
from jax.experimental import pallas as pl
from jax.experimental.pallas import tpu as pltpu

F32 = jnp.float32
BF16 = jnp.bfloat16
MESH = pl.DeviceIdType.MESH

D_MODEL = 1024
DEPTH = 2
HEAD_DIM = 64
N_Q_HEADS = 8
N_KV_HEADS = 2
GQA = N_Q_HEADS // N_KV_HEADS
ATTN_WIDTH = N_Q_HEADS * HEAD_DIM
ROPE_DIM = HEAD_DIM // 4
ROPE_HALF = ROPE_DIM // 2
ROPE_THETA = 500000.0
DILATIONS = (1, 4, 16)
ATTN_BLOCK = 128
SSM_P = 64
SSM_HEADS = 16
SSM_INNER = SSM_HEADS * SSM_P
SSM_GROUPS = 2
HEADS_PER_GROUP = SSM_HEADS // SSM_GROUPS
D_STATE = 128
CONV_WIDTH = 4
CHUNK = 128
CONV_CH = SSM_INNER + 2 * SSM_GROUPS * D_STATE
MIX_WIDTH = ATTN_WIDTH + SSM_INNER
Q_END = ATTN_WIDTH
K_END = Q_END + N_KV_HEADS * HEAD_DIM
V_END = K_END + N_KV_HEADS * HEAD_DIM
Z_END = V_END + SSM_INNER
XBC_END = Z_END + CONV_CH
IN_PROJ = XBC_END + SSM_HEADS
LANE = 128
IN_PAD = XBC_END + LANE
Q_COL, Z_COL, XBC_COL, K_COL, V_COL, DT_COL = 0, 512, 1536, 3072, 3200, 3328
FFN_HIDDEN = 2816
EPS = 1e-5
ADAM_LR, ADAM_B1, ADAM_B2, ADAM_EPS, ADAM_WD, ADAM_STEP = 0.001, 0.9, 0.999, 1e-8, 0.01, 10
N_CHIPS = 4
N_DEV = 8
VMEM_LIMIT = 48 * 1024 * 1024
NEG_BIG = -1e30


def _params(sem=None):
    return pltpu.CompilerParams(dimension_semantics=sem, vmem_limit_bytes=VMEM_LIMIT)


def _pick(n, prefs):
    for p in prefs:
        if n % p == 0:
            return p
    return n


def matmul(a, b, *, name, ta=False, tb=False, out_dtype=F32, residual=None):
    if ta:
        assert not tb and residual is None
        return _matmul_over_rows(a, b, name=name, out_dtype=out_dtype)
    return _matmul_full_k(a, b, name=name, tb=tb, out_dtype=out_dtype, residual=residual)


def _matmul_full_k(a, b, *, name, tb, out_dtype, residual):
    m, kdim = a.shape
    n = b.shape[0] if tb else b.shape[1]
    tm = _pick(m, (1024, 512, 256)) if kdim <= 1536 else _pick(m, (512, 256))
    tn = _pick(n, (1152, 1408, 1536, 1024, 768, 512, 384, 256, 128))
    b_spec = pl.BlockSpec((tn, kdim), lambda i, j: (j, 0)) if tb else pl.BlockSpec((kdim, tn), lambda i, j: (0, j))
    o_spec = pl.BlockSpec((tm, tn), lambda i, j: (i, j))
    dims = (((1,), (1 if tb else 0,)), ((), ()))
    has_res = residual is not None

    def body(*refs):
        a_ref, b_ref = refs[:2]
        o_ref = refs[-1]
        r = lax.dot_general(a_ref[...].astype(BF16), b_ref[...].astype(BF16), dims, preferred_element_type=F32)
        if has_res:
            r = r + refs[2][...]
        o_ref[...] = r.astype(out_dtype)

    in_specs = [pl.BlockSpec((tm, kdim), lambda i, j: (i, 0)), b_spec] + ([o_spec] if has_res else [])
    args = (a, b) + ((residual,) if has_res else ())
    return pl.pallas_call(
        body, name=name, grid=(m // tm, n // tn), in_specs=in_specs, out_specs=o_spec,
        out_shape=jax.ShapeDtypeStruct((m, n), out_dtype),
        compiler_params=_params(("parallel", "parallel")),
    )(*args)


def _matmul_over_rows(a, b, *, name, out_dtype):
    t, m = a.shape
    n = b.shape[1]
    tm = _pick(m, (1024, 1408, 768, 512, 256, 128))
    tn = _pick(n, (1152, 1408, 1024, 768, 512, 256, 128))
    tk = _pick(t, (1024, 512, 256, 128))
    nk = t // tk

    def body(a_ref, b_ref, o_ref, acc):
        k = pl.program_id(2)
        part = lax.dot_general(a_ref[...].astype(BF16), b_ref[...].astype(BF16), (((0,), (0,)), ((), ())),
                               preferred_element_type=F32)

        @pl.when(k == 0)
        def _():
            acc[...] = part

        @pl.when(k > 0)
        def _():
            acc[...] += part

        @pl.when(k == nk - 1)
        def _():
            o_ref[...] = acc[...].astype(out_dtype)

    return pl.pallas_call(
        body, name=name, grid=(m // tm, n // tn, nk),
        in_specs=[pl.BlockSpec((tk, tm), lambda i, j, k: (k, i)), pl.BlockSpec((tk, tn), lambda i, j, k: (k, j))],
        out_specs=pl.BlockSpec((tm, tn), lambda i, j, k: (i, j)),
        out_shape=jax.ShapeDtypeStruct((m, n), out_dtype),
        scratch_shapes=[pltpu.VMEM((tm, tn), F32)],
        compiler_params=_params(("parallel", "parallel", "arbitrary")),
    )(a, b)


ROW_BLOCK_BYTES = 8 * 1024 * 1024


def _row_tile(t, tr, widths, n_copies):
    lanes = sum(-(-wd // LANE) * LANE for wd in widths) * n_copies
    tr = min(tr, t)
    while tr > 8 and tr * lanes * 4 > ROW_BLOCK_BYTES:
        tr //= 2
    return tr


def _row_widths(rows, groups, windows):
    windows = windows or [None] * len(rows)
    widths = [(w[1] if w else a.shape[1]) // groups for a, w in zip(rows, windows)]
    assert all(w is None or w[0] % wd == 0 for w, wd in zip(windows, widths))
    return widths, [(w[0] // wd if w else 0) for w, wd in zip(windows, widths)]


def _row_specs(tr, widths, offs):
    return [pl.BlockSpec((tr, wd), functools.partial(lambda g, i, off: (i, g + off), off=off)) for wd, off in zip(widths, offs)]


def rowwise_fwd(fn, rows, params, out_dtypes, *, name, tr=512, groups=1, windows=None):
    t = rows[0].shape[0]
    widths, offs = _row_widths(rows, groups, windows)
    tr = _row_tile(t, tr, widths, 2)
    row_specs = _row_specs(tr, widths, offs)
    par_spec = lambda p: pl.BlockSpec((1, p.shape[1] // groups), lambda g, i: (0, g))
    n_in = len(rows) + len(params)
    out_cols = [o.shape[1] for o in jax.eval_shape(
        fn, *[jax.ShapeDtypeStruct((tr, wd), F32) for wd in widths],
        *[jax.ShapeDtypeStruct((1, p.shape[1] // groups), F32) for p in params])]

    def body(*refs):
        vals = [r[...].astype(F32) for r in refs[:n_in]]
        outs = fn(*vals)
        for o_ref, o in zip(refs[n_in:], outs):
            o_ref[...] = o.astype(o_ref.dtype)

    return pl.pallas_call(
        body, name=name, grid=(groups, t // tr),
        in_specs=row_specs + [par_spec(p) for p in params],
        out_specs=[pl.BlockSpec((tr, c), lambda g, i: (i, g)) for c in out_cols],
        out_shape=[jax.ShapeDtypeStruct((t, c * groups), d) for c, d in zip(out_cols, out_dtypes)],
        compiler_params=_params(("arbitrary", "arbitrary")),
    )(*rows, *params)


def rowwise_bwd(fn, rows, params, cts, drow_dtypes, *, name, tr=512, groups=1, add_to_first=None, windows=None):
    t = rows[0].shape[0]
    widths, offs = _row_widths(rows, groups, windows)
    tr = _row_tile(t, tr, widths + [a.shape[1] // groups for a in cts], 2)
    row_spec = lambda a: pl.BlockSpec((tr, a.shape[1] // groups), lambda g, i: (i, g))
    row_specs = _row_specs(tr, widths, offs)
    par_spec = lambda p: pl.BlockSpec((1, p.shape[1] // groups), lambda g, i: (0, g))
    n_rows, n_par, n_ct = len(rows), len(params), len(cts)
    has_add = add_to_first is not None
    n_in = n_rows + n_par + n_ct + (1 if has_add else 0)

    def body(*refs):
        i = pl.program_id(1)
        vals = [r[...].astype(F32) for r in refs[:n_rows + n_par]]
        ct_vals = tuple(r[...].astype(F32) for r in refs[n_rows + n_par:n_rows + n_par + n_ct])
        _, vjp = jax.vjp(fn, *vals)
        grads = vjp(ct_vals)
        out_refs = refs[n_in:]
        for idx in range(n_rows):
            g = grads[idx]
            if idx == 0 and has_add:
                g = g + refs[n_in - 1][...]
            out_refs[idx][...] = g.astype(out_refs[idx].dtype)
        for idx in range(n_par):
            p_ref = out_refs[n_rows + idx]

            @pl.when(i == 0)
            def _():
                p_ref[...] = jnp.zeros_like(p_ref)

            p_ref[...] += grads[n_rows + idx]

    ins = list(rows) + list(params) + list(cts) + ([add_to_first] if has_add else [])
    in_specs = (row_specs + [par_spec(p) for p in params] + [row_spec(a) for a in cts]
                + ([row_spec(add_to_first)] if has_add else []))
    return pl.pallas_call(
        body, name=name, grid=(groups, t // tr), in_specs=in_specs,
        out_specs=[pl.BlockSpec((tr, wd), lambda g, i: (i, g)) for wd in widths] + [par_spec(p) for p in params],
        out_shape=[jax.ShapeDtypeStruct((t, wd * groups), d) for wd, d in zip(widths, drow_dtypes)]
        + [jax.ShapeDtypeStruct(p.shape, F32) for p in params],
        compiler_params=_params(("arbitrary", "arbitrary")),
    )(*ins)


def rms_fn(x, w):
    return (x * lax.rsqrt(jnp.mean(x * x, axis=-1, keepdims=True) + EPS) * w,)


def swiglu_fn(g, u):
    return (g * jax.nn.sigmoid(g) * u,)


def gated_norm_fn(y, z, w):
    v = y * (z * jax.nn.sigmoid(z))
    return (v * lax.rsqrt(jnp.mean(v * v, axis=-1, keepdims=True) + EPS) * w,)


def combine_fn(o1, o2, o3, l1, l2, l3):
    m = jnp.maximum(jnp.maximum(l1, l2), l3)
    e1, e2, e3 = jnp.exp(l1 - m), jnp.exp(l2 - m), jnp.exp(l3 - m)
    inv = 1.0 / (e1 + e2 + e3)
    return ((e1 * inv) * o1 + (e2 * inv) * o2 + (e3 * inv) * o3,)


def loss_and_grad(h, target, w, *, tr=512):
    t, d = h.shape

    def loss_fn(hv, wv, tv):
        err = rms_fn(hv, wv)[0] - tv
        per_row = jnp.mean(err * err, axis=-1, keepdims=True)
        return 0.5 * jnp.sum(per_row, axis=0, keepdims=True)

    def body(h_ref, t_ref, w_ref, dh_ref, dw_ref, loss_ref):
        i = pl.program_id(0)

        @pl.when(i == 0)
        def _():
            dw_ref[...] = jnp.zeros_like(dw_ref)
            loss_ref[...] = jnp.zeros_like(loss_ref)

        tv = t_ref[...]
        val, vjp = jax.vjp(lambda hv, wv: loss_fn(hv, wv, tv), h_ref[...], w_ref[...])
        dh, dw = vjp(jnp.ones((1, 1), F32))
        dh_ref[...] = dh
        dw_ref[...] += dw
        loss_ref[...] += jnp.broadcast_to(val, loss_ref.shape)

    row = pl.BlockSpec((tr, d), lambda i: (i, 0))
    par = pl.BlockSpec((1, d), lambda i: (0, 0))
    return pl.pallas_call(
        body, name="loss_and_grad", grid=(t // tr,), in_specs=[row, row, par],
        out_specs=[row, par, pl.BlockSpec((1, LANE), lambda i: (0, 0))],
        out_shape=[jax.ShapeDtypeStruct((t, d), F32), jax.ShapeDtypeStruct((1, d), F32),
                   jax.ShapeDtypeStruct((1, LANE), F32)],
        compiler_params=_params(("arbitrary",)),
    )(h, target, w)


def _split3(x):
    hi = x.astype(BF16)
    r1 = x - hi.astype(F32)
    mid = r1.astype(BF16)
    lo = (r1 - mid.astype(F32)).astype(BF16)
    return hi, mid, lo


def _dot01_left(m01, x):
    return sum(jnp.dot(m01, p, preferred_element_type=F32) for p in _split3(x))


def _dot01_right(x, m01):
    return sum(jnp.dot(p, m01, preferred_element_type=F32) for p in _split3(x))


def rotary(xs_list, cosf, sinf, scale, *, adjoint, name, ts=512):
    b, h, s, c = xs_list[0].shape
    n_x = len(xs_list)

    def body(*refs):
        x = refs[0][0, 0]
        for r in refs[1:n_x]:
            x = x + r[0, 0]
        cos_v, sin_v = refs[n_x][0], refs[n_x + 1][0]
        o_ref = refs[n_x + 2]
        ci = lax.broadcasted_iota(jnp.int32, (c, c), 0)
        cj = lax.broadcasted_iota(jnp.int32, (c, c), 1)
        swap = ((cj == ci + ROPE_HALF) & (ci < ROPE_HALF)) | ((cj == ci - ROPE_HALF) & (ci >= ROPE_HALF) & (ci < ROPE_DIM))
        swap = swap.astype(BF16)
        if adjoint:
            out = x * cos_v + _dot01_right(x * sin_v, swap)
        else:
            out = x * cos_v + _dot01_right(x, swap) * sin_v
        o_ref[0, 0] = out * scale

    x_spec = pl.BlockSpec((1, 1, ts, c), lambda bi, hi, si: (bi, hi, si, 0))
    t_spec = pl.BlockSpec((1, ts, c), lambda bi, hi, si: (bi, si, 0))
    return pl.pallas_call(
        body, name=name, grid=(b, h, s // ts), in_specs=[x_spec] * n_x + [t_spec, t_spec], out_specs=x_spec,
        out_shape=jax.ShapeDtypeStruct((b, h, s, c), F32),
        compiler_params=_params(("parallel", "parallel", "parallel")),
    )(*xs_list, cosf, sinf)


def add3(a, b, c, *, name, tr=1024):
    def fn(x, y, z):
        return (x + y + z,)
    return rowwise_fwd(fn, [a, b, c], [], [F32], name=name, tr=tr)[0]


def _attn_mask(n):
    rows = GQA * ATTN_BLOCK
    qi = lax.broadcasted_iota(jnp.int32, (rows, 2 * ATTN_BLOCK), 0) % ATTN_BLOCK
    ki = lax.broadcasted_iota(jnp.int32, (rows, 2 * ATTN_BLOCK), 1)
    delta = qi + ATTN_BLOCK - ki
    return (delta >= 0) & (delta <= ATTN_BLOCK) & ((n - 1) * ATTN_BLOCK + ki >= 0)


def _attn_specs(l):
    q_spec = pl.BlockSpec((1, GQA, ATTN_BLOCK, HEAD_DIM), lambda p, n: (p, 0, n, 0))
    l_spec = pl.BlockSpec((1, GQA, ATTN_BLOCK, 1), lambda p, n: (p, 0, n, 0))
    kprev = pl.BlockSpec((1, ATTN_BLOCK, HEAD_DIM), lambda p, n: (p, jnp.maximum(n - 1, 0), 0))
    kcur = pl.BlockSpec((1, ATTN_BLOCK, HEAD_DIM), lambda p, n: (p, n, 0))
    kfull = pl.BlockSpec((1, l, HEAD_DIM), lambda p, n: (p, 0, 0))
    return q_spec, l_spec, kprev, kcur, kfull


def attn_branch_fwd(q, k, v, *, name):
    p_cnt, _, l, _ = q.shape
    rows = GQA * ATTN_BLOCK
    q_spec, l_spec, kprev, kcur, _ = _attn_specs(l)

    def body(q_ref, kp_ref, kc_ref, vp_ref, vc_ref, o_ref, lse_ref):
        n = pl.program_id(1)
        qv = q_ref[0].reshape(rows, HEAD_DIM).astype(BF16)
        kk = jnp.concatenate([kp_ref[0], kc_ref[0]], axis=0).astype(BF16)
        vv = jnp.concatenate([vp_ref[0], vc_ref[0]], axis=0).astype(BF16)
        s = lax.dot_general(qv, kk, (((1,), (1,)), ((), ())), preferred_element_type=F32)
        s = jnp.where(_attn_mask(n), s, NEG_BIG)
        m = jnp.max(s, axis=-1, keepdims=True)
        pr = jnp.exp(s - m)
        den = jnp.sum(pr, axis=-1, keepdims=True)
        o = jnp.dot(pr.astype(BF16), vv, preferred_element_type=F32) / den
        o_ref[0] = o.reshape(GQA, ATTN_BLOCK, HEAD_DIM)
        lse_ref[0] = (m + jnp.log(den)).reshape(GQA, ATTN_BLOCK, 1)

    return pl.pallas_call(
        body, name=name, grid=(p_cnt, l // ATTN_BLOCK), in_specs=[q_spec, kprev, kcur, kprev, kcur],
        out_specs=[q_spec, l_spec],
        out_shape=[jax.ShapeDtypeStruct(q.shape, F32), jax.ShapeDtypeStruct(q.shape[:3] + (1,), F32)],
        compiler_params=_params(("parallel", "arbitrary")),
    )(q, k, k, v, v)


def attn_branch_bwd(q, k, v, o, lse, do, dlse, *, name):
    p_cnt, _, l, _ = q.shape
    rows = GQA * ATTN_BLOCK
    q_spec, l_spec, kprev, kcur, kfull = _attn_specs(l)

    def body(q_ref, kp_ref, kc_ref, vp_ref, vc_ref, o_ref, lse_ref, do_ref, dlse_ref, dq_ref, dk_ref, dv_ref):
        n = pl.program_id(1)

        @pl.when(n == 0)
        def _():
            dk_ref[...] = jnp.zeros_like(dk_ref)
            dv_ref[...] = jnp.zeros_like(dv_ref)

        qv = q_ref[0].reshape(rows, HEAD_DIM).astype(BF16)
        kk = jnp.concatenate([kp_ref[0], kc_ref[0]], axis=0).astype(BF16)
        vv = jnp.concatenate([vp_ref[0], vc_ref[0]], axis=0).astype(BF16)
        ov = o_ref[0].reshape(rows, HEAD_DIM)
        dov = do_ref[0].reshape(rows, HEAD_DIM)
        lsev = lse_ref[0].reshape(rows, 1)
        dlsev = dlse_ref[0].reshape(rows, 1)
        s = lax.dot_general(qv, kk, (((1,), (1,)), ((), ())), preferred_element_type=F32)
        pr = jnp.where(_attn_mask(n), jnp.exp(s - lsev), 0.0)
        do16 = dov.astype(BF16)
        dv = lax.dot_general(pr.astype(BF16), do16, (((0,), (0,)), ((), ())), preferred_element_type=F32)
        dp = lax.dot_general(do16, vv, (((1,), (1,)), ((), ())), preferred_element_type=F32)
        delta = jnp.sum(dov * ov, axis=-1, keepdims=True)
        ds = (pr * (dp - delta + dlsev)).astype(BF16)
        dq = jnp.dot(ds, kk, preferred_element_type=F32)
        dk = lax.dot_general(ds, qv, (((0,), (0,)), ((), ())), preferred_element_type=F32)
        dq_ref[0] = dq.reshape(GQA, ATTN_BLOCK, HEAD_DIM)
        cur = pl.ds(pl.multiple_of(n * ATTN_BLOCK, ATTN_BLOCK), ATTN_BLOCK)
        dk_ref[0, cur, :] += dk[ATTN_BLOCK:]
        dv_ref[0, cur, :] += dv[ATTN_BLOCK:]

        @pl.when(n > 0)
        def _():
            prev = pl.ds(pl.multiple_of((n - 1) * ATTN_BLOCK, ATTN_BLOCK), ATTN_BLOCK)
            dk_ref[0, prev, :] += dk[:ATTN_BLOCK]
            dv_ref[0, prev, :] += dv[:ATTN_BLOCK]

    return pl.pallas_call(
        body, name=name, grid=(p_cnt, l // ATTN_BLOCK),
        in_specs=[q_spec, kprev, kcur, kprev, kcur, q_spec, l_spec, q_spec, l_spec],
        out_specs=[q_spec, kfull, kfull],
        out_shape=[jax.ShapeDtypeStruct(q.shape, F32), jax.ShapeDtypeStruct(k.shape, F32),
                   jax.ShapeDtypeStruct(v.shape, F32)],
        compiler_params=_params(("parallel", "arbitrary")),
    )(q, k, k, v, v, o, lse, do, dlse)


CONV_TC = 256
CONV_COL0 = XBC_COL // CONV_TC


def _shift_down(u, s):
    if s == 0:
        return u
    rows = lax.broadcasted_iota(jnp.int32, u.shape, 0)
    return jnp.where(rows >= s, pltpu.roll(u, s, 0), 0.0)


def _shift_up(u, s):
    if s == 0:
        return u
    n = u.shape[0]
    rows = lax.broadcasted_iota(jnp.int32, u.shape, 0)
    return jnp.where(rows < n - s, pltpu.roll(u, n - s, 0), 0.0)


def conv_silu_fwd(proj3, w, bias, *, name):
    b, s, _ = proj3.shape
    u_spec = pl.BlockSpec((1, s, CONV_TC), lambda j, bi: (bi, 0, CONV_COL0 + j))
    o_spec = pl.BlockSpec((1, s, CONV_TC), lambda j, bi: (bi, 0, j))
    w_spec = pl.BlockSpec((CONV_WIDTH, CONV_TC), lambda j, bi: (0, j))
    b_spec = pl.BlockSpec((1, CONV_TC), lambda j, bi: (0, j))

    def body(u_ref, w_ref, b_ref, o_ref):
        u = u_ref[0]
        y = jnp.broadcast_to(b_ref[...], u.shape)
        for k in range(CONV_WIDTH):
            y = y + w_ref[k:k + 1, :] * _shift_down(u, CONV_WIDTH - 1 - k)
        o_ref[0] = y * jax.nn.sigmoid(y)

    return pl.pallas_call(
        body, name=name, grid=(CONV_CH // CONV_TC, b), in_specs=[u_spec, w_spec, b_spec], out_specs=o_spec,
        out_shape=jax.ShapeDtypeStruct((b, s, CONV_CH), F32),
        compiler_params=_params(("parallel", "arbitrary")),
    )(proj3, w, bias)


def conv_silu_bwd(proj3, w, bias, dact, *, name):
    b, s, _ = proj3.shape
    u_spec = pl.BlockSpec((1, s, CONV_TC), lambda j, bi: (bi, 0, CONV_COL0 + j))
    o_spec = pl.BlockSpec((1, s, CONV_TC), lambda j, bi: (bi, 0, j))
    w_spec = pl.BlockSpec((CONV_WIDTH, CONV_TC), lambda j, bi: (0, j))
    b_spec = pl.BlockSpec((1, CONV_TC), lambda j, bi: (0, j))

    def body(u_ref, w_ref, b_ref, g_ref, du_ref, dw_ref, db_ref):
        bi = pl.program_id(1)

        @pl.when(bi == 0)
        def _():
            dw_ref[...] = jnp.zeros_like(dw_ref)
            db_ref[...] = jnp.zeros_like(db_ref)

        u = u_ref[0]
        y = jnp.broadcast_to(b_ref[...], u.shape)
        shifted = [_shift_down(u, CONV_WIDTH - 1 - k) for k in range(CONV_WIDTH)]
        for k in range(CONV_WIDTH):
            y = y + w_ref[k:k + 1, :] * shifted[k]
        sig = jax.nn.sigmoid(y)
        dy = g_ref[0] * (sig * (1.0 + y * (1.0 - sig)))
        du = jnp.zeros_like(u)
        for k in range(CONV_WIDTH):
            du = du + w_ref[k:k + 1, :] * _shift_up(dy, CONV_WIDTH - 1 - k)
            dw_ref[k:k + 1, :] += jnp.sum(dy * shifted[k], axis=0, keepdims=True)
        du_ref[0] = du
        db_ref[...] += jnp.sum(dy, axis=0, keepdims=True)

    return pl.pallas_call(
        body, name=name, grid=(CONV_CH // CONV_TC, b), in_specs=[u_spec, w_spec, b_spec, o_spec],
        out_specs=[o_spec, w_spec, b_spec],
        out_shape=[jax.ShapeDtypeStruct((b, s, CONV_CH), F32), jax.ShapeDtypeStruct((CONV_WIDTH, CONV_CH), F32),
                   jax.ShapeDtypeStruct((1, CONV_CH), F32)],
        compiler_params=_params(("parallel", "arbitrary")),
    )(proj3, w, bias, dact)


def _softplus(z):
    e = jnp.exp(-jnp.abs(z))
    u = 1.0 + e
    log1p = jnp.where(u == 1.0, e, jnp.log(u) * e / jnp.where(u == 1.0, 1.0, u - 1.0))
    return jnp.maximum(z, 0.0) + log1p


def _tri(lower):
    r = lax.broadcasted_iota(jnp.int32, (CHUNK, CHUNK), 0)
    c = lax.broadcasted_iota(jnp.int32, (CHUNK, CHUNK), 1)
    return (r >= c) if lower else (r <= c)


def _ssd_common(dtr_ref, dtb_ref, alog_ref):
    z = dtr_ref[0] + dtb_ref[...]
    dt = _softplus(z)
    aneg = -jnp.exp(alog_ref[...])
    acs = _dot01_left(_tri(True).astype(BF16), dt * aneg)
    return z, dt, aneg, acs


def _col(mat, onehot):
    return jnp.sum(mat * onehot, axis=1, keepdims=True)


def _ssd_head(x, dt_j, acs_j, cb, tri_mask, last_row, acs_row=None):
    acs_last = jnp.sum(acs_j * last_row, axis=0, keepdims=True)
    xg = x * dt_j
    bc = jnp.broadcast_to(acs_j, (CHUNK, CHUNK))
    dm = bc - (bc.T if acs_row is None else jnp.broadcast_to(acs_row, (CHUNK, CHUNK)))
    lm = jnp.where(tri_mask, jnp.exp(jnp.where(tri_mask, dm, 0.0)), 0.0)
    mm = cb * lm
    decay_s = jnp.exp(acs_last - acs_j)
    return acs_last, xg, lm, mm, decay_s


def _ssd_specs(nc, reverse):
    cidx = (lambda c: nc - 1 - c) if reverse else (lambda c: c)
    act_spec = pl.BlockSpec((1, CHUNK, CONV_CH), lambda b, c: (b, cidx(c), 0))
    y_spec = pl.BlockSpec((1, CHUNK, SSM_INNER), lambda b, c: (b, cidx(c), 0))
    dt_in_spec = pl.BlockSpec((1, CHUNK, LANE), lambda b, c: (b, cidx(c), DT_COL // LANE))
    dt_out_spec = pl.BlockSpec((1, CHUNK, LANE), lambda b, c: (b, cidx(c), 0))
    par_spec = pl.BlockSpec((1, LANE), lambda b, c: (0, 0))
    h_spec = pl.BlockSpec((1, SSM_HEADS, 1, SSM_P, D_STATE), lambda b, c: (b, 0, cidx(c), 0, 0))
    return act_spec, y_spec, dt_in_spec, dt_out_spec, par_spec, h_spec


def _head_cols(h):
    return slice(h * SSM_P, (h + 1) * SSM_P)


def _group_cols(g, which):
    start = SSM_INNER + which * SSM_GROUPS * D_STATE + g * D_STATE
    return slice(start, start + D_STATE)


def ssd_fwd(act3, proj3, dtb, alog, dsk, *, name):
    b, s, _ = act3.shape
    nc = s // CHUNK
    act_spec, y_spec, dt_in_spec, _, par_spec, h_spec = _ssd_specs(nc, False)

    def body(act_ref, dtr_ref, dtb_ref, alog_ref, dsk_ref, y_ref, hp_ref, state):
        c = pl.program_id(1)

        @pl.when(c == 0)
        def _():
            state[...] = jnp.zeros_like(state)

        _, dt, _, acs = _ssd_common(dtr_ref, dtb_ref, alog_ref)
        acs_t = acs.T
        tri_mask = _tri(True)
        last_row = (lax.broadcasted_iota(jnp.int32, (CHUNK, 1), 0) == CHUNK - 1).astype(F32)
        for g in range(SSM_GROUPS):
            b16 = act_ref[0, :, _group_cols(g, 0)].astype(BF16)
            c16 = act_ref[0, :, _group_cols(g, 1)].astype(BF16)
            cb = lax.dot_general(c16, b16, (((1,), (1,)), ((), ())), preferred_element_type=F32)
            for j in range(HEADS_PER_GROUP):
                hidx = g * HEADS_PER_GROUP + j
                x = act_ref[0, :, _head_cols(hidx)]
                dt_j, acs_j = dt[:, hidx:hidx + 1], acs[:, hidx:hidx + 1]
                acs_last, xg, _, mm, decay_s = _ssd_head(x, dt_j, acs_j, cb, tri_mask, last_row, acs_t[hidx:hidx + 1, :])
                y_diag = jnp.dot(mm.astype(BF16), xg.astype(BF16), preferred_element_type=F32)
                st = lax.dot_general((xg * decay_s).astype(BF16), b16, (((0,), (0,)), ((), ())), preferred_element_type=F32)
                hp = state[hidx]
                hp_ref[0, hidx, 0] = hp
                y_off = lax.dot_general(c16, hp.astype(BF16), (((1,), (1,)), ((), ())), preferred_element_type=F32)
                d_j = dsk_ref[:, hidx:hidx + 1]
                y_ref[0, :, _head_cols(hidx)] = y_diag + y_off * jnp.exp(acs_j) + d_j * x
                state[hidx] = hp * jnp.exp(acs_last) + st

    return pl.pallas_call(
        body, name=name, grid=(b, nc),
        in_specs=[act_spec, dt_in_spec, par_spec, par_spec, par_spec],
        out_specs=[y_spec, h_spec],
        out_shape=[jax.ShapeDtypeStruct((b, s, SSM_INNER), F32),
                   jax.ShapeDtypeStruct((b, SSM_HEADS, nc, SSM_P, D_STATE), F32)],
        scratch_shapes=[pltpu.VMEM((SSM_HEADS, SSM_P, D_STATE), F32)],
        compiler_params=_params(("arbitrary", "arbitrary")),
    )(act3, proj3, dtb, alog, dsk)


def ssd_bwd(act3, proj3, dtb, alog, dsk, hprev, dy3, *, name):
    b, s, _ = act3.shape
    nc = s // CHUNK
    act_spec, y_spec, dt_in_spec, dt_out_spec, par_spec, h_spec = _ssd_specs(nc, True)
    dpar_spec = pl.BlockSpec((8, LANE), lambda bi, c: (0, 0))

    def body(act_ref, dtr_ref, dtb_ref, alog_ref, dsk_ref, hp_ref, dy_ref, dact_ref, ddtr_ref, dpar_ref, dstate):
        bi, c = pl.program_id(0), pl.program_id(1)

        @pl.when(c == 0)
        def _():
            dstate[...] = jnp.zeros_like(dstate)

        @pl.when((bi == 0) & (c == 0))
        def _():
            dpar_ref[...] = jnp.zeros_like(dpar_ref)

        z, dt, aneg, acs = _ssd_common(dtr_ref, dtb_ref, alog_ref)
        acs_t = acs.T
        tri_mask = _tri(True)
        last_row = (lax.broadcasted_iota(jnp.int32, (CHUNK, 1), 0) == CHUNK - 1).astype(F32)
        lanes = lax.broadcasted_iota(jnp.int32, (1, LANE), 1)
        sublanes = lax.broadcasted_iota(jnp.int32, (LANE, 1), 0)
        ddt_mat = jnp.zeros((CHUNK, LANE), F32)
        dacs_mat = jnp.zeros((CHUNK, LANE), F32)
        dacs_rows = jnp.zeros((LANE, CHUNK), F32)
        ddsk_row = jnp.zeros((1, LANE), F32)
        for g in range(SSM_GROUPS):
            b16 = act_ref[0, :, _group_cols(g, 0)].astype(BF16)
            c16 = act_ref[0, :, _group_cols(g, 1)].astype(BF16)
            cb = lax.dot_general(c16, b16, (((1,), (1,)), ((), ())), preferred_element_type=F32)
            dcb = jnp.zeros((CHUNK, CHUNK), F32)
            db_acc = jnp.zeros((CHUNK, D_STATE), F32)
            dc_acc = jnp.zeros((CHUNK, D_STATE), F32)
            for j in range(HEADS_PER_GROUP):
                hidx = g * HEADS_PER_GROUP + j
                onehot = (lanes == hidx).astype(F32)
                x = act_ref[0, :, _head_cols(hidx)]
                dt_j, acs_j = dt[:, hidx:hidx + 1], acs[:, hidx:hidx + 1]
                acs_last, xg, lm, mm, decay_s = _ssd_head(x, dt_j, acs_j, cb, tri_mask, last_row, acs_t[hidx:hidx + 1, :])
                ea = jnp.exp(acs_j)
                cd = jnp.exp(acs_last)
                d_j = dsk_ref[:, hidx:hidx + 1]
                hp = hp_ref[0, hidx, 0]
                hp16 = hp.astype(BF16)
                g_y = dy_ref[0, :, _head_cols(hidx)]
                g_y16 = g_y.astype(BF16)
                g_hn = dstate[hidx]
                g_hn16 = g_hn.astype(BF16)
                xg16 = xg.astype(BF16)
                ddsk_row = ddsk_row + jnp.sum(jnp.sum(g_y * x, axis=1, keepdims=True), axis=0, keepdims=True) * onehot
                d_mm = lax.dot_general(g_y16, xg16, (((1,), (1,)), ((), ())), preferred_element_type=F32)
                d_xg = lax.dot_general(mm.astype(BF16), g_y16, (((0,), (0,)), ((), ())), preferred_element_type=F32)
                dcb = dcb + d_mm * lm
                d_dm = d_mm * mm
                d_acs = jnp.sum(d_dm, axis=1, keepdims=True)
                dacs_rows = dacs_rows + (sublanes == hidx).astype(F32) * jnp.sum(d_dm, axis=0, keepdims=True)
                t_off = lax.dot_general(c16, hp16, (((1,), (1,)), ((), ())), preferred_element_type=F32)
                d_t16 = (g_y * ea).astype(BF16)
                d_acs = d_acs + jnp.sum(g_y * t_off, axis=1, keepdims=True) * ea
                dc_acc = dc_acc + jnp.dot(d_t16, hp16, preferred_element_type=F32)
                d_hp = lax.dot_general(d_t16, c16, (((0,), (0,)), ((), ())), preferred_element_type=F32) + g_hn * cd
                d_last = jnp.sum(jnp.sum(g_hn * hp, axis=1, keepdims=True), axis=0, keepdims=True) * cd
                d_w = lax.dot_general(b16, g_hn16, (((1,), (1,)), ((), ())), preferred_element_type=F32)
                db_acc = db_acc + jnp.dot((xg * decay_s).astype(BF16), g_hn16, preferred_element_type=F32)
                d_xg = d_xg + d_w * decay_s
                d_ds = jnp.sum(d_w * xg, axis=1, keepdims=True) * decay_s
                d_last = d_last + jnp.sum(d_ds, axis=0, keepdims=True)
                d_acs = d_acs - d_ds + d_last * last_row
                dact_ref[0, :, _head_cols(hidx)] = d_j * g_y + d_xg * dt_j
                ddt_mat = ddt_mat + jnp.sum(d_xg * x, axis=1, keepdims=True) * onehot
                dacs_mat = dacs_mat + d_acs * onehot
                dstate[hidx] = d_hp
            dcb16 = dcb.astype(BF16)
            dact_ref[0, :, _group_cols(g, 1)] = dc_acc + jnp.dot(dcb16, b16, preferred_element_type=F32)
            dact_ref[0, :, _group_cols(g, 0)] = db_acc + lax.dot_general(dcb16, c16, (((0,), (0,)), ((), ())),
                                                                         preferred_element_type=F32)
        d_a = _dot01_left(_tri(False).astype(BF16), dacs_mat - dacs_rows.T)
        ddt_mat = ddt_mat + d_a * aneg
        d_raw = ddt_mat * jax.nn.sigmoid(z)
        ddtr_ref[0] = d_raw
        dpar_ref[0:1, :] += jnp.sum(d_raw, axis=0, keepdims=True)
        dpar_ref[1:2, :] += jnp.sum(d_a * dt, axis=0, keepdims=True) * aneg
        dpar_ref[2:3, :] += ddsk_row

    return pl.pallas_call(
        body, name=name, grid=(b, nc),
        in_specs=[act_spec, dt_in_spec, par_spec, par_spec, par_spec, h_spec, y_spec],
        out_specs=[act_spec, dt_out_spec, dpar_spec],
        out_shape=[jax.ShapeDtypeStruct(act3.shape, F32), jax.ShapeDtypeStruct((b, s, LANE), F32),
                   jax.ShapeDtypeStruct((8, LANE), F32)],
        scratch_shapes=[pltpu.VMEM((SSM_HEADS, SSM_P, D_STATE), F32)],
        compiler_params=_params(("arbitrary", "arbitrary")),
    )(act3, proj3, dtb, alog, dsk, hprev, dy3)


def _unused_ssd_specs(nc, reverse):
    cidx = (lambda c: nc - 1 - c) if reverse else (lambda c: c)
    x_spec = pl.BlockSpec((1, HEADS_PER_GROUP, CHUNK, SSM_P), lambda b, c, g: (b, g, cidx(c), 0))
    bc_spec = pl.BlockSpec((1, 1, CHUNK, D_STATE), lambda b, c, g: (b, g, cidx(c), 0))
    dt_spec = pl.BlockSpec((1, CHUNK, LANE), lambda b, c, g: (b, cidx(c), 0))
    par_spec = pl.BlockSpec((1, LANE), lambda b, c, g: (0, 0))
    h_spec = pl.BlockSpec((1, HEADS_PER_GROUP, 1, SSM_P, D_STATE), lambda b, c, g: (b, g, cidx(c), 0, 0))
    return x_spec, bc_spec, dt_spec, par_spec, h_spec


def _unused_ssd_fwd(xs, bm, cm, dtr, dtb, alog, dsk, *, name):
    b, _, s, _ = xs.shape
    nc = s // CHUNK
    x_spec, bc_spec, dt_spec, par_spec, h_spec = _ssd_specs(nc, False)

    def body(x_ref, b_ref, c_ref, dtr_ref, dtb_ref, alog_ref, dsk_ref, y_ref, hp_ref, state):
        c, g = pl.program_id(1), pl.program_id(2)

        @pl.when(c == 0)
        def _():
            state[pl.ds(g * HEADS_PER_GROUP, HEADS_PER_GROUP)] = jnp.zeros((HEADS_PER_GROUP, SSM_P, D_STATE), F32)

        _, dt, _, acs = _ssd_common(dtr_ref, dtb_ref, alog_ref)
        b16, c16 = b_ref[0, 0].astype(BF16), c_ref[0, 0].astype(BF16)
        cb = lax.dot_general(c16, b16, (((1,), (1,)), ((), ())), preferred_element_type=F32)
        tri_mask = _tri(True)
        last_row = (lax.broadcasted_iota(jnp.int32, (CHUNK, 1), 0) == CHUNK - 1).astype(F32)
        lanes = lax.broadcasted_iota(jnp.int32, (1, LANE), 1)
        for j in range(HEADS_PER_GROUP):
            hidx = g * HEADS_PER_GROUP + j
            onehot = (lanes == hidx).astype(F32)
            x = x_ref[0, j]
            dt_j, acs_j = _col(dt, onehot), _col(acs, onehot)
            acs_last, xg, _, mm, decay_s = _ssd_head(x, dt_j, acs_j, cb, tri_mask, last_row)
            xg16 = xg.astype(BF16)
            y_diag = jnp.dot(mm.astype(BF16), xg16, preferred_element_type=F32)
            st = lax.dot_general((xg * decay_s).astype(BF16), b16, (((0,), (0,)), ((), ())), preferred_element_type=F32)
            hp = state[hidx]
            hp_ref[0, j, 0] = hp
            y_off = lax.dot_general(c16, hp.astype(BF16), (((1,), (1,)), ((), ())), preferred_element_type=F32)
            d_j = jnp.sum(dsk_ref[...] * onehot, axis=1, keepdims=True)
            y_ref[0, j] = y_diag + y_off * jnp.exp(acs_j) + d_j * x
            state[hidx] = hp * jnp.exp(acs_last) + st

    return pl.pallas_call(
        body, name=name, grid=(b, nc, SSM_GROUPS),
        in_specs=[x_spec, bc_spec, bc_spec, dt_spec, par_spec, par_spec, par_spec],
        out_specs=[x_spec, h_spec],
        out_shape=[jax.ShapeDtypeStruct(xs.shape, F32),
                   jax.ShapeDtypeStruct((b, SSM_HEADS, nc, SSM_P, D_STATE), F32)],
        scratch_shapes=[pltpu.VMEM((SSM_HEADS, SSM_P, D_STATE), F32)],
        compiler_params=_params(("arbitrary", "arbitrary", "arbitrary")),
    )(xs, bm, cm, dtr, dtb, alog, dsk)


def _unused_ssd_bwd(xs, bm, cm, dtr, dtb, alog, dsk, hprev, dy, *, name):
    b, _, s, _ = xs.shape
    nc = s // CHUNK
    x_spec, bc_spec, dt_spec, par_spec, h_spec = _ssd_specs(nc, True)
    dpar_spec = pl.BlockSpec((8, LANE), lambda bi, c, g: (0, 0))

    def body(x_ref, b_ref, c_ref, dtr_ref, dtb_ref, alog_ref, dsk_ref, hp_ref, dy_ref,
             dx_ref, db_ref, dc_ref, ddtr_ref, dpar_ref, dstate):
        bi, c, g = pl.program_id(0), pl.program_id(1), pl.program_id(2)

        @pl.when(c == 0)
        def _():
            dstate[pl.ds(g * HEADS_PER_GROUP, HEADS_PER_GROUP)] = jnp.zeros((HEADS_PER_GROUP, SSM_P, D_STATE), F32)

        @pl.when((bi == 0) & (c == 0) & (g == 0))
        def _():
            dpar_ref[...] = jnp.zeros_like(dpar_ref)

        z, dt, aneg, acs = _ssd_common(dtr_ref, dtb_ref, alog_ref)
        bv, cv = b_ref[0, 0], c_ref[0, 0]
        b16, c16 = bv.astype(BF16), cv.astype(BF16)
        cb = lax.dot_general(c16, b16, (((1,), (1,)), ((), ())), preferred_element_type=F32)
        tri_mask = _tri(True)
        last_row = (lax.broadcasted_iota(jnp.int32, (CHUNK, 1), 0) == CHUNK - 1).astype(F32)
        lanes = lax.broadcasted_iota(jnp.int32, (1, LANE), 1)
        dcb = jnp.zeros((CHUNK, CHUNK), F32)
        db_acc = jnp.zeros((CHUNK, D_STATE), F32)
        dc_acc = jnp.zeros((CHUNK, D_STATE), F32)
        ddt_mat = jnp.zeros((CHUNK, LANE), F32)
        dacs_mat = jnp.zeros((CHUNK, LANE), F32)
        ddsk_row = jnp.zeros((1, LANE), F32)
        for j in range(HEADS_PER_GROUP):
            hidx = g * HEADS_PER_GROUP + j
            onehot = (lanes == hidx).astype(F32)
            x = x_ref[0, j]
            dt_j, acs_j = _col(dt, onehot), _col(acs, onehot)
            acs_last, xg, lm, mm, decay_s = _ssd_head(x, dt_j, acs_j, cb, tri_mask, last_row)
            ea = jnp.exp(acs_j)
            cd = jnp.exp(acs_last)
            d_j = jnp.sum(dsk_ref[...] * onehot, axis=1, keepdims=True)
            hp = hp_ref[0, j, 0]
            hp16 = hp.astype(BF16)
            g_y = dy_ref[0, j]
            g_y16 = g_y.astype(BF16)
            g_hn = dstate[hidx]
            g_hn16 = g_hn.astype(BF16)
            xg16 = xg.astype(BF16)
            ddsk_row = ddsk_row + jnp.sum(jnp.sum(g_y * x, axis=1, keepdims=True), axis=0, keepdims=True) * onehot
            d_mm = lax.dot_general(g_y16, xg16, (((1,), (1,)), ((), ())), preferred_element_type=F32)
            d_xg = lax.dot_general(mm.astype(BF16), g_y16, (((0,), (0,)), ((), ())), preferred_element_type=F32)
            dcb = dcb + d_mm * lm
            d_dm = d_mm * mm
            d_acs = jnp.sum(d_dm, axis=1, keepdims=True) - jnp.sum(d_dm.T, axis=1, keepdims=True)
            t_off = lax.dot_general(c16, hp16, (((1,), (1,)), ((), ())), preferred_element_type=F32)
            d_t16 = (g_y * ea).astype(BF16)
            d_acs = d_acs + jnp.sum(g_y * t_off, axis=1, keepdims=True) * ea
            dc_acc = dc_acc + jnp.dot(d_t16, hp16, preferred_element_type=F32)
            d_hp = lax.dot_general(d_t16, c16, (((0,), (0,)), ((), ())), preferred_element_type=F32) + g_hn * cd
            d_last = jnp.sum(jnp.sum(g_hn * hp, axis=1, keepdims=True), axis=0, keepdims=True) * cd
            d_w = lax.dot_general(b16, g_hn16, (((1,), (1,)), ((), ())), preferred_element_type=F32)
            db_acc = db_acc + jnp.dot((xg * decay_s).astype(BF16), g_hn16, preferred_element_type=F32)
            d_xg = d_xg + d_w * decay_s
            d_ds = jnp.sum(d_w * xg, axis=1, keepdims=True) * decay_s
            d_last = d_last + jnp.sum(d_ds, axis=0, keepdims=True)
            d_acs = d_acs - d_ds + d_last * last_row
            dx_ref[0, j] = d_j * g_y + d_xg * dt_j
            ddt_mat = ddt_mat + jnp.sum(d_xg * x, axis=1, keepdims=True) * onehot
            dacs_mat = dacs_mat + d_acs * onehot
            dstate[hidx] = d_hp
        dcb16 = dcb.astype(BF16)
        dc_ref[0, 0] = dc_acc + jnp.dot(dcb16, b16, preferred_element_type=F32)
        db_ref[0, 0] = db_acc + lax.dot_general(dcb16, c16, (((0,), (0,)), ((), ())), preferred_element_type=F32)
        d_a = _dot01_left(_tri(False).astype(BF16), dacs_mat)
        ddt_mat = ddt_mat + d_a * aneg
        d_aneg = jnp.sum(d_a * dt, axis=0, keepdims=True)
        d_raw = ddt_mat * jax.nn.sigmoid(z)

        @pl.when(g == 0)
        def _():
            ddtr_ref[0] = d_raw

        @pl.when(g != 0)
        def _():
            ddtr_ref[0] += d_raw

        dpar_ref[0:1, :] += jnp.sum(d_raw, axis=0, keepdims=True)
        dpar_ref[1:2, :] += d_aneg * aneg
        dpar_ref[2:3, :] += ddsk_row

    return pl.pallas_call(
        body, name=name, grid=(b, nc, SSM_GROUPS),
        in_specs=[x_spec, bc_spec, bc_spec, dt_spec, par_spec, par_spec, par_spec, h_spec, x_spec],
        out_specs=[x_spec, bc_spec, bc_spec, dt_spec, dpar_spec],
        out_shape=[jax.ShapeDtypeStruct(xs.shape, F32), jax.ShapeDtypeStruct(bm.shape, F32),
                   jax.ShapeDtypeStruct(cm.shape, F32), jax.ShapeDtypeStruct(dtr.shape, F32),
                   jax.ShapeDtypeStruct((8, LANE), F32)],
        scratch_shapes=[pltpu.VMEM((SSM_HEADS, SSM_P, D_STATE), F32)],
        compiler_params=_params(("arbitrary", "arbitrary", "arbitrary")),
    )(xs, bm, cm, dtr, dtb, alog, dsk, hprev, dy)


def to_heads(x, b, s, h):
    return x.reshape(b, s, h, -1).transpose(0, 2, 1, 3)


def from_heads(x):
    b, h, s, c = x.shape
    return x.transpose(0, 2, 1, 3).reshape(b * s, h * c)


def dilate_q(q, d):
    b, _, s, c = q.shape
    x = q.reshape(b, N_KV_HEADS, GQA, s // d, d, c).transpose(0, 1, 4, 2, 3, 5)
    return x.reshape(b * N_KV_HEADS * d, GQA, s // d, c)


def undilate_q(x, b, d):
    _, _, l, c = x.shape
    y = x.reshape(b, N_KV_HEADS, d, GQA, l, c).transpose(0, 1, 3, 4, 2, 5)
    return y.reshape(b, N_Q_HEADS, l * d, c)


def dilate_kv(k, d):
    b, h, s, c = k.shape
    return k.reshape(b, h, s // d, d, c).transpose(0, 1, 3, 2, 4).reshape(b * h * d, s // d, c)


def undilate_kv(x, b, d):
    _, l, c = x.shape
    return x.reshape(b, N_KV_HEADS, d, l, c).transpose(0, 1, 3, 2, 4).reshape(b, N_KV_HEADS, l * d, c)


def rotary_tables(positions):
    inv_freq = ROPE_THETA ** (-jnp.arange(0, ROPE_DIM, 2, dtype=F32) / ROPE_DIM)
    ang = positions.astype(F32)[..., None] * inv_freq
    cos, sin = jnp.cos(ang), jnp.sin(ang)
    rest = HEAD_DIM - ROPE_DIM
    cosf = jnp.concatenate([cos, cos, jnp.ones(cos.shape[:2] + (rest,), F32)], axis=-1)
    sinf = jnp.concatenate([-sin, sin, jnp.zeros(sin.shape[:2] + (rest,), F32)], axis=-1)
    return cosf, sinf


def w_in_columns(w):
    pad = jnp.zeros((w.shape[0], IN_PAD - IN_PROJ), w.dtype)
    return jnp.concatenate([w[:, :Q_END], w[:, V_END:XBC_END], w[:, Q_END:V_END], w[:, XBC_END:], pad], axis=1)


def w_in_grad_columns(g):
    return jnp.concatenate([g[:, :Z_COL], g[:, K_COL:DT_COL], g[:, Z_COL:K_COL], g[:, DT_COL:DT_COL + SSM_HEADS]], axis=1)


def lane_pad(v):
    return jnp.pad(v.reshape(1, -1), ((0, 0), (0, LANE - v.shape[-1])))


def layer_fwd(h, wts, small, cosf, sinf, b, s, tag):
    w_in, w_out, w_gate, w_up, w_down = wts
    t = b * s
    sv = {"h": h}
    hn = rowwise_fwd(rms_fn, [h], [small["norm_mix"]], [BF16], name=f"rms_mix_{tag}")[0]
    proj = matmul(hn, w_in, name=f"in_proj_{tag}")
    sv["hn"], sv["proj"] = hn, proj
    qh = rotary([to_heads(proj[:, :Z_COL], b, s, N_Q_HEADS)], cosf, sinf, HEAD_DIM ** -0.5, adjoint=False, name=f"rope_q_{tag}")
    kh = rotary([to_heads(proj[:, K_COL:V_COL], b, s, N_KV_HEADS)], cosf, sinf, 1.0, adjoint=False, name=f"rope_k_{tag}")
    vh = to_heads(proj[:, V_COL:DT_COL], b, s, N_KV_HEADS)
    sv["qh"], sv["kh"], sv["vh"] = qh, kh, vh
    outs, lses = [], []
    for d in DILATIONS:
        o, lse = attn_branch_fwd(dilate_q(qh, d), dilate_kv(kh, d), dilate_kv(vh, d), name=f"attn_d{d}_{tag}")
        outs.append(undilate_q(o, b, d).reshape(b * N_Q_HEADS * s, HEAD_DIM))
        lses.append(undilate_q(lse, b, d).reshape(b * N_Q_HEADS * s, 1))
    sv["attn_o"], sv["attn_lse"] = outs, lses
    attn = rowwise_fwd(combine_fn, outs + lses, [], [F32], name=f"attn_combine_{tag}", tr=2048)[0]
    attn = from_heads(attn.reshape(b, N_Q_HEADS, s, HEAD_DIM))
    proj3 = proj.reshape(b, s, IN_PAD)
    act3 = conv_silu_fwd(proj3, small["conv_w"], small["conv_b"], name=f"conv_{tag}")
    y3, hprev = ssd_fwd(act3, proj3, small["dt_bias"], small["a_log"], small["d_skip"], name=f"ssd_{tag}")
    y = y3.reshape(t, SSM_INNER)
    sv["act3"], sv["hprev"], sv["y"] = act3, hprev, y
    gn = rowwise_fwd(gated_norm_fn, [y, proj], [small["ssm_norm"]], [F32], name=f"gated_norm_{tag}", groups=SSM_GROUPS,
                     windows=[None, (Z_COL, SSM_INNER)])[0]
    cat = jnp.concatenate([attn, gn], axis=1).astype(BF16)
    sv["cat"] = cat
    h1 = matmul(cat, w_out, name=f"out_proj_{tag}", residual=h)
    sv["h1"] = h1
    hn2 = rowwise_fwd(rms_fn, [h1], [small["norm_ffn"]], [BF16], name=f"rms_ffn_{tag}")[0]
    gate = matmul(hn2, w_gate, name=f"ffn_gate_{tag}")
    up = matmul(hn2, w_up, name=f"ffn_up_{tag}")
    act2 = rowwise_fwd(swiglu_fn, [gate, up], [], [BF16], name=f"swiglu_{tag}")[0]
    sv["hn2"], sv["gate"], sv["up"], sv["act2"] = hn2, gate, up, act2
    h2 = matmul(act2, w_down, name=f"ffn_down_{tag}", residual=h1)
    return h2, sv


def layer_bwd(dh2, sv, wts, small, cosf, sinf, b, s, tag):
    w_in, w_out, w_gate, w_up, w_down = wts
    t = b * s
    gr = {}
    dh2_16 = dh2.astype(BF16)
    d_act2 = matmul(dh2_16, w_down, tb=True, name=f"ffn_down_dx_{tag}")
    gr["w_down"] = matmul(sv["act2"], dh2_16, ta=True, out_dtype=BF16, name=f"ffn_down_dw_{tag}")
    d_gate, d_up = rowwise_bwd(swiglu_fn, [sv["gate"], sv["up"]], [], [d_act2], [BF16, BF16], name=f"swiglu_bwd_{tag}")
    gr["w_gate"] = matmul(sv["hn2"], d_gate, ta=True, out_dtype=BF16, name=f"ffn_gate_dw_{tag}")
    gr["w_up"] = matmul(sv["hn2"], d_up, ta=True, out_dtype=BF16, name=f"ffn_up_dw_{tag}")
    d_hn2 = matmul(d_gate, w_gate, tb=True, name=f"ffn_gate_dx_{tag}")
    d_hn2 = matmul(d_up, w_up, tb=True, residual=d_hn2, name=f"ffn_up_dx_{tag}")
    dh1, gr["norm_ffn"] = rowwise_bwd(rms_fn, [sv["h1"]], [small["norm_ffn"]], [d_hn2], [F32],
                                      name=f"rms_ffn_bwd_{tag}", add_to_first=dh2)
    dh1_16 = dh1.astype(BF16)
    d_cat = matmul(dh1_16, w_out, tb=True, name=f"out_proj_dx_{tag}")
    gr["w_out"] = matmul(sv["cat"], dh1_16, ta=True, out_dtype=BF16, name=f"out_proj_dw_{tag}")
    d_attn, d_gn = d_cat[:, :ATTN_WIDTH], d_cat[:, ATTN_WIDTH:]
    d_y, d_z, gr["ssm_norm"] = rowwise_bwd(gated_norm_fn, [sv["y"], sv["proj"]], [small["ssm_norm"]], [d_gn], [F32, F32],
                                           name=f"gated_norm_bwd_{tag}", groups=SSM_GROUPS,
                                           windows=[None, (Z_COL, SSM_INNER)])
    proj3 = sv["proj"].reshape(b, s, IN_PAD)
    d_act3, d_dtr, d_par = ssd_bwd(sv["act3"], proj3, small["dt_bias"], small["a_log"], small["d_skip"], sv["hprev"],
                                   d_y.reshape(b, s, SSM_INNER), name=f"ssd_bwd_{tag}")
    gr["dt_bias"], gr["a_log"], gr["d_skip"] = d_par[0, :SSM_HEADS], d_par[1, :SSM_HEADS], d_par[2, :SSM_HEADS]
    d_xbc, gr["conv_w"], gr["conv_b"] = conv_silu_bwd(proj3, small["conv_w"], small["conv_b"], d_act3,
                                                      name=f"conv_bwd_{tag}")
    d_attn_h = to_heads(d_attn, b, s, N_Q_HEADS).reshape(b * N_Q_HEADS * s, HEAD_DIM)
    comb = rowwise_bwd(combine_fn, sv["attn_o"] + sv["attn_lse"], [], [d_attn_h], [F32] * 6,
                       name=f"attn_combine_bwd_{tag}", tr=2048)
    dqs, dks, dvs = [], [], []
    for i, d in enumerate(DILATIONS):
        as_q = lambda a, c: dilate_q(a.reshape(b, N_Q_HEADS, s, c), d)
        dq, dk, dv = attn_branch_bwd(dilate_q(sv["qh"], d), dilate_kv(sv["kh"], d), dilate_kv(sv["vh"], d),
                                     as_q(sv["attn_o"][i], HEAD_DIM), as_q(sv["attn_lse"][i], 1),
                                     as_q(comb[i], HEAD_DIM), as_q(comb[3 + i], 1), name=f"attn_d{d}_bwd_{tag}")
        dqs.append(undilate_q(dq, b, d))
        dks.append(undilate_kv(dk, b, d))
        dvs.append(undilate_kv(dv, b, d))
    d_q = rotary(dqs, cosf, sinf, HEAD_DIM ** -0.5, adjoint=True, name=f"rope_q_bwd_{tag}")
    d_k = rotary(dks, cosf, sinf, 1.0, adjoint=True, name=f"rope_k_bwd_{tag}")
    d_v = add3(*[from_heads(a) for a in dvs], name=f"dv_sum_{tag}")
    d_proj = jnp.concatenate([from_heads(d_q), d_z, d_xbc.reshape(t, CONV_CH), from_heads(d_k), d_v,
                              d_dtr.reshape(t, LANE)], axis=1).astype(BF16)
    d_hn = matmul(d_proj, w_in, tb=True, name=f"in_proj_dx_{tag}")
    gr["w_in"] = w_in_grad_columns(matmul(sv["hn"], d_proj, ta=True, out_dtype=BF16, name=f"in_proj_dw_{tag}"))
    dh, gr["norm_mix"] = rowwise_bwd(rms_fn, [sv["h"]], [small["norm_mix"]], [d_hn], [F32],
                                     name=f"rms_mix_bwd_{tag}", add_to_first=dh1)
    return dh, gr


def local_step(x, positions, big, small_all, final_norm, loss_target):
    b, s, _ = x.shape
    t = b * s
    cosf, sinf = rotary_tables(positions)
    h = x.reshape(t, D_MODEL)
    saved = []
    for l in range(DEPTH):
        h, sv = layer_fwd(h, big[l], small_all[l], cosf, sinf, b, s, f"l{l}")
        saved.append(sv)
    dh, d_final, loss = loss_and_grad(h, loss_target.reshape(t, D_MODEL), final_norm.reshape(1, D_MODEL))
    grads = [None] * DEPTH
    for l in reversed(range(DEPTH)):
        dh, grads[l] = layer_bwd(dh, saved[l], big[l], small_all[l], cosf, sinf, b, s, f"l{l}")
    return loss, dh.reshape(b, s, D_MODEL), grads, d_final


def _slab_rows(r):
    return r if r <= 512 else _pick(r, (512, 256))


def cast_bf16(x, *, name):
    def fn(v):
        return (v,)
    return rowwise_fwd(fn, [x], [], [BF16], name=name, tr=_slab_rows(x.shape[0]))[0]


def sum_slots(x, *, name):
    n, r, c = x.shape
    tr = _slab_rows(r)

    def body(x_ref, o_ref):
        acc = x_ref[0].astype(F32)
        for i in range(1, n):
            acc = acc + x_ref[i].astype(F32)
        o_ref[...] = acc

    return pl.pallas_call(
        body, name=name, grid=(r // tr,), in_specs=[pl.BlockSpec((n, tr, c), lambda i: (0, i, 0))],
        out_specs=pl.BlockSpec((tr, c), lambda i: (i, 0)), out_shape=jax.ShapeDtypeStruct((r, c), F32),
        compiler_params=_params(("parallel",)),
    )(x)


def adamw(g_parts, w, m, v, *, name):
    r, c = w.shape
    tr = _slab_rows(r)
    n_g = len(g_parts)
    bc1 = 1.0 / (1.0 - ADAM_B1 ** ADAM_STEP)
    bc2 = 1.0 / (1.0 - ADAM_B2 ** ADAM_STEP)

    def body(*refs):
        g = refs[0][...]
        for r_ in refs[1:n_g]:
            g = g + r_[...]
        w_ref, m_ref, v_ref, g_out, d_out, m_out, v_out = refs[n_g:]
        m_new = ADAM_B1 * m_ref[...] + (1.0 - ADAM_B1) * g
        v_new = ADAM_B2 * v_ref[...] + (1.0 - ADAM_B2) * (g * g)
        g_out[...] = g
        m_out[...] = m_new
        v_out[...] = v_new
        d_out[...] = -ADAM_LR * ((m_new * bc1) / (jnp.sqrt(v_new * bc2) + ADAM_EPS) + ADAM_WD * w_ref[...])

    spec = pl.BlockSpec((tr, c), lambda i: (i, 0))
    return pl.pallas_call(
        body, name=name, grid=(r // tr,), in_specs=[spec] * (n_g + 3), out_specs=[spec] * 4,
        out_shape=[jax.ShapeDtypeStruct((r, c), F32)] * 4, compiler_params=_params(("parallel",)),
    )(*g_parts, w, m, v)


def _other_chips(x, y):
    return [(1 - x, y), (x, 1 - y), (1 - x, 1 - y)]


def allgather_chips(shards):
    n_arr = len(shards)

    def body(*refs):
        in_refs, out_refs = refs[:n_arr], refs[n_arr:2 * n_arr]
        send_sems, recv_sems, local_sems = refs[2 * n_arr:]
        x, y, c = lax.axis_index("x"), lax.axis_index("y"), lax.axis_index("c")
        chip = 2 * x + y
        started = []
        for a, (in_ref, out_ref) in enumerate(zip(in_refs, out_refs)):
            mine = pltpu.make_async_copy(in_ref, out_ref.at[chip], local_sems.at[a])
            mine.start()
            started.append(mine.wait)
            for k, (px, py) in enumerate(_other_chips(x, y)):
                cp = pltpu.make_async_remote_copy(src_ref=in_ref, dst_ref=out_ref.at[chip], send_sem=send_sems.at[3 * a + k],
                                                  recv_sem=recv_sems.at[3 * a + k], device_id=(px, py, c), device_id_type=MESH)
                cp.start()
                started.append(cp.wait_send)
        for a, (in_ref, out_ref) in enumerate(zip(in_refs, out_refs)):
            for k, (px, py) in enumerate(_other_chips(x, y)):
                pltpu.make_async_remote_copy(src_ref=in_ref, dst_ref=out_ref.at[2 * px + py], send_sem=send_sems.at[3 * a + k],
                                             recv_sem=recv_sems.at[3 * a + k], device_id=(px, py, c),
                                             device_id_type=MESH).wait_recv()
        for wait in started:
            wait()

    hbm = pl.BlockSpec(memory_space=pltpu.HBM)
    return pl.pallas_call(
        body, name="allgather_weights", in_specs=[hbm] * n_arr, out_specs=[hbm] * n_arr,
        out_shape=[jax.ShapeDtypeStruct((N_CHIPS,) + s.shape, s.dtype) for s in shards],
        scratch_shapes=[pltpu.SemaphoreType.DMA((3 * n_arr,)), pltpu.SemaphoreType.DMA((3 * n_arr,)),
                        pltpu.SemaphoreType.DMA((n_arr,))],
    )(*shards)


def exchange_grads(big, small):
    def body(big_ref, small_ref, big_out, small_out, send_sems, recv_sems, local_sems):
        x, y, c = lax.axis_index("x"), lax.axis_index("y"), lax.axis_index("c")
        chip = 2 * x + y
        dev = 4 * x + 2 * y + c
        own_big = pltpu.make_async_copy(big_ref.at[chip], big_out.at[chip], local_sems.at[0])
        own_small = pltpu.make_async_copy(small_ref, small_out.at[dev], local_sems.at[1])
        own_big.start()
        own_small.start()
        sends = []
        for k, (px, py) in enumerate(_other_chips(x, y)):
            cp = pltpu.make_async_remote_copy(src_ref=big_ref.at[2 * px + py], dst_ref=big_out.at[chip],
                                              send_sem=send_sems.at[k], recv_sem=recv_sems.at[k],
                                              device_id=(px, py, c), device_id_type=MESH)
            cp.start()
            sends.append(cp)
        peers = []
        for r in range(1, N_DEV):
            fx, fy, fc = (r >> 2) & 1, (r >> 1) & 1, r & 1
            px, py, pc = (x + fx) % 2, (y + fy) % 2, (c + fc) % 2
            peers.append((px, py, pc))
            cp = pltpu.make_async_remote_copy(src_ref=small_ref, dst_ref=small_out.at[dev], send_sem=send_sems.at[2 + r],
                                              recv_sem=recv_sems.at[2 + r], device_id=(px, py, pc), device_id_type=MESH)
            cp.start()
            sends.append(cp)
        for k, (px, py) in enumerate(_other_chips(x, y)):
            pltpu.make_async_remote_copy(src_ref=big_ref.at[chip], dst_ref=big_out.at[2 * px + py],
                                         send_sem=send_sems.at[k], recv_sem=recv_sems.at[k],
                                         device_id=(px, py, c), device_id_type=MESH).wait_recv()
        for r, (px, py, pc) in zip(range(1, N_DEV), peers):
            pltpu.make_async_remote_copy(src_ref=small_ref, dst_ref=small_out.at[4 * px + 2 * py + pc],
                                         send_sem=send_sems.at[2 + r], recv_sem=recv_sems.at[2 + r],
                                         device_id=(px, py, pc), device_id_type=MESH).wait_recv()
        for cp in sends:
            cp.wait_send()
        own_big.wait()
        own_small.wait()

    hbm = pl.BlockSpec(memory_space=pltpu.HBM)
    n_sem = 3 + N_DEV - 1
    return pl.pallas_call(
        body, name="exchange_grads", in_specs=[hbm, hbm], out_specs=[hbm, hbm],
        out_shape=[jax.ShapeDtypeStruct(big.shape, big.dtype), jax.ShapeDtypeStruct((N_DEV,) + small.shape, small.dtype)],
        scratch_shapes=[pltpu.SemaphoreType.DMA((n_sem,)), pltpu.SemaphoreType.DMA((n_sem,)), pltpu.SemaphoreType.DMA((2,))],
    )(big, small)


SWAP_CHUNKS = 27


def swap_cores(mine):
    rows = mine.shape[0] // SWAP_CHUNKS
    assert rows * SWAP_CHUNKS == mine.shape[0] and rows % 8 == 0

    def body(in_ref, out_ref, send_sems, recv_sems):
        x, y, c = lax.axis_index("x"), lax.axis_index("y"), lax.axis_index("c")

        def chunk(k):
            part = pl.ds(k * rows, rows)
            return pltpu.make_async_remote_copy(src_ref=in_ref.at[part], dst_ref=out_ref.at[part],
                                                send_sem=send_sems.at[k], recv_sem=recv_sems.at[k],
                                                device_id=(x, y, 1 - c), device_id_type=MESH)

        for k in range(SWAP_CHUNKS):
            chunk(k).start()
        for k in range(SWAP_CHUNKS):
            chunk(k).wait_recv()
        for k in range(SWAP_CHUNKS):
            chunk(k).wait_send()

    hbm = pl.BlockSpec(memory_space=pltpu.HBM)
    return pl.pallas_call(
        body, name="swap_cores", in_specs=[hbm], out_specs=hbm,
        out_shape=jax.ShapeDtypeStruct(mine.shape, mine.dtype),
        scratch_shapes=[pltpu.SemaphoreType.DMA((SWAP_CHUNKS,)), pltpu.SemaphoreType.DMA((SWAP_CHUNKS,))],
    )(mine)


BIG_NAMES = ("w_in", "w_out", "w_gate", "w_up", "w_down")
BIG_SHARD_AXIS = {"w_in": 1, "w_out": 0, "w_gate": 1, "w_up": 1, "w_down": 0}
PACK_COLS = 1024
SMALL_NAMES = ("norm_mix", "conv_w", "conv_b", "dt_bias", "a_log", "d_skip", "ssm_norm", "norm_ffn")


PACK_ROW_TILE = 256


def pack_big(shards):
    flat = jnp.concatenate([shards[n].reshape(-1) for n in BIG_NAMES])
    unit = PACK_ROW_TILE * PACK_COLS
    total = -(-flat.size // unit) * unit
    return jnp.pad(flat, (0, total - flat.size)).reshape(-1, PACK_COLS)


def unpack_big(packed, like):
    out, off = {}, 0
    flat = packed.reshape(-1)
    for n in BIG_NAMES:
        size = like[n].size
        out[n] = flat[off:off + size].reshape(like[n].shape)
        off += size
    return out


def pack_small(parts):
    flat = jnp.concatenate([p.reshape(-1).astype(F32) for p in parts])
    rows = -(-flat.size // LANE)
    rows = -(-rows // 8) * 8
    return jnp.pad(flat, (0, rows * LANE - flat.size)).reshape(rows, LANE)


def unpack_small(packed, like):
    out, off = [], 0
    flat = packed.reshape(-1)
    for a in like:
        out.append(flat[off:off + a.size].reshape(a.shape))
        off += a.size
    return out


def kernel(x, positions, norm_mix, w_in, conv_w, conv_b, dt_bias, a_log, d_skip, ssm_norm, w_out, norm_ffn, w_gate, w_up, w_down, final_norm, loss_target, m_norm_mix, m_w_in, m_conv_w, m_conv_b, m_dt_bias, m_a_log, m_d_skip, m_ssm_norm, m_w_out, m_norm_ffn, m_w_gate, m_w_up, m_w_down, m_final_norm, v_norm_mix, v_w_in, v_conv_w, v_conv_b, v_dt_bias, v_a_log, v_d_skip, v_ssm_norm, v_w_out, v_norm_ffn, v_w_gate, v_w_up, v_w_down, v_final_norm):
    chip = 2 * lax.axis_index("x") + lax.axis_index("y")
    w_sh = {"w_in": w_in, "w_out": w_out, "w_gate": w_gate, "w_up": w_up, "w_down": w_down}
    m_sh = {"w_in": m_w_in, "w_out": m_w_out, "w_gate": m_w_gate, "w_up": m_w_up, "w_down": m_w_down}
    v_sh = {"w_in": v_w_in, "w_out": v_w_out, "w_gate": v_w_gate, "w_up": v_w_up, "w_down": v_w_down}

    w_packed = pack_big(w_sh)
    conv_cols = CONV_CH // N_CHIPS
    gathered, conv_g = allgather_chips([cast_bf16(w_packed, name="cast_weights"), conv_w.reshape(-1, LANE)])
    pieces = [unpack_big(gathered[j], w_sh) for j in range(N_CHIPS)]
    full = {n: jnp.concatenate([p[n] for p in pieces], axis=BIG_SHARD_AXIS[n] + 1) for n in BIG_NAMES}
    big = []
    for l in range(DEPTH):
        big.append((w_in_columns(full["w_in"][l]), full["w_out"][l], full["w_gate"][l], full["w_up"][l], full["w_down"][l]))
    conv_w_full = jnp.concatenate([conv_g[j].reshape(DEPTH, CONV_WIDTH, conv_cols) for j in range(N_CHIPS)], axis=2)

    small_all = []
    for l in range(DEPTH):
        small_all.append({
            "norm_mix": norm_mix[l].reshape(1, -1), "conv_w": conv_w_full[l], "conv_b": conv_b[l].reshape(1, -1),
            "dt_bias": lane_pad(dt_bias[l]), "a_log": lane_pad(a_log[l]), "d_skip": lane_pad(d_skip[l]),
            "ssm_norm": ssm_norm[l].reshape(1, -1), "norm_ffn": norm_ffn[l].reshape(1, -1)})

    loss_part, grad_x, grads, d_final = local_step(x, positions, big, small_all, final_norm, loss_target)

    def shard_of(name, g, j):
        n = g.shape[BIG_SHARD_AXIS[name]] // N_CHIPS
        return lax.slice_in_dim(g, j * n, (j + 1) * n, axis=BIG_SHARD_AXIS[name])

    to_chip = []
    for j in range(N_CHIPS):
        to_chip.append(pack_big({n: jnp.stack([shard_of(n, grads[l][n], j) for l in range(DEPTH)]) for n in BIG_NAMES}))
    small_parts = [jnp.stack([grads[l][n].reshape(-1) for l in range(DEPTH)]) for n in SMALL_NAMES]
    small_parts += [d_final.reshape(-1), loss_part.reshape(-1)]
    recv_big, recv_small = exchange_grads(jnp.stack(to_chip), pack_small(small_parts))
    plane_sum = sum_slots(recv_big, name="sum_chip_partials")
    other_plane = swap_cores(plane_sum)

    g_big, d_big, m_big, v_big = adamw([plane_sum, other_plane], w_packed, pack_big(m_sh), pack_big(v_sh), name="adamw_big")
    g_big, d_big, m_big, v_big = (unpack_big(a, w_sh) for a in (g_big, d_big, m_big, v_big))

    small_sum = sum_slots(recv_small, name="sum_small")
    like = [norm_mix, conv_w_full, conv_b, dt_bias, a_log, d_skip, ssm_norm, norm_ffn, final_norm, loss_part.reshape(-1)]
    g_small = unpack_small(small_sum, like)
    loss = g_small[-1][0]
    g_small = dict(zip(SMALL_NAMES + ("final_norm",), g_small[:-1]))
    g_small["conv_w"] = lax.dynamic_slice_in_dim(g_small["conv_w"], chip * conv_cols, conv_cols, axis=2)
    w_small = {"norm_mix": norm_mix, "conv_w": conv_w, "conv_b": conv_b, "dt_bias": dt_bias, "a_log": a_log, "d_skip": d_skip,
               "ssm_norm": ssm_norm, "norm_ffn": norm_ffn, "final_norm": final_norm}
    m_small = {"norm_mix": m_norm_mix, "conv_w": m_conv_w, "conv_b": m_conv_b, "dt_bias": m_dt_bias, "a_log": m_a_log,
               "d_skip": m_d_skip, "ssm_norm": m_ssm_norm, "norm_ffn": m_norm_ffn, "final_norm": m_final_norm}
    v_small = {"norm_mix": v_norm_mix, "conv_w": v_conv_w, "conv_b": v_conv_b, "dt_bias": v_dt_bias, "a_log": v_a_log,
               "d_skip": v_d_skip, "ssm_norm": v_ssm_norm, "norm_ffn": v_norm_ffn, "final_norm": v_final_norm}
    names = SMALL_NAMES + ("final_norm",)
    order = [w_small[n] for n in names]
    res = adamw([pack_small([g_small[n] for n in names])], pack_small(order), pack_small([m_small[n] for n in names]),
                pack_small([v_small[n] for n in names]), name="adamw_small")
    g_s, d_s, m_s, v_s = (dict(zip(names, unpack_small(a, order))) for a in res)

    all_names = ("norm_mix", "w_in", "conv_w", "conv_b", "dt_bias", "a_log", "d_skip", "ssm_norm", "w_out", "norm_ffn",
                 "w_gate", "w_up", "w_down", "final_norm")
    outs = [loss, grad_x]
    for src_big, src_small in ((g_big, g_s), (d_big, d_s), (m_big, m_s), (v_big, v_s)):
        outs += [src_big[n] if n in BIG_NAMES else src_small[n] for n in all_names]
    return tuple(outs)
```

```python
import functools

import jax
import jax.numpy as jnp
from jax import lax
from jax.experimental import pallas as pl
from jax.experimental.pallas import tpu as pltpu

F32 = jnp.float32
BF16 = jnp.bfloat16
MESH = pl.DeviceIdType.MESH

D_MODEL = 1024
DEPTH = 2
HEAD_DIM = 64
N_Q_HEADS = 8
N_KV_HEADS = 2
GQA = N_Q_HEADS // N_KV_HEADS
ATTN_WIDTH = N_Q_HEADS * HEAD_DIM
ROPE_DIM = HEAD_DIM // 4
ROPE_HALF = ROPE_DIM // 2
ROPE_THETA = 500000.0
DILATIONS = (1, 4, 16)
ATTN_BLOCK = 128
SSM_P = 64
SSM_HEADS = 16
SSM_INNER = SSM_HEADS * SSM_P
SSM_GROUPS = 2
HEADS_PER_GROUP = SSM_HEADS // SSM_GROUPS
D_STATE = 128
CONV_WIDTH = 4
CHUNK = 128
CONV_CH = SSM_INNER + 2 * SSM_GROUPS * D_STATE
MIX_WIDTH = ATTN_WIDTH + SSM_INNER
Q_END = ATTN_WIDTH
K_END = Q_END + N_KV_HEADS * HEAD_DIM
V_END = K_END + N_KV_HEADS * HEAD_DIM
Z_END = V_END + SSM_INNER
XBC_END = Z_END + CONV_CH
IN_PROJ = XBC_END + SSM_HEADS
LANE = 128
IN_PAD = XBC_END + LANE
Q_COL, Z_COL, XBC_COL, K_COL, V_COL, DT_COL = 0, 512, 1536, 3072, 3200, 3328
FFN_HIDDEN = 2816
EPS = 1e-5
ADAM_LR, ADAM_B1, ADAM_B2, ADAM_EPS, ADAM_WD, ADAM_STEP = 0.001, 0.9, 0.999, 1e-8, 0.01, 10
N_CHIPS = 4
N_DEV = 8
VMEM_LIMIT = 48 * 1024 * 1024
NEG_BIG = -1e30


def _params(sem=None):
    return pltpu.CompilerParams(dimension_semantics=sem, vmem_limit_bytes=VMEM_LIMIT)


def _pick(n, prefs):
    for p in prefs:
        if n % p == 0:
            return p
    return n


def matmul(a, b, *, name, ta=False, tb=False, out_dtype=F32, residual=None):
    if ta:
        assert not tb and residual is None
        return _matmul_over_rows(a, b, name=name, out_dtype=out_dtype)
    return _matmul_full_k(a, b, name=name, tb=tb, out_dtype=out_dtype, residual=residual)


def _matmul_full_k(a, b, *, name, tb, out_dtype, residual):
    m, kdim = a.shape
    n = b.shape[0] if tb else b.shape[1]
    tm = _pick(m, (1024, 512, 256)) if kdim <= 1536 else _pick(m, (512, 256))
    tn = _pick(n, (1152, 1408, 1536, 1024, 768, 512, 384, 256, 128))
    b_spec = pl.BlockSpec((tn, kdim), lambda i, j: (j, 0)) if tb else pl.BlockSpec((kdim, tn), lambda i, j: (0, j))
    o_spec = pl.BlockSpec((tm, tn), lambda i, j: (i, j))
    dims = (((1,), (1 if tb else 0,)), ((), ()))
    has_res = residual is not None

    def body(*refs):
        a_ref, b_ref = refs[:2]
        o_ref = refs[-1]
        r = lax.dot_general(a_ref[...].astype(BF16), b_ref[...].astype(BF16), dims, preferred_element_type=F32)
        if has_res:
            r = r + refs[2][...]
        o_ref[...] = r.astype(out_dtype)

    in_specs = [pl.BlockSpec((tm, kdim), lambda i, j: (i, 0)), b_spec] + ([o_spec] if has_res else [])
    args = (a, b) + ((residual,) if has_res else ())
    return pl.pallas_call(
        body, name=name, grid=(m // tm, n // tn), in_specs=in_specs, out_specs=o_spec,
        out_shape=jax.ShapeDtypeStruct((m, n), out_dtype),
        compiler_params=_params(("parallel", "parallel")),
    )(*args)


def _matmul_over_rows(a, b, *, name, out_dtype):
    t, m = a.shape
    n = b.shape[1]
    tm = _pick(m, (1024, 1408, 768, 512, 256, 128))
    tn = _pick(n, (1152, 1408, 1024, 768, 512, 256, 128))
    tk = _pick(t, (1024, 512, 256, 128))
    nk = t // tk

    def body(a_ref, b_ref, o_ref, acc):
        k = pl.program_id(2)
        part = lax.dot_general(a_ref[...].astype(BF16), b_ref[...].astype(BF16), (((0,), (0,)), ((), ())),
                               preferred_element_type=F32)

        @pl.when(k == 0)
        def _():
            acc[...] = part

        @pl.when(k > 0)
        def _():
            acc[...] += part

        @pl.when(k == nk - 1)
        def _():
            o_ref[...] = acc[...].astype(out_dtype)

    return pl.pallas_call(
        body, name=name, grid=(m // tm, n // tn, nk),
        in_specs=[pl.BlockSpec((tk, tm), lambda i, j, k: (k, i)), pl.BlockSpec((tk, tn), lambda i, j, k: (k, j))],
        out_specs=pl.BlockSpec((tm, tn), lambda i, j, k: (i, j)),
        out_shape=jax.ShapeDtypeStruct((m, n), out_dtype),
        scratch_shapes=[pltpu.VMEM((tm, tn), F32)],
        compiler_params=_params(("parallel", "parallel", "arbitrary")),
    )(a, b)


ROW_BLOCK_BYTES = 8 * 1024 * 1024


def _row_tile(t, tr, widths, n_copies):
    lanes = sum(-(-wd // LANE) * LANE for wd in widths) * n_copies
    tr = min(tr, t)
    while tr > 8 and tr * lanes * 4 > ROW_BLOCK_BYTES:
        tr //= 2
    return tr


def _row_widths(rows, groups, windows):
    windows = windows or [None] * len(rows)
    widths = [(w[1] if w else a.shape[1]) // groups for a, w in zip(rows, windows)]
    assert all(w is None or w[0] % wd == 0 for w, wd in zip(windows, widths))
    return widths, [(w[0] // wd if w else 0) for w, wd in zip(windows, widths)]


def _row_specs(tr, widths, offs):
    return [pl.BlockSpec((tr, wd), functools.partial(lambda g, i, off: (i, g + off), off=off)) for wd, off in zip(widths, offs)]


def rowwise_fwd(fn, rows, params, out_dtypes, *, name, tr=512, groups=1, windows=None):
    t = rows[0].shape[0]
    widths, offs = _row_widths(rows, groups, windows)
    tr = _row_tile(t, tr, widths, 2)
    row_specs = _row_specs(tr, widths, offs)
    par_spec = lambda p: pl.BlockSpec((1, p.shape[1] // groups), lambda g, i: (0, g))
    n_in = len(rows) + len(params)
    out_cols = [o.shape[1] for o in jax.eval_shape(
        fn, *[jax.ShapeDtypeStruct((tr, wd), F32) for wd in widths],
        *[jax.ShapeDtypeStruct((1, p.shape[1] // groups), F32) for p in params])]

    def body(*refs):
        vals = [r[...].astype(F32) for r in refs[:n_in]]
        outs = fn(*vals)
        for o_ref, o in zip(refs[n_in:], outs):
            o_ref[...] = o.astype(o_ref.dtype)

    return pl.pallas_call(
        body, name=name, grid=(groups, t // tr),
        in_specs=row_specs + [par_spec(p) for p in params],
        out_specs=[pl.BlockSpec((tr, c), lambda g, i: (i, g)) for c in out_cols],
        out_shape=[jax.ShapeDtypeStruct((t, c * groups), d) for c, d in zip(out_cols, out_dtypes)],
        compiler_params=_params(("arbitrary", "arbitrary")),
    )(*rows, *params)


def rowwise_bwd(fn, rows, params, cts, drow_dtypes, *, name, tr=512, groups=1, add_to_first=None, windows=None):
    t = rows[0].shape[0]
    widths, offs = _row_widths(rows, groups, windows)
    tr = _row_tile(t, tr, widths + [a.shape[1] // groups for a in cts], 2)
    row_spec = lambda a: pl.BlockSpec((tr, a.shape[1] // groups), lambda g, i: (i, g))
    row_specs = _row_specs(tr, widths, offs)
    par_spec = lambda p: pl.BlockSpec((1, p.shape[1] // groups), lambda g, i: (0, g))
    n_rows, n_par, n_ct = len(rows), len(params), len(cts)
    has_add = add_to_first is not None
    n_in = n_rows + n_par + n_ct + (1 if has_add else 0)

    def body(*refs):
        i = pl.program_id(1)
        vals = [r[...].astype(F32) for r in refs[:n_rows + n_par]]
        ct_vals = tuple(r[...].astype(F32) for r in refs[n_rows + n_par:n_rows + n_par + n_ct])
        _, vjp = jax.vjp(fn, *vals)
        grads = vjp(ct_vals)
        out_refs = refs[n_in:]
        for idx in range(n_rows):
            g = grads[idx]
            if idx == 0 and has_add:
                g = g + refs[n_in - 1][...]
            out_refs[idx][...] = g.astype(out_refs[idx].dtype)
        for idx in range(n_par):
            p_ref = out_refs[n_rows + idx]

            @pl.when(i == 0)
            def _():
                p_ref[...] = jnp.zeros_like(p_ref)

            p_ref[...] += grads[n_rows + idx]

    ins = list(rows) + list(params) + list(cts) + ([add_to_first] if has_add else [])
    in_specs = (row_specs + [par_spec(p) for p in params] + [row_spec(a) for a in cts]
                + ([row_spec(add_to_first)] if has_add else []))
    return pl.pallas_call(
        body, name=name, grid=(groups, t // tr), in_specs=in_specs,
        out_specs=[pl.BlockSpec((tr, wd), lambda g, i: (i, g)) for wd in widths] + [par_spec(p) for p in params],
        out_shape=[jax.ShapeDtypeStruct((t, wd * groups), d) for wd, d in zip(widths, drow_dtypes)]
        + [jax.ShapeDtypeStruct(p.shape, F32) for p in params],
        compiler_params=_params(("arbitrary", "arbitrary")),
    )(*ins)


def rms_fn(x, w):
    return (x * lax.rsqrt(jnp.mean(x * x, axis=-1, keepdims=True) + EPS) * w,)


def swiglu_fn(g, u):
    return (g * jax.nn.sigmoid(g) * u,)


def gated_norm_fn(y, z, w):
    v = y * (z * jax.nn.sigmoid(z))
    return (v * lax.rsqrt(jnp.mean(v * v, axis=-1, keepdims=True) + EPS) * w,)


def combine_fn(o1, o2, o3, l1, l2, l3):
    m = jnp.maximum(jnp.maximum(l1, l2), l3)
    e1, e2, e3 = jnp.exp(l1 - m), jnp.exp(l2 - m), jnp.exp(l3 - m)
    inv = 1.0 / (e1 + e2 + e3)
    return ((e1 * inv) * o1 + (e2 * inv) * o2 + (e3 * inv) * o3,)


def loss_and_grad(h, target, w, *, tr=512):
    t, d = h.shape

    def loss_fn(hv, wv, tv):
        err = rms_fn(hv, wv)[0] - tv
        per_row = jnp.mean(err * err, axis=-1, keepdims=True)
        return 0.5 * jnp.sum(per_row, axis=0, keepdims=True)

    def body(h_ref, t_ref, w_ref, dh_ref, dw_ref, loss_ref):
        i = pl.program_id(0)

        @pl.when(i == 0)
        def _():
            dw_ref[...] = jnp.zeros_like(dw_ref)
            loss_ref[...] = jnp.zeros_like(loss_ref)

        tv = t_ref[...]
        val, vjp = jax.vjp(lambda hv, wv: loss_fn(hv, wv, tv), h_ref[...], w_ref[...])
        dh, dw = vjp(jnp.ones((1, 1), F32))
        dh_ref[...] = dh
        dw_ref[...] += dw
        loss_ref[...] += jnp.broadcast_to(val, loss_ref.shape)

    row = pl.BlockSpec((tr, d), lambda i: (i, 0))
    par = pl.BlockSpec((1, d), lambda i: (0, 0))
    return pl.pallas_call(
        body, name="loss_and_grad", grid=(t // tr,), in_specs=[row, row, par],
        out_specs=[row, par, pl.BlockSpec((1, LANE), lambda i: (0, 0))],
        out_shape=[jax.ShapeDtypeStruct((t, d), F32), jax.ShapeDtypeStruct((1, d), F32),
                   jax.ShapeDtypeStruct((1, LANE), F32)],
        compiler_params=_params(("arbitrary",)),
    )(h, target, w)


def _split3(x):
    hi = x.astype(BF16)
    r1 = x - hi.astype(F32)
    mid = r1.astype(BF16)
    lo = (r1 - mid.astype(F32)).astype(BF16)
    return hi, mid, lo


def _dot01_left(m01, x):
    return sum(jnp.dot(m01, p, preferred_element_type=F32) for p in _split3(x))


def _dot01_right(x, m01):
    return sum(jnp.dot(p, m01, preferred_element_type=F32) for p in _split3(x))


def rotary(xs_list, cosf, sinf, scale, *, adjoint, name, ts=512):
    b, h, s, c = xs_list[0].shape
    n_x = len(xs_list)

    def body(*refs):
        x = refs[0][0, 0]
        for r in refs[1:n_x]:
            x = x + r[0, 0]
        cos_v, sin_v = refs[n_x][0], refs[n_x + 1][0]
        o_ref = refs[n_x + 2]
        ci = lax.broadcasted_iota(jnp.int32, (c, c), 0)
        cj = lax.broadcasted_iota(jnp.int32, (c, c), 1)
        swap = ((cj == ci + ROPE_HALF) & (ci < ROPE_HALF)) | ((cj == ci - ROPE_HALF) & (ci >= ROPE_HALF) & (ci < ROPE_DIM))
        swap = swap.astype(BF16)
        if adjoint:
            out = x * cos_v + _dot01_right(x * sin_v, swap)
        else:
            out = x * cos_v + _dot01_right(x, swap) * sin_v
        o_ref[0, 0] = out * scale

    x_spec = pl.BlockSpec((1, 1, ts, c), lambda bi, hi, si: (bi, hi, si, 0))
    t_spec = pl.BlockSpec((1, ts, c), lambda bi, hi, si: (bi, si, 0))
    return pl.pallas_call(
        body, name=name, grid=(b, h, s // ts), in_specs=[x_spec] * n_x + [t_spec, t_spec], out_specs=x_spec,
        out_shape=jax.ShapeDtypeStruct((b, h, s, c), F32),
        compiler_params=_params(("parallel", "parallel", "parallel")),
    )(*xs_list, cosf, sinf)


def add3(a, b, c, *, name, tr=1024):
    def fn(x, y, z):
        return (x + y + z,)
    return rowwise_fwd(fn, [a, b, c], [], [F32], name=name, tr=tr)[0]


def _attn_mask(n):
    rows = GQA * ATTN_BLOCK
    qi = lax.broadcasted_iota(jnp.int32, (rows, 2 * ATTN_BLOCK), 0) % ATTN_BLOCK
    ki = lax.broadcasted_iota(jnp.int32, (rows, 2 * ATTN_BLOCK), 1)
    delta = qi + ATTN_BLOCK - ki
    return (delta >= 0) & (delta <= ATTN_BLOCK) & ((n - 1) * ATTN_BLOCK + ki >= 0)


def _attn_specs(l):
    q_spec = pl.BlockSpec((1, GQA, ATTN_BLOCK, HEAD_DIM), lambda p, n: (p, 0, n, 0))
    l_spec = pl.BlockSpec((1, GQA, ATTN_BLOCK, 1), lambda p, n: (p, 0, n, 0))
    kprev = pl.BlockSpec((1, ATTN_BLOCK, HEAD_DIM), lambda p, n: (p, jnp.maximum(n - 1, 0), 0))
    kcur = pl.BlockSpec((1, ATTN_BLOCK, HEAD_DIM), lambda p, n: (p, n, 0))
    kfull = pl.BlockSpec((1, l, HEAD_DIM), lambda p, n: (p, 0, 0))
    return q_spec, l_spec, kprev, kcur, kfull


def attn_branch_fwd(q, k, v, *, name):
    p_cnt, _, l, _ = q.shape
    rows = GQA * ATTN_BLOCK
    q_spec, l_spec, kprev, kcur, _ = _attn_specs(l)

    def body(q_ref, kp_ref, kc_ref, vp_ref, vc_ref, o_ref, lse_ref):
        n = pl.program_id(1)
        qv = q_ref[0].reshape(rows, HEAD_DIM).astype(BF16)
        kk = jnp.concatenate([kp_ref[0], kc_ref[0]], axis=0).astype(BF16)
        vv = jnp.concatenate([vp_ref[0], vc_ref[0]], axis=0).astype(BF16)
        s = lax.dot_general(qv, kk, (((1,), (1,)), ((), ())), preferred_element_type=F32)
        s = jnp.where(_attn_mask(n), s, NEG_BIG)
        m = jnp.max(s, axis=-1, keepdims=True)
        pr = jnp.exp(s - m)
        den = jnp.sum(pr, axis=-1, keepdims=True)
        o = jnp.dot(pr.astype(BF16), vv, preferred_element_type=F32) / den
        o_ref[0] = o.reshape(GQA, ATTN_BLOCK, HEAD_DIM)
        lse_ref[0] = (m + jnp.log(den)).reshape(GQA, ATTN_BLOCK, 1)

    return pl.pallas_call(
        body, name=name, grid=(p_cnt, l // ATTN_BLOCK), in_specs=[q_spec, kprev, kcur, kprev, kcur],
        out_specs=[q_spec, l_spec],
        out_shape=[jax.ShapeDtypeStruct(q.shape, F32), jax.ShapeDtypeStruct(q.shape[:3] + (1,), F32)],
        compiler_params=_params(("parallel", "arbitrary")),
    )(q, k, k, v, v)


def attn_branch_bwd(q, k, v, o, lse, do, dlse, *, name):
    p_cnt, _, l, _ = q.shape
    rows = GQA * ATTN_BLOCK
    q_spec, l_spec, kprev, kcur, kfull = _attn_specs(l)

    def body(q_ref, kp_ref, kc_ref, vp_ref, vc_ref, o_ref, lse_ref, do_ref, dlse_ref, dq_ref, dk_ref, dv_ref):
        n = pl.program_id(1)

        @pl.when(n == 0)
        def _():
            dk_ref[...] = jnp.zeros_like(dk_ref)
            dv_ref[...] = jnp.zeros_like(dv_ref)

        qv = q_ref[0].reshape(rows, HEAD_DIM).astype(BF16)
        kk = jnp.concatenate([kp_ref[0], kc_ref[0]], axis=0).astype(BF16)
        vv = jnp.concatenate([vp_ref[0], vc_ref[0]], axis=0).astype(BF16)
        ov = o_ref[0].reshape(rows, HEAD_DIM)
        dov = do_ref[0].reshape(rows, HEAD_DIM)
        lsev = lse_ref[0].reshape(rows, 1)
        dlsev = dlse_ref[0].reshape(rows, 1)
        s = lax.dot_general(qv, kk, (((1,), (1,)), ((), ())), preferred_element_type=F32)
        pr = jnp.where(_attn_mask(n), jnp.exp(s - lsev), 0.0)
        do16 = dov.astype(BF16)
        dv = lax.dot_general(pr.astype(BF16), do16, (((0,), (0,)), ((), ())), preferred_element_type=F32)
        dp = lax.dot_general(do16, vv, (((1,), (1,)), ((), ())), preferred_element_type=F32)
        delta = jnp.sum(dov * ov, axis=-1, keepdims=True)
        ds = (pr * (dp - delta + dlsev)).astype(BF16)
        dq = jnp.dot(ds, kk, preferred_element_type=F32)
        dk = lax.dot_general(ds, qv, (((0,), (0,)), ((), ())), preferred_element_type=F32)
        dq_ref[0] = dq.reshape(GQA, ATTN_BLOCK, HEAD_DIM)
        cur = pl.ds(pl.multiple_of(n * ATTN_BLOCK, ATTN_BLOCK), ATTN_BLOCK)
        dk_ref[0, cur, :] += dk[ATTN_BLOCK:]
        dv_ref[0, cur, :] += dv[ATTN_BLOCK:]

        @pl.when(n > 0)
        def _():
            prev = pl.ds(pl.multiple_of((n - 1) * ATTN_BLOCK, ATTN_BLOCK), ATTN_BLOCK)
            dk_ref[0, prev, :] += dk[:ATTN_BLOCK]
            dv_ref[0, prev, :] += dv[:ATTN_BLOCK]

    return pl.pallas_call(
        body, name=name, grid=(p_cnt, l // ATTN_BLOCK),
        in_specs=[q_spec, kprev, kcur, kprev, kcur, q_spec, l_spec, q_spec, l_spec],
        out_specs=[q_spec, kfull, kfull],
        out_shape=[jax.ShapeDtypeStruct(q.shape, F32), jax.ShapeDtypeStruct(k.shape, F32),
                   jax.ShapeDtypeStruct(v.shape, F32)],
        compiler_params=_params(("parallel", "arbitrary")),
    )(q, k, k, v, v, o, lse, do, dlse)


ATTN_PAD = ATTN_BLOCK * DILATIONS[-1]
Q_GROUP_W = GQA * HEAD_DIM
ATTN_VMEM_LIMIT = 56 * 1024 * 1024


def _rope(x, cos_v, sin_v, swap, scale, adjoint):
    if adjoint:
        return (x * cos_v + _dot01_right(x * sin_v, swap)) * scale
    return (x * cos_v + _dot01_right(x, swap) * sin_v) * scale


def _swap_matrix():
    c = HEAD_DIM
    ci = lax.broadcasted_iota(jnp.int32, (c, c), 0)
    cj = lax.broadcasted_iota(jnp.int32, (c, c), 1)
    swap = ((cj == ci + ROPE_HALF) & (ci < ROPE_HALF)) | ((cj == ci - ROPE_HALF) & (ci >= ROPE_HALF) & (ci < ROPE_DIM))
    return swap.astype(BF16)


def _attn_prologue(q_ref, kv_ref, tab_ref, q_s, k_s, v_s, hk, s_len):
    swap = _swap_matrix()
    cos_v, sin_v = tab_ref[0, :, :HEAD_DIM], tab_ref[0, :, HEAD_DIM:]
    for g in range(GQA):
        cols = slice(g * HEAD_DIM, (g + 1) * HEAD_DIM)
        q_s[:, cols] = _rope(q_ref[0, :, cols], cos_v, sin_v, swap, HEAD_DIM ** -0.5, False)
    zeros = jnp.zeros((ATTN_PAD, HEAD_DIM), F32)
    k_s[0:ATTN_PAD, :] = zeros
    v_s[0:ATTN_PAD, :] = zeros
    for h in range(N_KV_HEADS):
        @pl.when(hk == h)
        def _():
            k_s[ATTN_PAD:ATTN_PAD + s_len, :] = _rope(kv_ref[0, :, h * HEAD_DIM:(h + 1) * HEAD_DIM], cos_v, sin_v, swap, 1.0, False)
            v_s[ATTN_PAD:ATTN_PAD + s_len, :] = kv_ref[0, :, LANE + h * HEAD_DIM:LANE + (h + 1) * HEAD_DIM]


def _attn_blocks(s_len):
    out = []
    for i, d in enumerate(DILATIONS):
        nb = s_len // (ATTN_BLOCK * d)
        for r in range(d):
            for n in range(nb):
                start = r + d * ATTN_BLOCK * n
                out.append((i, d, start, ATTN_PAD + start - d * ATTN_BLOCK, n))
    return out


def _rows(start, size, d):
    return pl.ds(start, size, stride=d) if d > 1 else pl.ds(start, size)


def _stack_heads(blk):
    return jnp.concatenate([blk[:, g * HEAD_DIM:(g + 1) * HEAD_DIM] for g in range(GQA)], axis=0)


def _stack_stats(blk):
    return jnp.concatenate([jnp.max(blk[:, g * HEAD_DIM:(g + 1) * HEAD_DIM], axis=1, keepdims=True) for g in range(GQA)], axis=0)


def _attn_in_specs(s_len):
    assert K_COL % (2 * LANE) == 0 and V_COL == K_COL + LANE
    q_spec = pl.BlockSpec((1, s_len, Q_GROUP_W), lambda b, h: (b, 0, Q_COL // Q_GROUP_W + h))
    kv_spec = pl.BlockSpec((1, s_len, 2 * LANE), lambda b, h: (b, 0, K_COL // (2 * LANE)))
    t_spec = pl.BlockSpec((1, s_len, 2 * HEAD_DIM), lambda b, h: (b, 0, 0))
    o_spec = pl.BlockSpec((1, s_len, Q_GROUP_W), lambda b, h: (b, 0, h))
    return q_spec, kv_spec, t_spec, o_spec


def attn_fwd(proj3, rope_tab, *, name):
    b, s_len, _ = proj3.shape
    q_spec, kv_spec, t_spec, o_spec = _attn_in_specs(s_len)
    n_br = len(DILATIONS)

    def body(q_ref, kv_ref, tab_ref, o_ref, lse_ref, q_s, k_s, v_s, *branch_s):
        o_s, l_s = branch_s[:n_br], branch_s[n_br:]
        _attn_prologue(q_ref, kv_ref, tab_ref, q_s, k_s, v_s, pl.program_id(1), s_len)
        for i, d, q0, k0, n in _attn_blocks(s_len):
            qv = _stack_heads(q_s[_rows(q0, ATTN_BLOCK, d), :]).astype(BF16)
            kk = k_s[_rows(k0, 2 * ATTN_BLOCK, d), :].astype(BF16)
            vv = v_s[_rows(k0, 2 * ATTN_BLOCK, d), :].astype(BF16)
            sc = lax.dot_general(qv, kk, (((1,), (1,)), ((), ())), preferred_element_type=F32)
            sc = jnp.where(_attn_mask(n), sc, NEG_BIG)
            m = jnp.max(sc, axis=-1, keepdims=True)
            pr = jnp.exp(sc - m)
            den = jnp.sum(pr, axis=-1, keepdims=True)
            o = jnp.dot(pr.astype(BF16), vv, preferred_element_type=F32) / den
            lse = m + jnp.log(den)
            for g in range(GQA):
                part = slice(g * ATTN_BLOCK, (g + 1) * ATTN_BLOCK)
                o_s[i][_rows(q0, ATTN_BLOCK, d), g * HEAD_DIM:(g + 1) * HEAD_DIM] = o[part]
                l_s[i][_rows(q0, ATTN_BLOCK, d), g * HEAD_DIM:(g + 1) * HEAD_DIM] = jnp.broadcast_to(lse[part], (ATTN_BLOCK, HEAD_DIM))
        step = 256
        for t0 in range(0, s_len, step):
            rs = pl.ds(t0, step)
            for g in range(GQA):
                ls = [l_s[i][rs, g * HEAD_DIM:(g + 1) * HEAD_DIM] for i in range(n_br)]
                m = functools.reduce(jnp.maximum, ls)
                es = [jnp.exp(l - m) for l in ls]
                tot = functools.reduce(lambda a, c: a + c, es)
                inv = 1.0 / tot
                acc = None
                for i in range(n_br):
                    term = (es[i] * inv) * o_s[i][rs, g * HEAD_DIM:(g + 1) * HEAD_DIM]
                    acc = term if acc is None else acc + term
                o_ref[0, rs, g * HEAD_DIM:(g + 1) * HEAD_DIM] = acc
                lse_ref[0, rs, g * HEAD_DIM:(g + 1) * HEAD_DIM] = m + jnp.log(tot)

    return pl.pallas_call(
        body, name=name, grid=(b, N_KV_HEADS), in_specs=[q_spec, kv_spec, t_spec],
        out_specs=[o_spec, o_spec],
        out_shape=[jax.ShapeDtypeStruct((b, s_len, ATTN_WIDTH), F32)] * 2,
        scratch_shapes=[pltpu.VMEM((s_len, Q_GROUP_W), F32), pltpu.VMEM((ATTN_PAD + s_len, HEAD_DIM), F32),
                        pltpu.VMEM((ATTN_PAD + s_len, HEAD_DIM), F32)] + [pltpu.VMEM((s_len, Q_GROUP_W), F32)] * (2 * n_br),
        compiler_params=pltpu.CompilerParams(dimension_semantics=("arbitrary", "arbitrary"), vmem_limit_bytes=ATTN_VMEM_LIMIT),
    )(proj3, proj3, rope_tab)


def attn_bwd(proj3, rope_tab, attn3, lse3, d_attn3, *, name):
    b, s_len, _ = proj3.shape
    q_spec, kv_spec, t_spec, o_spec = _attn_in_specs(s_len)
    kv_out = pl.BlockSpec((1, 1, s_len, HEAD_DIM), lambda bi, h: (bi, h, 0, 0))

    def body(q_ref, kv_ref, tab_ref, o_ref, lse_ref, do_ref, dq_ref, dk_ref, dv_ref,
             q_s, k_s, v_s, dl_s, dq_s, dk_s, dv_s):
        _attn_prologue(q_ref, kv_ref, tab_ref, q_s, k_s, v_s, pl.program_id(1), s_len)
        dq_s[...] = jnp.zeros_like(dq_s)
        dk_s[...] = jnp.zeros_like(dk_s)
        dv_s[...] = jnp.zeros_like(dv_s)
        for g in range(GQA):
            cols = slice(g * HEAD_DIM, (g + 1) * HEAD_DIM)
            delta = jnp.sum(do_ref[0, :, cols] * o_ref[0, :, cols], axis=1, keepdims=True)
            dl_s[:, cols] = jnp.broadcast_to(delta, (s_len, HEAD_DIM))
        for i, d, q0, k0, n in _attn_blocks(s_len):
            qrows, krows = _rows(q0, ATTN_BLOCK, d), _rows(k0, 2 * ATTN_BLOCK, d)
            qv = _stack_heads(q_s[qrows, :]).astype(BF16)
            kk = k_s[krows, :].astype(BF16)
            vv = v_s[krows, :].astype(BF16)
            do16 = _stack_heads(do_ref.at[0][qrows, :]).astype(BF16)
            lse = _stack_stats(lse_ref.at[0][qrows, :])
            delta = _stack_stats(dl_s[qrows, :])
            sc = lax.dot_general(qv, kk, (((1,), (1,)), ((), ())), preferred_element_type=F32)
            pr = jnp.where(_attn_mask(n), jnp.exp(sc - lse), 0.0)
            dv = lax.dot_general(pr.astype(BF16), do16, (((0,), (0,)), ((), ())), preferred_element_type=F32)
            dp = lax.dot_general(do16, vv, (((1,), (1,)), ((), ())), preferred_element_type=F32)
            ds = (pr * (dp - delta)).astype(BF16)
            dq = jnp.dot(ds, kk, preferred_element_type=F32)
            dk = lax.dot_general(ds, qv, (((0,), (0,)), ((), ())), preferred_element_type=F32)
            for g in range(GQA):
                cols = slice(g * HEAD_DIM, (g + 1) * HEAD_DIM)
                dq_s[qrows, cols] += dq[g * ATTN_BLOCK:(g + 1) * ATTN_BLOCK]
            dk_s[krows, :] += dk
            dv_s[krows, :] += dv
        swap = _swap_matrix()
        cos_v, sin_v = tab_ref[0, :, :HEAD_DIM], tab_ref[0, :, HEAD_DIM:]
        for g in range(GQA):
            cols = slice(g * HEAD_DIM, (g + 1) * HEAD_DIM)
            dq_ref[0, :, cols] = _rope(dq_s[:, cols], cos_v, sin_v, swap, HEAD_DIM ** -0.5, True)
        dk_ref[0, 0] = _rope(dk_s[ATTN_PAD:ATTN_PAD + s_len, :], cos_v, sin_v, swap, 1.0, True)
        dv_ref[0, 0] = dv_s[ATTN_PAD:ATTN_PAD + s_len, :]

    kv_shape = jax.ShapeDtypeStruct((b, N_KV_HEADS, s_len, HEAD_DIM), F32)
    return pl.pallas_call(
        body, name=name, grid=(b, N_KV_HEADS),
        in_specs=[q_spec, kv_spec, t_spec, o_spec, o_spec, o_spec],
        out_specs=[o_spec, kv_out, kv_out],
        out_shape=[jax.ShapeDtypeStruct((b, s_len, ATTN_WIDTH), F32), kv_shape, kv_shape],
        scratch_shapes=[pltpu.VMEM((s_len, Q_GROUP_W), F32), pltpu.VMEM((ATTN_PAD + s_len, HEAD_DIM), F32),
                        pltpu.VMEM((ATTN_PAD + s_len, HEAD_DIM), F32), pltpu.VMEM((s_len, Q_GROUP_W), F32),
                        pltpu.VMEM((s_len, Q_GROUP_W), F32), pltpu.VMEM((ATTN_PAD + s_len, HEAD_DIM), F32),
                        pltpu.VMEM((ATTN_PAD + s_len, HEAD_DIM), F32)],
        compiler_params=pltpu.CompilerParams(dimension_semantics=("arbitrary", "arbitrary"), vmem_limit_bytes=ATTN_VMEM_LIMIT),
    )(proj3, proj3, rope_tab, attn3, lse3, d_attn3)


HALF_W = 2 * HEAD_DIM
N_HALF = Q_GROUP_W // HALF_W


def _attn_prologue(q_refs, kv_ref, tab_ref, q_s, kv_s, hk, s_len):
    swap = _swap_matrix()
    cos_v, sin_v = tab_ref[0, :, :HEAD_DIM], tab_ref[0, :, HEAD_DIM:]
    for j in range(N_HALF):
        for e in range(2):
            cols = slice(e * HEAD_DIM, (e + 1) * HEAD_DIM)
            q_s[j][:, cols] = _rope(q_refs[j][0, :, cols], cos_v, sin_v, swap, HEAD_DIM ** -0.5, False)
    kv_s[0:ATTN_PAD, :] = jnp.zeros((ATTN_PAD, HALF_W), F32)
    for h in range(N_KV_HEADS):
        @pl.when(hk == h)
        def _():
            kv_s[ATTN_PAD:ATTN_PAD + s_len, :HEAD_DIM] = _rope(kv_ref[0, :, h * HEAD_DIM:(h + 1) * HEAD_DIM], cos_v, sin_v,
                                                               swap, 1.0, False)
            kv_s[ATTN_PAD:ATTN_PAD + s_len, HEAD_DIM:] = kv_ref[0, :, LANE + h * HEAD_DIM:LANE + (h + 1) * HEAD_DIM]


def _stack_heads(halves):
    return jnp.concatenate([h[:, e * HEAD_DIM:(e + 1) * HEAD_DIM] for h in halves for e in range(2)], axis=0)


def _unstack_heads(x, j):
    return jnp.concatenate([x[(2 * j + e) * ATTN_BLOCK:(2 * j + e + 1) * ATTN_BLOCK] for e in range(2)], axis=1)


def _stack_stats(halves):
    return jnp.concatenate([jnp.max(h[:, e * HEAD_DIM:(e + 1) * HEAD_DIM], axis=1, keepdims=True)
                            for h in halves for e in range(2)], axis=0)


def _attn_in_specs(s_len):
    assert K_COL % (2 * LANE) == 0 and V_COL == K_COL + LANE

    def halves(first_tile):
        return [pl.BlockSpec((1, s_len, HALF_W), functools.partial(lambda b, h, j: (b, 0, first_tile + N_HALF * h + j), j=j))
                for j in range(N_HALF)]

    kv_spec = pl.BlockSpec((1, s_len, 2 * LANE), lambda b, h: (b, 0, K_COL // (2 * LANE)))
    t_spec = pl.BlockSpec((1, s_len, 2 * HEAD_DIM), lambda b, h: (b, 0, 0))
    o_spec = pl.BlockSpec((1, s_len, Q_GROUP_W), lambda b, h: (b, 0, h))
    return halves(Q_COL // HALF_W), kv_spec, t_spec, o_spec, halves(0)


def attn_fwd(proj3, rope_tab, *, name):
    b, s_len, _ = proj3.shape
    q_specs, kv_spec, t_spec, o_spec, _ = _attn_in_specs(s_len)
    n_br = len(DILATIONS)

    def body(*refs):
        q_refs, (kv_ref, tab_ref, o_ref, lse_ref) = refs[:N_HALF], refs[N_HALF:N_HALF + 4]
        scratch = refs[N_HALF + 4:]
        q_s, kv_s = scratch[:N_HALF], scratch[N_HALF]
        o_s = [scratch[N_HALF + 1 + i * N_HALF:N_HALF + 1 + (i + 1) * N_HALF] for i in range(n_br)]
        l_s = [scratch[N_HALF + 1 + (n_br + i) * N_HALF:N_HALF + 1 + (n_br + i + 1) * N_HALF] for i in range(n_br)]
        _attn_prologue(q_refs, kv_ref, tab_ref, q_s, kv_s, pl.program_id(1), s_len)
        for i, d, q0, k0, n in _attn_blocks(s_len):
            qrows = _rows(q0, ATTN_BLOCK, d)
            qv = _stack_heads([q_s[j][qrows, :] for j in range(N_HALF)]).astype(BF16)
            kvb = kv_s[_rows(k0, 2 * ATTN_BLOCK, d), :].astype(BF16)
            kk, vv = kvb[:, :HEAD_DIM], kvb[:, HEAD_DIM:]
            sc = lax.dot_general(qv, kk, (((1,), (1,)), ((), ())), preferred_element_type=F32)
            sc = jnp.where(_attn_mask(n), sc, NEG_BIG)
            m = jnp.max(sc, axis=-1, keepdims=True)
            pr = jnp.exp(sc - m)
            den = jnp.sum(pr, axis=-1, keepdims=True)
            o = jnp.dot(pr.astype(BF16), vv, preferred_element_type=F32) / den
            lse_b = jnp.broadcast_to(m + jnp.log(den), (GQA * ATTN_BLOCK, HEAD_DIM))
            for j in range(N_HALF):
                o_s[i][j][qrows, :] = _unstack_heads(o, j)
                l_s[i][j][qrows, :] = _unstack_heads(lse_b, j)
        step = 256
        for t0 in range(0, s_len, step):
            rs = pl.ds(t0, step)
            for j in range(N_HALF):
                ls = [l_s[i][j][rs, :] for i in range(n_br)]
                m = functools.reduce(jnp.maximum, ls)
                es = [jnp.exp(l - m) for l in ls]
                tot = functools.reduce(lambda a, c: a + c, es)
                inv = 1.0 / tot
                acc = None
                for i in range(n_br):
                    term = (es[i] * inv) * o_s[i][j][rs, :]
                    acc = term if acc is None else acc + term
                o_ref[0, rs, j * HALF_W:(j + 1) * HALF_W] = acc
                lse_ref[0, rs, j * HALF_W:(j + 1) * HALF_W] = m + jnp.log(tot)

    half_buf = pltpu.VMEM((s_len, HALF_W), F32)
    return pl.pallas_call(
        body, name=name, grid=(b, N_KV_HEADS), in_specs=q_specs + [kv_spec, t_spec],
        out_specs=[o_spec, o_spec],
        out_shape=[jax.ShapeDtypeStruct((b, s_len, ATTN_WIDTH), F32)] * 2,
        scratch_shapes=[half_buf] * N_HALF + [pltpu.VMEM((ATTN_PAD + s_len, HALF_W), F32)] + [half_buf] * (2 * n_br * N_HALF),
        compiler_params=pltpu.CompilerParams(dimension_semantics=("arbitrary", "arbitrary"), vmem_limit_bytes=ATTN_VMEM_LIMIT),
    )(*([proj3] * (N_HALF + 1)), rope_tab)


def attn_bwd(proj3, rope_tab, attn3, lse3, d_attn3, *, name):
    b, s_len, _ = proj3.shape
    q_specs, kv_spec, t_spec, o_spec, half_specs = _attn_in_specs(s_len)
    kv_out = pl.BlockSpec((1, 1, s_len, HEAD_DIM), lambda bi, h: (bi, h, 0, 0))

    def body(*refs):
        q_refs = refs[:N_HALF]
        kv_ref, tab_ref, o_ref = refs[N_HALF:N_HALF + 3]
        lse_refs = refs[N_HALF + 3:2 * N_HALF + 3]
        do_refs = refs[2 * N_HALF + 3:3 * N_HALF + 3]
        dq_ref, dk_ref, dv_ref = refs[3 * N_HALF + 3:3 * N_HALF + 6]
        scratch = refs[3 * N_HALF + 6:]
        q_s, kv_s = scratch[:N_HALF], scratch[N_HALF]
        dl_s = scratch[N_HALF + 1:2 * N_HALF + 1]
        dq_s = scratch[2 * N_HALF + 1:3 * N_HALF + 1]
        dkv_s = scratch[3 * N_HALF + 1]
        _attn_prologue(q_refs, kv_ref, tab_ref, q_s, kv_s, pl.program_id(1), s_len)
        dkv_s[...] = jnp.zeros_like(dkv_s)
        for j in range(N_HALF):
            dq_s[j][...] = jnp.zeros_like(dq_s[j])
            for e in range(2):
                cols = slice(e * HEAD_DIM, (e + 1) * HEAD_DIM)
                ocols = slice(j * HALF_W + e * HEAD_DIM, j * HALF_W + (e + 1) * HEAD_DIM)
                delta = jnp.sum(do_refs[j][0, :, cols] * o_ref[0, :, ocols], axis=1, keepdims=True)
                dl_s[j][:, cols] = jnp.broadcast_to(delta, (s_len, HEAD_DIM))
        for i, d, q0, k0, n in _attn_blocks(s_len):
            qrows, krows = _rows(q0, ATTN_BLOCK, d), _rows(k0, 2 * ATTN_BLOCK, d)
            qv = _stack_heads([q_s[j][qrows, :] for j in range(N_HALF)]).astype(BF16)
            kvb = kv_s[krows, :].astype(BF16)
            kk, vv = kvb[:, :HEAD_DIM], kvb[:, HEAD_DIM:]
            do16 = _stack_heads([do_refs[j].at[0][qrows, :] for j in range(N_HALF)]).astype(BF16)
            lse = _stack_stats([lse_refs[j].at[0][qrows, :] for j in range(N_HALF)])
            delta = _stack_stats([dl_s[j][qrows, :] for j in range(N_HALF)])
            sc = lax.dot_general(qv, kk, (((1,), (1,)), ((), ())), preferred_element_type=F32)
            pr = jnp.where(_attn_mask(n), jnp.exp(sc - lse), 0.0)
            dv = lax.dot_general(pr.astype(BF16), do16, (((0,), (0,)), ((), ())), preferred_element_type=F32)
            dp = lax.dot_general(do16, vv, (((1,), (1,)), ((), ())), preferred_element_type=F32)
            ds = (pr * (dp - delta)).astype(BF16)
            dq = jnp.dot(ds, kk, preferred_element_type=F32)
            dk = lax.dot_general(ds, qv, (((0,), (0,)), ((), ())), preferred_element_type=F32)
            for j in range(N_HALF):
                dq_s[j][qrows, :] += _unstack_heads(dq, j)
            dkv_s[krows, :] += jnp.concatenate([dk, dv], axis=1)
        swap = _swap_matrix()
        cos_v, sin_v = tab_ref[0, :, :HEAD_DIM], tab_ref[0, :, HEAD_DIM:]
        for j in range(N_HALF):
            for e in range(2):
                cols = slice(e * HEAD_DIM, (e + 1) * HEAD_DIM)
                ocols = slice(j * HALF_W + e * HEAD_DIM, j * HALF_W + (e + 1) * HEAD_DIM)
                dq_ref[0, :, ocols] = _rope(dq_s[j][:, cols], cos_v, sin_v, swap, HEAD_DIM ** -0.5, True)
        dk_ref[0, 0] = _rope(dkv_s[ATTN_PAD:ATTN_PAD + s_len, :HEAD_DIM], cos_v, sin_v, swap, 1.0, True)
        dv_ref[0, 0] = dkv_s[ATTN_PAD:ATTN_PAD + s_len, HEAD_DIM:]

    kv_shape = jax.ShapeDtypeStruct((b, N_KV_HEADS, s_len, HEAD_DIM), F32)
    half_buf = pltpu.VMEM((s_len, HALF_W), F32)
    pad_buf = pltpu.VMEM((ATTN_PAD + s_len, HALF_W), F32)
    return pl.pallas_call(
        body, name=name, grid=(b, N_KV_HEADS),
        in_specs=q_specs + [kv_spec, t_spec, o_spec] + half_specs + half_specs,
        out_specs=[o_spec, kv_out, kv_out],
        out_shape=[jax.ShapeDtypeStruct((b, s_len, ATTN_WIDTH), F32), kv_shape, kv_shape],
        scratch_shapes=[half_buf] * N_HALF + [pad_buf] + [half_buf] * (2 * N_HALF) + [pad_buf],
        compiler_params=pltpu.CompilerParams(dimension_semantics=("arbitrary", "arbitrary"), vmem_limit_bytes=ATTN_VMEM_LIMIT),
    )(*([proj3] * (N_HALF + 1)), rope_tab, attn3, *([lse3] * N_HALF), *([d_attn3] * N_HALF))


CONV_TC = 256
CONV_COL0 = XBC_COL // CONV_TC


def _shift_down(u, s):
    if s == 0:
        return u
    rows = lax.broadcasted_iota(jnp.int32, u.shape, 0)
    return jnp.where(rows >= s, pltpu.roll(u, s, 0), 0.0)


def _shift_up(u, s):
    if s == 0:
        return u
    n = u.shape[0]
    rows = lax.broadcasted_iota(jnp.int32, u.shape, 0)
    return jnp.where(rows < n - s, pltpu.roll(u, n - s, 0), 0.0)


def conv_silu_fwd(proj3, w, bias, *, name):
    b, s, _ = proj3.shape
    u_spec = pl.BlockSpec((1, s, CONV_TC), lambda j, bi: (bi, 0, CONV_COL0 + j))
    o_spec = pl.BlockSpec((1, s, CONV_TC), lambda j, bi: (bi, 0, j))
    w_spec = pl.BlockSpec((CONV_WIDTH, CONV_TC), lambda j, bi: (0, j))
    b_spec = pl.BlockSpec((1, CONV_TC), lambda j, bi: (0, j))

    def body(u_ref, w_ref, b_ref, o_ref):
        u = u_ref[0]
        y = jnp.broadcast_to(b_ref[...], u.shape)
        for k in range(CONV_WIDTH):
            y = y + w_ref[k:k + 1, :] * _shift_down(u, CONV_WIDTH - 1 - k)
        o_ref[0] = y * jax.nn.sigmoid(y)

    return pl.pallas_call(
        body, name=name, grid=(CONV_CH // CONV_TC, b), in_specs=[u_spec, w_spec, b_spec], out_specs=o_spec,
        out_shape=jax.ShapeDtypeStruct((b, s, CONV_CH), F32),
        compiler_params=_params(("parallel", "arbitrary")),
    )(proj3, w, bias)


def conv_silu_bwd(proj3, w, bias, dact, *, name):
    b, s, _ = proj3.shape
    u_spec = pl.BlockSpec((1, s, CONV_TC), lambda j, bi: (bi, 0, CONV_COL0 + j))
    o_spec = pl.BlockSpec((1, s, CONV_TC), lambda j, bi: (bi, 0, j))
    w_spec = pl.BlockSpec((CONV_WIDTH, CONV_TC), lambda j, bi: (0, j))
    b_spec = pl.BlockSpec((1, CONV_TC), lambda j, bi: (0, j))

    def body(u_ref, w_ref, b_ref, g_ref, du_ref, dw_ref, db_ref):
        bi = pl.program_id(1)

        @pl.when(bi == 0)
        def _():
            dw_ref[...] = jnp.zeros_like(dw_ref)
            db_ref[...] = jnp.zeros_like(db_ref)

        u = u_ref[0]
        y = jnp.broadcast_to(b_ref[...], u.shape)
        shifted = [_shift_down(u, CONV_WIDTH - 1 - k) for k in range(CONV_WIDTH)]
        for k in range(CONV_WIDTH):
            y = y + w_ref[k:k + 1, :] * shifted[k]
        sig = jax.nn.sigmoid(y)
        dy = g_ref[0] * (sig * (1.0 + y * (1.0 - sig)))
        du = jnp.zeros_like(u)
        for k in range(CONV_WIDTH):
            du = du + w_ref[k:k + 1, :] * _shift_up(dy, CONV_WIDTH - 1 - k)
            dw_ref[k:k + 1, :] += jnp.sum(dy * shifted[k], axis=0, keepdims=True)
        du_ref[0] = du
        db_ref[...] += jnp.sum(dy, axis=0, keepdims=True)

    return pl.pallas_call(
        body, name=name, grid=(CONV_CH // CONV_TC, b), in_specs=[u_spec, w_spec, b_spec, o_spec],
        out_specs=[o_spec, w_spec, b_spec],
        out_shape=[jax.ShapeDtypeStruct((b, s, CONV_CH), F32), jax.ShapeDtypeStruct((CONV_WIDTH, CONV_CH), F32),
                   jax.ShapeDtypeStruct((1, CONV_CH), F32)],
        compiler_params=_params(("parallel", "arbitrary")),
    )(proj3, w, bias, dact)


def _softplus(z):
    e = jnp.exp(-jnp.abs(z))
    u = 1.0 + e
    log1p = jnp.where(u == 1.0, e, jnp.log(u) * e / jnp.where(u == 1.0, 1.0, u - 1.0))
    return jnp.maximum(z, 0.0) + log1p


def _tri(lower):
    r = lax.broadcasted_iota(jnp.int32, (CHUNK, CHUNK), 0)
    c = lax.broadcasted_iota(jnp.int32, (CHUNK, CHUNK), 1)
    return (r >= c) if lower else (r <= c)


def _ssd_common(dtr_ref, dtb_ref, alog_ref):
    z = dtr_ref[0] + dtb_ref[...]
    dt = _softplus(z)
    aneg = -jnp.exp(alog_ref[...])
    acs = _dot01_left(_tri(True).astype(BF16), dt * aneg)
    return z, dt, aneg, acs


def _col(mat, onehot):
    return jnp.sum(mat * onehot, axis=1, keepdims=True)


def _ssd_head(x, dt_j, acs_j, cb, tri_mask, last_row, acs_row=None):
    acs_last = jnp.sum(acs_j * last_row, axis=0, keepdims=True)
    xg = x * dt_j
    bc = jnp.broadcast_to(acs_j, (CHUNK, CHUNK))
    dm = bc - (bc.T if acs_row is None else jnp.broadcast_to(acs_row, (CHUNK, CHUNK)))
    lm = jnp.where(tri_mask, jnp.exp(jnp.where(tri_mask, dm, 0.0)), 0.0)
    mm = cb * lm
    decay_s = jnp.exp(acs_last - acs_j)
    return acs_last, xg, lm, mm, decay_s


def _ssd_specs(nc, reverse):
    cidx = (lambda c: nc - 1 - c) if reverse else (lambda c: c)
    act_spec = pl.BlockSpec((1, CHUNK, CONV_CH), lambda b, c: (b, cidx(c), 0))
    y_spec = pl.BlockSpec((1, CHUNK, SSM_INNER), lambda b, c: (b, cidx(c), 0))
    dt_in_spec = pl.BlockSpec((1, CHUNK, LANE), lambda b, c: (b, cidx(c), DT_COL // LANE))
    dt_out_spec = pl.BlockSpec((1, CHUNK, LANE), lambda b, c: (b, cidx(c), 0))
    par_spec = pl.BlockSpec((1, LANE), lambda b, c: (0, 0))
    h_spec = pl.BlockSpec((1, SSM_HEADS, 1, SSM_P, D_STATE), lambda b, c: (b, 0, cidx(c), 0, 0))
    return act_spec, y_spec, dt_in_spec, dt_out_spec, par_spec, h_spec


def _head_cols(h):
    return slice(h * SSM_P, (h + 1) * SSM_P)


def _group_cols(g, which):
    start = SSM_INNER + which * SSM_GROUPS * D_STATE + g * D_STATE
    return slice(start, start + D_STATE)


def ssd_fwd(act3, proj3, dtb, alog, dsk, *, name):
    b, s, _ = act3.shape
    nc = s // CHUNK
    act_spec, y_spec, dt_in_spec, _, par_spec, h_spec = _ssd_specs(nc, False)

    def body(act_ref, dtr_ref, dtb_ref, alog_ref, dsk_ref, y_ref, hp_ref, state):
        c = pl.program_id(1)

        @pl.when(c == 0)
        def _():
            state[...] = jnp.zeros_like(state)

        _, dt, _, acs = _ssd_common(dtr_ref, dtb_ref, alog_ref)
        acs_t = acs.T
        tri_mask = _tri(True)
        last_row = (lax.broadcasted_iota(jnp.int32, (CHUNK, 1), 0) == CHUNK - 1).astype(F32)
        for g in range(SSM_GROUPS):
            b16 = act_ref[0, :, _group_cols(g, 0)].astype(BF16)
            c16 = act_ref[0, :, _group_cols(g, 1)].astype(BF16)
            cb = lax.dot_general(c16, b16, (((1,), (1,)), ((), ())), preferred_element_type=F32)
            for j in range(HEADS_PER_GROUP):
                hidx = g * HEADS_PER_GROUP + j
                x = act_ref[0, :, _head_cols(hidx)]
                dt_j, acs_j = dt[:, hidx:hidx + 1], acs[:, hidx:hidx + 1]
                acs_last, xg, _, mm, decay_s = _ssd_head(x, dt_j, acs_j, cb, tri_mask, last_row, acs_t[hidx:hidx + 1, :])
                y_diag = jnp.dot(mm.astype(BF16), xg.astype(BF16), preferred_element_type=F32)
                st = lax.dot_general((xg * decay_s).astype(BF16), b16, (((0,), (0,)), ((), ())), preferred_element_type=F32)
                hp = state[hidx]
                hp_ref[0, hidx, 0] = hp
                y_off = lax.dot_general(c16, hp.astype(BF16), (((1,), (1,)), ((), ())), preferred_element_type=F32)
                d_j = dsk_ref[:, hidx:hidx + 1]
                y_ref[0, :, _head_cols(hidx)] = y_diag + y_off * jnp.exp(acs_j) + d_j * x
                state[hidx] = hp * jnp.exp(acs_last) + st

    return pl.pallas_call(
        body, name=name, grid=(b, nc),
        in_specs=[act_spec, dt_in_spec, par_spec, par_spec, par_spec],
        out_specs=[y_spec, h_spec],
        out_shape=[jax.ShapeDtypeStruct((b, s, SSM_INNER), F32),
                   jax.ShapeDtypeStruct((b, SSM_HEADS, nc, SSM_P, D_STATE), F32)],
        scratch_shapes=[pltpu.VMEM((SSM_HEADS, SSM_P, D_STATE), F32)],
        compiler_params=_params(("arbitrary", "arbitrary")),
    )(act3, proj3, dtb, alog, dsk)


def ssd_bwd(act3, proj3, dtb, alog, dsk, hprev, dy3, *, name):
    b, s, _ = act3.shape
    nc = s // CHUNK
    act_spec, y_spec, dt_in_spec, dt_out_spec, par_spec, h_spec = _ssd_specs(nc, True)
    dpar_spec = pl.BlockSpec((8, LANE), lambda bi, c: (0, 0))

    def body(act_ref, dtr_ref, dtb_ref, alog_ref, dsk_ref, hp_ref, dy_ref, dact_ref, ddtr_ref, dpar_ref, dstate):
        bi, c = pl.program_id(0), pl.program_id(1)

        @pl.when(c == 0)
        def _():
            dstate[...] = jnp.zeros_like(dstate)

        @pl.when((bi == 0) & (c == 0))
        def _():
            dpar_ref[...] = jnp.zeros_like(dpar_ref)

        z, dt, aneg, acs = _ssd_common(dtr_ref, dtb_ref, alog_ref)
        acs_t = acs.T
        tri_mask = _tri(True)
        last_row = (lax.broadcasted_iota(jnp.int32, (CHUNK, 1), 0) == CHUNK - 1).astype(F32)
        lanes = lax.broadcasted_iota(jnp.int32, (1, LANE), 1)
        sublanes = lax.broadcasted_iota(jnp.int32, (LANE, 1), 0)
        ddt_mat = jnp.zeros((CHUNK, LANE), F32)
        dacs_mat = jnp.zeros((CHUNK, LANE), F32)
        dacs_rows = jnp.zeros((LANE, CHUNK), F32)
        ddsk_row = jnp.zeros((1, LANE), F32)
        for g in range(SSM_GROUPS):
            b16 = act_ref[0, :, _group_cols(g, 0)].astype(BF16)
            c16 = act_ref[0, :, _group_cols(g, 1)].astype(BF16)
            cb = lax.dot_general(c16, b16, (((1,), (1,)), ((), ())), preferred_element_type=F32)
            dcb = jnp.zeros((CHUNK, CHUNK), F32)
            db_acc = jnp.zeros((CHUNK, D_STATE), F32)
            dc_acc = jnp.zeros((CHUNK, D_STATE), F32)
            for j in range(HEADS_PER_GROUP):
                hidx = g * HEADS_PER_GROUP + j
                onehot = (lanes == hidx).astype(F32)
                x = act_ref[0, :, _head_cols(hidx)]
                dt_j, acs_j = dt[:, hidx:hidx + 1], acs[:, hidx:hidx + 1]
                acs_last, xg, lm, mm, decay_s = _ssd_head(x, dt_j, acs_j, cb, tri_mask, last_row, acs_t[hidx:hidx + 1, :])
                ea = jnp.exp(acs_j)
                cd = jnp.exp(acs_last)
                d_j = dsk_ref[:, hidx:hidx + 1]
                hp = hp_ref[0, hidx, 0]
                hp16 = hp.astype(BF16)
                g_y = dy_ref[0, :, _head_cols(hidx)]
                g_y16 = g_y.astype(BF16)
                g_hn = dstate[hidx]
                g_hn16 = g_hn.astype(BF16)
                xg16 = xg.astype(BF16)
                ddsk_row = ddsk_row + jnp.sum(jnp.sum(g_y * x, axis=1, keepdims=True), axis=0, keepdims=True) * onehot
                d_mm = lax.dot_general(g_y16, xg16, (((1,), (1,)), ((), ())), preferred_element_type=F32)
                d_xg = lax.dot_general(mm.astype(BF16), g_y16, (((0,), (0,)), ((), ())), preferred_element_type=F32)
                dcb = dcb + d_mm * lm
                d_dm = d_mm * mm
                d_acs = jnp.sum(d_dm, axis=1, keepdims=True)
                dacs_rows = dacs_rows + (sublanes == hidx).astype(F32) * jnp.sum(d_dm, axis=0, keepdims=True)
                t_off = lax.dot_general(c16, hp16, (((1,), (1,)), ((), ())), preferred_element_type=F32)
                d_t16 = (g_y * ea).astype(BF16)
                d_acs = d_acs + jnp.sum(g_y * t_off, axis=1, keepdims=True) * ea
                dc_acc = dc_acc + jnp.dot(d_t16, hp16, preferred_element_type=F32)
                d_hp = lax.dot_general(d_t16, c16, (((0,), (0,)), ((), ())), preferred_element_type=F32) + g_hn * cd
                d_last = jnp.sum(jnp.sum(g_hn * hp, axis=1, keepdims=True), axis=0, keepdims=True) * cd
                d_w = lax.dot_general(b16, g_hn16, (((1,), (1,)), ((), ())), preferred_element_type=F32)
                db_acc = db_acc + jnp.dot((xg * decay_s).astype(BF16), g_hn16, preferred_element_type=F32)
                d_xg = d_xg + d_w * decay_s
                d_ds = jnp.sum(d_w * xg, axis=1, keepdims=True) * decay_s
                d_last = d_last + jnp.sum(d_ds, axis=0, keepdims=True)
                d_acs = d_acs - d_ds + d_last * last_row
                dact_ref[0, :, _head_cols(hidx)] = d_j * g_y + d_xg * dt_j
                ddt_mat = ddt_mat + jnp.sum(d_xg * x, axis=1, keepdims=True) * onehot
                dacs_mat = dacs_mat + d_acs * onehot
                dstate[hidx] = d_hp
            dcb16 = dcb.astype(BF16)
            dact_ref[0, :, _group_cols(g, 1)] = dc_acc + jnp.dot(dcb16, b16, preferred_element_type=F32)
            dact_ref[0, :, _group_cols(g, 0)] = db_acc + lax.dot_general(dcb16, c16, (((0,), (0,)), ((), ())),
                                                                         preferred_element_type=F32)
        d_a = _dot01_left(_tri(False).astype(BF16), dacs_mat - dacs_rows.T)
        ddt_mat = ddt_mat + d_a * aneg
        d_raw = ddt_mat * jax.nn.sigmoid(z)
        ddtr_ref[0] = d_raw
        dpar_ref[0:1, :] += jnp.sum(d_raw, axis=0, keepdims=True)
        dpar_ref[1:2, :] += jnp.sum(d_a * dt, axis=0, keepdims=True) * aneg
        dpar_ref[2:3, :] += ddsk_row

    return pl.pallas_call(
        body, name=name, grid=(b, nc),
        in_specs=[act_spec, dt_in_spec, par_spec, par_spec, par_spec, h_spec, y_spec],
        out_specs=[act_spec, dt_out_spec, dpar_spec],
        out_shape=[jax.ShapeDtypeStruct(act3.shape, F32), jax.ShapeDtypeStruct((b, s, LANE), F32),
                   jax.ShapeDtypeStruct((8, LANE), F32)],
        scratch_shapes=[pltpu.VMEM((SSM_HEADS, SSM_P, D_STATE), F32)],
        compiler_params=_params(("arbitrary", "arbitrary")),
    )(act3, proj3, dtb, alog, dsk, hprev, dy3)


def _unused_ssd_specs(nc, reverse):
    cidx = (lambda c: nc - 1 - c) if reverse else (lambda c: c)
    x_spec = pl.BlockSpec((1, HEADS_PER_GROUP, CHUNK, SSM_P), lambda b, c, g: (b, g, cidx(c), 0))
    bc_spec = pl.BlockSpec((1, 1, CHUNK, D_STATE), lambda b, c, g: (b, g, cidx(c), 0))
    dt_spec = pl.BlockSpec((1, CHUNK, LANE), lambda b, c, g: (b, cidx(c), 0))
    par_spec = pl.BlockSpec((1, LANE), lambda b, c, g: (0, 0))
    h_spec = pl.BlockSpec((1, HEADS_PER_GROUP, 1, SSM_P, D_STATE), lambda b, c, g: (b, g, cidx(c), 0, 0))
    return x_spec, bc_spec, dt_spec, par_spec, h_spec


def _unused_ssd_fwd(xs, bm, cm, dtr, dtb, alog, dsk, *, name):
    b, _, s, _ = xs.shape
    nc = s // CHUNK
    x_spec, bc_spec, dt_spec, par_spec, h_spec = _ssd_specs(nc, False)

    def body(x_ref, b_ref, c_ref, dtr_ref, dtb_ref, alog_ref, dsk_ref, y_ref, hp_ref, state):
        c, g = pl.program_id(1), pl.program_id(2)

        @pl.when(c == 0)
        def _():
            state[pl.ds(g * HEADS_PER_GROUP, HEADS_PER_GROUP)] = jnp.zeros((HEADS_PER_GROUP, SSM_P, D_STATE), F32)

        _, dt, _, acs = _ssd_common(dtr_ref, dtb_ref, alog_ref)
        b16, c16 = b_ref[0, 0].astype(BF16), c_ref[0, 0].astype(BF16)
        cb = lax.dot_general(c16, b16, (((1,), (1,)), ((), ())), preferred_element_type=F32)
        tri_mask = _tri(True)
        last_row = (lax.broadcasted_iota(jnp.int32, (CHUNK, 1), 0) == CHUNK - 1).astype(F32)
        lanes = lax.broadcasted_iota(jnp.int32, (1, LANE), 1)
        for j in range(HEADS_PER_GROUP):
            hidx = g * HEADS_PER_GROUP + j
            onehot = (lanes == hidx).astype(F32)
            x = x_ref[0, j]
            dt_j, acs_j = _col(dt, onehot), _col(acs, onehot)
            acs_last, xg, _, mm, decay_s = _ssd_head(x, dt_j, acs_j, cb, tri_mask, last_row)
            xg16 = xg.astype(BF16)
            y_diag = jnp.dot(mm.astype(BF16), xg16, preferred_element_type=F32)
            st = lax.dot_general((xg * decay_s).astype(BF16), b16, (((0,), (0,)), ((), ())), preferred_element_type=F32)
            hp = state[hidx]
            hp_ref[0, j, 0] = hp
            y_off = lax.dot_general(c16, hp.astype(BF16), (((1,), (1,)), ((), ())), preferred_element_type=F32)
            d_j = jnp.sum(dsk_ref[...] * onehot, axis=1, keepdims=True)
            y_ref[0, j] = y_diag + y_off * jnp.exp(acs_j) + d_j * x
            state[hidx] = hp * jnp.exp(acs_last) + st

    return pl.pallas_call(
        body, name=name, grid=(b, nc, SSM_GROUPS),
        in_specs=[x_spec, bc_spec, bc_spec, dt_spec, par_spec, par_spec, par_spec],
        out_specs=[x_spec, h_spec],
        out_shape=[jax.ShapeDtypeStruct(xs.shape, F32),
                   jax.ShapeDtypeStruct((b, SSM_HEADS, nc, SSM_P, D_STATE), F32)],
        scratch_shapes=[pltpu.VMEM((SSM_HEADS, SSM_P, D_STATE), F32)],
        compiler_params=_params(("arbitrary", "arbitrary", "arbitrary")),
    )(xs, bm, cm, dtr, dtb, alog, dsk)


def _unused_ssd_bwd(xs, bm, cm, dtr, dtb, alog, dsk, hprev, dy, *, name):
    b, _, s, _ = xs.shape
    nc = s // CHUNK
    x_spec, bc_spec, dt_spec, par_spec, h_spec = _ssd_specs(nc, True)
    dpar_spec = pl.BlockSpec((8, LANE), lambda bi, c, g: (0, 0))

    def body(x_ref, b_ref, c_ref, dtr_ref, dtb_ref, alog_ref, dsk_ref, hp_ref, dy_ref,
             dx_ref, db_ref, dc_ref, ddtr_ref, dpar_ref, dstate):
        bi, c, g = pl.program_id(0), pl.program_id(1), pl.program_id(2)

        @pl.when(c == 0)
        def _():
            dstate[pl.ds(g * HEADS_PER_GROUP, HEADS_PER_GROUP)] = jnp.zeros((HEADS_PER_GROUP, SSM_P, D_STATE), F32)

        @pl.when((bi == 0) & (c == 0) & (g == 0))
        def _():
            dpar_ref[...] = jnp.zeros_like(dpar_ref)

        z, dt, aneg, acs = _ssd_common(dtr_ref, dtb_ref, alog_ref)
        bv, cv = b_ref[0, 0], c_ref[0, 0]
        b16, c16 = bv.astype(BF16), cv.astype(BF16)
        cb = lax.dot_general(c16, b16, (((1,), (1,)), ((), ())), preferred_element_type=F32)
        tri_mask = _tri(True)
        last_row = (lax.broadcasted_iota(jnp.int32, (CHUNK, 1), 0) == CHUNK - 1).astype(F32)
        lanes = lax.broadcasted_iota(jnp.int32, (1, LANE), 1)
        dcb = jnp.zeros((CHUNK, CHUNK), F32)
        db_acc = jnp.zeros((CHUNK, D_STATE), F32)
        dc_acc = jnp.zeros((CHUNK, D_STATE), F32)
        ddt_mat = jnp.zeros((CHUNK, LANE), F32)
        dacs_mat = jnp.zeros((CHUNK, LANE), F32)
        ddsk_row = jnp.zeros((1, LANE), F32)
        for j in range(HEADS_PER_GROUP):
            hidx = g * HEADS_PER_GROUP + j
            onehot = (lanes == hidx).astype(F32)
            x = x_ref[0, j]
            dt_j, acs_j = _col(dt, onehot), _col(acs, onehot)
            acs_last, xg, lm, mm, decay_s = _ssd_head(x, dt_j, acs_j, cb, tri_mask, last_row)
            ea = jnp.exp(acs_j)
            cd = jnp.exp(acs_last)
            d_j = jnp.sum(dsk_ref[...] * onehot, axis=1, keepdims=True)
            hp = hp_ref[0, j, 0]
            hp16 = hp.astype(BF16)
            g_y = dy_ref[0, j]
            g_y16 = g_y.astype(BF16)
            g_hn = dstate[hidx]
            g_hn16 = g_hn.astype(BF16)
            xg16 = xg.astype(BF16)
            ddsk_row = ddsk_row + jnp.sum(jnp.sum(g_y * x, axis=1, keepdims=True), axis=0, keepdims=True) * onehot
            d_mm = lax.dot_general(g_y16, xg16, (((1,), (1,)), ((), ())), preferred_element_type=F32)
            d_xg = lax.dot_general(mm.astype(BF16), g_y16, (((0,), (0,)), ((), ())), preferred_element_type=F32)
            dcb = dcb + d_mm * lm
            d_dm = d_mm * mm
            d_acs = jnp.sum(d_dm, axis=1, keepdims=True) - jnp.sum(d_dm.T, axis=1, keepdims=True)
            t_off = lax.dot_general(c16, hp16, (((1,), (1,)), ((), ())), preferred_element_type=F32)
            d_t16 = (g_y * ea).astype(BF16)
            d_acs = d_acs + jnp.sum(g_y * t_off, axis=1, keepdims=True) * ea
            dc_acc = dc_acc + jnp.dot(d_t16, hp16, preferred_element_type=F32)
            d_hp = lax.dot_general(d_t16, c16, (((0,), (0,)), ((), ())), preferred_element_type=F32) + g_hn * cd
            d_last = jnp.sum(jnp.sum(g_hn * hp, axis=1, keepdims=True), axis=0, keepdims=True) * cd
            d_w = lax.dot_general(b16, g_hn16, (((1,), (1,)), ((), ())), preferred_element_type=F32)
            db_acc = db_acc + jnp.dot((xg * decay_s).astype(BF16), g_hn16, preferred_element_type=F32)
            d_xg = d_xg + d_w * decay_s
            d_ds = jnp.sum(d_w * xg, axis=1, keepdims=True) * decay_s
            d_last = d_last + jnp.sum(d_ds, axis=0, keepdims=True)
            d_acs = d_acs - d_ds + d_last * last_row
            dx_ref[0, j] = d_j * g_y + d_xg * dt_j
            ddt_mat = ddt_mat + jnp.sum(d_xg * x, axis=1, keepdims=True) * onehot
            dacs_mat = dacs_mat + d_acs * onehot
            dstate[hidx] = d_hp
        dcb16 = dcb.astype(BF16)
        dc_ref[0, 0] = dc_acc + jnp.dot(dcb16, b16, preferred_element_type=F32)
        db_ref[0, 0] = db_acc + lax.dot_general(dcb16, c16, (((0,), (0,)), ((), ())), preferred_element_type=F32)
        d_a = _dot01_left(_tri(False).astype(BF16), dacs_mat)
        ddt_mat = ddt_mat + d_a * aneg
        d_aneg = jnp.sum(d_a * dt, axis=0, keepdims=True)
        d_raw = ddt_mat * jax.nn.sigmoid(z)

        @pl.when(g == 0)
        def _():
            ddtr_ref[0] = d_raw

        @pl.when(g != 0)
        def _():
            ddtr_ref[0] += d_raw

        dpar_ref[0:1, :] += jnp.sum(d_raw, axis=0, keepdims=True)
        dpar_ref[1:2, :] += d_aneg * aneg
        dpar_ref[2:3, :] += ddsk_row

    return pl.pallas_call(
        body, name=name, grid=(b, nc, SSM_GROUPS),
        in_specs=[x_spec, bc_spec, bc_spec, dt_spec, par_spec, par_spec, par_spec, h_spec, x_spec],
        out_specs=[x_spec, bc_spec, bc_spec, dt_spec, dpar_spec],
        out_shape=[jax.ShapeDtypeStruct(xs.shape, F32), jax.ShapeDtypeStruct(bm.shape, F32),
                   jax.ShapeDtypeStruct(cm.shape, F32), jax.ShapeDtypeStruct(dtr.shape, F32),
                   jax.ShapeDtypeStruct((8, LANE), F32)],
        scratch_shapes=[pltpu.VMEM((SSM_HEADS, SSM_P, D_STATE), F32)],
        compiler_params=_params(("arbitrary", "arbitrary", "arbitrary")),
    )(xs, bm, cm, dtr, dtb, alog, dsk, hprev, dy)


def to_heads(x, b, s, h):
    return x.reshape(b, s, h, -1).transpose(0, 2, 1, 3)


def from_heads(x):
    b, h, s, c = x.shape
    return x.transpose(0, 2, 1, 3).reshape(b * s, h * c)


def dilate_q(q, d):
    b, _, s, c = q.shape
    x = q.reshape(b, N_KV_HEADS, GQA, s // d, d, c).transpose(0, 1, 4, 2, 3, 5)
    return x.reshape(b * N_KV_HEADS * d, GQA, s // d, c)


def undilate_q(x, b, d):
    _, _, l, c = x.shape
    y = x.reshape(b, N_KV_HEADS, d, GQA, l, c).transpose(0, 1, 3, 4, 2, 5)
    return y.reshape(b, N_Q_HEADS, l * d, c)


def dilate_kv(k, d):
    b, h, s, c = k.shape
    return k.reshape(b, h, s // d, d, c).transpose(0, 1, 3, 2, 4).reshape(b * h * d, s // d, c)


def undilate_kv(x, b, d):
    _, l, c = x.shape
    return x.reshape(b, N_KV_HEADS, d, l, c).transpose(0, 1, 3, 2, 4).reshape(b, N_KV_HEADS, l * d, c)


def rotary_tables(positions):
    inv_freq = ROPE_THETA ** (-jnp.arange(0, ROPE_DIM, 2, dtype=F32) / ROPE_DIM)
    ang = positions.astype(F32)[..., None] * inv_freq
    cos, sin = jnp.cos(ang), jnp.sin(ang)
    rest = HEAD_DIM - ROPE_DIM
    cosf = jnp.concatenate([cos, cos, jnp.ones(cos.shape[:2] + (rest,), F32)], axis=-1)
    sinf = jnp.concatenate([-sin, sin, jnp.zeros(sin.shape[:2] + (rest,), F32)], axis=-1)
    return cosf, sinf


def w_in_columns(w):
    pad = jnp.zeros((w.shape[0], IN_PAD - IN_PROJ), w.dtype)
    return jnp.concatenate([w[:, :Q_END], w[:, V_END:XBC_END], w[:, Q_END:V_END], w[:, XBC_END:], pad], axis=1)


def w_in_grad_columns(g):
    return jnp.concatenate([g[:, :Z_COL], g[:, K_COL:DT_COL], g[:, Z_COL:K_COL], g[:, DT_COL:DT_COL + SSM_HEADS]], axis=1)


def lane_pad(v):
    return jnp.pad(v.reshape(1, -1), ((0, 0), (0, LANE - v.shape[-1])))


def layer_fwd(h, wts, small, rope_tab, b, s, tag):
    w_in, w_out, w_gate, w_up, w_down = wts
    t = b * s
    sv = {"h": h}
    hn = rowwise_fwd(rms_fn, [h], [small["norm_mix"]], [BF16], name=f"rms_mix_{tag}")[0]
    proj = matmul(hn, w_in, name=f"in_proj_{tag}")
    sv["hn"], sv["proj"] = hn, proj
    proj3 = proj.reshape(b, s, IN_PAD)
    attn3, lse3 = attn_fwd(proj3, rope_tab, name=f"attn_{tag}")
    sv["attn3"], sv["lse3"] = attn3, lse3
    attn = attn3.reshape(t, ATTN_WIDTH)
    act3 = conv_silu_fwd(proj3, small["conv_w"], small["conv_b"], name=f"conv_{tag}")
    y3, hprev = ssd_fwd(act3, proj3, small["dt_bias"], small["a_log"], small["d_skip"], name=f"ssd_{tag}")
    y = y3.reshape(t, SSM_INNER)
    sv["act3"], sv["hprev"], sv["y"] = act3, hprev, y
    gn = rowwise_fwd(gated_norm_fn, [y, proj], [small["ssm_norm"]], [F32], name=f"gated_norm_{tag}", groups=SSM_GROUPS,
                     windows=[None, (Z_COL, SSM_INNER)])[0]
    cat = jnp.concatenate([attn, gn], axis=1).astype(BF16)
    sv["cat"] = cat
    h1 = matmul(cat, w_out, name=f"out_proj_{tag}", residual=h)
    sv["h1"] = h1
    hn2 = rowwise_fwd(rms_fn, [h1], [small["norm_ffn"]], [BF16], name=f"rms_ffn_{tag}")[0]
    gate = matmul(hn2, w_gate, name=f"ffn_gate_{tag}")
    up = matmul(hn2, w_up, name=f"ffn_up_{tag}")
    act2 = rowwise_fwd(swiglu_fn, [gate, up], [], [BF16], name=f"swiglu_{tag}")[0]
    sv["hn2"], sv["gate"], sv["up"], sv["act2"] = hn2, gate, up, act2
    h2 = matmul(act2, w_down, name=f"ffn_down_{tag}", residual=h1)
    return h2, sv


def layer_bwd(dh2, sv, wts, small, rope_tab, b, s, tag):
    w_in, w_out, w_gate, w_up, w_down = wts
    t = b * s
    gr = {}
    dh2_16 = dh2.astype(BF16)
    d_act2 = matmul(dh2_16, w_down, tb=True, name=f"ffn_down_dx_{tag}")
    gr["w_down"] = matmul(sv["act2"], dh2_16, ta=True, out_dtype=BF16, name=f"ffn_down_dw_{tag}")
    d_gate, d_up = rowwise_bwd(swiglu_fn, [sv["gate"], sv["up"]], [], [d_act2], [BF16, BF16], name=f"swiglu_bwd_{tag}")
    gr["w_gate"] = matmul(sv["hn2"], d_gate, ta=True, out_dtype=BF16, name=f"ffn_gate_dw_{tag}")
    gr["w_up"] = matmul(sv["hn2"], d_up, ta=True, out_dtype=BF16, name=f"ffn_up_dw_{tag}")
    d_hn2 = matmul(d_gate, w_gate, tb=True, name=f"ffn_gate_dx_{tag}")
    d_hn2 = matmul(d_up, w_up, tb=True, residual=d_hn2, name=f"ffn_up_dx_{tag}")
    dh1, gr["norm_ffn"] = rowwise_bwd(rms_fn, [sv["h1"]], [small["norm_ffn"]], [d_hn2], [F32],
                                      name=f"rms_ffn_bwd_{tag}", add_to_first=dh2)
    dh1_16 = dh1.astype(BF16)
    d_cat = matmul(dh1_16, w_out, tb=True, name=f"out_proj_dx_{tag}")
    gr["w_out"] = matmul(sv["cat"], dh1_16, ta=True, out_dtype=BF16, name=f"out_proj_dw_{tag}")
    d_attn, d_gn = d_cat[:, :ATTN_WIDTH], d_cat[:, ATTN_WIDTH:]
    d_y, d_z, gr["ssm_norm"] = rowwise_bwd(gated_norm_fn, [sv["y"], sv["proj"]], [small["ssm_norm"]], [d_gn], [F32, F32],
                                           name=f"gated_norm_bwd_{tag}", groups=SSM_GROUPS,
                                           windows=[None, (Z_COL, SSM_INNER)])
    proj3 = sv["proj"].reshape(b, s, IN_PAD)
    d_act3, d_dtr, d_par = ssd_bwd(sv["act3"], proj3, small["dt_bias"], small["a_log"], small["d_skip"], sv["hprev"],
                                   d_y.reshape(b, s, SSM_INNER), name=f"ssd_bwd_{tag}")
    gr["dt_bias"], gr["a_log"], gr["d_skip"] = d_par[0, :SSM_HEADS], d_par[1, :SSM_HEADS], d_par[2, :SSM_HEADS]
    d_xbc, gr["conv_w"], gr["conv_b"] = conv_silu_bwd(proj3, small["conv_w"], small["conv_b"], d_act3,
                                                      name=f"conv_bwd_{tag}")
    d_q3, d_k4, d_v4 = attn_bwd(proj3, rope_tab, sv["attn3"], sv["lse3"], d_attn.reshape(b, s, ATTN_WIDTH),
                                name=f"attn_bwd_{tag}")
    d_proj = jnp.concatenate([d_q3.reshape(t, ATTN_WIDTH), d_z, d_xbc.reshape(t, CONV_CH), from_heads(d_k4),
                              from_heads(d_v4), d_dtr.reshape(t, LANE)], axis=1).astype(BF16)
    d_hn = matmul(d_proj, w_in, tb=True, name=f"in_proj_dx_{tag}")
    gr["w_in"] = w_in_grad_columns(matmul(sv["hn"], d_proj, ta=True, out_dtype=BF16, name=f"in_proj_dw_{tag}"))
    dh, gr["norm_mix"] = rowwise_bwd(rms_fn, [sv["h"]], [small["norm_mix"]], [d_hn], [F32],
                                     name=f"rms_mix_bwd_{tag}", add_to_first=dh1)
    return dh, gr


def local_step(x, positions, big, small_all, final_norm, loss_target):
    b, s, _ = x.shape
    t = b * s
    rope_tab = jnp.concatenate(rotary_tables(positions), axis=-1)
    h = x.reshape(t, D_MODEL)
    saved = []
    for l in range(DEPTH):
        h, sv = layer_fwd(h, big[l], small_all[l], rope_tab, b, s, f"l{l}")
        saved.append(sv)
    dh, d_final, loss = loss_and_grad(h, loss_target.reshape(t, D_MODEL), final_norm.reshape(1, D_MODEL))
    grads = [None] * DEPTH
    for l in reversed(range(DEPTH)):
        dh, grads[l] = layer_bwd(dh, saved[l], big[l], small_all[l], rope_tab, b, s, f"l{l}")
    return loss, dh.reshape(b, s, D_MODEL), grads, d_final


def _slab_rows(r):
    return r if r <= 512 else _pick(r, (512, 256))


def cast_bf16(x, *, name):
    def fn(v):
        return (v,)
    return rowwise_fwd(fn, [x], [], [BF16], name=name, tr=_slab_rows(x.shape[0]))[0]


def sum_slots(x, *, name):
    n, r, c = x.shape
    tr = _slab_rows(r)

    def body(x_ref, o_ref):
        acc = x_ref[0].astype(F32)
        for i in range(1, n):
            acc = acc + x_ref[i].astype(F32)
        o_ref[...] = acc

    return pl.pallas_call(
        body, name=name, grid=(r // tr,), in_specs=[pl.BlockSpec((n, tr, c), lambda i: (0, i, 0))],
        out_specs=pl.BlockSpec((tr, c), lambda i: (i, 0)), out_shape=jax.ShapeDtypeStruct((r, c), F32),
        compiler_params=_params(("parallel",)),
    )(x)


def adamw(g_parts, w, m, v, *, name):
    r, c = w.shape
    tr = _slab_rows(r)
    n_g = len(g_parts)
    bc1 = 1.0 / (1.0 - ADAM_B1 ** ADAM_STEP)
    bc2 = 1.0 / (1.0 - ADAM_B2 ** ADAM_STEP)

    def body(*refs):
        g = refs[0][...]
        for r_ in refs[1:n_g]:
            g = g + r_[...]
        w_ref, m_ref, v_ref, g_out, d_out, m_out, v_out = refs[n_g:]
        m_new = ADAM_B1 * m_ref[...] + (1.0 - ADAM_B1) * g
        v_new = ADAM_B2 * v_ref[...] + (1.0 - ADAM_B2) * (g * g)
        g_out[...] = g
        m_out[...] = m_new
        v_out[...] = v_new
        d_out[...] = -ADAM_LR * ((m_new * bc1) / (jnp.sqrt(v_new * bc2) + ADAM_EPS) + ADAM_WD * w_ref[...])

    spec = pl.BlockSpec((tr, c), lambda i: (i, 0))
    return pl.pallas_call(
        body, name=name, grid=(r // tr,), in_specs=[spec] * (n_g + 3), out_specs=[spec] * 4,
        out_shape=[jax.ShapeDtypeStruct((r, c), F32)] * 4, compiler_params=_params(("parallel",)),
    )(*g_parts, w, m, v)


def _other_chips(x, y):
    return [(1 - x, y), (x, 1 - y), (1 - x, 1 - y)]


def allgather_chips(shards):
    n_arr = len(shards)

    def body(*refs):
        in_refs, out_refs = refs[:n_arr], refs[n_arr:2 * n_arr]
        send_sems, recv_sems, local_sems = refs[2 * n_arr:]
        x, y, c = lax.axis_index("x"), lax.axis_index("y"), lax.axis_index("c")
        chip = 2 * x + y
        started = []
        for a, (in_ref, out_ref) in enumerate(zip(in_refs, out_refs)):
            mine = pltpu.make_async_copy(in_ref, out_ref.at[chip], local_sems.at[a])
            mine.start()
            started.append(mine.wait)
            for k, (px, py) in enumerate(_other_chips(x, y)):
                cp = pltpu.make_async_remote_copy(src_ref=in_ref, dst_ref=out_ref.at[chip], send_sem=send_sems.at[3 * a + k],
                                                  recv_sem=recv_sems.at[3 * a + k], device_id=(px, py, c), device_id_type=MESH)
                cp.start()
                started.append(cp.wait_send)
        for a, (in_ref, out_ref) in enumerate(zip(in_refs, out_refs)):
            for k, (px, py) in enumerate(_other_chips(x, y)):
                pltpu.make_async_remote_copy(src_ref=in_ref, dst_ref=out_ref.at[2 * px + py], send_sem=send_sems.at[3 * a + k],
                                             recv_sem=recv_sems.at[3 * a + k], device_id=(px, py, c),
                                             device_id_type=MESH).wait_recv()
        for wait in started:
            wait()

    hbm = pl.BlockSpec(memory_space=pltpu.HBM)
    return pl.pallas_call(
        body, name="allgather_weights", in_specs=[hbm] * n_arr, out_specs=[hbm] * n_arr,
        out_shape=[jax.ShapeDtypeStruct((N_CHIPS,) + s.shape, s.dtype) for s in shards],
        scratch_shapes=[pltpu.SemaphoreType.DMA((3 * n_arr,)), pltpu.SemaphoreType.DMA((3 * n_arr,)),
                        pltpu.SemaphoreType.DMA((n_arr,))],
    )(*shards)


def exchange_grads(big, small):
    def body(big_ref, small_ref, big_out, small_out, send_sems, recv_sems, local_sems):
        x, y, c = lax.axis_index("x"), lax.axis_index("y"), lax.axis_index("c")
        chip = 2 * x + y
        dev = 4 * x + 2 * y + c
        own_big = pltpu.make_async_copy(big_ref.at[chip], big_out.at[chip], local_sems.at[0])
        own_small = pltpu.make_async_copy(small_ref, small_out.at[dev], local_sems.at[1])
        own_big.start()
        own_small.start()
        sends = []
        for k, (px, py) in enumerate(_other_chips(x, y)):
            cp = pltpu.make_async_remote_copy(src_ref=big_ref.at[2 * px + py], dst_ref=big_out.at[chip],
                                              send_sem=send_sems.at[k], recv_sem=recv_sems.at[k],
                                              device_id=(px, py, c), device_id_type=MESH)
            cp.start()
            sends.append(cp)
        peers = []
        for r in range(1, N_DEV):
            fx, fy, fc = (r >> 2) & 1, (r >> 1) & 1, r & 1
            px, py, pc = (x + fx) % 2, (y + fy) % 2, (c + fc) % 2
            peers.append((px, py, pc))
            cp = pltpu.make_async_remote_copy(src_ref=small_ref, dst_ref=small_out.at[dev], send_sem=send_sems.at[2 + r],
                                              recv_sem=recv_sems.at[2 + r], device_id=(px, py, pc), device_id_type=MESH)
            cp.start()
            sends.append(cp)
        for k, (px, py) in enumerate(_other_chips(x, y)):
            pltpu.make_async_remote_copy(src_ref=big_ref.at[chip], dst_ref=big_out.at[2 * px + py],
                                         send_sem=send_sems.at[k], recv_sem=recv_sems.at[k],
                                         device_id=(px, py, c), device_id_type=MESH).wait_recv()
        for r, (px, py, pc) in zip(range(1, N_DEV), peers):
            pltpu.make_async_remote_copy(src_ref=small_ref, dst_ref=small_out.at[4 * px + 2 * py + pc],
                                         send_sem=send_sems.at[2 + r], recv_sem=recv_sems.at[2 + r],
                                         device_id=(px, py, pc), device_id_type=MESH).wait_recv()
        for cp in sends:
            cp.wait_send()
        own_big.wait()
        own_small.wait()

    hbm = pl.BlockSpec(memory_space=pltpu.HBM)
    n_sem = 3 + N_DEV - 1
    return pl.pallas_call(
        body, name="exchange_grads", in_specs=[hbm, hbm], out_specs=[hbm, hbm],
        out_shape=[jax.ShapeDtypeStruct(big.shape, big.dtype), jax.ShapeDtypeStruct((N_DEV,) + small.shape, small.dtype)],
        scratch_shapes=[pltpu.SemaphoreType.DMA((n_sem,)), pltpu.SemaphoreType.DMA((n_sem,)), pltpu.SemaphoreType.DMA((2,))],
    )(big, small)


SWAP_CHUNKS = 27


def swap_cores(mine):
    rows = mine.shape[0] // SWAP_CHUNKS
    assert rows * SWAP_CHUNKS == mine.shape[0] and rows % 8 == 0

    def body(in_ref, out_ref, send_sems, recv_sems):
        x, y, c = lax.axis_index("x"), lax.axis_index("y"), lax.axis_index("c")

        def chunk(k):
            part = pl.ds(k * rows, rows)
            return pltpu.make_async_remote_copy(src_ref=in_ref.at[part], dst_ref=out_ref.at[part],
                                                send_sem=send_sems.at[k], recv_sem=recv_sems.at[k],
                                                device_id=(x, y, 1 - c), device_id_type=MESH)

        for k in range(SWAP_CHUNKS):
            chunk(k).start()
        for k in range(SWAP_CHUNKS):
            chunk(k).wait_recv()
        for k in range(SWAP_CHUNKS):
            chunk(k).wait_send()

    hbm = pl.BlockSpec(memory_space=pltpu.HBM)
    return pl.pallas_call(
        body, name="swap_cores", in_specs=[hbm], out_specs=hbm,
        out_shape=jax.ShapeDtypeStruct(mine.shape, mine.dtype),
        scratch_shapes=[pltpu.SemaphoreType.DMA((SWAP_CHUNKS,)), pltpu.SemaphoreType.DMA((SWAP_CHUNKS,))],
    )(mine)


BIG_NAMES = ("w_in", "w_out", "w_gate", "w_up", "w_down")
BIG_SHARD_AXIS = {"w_in": 1, "w_out": 0, "w_gate": 1, "w_up": 1, "w_down": 0}
PACK_COLS = 1024
SMALL_NAMES = ("norm_mix", "conv_w", "conv_b", "dt_bias", "a_log", "d_skip", "ssm_norm", "norm_ffn")


PACK_ROW_TILE = 256


def pack_big(shards):
    flat = jnp.concatenate([shards[n].reshape(-1) for n in BIG_NAMES])
    unit = PACK_ROW_TILE * PACK_COLS
    total = -(-flat.size // unit) * unit
    return jnp.pad(flat, (0, total - flat.size)).reshape(-1, PACK_COLS)


def unpack_big(packed, like):
    out, off = {}, 0
    flat = packed.reshape(-1)
    for n in BIG_NAMES:
        size = like[n].size
        out[n] = flat[off:off + size].reshape(like[n].shape)
        off += size
    return out


def pack_small(parts):
    flat = jnp.concatenate([p.reshape(-1).astype(F32) for p in parts])
    rows = -(-flat.size // LANE)
    rows = -(-rows // 8) * 8
    return jnp.pad(flat, (0, rows * LANE - flat.size)).reshape(rows, LANE)


def unpack_small(packed, like):
    out, off = [], 0
    flat = packed.reshape(-1)
    for a in like:
        out.append(flat[off:off + a.size].reshape(a.shape))
        off += a.size
    return out


def kernel(x, positions, norm_mix, w_in, conv_w, conv_b, dt_bias, a_log, d_skip, ssm_norm, w_out, norm_ffn, w_gate, w_up, w_down, final_norm, loss_target, m_norm_mix, m_w_in, m_conv_w, m_conv_b, m_dt_bias, m_a_log, m_d_skip, m_ssm_norm, m_w_out, m_norm_ffn, m_w_gate, m_w_up, m_w_down, m_final_norm, v_norm_mix, v_w_in, v_conv_w, v_conv_b, v_dt_bias, v_a_log, v_d_skip, v_ssm_norm, v_w_out, v_norm_ffn, v_w_gate, v_w_up, v_w_down, v_final_norm):
    chip = 2 * lax.axis_index("x") + lax.axis_index("y")
    w_sh = {"w_in": w_in, "w_out": w_out, "w_gate": w_gate, "w_up": w_up, "w_down": w_down}
    m_sh = {"w_in": m_w_in, "w_out": m_w_out, "w_gate": m_w_gate, "w_up": m_w_up, "w_down": m_w_down}
    v_sh = {"w_in": v_w_in, "w_out": v_w_out, "w_gate": v_w_gate, "w_up": v_w_up, "w_down": v_w_down}

    w_packed = pack_big(w_sh)
    conv_cols = CONV_CH // N_CHIPS
    gathered, conv_g = allgather_chips([cast_bf16(w_packed, name="cast_weights"), conv_w.reshape(-1, LANE)])
    pieces = [unpack_big(gathered[j], w_sh) for j in range(N_CHIPS)]
    full = {n: jnp.concatenate([p[n] for p in pieces], axis=BIG_SHARD_AXIS[n] + 1) for n in BIG_NAMES}
    big = []
    for l in range(DEPTH):
        big.append((w_in_columns(full["w_in"][l]), full["w_out"][l], full["w_gate"][l], full["w_up"][l], full["w_down"][l]))
    conv_w_full = jnp.concatenate([conv_g[j].reshape(DEPTH, CONV_WIDTH, conv_cols) for j in range(N_CHIPS)], axis=2)

    small_all = []
    for l in range(DEPTH):
        small_all.append({
            "norm_mix": norm_mix[l].reshape(1, -1), "conv_w": conv_w_full[l], "conv_b": conv_b[l].reshape(1, -1),
            "dt_bias": lane_pad(dt_bias[l]), "a_log": lane_pad(a_log[l]), "d_skip": lane_pad(d_skip[l]),
            "ssm_norm": ssm_norm[l].reshape(1, -1), "norm_ffn": norm_ffn[l].reshape(1, -1)})

    loss_part, grad_x, grads, d_final = local_step(x, positions, big, small_all, final_norm, loss_target)

    def shard_of(name, g, j):
        n = g.shape[BIG_SHARD_AXIS[name]] // N_CHIPS
        return lax.slice_in_dim(g, j * n, (j + 1) * n, axis=BIG_SHARD_AXIS[name])

    to_chip = []
    for j in range(N_CHIPS):
        to_chip.append(pack_big({n: jnp.stack([shard_of(n, grads[l][n], j) for l in range(DEPTH)]) for n in BIG_NAMES}))
    small_parts = [jnp.stack([grads[l][n].reshape(-1) for l in range(DEPTH)]) for n in SMALL_NAMES]
    small_parts += [d_final.reshape(-1), loss_part.reshape(-1)]
    recv_big, recv_small = exchange_grads(jnp.stack(to_chip), pack_small(small_parts))
    plane_sum = sum_slots(recv_big, name="sum_chip_partials")
    other_plane = swap_cores(plane_sum)

    g_big, d_big, m_big, v_big = adamw([plane_sum, other_plane], w_packed, pack_big(m_sh), pack_big(v_sh), name="adamw_big")
    g_big, d_big, m_big, v_big = (unpack_big(a, w_sh) for a in (g_big, d_big, m_big, v_big))

    small_sum = sum_slots(recv_small, name="sum_small")
    like = [norm_mix, conv_w_full, conv_b, dt_bias, a_log, d_skip, ssm_norm, norm_ffn, final_norm, loss_part.reshape(-1)]
    g_small = unpack_small(small_sum, like)
    loss = g_small[-1][0]
    g_small = dict(zip(SMALL_NAMES + ("final_norm",), g_small[:-1]))
    g_small["conv_w"] = lax.dynamic_slice_in_dim(g_small["conv_w"], chip * conv_cols, conv_cols, axis=2)
    w_small = {"norm_mix": norm_mix, "conv_w": conv_w, "conv_b": conv_b, "dt_bias": dt_bias, "a_log": a_log, "d_skip": d_skip,
               "ssm_norm": ssm_norm, "norm_ffn": norm_ffn, "final_norm": final_norm}
    m_small = {"norm_mix": m_norm_mix, "conv_w": m_conv_w, "conv_b": m_conv_b, "dt_bias": m_dt_bias, "a_log": m_a_log,
               "d_skip": m_d_skip, "ssm_norm": m_ssm_norm, "norm_ffn": m_norm_ffn, "final_norm": m_final_norm}
    v_small = {"norm_mix": v_norm_mix, "conv_w": v_conv_w, "conv_b": v_conv_b, "dt_bias": v_dt_bias, "a_log": v_a_log,
               "d_skip": v_d_skip, "ssm_norm": v_ssm_norm, "norm_ffn": v_norm_ffn, "final_norm": v_final_norm}
    names = SMALL_NAMES + ("final_norm",)
    order = [w_small[n] for n in names]
    res = adamw([pack_small([g_small[n] for n in names])], pack_small(order), pack_small([m_small[n] for n in names]),
                pack_small([v_small[n] for n in names]), name="adamw_small")
    g_s, d_s, m_s, v_s = (dict(zip(names, unpack_small(a, order))) for a in res)

    all_names = ("norm_mix", "w_in", "conv_w", "conv_b", "dt_bias", "a_log", "d_skip", "ssm_norm", "w_out", "norm_ffn",
                 "w_gate", "w_up", "w_down", "final_norm")
    outs = [loss, grad_x]
    for src_big, src_small in ((g_big, g_s), (d_big, d_s), (m_big, m_s), (v_big, v_s)):
        outs += [src_big[n] if n in BIG_NAMES else src_small[n] for n in all_names]
    return tuple(outs)
```

```python
import functools

import jax
import jax.numpy as jnp
from jax import lax
from jax.experimental import pallas as pl
from jax.experimental.pallas import tpu as pltpu

F32 = jnp.float32
BF16 = jnp.bfloat16
MESH = pl.DeviceIdType.MESH

D_MODEL = 1024
DEPTH = 2
HEAD_DIM = 64
N_Q_HEADS = 8
N_KV_HEADS = 2
GQA = N_Q_HEADS // N_KV_HEADS
ATTN_WIDTH = N_Q_HEADS * HEAD_DIM
ROPE_DIM = HEAD_DIM // 4
ROPE_HALF = ROPE_DIM // 2
ROPE_THETA = 500000.0
DILATIONS = (1, 4, 16)
ATTN_BLOCK = 128
SSM_P = 64
SSM_HEADS = 16
SSM_INNER = SSM_HEADS * SSM_P
SSM_GROUPS = 2
HEADS_PER_GROUP = SSM_HEADS // SSM_GROUPS
D_STATE = 128
CONV_WIDTH = 4
CHUNK = 128
CONV_CH = SSM_INNER + 2 * SSM_GROUPS * D_STATE
MIX_WIDTH = ATTN_WIDTH + SSM_INNER
Q_END = ATTN_WIDTH
K_END = Q_END + N_KV_HEADS * HEAD_DIM
V_END = K_END + N_KV_HEADS * HEAD_DIM
Z_END = V_END + SSM_INNER
XBC_END = Z_END + CONV_CH
IN_PROJ = XBC_END + SSM_HEADS
LANE = 128
IN_PAD = XBC_END + LANE
Q_COL, Z_COL, XBC_COL, K_COL, V_COL, DT_COL = 0, 512, 1536, 3072, 3200, 3328
FFN_HIDDEN = 2816
EPS = 1e-5
ADAM_LR, ADAM_B1, ADAM_B2, ADAM_EPS, ADAM_WD, ADAM_STEP = 0.001, 0.9, 0.999, 1e-8, 0.01, 10
N_CHIPS = 4
N_DEV = 8
VMEM_LIMIT = 48 * 1024 * 1024
NEG_BIG = -1e30


def _params(sem=None):
    return pltpu.CompilerParams(dimension_semantics=sem, vmem_limit_bytes=VMEM_LIMIT)


def _pick(n, prefs):
    for p in prefs:
        if n % p == 0:
            return p
    return n


def matmul(a, b, *, name, ta=False, tb=False, out_dtype=F32, residual=None):
    if ta:
        assert not tb and residual is None
        return _matmul_over_rows(a, b, name=name, out_dtype=out_dtype)
    return _matmul_full_k(a, b, name=name, tb=tb, out_dtype=out_dtype, residual=residual)


def _matmul_full_k(a, b, *, name, tb, out_dtype, residual):
    m, kdim = a.shape
    n = b.shape[0] if tb else b.shape[1]
    tm = _pick(m, (1024, 512, 256)) if kdim <= 1536 else _pick(m, (512, 256))
    tn = _pick(n, (1152, 1408, 1536, 1024, 768, 512, 384, 256, 128))
    b_spec = pl.BlockSpec((tn, kdim), lambda i, j: (j, 0)) if tb else pl.BlockSpec((kdim, tn), lambda i, j: (0, j))
    o_spec = pl.BlockSpec((tm, tn), lambda i, j: (i, j))
    dims = (((1,), (1 if tb else 0,)), ((), ()))
    has_res = residual is not None

    def body(*refs):
        a_ref, b_ref = refs[:2]
        o_ref = refs[-1]
        r = lax.dot_general(a_ref[...].astype(BF16), b_ref[...].astype(BF16), dims, preferred_element_type=F32)
        if has_res:
            r = r + refs[2][...]
        o_ref[...] = r.astype(out_dtype)

    in_specs = [pl.BlockSpec((tm, kdim), lambda i, j: (i, 0)), b_spec] + ([o_spec] if has_res else [])
    args = (a, b) + ((residual,) if has_res else ())
    return pl.pallas_call(
        body, name=name, grid=(m // tm, n // tn), in_specs=in_specs, out_specs=o_spec,
        out_shape=jax.ShapeDtypeStruct((m, n), out_dtype),
        compiler_params=_params(("parallel", "parallel")),
    )(*args)


def _matmul_over_rows(a, b, *, name, out_dtype):
    t, m = a.shape
    n = b.shape[1]
    tm = _pick(m, (1024, 1408, 768, 512, 256, 128))
    tn = _pick(n, (1152, 1408, 1024, 768, 512, 256, 128))
    tk = _pick(t, (1024, 512, 256, 128))
    nk = t // tk

    def body(a_ref, b_ref, o_ref, acc):
        k = pl.program_id(2)
        part = lax.dot_general(a_ref[...].astype(BF16), b_ref[...].astype(BF16), (((0,), (0,)), ((), ())),
                               preferred_element_type=F32)

        @pl.when(k == 0)
        def _():
            acc[...] = part

        @pl.when(k > 0)
        def _():
            acc[...] += part

        @pl.when(k == nk - 1)
        def _():
            o_ref[...] = acc[...].astype(out_dtype)

    return pl.pallas_call(
        body, name=name, grid=(m // tm, n // tn, nk),
        in_specs=[pl.BlockSpec((tk, tm), lambda i, j, k: (k, i)), pl.BlockSpec((tk, tn), lambda i, j, k: (k, j))],
        out_specs=pl.BlockSpec((tm, tn), lambda i, j, k: (i, j)),
        out_shape=jax.ShapeDtypeStruct((m, n), out_dtype),
        scratch_shapes=[pltpu.VMEM((tm, tn), F32)],
        compiler_params=_params(("parallel", "parallel", "arbitrary")),
    )(a, b)


ROW_BLOCK_BYTES = 16 * 1024 * 1024


def _row_tile(t, tr, widths, n_copies):
    lanes = sum(-(-wd // LANE) * LANE for wd in widths) * n_copies
    tr = min(tr, t)
    while tr > 8 and tr * lanes * 4 > ROW_BLOCK_BYTES:
        tr //= 2
    return tr


def _row_widths(rows, groups, windows):
    windows = windows or [None] * len(rows)
    widths = [(w[1] if w else a.shape[1]) // groups for a, w in zip(rows, windows)]
    assert all(w is None or w[0] % wd == 0 for w, wd in zip(windows, widths))
    return widths, [(w[0] // wd if w else 0) for w, wd in zip(windows, widths)]


def _row_specs(tr, widths, offs):
    return [pl.BlockSpec((tr, wd), functools.partial(lambda g, i, off: (i, g + off), off=off)) for wd, off in zip(widths, offs)]


def rowwise_fwd(fn, rows, params, out_dtypes, *, name, tr=512, groups=1, windows=None):
    t = rows[0].shape[0]
    widths, offs = _row_widths(rows, groups, windows)
    tr = _row_tile(t, tr, widths, 2)
    row_specs = _row_specs(tr, widths, offs)
    par_spec = lambda p: pl.BlockSpec((1, p.shape[1] // groups), lambda g, i: (0, g))
    n_in = len(rows) + len(params)
    out_cols = [o.shape[1] for o in jax.eval_shape(
        fn, *[jax.ShapeDtypeStruct((tr, wd), F32) for wd in widths],
        *[jax.ShapeDtypeStruct((1, p.shape[1] // groups), F32) for p in params])]

    def body(*refs):
        vals = [r[...].astype(F32) for r in refs[:n_in]]
        outs = fn(*vals)
        for o_ref, o in zip(refs[n_in:], outs):
            o_ref[...] = o.astype(o_ref.dtype)

    return pl.pallas_call(
        body, name=name, grid=(groups, t // tr),
        in_specs=row_specs + [par_spec(p) for p in params],
        out_specs=[pl.BlockSpec((tr, c), lambda g, i: (i, g)) for c in out_cols],
        out_shape=[jax.ShapeDtypeStruct((t, c * groups), d) for c, d in zip(out_cols, out_dtypes)],
        compiler_params=_params(("arbitrary", "arbitrary")),
    )(*rows, *params)


def rowwise_bwd(fn, rows, params, cts, drow_dtypes, *, name, tr=512, groups=1, add_to_first=None, windows=None):
    t = rows[0].shape[0]
    widths, offs = _row_widths(rows, groups, windows)
    tr = _row_tile(t, tr, widths + [a.shape[1] // groups for a in cts], 2)
    row_spec = lambda a: pl.BlockSpec((tr, a.shape[1] // groups), lambda g, i: (i, g))
    row_specs = _row_specs(tr, widths, offs)
    par_spec = lambda p: pl.BlockSpec((1, p.shape[1] // groups), lambda g, i: (0, g))
    n_rows, n_par, n_ct = len(rows), len(params), len(cts)
    has_add = add_to_first is not None
    n_in = n_rows + n_par + n_ct + (1 if has_add else 0)

    def body(*refs):
        i = pl.program_id(1)
        vals = [r[...].astype(F32) for r in refs[:n_rows + n_par]]
        ct_vals = tuple(r[...].astype(F32) for r in refs[n_rows + n_par:n_rows + n_par + n_ct])
        _, vjp = jax.vjp(fn, *vals)
        grads = vjp(ct_vals)
        out_refs = refs[n_in:]
        for idx in range(n_rows):
            g = grads[idx]
            if idx == 0 and has_add:
                g = g + refs[n_in - 1][...]
            out_refs[idx][...] = g.astype(out_refs[idx].dtype)
        for idx in range(n_par):
            p_ref = out_refs[n_rows + idx]

            @pl.when(i == 0)
            def _():
                p_ref[...] = jnp.zeros_like(p_ref)

            p_ref[...] += grads[n_rows + idx]

    ins = list(rows) + list(params) + list(cts) + ([add_to_first] if has_add else [])
    in_specs = (row_specs + [par_spec(p) for p in params] + [row_spec(a) for a in cts]
                + ([row_spec(add_to_first)] if has_add else []))
    return pl.pallas_call(
        body, name=name, grid=(groups, t // tr), in_specs=in_specs,
        out_specs=[pl.BlockSpec((tr, wd), lambda g, i: (i, g)) for wd in widths] + [par_spec(p) for p in params],
        out_shape=[jax.ShapeDtypeStruct((t, wd * groups), d) for wd, d in zip(widths, drow_dtypes)]
        + [jax.ShapeDtypeStruct(p.shape, F32) for p in params],
        compiler_params=_params(("arbitrary", "arbitrary")),
    )(*ins)


def rms_fn(x, w):
    return (x * lax.rsqrt(jnp.mean(x * x, axis=-1, keepdims=True) + EPS) * w,)


def swiglu_fn(g, u):
    return (g * jax.nn.sigmoid(g) * u,)


def gated_norm_fn(y, z, w):
    v = y * (z * jax.nn.sigmoid(z))
    return (v * lax.rsqrt(jnp.mean(v * v, axis=-1, keepdims=True) + EPS) * w,)


def combine_fn(o1, o2, o3, l1, l2, l3):
    m = jnp.maximum(jnp.maximum(l1, l2), l3)
    e1, e2, e3 = jnp.exp(l1 - m), jnp.exp(l2 - m), jnp.exp(l3 - m)
    inv = 1.0 / (e1 + e2 + e3)
    return ((e1 * inv) * o1 + (e2 * inv) * o2 + (e3 * inv) * o3,)


def loss_and_grad(h, target, w, *, tr=512):
    t, d = h.shape

    def loss_fn(hv, wv, tv):
        err = rms_fn(hv, wv)[0] - tv
        per_row = jnp.mean(err * err, axis=-1, keepdims=True)
        return 0.5 * jnp.sum(per_row, axis=0, keepdims=True)

    def body(h_ref, t_ref, w_ref, dh_ref, dw_ref, loss_ref):
        i = pl.program_id(0)

        @pl.when(i == 0)
        def _():
            dw_ref[...] = jnp.zeros_like(dw_ref)
            loss_ref[...] = jnp.zeros_like(loss_ref)

        tv = t_ref[...]
        val, vjp = jax.vjp(lambda hv, wv: loss_fn(hv, wv, tv), h_ref[...], w_ref[...])
        dh, dw = vjp(jnp.ones((1, 1), F32))
        dh_ref[...] = dh
        dw_ref[...] += dw
        loss_ref[...] += jnp.broadcast_to(val, loss_ref.shape)

    row = pl.BlockSpec((tr, d), lambda i: (i, 0))
    par = pl.BlockSpec((1, d), lambda i: (0, 0))
    return pl.pallas_call(
        body, name="loss_and_grad", grid=(t // tr,), in_specs=[row, row, par],
        out_specs=[row, par, pl.BlockSpec((1, LANE), lambda i: (0, 0))],
        out_shape=[jax.ShapeDtypeStruct((t, d), F32), jax.ShapeDtypeStruct((1, d), F32),
                   jax.ShapeDtypeStruct((1, LANE), F32)],
        compiler_params=_params(("arbitrary",)),
    )(h, target, w)


def _split3(x):
    hi = x.astype(BF16)
    r1 = x - hi.astype(F32)
    mid = r1.astype(BF16)
    lo = (r1 - mid.astype(F32)).astype(BF16)
    return hi, mid, lo


def _dot01_left(m01, x):
    return sum(jnp.dot(m01, p, preferred_element_type=F32) for p in _split3(x))


def _dot01_right(x, m01):
    return sum(jnp.dot(p, m01, preferred_element_type=F32) for p in _split3(x))


def rotary(xs_list, cosf, sinf, scale, *, adjoint, name, ts=512):
    b, h, s, c = xs_list[0].shape
    n_x = len(xs_list)

    def body(*refs):
        x = refs[0][0, 0]
        for r in refs[1:n_x]:
            x = x + r[0, 0]
        cos_v, sin_v = refs[n_x][0], refs[n_x + 1][0]
        o_ref = refs[n_x + 2]
        ci = lax.broadcasted_iota(jnp.int32, (c, c), 0)
        cj = lax.broadcasted_iota(jnp.int32, (c, c), 1)
        swap = ((cj == ci + ROPE_HALF) & (ci < ROPE_HALF)) | ((cj == ci - ROPE_HALF) & (ci >= ROPE_HALF) & (ci < ROPE_DIM))
        swap = swap.astype(BF16)
        if adjoint:
            out = x * cos_v + _dot01_right(x * sin_v, swap)
        else:
            out = x * cos_v + _dot01_right(x, swap) * sin_v
        o_ref[0, 0] = out * scale

    x_spec = pl.BlockSpec((1, 1, ts, c), lambda bi, hi, si: (bi, hi, si, 0))
    t_spec = pl.BlockSpec((1, ts, c), lambda bi, hi, si: (bi, si, 0))
    return pl.pallas_call(
        body, name=name, grid=(b, h, s // ts), in_specs=[x_spec] * n_x + [t_spec, t_spec], out_specs=x_spec,
        out_shape=jax.ShapeDtypeStruct((b, h, s, c), F32),
        compiler_params=_params(("parallel", "parallel", "parallel")),
    )(*xs_list, cosf, sinf)


def add3(a, b, c, *, name, tr=1024):
    def fn(x, y, z):
        return (x + y + z,)
    return rowwise_fwd(fn, [a, b, c], [], [F32], name=name, tr=tr)[0]


def _attn_mask(n):
    rows = GQA * ATTN_BLOCK
    qi = lax.broadcasted_iota(jnp.int32, (rows, 2 * ATTN_BLOCK), 0) % ATTN_BLOCK
    ki = lax.broadcasted_iota(jnp.int32, (rows, 2 * ATTN_BLOCK), 1)
    delta = qi + ATTN_BLOCK - ki
    return (delta >= 0) & (delta <= ATTN_BLOCK) & ((n - 1) * ATTN_BLOCK + ki >= 0)


def _attn_specs(l):
    q_spec = pl.BlockSpec((1, GQA, ATTN_BLOCK, HEAD_DIM), lambda p, n: (p, 0, n, 0))
    l_spec = pl.BlockSpec((1, GQA, ATTN_BLOCK, 1), lambda p, n: (p, 0, n, 0))
    kprev = pl.BlockSpec((1, ATTN_BLOCK, HEAD_DIM), lambda p, n: (p, jnp.maximum(n - 1, 0), 0))
    kcur = pl.BlockSpec((1, ATTN_BLOCK, HEAD_DIM), lambda p, n: (p, n, 0))
    kfull = pl.BlockSpec((1, l, HEAD_DIM), lambda p, n: (p, 0, 0))
    return q_spec, l_spec, kprev, kcur, kfull


def attn_branch_fwd(q, k, v, *, name):
    p_cnt, _, l, _ = q.shape
    rows = GQA * ATTN_BLOCK
    q_spec, l_spec, kprev, kcur, _ = _attn_specs(l)

    def body(q_ref, kp_ref, kc_ref, vp_ref, vc_ref, o_ref, lse_ref):
        n = pl.program_id(1)
        qv = q_ref[0].reshape(rows, HEAD_DIM).astype(BF16)
        kk = jnp.concatenate([kp_ref[0], kc_ref[0]], axis=0).astype(BF16)
        vv = jnp.concatenate([vp_ref[0], vc_ref[0]], axis=0).astype(BF16)
        s = lax.dot_general(qv, kk, (((1,), (1,)), ((), ())), preferred_element_type=F32)
        s = jnp.where(_attn_mask(n), s, NEG_BIG)
        m = jnp.max(s, axis=-1, keepdims=True)
        pr = jnp.exp(s - m)
        den = jnp.sum(pr, axis=-1, keepdims=True)
        o = jnp.dot(pr.astype(BF16), vv, preferred_element_type=F32) / den
        o_ref[0] = o.reshape(GQA, ATTN_BLOCK, HEAD_DIM)
        lse_ref[0] = (m + jnp.log(den)).reshape(GQA, ATTN_BLOCK, 1)

    return pl.pallas_call(
        body, name=name, grid=(p_cnt, l // ATTN_BLOCK), in_specs=[q_spec, kprev, kcur, kprev, kcur],
        out_specs=[q_spec, l_spec],
        out_shape=[jax.ShapeDtypeStruct(q.shape, F32), jax.ShapeDtypeStruct(q.shape[:3] + (1,), F32)],
        compiler_params=_params(("parallel", "arbitrary")),
    )(q, k, k, v, v)


def attn_branch_bwd(q, k, v, o, lse, do, dlse, *, name):
    p_cnt, _, l, _ = q.shape
    rows = GQA * ATTN_BLOCK
    q_spec, l_spec, kprev, kcur, kfull = _attn_specs(l)

    def body(q_ref, kp_ref, kc_ref, vp_ref, vc_ref, o_ref, lse_ref, do_ref, dlse_ref, dq_ref, dk_ref, dv_ref):
        n = pl.program_id(1)

        @pl.when(n == 0)
        def _():
            dk_ref[...] = jnp.zeros_like(dk_ref)
            dv_ref[...] = jnp.zeros_like(dv_ref)

        qv = q_ref[0].reshape(rows, HEAD_DIM).astype(BF16)
        kk = jnp.concatenate([kp_ref[0], kc_ref[0]], axis=0).astype(BF16)
        vv = jnp.concatenate([vp_ref[0], vc_ref[0]], axis=0).astype(BF16)
        ov = o_ref[0].reshape(rows, HEAD_DIM)
        dov = do_ref[0].reshape(rows, HEAD_DIM)
        lsev = lse_ref[0].reshape(rows, 1)
        dlsev = dlse_ref[0].reshape(rows, 1)
        s = lax.dot_general(qv, kk, (((1,), (1,)), ((), ())), preferred_element_type=F32)
        pr = jnp.where(_attn_mask(n), jnp.exp(s - lsev), 0.0)
        do16 = dov.astype(BF16)
        dv = lax.dot_general(pr.astype(BF16), do16, (((0,), (0,)), ((), ())), preferred_element_type=F32)
        dp = lax.dot_general(do16, vv, (((1,), (1,)), ((), ())), preferred_element_type=F32)
        delta = jnp.sum(dov * ov, axis=-1, keepdims=True)
        ds = (pr * (dp - delta + dlsev)).astype(BF16)
        dq = jnp.dot(ds, kk, preferred_element_type=F32)
        dk = lax.dot_general(ds, qv, (((0,), (0,)), ((), ())), preferred_element_type=F32)
        dq_ref[0] = dq.reshape(GQA, ATTN_BLOCK, HEAD_DIM)
        cur = pl.ds(pl.multiple_of(n * ATTN_BLOCK, ATTN_BLOCK), ATTN_BLOCK)
        dk_ref[0, cur, :] += dk[ATTN_BLOCK:]
        dv_ref[0, cur, :] += dv[ATTN_BLOCK:]

        @pl.when(n > 0)
        def _():
            prev = pl.ds(pl.multiple_of((n - 1) * ATTN_BLOCK, ATTN_BLOCK), ATTN_BLOCK)
            dk_ref[0, prev, :] += dk[:ATTN_BLOCK]
            dv_ref[0, prev, :] += dv[:ATTN_BLOCK]

    return pl.pallas_call(
        body, name=name, grid=(p_cnt, l // ATTN_BLOCK),
        in_specs=[q_spec, kprev, kcur, kprev, kcur, q_spec, l_spec, q_spec, l_spec],
        out_specs=[q_spec, kfull, kfull],
        out_shape=[jax.ShapeDtypeStruct(q.shape, F32), jax.ShapeDtypeStruct(k.shape, F32),
                   jax.ShapeDtypeStruct(v.shape, F32)],
        compiler_params=_params(("parallel", "arbitrary")),
    )(q, k, k, v, v, o, lse, do, dlse)


ATTN_PAD = ATTN_BLOCK * DILATIONS[-1]
Q_GROUP_W = GQA * HEAD_DIM
ATTN_VMEM_LIMIT = 56 * 1024 * 1024


def _rope(x, cos_v, sin_v, swap, scale, adjoint):
    if adjoint:
        return (x * cos_v + _dot01_right(x * sin_v, swap)) * scale
    return (x * cos_v + _dot01_right(x, swap) * sin_v) * scale


def _swap_matrix():
    c = HEAD_DIM
    ci = lax.broadcasted_iota(jnp.int32, (c, c), 0)
    cj = lax.broadcasted_iota(jnp.int32, (c, c), 1)
    swap = ((cj == ci + ROPE_HALF) & (ci < ROPE_HALF)) | ((cj == ci - ROPE_HALF) & (ci >= ROPE_HALF) & (ci < ROPE_DIM))
    return swap.astype(BF16)


def _attn_prologue(q_ref, kv_ref, tab_ref, q_s, k_s, v_s, hk, s_len):
    swap = _swap_matrix()
    cos_v, sin_v = tab_ref[0, :, :HEAD_DIM], tab_ref[0, :, HEAD_DIM:]
    for g in range(GQA):
        cols = slice(g * HEAD_DIM, (g + 1) * HEAD_DIM)
        q_s[:, cols] = _rope(q_ref[0, :, cols], cos_v, sin_v, swap, HEAD_DIM ** -0.5, False)
    zeros = jnp.zeros((ATTN_PAD, HEAD_DIM), F32)
    k_s[0:ATTN_PAD, :] = zeros
    v_s[0:ATTN_PAD, :] = zeros
    for h in range(N_KV_HEADS):
        @pl.when(hk == h)
        def _():
            k_s[ATTN_PAD:ATTN_PAD + s_len, :] = _rope(kv_ref[0, :, h * HEAD_DIM:(h + 1) * HEAD_DIM], cos_v, sin_v, swap, 1.0, False)
            v_s[ATTN_PAD:ATTN_PAD + s_len, :] = kv_ref[0, :, LANE + h * HEAD_DIM:LANE + (h + 1) * HEAD_DIM]


def _attn_blocks(s_len):
    out = []
    for i, d in enumerate(DILATIONS):
        nb = s_len // (ATTN_BLOCK * d)
        for r in range(d):
            for n in range(nb):
                start = r + d * ATTN_BLOCK * n
                out.append((i, d, start, ATTN_PAD + start - d * ATTN_BLOCK, n))
    return out


def _rows(start, size, d):
    return pl.ds(start, size, stride=d) if d > 1 else pl.ds(start, size)


def _stack_heads(blk):
    return jnp.concatenate([blk[:, g * HEAD_DIM:(g + 1) * HEAD_DIM] for g in range(GQA)], axis=0)


def _stack_stats(blk):
    return jnp.concatenate([jnp.max(blk[:, g * HEAD_DIM:(g + 1) * HEAD_DIM], axis=1, keepdims=True) for g in range(GQA)], axis=0)


def _attn_in_specs(s_len):
    assert K_COL % (2 * LANE) == 0 and V_COL == K_COL + LANE
    q_spec = pl.BlockSpec((1, s_len, Q_GROUP_W), lambda b, h: (b, 0, Q_COL // Q_GROUP_W + h))
    kv_spec = pl.BlockSpec((1, s_len, 2 * LANE), lambda b, h: (b, 0, K_COL // (2 * LANE)))
    t_spec = pl.BlockSpec((1, s_len, 2 * HEAD_DIM), lambda b, h: (b, 0, 0))
    o_spec = pl.BlockSpec((1, s_len, Q_GROUP_W), lambda b, h: (b, 0, h))
    return q_spec, kv_spec, t_spec, o_spec


def attn_fwd(proj3, rope_tab, *, name):
    b, s_len, _ = proj3.shape
    q_spec, kv_spec, t_spec, o_spec = _attn_in_specs(s_len)
    n_br = len(DILATIONS)

    def body(q_ref, kv_ref, tab_ref, o_ref, lse_ref, q_s, k_s, v_s, *branch_s):
        o_s, l_s = branch_s[:n_br], branch_s[n_br:]
        _attn_prologue(q_ref, kv_ref, tab_ref, q_s, k_s, v_s, pl.program_id(1), s_len)
        for i, d, q0, k0, n in _attn_blocks(s_len):
            qv = _stack_heads(q_s[_rows(q0, ATTN_BLOCK, d), :]).astype(BF16)
            kk = k_s[_rows(k0, 2 * ATTN_BLOCK, d), :].astype(BF16)
            vv = v_s[_rows(k0, 2 * ATTN_BLOCK, d), :].astype(BF16)
            sc = lax.dot_general(qv, kk, (((1,), (1,)), ((), ())), preferred_element_type=F32)
            sc = jnp.where(_attn_mask(n), sc, NEG_BIG)
            m = jnp.max(sc, axis=-1, keepdims=True)
            pr = jnp.exp(sc - m)
            den = jnp.sum(pr, axis=-1, keepdims=True)
            o = jnp.dot(pr.astype(BF16), vv, preferred_element_type=F32) / den
            lse = m + jnp.log(den)
            for g in range(GQA):
                part = slice(g * ATTN_BLOCK, (g + 1) * ATTN_BLOCK)
                o_s[i][_rows(q0, ATTN_BLOCK, d), g * HEAD_DIM:(g + 1) * HEAD_DIM] = o[part]
                l_s[i][_rows(q0, ATTN_BLOCK, d), g * HEAD_DIM:(g + 1) * HEAD_DIM] = jnp.broadcast_to(lse[part], (ATTN_BLOCK, HEAD_DIM))
        step = 256
        for t0 in range(0, s_len, step):
            rs = pl.ds(t0, step)
            for g in range(GQA):
                ls = [l_s[i][rs, g * HEAD_DIM:(g + 1) * HEAD_DIM] for i in range(n_br)]
                m = functools.reduce(jnp.maximum, ls)
                es = [jnp.exp(l - m) for l in ls]
                tot = functools.reduce(lambda a, c: a + c, es)
                inv = 1.0 / tot
                acc = None
                for i in range(n_br):
                    term = (es[i] * inv) * o_s[i][rs, g * HEAD_DIM:(g + 1) * HEAD_DIM]
                    acc = term if acc is None else acc + term
                o_ref[0, rs, g * HEAD_DIM:(g + 1) * HEAD_DIM] = acc
                lse_ref[0, rs, g * HEAD_DIM:(g + 1) * HEAD_DIM] = m + jnp.log(tot)

    return pl.pallas_call(
        body, name=name, grid=(b, N_KV_HEADS), in_specs=[q_spec, kv_spec, t_spec],
        out_specs=[o_spec, o_spec],
        out_shape=[jax.ShapeDtypeStruct((b, s_len, ATTN_WIDTH), F32)] * 2,
        scratch_shapes=[pltpu.VMEM((s_len, Q_GROUP_W), F32), pltpu.VMEM((ATTN_PAD + s_len, HEAD_DIM), F32),
                        pltpu.VMEM((ATTN_PAD + s_len, HEAD_DIM), F32)] + [pltpu.VMEM((s_len, Q_GROUP_W), F32)] * (2 * n_br),
        compiler_params=pltpu.CompilerParams(dimension_semantics=("arbitrary", "arbitrary"), vmem_limit_bytes=ATTN_VMEM_LIMIT),
    )(proj3, proj3, rope_tab)


def attn_bwd(proj3, rope_tab, attn3, lse3, d_attn3, *, name):
    b, s_len, _ = proj3.shape
    q_spec, kv_spec, t_spec, o_spec = _attn_in_specs(s_len)
    kv_out = pl.BlockSpec((1, 1, s_len, HEAD_DIM), lambda bi, h: (bi, h, 0, 0))

    def body(q_ref, kv_ref, tab_ref, o_ref, lse_ref, do_ref, dq_ref, dk_ref, dv_ref,
             q_s, k_s, v_s, dl_s, dq_s, dk_s, dv_s):
        _attn_prologue(q_ref, kv_ref, tab_ref, q_s, k_s, v_s, pl.program_id(1), s_len)
        dq_s[...] = jnp.zeros_like(dq_s)
        dk_s[...] = jnp.zeros_like(dk_s)
        dv_s[...] = jnp.zeros_like(dv_s)
        for g in range(GQA):
            cols = slice(g * HEAD_DIM, (g + 1) * HEAD_DIM)
            delta = jnp.sum(do_ref[0, :, cols] * o_ref[0, :, cols], axis=1, keepdims=True)
            dl_s[:, cols] = jnp.broadcast_to(delta, (s_len, HEAD_DIM))
        for i, d, q0, k0, n in _attn_blocks(s_len):
            qrows, krows = _rows(q0, ATTN_BLOCK, d), _rows(k0, 2 * ATTN_BLOCK, d)
            qv = _stack_heads(q_s[qrows, :]).astype(BF16)
            kk = k_s[krows, :].astype(BF16)
            vv = v_s[krows, :].astype(BF16)
            do16 = _stack_heads(do_ref.at[0][qrows, :]).astype(BF16)
            lse = _stack_stats(lse_ref.at[0][qrows, :])
            delta = _stack_stats(dl_s[qrows, :])
            sc = lax.dot_general(qv, kk, (((1,), (1,)), ((), ())), preferred_element_type=F32)
            pr = jnp.where(_attn_mask(n), jnp.exp(sc - lse), 0.0)
            dv = lax.dot_general(pr.astype(BF16), do16, (((0,), (0,)), ((), ())), preferred_element_type=F32)
            dp = lax.dot_general(do16, vv, (((1,), (1,)), ((), ())), preferred_element_type=F32)
            ds = (pr * (dp - delta)).astype(BF16)
            dq = jnp.dot(ds, kk, preferred_element_type=F32)
            dk = lax.dot_general(ds, qv, (((0,), (0,)), ((), ())), preferred_element_type=F32)
            for g in range(GQA):
                cols = slice(g * HEAD_DIM, (g + 1) * HEAD_DIM)
                dq_s[qrows, cols] += dq[g * ATTN_BLOCK:(g + 1) * ATTN_BLOCK]
            dk_s[krows, :] += dk
            dv_s[krows, :] += dv
        swap = _swap_matrix()
        cos_v, sin_v = tab_ref[0, :, :HEAD_DIM], tab_ref[0, :, HEAD_DIM:]
        for g in range(GQA):
            cols = slice(g * HEAD_DIM, (g + 1) * HEAD_DIM)
            dq_ref[0, :, cols] = _rope(dq_s[:, cols], cos_v, sin_v, swap, HEAD_DIM ** -0.5, True)
        dk_ref[0, 0] = _rope(dk_s[ATTN_PAD:ATTN_PAD + s_len, :], cos_v, sin_v, swap, 1.0, True)
        dv_ref[0, 0] = dv_s[ATTN_PAD:ATTN_PAD + s_len, :]

    kv_shape = jax.ShapeDtypeStruct((b, N_KV_HEADS, s_len, HEAD_DIM), F32)
    return pl.pallas_call(
        body, name=name, grid=(b, N_KV_HEADS),
        in_specs=[q_spec, kv_spec, t_spec, o_spec, o_spec, o_spec],
        out_specs=[o_spec, kv_out, kv_out],
        out_shape=[jax.ShapeDtypeStruct((b, s_len, ATTN_WIDTH), F32), kv_shape, kv_shape],
        scratch_shapes=[pltpu.VMEM((s_len, Q_GROUP_W), F32), pltpu.VMEM((ATTN_PAD + s_len, HEAD_DIM), F32),
                        pltpu.VMEM((ATTN_PAD + s_len, HEAD_DIM), F32), pltpu.VMEM((s_len, Q_GROUP_W), F32),
                        pltpu.VMEM((s_len, Q_GROUP_W), F32), pltpu.VMEM((ATTN_PAD + s_len, HEAD_DIM), F32),
                        pltpu.VMEM((ATTN_PAD + s_len, HEAD_DIM), F32)],
        compiler_params=pltpu.CompilerParams(dimension_semantics=("arbitrary", "arbitrary"), vmem_limit_bytes=ATTN_VMEM_LIMIT),
    )(proj3, proj3, rope_tab, attn3, lse3, d_attn3)


HALF_W = 2 * HEAD_DIM
N_HALF = Q_GROUP_W // HALF_W
_ATTN_BIAS_BUF = pltpu.VMEM((2, GQA * ATTN_BLOCK, 2 * ATTN_BLOCK), F32)


def _attn_bias(bias_s):
    for first in (0, 1):
        bias_s[first] = jnp.where(_attn_mask(first), 0.0, NEG_BIG)


def _attn_prologue(q_refs, kv_ref, tab_ref, q_s, kv_s, hk, s_len):
    swap = _swap_matrix()
    cos_v, sin_v = tab_ref[0, :, :HEAD_DIM], tab_ref[0, :, HEAD_DIM:]
    for j in range(N_HALF):
        for e in range(2):
            cols = slice(e * HEAD_DIM, (e + 1) * HEAD_DIM)
            q_s[j][:, cols] = _rope(q_refs[j][0, :, cols], cos_v, sin_v, swap, HEAD_DIM ** -0.5, False)
    kv_s[0:ATTN_PAD, :] = jnp.zeros((ATTN_PAD, HALF_W), F32)
    for h in range(N_KV_HEADS):
        @pl.when(hk == h)
        def _():
            kv_s[ATTN_PAD:ATTN_PAD + s_len, :HEAD_DIM] = _rope(kv_ref[0, :, h * HEAD_DIM:(h + 1) * HEAD_DIM], cos_v, sin_v,
                                                               swap, 1.0, False)
            kv_s[ATTN_PAD:ATTN_PAD + s_len, HEAD_DIM:] = kv_ref[0, :, LANE + h * HEAD_DIM:LANE + (h + 1) * HEAD_DIM]


def _stack_heads(halves):
    return jnp.concatenate([h[:, e * HEAD_DIM:(e + 1) * HEAD_DIM] for h in halves for e in range(2)], axis=0)


def _unstack_heads(x, j):
    return jnp.concatenate([x[(2 * j + e) * ATTN_BLOCK:(2 * j + e + 1) * ATTN_BLOCK] for e in range(2)], axis=1)


def _stack_stats(halves):
    return jnp.concatenate([jnp.max(h[:, e * HEAD_DIM:(e + 1) * HEAD_DIM], axis=1, keepdims=True)
                            for h in halves for e in range(2)], axis=0)


def _attn_in_specs(s_len):
    assert K_COL % (2 * LANE) == 0 and V_COL == K_COL + LANE

    def halves(first_tile):
        return [pl.BlockSpec((1, s_len, HALF_W), functools.partial(lambda b, h, j: (b, 0, first_tile + N_HALF * h + j), j=j))
                for j in range(N_HALF)]

    kv_spec = pl.BlockSpec((1, s_len, 2 * LANE), lambda b, h: (b, 0, K_COL // (2 * LANE)))
    t_spec = pl.BlockSpec((1, s_len, 2 * HEAD_DIM), lambda b, h: (b, 0, 0))
    o_spec = pl.BlockSpec((1, s_len, Q_GROUP_W), lambda b, h: (b, 0, h))
    return halves(Q_COL // HALF_W), kv_spec, t_spec, o_spec, halves(0)


def attn_fwd(proj3, rope_tab, *, name):
    b, s_len, _ = proj3.shape
    q_specs, kv_spec, t_spec, o_spec, _ = _attn_in_specs(s_len)
    n_br = len(DILATIONS)

    def body(*refs):
        q_refs, (kv_ref, tab_ref, o_ref, lse_ref) = refs[:N_HALF], refs[N_HALF:N_HALF + 4]
        scratch = refs[N_HALF + 4:]
        q_s, kv_s = scratch[:N_HALF], scratch[N_HALF]
        o_s = [scratch[N_HALF + 1 + i * N_HALF:N_HALF + 1 + (i + 1) * N_HALF] for i in range(n_br)]
        l_s = [scratch[N_HALF + 1 + (n_br + i) * N_HALF:N_HALF + 1 + (n_br + i + 1) * N_HALF] for i in range(n_br)]
        bias_s = scratch[-1]
        _attn_prologue(q_refs, kv_ref, tab_ref, q_s, kv_s, pl.program_id(1), s_len)
        _attn_bias(bias_s)
        for i, d, q0, k0, n in _attn_blocks(s_len):
            qrows = _rows(q0, ATTN_BLOCK, d)
            qv = _stack_heads([q_s[j][qrows, :] for j in range(N_HALF)]).astype(BF16)
            kvb = kv_s[_rows(k0, 2 * ATTN_BLOCK, d), :].astype(BF16)
            kk, vv = kvb[:, :HEAD_DIM], kvb[:, HEAD_DIM:]
            sc = lax.dot_general(qv, kk, (((1,), (1,)), ((), ())), preferred_element_type=F32)
            sc = sc + bias_s[min(n, 1)]
            m = jnp.max(sc, axis=-1, keepdims=True)
            pr = jnp.exp(sc - m)
            den = jnp.sum(pr, axis=-1, keepdims=True)
            o = jnp.dot(pr.astype(BF16), vv, preferred_element_type=F32) / den
            lse_b = jnp.broadcast_to(m + jnp.log(den), (GQA * ATTN_BLOCK, HEAD_DIM))
            for j in range(N_HALF):
                o_s[i][j][qrows, :] = _unstack_heads(o, j)
                l_s[i][j][qrows, :] = _unstack_heads(lse_b, j)
        step = 256
        for t0 in range(0, s_len, step):
            rs = pl.ds(t0, step)
            for j in range(N_HALF):
                ls = [l_s[i][j][rs, :] for i in range(n_br)]
                m = functools.reduce(jnp.maximum, ls)
                es = [jnp.exp(l - m) for l in ls]
                tot = functools.reduce(lambda a, c: a + c, es)
                inv = 1.0 / tot
                acc = None
                for i in range(n_br):
                    term = (es[i] * inv) * o_s[i][j][rs, :]
                    acc = term if acc is None else acc + term
                o_ref[0, rs, j * HALF_W:(j + 1) * HALF_W] = acc
                lse_ref[0, rs, j * HALF_W:(j + 1) * HALF_W] = m + jnp.log(tot)

    half_buf = pltpu.VMEM((s_len, HALF_W), F32)
    return pl.pallas_call(
        body, name=name, grid=(b, N_KV_HEADS), in_specs=q_specs + [kv_spec, t_spec],
        out_specs=[o_spec, o_spec],
        out_shape=[jax.ShapeDtypeStruct((b, s_len, ATTN_WIDTH), F32)] * 2,
        scratch_shapes=[half_buf] * N_HALF + [pltpu.VMEM((ATTN_PAD + s_len, HALF_W), F32)] + [half_buf] * (2 * n_br * N_HALF)
        + [_ATTN_BIAS_BUF],
        compiler_params=pltpu.CompilerParams(dimension_semantics=("arbitrary", "arbitrary"), vmem_limit_bytes=ATTN_VMEM_LIMIT),
    )(*([proj3] * (N_HALF + 1)), rope_tab)


def attn_bwd(proj3, rope_tab, attn3, lse3, d_attn3, *, name):
    b, s_len, _ = proj3.shape
    q_specs, kv_spec, t_spec, o_spec, half_specs = _attn_in_specs(s_len)
    kv_out = pl.BlockSpec((1, 1, s_len, HEAD_DIM), lambda bi, h: (bi, h, 0, 0))

    def body(*refs):
        q_refs = refs[:N_HALF]
        kv_ref, tab_ref, o_ref = refs[N_HALF:N_HALF + 3]
        lse_refs = refs[N_HALF + 3:2 * N_HALF + 3]
        do_refs = refs[2 * N_HALF + 3:3 * N_HALF + 3]
        dq_ref, dk_ref, dv_ref = refs[3 * N_HALF + 3:3 * N_HALF + 6]
        scratch = refs[3 * N_HALF + 6:]
        q_s, kv_s = scratch[:N_HALF], scratch[N_HALF]
        dl_s = scratch[N_HALF + 1:2 * N_HALF + 1]
        dq_s = scratch[2 * N_HALF + 1:3 * N_HALF + 1]
        dkv_s = scratch[3 * N_HALF + 1]
        bias_s = scratch[-1]
        _attn_prologue(q_refs, kv_ref, tab_ref, q_s, kv_s, pl.program_id(1), s_len)
        _attn_bias(bias_s)
        dkv_s[...] = jnp.zeros_like(dkv_s)
        for j in range(N_HALF):
            dq_s[j][...] = jnp.zeros_like(dq_s[j])
            for e in range(2):
                cols = slice(e * HEAD_DIM, (e + 1) * HEAD_DIM)
                ocols = slice(j * HALF_W + e * HEAD_DIM, j * HALF_W + (e + 1) * HEAD_DIM)
                delta = jnp.sum(do_refs[j][0, :, cols] * o_ref[0, :, ocols], axis=1, keepdims=True)
                dl_s[j][:, cols] = jnp.broadcast_to(delta, (s_len, HEAD_DIM))
        for i, d, q0, k0, n in _attn_blocks(s_len):
            qrows, krows = _rows(q0, ATTN_BLOCK, d), _rows(k0, 2 * ATTN_BLOCK, d)
            qv = _stack_heads([q_s[j][qrows, :] for j in range(N_HALF)]).astype(BF16)
            kvb = kv_s[krows, :].astype(BF16)
            kk, vv = kvb[:, :HEAD_DIM], kvb[:, HEAD_DIM:]
            do16 = _stack_heads([do_refs[j].at[0][qrows, :] for j in range(N_HALF)]).astype(BF16)
            lse = _stack_stats([lse_refs[j].at[0][qrows, :] for j in range(N_HALF)])
            delta = _stack_stats([dl_s[j][qrows, :] for j in range(N_HALF)])
            sc = lax.dot_general(qv, kk, (((1,), (1,)), ((), ())), preferred_element_type=F32)
            pr = jnp.exp(sc + bias_s[min(n, 1)] - lse)
            dv = lax.dot_general(pr.astype(BF16), do16, (((0,), (0,)), ((), ())), preferred_element_type=F32)
            dp = lax.dot_general(do16, vv, (((1,), (1,)), ((), ())), preferred_element_type=F32)
            ds = (pr * (dp - delta)).astype(BF16)
            dq = jnp.dot(ds, kk, preferred_element_type=F32)
            dk = lax.dot_general(ds, qv, (((0,), (0,)), ((), ())), preferred_element_type=F32)
            for j in range(N_HALF):
                dq_s[j][qrows, :] += _unstack_heads(dq, j)
            dkv_s[krows, :] += jnp.concatenate([dk, dv], axis=1)
        swap = _swap_matrix()
        cos_v, sin_v = tab_ref[0, :, :HEAD_DIM], tab_ref[0, :, HEAD_DIM:]
        for j in range(N_HALF):
            for e in range(2):
                cols = slice(e * HEAD_DIM, (e + 1) * HEAD_DIM)
                ocols = slice(j * HALF_W + e * HEAD_DIM, j * HALF_W + (e + 1) * HEAD_DIM)
                dq_ref[0, :, ocols] = _rope(dq_s[j][:, cols], cos_v, sin_v, swap, HEAD_DIM ** -0.5, True)
        dk_ref[0, 0] = _rope(dkv_s[ATTN_PAD:ATTN_PAD + s_len, :HEAD_DIM], cos_v, sin_v, swap, 1.0, True)
        dv_ref[0, 0] = dkv_s[ATTN_PAD:ATTN_PAD + s_len, HEAD_DIM:]

    kv_shape = jax.ShapeDtypeStruct((b, N_KV_HEADS, s_len, HEAD_DIM), F32)
    half_buf = pltpu.VMEM((s_len, HALF_W), F32)
    pad_buf = pltpu.VMEM((ATTN_PAD + s_len, HALF_W), F32)
    return pl.pallas_call(
        body, name=name, grid=(b, N_KV_HEADS),
        in_specs=q_specs + [kv_spec, t_spec, o_spec] + half_specs + half_specs,
        out_specs=[o_spec, kv_out, kv_out],
        out_shape=[jax.ShapeDtypeStruct((b, s_len, ATTN_WIDTH), F32), kv_shape, kv_shape],
        scratch_shapes=[half_buf] * N_HALF + [pad_buf] + [half_buf] * (2 * N_HALF) + [pad_buf, _ATTN_BIAS_BUF],
        compiler_params=pltpu.CompilerParams(dimension_semantics=("arbitrary", "arbitrary"), vmem_limit_bytes=ATTN_VMEM_LIMIT),
    )(*([proj3] * (N_HALF + 1)), rope_tab, attn3, *([lse3] * N_HALF), *([d_attn3] * N_HALF))


CONV_TC = 256
CONV_COL0 = XBC_COL // CONV_TC


def _shift_down(u, s):
    if s == 0:
        return u
    rows = lax.broadcasted_iota(jnp.int32, u.shape, 0)
    return jnp.where(rows >= s, pltpu.roll(u, s, 0), 0.0)


def _shift_up(u, s):
    if s == 0:
        return u
    n = u.shape[0]
    rows = lax.broadcasted_iota(jnp.int32, u.shape, 0)
    return jnp.where(rows < n - s, pltpu.roll(u, n - s, 0), 0.0)


def conv_silu_fwd(proj3, w, bias, *, name):
    b, s, _ = proj3.shape
    u_spec = pl.BlockSpec((1, s, CONV_TC), lambda j, bi: (bi, 0, CONV_COL0 + j))
    o_spec = pl.BlockSpec((1, s, CONV_TC), lambda j, bi: (bi, 0, j))
    w_spec = pl.BlockSpec((CONV_WIDTH, CONV_TC), lambda j, bi: (0, j))
    b_spec = pl.BlockSpec((1, CONV_TC), lambda j, bi: (0, j))

    def body(u_ref, w_ref, b_ref, o_ref):
        u = u_ref[0]
        y = jnp.broadcast_to(b_ref[...], u.shape)
        for k in range(CONV_WIDTH):
            y = y + w_ref[k:k + 1, :] * _shift_down(u, CONV_WIDTH - 1 - k)
        o_ref[0] = y * jax.nn.sigmoid(y)

    return pl.pallas_call(
        body, name=name, grid=(CONV_CH // CONV_TC, b), in_specs=[u_spec, w_spec, b_spec], out_specs=o_spec,
        out_shape=jax.ShapeDtypeStruct((b, s, CONV_CH), F32),
        compiler_params=_params(("parallel", "arbitrary")),
    )(proj3, w, bias)


def conv_silu_bwd(proj3, w, bias, dact, *, name):
    b, s, _ = proj3.shape
    u_spec = pl.BlockSpec((1, s, CONV_TC), lambda j, bi: (bi, 0, CONV_COL0 + j))
    o_spec = pl.BlockSpec((1, s, CONV_TC), lambda j, bi: (bi, 0, j))
    w_spec = pl.BlockSpec((CONV_WIDTH, CONV_TC), lambda j, bi: (0, j))
    b_spec = pl.BlockSpec((1, CONV_TC), lambda j, bi: (0, j))

    def body(u_ref, w_ref, b_ref, g_ref, du_ref, dw_ref, db_ref):
        bi = pl.program_id(1)

        @pl.when(bi == 0)
        def _():
            dw_ref[...] = jnp.zeros_like(dw_ref)
            db_ref[...] = jnp.zeros_like(db_ref)

        u = u_ref[0]
        y = jnp.broadcast_to(b_ref[...], u.shape)
        shifted = [_shift_down(u, CONV_WIDTH - 1 - k) for k in range(CONV_WIDTH)]
        for k in range(CONV_WIDTH):
            y = y + w_ref[k:k + 1, :] * shifted[k]
        sig = jax.nn.sigmoid(y)
        dy = g_ref[0] * (sig * (1.0 + y * (1.0 - sig)))
        du = jnp.zeros_like(u)
        for k in range(CONV_WIDTH):
            du = du + w_ref[k:k + 1, :] * _shift_up(dy, CONV_WIDTH - 1 - k)
            dw_ref[k:k + 1, :] += jnp.sum(dy * shifted[k], axis=0, keepdims=True)
        du_ref[0] = du
        db_ref[...] += jnp.sum(dy, axis=0, keepdims=True)

    return pl.pallas_call(
        body, name=name, grid=(CONV_CH // CONV_TC, b), in_specs=[u_spec, w_spec, b_spec, o_spec],
        out_specs=[o_spec, w_spec, b_spec],
        out_shape=[jax.ShapeDtypeStruct((b, s, CONV_CH), F32), jax.ShapeDtypeStruct((CONV_WIDTH, CONV_CH), F32),
                   jax.ShapeDtypeStruct((1, CONV_CH), F32)],
        compiler_params=_params(("parallel", "arbitrary")),
    )(proj3, w, bias, dact)


def _softplus(z):
    e = jnp.exp(-jnp.abs(z))
    u = 1.0 + e
    log1p = jnp.where(u == 1.0, e, jnp.log(u) * e / jnp.where(u == 1.0, 1.0, u - 1.0))
    return jnp.maximum(z, 0.0) + log1p


def _tri(lower):
    r = lax.broadcasted_iota(jnp.int32, (CHUNK, CHUNK), 0)
    c = lax.broadcasted_iota(jnp.int32, (CHUNK, CHUNK), 1)
    return (r >= c) if lower else (r <= c)


def _ssd_common(dtr_ref, dtb_ref, alog_ref):
    z = dtr_ref[0] + dtb_ref[...]
    dt = _softplus(z)
    aneg = -jnp.exp(alog_ref[...])
    acs = _dot01_left(_tri(True).astype(BF16), dt * aneg)
    return z, dt, aneg, acs


def _col(mat, onehot):
    return jnp.sum(mat * onehot, axis=1, keepdims=True)


def _ssd_head(x, dt_j, acs_j, cb, tri_mask, last_row, acs_row=None):
    acs_last = jnp.sum(acs_j * last_row, axis=0, keepdims=True)
    xg = x * dt_j
    bc = jnp.broadcast_to(acs_j, (CHUNK, CHUNK))
    dm = bc - (bc.T if acs_row is None else jnp.broadcast_to(acs_row, (CHUNK, CHUNK)))
    lm = jnp.where(tri_mask, jnp.exp(jnp.where(tri_mask, dm, 0.0)), 0.0)
    mm = cb * lm
    decay_s = jnp.exp(acs_last - acs_j)
    return acs_last, xg, lm, mm, decay_s


def _ssd_specs(nc, reverse):
    cidx = (lambda c: nc - 1 - c) if reverse else (lambda c: c)
    act_spec = pl.BlockSpec((1, CHUNK, CONV_CH), lambda b, c: (b, cidx(c), 0))
    y_spec = pl.BlockSpec((1, CHUNK, SSM_INNER), lambda b, c: (b, cidx(c), 0))
    dt_in_spec = pl.BlockSpec((1, CHUNK, LANE), lambda b, c: (b, cidx(c), DT_COL // LANE))
    dt_out_spec = pl.BlockSpec((1, CHUNK, LANE), lambda b, c: (b, cidx(c), 0))
    par_spec = pl.BlockSpec((1, LANE), lambda b, c: (0, 0))
    h_spec = pl.BlockSpec((1, SSM_HEADS, 1, SSM_P, D_STATE), lambda b, c: (b, 0, cidx(c), 0, 0))
    return act_spec, y_spec, dt_in_spec, dt_out_spec, par_spec, h_spec


def _head_cols(h):
    return slice(h * SSM_P, (h + 1) * SSM_P)


def _group_cols(g, which):
    start = SSM_INNER + which * SSM_GROUPS * D_STATE + g * D_STATE
    return slice(start, start + D_STATE)


def ssd_fwd(act3, proj3, dtb, alog, dsk, *, name):
    b, s, _ = act3.shape
    nc = s // CHUNK
    act_spec, y_spec, dt_in_spec, _, par_spec, h_spec = _ssd_specs(nc, False)

    def body(act_ref, dtr_ref, dtb_ref, alog_ref, dsk_ref, y_ref, hp_ref, state):
        c = pl.program_id(1)

        @pl.when(c == 0)
        def _():
            state[...] = jnp.zeros_like(state)

        _, dt, _, acs = _ssd_common(dtr_ref, dtb_ref, alog_ref)
        acs_t = acs.T
        tri_mask = _tri(True)
        last_row = (lax.broadcasted_iota(jnp.int32, (CHUNK, 1), 0) == CHUNK - 1).astype(F32)
        for g in range(SSM_GROUPS):
            b16 = act_ref[0, :, _group_cols(g, 0)].astype(BF16)
            c16 = act_ref[0, :, _group_cols(g, 1)].astype(BF16)
            cb = lax.dot_general(c16, b16, (((1,), (1,)), ((), ())), preferred_element_type=F32)
            for j in range(HEADS_PER_GROUP):
                hidx = g * HEADS_PER_GROUP + j
                x = act_ref[0, :, _head_cols(hidx)]
                dt_j, acs_j = dt[:, hidx:hidx + 1], acs[:, hidx:hidx + 1]
                acs_last, xg, _, mm, decay_s = _ssd_head(x, dt_j, acs_j, cb, tri_mask, last_row, acs_t[hidx:hidx + 1, :])
                y_diag = jnp.dot(mm.astype(BF16), xg.astype(BF16), preferred_element_type=F32)
                st = lax.dot_general((xg * decay_s).astype(BF16), b16, (((0,), (0,)), ((), ())), preferred_element_type=F32)
                hp = state[hidx]
                hp_ref[0, hidx, 0] = hp
                y_off = lax.dot_general(c16, hp.astype(BF16), (((1,), (1,)), ((), ())), preferred_element_type=F32)
                d_j = dsk_ref[:, hidx:hidx + 1]
                y_ref[0, :, _head_cols(hidx)] = y_diag + y_off * jnp.exp(acs_j) + d_j * x
                state[hidx] = hp * jnp.exp(acs_last) + st

    return pl.pallas_call(
        body, name=name, grid=(b, nc),
        in_specs=[act_spec, dt_in_spec, par_spec, par_spec, par_spec],
        out_specs=[y_spec, h_spec],
        out_shape=[jax.ShapeDtypeStruct((b, s, SSM_INNER), F32),
                   jax.ShapeDtypeStruct((b, SSM_HEADS, nc, SSM_P, D_STATE), F32)],
        scratch_shapes=[pltpu.VMEM((SSM_HEADS, SSM_P, D_STATE), F32)],
        compiler_params=_params(("arbitrary", "arbitrary")),
    )(act3, proj3, dtb, alog, dsk)


def ssd_bwd(act3, proj3, dtb, alog, dsk, hprev, dy3, *, name):
    b, s, _ = act3.shape
    nc = s // CHUNK
    act_spec, y_spec, dt_in_spec, dt_out_spec, par_spec, h_spec = _ssd_specs(nc, True)
    dpar_spec = pl.BlockSpec((8, LANE), lambda bi, c: (0, 0))

    def body(act_ref, dtr_ref, dtb_ref, alog_ref, dsk_ref, hp_ref, dy_ref, dact_ref, ddtr_ref, dpar_ref, dstate):
        bi, c = pl.program_id(0), pl.program_id(1)

        @pl.when(c == 0)
        def _():
            dstate[...] = jnp.zeros_like(dstate)

        @pl.when((bi == 0) & (c == 0))
        def _():
            dpar_ref[...] = jnp.zeros_like(dpar_ref)

        z, dt, aneg, acs = _ssd_common(dtr_ref, dtb_ref, alog_ref)
        acs_t = acs.T
        tri_mask = _tri(True)
        last_row = (lax.broadcasted_iota(jnp.int32, (CHUNK, 1), 0) == CHUNK - 1).astype(F32)
        lanes = lax.broadcasted_iota(jnp.int32, (1, LANE), 1)
        sublanes = lax.broadcasted_iota(jnp.int32, (LANE, 1), 0)
        ddt_mat = jnp.zeros((CHUNK, LANE), F32)
        dacs_mat = jnp.zeros((CHUNK, LANE), F32)
        dacs_rows = jnp.zeros((LANE, CHUNK), F32)
        ddsk_row = jnp.zeros((1, LANE), F32)
        for g in range(SSM_GROUPS):
            b16 = act_ref[0, :, _group_cols(g, 0)].astype(BF16)
            c16 = act_ref[0, :, _group_cols(g, 1)].astype(BF16)
            cb = lax.dot_general(c16, b16, (((1,), (1,)), ((), ())), preferred_element_type=F32)
            dcb = jnp.zeros((CHUNK, CHUNK), F32)
            db_acc = jnp.zeros((CHUNK, D_STATE), F32)
            dc_acc = jnp.zeros((CHUNK, D_STATE), F32)
            for j in range(HEADS_PER_GROUP):
                hidx = g * HEADS_PER_GROUP + j
                onehot = (lanes == hidx).astype(F32)
                x = act_ref[0, :, _head_cols(hidx)]
                dt_j, acs_j = dt[:, hidx:hidx + 1], acs[:, hidx:hidx + 1]
                acs_last, xg, lm, mm, decay_s = _ssd_head(x, dt_j, acs_j, cb, tri_mask, last_row, acs_t[hidx:hidx + 1, :])
                ea = jnp.exp(acs_j)
                cd = jnp.exp(acs_last)
                d_j = dsk_ref[:, hidx:hidx + 1]
                hp = hp_ref[0, hidx, 0]
                hp16 = hp.astype(BF16)
                g_y = dy_ref[0, :, _head_cols(hidx)]
                g_y16 = g_y.astype(BF16)
                g_hn = dstate[hidx]
                g_hn16 = g_hn.astype(BF16)
                xg16 = xg.astype(BF16)
                ddsk_row = ddsk_row + jnp.sum(jnp.sum(g_y * x, axis=1, keepdims=True), axis=0, keepdims=True) * onehot
                d_mm = lax.dot_general(g_y16, xg16, (((1,), (1,)), ((), ())), preferred_element_type=F32)
                d_xg = lax.dot_general(mm.astype(BF16), g_y16, (((0,), (0,)), ((), ())), preferred_element_type=F32)
                dcb = dcb + d_mm * lm
                d_dm = d_mm * mm
                d_acs = jnp.sum(d_dm, axis=1, keepdims=True)
                dacs_rows = dacs_rows + (sublanes == hidx).astype(F32) * jnp.sum(d_dm, axis=0, keepdims=True)
                t_off = lax.dot_general(c16, hp16, (((1,), (1,)), ((), ())), preferred_element_type=F32)
                d_t16 = (g_y * ea).astype(BF16)
                d_acs = d_acs + jnp.sum(g_y * t_off, axis=1, keepdims=True) * ea
                dc_acc = dc_acc + jnp.dot(d_t16, hp16, preferred_element_type=F32)
                d_hp = lax.dot_general(d_t16, c16, (((0,), (0,)), ((), ())), preferred_element_type=F32) + g_hn * cd
                d_last = jnp.sum(jnp.sum(g_hn * hp, axis=1, keepdims=True), axis=0, keepdims=True) * cd
                d_w = lax.dot_general(b16, g_hn16, (((1,), (1,)), ((), ())), preferred_element_type=F32)
                db_acc = db_acc + jnp.dot((xg * decay_s).astype(BF16), g_hn16, preferred_element_type=F32)
                d_xg = d_xg + d_w * decay_s
                d_ds = jnp.sum(d_w * xg, axis=1, keepdims=True) * decay_s
                d_last = d_last + jnp.sum(d_ds, axis=0, keepdims=True)
                d_acs = d_acs - d_ds + d_last * last_row
                dact_ref[0, :, _head_cols(hidx)] = d_j * g_y + d_xg * dt_j
                ddt_mat = ddt_mat + jnp.sum(d_xg * x, axis=1, keepdims=True) * onehot
                dacs_mat = dacs_mat + d_acs * onehot
                dstate[hidx] = d_hp
            dcb16 = dcb.astype(BF16)
            dact_ref[0, :, _group_cols(g, 1)] = dc_acc + jnp.dot(dcb16, b16, preferred_element_type=F32)
            dact_ref[0, :, _group_cols(g, 0)] = db_acc + lax.dot_general(dcb16, c16, (((0,), (0,)), ((), ())),
                                                                         preferred_element_type=F32)
        d_a = _dot01_left(_tri(False).astype(BF16), dacs_mat - dacs_rows.T)
        ddt_mat = ddt_mat + d_a * aneg
        d_raw = ddt_mat * jax.nn.sigmoid(z)
        ddtr_ref[0] = d_raw
        dpar_ref[0:1, :] += jnp.sum(d_raw, axis=0, keepdims=True)
        dpar_ref[1:2, :] += jnp.sum(d_a * dt, axis=0, keepdims=True) * aneg
        dpar_ref[2:3, :] += ddsk_row

    return pl.pallas_call(
        body, name=name, grid=(b, nc),
        in_specs=[act_spec, dt_in_spec, par_spec, par_spec, par_spec, h_spec, y_spec],
        out_specs=[act_spec, dt_out_spec, dpar_spec],
        out_shape=[jax.ShapeDtypeStruct(act3.shape, F32), jax.ShapeDtypeStruct((b, s, LANE), F32),
                   jax.ShapeDtypeStruct((8, LANE), F32)],
        scratch_shapes=[pltpu.VMEM((SSM_HEADS, SSM_P, D_STATE), F32)],
        compiler_params=_params(("arbitrary", "arbitrary")),
    )(act3, proj3, dtb, alog, dsk, hprev, dy3)


def _unused_ssd_specs(nc, reverse):
    cidx = (lambda c: nc - 1 - c) if reverse else (lambda c: c)
    x_spec = pl.BlockSpec((1, HEADS_PER_GROUP, CHUNK, SSM_P), lambda b, c, g: (b, g, cidx(c), 0))
    bc_spec = pl.BlockSpec((1, 1, CHUNK, D_STATE), lambda b, c, g: (b, g, cidx(c), 0))
    dt_spec = pl.BlockSpec((1, CHUNK, LANE), lambda b, c, g: (b, cidx(c), 0))
    par_spec = pl.BlockSpec((1, LANE), lambda b, c, g: (0, 0))
    h_spec = pl.BlockSpec((1, HEADS_PER_GROUP, 1, SSM_P, D_STATE), lambda b, c, g: (b, g, cidx(c), 0, 0))
    return x_spec, bc_spec, dt_spec, par_spec, h_spec


def _unused_ssd_fwd(xs, bm, cm, dtr, dtb, alog, dsk, *, name):
    b, _, s, _ = xs.shape
    nc = s // CHUNK
    x_spec, bc_spec, dt_spec, par_spec, h_spec = _ssd_specs(nc, False)

    def body(x_ref, b_ref, c_ref, dtr_ref, dtb_ref, alog_ref, dsk_ref, y_ref, hp_ref, state):
        c, g = pl.program_id(1), pl.program_id(2)

        @pl.when(c == 0)
        def _():
            state[pl.ds(g * HEADS_PER_GROUP, HEADS_PER_GROUP)] = jnp.zeros((HEADS_PER_GROUP, SSM_P, D_STATE), F32)

        _, dt, _, acs = _ssd_common(dtr_ref, dtb_ref, alog_ref)
        b16, c16 = b_ref[0, 0].astype(BF16), c_ref[0, 0].astype(BF16)
        cb = lax.dot_general(c16, b16, (((1,), (1,)), ((), ())), preferred_element_type=F32)
        tri_mask = _tri(True)
        last_row = (lax.broadcasted_iota(jnp.int32, (CHUNK, 1), 0) == CHUNK - 1).astype(F32)
        lanes = lax.broadcasted_iota(jnp.int32, (1, LANE), 1)
        for j in range(HEADS_PER_GROUP):
            hidx = g * HEADS_PER_GROUP + j
            onehot = (lanes == hidx).astype(F32)
            x = x_ref[0, j]
            dt_j, acs_j = _col(dt, onehot), _col(acs, onehot)
            acs_last, xg, _, mm, decay_s = _ssd_head(x, dt_j, acs_j, cb, tri_mask, last_row)
            xg16 = xg.astype(BF16)
            y_diag = jnp.dot(mm.astype(BF16), xg16, preferred_element_type=F32)
            st = lax.dot_general((xg * decay_s).astype(BF16), b16, (((0,), (0,)), ((), ())), preferred_element_type=F32)
            hp = state[hidx]
            hp_ref[0, j, 0] = hp
            y_off = lax.dot_general(c16, hp.astype(BF16), (((1,), (1,)), ((), ())), preferred_element_type=F32)
            d_j = jnp.sum(dsk_ref[...] * onehot, axis=1, keepdims=True)
            y_ref[0, j] = y_diag + y_off * jnp.exp(acs_j) + d_j * x
            state[hidx] = hp * jnp.exp(acs_last) + st

    return pl.pallas_call(
        body, name=name, grid=(b, nc, SSM_GROUPS),
        in_specs=[x_spec, bc_spec, bc_spec, dt_spec, par_spec, par_spec, par_spec],
        out_specs=[x_spec, h_spec],
        out_shape=[jax.ShapeDtypeStruct(xs.shape, F32),
                   jax.ShapeDtypeStruct((b, SSM_HEADS, nc, SSM_P, D_STATE), F32)],
        scratch_shapes=[pltpu.VMEM((SSM_HEADS, SSM_P, D_STATE), F32)],
        compiler_params=_params(("arbitrary", "arbitrary", "arbitrary")),
    )(xs, bm, cm, dtr, dtb, alog, dsk)


def _unused_ssd_bwd(xs, bm, cm, dtr, dtb, alog, dsk, hprev, dy, *, name):
    b, _, s, _ = xs.shape
    nc = s // CHUNK
    x_spec, bc_spec, dt_spec, par_spec, h_spec = _ssd_specs(nc, True)
    dpar_spec = pl.BlockSpec((8, LANE), lambda bi, c, g: (0, 0))

    def body(x_ref, b_ref, c_ref, dtr_ref, dtb_ref, alog_ref, dsk_ref, hp_ref, dy_ref,
             dx_ref, db_ref, dc_ref, ddtr_ref, dpar_ref, dstate):
        bi, c, g = pl.program_id(0), pl.program_id(1), pl.program_id(2)

        @pl.when(c == 0)
        def _():
            dstate[pl.ds(g * HEADS_PER_GROUP, HEADS_PER_GROUP)] = jnp.zeros((HEADS_PER_GROUP, SSM_P, D_STATE), F32)

        @pl.when((bi == 0) & (c == 0) & (g == 0))
        def _():
            dpar_ref[...] = jnp.zeros_like(dpar_ref)

        z, dt, aneg, acs = _ssd_common(dtr_ref, dtb_ref, alog_ref)
        bv, cv = b_ref[0, 0], c_ref[0, 0]
        b16, c16 = bv.astype(BF16), cv.astype(BF16)
        cb = lax.dot_general(c16, b16, (((1,), (1,)), ((), ())), preferred_element_type=F32)
        tri_mask = _tri(True)
        last_row = (lax.broadcasted_iota(jnp.int32, (CHUNK, 1), 0) == CHUNK - 1).astype(F32)
        lanes = lax.broadcasted_iota(jnp.int32, (1, LANE), 1)
        dcb = jnp.zeros((CHUNK, CHUNK), F32)
        db_acc = jnp.zeros((CHUNK, D_STATE), F32)
        dc_acc = jnp.zeros((CHUNK, D_STATE), F32)
        ddt_mat = jnp.zeros((CHUNK, LANE), F32)
        dacs_mat = jnp.zeros((CHUNK, LANE), F32)
        ddsk_row = jnp.zeros((1, LANE), F32)
        for j in range(HEADS_PER_GROUP):
            hidx = g * HEADS_PER_GROUP + j
            onehot = (lanes == hidx).astype(F32)
            x = x_ref[0, j]
            dt_j, acs_j = _col(dt, onehot), _col(acs, onehot)
            acs_last, xg, lm, mm, decay_s = _ssd_head(x, dt_j, acs_j, cb, tri_mask, last_row)
            ea = jnp.exp(acs_j)
            cd = jnp.exp(acs_last)
            d_j = jnp.sum(dsk_ref[...] * onehot, axis=1, keepdims=True)
            hp = hp_ref[0, j, 0]
            hp16 = hp.astype(BF16)
            g_y = dy_ref[0, j]
            g_y16 = g_y.astype(BF16)
            g_hn = dstate[hidx]
            g_hn16 = g_hn.astype(BF16)
            xg16 = xg.astype(BF16)
            ddsk_row = ddsk_row + jnp.sum(jnp.sum(g_y * x, axis=1, keepdims=True), axis=0, keepdims=True) * onehot
            d_mm = lax.dot_general(g_y16, xg16, (((1,), (1,)), ((), ())), preferred_element_type=F32)
            d_xg = lax.dot_general(mm.astype(BF16), g_y16, (((0,), (0,)), ((), ())), preferred_element_type=F32)
            dcb = dcb + d_mm * lm
            d_dm = d_mm * mm
            d_acs = jnp.sum(d_dm, axis=1, keepdims=True) - jnp.sum(d_dm.T, axis=1, keepdims=True)
            t_off = lax.dot_general(c16, hp16, (((1,), (1,)), ((), ())), preferred_element_type=F32)
            d_t16 = (g_y * ea).astype(BF16)
            d_acs = d_acs + jnp.sum(g_y * t_off, axis=1, keepdims=True) * ea
            dc_acc = dc_acc + jnp.dot(d_t16, hp16, preferred_element_type=F32)
            d_hp = lax.dot_general(d_t16, c16, (((0,), (0,)), ((), ())), preferred_element_type=F32) + g_hn * cd
            d_last = jnp.sum(jnp.sum(g_hn * hp, axis=1, keepdims=True), axis=0, keepdims=True) * cd
            d_w = lax.dot_general(b16, g_hn16, (((1,), (1,)), ((), ())), preferred_element_type=F32)
            db_acc = db_acc + jnp.dot((xg * decay_s).astype(BF16), g_hn16, preferred_element_type=F32)
            d_xg = d_xg + d_w * decay_s
            d_ds = jnp.sum(d_w * xg, axis=1, keepdims=True) * decay_s
            d_last = d_last + jnp.sum(d_ds, axis=0, keepdims=True)
            d_acs = d_acs - d_ds + d_last * last_row
            dx_ref[0, j] = d_j * g_y + d_xg * dt_j
            ddt_mat = ddt_mat + jnp.sum(d_xg * x, axis=1, keepdims=True) * onehot
            dacs_mat = dacs_mat + d_acs * onehot
            dstate[hidx] = d_hp
        dcb16 = dcb.astype(BF16)
        dc_ref[0, 0] = dc_acc + jnp.dot(dcb16, b16, preferred_element_type=F32)
        db_ref[0, 0] = db_acc + lax.dot_general(dcb16, c16, (((0,), (0,)), ((), ())), preferred_element_type=F32)
        d_a = _dot01_left(_tri(False).astype(BF16), dacs_mat)
        ddt_mat = ddt_mat + d_a * aneg
        d_aneg = jnp.sum(d_a * dt, axis=0, keepdims=True)
        d_raw = ddt_mat * jax.nn.sigmoid(z)

        @pl.when(g == 0)
        def _():
            ddtr_ref[0] = d_raw

        @pl.when(g != 0)
        def _():
            ddtr_ref[0] += d_raw

        dpar_ref[0:1, :] += jnp.sum(d_raw, axis=0, keepdims=True)
        dpar_ref[1:2, :] += d_aneg * aneg
        dpar_ref[2:3, :] += ddsk_row

    return pl.pallas_call(
        body, name=name, grid=(b, nc, SSM_GROUPS),
        in_specs=[x_spec, bc_spec, bc_spec, dt_spec, par_spec, par_spec, par_spec, h_spec, x_spec],
        out_specs=[x_spec, bc_spec, bc_spec, dt_spec, dpar_spec],
        out_shape=[jax.ShapeDtypeStruct(xs.shape, F32), jax.ShapeDtypeStruct(bm.shape, F32),
                   jax.ShapeDtypeStruct(cm.shape, F32), jax.ShapeDtypeStruct(dtr.shape, F32),
                   jax.ShapeDtypeStruct((8, LANE), F32)],
        scratch_shapes=[pltpu.VMEM((SSM_HEADS, SSM_P, D_STATE), F32)],
        compiler_params=_params(("arbitrary", "arbitrary", "arbitrary")),
    )(xs, bm, cm, dtr, dtb, alog, dsk, hprev, dy)


SSD_INTERLEAVE = 8


def _each(f, *lists):
    return [f(*a) for a in zip(*lists)]


def _nt(a, b):
    return lax.dot_general(a, b, (((1,), (1,)), ((), ())), preferred_element_type=F32)


def _tn(a, b):
    return lax.dot_general(a, b, (((0,), (0,)), ((), ())), preferred_element_type=F32)


def _nn(a, b):
    return jnp.dot(a, b, preferred_element_type=F32)


def _rowsum(a):
    return jnp.sum(a, axis=1, keepdims=True)


def _colsum(a):
    return jnp.sum(a, axis=0, keepdims=True)


def _bf(a):
    return a.astype(BF16)


def _head_batches(g):
    first = g * HEADS_PER_GROUP
    return [list(range(first + k, first + k + SSD_INTERLEAVE)) for k in range(0, HEADS_PER_GROUP, SSD_INTERLEAVE)]


def _decay_matrix(acs_j, acs_row, tri_mask):
    dm = jnp.broadcast_to(acs_j, (CHUNK, CHUNK)) - jnp.broadcast_to(acs_row, (CHUNK, CHUNK))
    return jnp.where(tri_mask, jnp.exp(jnp.where(tri_mask, dm, 0.0)), 0.0)


def ssd_fwd(act3, proj3, dtb, alog, dsk, *, name):
    b, s, _ = act3.shape
    nc = s // CHUNK
    act_spec, y_spec, dt_in_spec, _, par_spec, h_spec = _ssd_specs(nc, False)

    def body(act_ref, dtr_ref, dtb_ref, alog_ref, dsk_ref, y_ref, hp_ref, state):
        c = pl.program_id(1)

        @pl.when(c == 0)
        def _():
            state[...] = jnp.zeros_like(state)

        _, dt, _, acs = _ssd_common(dtr_ref, dtb_ref, alog_ref)
        acs_t = acs.T
        tri_mask = _tri(True)
        last_row = (lax.broadcasted_iota(jnp.int32, (CHUNK, 1), 0) == CHUNK - 1).astype(F32)
        for g in range(SSM_GROUPS):
            b16 = _bf(act_ref[0, :, _group_cols(g, 0)])
            c16 = _bf(act_ref[0, :, _group_cols(g, 1)])
            cb = _nt(c16, b16)
            for hs in _head_batches(g):
                x = [act_ref[0, :, _head_cols(h)] for h in hs]
                dt_j = [dt[:, h:h + 1] for h in hs]
                acs_j = [acs[:, h:h + 1] for h in hs]
                acs_last = [_colsum(a * last_row) for a in acs_j]
                xg = _each(lambda xv, d: xv * d, x, dt_j)
                mm = [cb * _decay_matrix(a, acs_t[h:h + 1, :], tri_mask) for a, h in zip(acs_j, hs)]
                decay_s = _each(lambda al, a: jnp.exp(al - a), acs_last, acs_j)
                y_diag = _each(lambda m_, v: _nn(_bf(m_), _bf(v)), mm, xg)
                st = _each(lambda v, d: _tn(_bf(v * d), b16), xg, decay_s)
                hp = [state[h] for h in hs]
                for h, v in zip(hs, hp):
                    hp_ref[0, h, 0] = v
                y_off = [_nt(c16, _bf(v)) for v in hp]
                for h, yd, yo, a, xv in zip(hs, y_diag, y_off, acs_j, x):
                    y_ref[0, :, _head_cols(h)] = yd + yo * jnp.exp(a) + dsk_ref[:, h:h + 1] * xv
                for h, v, al, sv in zip(hs, hp, acs_last, st):
                    state[h] = v * jnp.exp(al) + sv

    return pl.pallas_call(
        body, name=name, grid=(b, nc),
        in_specs=[act_spec, dt_in_spec, par_spec, par_spec, par_spec],
        out_specs=[y_spec, h_spec],
        out_shape=[jax.ShapeDtypeStruct((b, s, SSM_INNER), F32),
                   jax.ShapeDtypeStruct((b, SSM_HEADS, nc, SSM_P, D_STATE), F32)],
        scratch_shapes=[pltpu.VMEM((SSM_HEADS, SSM_P, D_STATE), F32)],
        compiler_params=_params(("arbitrary", "arbitrary")),
    )(act3, proj3, dtb, alog, dsk)


def ssd_bwd(act3, proj3, dtb, alog, dsk, hprev, dy3, *, name):
    b, s, _ = act3.shape
    nc = s // CHUNK
    act_spec, y_spec, dt_in_spec, dt_out_spec, par_spec, h_spec = _ssd_specs(nc, True)
    dpar_spec = pl.BlockSpec((8, LANE), lambda bi, c: (0, 0))

    def body(act_ref, dtr_ref, dtb_ref, alog_ref, dsk_ref, hp_ref, dy_ref, dact_ref, ddtr_ref, dpar_ref, dstate):
        bi, c = pl.program_id(0), pl.program_id(1)

        @pl.when(c == 0)
        def _():
            dstate[...] = jnp.zeros_like(dstate)

        @pl.when((bi == 0) & (c == 0))
        def _():
            dpar_ref[...] = jnp.zeros_like(dpar_ref)

        z, dt, aneg, acs = _ssd_common(dtr_ref, dtb_ref, alog_ref)
        acs_t = acs.T
        tri_mask = _tri(True)
        last_row = (lax.broadcasted_iota(jnp.int32, (CHUNK, 1), 0) == CHUNK - 1).astype(F32)
        lanes = lax.broadcasted_iota(jnp.int32, (1, LANE), 1)
        sublanes = lax.broadcasted_iota(jnp.int32, (LANE, 1), 0)
        ddt_mat = jnp.zeros((CHUNK, LANE), F32)
        dacs_mat = jnp.zeros((CHUNK, LANE), F32)
        dacs_rows = jnp.zeros((LANE, CHUNK), F32)
        ddsk_row = jnp.zeros((1, LANE), F32)
        for g in range(SSM_GROUPS):
            b16 = _bf(act_ref[0, :, _group_cols(g, 0)])
            c16 = _bf(act_ref[0, :, _group_cols(g, 1)])
            cb = _nt(c16, b16)
            dcb = jnp.zeros((CHUNK, CHUNK), F32)
            db_acc = jnp.zeros((CHUNK, D_STATE), F32)
            dc_acc = jnp.zeros((CHUNK, D_STATE), F32)
            for hs in _head_batches(g):
                x = [act_ref[0, :, _head_cols(h)] for h in hs]
                g_y = [dy_ref[0, :, _head_cols(h)] for h in hs]
                hp = [hp_ref[0, h, 0] for h in hs]
                g_hn = [dstate[h] for h in hs]
                dt_j = [dt[:, h:h + 1] for h in hs]
                acs_j = [acs[:, h:h + 1] for h in hs]
                acs_last = [_colsum(a * last_row) for a in acs_j]
                xg = _each(lambda xv, d: xv * d, x, dt_j)
                lm = [_decay_matrix(a, acs_t[h:h + 1, :], tri_mask) for a, h in zip(acs_j, hs)]
                mm = [cb * l for l in lm]
                decay_s = _each(lambda al, a: jnp.exp(al - a), acs_last, acs_j)
                ea = [jnp.exp(a) for a in acs_j]
                cd = [jnp.exp(al) for al in acs_last]
                g_y16, xg16, hp16, g_hn16 = [[_bf(v) for v in vs] for vs in (g_y, xg, hp, g_hn)]
                d_mm = _each(_nt, g_y16, xg16)
                d_xg = _each(lambda m_, gy: _tn(_bf(m_), gy), mm, g_y16)
                d_dm = _each(lambda a, m_: a * m_, d_mm, mm)
                d_acs = [_rowsum(v) for v in d_dm]
                t_off = [_nt(c16, v) for v in hp16]
                d_t16 = _each(lambda gy, e: _bf(gy * e), g_y, ea)
                d_acs = _each(lambda da, gy, t, e: da + _rowsum(gy * t) * e, d_acs, g_y, t_off, ea)
                d_hp = _each(lambda dtv, gh, cdv: _tn(dtv, c16) + gh * cdv, d_t16, g_hn, cd)
                d_w = [_nt(b16, v) for v in g_hn16]
                d_xg = _each(lambda dx, dw, ds: dx + dw * ds, d_xg, d_w, decay_s)
                d_ds = _each(lambda dw, v, ds: _rowsum(dw * v) * ds, d_w, xg, decay_s)
                d_last = _each(lambda gh, hv, cdv, dd: _colsum(_rowsum(gh * hv)) * cdv + _colsum(dd), g_hn, hp, cd, d_ds)
                d_acs = _each(lambda da, dd, dl: da - dd + dl * last_row, d_acs, d_ds, d_last)
                for h, gy, dx, d, xv in zip(hs, g_y, d_xg, dt_j, x):
                    dact_ref[0, :, _head_cols(h)] = dsk_ref[:, h:h + 1] * gy + dx * d
                for h, v in zip(hs, d_hp):
                    dstate[h] = v
                for k, h in enumerate(hs):
                    onehot = (lanes == h).astype(F32)
                    dcb = dcb + d_mm[k] * lm[k]
                    dc_acc = dc_acc + _nn(d_t16[k], hp16[k])
                    db_acc = db_acc + _nn(_bf(xg[k] * decay_s[k]), g_hn16[k])
                    ddsk_row = ddsk_row + _colsum(_rowsum(g_y[k] * x[k])) * onehot
                    ddt_mat = ddt_mat + _rowsum(d_xg[k] * x[k]) * onehot
                    dacs_mat = dacs_mat + d_acs[k] * onehot
                    dacs_rows = dacs_rows + (sublanes == h).astype(F32) * _colsum(d_dm[k])
            dcb16 = _bf(dcb)
            dact_ref[0, :, _group_cols(g, 1)] = dc_acc + _nn(dcb16, b16)
            dact_ref[0, :, _group_cols(g, 0)] = db_acc + _tn(dcb16, c16)
        d_a = _dot01_left(_tri(False).astype(BF16), dacs_mat - dacs_rows.T)
        ddt_mat = ddt_mat + d_a * aneg
        d_raw = ddt_mat * jax.nn.sigmoid(z)
        ddtr_ref[0] = d_raw
        dpar_ref[0:1, :] += _colsum(d_raw)
        dpar_ref[1:2, :] += _colsum(d_a * dt) * aneg
        dpar_ref[2:3, :] += ddsk_row

    return pl.pallas_call(
        body, name=name, grid=(b, nc),
        in_specs=[act_spec, dt_in_spec, par_spec, par_spec, par_spec, h_spec, y_spec],
        out_specs=[act_spec, dt_out_spec, dpar_spec],
        out_shape=[jax.ShapeDtypeStruct(act3.shape, F32), jax.ShapeDtypeStruct((b, s, LANE), F32),
                   jax.ShapeDtypeStruct((8, LANE), F32)],
        scratch_shapes=[pltpu.VMEM((SSM_HEADS, SSM_P, D_STATE), F32)],
        compiler_params=_params(("arbitrary", "arbitrary")),
    )(act3, proj3, dtb, alog, dsk, hprev, dy3)


def to_heads(x, b, s, h):
    return x.reshape(b, s, h, -1).transpose(0, 2, 1, 3)


def from_heads(x):
    b, h, s, c = x.shape
    return x.transpose(0, 2, 1, 3).reshape(b * s, h * c)


def dilate_q(q, d):
    b, _, s, c = q.shape
    x = q.reshape(b, N_KV_HEADS, GQA, s // d, d, c).transpose(0, 1, 4, 2, 3, 5)
    return x.reshape(b * N_KV_HEADS * d, GQA, s // d, c)


def undilate_q(x, b, d):
    _, _, l, c = x.shape
    y = x.reshape(b, N_KV_HEADS, d, GQA, l, c).transpose(0, 1, 3, 4, 2, 5)
    return y.reshape(b, N_Q_HEADS, l * d, c)


def dilate_kv(k, d):
    b, h, s, c = k.shape
    return k.reshape(b, h, s // d, d, c).transpose(0, 1, 3, 2, 4).reshape(b * h * d, s // d, c)


def undilate_kv(x, b, d):
    _, l, c = x.shape
    return x.reshape(b, N_KV_HEADS, d, l, c).transpose(0, 1, 3, 2, 4).reshape(b, N_KV_HEADS, l * d, c)


def rotary_tables(positions):
    inv_freq = ROPE_THETA ** (-jnp.arange(0, ROPE_DIM, 2, dtype=F32) / ROPE_DIM)
    ang = positions.astype(F32)[..., None] * inv_freq
    cos, sin = jnp.cos(ang), jnp.sin(ang)
    rest = HEAD_DIM - ROPE_DIM
    cosf = jnp.concatenate([cos, cos, jnp.ones(cos.shape[:2] + (rest,), F32)], axis=-1)
    sinf = jnp.concatenate([-sin, sin, jnp.zeros(sin.shape[:2] + (rest,), F32)], axis=-1)
    return cosf, sinf


def w_in_columns(w):
    pad = jnp.zeros((w.shape[0], IN_PAD - IN_PROJ), w.dtype)
    return jnp.concatenate([w[:, :Q_END], w[:, V_END:XBC_END], w[:, Q_END:V_END], w[:, XBC_END:], pad], axis=1)


def w_in_grad_columns(g):
    return jnp.concatenate([g[:, :Z_COL], g[:, K_COL:DT_COL], g[:, Z_COL:K_COL], g[:, DT_COL:DT_COL + SSM_HEADS]], axis=1)


def lane_pad(v):
    return jnp.pad(v.reshape(1, -1), ((0, 0), (0, LANE - v.shape[-1])))


def layer_fwd(h, wts, small, rope_tab, b, s, tag):
    w_in, w_out, w_gate, w_up, w_down = wts
    t = b * s
    sv = {"h": h}
    hn = rowwise_fwd(rms_fn, [h], [small["norm_mix"]], [BF16], name=f"rms_mix_{tag}")[0]
    proj = matmul(hn, w_in, name=f"in_proj_{tag}")
    sv["hn"], sv["proj"] = hn, proj
    proj3 = proj.reshape(b, s, IN_PAD)
    attn3, lse3 = attn_fwd(proj3, rope_tab, name=f"attn_{tag}")
    sv["attn3"], sv["lse3"] = attn3, lse3
    attn = attn3.reshape(t, ATTN_WIDTH)
    act3 = conv_silu_fwd(proj3, small["conv_w"], small["conv_b"], name=f"conv_{tag}")
    y3, hprev = ssd_fwd(act3, proj3, small["dt_bias"], small["a_log"], small["d_skip"], name=f"ssd_{tag}")
    y = y3.reshape(t, SSM_INNER)
    sv["act3"], sv["hprev"], sv["y"] = act3, hprev, y
    gn = rowwise_fwd(gated_norm_fn, [y, proj], [small["ssm_norm"]], [F32], name=f"gated_norm_{tag}", groups=SSM_GROUPS,
                     windows=[None, (Z_COL, SSM_INNER)])[0]
    cat = jnp.concatenate([attn, gn], axis=1).astype(BF16)
    sv["cat"] = cat
    h1 = matmul(cat, w_out, name=f"out_proj_{tag}", residual=h)
    sv["h1"] = h1
    hn2 = rowwise_fwd(rms_fn, [h1], [small["norm_ffn"]], [BF16], name=f"rms_ffn_{tag}")[0]
    gate = matmul(hn2, w_gate, out_dtype=BF16, name=f"ffn_gate_{tag}")
    up = matmul(hn2, w_up, out_dtype=BF16, name=f"ffn_up_{tag}")
    act2 = rowwise_fwd(swiglu_fn, [gate, up], [], [BF16], name=f"swiglu_{tag}")[0]
    sv["hn2"], sv["gate"], sv["up"], sv["act2"] = hn2, gate, up, act2
    h2 = matmul(act2, w_down, name=f"ffn_down_{tag}", residual=h1)
    return h2, sv


def layer_bwd(dh2, sv, wts, small, rope_tab, b, s, tag):
    w_in, w_out, w_gate, w_up, w_down = wts
    t = b * s
    gr = {}
    dh2_16 = dh2.astype(BF16)
    d_act2 = matmul(dh2_16, w_down, tb=True, out_dtype=BF16, name=f"ffn_down_dx_{tag}")
    gr["w_down"] = matmul(sv["act2"], dh2_16, ta=True, out_dtype=BF16, name=f"ffn_down_dw_{tag}")
    d_gate, d_up = rowwise_bwd(swiglu_fn, [sv["gate"], sv["up"]], [], [d_act2], [BF16, BF16], name=f"swiglu_bwd_{tag}")
    gr["w_gate"] = matmul(sv["hn2"], d_gate, ta=True, out_dtype=BF16, name=f"ffn_gate_dw_{tag}")
    gr["w_up"] = matmul(sv["hn2"], d_up, ta=True, out_dtype=BF16, name=f"ffn_up_dw_{tag}")
    d_hn2 = matmul(d_gate, w_gate, tb=True, name=f"ffn_gate_dx_{tag}")
    d_hn2 = matmul(d_up, w_up, tb=True, residual=d_hn2, name=f"ffn_up_dx_{tag}")
    dh1, gr["norm_ffn"] = rowwise_bwd(rms_fn, [sv["h1"]], [small["norm_ffn"]], [d_hn2], [F32],
                                      name=f"rms_ffn_bwd_{tag}", add_to_first=dh2)
    dh1_16 = dh1.astype(BF16)
    d_cat = matmul(dh1_16, w_out, tb=True, name=f"out_proj_dx_{tag}")
    gr["w_out"] = matmul(sv["cat"], dh1_16, ta=True, out_dtype=BF16, name=f"out_proj_dw_{tag}")
    d_attn, d_gn = d_cat[:, :ATTN_WIDTH], d_cat[:, ATTN_WIDTH:]
    d_y, d_z, gr["ssm_norm"] = rowwise_bwd(gated_norm_fn, [sv["y"], sv["proj"]], [small["ssm_norm"]], [d_gn], [F32, F32],
                                           name=f"gated_norm_bwd_{tag}", groups=SSM_GROUPS,
                                           windows=[None, (Z_COL, SSM_INNER)])
    proj3 = sv["proj"].reshape(b, s, IN_PAD)
    d_act3, d_dtr, d_par = ssd_bwd(sv["act3"], proj3, small["dt_bias"], small["a_log"], small["d_skip"], sv["hprev"],
                                   d_y.reshape(b, s, SSM_INNER), name=f"ssd_bwd_{tag}")
    gr["dt_bias"], gr["a_log"], gr["d_skip"] = d_par[0, :SSM_HEADS], d_par[1, :SSM_HEADS], d_par[2, :SSM_HEADS]
    d_xbc, gr["conv_w"], gr["conv_b"] = conv_silu_bwd(proj3, small["conv_w"], small["conv_b"], d_act3,
                                                      name=f"conv_bwd_{tag}")
    d_q3, d_k4, d_v4 = attn_bwd(proj3, rope_tab, sv["attn3"], sv["lse3"], d_attn.reshape(b, s, ATTN_WIDTH),
                                name=f"attn_bwd_{tag}")
    d_proj = jnp.concatenate([d_q3.reshape(t, ATTN_WIDTH), d_z, d_xbc.reshape(t, CONV_CH), from_heads(d_k4),
                              from_heads(d_v4), d_dtr.reshape(t, LANE)], axis=1).astype(BF16)
    d_hn = matmul(d_proj, w_in, tb=True, name=f"in_proj_dx_{tag}")
    gr["w_in"] = w_in_grad_columns(matmul(sv["hn"], d_proj, ta=True, out_dtype=BF16, name=f"in_proj_dw_{tag}"))
    dh, gr["norm_mix"] = rowwise_bwd(rms_fn, [sv["h"]], [small["norm_mix"]], [d_hn], [F32],
                                     name=f"rms_mix_bwd_{tag}", add_to_first=dh1)
    return dh, gr


def local_step(x, positions, big, small_all, final_norm, loss_target):
    b, s, _ = x.shape
    t = b * s
    rope_tab = jnp.concatenate(rotary_tables(positions), axis=-1)
    h = x.reshape(t, D_MODEL)
    saved = []
    for l in range(DEPTH):
        h, sv = layer_fwd(h, big[l], small_all[l], rope_tab, b, s, f"l{l}")
        saved.append(sv)
    dh, d_final, loss = loss_and_grad(h, loss_target.reshape(t, D_MODEL), final_norm.reshape(1, D_MODEL))
    grads = [None] * DEPTH
    for l in reversed(range(DEPTH)):
        dh, grads[l] = layer_bwd(dh, saved[l], big[l], small_all[l], rope_tab, b, s, f"l{l}")
    return loss, dh.reshape(b, s, D_MODEL), grads, d_final


def _slab_rows(r):
    return r if r <= 512 else _pick(r, (512, 256))


def cast_bf16(x, *, name):
    def fn(v):
        return (v,)
    return rowwise_fwd(fn, [x], [], [BF16], name=name, tr=_slab_rows(x.shape[0]))[0]


def sum_slots(x, *, name):
    n, r, c = x.shape
    tr = _slab_rows(r)

    def body(x_ref, o_ref):
        acc = x_ref[0].astype(F32)
        for i in range(1, n):
            acc = acc + x_ref[i].astype(F32)
        o_ref[...] = acc

    return pl.pallas_call(
        body, name=name, grid=(r // tr,), in_specs=[pl.BlockSpec((n, tr, c), lambda i: (0, i, 0))],
        out_specs=pl.BlockSpec((tr, c), lambda i: (i, 0)), out_shape=jax.ShapeDtypeStruct((r, c), F32),
        compiler_params=_params(("parallel",)),
    )(x)


def adamw(g_parts, w, m, v, *, name):
    r, c = w.shape
    tr = _slab_rows(r)
    n_g = len(g_parts)
    bc1 = 1.0 / (1.0 - ADAM_B1 ** ADAM_STEP)
    bc2 = 1.0 / (1.0 - ADAM_B2 ** ADAM_STEP)

    def body(*refs):
        g = refs[0][...]
        for r_ in refs[1:n_g]:
            g = g + r_[...]
        w_ref, m_ref, v_ref, g_out, d_out, m_out, v_out = refs[n_g:]
        m_new = ADAM_B1 * m_ref[...] + (1.0 - ADAM_B1) * g
        v_new = ADAM_B2 * v_ref[...] + (1.0 - ADAM_B2) * (g * g)
        g_out[...] = g
        m_out[...] = m_new
        v_out[...] = v_new
        d_out[...] = -ADAM_LR * ((m_new * bc1) / (jnp.sqrt(v_new * bc2) + ADAM_EPS) + ADAM_WD * w_ref[...])

    spec = pl.BlockSpec((tr, c), lambda i: (i, 0))
    return pl.pallas_call(
        body, name=name, grid=(r // tr,), in_specs=[spec] * (n_g + 3), out_specs=[spec] * 4,
        out_shape=[jax.ShapeDtypeStruct((r, c), F32)] * 4, compiler_params=_params(("parallel",)),
    )(*g_parts, w, m, v)


def _other_chips(x, y):
    return [(1 - x, y), (x, 1 - y), (1 - x, 1 - y)]


def allgather_chips(shards):
    n_arr = len(shards)

    def body(*refs):
        in_refs, out_refs = refs[:n_arr], refs[n_arr:2 * n_arr]
        send_sems, recv_sems, local_sems = refs[2 * n_arr:]
        x, y, c = lax.axis_index("x"), lax.axis_index("y"), lax.axis_index("c")
        chip = 2 * x + y
        started = []
        for a, (in_ref, out_ref) in enumerate(zip(in_refs, out_refs)):
            mine = pltpu.make_async_copy(in_ref, out_ref.at[chip], local_sems.at[a])
            mine.start()
            started.append(mine.wait)
            for k, (px, py) in enumerate(_other_chips(x, y)):
                cp = pltpu.make_async_remote_copy(src_ref=in_ref, dst_ref=out_ref.at[chip], send_sem=send_sems.at[3 * a + k],
                                                  recv_sem=recv_sems.at[3 * a + k], device_id=(px, py, c), device_id_type=MESH)
                cp.start()
                started.append(cp.wait_send)
        for a, (in_ref, out_ref) in enumerate(zip(in_refs, out_refs)):
            for k, (px, py) in enumerate(_other_chips(x, y)):
                pltpu.make_async_remote_copy(src_ref=in_ref, dst_ref=out_ref.at[2 * px + py], send_sem=send_sems.at[3 * a + k],
                                             recv_sem=recv_sems.at[3 * a + k], device_id=(px, py, c),
                                             device_id_type=MESH).wait_recv()
        for wait in started:
            wait()

    hbm = pl.BlockSpec(memory_space=pltpu.HBM)
    return pl.pallas_call(
        body, name="allgather_weights", in_specs=[hbm] * n_arr, out_specs=[hbm] * n_arr,
        out_shape=[jax.ShapeDtypeStruct((N_CHIPS,) + s.shape, s.dtype) for s in shards],
        scratch_shapes=[pltpu.SemaphoreType.DMA((3 * n_arr,)), pltpu.SemaphoreType.DMA((3 * n_arr,)),
                        pltpu.SemaphoreType.DMA((n_arr,))],
    )(*shards)


def exchange_grads(big, small):
    def body(big_ref, small_ref, big_out, small_out, send_sems, recv_sems, local_sems):
        x, y, c = lax.axis_index("x"), lax.axis_index("y"), lax.axis_index("c")
        chip = 2 * x + y
        dev = 4 * x + 2 * y + c
        own_big = pltpu.make_async_copy(big_ref.at[chip], big_out.at[chip], local_sems.at[0])
        own_small = pltpu.make_async_copy(small_ref, small_out.at[dev], local_sems.at[1])
        own_big.start()
        own_small.start()
        sends = []
        for k, (px, py) in enumerate(_other_chips(x, y)):
            cp = pltpu.make_async_remote_copy(src_ref=big_ref.at[2 * px + py], dst_ref=big_out.at[chip],
                                              send_sem=send_sems.at[k], recv_sem=recv_sems.at[k],
                                              device_id=(px, py, c), device_id_type=MESH)
            cp.start()
            sends.append(cp)
        peers = []
        for r in range(1, N_DEV):
            fx, fy, fc = (r >> 2) & 1, (r >> 1) & 1, r & 1
            px, py, pc = (x + fx) % 2, (y + fy) % 2, (c + fc) % 2
            peers.append((px, py, pc))
            cp = pltpu.make_async_remote_copy(src_ref=small_ref, dst_ref=small_out.at[dev], send_sem=send_sems.at[2 + r],
                                              recv_sem=recv_sems.at[2 + r], device_id=(px, py, pc), device_id_type=MESH)
            cp.start()
            sends.append(cp)
        for k, (px, py) in enumerate(_other_chips(x, y)):
            pltpu.make_async_remote_copy(src_ref=big_ref.at[chip], dst_ref=big_out.at[2 * px + py],
                                         send_sem=send_sems.at[k], recv_sem=recv_sems.at[k],
                                         device_id=(px, py, c), device_id_type=MESH).wait_recv()
        for r, (px, py, pc) in zip(range(1, N_DEV), peers):
            pltpu.make_async_remote_copy(src_ref=small_ref, dst_ref=small_out.at[4 * px + 2 * py + pc],
                                         send_sem=send_sems.at[2 + r], recv_sem=recv_sems.at[2 + r],
                                         device_id=(px, py, pc), device_id_type=MESH).wait_recv()
        for cp in sends:
            cp.wait_send()
        own_big.wait()
        own_small.wait()

    hbm = pl.BlockSpec(memory_space=pltpu.HBM)
    n_sem = 3 + N_DEV - 1
    return pl.pallas_call(
        body, name="exchange_grads", in_specs=[hbm, hbm], out_specs=[hbm, hbm],
        out_shape=[jax.ShapeDtypeStruct(big.shape, big.dtype), jax.ShapeDtypeStruct((N_DEV,) + small.shape, small.dtype)],
        scratch_shapes=[pltpu.SemaphoreType.DMA((n_sem,)), pltpu.SemaphoreType.DMA((n_sem,)), pltpu.SemaphoreType.DMA((2,))],
    )(big, small)


SWAP_CHUNKS = 27


def swap_cores(mine):
    rows = mine.shape[0] // SWAP_CHUNKS
    assert rows * SWAP_CHUNKS == mine.shape[0] and rows % 8 == 0

    def body(in_ref, out_ref, send_sems, recv_sems):
        x, y, c = lax.axis_index("x"), lax.axis_index("y"), lax.axis_index("c")

        def chunk(k):
            part = pl.ds(k * rows, rows)
            return pltpu.make_async_remote_copy(src_ref=in_ref.at[part], dst_ref=out_ref.at[part],
                                                send_sem=send_sems.at[k], recv_sem=recv_sems.at[k],
                                                device_id=(x, y, 1 - c), device_id_type=MESH)

        for k in range(SWAP_CHUNKS):
            chunk(k).start()
        for k in range(SWAP_CHUNKS):
            chunk(k).wait_recv()
        for k in range(SWAP_CHUNKS):
            chunk(k).wait_send()

    hbm = pl.BlockSpec(memory_space=pltpu.HBM)
    return pl.pallas_call(
        body, name="swap_cores", in_specs=[hbm], out_specs=hbm,
        out_shape=jax.ShapeDtypeStruct(mine.shape, mine.dtype),
        scratch_shapes=[pltpu.SemaphoreType.DMA((SWAP_CHUNKS,)), pltpu.SemaphoreType.DMA((SWAP_CHUNKS,))],
    )(mine)


BIG_NAMES = ("w_in", "w_out", "w_gate", "w_up", "w_down")
BIG_SHARD_AXIS = {"w_in": 1, "w_out": 0, "w_gate": 1, "w_up": 1, "w_down": 0}
PACK_COLS = 1024
SMALL_NAMES = ("norm_mix", "conv_w", "conv_b", "dt_bias", "a_log", "d_skip", "ssm_norm", "norm_ffn")


PACK_ROW_TILE = 256


def pack_big(shards):
    flat = jnp.concatenate([shards[n].reshape(-1) for n in BIG_NAMES])
    unit = PACK_ROW_TILE * PACK_COLS
    total = -(-flat.size // unit) * unit
    return jnp.pad(flat, (0, total - flat.size)).reshape(-1, PACK_COLS)


def unpack_big(packed, like):
    out, off = {}, 0
    flat = packed.reshape(-1)
    for n in BIG_NAMES:
        size = like[n].size
        out[n] = flat[off:off + size].reshape(like[n].shape)
        off += size
    return out


def pack_small(parts):
    flat = jnp.concatenate([p.reshape(-1).astype(F32) for p in parts])
    rows = -(-flat.size // LANE)
    rows = -(-rows // 8) * 8
    return jnp.pad(flat, (0, rows * LANE - flat.size)).reshape(rows, LANE)


def unpack_small(packed, like):
    out, off = [], 0
    flat = packed.reshape(-1)
    for a in like:
        out.append(flat[off:off + a.size].reshape(a.shape))
        off += a.size
    return out


def kernel(x, positions, norm_mix, w_in, conv_w, conv_b, dt_bias, a_log, d_skip, ssm_norm, w_out, norm_ffn, w_gate, w_up, w_down, final_norm, loss_target, m_norm_mix, m_w_in, m_conv_w, m_conv_b, m_dt_bias, m_a_log, m_d_skip, m_ssm_norm, m_w_out, m_norm_ffn, m_w_gate, m_w_up, m_w_down, m_final_norm, v_norm_mix, v_w_in, v_conv_w, v_conv_b, v_dt_bias, v_a_log, v_d_skip, v_ssm_norm, v_w_out, v_norm_ffn, v_w_gate, v_w_up, v_w_down, v_final_norm):
    chip = 2 * lax.axis_index("x") + lax.axis_index("y")
    w_sh = {"w_in": w_in, "w_out": w_out, "w_gate": w_gate, "w_up": w_up, "w_down": w_down}
    m_sh = {"w_in": m_w_in, "w_out": m_w_out, "w_gate": m_w_gate, "w_up": m_w_up, "w_down": m_w_down}
    v_sh = {"w_in": v_w_in, "w_out": v_w_out, "w_gate": v_w_gate, "w_up": v_w_up, "w_down": v_w_down}

    w_packed = pack_big(w_sh)
    conv_cols = CONV_CH // N_CHIPS
    gathered, conv_g = allgather_chips([cast_bf16(w_packed, name="cast_weights"), conv_w.reshape(-1, LANE)])
    pieces = [unpack_big(gathered[j], w_sh) for j in range(N_CHIPS)]
    full = {n: jnp.concatenate([p[n] for p in pieces], axis=BIG_SHARD_AXIS[n] + 1) for n in BIG_NAMES}
    big = []
    for l in range(DEPTH):
        big.append((w_in_columns(full["w_in"][l]), full["w_out"][l], full["w_gate"][l], full["w_up"][l], full["w_down"][l]))
    conv_w_full = jnp.concatenate([conv_g[j].reshape(DEPTH, CONV_WIDTH, conv_cols) for j in range(N_CHIPS)], axis=2)

    small_all = []
    for l in range(DEPTH):
        small_all.append({
            "norm_mix": norm_mix[l].reshape(1, -1), "conv_w": conv_w_full[l], "conv_b": conv_b[l].reshape(1, -1),
            "dt_bias": lane_pad(dt_bias[l]), "a_log": lane_pad(a_log[l]), "d_skip": lane_pad(d_skip[l]),
            "ssm_norm": ssm_norm[l].reshape(1, -1), "norm_ffn": norm_ffn[l].reshape(1, -1)})

    loss_part, grad_x, grads, d_final = local_step(x, positions, big, small_all, final_norm, loss_target)

    def shard_of(name, g, j):
        n = g.shape[BIG_SHARD_AXIS[name]] // N_CHIPS
        return lax.slice_in_dim(g, j * n, (j + 1) * n, axis=BIG_SHARD_AXIS[name])

    to_chip = []
    for j in range(N_CHIPS):
        to_chip.append(pack_big({n: jnp.stack([shard_of(n, grads[l][n], j) for l in range(DEPTH)]) for n in BIG_NAMES}))
    small_parts = [jnp.stack([grads[l][n].reshape(-1) for l in range(DEPTH)]) for n in SMALL_NAMES]
    small_parts += [d_final.reshape(-1), loss_part.reshape(-1)]
    recv_big, recv_small = exchange_grads(jnp.stack(to_chip), pack_small(small_parts))
    plane_sum = sum_slots(recv_big, name="sum_chip_partials")
    other_plane = swap_cores(plane_sum)

    g_big, d_big, m_big, v_big = adamw([plane_sum, other_plane], w_packed, pack_big(m_sh), pack_big(v_sh), name="adamw_big")
    g_big, d_big, m_big, v_big = (unpack_big(a, w_sh) for a in (g_big, d_big, m_big, v_big))

    small_sum = sum_slots(recv_small, name="sum_small")
    like = [norm_mix, conv_w_full, conv_b, dt_bias, a_log, d_skip, ssm_norm, norm_ffn, final_norm, loss_part.reshape(-1)]
    g_small = unpack_small(small_sum, like)
    loss = g_small[-1][0]
    g_small = dict(zip(SMALL_NAMES + ("final_norm",), g_small[:-1]))
    g_small["conv_w"] = lax.dynamic_slice_in_dim(g_small["conv_w"], chip * conv_cols, conv_cols, axis=2)
    w_small = {"norm_mix": norm_mix, "conv_w": conv_w, "conv_b": conv_b, "dt_bias": dt_bias, "a_log": a_log, "d_skip": d_skip,
               "ssm_norm": ssm_norm, "norm_ffn": norm_ffn, "final_norm": final_norm}
    m_small = {"norm_mix": m_norm_mix, "conv_w": m_conv_w, "conv_b": m_conv_b, "dt_bias": m_dt_bias, "a_log": m_a_log,
               "d_skip": m_d_skip, "ssm_norm": m_ssm_norm, "norm_ffn": m_norm_ffn, "final_norm": m_final_norm}
    v_small = {"norm_mix": v_norm_mix, "conv_w": v_conv_w, "conv_b": v_conv_b, "dt_bias": v_dt_bias, "a_log": v_a_log,
               "d_skip": v_d_skip, "ssm_norm": v_ssm_norm, "norm_ffn": v_norm_ffn, "final_norm": v_final_norm}
    names = SMALL_NAMES + ("final_norm",)
    order = [w_small[n] for n in names]
    res = adamw([pack_small([g_small[n] for n in names])], pack_small(order), pack_small([m_small[n] for n in names]),
                pack_small([v_small[n] for n in names]), name="adamw_small")
    g_s, d_s, m_s, v_s = (dict(zip(names, unpack_small(a, order))) for a in res)

    all_names = ("norm_mix", "w_in", "conv_w", "conv_b", "dt_bias", "a_log", "d_skip", "ssm_norm", "w_out", "norm_ffn",
                 "w_gate", "w_up", "w_down", "final_norm")
    outs = [loss, grad_x]
    for src_big, src_small in ((g_big, g_s), (d_big, d_s), (m_big, m_s), (v_big, v_s)):
        outs += [src_big[n] if n in BIG_NAMES else src_small[n] for n in all_names]
    return tuple(outs)
```

```python
import functools

import jax
import jax.numpy as jnp
from jax import lax
from jax.experimental import pallas as pl
from jax.experimental.pallas import tpu as pltpu

F32 = jnp.float32
BF16 = jnp.bfloat16
MESH = pl.DeviceIdType.MESH

D_MODEL = 1024
DEPTH = 2
HEAD_DIM = 64
N_Q_HEADS = 8
N_KV_HEADS = 2
GQA = N_Q_HEADS // N_KV_HEADS
ATTN_WIDTH = N_Q_HEADS * HEAD_DIM
ROPE_DIM = HEAD_DIM // 4
ROPE_HALF = ROPE_DIM // 2
ROPE_THETA = 500000.0
DILATIONS = (1, 4, 16)
ATTN_BLOCK = 128
SSM_P = 64
SSM_HEADS = 16
SSM_INNER = SSM_HEADS * SSM_P
SSM_GROUPS = 2
HEADS_PER_GROUP = SSM_HEADS // SSM_GROUPS
D_STATE = 128
CONV_WIDTH = 4
CHUNK = 128
CONV_CH = SSM_INNER + 2 * SSM_GROUPS * D_STATE
MIX_WIDTH = ATTN_WIDTH + SSM_INNER
Q_END = ATTN_WIDTH
K_END = Q_END + N_KV_HEADS * HEAD_DIM
V_END = K_END + N_KV_HEADS * HEAD_DIM
Z_END = V_END + SSM_INNER
XBC_END = Z_END + CONV_CH
IN_PROJ = XBC_END + SSM_HEADS
LANE = 128
IN_PAD = XBC_END + LANE
Q_COL, Z_COL, XBC_COL, K_COL, V_COL, DT_COL = 0, 512, 1536, 3072, 3200, 3328
FFN_HIDDEN = 2816
EPS = 1e-5
ADAM_LR, ADAM_B1, ADAM_B2, ADAM_EPS, ADAM_WD, ADAM_STEP = 0.001, 0.9, 0.999, 1e-8, 0.01, 10
N_CHIPS = 4
N_DEV = 8
VMEM_LIMIT = 48 * 1024 * 1024
NEG_BIG = -1e30


def _params(sem=None):
    return pltpu.CompilerParams(dimension_semantics=sem, vmem_limit_bytes=VMEM_LIMIT)


def _pick(n, prefs):
    for p in prefs:
        if n % p == 0:
            return p
    return n


def matmul(a, b, *, name, ta=False, tb=False, out_dtype=F32, residual=None):
    if ta:
        assert not tb and residual is None
        return _matmul_over_rows(a, b, name=name, out_dtype=out_dtype)
    return _matmul_full_k(a, b, name=name, tb=tb, out_dtype=out_dtype, residual=residual)


def _matmul_full_k(a, b, *, name, tb, out_dtype, residual):
    m, kdim = a.shape
    n = b.shape[0] if tb else b.shape[1]
    tm = _pick(m, (1024, 512, 256)) if kdim <= 1536 else _pick(m, (512, 256))
    tn = _pick(n, (1152, 1408, 1536, 1024, 768, 512, 384, 256, 128))
    b_spec = pl.BlockSpec((tn, kdim), lambda i, j: (j, 0)) if tb else pl.BlockSpec((kdim, tn), lambda i, j: (0, j))
    o_spec = pl.BlockSpec((tm, tn), lambda i, j: (i, j))
    dims = (((1,), (1 if tb else 0,)), ((), ()))
    has_res = residual is not None

    def body(*refs):
        a_ref, b_ref = refs[:2]
        o_ref = refs[-1]
        r = lax.dot_general(a_ref[...].astype(BF16), b_ref[...].astype(BF16), dims, preferred_element_type=F32)
        if has_res:
            r = r + refs[2][...]
        o_ref[...] = r.astype(out_dtype)

    in_specs = [pl.BlockSpec((tm, kdim), lambda i, j: (i, 0)), b_spec] + ([o_spec] if has_res else [])
    args = (a, b) + ((residual,) if has_res else ())
    return pl.pallas_call(
        body, name=name, grid=(m // tm, n // tn), in_specs=in_specs, out_specs=o_spec,
        out_shape=jax.ShapeDtypeStruct((m, n), out_dtype),
        compiler_params=_params(("parallel", "parallel")),
    )(*args)


def _matmul_over_rows(a, b, *, name, out_dtype):
    t, m = a.shape
    n = b.shape[1]
    tm = _pick(m, (1024, 1408, 768, 512, 256, 128))
    tn = _pick(n, (1152, 1408, 1024, 768, 512, 256, 128))
    tk = _pick(t, (1024, 512, 256, 128))
    nk = t // tk

    def body(a_ref, b_ref, o_ref, acc):
        k = pl.program_id(2)
        part = lax.dot_general(a_ref[...].astype(BF16), b_ref[...].astype(BF16), (((0,), (0,)), ((), ())),
                               preferred_element_type=F32)

        @pl.when(k == 0)
        def _():
            acc[...] = part

        @pl.when(k > 0)
        def _():
            acc[...] += part

        @pl.when(k == nk - 1)
        def _():
            o_ref[...] = acc[...].astype(out_dtype)

    return pl.pallas_call(
        body, name=name, grid=(m // tm, n // tn, nk),
        in_specs=[pl.BlockSpec((tk, tm), lambda i, j, k: (k, i)), pl.BlockSpec((tk, tn), lambda i, j, k: (k, j))],
        out_specs=pl.BlockSpec((tm, tn), lambda i, j, k: (i, j)),
        out_shape=jax.ShapeDtypeStruct((m, n), out_dtype),
        scratch_shapes=[pltpu.VMEM((tm, tn), F32)],
        compiler_params=_params(("parallel", "parallel", "arbitrary")),
    )(a, b)


ROW_BLOCK_BYTES = 16 * 1024 * 1024


def _row_tile(t, tr, widths, n_copies):
    lanes = sum(-(-wd // LANE) * LANE for wd in widths) * n_copies
    tr = min(tr, t)
    while tr > 8 and tr * lanes * 4 > ROW_BLOCK_BYTES:
        tr //= 2
    return tr


def _row_widths(rows, groups, windows):
    windows = windows or [None] * len(rows)
    widths = [(w[1] if w else a.shape[1]) // groups for a, w in zip(rows, windows)]
    assert all(w is None or w[0] % wd == 0 for w, wd in zip(windows, widths))
    return widths, [(w[0] // wd if w else 0) for w, wd in zip(windows, widths)]


def _row_specs(tr, widths, offs):
    return [pl.BlockSpec((tr, wd), functools.partial(lambda g, i, off: (i, g + off), off=off)) for wd, off in zip(widths, offs)]


def rowwise_fwd(fn, rows, params, out_dtypes, *, name, tr=512, groups=1, windows=None):
    t = rows[0].shape[0]
    widths, offs = _row_widths(rows, groups, windows)
    tr = _row_tile(t, tr, widths, 2)
    row_specs = _row_specs(tr, widths, offs)
    par_spec = lambda p: pl.BlockSpec((1, p.shape[1] // groups), lambda g, i: (0, g))
    n_in = len(rows) + len(params)
    out_cols = [o.shape[1] for o in jax.eval_shape(
        fn, *[jax.ShapeDtypeStruct((tr, wd), F32) for wd in widths],
        *[jax.ShapeDtypeStruct((1, p.shape[1] // groups), F32) for p in params])]

    def body(*refs):
        vals = [r[...].astype(F32) for r in refs[:n_in]]
        outs = fn(*vals)
        for o_ref, o in zip(refs[n_in:], outs):
            o_ref[...] = o.astype(o_ref.dtype)

    return pl.pallas_call(
        body, name=name, grid=(groups, t // tr),
        in_specs=row_specs + [par_spec(p) for p in params],
        out_specs=[pl.BlockSpec((tr, c), lambda g, i: (i, g)) for c in out_cols],
        out_shape=[jax.ShapeDtypeStruct((t, c * groups), d) for c, d in zip(out_cols, out_dtypes)],
        compiler_params=_params(("arbitrary", "arbitrary")),
    )(*rows, *params)


def rowwise_bwd(fn, rows, params, cts, drow_dtypes, *, name, tr=512, groups=1, add_to_first=None, windows=None):
    t = rows[0].shape[0]
    widths, offs = _row_widths(rows, groups, windows)
    tr = _row_tile(t, tr, widths + [a.shape[1] // groups for a in cts], 2)
    row_spec = lambda a: pl.BlockSpec((tr, a.shape[1] // groups), lambda g, i: (i, g))
    row_specs = _row_specs(tr, widths, offs)
    par_spec = lambda p: pl.BlockSpec((1, p.shape[1] // groups), lambda g, i: (0, g))
    n_rows, n_par, n_ct = len(rows), len(params), len(cts)
    has_add = add_to_first is not None
    n_in = n_rows + n_par + n_ct + (1 if has_add else 0)

    def body(*refs):
        i = pl.program_id(1)
        vals = [r[...].astype(F32) for r in refs[:n_rows + n_par]]
        ct_vals = tuple(r[...].astype(F32) for r in refs[n_rows + n_par:n_rows + n_par + n_ct])
        _, vjp = jax.vjp(fn, *vals)
        grads = vjp(ct_vals)
        out_refs = refs[n_in:]
        for idx in range(n_rows):
            g = grads[idx]
            if idx == 0 and has_add:
                g = g + refs[n_in - 1][...]
            out_refs[idx][...] = g.astype(out_refs[idx].dtype)
        for idx in range(n_par):
            p_ref = out_refs[n_rows + idx]

            @pl.when(i == 0)
            def _():
                p_ref[...] = jnp.zeros_like(p_ref)

            p_ref[...] += grads[n_rows + idx]

    ins = list(rows) + list(params) + list(cts) + ([add_to_first] if has_add else [])
    in_specs = (row_specs + [par_spec(p) for p in params] + [row_spec(a) for a in cts]
                + ([row_spec(add_to_first)] if has_add else []))
    return pl.pallas_call(
        body, name=name, grid=(groups, t // tr), in_specs=in_specs,
        out_specs=[pl.BlockSpec((tr, wd), lambda g, i: (i, g)) for wd in widths] + [par_spec(p) for p in params],
        out_shape=[jax.ShapeDtypeStruct((t, wd * groups), d) for wd, d in zip(widths, drow_dtypes)]
        + [jax.ShapeDtypeStruct(p.shape, F32) for p in params],
        compiler_params=_params(("arbitrary", "arbitrary")),
    )(*ins)


def rms_fn(x, w):
    return (x * lax.rsqrt(jnp.mean(x * x, axis=-1, keepdims=True) + EPS) * w,)


def swiglu_fn(g, u):
    return (g * jax.nn.sigmoid(g) * u,)


def gated_norm_fn(y, z, w):
    v = y * (z * jax.nn.sigmoid(z))
    return (v * lax.rsqrt(jnp.mean(v * v, axis=-1, keepdims=True) + EPS) * w,)


def combine_fn(o1, o2, o3, l1, l2, l3):
    m = jnp.maximum(jnp.maximum(l1, l2), l3)
    e1, e2, e3 = jnp.exp(l1 - m), jnp.exp(l2 - m), jnp.exp(l3 - m)
    inv = 1.0 / (e1 + e2 + e3)
    return ((e1 * inv) * o1 + (e2 * inv) * o2 + (e3 * inv) * o3,)


def loss_and_grad(h, target, w, *, tr=512):
    t, d = h.shape

    def loss_fn(hv, wv, tv):
        err = rms_fn(hv, wv)[0] - tv
        per_row = jnp.mean(err * err, axis=-1, keepdims=True)
        return 0.5 * jnp.sum(per_row, axis=0, keepdims=True)

    def body(h_ref, t_ref, w_ref, dh_ref, dw_ref, loss_ref):
        i = pl.program_id(0)

        @pl.when(i == 0)
        def _():
            dw_ref[...] = jnp.zeros_like(dw_ref)
            loss_ref[...] = jnp.zeros_like(loss_ref)

        tv = t_ref[...]
        val, vjp = jax.vjp(lambda hv, wv: loss_fn(hv, wv, tv), h_ref[...], w_ref[...])
        dh, dw = vjp(jnp.ones((1, 1), F32))
        dh_ref[...] = dh
        dw_ref[...] += dw
        loss_ref[...] += jnp.broadcast_to(val, loss_ref.shape)

    row = pl.BlockSpec((tr, d), lambda i: (i, 0))
    par = pl.BlockSpec((1, d), lambda i: (0, 0))
    return pl.pallas_call(
        body, name="loss_and_grad", grid=(t // tr,), in_specs=[row, row, par],
        out_specs=[row, par, pl.BlockSpec((1, LANE), lambda i: (0, 0))],
        out_shape=[jax.ShapeDtypeStruct((t, d), F32), jax.ShapeDtypeStruct((1, d), F32),
                   jax.ShapeDtypeStruct((1, LANE), F32)],
        compiler_params=_params(("arbitrary",)),
    )(h, target, w)


def _split3(x):
    hi = x.astype(BF16)
    r1 = x - hi.astype(F32)
    mid = r1.astype(BF16)
    lo = (r1 - mid.astype(F32)).astype(BF16)
    return hi, mid, lo


def _dot01_left(m01, x):
    return sum(jnp.dot(m01, p, preferred_element_type=F32) for p in _split3(x))


def _dot01_right(x, m01):
    return sum(jnp.dot(p, m01, preferred_element_type=F32) for p in _split3(x))


def rotary(xs_list, cosf, sinf, scale, *, adjoint, name, ts=512):
    b, h, s, c = xs_list[0].shape
    n_x = len(xs_list)

    def body(*refs):
        x = refs[0][0, 0]
        for r in refs[1:n_x]:
            x = x + r[0, 0]
        cos_v, sin_v = refs[n_x][0], refs[n_x + 1][0]
        o_ref = refs[n_x + 2]
        ci = lax.broadcasted_iota(jnp.int32, (c, c), 0)
        cj = lax.broadcasted_iota(jnp.int32, (c, c), 1)
        swap = ((cj == ci + ROPE_HALF) & (ci < ROPE_HALF)) | ((cj == ci - ROPE_HALF) & (ci >= ROPE_HALF) & (ci < ROPE_DIM))
        swap = swap.astype(BF16)
        if adjoint:
            out = x * cos_v + _dot01_right(x * sin_v, swap)
        else:
            out = x * cos_v + _dot01_right(x, swap) * sin_v
        o_ref[0, 0] = out * scale

    x_spec = pl.BlockSpec((1, 1, ts, c), lambda bi, hi, si: (bi, hi, si, 0))
    t_spec = pl.BlockSpec((1, ts, c), lambda bi, hi, si: (bi, si, 0))
    return pl.pallas_call(
        body, name=name, grid=(b, h, s // ts), in_specs=[x_spec] * n_x + [t_spec, t_spec], out_specs=x_spec,
        out_shape=jax.ShapeDtypeStruct((b, h, s, c), F32),
        compiler_params=_params(("parallel", "parallel", "parallel")),
    )(*xs_list, cosf, sinf)


def add3(a, b, c, *, name, tr=1024):
    def fn(x, y, z):
        return (x + y + z,)
    return rowwise_fwd(fn, [a, b, c], [], [F32], name=name, tr=tr)[0]


def _attn_mask(n):
    rows = GQA * ATTN_BLOCK
    qi = lax.broadcasted_iota(jnp.int32, (rows, 2 * ATTN_BLOCK), 0) % ATTN_BLOCK
    ki = lax.broadcasted_iota(jnp.int32, (rows, 2 * ATTN_BLOCK), 1)
    delta = qi + ATTN_BLOCK - ki
    return (delta >= 0) & (delta <= ATTN_BLOCK) & ((n - 1) * ATTN_BLOCK + ki >= 0)


def _attn_specs(l):
    q_spec = pl.BlockSpec((1, GQA, ATTN_BLOCK, HEAD_DIM), lambda p, n: (p, 0, n, 0))
    l_spec = pl.BlockSpec((1, GQA, ATTN_BLOCK, 1), lambda p, n: (p, 0, n, 0))
    kprev = pl.BlockSpec((1, ATTN_BLOCK, HEAD_DIM), lambda p, n: (p, jnp.maximum(n - 1, 0), 0))
    kcur = pl.BlockSpec((1, ATTN_BLOCK, HEAD_DIM), lambda p, n: (p, n, 0))
    kfull = pl.BlockSpec((1, l, HEAD_DIM), lambda p, n: (p, 0, 0))
    return q_spec, l_spec, kprev, kcur, kfull


def attn_branch_fwd(q, k, v, *, name):
    p_cnt, _, l, _ = q.shape
    rows = GQA * ATTN_BLOCK
    q_spec, l_spec, kprev, kcur, _ = _attn_specs(l)

    def body(q_ref, kp_ref, kc_ref, vp_ref, vc_ref, o_ref, lse_ref):
        n = pl.program_id(1)
        qv = q_ref[0].reshape(rows, HEAD_DIM).astype(BF16)
        kk = jnp.concatenate([kp_ref[0], kc_ref[0]], axis=0).astype(BF16)
        vv = jnp.concatenate([vp_ref[0], vc_ref[0]], axis=0).astype(BF16)
        s = lax.dot_general(qv, kk, (((1,), (1,)), ((), ())), preferred_element_type=F32)
        s = jnp.where(_attn_mask(n), s, NEG_BIG)
        m = jnp.max(s, axis=-1, keepdims=True)
        pr = jnp.exp(s - m)
        den = jnp.sum(pr, axis=-1, keepdims=True)
        o = jnp.dot(pr.astype(BF16), vv, preferred_element_type=F32) / den
        o_ref[0] = o.reshape(GQA, ATTN_BLOCK, HEAD_DIM)
        lse_ref[0] = (m + jnp.log(den)).reshape(GQA, ATTN_BLOCK, 1)

    return pl.pallas_call(
        body, name=name, grid=(p_cnt, l // ATTN_BLOCK), in_specs=[q_spec, kprev, kcur, kprev, kcur],
        out_specs=[q_spec, l_spec],
        out_shape=[jax.ShapeDtypeStruct(q.shape, F32), jax.ShapeDtypeStruct(q.shape[:3] + (1,), F32)],
        compiler_params=_params(("parallel", "arbitrary")),
    )(q, k, k, v, v)


def attn_branch_bwd(q, k, v, o, lse, do, dlse, *, name):
    p_cnt, _, l, _ = q.shape
    rows = GQA * ATTN_BLOCK
    q_spec, l_spec, kprev, kcur, kfull = _attn_specs(l)

    def body(q_ref, kp_ref, kc_ref, vp_ref, vc_ref, o_ref, lse_ref, do_ref, dlse_ref, dq_ref, dk_ref, dv_ref):
        n = pl.program_id(1)

        @pl.when(n == 0)
        def _():
            dk_ref[...] = jnp.zeros_like(dk_ref)
            dv_ref[...] = jnp.zeros_like(dv_ref)

        qv = q_ref[0].reshape(rows, HEAD_DIM).astype(BF16)
        kk = jnp.concatenate([kp_ref[0], kc_ref[0]], axis=0).astype(BF16)
        vv = jnp.concatenate([vp_ref[0], vc_ref[0]], axis=0).astype(BF16)
        ov = o_ref[0].reshape(rows, HEAD_DIM)
        dov = do_ref[0].reshape(rows, HEAD_DIM)
        lsev = lse_ref[0].reshape(rows, 1)
        dlsev = dlse_ref[0].reshape(rows, 1)
        s = lax.dot_general(qv, kk, (((1,), (1,)), ((), ())), preferred_element_type=F32)
        pr = jnp.where(_attn_mask(n), jnp.exp(s - lsev), 0.0)
        do16 = dov.astype(BF16)
        dv = lax.dot_general(pr.astype(BF16), do16, (((0,), (0,)), ((), ())), preferred_element_type=F32)
        dp = lax.dot_general(do16, vv, (((1,), (1,)), ((), ())), preferred_element_type=F32)
        delta = jnp.sum(dov * ov, axis=-1, keepdims=True)
        ds = (pr * (dp - delta + dlsev)).astype(BF16)
        dq = jnp.dot(ds, kk, preferred_element_type=F32)
        dk = lax.dot_general(ds, qv, (((0,), (0,)), ((), ())), preferred_element_type=F32)
        dq_ref[0] = dq.reshape(GQA, ATTN_BLOCK, HEAD_DIM)
        cur = pl.ds(pl.multiple_of(n * ATTN_BLOCK, ATTN_BLOCK), ATTN_BLOCK)
        dk_ref[0, cur, :] += dk[ATTN_BLOCK:]
        dv_ref[0, cur, :] += dv[ATTN_BLOCK:]

        @pl.when(n > 0)
        def _():
            prev = pl.ds(pl.multiple_of((n - 1) * ATTN_BLOCK, ATTN_BLOCK), ATTN_BLOCK)
            dk_ref[0, prev, :] += dk[:ATTN_BLOCK]
            dv_ref[0, prev, :] += dv[:ATTN_BLOCK]

    return pl.pallas_call(
        body, name=name, grid=(p_cnt, l // ATTN_BLOCK),
        in_specs=[q_spec, kprev, kcur, kprev, kcur, q_spec, l_spec, q_spec, l_spec],
        out_specs=[q_spec, kfull, kfull],
        out_shape=[jax.ShapeDtypeStruct(q.shape, F32), jax.ShapeDtypeStruct(k.shape, F32),
                   jax.ShapeDtypeStruct(v.shape, F32)],
        compiler_params=_params(("parallel", "arbitrary")),
    )(q, k, k, v, v, o, lse, do, dlse)


ATTN_PAD = ATTN_BLOCK * DILATIONS[-1]
Q_GROUP_W = GQA * HEAD_DIM
ATTN_VMEM_LIMIT = 56 * 1024 * 1024


def _rope(x, cos_v, sin_v, swap, scale, adjoint):
    if adjoint:
        return (x * cos_v + _dot01_right(x * sin_v, swap)) * scale
    return (x * cos_v + _dot01_right(x, swap) * sin_v) * scale


def _swap_matrix():
    c = HEAD_DIM
    ci = lax.broadcasted_iota(jnp.int32, (c, c), 0)
    cj = lax.broadcasted_iota(jnp.int32, (c, c), 1)
    swap = ((cj == ci + ROPE_HALF) & (ci < ROPE_HALF)) | ((cj == ci - ROPE_HALF) & (ci >= ROPE_HALF) & (ci < ROPE_DIM))
    return swap.astype(BF16)


def _attn_prologue(q_ref, kv_ref, tab_ref, q_s, k_s, v_s, hk, s_len):
    swap = _swap_matrix()
    cos_v, sin_v = tab_ref[0, :, :HEAD_DIM], tab_ref[0, :, HEAD_DIM:]
    for g in range(GQA):
        cols = slice(g * HEAD_DIM, (g + 1) * HEAD_DIM)
        q_s[:, cols] = _rope(q_ref[0, :, cols], cos_v, sin_v, swap, HEAD_DIM ** -0.5, False)
    zeros = jnp.zeros((ATTN_PAD, HEAD_DIM), F32)
    k_s[0:ATTN_PAD, :] = zeros
    v_s[0:ATTN_PAD, :] = zeros
    for h in range(N_KV_HEADS):
        @pl.when(hk == h)
        def _():
            k_s[ATTN_PAD:ATTN_PAD + s_len, :] = _rope(kv_ref[0, :, h * HEAD_DIM:(h + 1) * HEAD_DIM], cos_v, sin_v, swap, 1.0, False)
            v_s[ATTN_PAD:ATTN_PAD + s_len, :] = kv_ref[0, :, LANE + h * HEAD_DIM:LANE + (h + 1) * HEAD_DIM]


def _attn_blocks(s_len):
    out = []
    for i, d in enumerate(DILATIONS):
        nb = s_len // (ATTN_BLOCK * d)
        for r in range(d):
            for n in range(nb):
                start = r + d * ATTN_BLOCK * n
                out.append((i, d, start, ATTN_PAD + start - d * ATTN_BLOCK, n))
    return out


def _rows(start, size, d):
    return pl.ds(start, size, stride=d) if d > 1 else pl.ds(start, size)


def _stack_heads(blk):
    return jnp.concatenate([blk[:, g * HEAD_DIM:(g + 1) * HEAD_DIM] for g in range(GQA)], axis=0)


def _stack_stats(blk):
    return jnp.concatenate([jnp.max(blk[:, g * HEAD_DIM:(g + 1) * HEAD_DIM], axis=1, keepdims=True) for g in range(GQA)], axis=0)


def _attn_in_specs(s_len):
    assert K_COL % (2 * LANE) == 0 and V_COL == K_COL + LANE
    q_spec = pl.BlockSpec((1, s_len, Q_GROUP_W), lambda b, h: (b, 0, Q_COL // Q_GROUP_W + h))
    kv_spec = pl.BlockSpec((1, s_len, 2 * LANE), lambda b, h: (b, 0, K_COL // (2 * LANE)))
    t_spec = pl.BlockSpec((1, s_len, 2 * HEAD_DIM), lambda b, h: (b, 0, 0))
    o_spec = pl.BlockSpec((1, s_len, Q_GROUP_W), lambda b, h: (b, 0, h))
    return q_spec, kv_spec, t_spec, o_spec


def attn_fwd(proj3, rope_tab, *, name):
    b, s_len, _ = proj3.shape
    q_spec, kv_spec, t_spec, o_spec = _attn_in_specs(s_len)
    n_br = len(DILATIONS)

    def body(q_ref, kv_ref, tab_ref, o_ref, lse_ref, q_s, k_s, v_s, *branch_s):
        o_s, l_s = branch_s[:n_br], branch_s[n_br:]
        _attn_prologue(q_ref, kv_ref, tab_ref, q_s, k_s, v_s, pl.program_id(1), s_len)
        for i, d, q0, k0, n in _attn_blocks(s_len):
            qv = _stack_heads(q_s[_rows(q0, ATTN_BLOCK, d), :]).astype(BF16)
            kk = k_s[_rows(k0, 2 * ATTN_BLOCK, d), :].astype(BF16)
            vv = v_s[_rows(k0, 2 * ATTN_BLOCK, d), :].astype(BF16)
            sc = lax.dot_general(qv, kk, (((1,), (1,)), ((), ())), preferred_element_type=F32)
            sc = jnp.where(_attn_mask(n), sc, NEG_BIG)
            m = jnp.max(sc, axis=-1, keepdims=True)
            pr = jnp.exp(sc - m)
            den = jnp.sum(pr, axis=-1, keepdims=True)
            o = jnp.dot(pr.astype(BF16), vv, preferred_element_type=F32) / den
            lse = m + jnp.log(den)
            for g in range(GQA):
                part = slice(g * ATTN_BLOCK, (g + 1) * ATTN_BLOCK)
                o_s[i][_rows(q0, ATTN_BLOCK, d), g * HEAD_DIM:(g + 1) * HEAD_DIM] = o[part]
                l_s[i][_rows(q0, ATTN_BLOCK, d), g * HEAD_DIM:(g + 1) * HEAD_DIM] = jnp.broadcast_to(lse[part], (ATTN_BLOCK, HEAD_DIM))
        step = 256
        for t0 in range(0, s_len, step):
            rs = pl.ds(t0, step)
            for g in range(GQA):
                ls = [l_s[i][rs, g * HEAD_DIM:(g + 1) * HEAD_DIM] for i in range(n_br)]
                m = functools.reduce(jnp.maximum, ls)
                es = [jnp.exp(l - m) for l in ls]
                tot = functools.reduce(lambda a, c: a + c, es)
                inv = 1.0 / tot
                acc = None
                for i in range(n_br):
                    term = (es[i] * inv) * o_s[i][rs, g * HEAD_DIM:(g + 1) * HEAD_DIM]
                    acc = term if acc is None else acc + term
                o_ref[0, rs, g * HEAD_DIM:(g + 1) * HEAD_DIM] = acc
                lse_ref[0, rs, g * HEAD_DIM:(g + 1) * HEAD_DIM] = m + jnp.log(tot)

    return pl.pallas_call(
        body, name=name, grid=(b, N_KV_HEADS), in_specs=[q_spec, kv_spec, t_spec],
        out_specs=[o_spec, o_spec],
        out_shape=[jax.ShapeDtypeStruct((b, s_len, ATTN_WIDTH), F32)] * 2,
        scratch_shapes=[pltpu.VMEM((s_len, Q_GROUP_W), F32), pltpu.VMEM((ATTN_PAD + s_len, HEAD_DIM), F32),
                        pltpu.VMEM((ATTN_PAD + s_len, HEAD_DIM), F32)] + [pltpu.VMEM((s_len, Q_GROUP_W), F32)] * (2 * n_br),
        compiler_params=pltpu.CompilerParams(dimension_semantics=("arbitrary", "arbitrary"), vmem_limit_bytes=ATTN_VMEM_LIMIT),
    )(proj3, proj3, rope_tab)


def attn_bwd(proj3, rope_tab, attn3, lse3, d_attn3, *, name):
    b, s_len, _ = proj3.shape
    q_spec, kv_spec, t_spec, o_spec = _attn_in_specs(s_len)
    kv_out = pl.BlockSpec((1, 1, s_len, HEAD_DIM), lambda bi, h: (bi, h, 0, 0))

    def body(q_ref, kv_ref, tab_ref, o_ref, lse_ref, do_ref, dq_ref, dk_ref, dv_ref,
             q_s, k_s, v_s, dl_s, dq_s, dk_s, dv_s):
        _attn_prologue(q_ref, kv_ref, tab_ref, q_s, k_s, v_s, pl.program_id(1), s_len)
        dq_s[...] = jnp.zeros_like(dq_s)
        dk_s[...] = jnp.zeros_like(dk_s)
        dv_s[...] = jnp.zeros_like(dv_s)
        for g in range(GQA):
            cols = slice(g * HEAD_DIM, (g + 1) * HEAD_DIM)
            delta = jnp.sum(do_ref[0, :, cols] * o_ref[0, :, cols], axis=1, keepdims=True)
            dl_s[:, cols] = jnp.broadcast_to(delta, (s_len, HEAD_DIM))
        for i, d, q0, k0, n in _attn_blocks(s_len):
            qrows, krows = _rows(q0, ATTN_BLOCK, d), _rows(k0, 2 * ATTN_BLOCK, d)
            qv = _stack_heads(q_s[qrows, :]).astype(BF16)
            kk = k_s[krows, :].astype(BF16)
            vv = v_s[krows, :].astype(BF16)
            do16 = _stack_heads(do_ref.at[0][qrows, :]).astype(BF16)
            lse = _stack_stats(lse_ref.at[0][qrows, :])
            delta = _stack_stats(dl_s[qrows, :])
            sc = lax.dot_general(qv, kk, (((1,), (1,)), ((), ())), preferred_element_type=F32)
            pr = jnp.where(_attn_mask(n), jnp.exp(sc - lse), 0.0)
            dv = lax.dot_general(pr.astype(BF16), do16, (((0,), (0,)), ((), ())), preferred_element_type=F32)
            dp = lax.dot_general(do16, vv, (((1,), (1,)), ((), ())), preferred_element_type=F32)
            ds = (pr * (dp - delta)).astype(BF16)
            dq = jnp.dot(ds, kk, preferred_element_type=F32)
            dk = lax.dot_general(ds, qv, (((0,), (0,)), ((), ())), preferred_element_type=F32)
            for g in range(GQA):
                cols = slice(g * HEAD_DIM, (g + 1) * HEAD_DIM)
                dq_s[qrows, cols] += dq[g * ATTN_BLOCK:(g + 1) * ATTN_BLOCK]
            dk_s[krows, :] += dk
            dv_s[krows, :] += dv
        swap = _swap_matrix()
        cos_v, sin_v = tab_ref[0, :, :HEAD_DIM], tab_ref[0, :, HEAD_DIM:]
        for g in range(GQA):
            cols = slice(g * HEAD_DIM, (g + 1) * HEAD_DIM)
            dq_ref[0, :, cols] = _rope(dq_s[:, cols], cos_v, sin_v, swap, HEAD_DIM ** -0.5, True)
        dk_ref[0, 0] = _rope(dk_s[ATTN_PAD:ATTN_PAD + s_len, :], cos_v, sin_v, swap, 1.0, True)
        dv_ref[0, 0] = dv_s[ATTN_PAD:ATTN_PAD + s_len, :]

    kv_shape = jax.ShapeDtypeStruct((b, N_KV_HEADS, s_len, HEAD_DIM), F32)
    return pl.pallas_call(
        body, name=name, grid=(b, N_KV_HEADS),
        in_specs=[q_spec, kv_spec, t_spec, o_spec, o_spec, o_spec],
        out_specs=[o_spec, kv_out, kv_out],
        out_shape=[jax.ShapeDtypeStruct((b, s_len, ATTN_WIDTH), F32), kv_shape, kv_shape],
        scratch_shapes=[pltpu.VMEM((s_len, Q_GROUP_W), F32), pltpu.VMEM((ATTN_PAD + s_len, HEAD_DIM), F32),
                        pltpu.VMEM((ATTN_PAD + s_len, HEAD_DIM), F32), pltpu.VMEM((s_len, Q_GROUP_W), F32),
                        pltpu.VMEM((s_len, Q_GROUP_W), F32), pltpu.VMEM((ATTN_PAD + s_len, HEAD_DIM), F32),
                        pltpu.VMEM((ATTN_PAD + s_len, HEAD_DIM), F32)],
        compiler_params=pltpu.CompilerParams(dimension_semantics=("arbitrary", "arbitrary"), vmem_limit_bytes=ATTN_VMEM_LIMIT),
    )(proj3, proj3, rope_tab, attn3, lse3, d_attn3)


HALF_W = 2 * HEAD_DIM
N_HALF = Q_GROUP_W // HALF_W
_ATTN_BIAS_BUF = pltpu.VMEM((2, GQA * ATTN_BLOCK, 2 * ATTN_BLOCK), F32)


def _attn_bias(bias_s):
    for first in (0, 1):
        bias_s[first] = jnp.where(_attn_mask(first), 0.0, NEG_BIG)


def _attn_prologue(q_refs, kv_ref, tab_ref, q_s, kv_s, hk, s_len):
    swap = _swap_matrix()
    cos_v, sin_v = tab_ref[0, :, :HEAD_DIM], tab_ref[0, :, HEAD_DIM:]
    for j in range(N_HALF):
        for e in range(2):
            cols = slice(e * HEAD_DIM, (e + 1) * HEAD_DIM)
            q_s[j][:, cols] = _rope(q_refs[j][0, :, cols], cos_v, sin_v, swap, HEAD_DIM ** -0.5, False)
    kv_s[0:ATTN_PAD, :] = jnp.zeros((ATTN_PAD, HALF_W), F32)
    for h in range(N_KV_HEADS):
        @pl.when(hk == h)
        def _():
            kv_s[ATTN_PAD:ATTN_PAD + s_len, :HEAD_DIM] = _rope(kv_ref[0, :, h * HEAD_DIM:(h + 1) * HEAD_DIM], cos_v, sin_v,
                                                               swap, 1.0, False)
            kv_s[ATTN_PAD:ATTN_PAD + s_len, HEAD_DIM:] = kv_ref[0, :, LANE + h * HEAD_DIM:LANE + (h + 1) * HEAD_DIM]


def _stack_heads(halves):
    return jnp.concatenate([h[:, e * HEAD_DIM:(e + 1) * HEAD_DIM] for h in halves for e in range(2)], axis=0)


def _unstack_heads(x, j):
    return jnp.concatenate([x[(2 * j + e) * ATTN_BLOCK:(2 * j + e + 1) * ATTN_BLOCK] for e in range(2)], axis=1)


def _stack_stats(halves):
    return jnp.concatenate([jnp.max(h[:, e * HEAD_DIM:(e + 1) * HEAD_DIM], axis=1, keepdims=True)
                            for h in halves for e in range(2)], axis=0)


def _attn_in_specs(s_len):
    assert K_COL % (2 * LANE) == 0 and V_COL == K_COL + LANE

    def halves(first_tile):
        return [pl.BlockSpec((1, s_len, HALF_W), functools.partial(lambda b, h, j: (b, 0, first_tile + N_HALF * h + j), j=j))
                for j in range(N_HALF)]

    kv_spec = pl.BlockSpec((1, s_len, 2 * LANE), lambda b, h: (b, 0, K_COL // (2 * LANE)))
    t_spec = pl.BlockSpec((1, s_len, 2 * HEAD_DIM), lambda b, h: (b, 0, 0))
    o_spec = pl.BlockSpec((1, s_len, Q_GROUP_W), lambda b, h: (b, 0, h))
    return halves(Q_COL // HALF_W), kv_spec, t_spec, o_spec, halves(0)


class SideCopy:
    def __init__(self, side, *, n_in, n_out, grid):
        self.side, self.n_in, self.n_out, self.grid = side, n_in, n_out, grid
        hbm = pl.BlockSpec(memory_space=pltpu.HBM)
        if side is None:
            self.in_specs, self.out_specs, self.out_shape, self.scratch, self.args = [], [], [], [], []
            return
        src, per_dest = side
        shape = src.shape if per_dest else (N_CHIPS,) + src.shape
        self.in_specs, self.out_specs, self.args = [hbm], [hbm], [src]
        self.out_shape = [jax.ShapeDtypeStruct(shape, src.dtype)]
        self.scratch = [pltpu.SemaphoreType.DMA((N_CHIPS - 1,)), pltpu.SemaphoreType.DMA((N_CHIPS - 1,)), pltpu.SemaphoreType.DMA]

    def wrap(self, body):
        if self.side is None:
            return body
        n_in, n_out, grid, per_dest = self.n_in, self.n_out, self.grid, self.side[1]

        def wrapped(*refs):
            ins, src = refs[:n_in], refs[n_in]
            outs, dst = refs[n_in + 1:n_in + 1 + n_out], refs[n_in + 1 + n_out]
            scratch, sems = refs[n_in + 2 + n_out:-3], refs[-3:]
            ids = [pl.program_id(a) for a in range(len(grid))]
            first = functools.reduce(lambda p, q: p & q, [i == 0 for i in ids])
            last = functools.reduce(lambda p, q: p & q, [i == g - 1 for i, g in zip(ids, grid)])

            @pl.when(first)
            def _():
                local, sends, _ = _chip_copies(src, dst, *sems, per_dest)
                local.start()
                for cp in sends:
                    cp.start()

            body(*ins, *outs, *scratch)

            @pl.when(last)
            def _():
                local, sends, recvs = _chip_copies(src, dst, *sems, per_dest)
                for cp in recvs:
                    cp.wait_recv()
                for cp in sends:
                    cp.wait_send()
                local.wait()

        return wrapped


def _chip_copies(src_ref, dst_ref, send_sems, recv_sems, local_sem, per_dest):
    x, y, c = lax.axis_index("x"), lax.axis_index("y"), lax.axis_index("c")
    chip = 2 * x + y
    own = src_ref.at[chip] if per_dest else src_ref
    local = pltpu.make_async_copy(own, dst_ref.at[chip], local_sem)
    sends, recvs = [], []
    for k, (px, py) in enumerate([(1 - x, y), (x, 1 - y), (1 - x, 1 - y)]):
        peer = dict(send_sem=send_sems.at[k], recv_sem=recv_sems.at[k], device_id=(px, py, c), device_id_type=MESH)
        sends.append(pltpu.make_async_remote_copy(src_ref=src_ref.at[2 * px + py] if per_dest else src_ref,
                                                  dst_ref=dst_ref.at[chip], **peer))
        recvs.append(pltpu.make_async_remote_copy(src_ref=own, dst_ref=dst_ref.at[2 * px + py], **peer))
    return local, sends, recvs


def attn_fwd(proj3, rope_tab, *, name, side=None):
    b, s_len, _ = proj3.shape
    q_specs, kv_spec, t_spec, o_spec, _ = _attn_in_specs(s_len)
    n_br = len(DILATIONS)

    def body(*refs):
        q_refs, (kv_ref, tab_ref, o_ref, lse_ref) = refs[:N_HALF], refs[N_HALF:N_HALF + 4]
        scratch = refs[N_HALF + 4:]
        q_s, kv_s = scratch[:N_HALF], scratch[N_HALF]
        o_s = [scratch[N_HALF + 1 + i * N_HALF:N_HALF + 1 + (i + 1) * N_HALF] for i in range(n_br)]
        l_s = [scratch[N_HALF + 1 + (n_br + i) * N_HALF:N_HALF + 1 + (n_br + i + 1) * N_HALF] for i in range(n_br)]
        bias_s = scratch[-1]
        _attn_prologue(q_refs, kv_ref, tab_ref, q_s, kv_s, pl.program_id(1), s_len)
        _attn_bias(bias_s)
        for i, d, q0, k0, n in _attn_blocks(s_len):
            qrows = _rows(q0, ATTN_BLOCK, d)
            qv = _stack_heads([q_s[j][qrows, :] for j in range(N_HALF)]).astype(BF16)
            kvb = kv_s[_rows(k0, 2 * ATTN_BLOCK, d), :].astype(BF16)
            kk, vv = kvb[:, :HEAD_DIM], kvb[:, HEAD_DIM:]
            sc = lax.dot_general(qv, kk, (((1,), (1,)), ((), ())), preferred_element_type=F32)
            sc = sc + bias_s[min(n, 1)]
            m = jnp.max(sc, axis=-1, keepdims=True)
            pr = jnp.exp(sc - m)
            den = jnp.sum(pr, axis=-1, keepdims=True)
            o = jnp.dot(pr.astype(BF16), vv, preferred_element_type=F32) / den
            lse_b = jnp.broadcast_to(m + jnp.log(den), (GQA * ATTN_BLOCK, HEAD_DIM))
            for j in range(N_HALF):
                o_s[i][j][qrows, :] = _unstack_heads(o, j)
                l_s[i][j][qrows, :] = _unstack_heads(lse_b, j)
        step = 256
        for t0 in range(0, s_len, step):
            rs = pl.ds(t0, step)
            for j in range(N_HALF):
                ls = [l_s[i][j][rs, :] for i in range(n_br)]
                m = functools.reduce(jnp.maximum, ls)
                es = [jnp.exp(l - m) for l in ls]
                tot = functools.reduce(lambda a, c: a + c, es)
                inv = 1.0 / tot
                acc = None
                for i in range(n_br):
                    term = (es[i] * inv) * o_s[i][j][rs, :]
                    acc = term if acc is None else acc + term
                o_ref[0, rs, j * HALF_W:(j + 1) * HALF_W] = acc
                lse_ref[0, rs, j * HALF_W:(j + 1) * HALF_W] = m + jnp.log(tot)

    half_buf = pltpu.VMEM((s_len, HALF_W), F32)
    call = SideCopy(side, n_in=N_HALF + 2, n_out=2, grid=(b, N_KV_HEADS))
    return pl.pallas_call(
        call.wrap(body), name=name, grid=(b, N_KV_HEADS), in_specs=q_specs + [kv_spec, t_spec] + call.in_specs,
        out_specs=[o_spec, o_spec] + call.out_specs,
        out_shape=[jax.ShapeDtypeStruct((b, s_len, ATTN_WIDTH), F32)] * 2 + call.out_shape,
        scratch_shapes=[half_buf] * N_HALF + [pltpu.VMEM((ATTN_PAD + s_len, HALF_W), F32)] + [half_buf] * (2 * n_br * N_HALF)
        + [_ATTN_BIAS_BUF] + call.scratch,
        compiler_params=pltpu.CompilerParams(dimension_semantics=("arbitrary", "arbitrary"), vmem_limit_bytes=ATTN_VMEM_LIMIT),
    )(*([proj3] * (N_HALF + 1)), rope_tab, *call.args)


def attn_bwd(proj3, rope_tab, attn3, lse3, d_attn3, *, name, side=None):
    b, s_len, _ = proj3.shape
    q_specs, kv_spec, t_spec, o_spec, half_specs = _attn_in_specs(s_len)
    kv_out = pl.BlockSpec((1, 1, s_len, HEAD_DIM), lambda bi, h: (bi, h, 0, 0))

    def body(*refs):
        q_refs = refs[:N_HALF]
        kv_ref, tab_ref, o_ref = refs[N_HALF:N_HALF + 3]
        lse_refs = refs[N_HALF + 3:2 * N_HALF + 3]
        do_refs = refs[2 * N_HALF + 3:3 * N_HALF + 3]
        dq_ref, dk_ref, dv_ref = refs[3 * N_HALF + 3:3 * N_HALF + 6]
        scratch = refs[3 * N_HALF + 6:]
        q_s, kv_s = scratch[:N_HALF], scratch[N_HALF]
        dl_s = scratch[N_HALF + 1:2 * N_HALF + 1]
        dq_s = scratch[2 * N_HALF + 1:3 * N_HALF + 1]
        dkv_s = scratch[3 * N_HALF + 1]
        bias_s = scratch[-1]
        _attn_prologue(q_refs, kv_ref, tab_ref, q_s, kv_s, pl.program_id(1), s_len)
        _attn_bias(bias_s)
        dkv_s[...] = jnp.zeros_like(dkv_s)
        for j in range(N_HALF):
            dq_s[j][...] = jnp.zeros_like(dq_s[j])
            for e in range(2):
                cols = slice(e * HEAD_DIM, (e + 1) * HEAD_DIM)
                ocols = slice(j * HALF_W + e * HEAD_DIM, j * HALF_W + (e + 1) * HEAD_DIM)
                delta = jnp.sum(do_refs[j][0, :, cols] * o_ref[0, :, ocols], axis=1, keepdims=True)
                dl_s[j][:, cols] = jnp.broadcast_to(delta, (s_len, HEAD_DIM))
        for i, d, q0, k0, n in _attn_blocks(s_len):
            qrows, krows = _rows(q0, ATTN_BLOCK, d), _rows(k0, 2 * ATTN_BLOCK, d)
            qv = _stack_heads([q_s[j][qrows, :] for j in range(N_HALF)]).astype(BF16)
            kvb = kv_s[krows, :].astype(BF16)
            kk, vv = kvb[:, :HEAD_DIM], kvb[:, HEAD_DIM:]
            do16 = _stack_heads([do_refs[j].at[0][qrows, :] for j in range(N_HALF)]).astype(BF16)
            lse = _stack_stats([lse_refs[j].at[0][qrows, :] for j in range(N_HALF)])
            delta = _stack_stats([dl_s[j][qrows, :] for j in range(N_HALF)])
            sc = lax.dot_general(qv, kk, (((1,), (1,)), ((), ())), preferred_element_type=F32)
            pr = jnp.exp(sc + bias_s[min(n, 1)] - lse)
            dv = lax.dot_general(pr.astype(BF16), do16, (((0,), (0,)), ((), ())), preferred_element_type=F32)
            dp = lax.dot_general(do16, vv, (((1,), (1,)), ((), ())), preferred_element_type=F32)
            ds = (pr * (dp - delta)).astype(BF16)
            dq = jnp.dot(ds, kk, preferred_element_type=F32)
            dk = lax.dot_general(ds, qv, (((0,), (0,)), ((), ())), preferred_element_type=F32)
            for j in range(N_HALF):
                dq_s[j][qrows, :] += _unstack_heads(dq, j)
            dkv_s[krows, :] += jnp.concatenate([dk, dv], axis=1)
        swap = _swap_matrix()
        cos_v, sin_v = tab_ref[0, :, :HEAD_DIM], tab_ref[0, :, HEAD_DIM:]
        for j in range(N_HALF):
            for e in range(2):
                cols = slice(e * HEAD_DIM, (e + 1) * HEAD_DIM)
                ocols = slice(j * HALF_W + e * HEAD_DIM, j * HALF_W + (e + 1) * HEAD_DIM)
                dq_ref[0, :, ocols] = _rope(dq_s[j][:, cols], cos_v, sin_v, swap, HEAD_DIM ** -0.5, True)
        dk_ref[0, 0] = _rope(dkv_s[ATTN_PAD:ATTN_PAD + s_len, :HEAD_DIM], cos_v, sin_v, swap, 1.0, True)
        dv_ref[0, 0] = dkv_s[ATTN_PAD:ATTN_PAD + s_len, HEAD_DIM:]

    kv_shape = jax.ShapeDtypeStruct((b, N_KV_HEADS, s_len, HEAD_DIM), F32)
    half_buf = pltpu.VMEM((s_len, HALF_W), F32)
    pad_buf = pltpu.VMEM((ATTN_PAD + s_len, HALF_W), F32)
    call = SideCopy(side, n_in=3 * N_HALF + 3, n_out=3, grid=(b, N_KV_HEADS))
    return pl.pallas_call(
        call.wrap(body), name=name, grid=(b, N_KV_HEADS),
        in_specs=q_specs + [kv_spec, t_spec, o_spec] + half_specs + half_specs + call.in_specs,
        out_specs=[o_spec, kv_out, kv_out] + call.out_specs,
        out_shape=[jax.ShapeDtypeStruct((b, s_len, ATTN_WIDTH), F32), kv_shape, kv_shape] + call.out_shape,
        scratch_shapes=[half_buf] * N_HALF + [pad_buf] + [half_buf] * (2 * N_HALF) + [pad_buf, _ATTN_BIAS_BUF] + call.scratch,
        compiler_params=pltpu.CompilerParams(dimension_semantics=("arbitrary", "arbitrary"), vmem_limit_bytes=ATTN_VMEM_LIMIT),
    )(*([proj3] * (N_HALF + 1)), rope_tab, attn3, *([lse3] * N_HALF), *([d_attn3] * N_HALF), *call.args)


CONV_TC = 256
CONV_COL0 = XBC_COL // CONV_TC


def _shift_down(u, s):
    if s == 0:
        return u
    rows = lax.broadcasted_iota(jnp.int32, u.shape, 0)
    return jnp.where(rows >= s, pltpu.roll(u, s, 0), 0.0)


def _shift_up(u, s):
    if s == 0:
        return u
    n = u.shape[0]
    rows = lax.broadcasted_iota(jnp.int32, u.shape, 0)
    return jnp.where(rows < n - s, pltpu.roll(u, n - s, 0), 0.0)


def conv_silu_fwd(proj3, w, bias, *, name):
    b, s, _ = proj3.shape
    u_spec = pl.BlockSpec((1, s, CONV_TC), lambda j, bi: (bi, 0, CONV_COL0 + j))
    o_spec = pl.BlockSpec((1, s, CONV_TC), lambda j, bi: (bi, 0, j))
    w_spec = pl.BlockSpec((CONV_WIDTH, CONV_TC), lambda j, bi: (0, j))
    b_spec = pl.BlockSpec((1, CONV_TC), lambda j, bi: (0, j))

    def body(u_ref, w_ref, b_ref, o_ref):
        u = u_ref[0]
        y = jnp.broadcast_to(b_ref[...], u.shape)
        for k in range(CONV_WIDTH):
            y = y + w_ref[k:k + 1, :] * _shift_down(u, CONV_WIDTH - 1 - k)
        o_ref[0] = y * jax.nn.sigmoid(y)

    return pl.pallas_call(
        body, name=name, grid=(CONV_CH // CONV_TC, b), in_specs=[u_spec, w_spec, b_spec], out_specs=o_spec,
        out_shape=jax.ShapeDtypeStruct((b, s, CONV_CH), F32),
        compiler_params=_params(("parallel", "arbitrary")),
    )(proj3, w, bias)


def conv_silu_bwd(proj3, w, bias, dact, *, name):
    b, s, _ = proj3.shape
    u_spec = pl.BlockSpec((1, s, CONV_TC), lambda j, bi: (bi, 0, CONV_COL0 + j))
    o_spec = pl.BlockSpec((1, s, CONV_TC), lambda j, bi: (bi, 0, j))
    w_spec = pl.BlockSpec((CONV_WIDTH, CONV_TC), lambda j, bi: (0, j))
    b_spec = pl.BlockSpec((1, CONV_TC), lambda j, bi: (0, j))

    def body(u_ref, w_ref, b_ref, g_ref, du_ref, dw_ref, db_ref):
        bi = pl.program_id(1)

        @pl.when(bi == 0)
        def _():
            dw_ref[...] = jnp.zeros_like(dw_ref)
            db_ref[...] = jnp.zeros_like(db_ref)

        u = u_ref[0]
        y = jnp.broadcast_to(b_ref[...], u.shape)
        shifted = [_shift_down(u, CONV_WIDTH - 1 - k) for k in range(CONV_WIDTH)]
        for k in range(CONV_WIDTH):
            y = y + w_ref[k:k + 1, :] * shifted[k]
        sig = jax.nn.sigmoid(y)
        dy = g_ref[0] * (sig * (1.0 + y * (1.0 - sig)))
        du = jnp.zeros_like(u)
        for k in range(CONV_WIDTH):
            du = du + w_ref[k:k + 1, :] * _shift_up(dy, CONV_WIDTH - 1 - k)
            dw_ref[k:k + 1, :] += jnp.sum(dy * shifted[k], axis=0, keepdims=True)
        du_ref[0] = du
        db_ref[...] += jnp.sum(dy, axis=0, keepdims=True)

    return pl.pallas_call(
        body, name=name, grid=(CONV_CH // CONV_TC, b), in_specs=[u_spec, w_spec, b_spec, o_spec],
        out_specs=[o_spec, w_spec, b_spec],
        out_shape=[jax.ShapeDtypeStruct((b, s, CONV_CH), F32), jax.ShapeDtypeStruct((CONV_WIDTH, CONV_CH), F32),
                   jax.ShapeDtypeStruct((1, CONV_CH), F32)],
        compiler_params=_params(("parallel", "arbitrary")),
    )(proj3, w, bias, dact)


def _softplus(z):
    e = jnp.exp(-jnp.abs(z))
    u = 1.0 + e
    log1p = jnp.where(u == 1.0, e, jnp.log(u) * e / jnp.where(u == 1.0, 1.0, u - 1.0))
    return jnp.maximum(z, 0.0) + log1p


def _tri(lower):
    r = lax.broadcasted_iota(jnp.int32, (CHUNK, CHUNK), 0)
    c = lax.broadcasted_iota(jnp.int32, (CHUNK, CHUNK), 1)
    return (r >= c) if lower else (r <= c)


def _ssd_common(dtr_ref, dtb_ref, alog_ref):
    z = dtr_ref[0] + dtb_ref[...]
    dt = _softplus(z)
    aneg = -jnp.exp(alog_ref[...])
    acs = _dot01_left(_tri(True).astype(BF16), dt * aneg)
    return z, dt, aneg, acs


def _col(mat, onehot):
    return jnp.sum(mat * onehot, axis=1, keepdims=True)


def _ssd_head(x, dt_j, acs_j, cb, tri_mask, last_row, acs_row=None):
    acs_last = jnp.sum(acs_j * last_row, axis=0, keepdims=True)
    xg = x * dt_j
    bc = jnp.broadcast_to(acs_j, (CHUNK, CHUNK))
    dm = bc - (bc.T if acs_row is None else jnp.broadcast_to(acs_row, (CHUNK, CHUNK)))
    lm = jnp.where(tri_mask, jnp.exp(jnp.where(tri_mask, dm, 0.0)), 0.0)
    mm = cb * lm
    decay_s = jnp.exp(acs_last - acs_j)
    return acs_last, xg, lm, mm, decay_s


def _ssd_specs(nc, reverse):
    cidx = (lambda c: nc - 1 - c) if reverse else (lambda c: c)
    act_spec = pl.BlockSpec((1, CHUNK, CONV_CH), lambda b, c: (b, cidx(c), 0))
    y_spec = pl.BlockSpec((1, CHUNK, SSM_INNER), lambda b, c: (b, cidx(c), 0))
    dt_in_spec = pl.BlockSpec((1, CHUNK, LANE), lambda b, c: (b, cidx(c), DT_COL // LANE))
    dt_out_spec = pl.BlockSpec((1, CHUNK, LANE), lambda b, c: (b, cidx(c), 0))
    par_spec = pl.BlockSpec((1, LANE), lambda b, c: (0, 0))
    h_spec = pl.BlockSpec((1, SSM_HEADS, 1, SSM_P, D_STATE), lambda b, c: (b, 0, cidx(c), 0, 0))
    return act_spec, y_spec, dt_in_spec, dt_out_spec, par_spec, h_spec


def _head_cols(h):
    return slice(h * SSM_P, (h + 1) * SSM_P)


def _group_cols(g, which):
    start = SSM_INNER + which * SSM_GROUPS * D_STATE + g * D_STATE
    return slice(start, start + D_STATE)


def ssd_fwd(act3, proj3, dtb, alog, dsk, *, name):
    b, s, _ = act3.shape
    nc = s // CHUNK
    act_spec, y_spec, dt_in_spec, _, par_spec, h_spec = _ssd_specs(nc, False)

    def body(act_ref, dtr_ref, dtb_ref, alog_ref, dsk_ref, y_ref, hp_ref, state):
        c = pl.program_id(1)

        @pl.when(c == 0)
        def _():
            state[...] = jnp.zeros_like(state)

        _, dt, _, acs = _ssd_common(dtr_ref, dtb_ref, alog_ref)
        acs_t = acs.T
        tri_mask = _tri(True)
        last_row = (lax.broadcasted_iota(jnp.int32, (CHUNK, 1), 0) == CHUNK - 1).astype(F32)
        for g in range(SSM_GROUPS):
            b16 = act_ref[0, :, _group_cols(g, 0)].astype(BF16)
            c16 = act_ref[0, :, _group_cols(g, 1)].astype(BF16)
            cb = lax.dot_general(c16, b16, (((1,), (1,)), ((), ())), preferred_element_type=F32)
            for j in range(HEADS_PER_GROUP):
                hidx = g * HEADS_PER_GROUP + j
                x = act_ref[0, :, _head_cols(hidx)]
                dt_j, acs_j = dt[:, hidx:hidx + 1], acs[:, hidx:hidx + 1]
                acs_last, xg, _, mm, decay_s = _ssd_head(x, dt_j, acs_j, cb, tri_mask, last_row, acs_t[hidx:hidx + 1, :])
                y_diag = jnp.dot(mm.astype(BF16), xg.astype(BF16), preferred_element_type=F32)
                st = lax.dot_general((xg * decay_s).astype(BF16), b16, (((0,), (0,)), ((), ())), preferred_element_type=F32)
                hp = state[hidx]
                hp_ref[0, hidx, 0] = hp
                y_off = lax.dot_general(c16, hp.astype(BF16), (((1,), (1,)), ((), ())), preferred_element_type=F32)
                d_j = dsk_ref[:, hidx:hidx + 1]
                y_ref[0, :, _head_cols(hidx)] = y_diag + y_off * jnp.exp(acs_j) + d_j * x
                state[hidx] = hp * jnp.exp(acs_last) + st

    return pl.pallas_call(
        body, name=name, grid=(b, nc),
        in_specs=[act_spec, dt_in_spec, par_spec, par_spec, par_spec],
        out_specs=[y_spec, h_spec],
        out_shape=[jax.ShapeDtypeStruct((b, s, SSM_INNER), F32),
                   jax.ShapeDtypeStruct((b, SSM_HEADS, nc, SSM_P, D_STATE), F32)],
        scratch_shapes=[pltpu.VMEM((SSM_HEADS, SSM_P, D_STATE), F32)],
        compiler_params=_params(("arbitrary", "arbitrary")),
    )(act3, proj3, dtb, alog, dsk)


def ssd_bwd(act3, proj3, dtb, alog, dsk, hprev, dy3, *, name):
    b, s, _ = act3.shape
    nc = s // CHUNK
    act_spec, y_spec, dt_in_spec, dt_out_spec, par_spec, h_spec = _ssd_specs(nc, True)
    dpar_spec = pl.BlockSpec((8, LANE), lambda bi, c: (0, 0))

    def body(act_ref, dtr_ref, dtb_ref, alog_ref, dsk_ref, hp_ref, dy_ref, dact_ref, ddtr_ref, dpar_ref, dstate):
        bi, c = pl.program_id(0), pl.program_id(1)

        @pl.when(c == 0)
        def _():
            dstate[...] = jnp.zeros_like(dstate)

        @pl.when((bi == 0) & (c == 0))
        def _():
            dpar_ref[...] = jnp.zeros_like(dpar_ref)

        z, dt, aneg, acs = _ssd_common(dtr_ref, dtb_ref, alog_ref)
        acs_t = acs.T
        tri_mask = _tri(True)
        last_row = (lax.broadcasted_iota(jnp.int32, (CHUNK, 1), 0) == CHUNK - 1).astype(F32)
        lanes = lax.broadcasted_iota(jnp.int32, (1, LANE), 1)
        sublanes = lax.broadcasted_iota(jnp.int32, (LANE, 1), 0)
        ddt_mat = jnp.zeros((CHUNK, LANE), F32)
        dacs_mat = jnp.zeros((CHUNK, LANE), F32)
        dacs_rows = jnp.zeros((LANE, CHUNK), F32)
        ddsk_row = jnp.zeros((1, LANE), F32)
        for g in range(SSM_GROUPS):
            b16 = act_ref[0, :, _group_cols(g, 0)].astype(BF16)
            c16 = act_ref[0, :, _group_cols(g, 1)].astype(BF16)
            cb = lax.dot_general(c16, b16, (((1,), (1,)), ((), ())), preferred_element_type=F32)
            dcb = jnp.zeros((CHUNK, CHUNK), F32)
            db_acc = jnp.zeros((CHUNK, D_STATE), F32)
            dc_acc = jnp.zeros((CHUNK, D_STATE), F32)
            for j in range(HEADS_PER_GROUP):
                hidx = g * HEADS_PER_GROUP + j
                onehot = (lanes == hidx).astype(F32)
                x = act_ref[0, :, _head_cols(hidx)]
                dt_j, acs_j = dt[:, hidx:hidx + 1], acs[:, hidx:hidx + 1]
                acs_last, xg, lm, mm, decay_s = _ssd_head(x, dt_j, acs_j, cb, tri_mask, last_row, acs_t[hidx:hidx + 1, :])
                ea = jnp.exp(acs_j)
                cd = jnp.exp(acs_last)
                d_j = dsk_ref[:, hidx:hidx + 1]
                hp = hp_ref[0, hidx, 0]
                hp16 = hp.astype(BF16)
                g_y = dy_ref[0, :, _head_cols(hidx)]
                g_y16 = g_y.astype(BF16)
                g_hn = dstate[hidx]
                g_hn16 = g_hn.astype(BF16)
                xg16 = xg.astype(BF16)
                ddsk_row = ddsk_row + jnp.sum(jnp.sum(g_y * x, axis=1, keepdims=True), axis=0, keepdims=True) * onehot
                d_mm = lax.dot_general(g_y16, xg16, (((1,), (1,)), ((), ())), preferred_element_type=F32)
                d_xg = lax.dot_general(mm.astype(BF16), g_y16, (((0,), (0,)), ((), ())), preferred_element_type=F32)
                dcb = dcb + d_mm * lm
                d_dm = d_mm * mm
                d_acs = jnp.sum(d_dm, axis=1, keepdims=True)
                dacs_rows = dacs_rows + (sublanes == hidx).astype(F32) * jnp.sum(d_dm, axis=0, keepdims=True)
                t_off = lax.dot_general(c16, hp16, (((1,), (1,)), ((), ())), preferred_element_type=F32)
                d_t16 = (g_y * ea).astype(BF16)
                d_acs = d_acs + jnp.sum(g_y * t_off, axis=1, keepdims=True) * ea
                dc_acc = dc_acc + jnp.dot(d_t16, hp16, preferred_element_type=F32)
                d_hp = lax.dot_general(d_t16, c16, (((0,), (0,)), ((), ())), preferred_element_type=F32) + g_hn * cd
                d_last = jnp.sum(jnp.sum(g_hn * hp, axis=1, keepdims=True), axis=0, keepdims=True) * cd
                d_w = lax.dot_general(b16, g_hn16, (((1,), (1,)), ((), ())), preferred_element_type=F32)
                db_acc = db_acc + jnp.dot((xg * decay_s).astype(BF16), g_hn16, preferred_element_type=F32)
                d_xg = d_xg + d_w * decay_s
                d_ds = jnp.sum(d_w * xg, axis=1, keepdims=True) * decay_s
                d_last = d_last + jnp.sum(d_ds, axis=0, keepdims=True)
                d_acs = d_acs - d_ds + d_last * last_row
                dact_ref[0, :, _head_cols(hidx)] = d_j * g_y + d_xg * dt_j
                ddt_mat = ddt_mat + jnp.sum(d_xg * x, axis=1, keepdims=True) * onehot
                dacs_mat = dacs_mat + d_acs * onehot
                dstate[hidx] = d_hp
            dcb16 = dcb.astype(BF16)
            dact_ref[0, :, _group_cols(g, 1)] = dc_acc + jnp.dot(dcb16, b16, preferred_element_type=F32)
            dact_ref[0, :, _group_cols(g, 0)] = db_acc + lax.dot_general(dcb16, c16, (((0,), (0,)), ((), ())),
                                                                         preferred_element_type=F32)
        d_a = _dot01_left(_tri(False).astype(BF16), dacs_mat - dacs_rows.T)
        ddt_mat = ddt_mat + d_a * aneg
        d_raw = ddt_mat * jax.nn.sigmoid(z)
        ddtr_ref[0] = d_raw
        dpar_ref[0:1, :] += jnp.sum(d_raw, axis=0, keepdims=True)
        dpar_ref[1:2, :] += jnp.sum(d_a * dt, axis=0, keepdims=True) * aneg
        dpar_ref[2:3, :] += ddsk_row

    return pl.pallas_call(
        body, name=name, grid=(b, nc),
        in_specs=[act_spec, dt_in_spec, par_spec, par_spec, par_spec, h_spec, y_spec],
        out_specs=[act_spec, dt_out_spec, dpar_spec],
        out_shape=[jax.ShapeDtypeStruct(act3.shape, F32), jax.ShapeDtypeStruct((b, s, LANE), F32),
                   jax.ShapeDtypeStruct((8, LANE), F32)],
        scratch_shapes=[pltpu.VMEM((SSM_HEADS, SSM_P, D_STATE), F32)],
        compiler_params=_params(("arbitrary", "arbitrary")),
    )(act3, proj3, dtb, alog, dsk, hprev, dy3)


def _unused_ssd_specs(nc, reverse):
    cidx = (lambda c: nc - 1 - c) if reverse else (lambda c: c)
    x_spec = pl.BlockSpec((1, HEADS_PER_GROUP, CHUNK, SSM_P), lambda b, c, g: (b, g, cidx(c), 0))
    bc_spec = pl.BlockSpec((1, 1, CHUNK, D_STATE), lambda b, c, g: (b, g, cidx(c), 0))
    dt_spec = pl.BlockSpec((1, CHUNK, LANE), lambda b, c, g: (b, cidx(c), 0))
    par_spec = pl.BlockSpec((1, LANE), lambda b, c, g: (0, 0))
    h_spec = pl.BlockSpec((1, HEADS_PER_GROUP, 1, SSM_P, D_STATE), lambda b, c, g: (b, g, cidx(c), 0, 0))
    return x_spec, bc_spec, dt_spec, par_spec, h_spec


def _unused_ssd_fwd(xs, bm, cm, dtr, dtb, alog, dsk, *, name):
    b, _, s, _ = xs.shape
    nc = s // CHUNK
    x_spec, bc_spec, dt_spec, par_spec, h_spec = _ssd_specs(nc, False)

    def body(x_ref, b_ref, c_ref, dtr_ref, dtb_ref, alog_ref, dsk_ref, y_ref, hp_ref, state):
        c, g = pl.program_id(1), pl.program_id(2)

        @pl.when(c == 0)
        def _():
            state[pl.ds(g * HEADS_PER_GROUP, HEADS_PER_GROUP)] = jnp.zeros((HEADS_PER_GROUP, SSM_P, D_STATE), F32)

        _, dt, _, acs = _ssd_common(dtr_ref, dtb_ref, alog_ref)
        b16, c16 = b_ref[0, 0].astype(BF16), c_ref[0, 0].astype(BF16)
        cb = lax.dot_general(c16, b16, (((1,), (1,)), ((), ())), preferred_element_type=F32)
        tri_mask = _tri(True)
        last_row = (lax.broadcasted_iota(jnp.int32, (CHUNK, 1), 0) == CHUNK - 1).astype(F32)
        lanes = lax.broadcasted_iota(jnp.int32, (1, LANE), 1)
        for j in range(HEADS_PER_GROUP):
            hidx = g * HEADS_PER_GROUP + j
            onehot = (lanes == hidx).astype(F32)
            x = x_ref[0, j]
            dt_j, acs_j = _col(dt, onehot), _col(acs, onehot)
            acs_last, xg, _, mm, decay_s = _ssd_head(x, dt_j, acs_j, cb, tri_mask, last_row)
            xg16 = xg.astype(BF16)
            y_diag = jnp.dot(mm.astype(BF16), xg16, preferred_element_type=F32)
            st = lax.dot_general((xg * decay_s).astype(BF16), b16, (((0,), (0,)), ((), ())), preferred_element_type=F32)
            hp = state[hidx]
            hp_ref[0, j, 0] = hp
            y_off = lax.dot_general(c16, hp.astype(BF16), (((1,), (1,)), ((), ())), preferred_element_type=F32)
            d_j = jnp.sum(dsk_ref[...] * onehot, axis=1, keepdims=True)
            y_ref[0, j] = y_diag + y_off * jnp.exp(acs_j) + d_j * x
            state[hidx] = hp * jnp.exp(acs_last) + st

    return pl.pallas_call(
        body, name=name, grid=(b, nc, SSM_GROUPS),
        in_specs=[x_spec, bc_spec, bc_spec, dt_spec, par_spec, par_spec, par_spec],
        out_specs=[x_spec, h_spec],
        out_shape=[jax.ShapeDtypeStruct(xs.shape, F32),
                   jax.ShapeDtypeStruct((b, SSM_HEADS, nc, SSM_P, D_STATE), F32)],
        scratch_shapes=[pltpu.VMEM((SSM_HEADS, SSM_P, D_STATE), F32)],
        compiler_params=_params(("arbitrary", "arbitrary", "arbitrary")),
    )(xs, bm, cm, dtr, dtb, alog, dsk)


def _unused_ssd_bwd(xs, bm, cm, dtr, dtb, alog, dsk, hprev, dy, *, name):
    b, _, s, _ = xs.shape
    nc = s // CHUNK
    x_spec, bc_spec, dt_spec, par_spec, h_spec = _ssd_specs(nc, True)
    dpar_spec = pl.BlockSpec((8, LANE), lambda bi, c, g: (0, 0))

    def body(x_ref, b_ref, c_ref, dtr_ref, dtb_ref, alog_ref, dsk_ref, hp_ref, dy_ref,
             dx_ref, db_ref, dc_ref, ddtr_ref, dpar_ref, dstate):
        bi, c, g = pl.program_id(0), pl.program_id(1), pl.program_id(2)

        @pl.when(c == 0)
        def _():
            dstate[pl.ds(g * HEADS_PER_GROUP, HEADS_PER_GROUP)] = jnp.zeros((HEADS_PER_GROUP, SSM_P, D_STATE), F32)

        @pl.when((bi == 0) & (c == 0) & (g == 0))
        def _():
            dpar_ref[...] = jnp.zeros_like(dpar_ref)

        z, dt, aneg, acs = _ssd_common(dtr_ref, dtb_ref, alog_ref)
        bv, cv = b_ref[0, 0], c_ref[0, 0]
        b16, c16 = bv.astype(BF16), cv.astype(BF16)
        cb = lax.dot_general(c16, b16, (((1,), (1,)), ((), ())), preferred_element_type=F32)
        tri_mask = _tri(True)
        last_row = (lax.broadcasted_iota(jnp.int32, (CHUNK, 1), 0) == CHUNK - 1).astype(F32)
        lanes = lax.broadcasted_iota(jnp.int32, (1, LANE), 1)
        dcb = jnp.zeros((CHUNK, CHUNK), F32)
        db_acc = jnp.zeros((CHUNK, D_STATE), F32)
        dc_acc = jnp.zeros((CHUNK, D_STATE), F32)
        ddt_mat = jnp.zeros((CHUNK, LANE), F32)
        dacs_mat = jnp.zeros((CHUNK, LANE), F32)
        ddsk_row = jnp.zeros((1, LANE), F32)
        for j in range(HEADS_PER_GROUP):
            hidx = g * HEADS_PER_GROUP + j
            onehot = (lanes == hidx).astype(F32)
            x = x_ref[0, j]
            dt_j, acs_j = _col(dt, onehot), _col(acs, onehot)
            acs_last, xg, lm, mm, decay_s = _ssd_head(x, dt_j, acs_j, cb, tri_mask, last_row)
            ea = jnp.exp(acs_j)
            cd = jnp.exp(acs_last)
            d_j = jnp.sum(dsk_ref[...] * onehot, axis=1, keepdims=True)
            hp = hp_ref[0, j, 0]
            hp16 = hp.astype(BF16)
            g_y = dy_ref[0, j]
            g_y16 = g_y.astype(BF16)
            g_hn = dstate[hidx]
            g_hn16 = g_hn.astype(BF16)
            xg16 = xg.astype(BF16)
            ddsk_row = ddsk_row + jnp.sum(jnp.sum(g_y * x, axis=1, keepdims=True), axis=0, keepdims=True) * onehot
            d_mm = lax.dot_general(g_y16, xg16, (((1,), (1,)), ((), ())), preferred_element_type=F32)
            d_xg = lax.dot_general(mm.astype(BF16), g_y16, (((0,), (0,)), ((), ())), preferred_element_type=F32)
            dcb = dcb + d_mm * lm
            d_dm = d_mm * mm
            d_acs = jnp.sum(d_dm, axis=1, keepdims=True) - jnp.sum(d_dm.T, axis=1, keepdims=True)
            t_off = lax.dot_general(c16, hp16, (((1,), (1,)), ((), ())), preferred_element_type=F32)
            d_t16 = (g_y * ea).astype(BF16)
            d_acs = d_acs + jnp.sum(g_y * t_off, axis=1, keepdims=True) * ea
            dc_acc = dc_acc + jnp.dot(d_t16, hp16, preferred_element_type=F32)
            d_hp = lax.dot_general(d_t16, c16, (((0,), (0,)), ((), ())), preferred_element_type=F32) + g_hn * cd
            d_last = jnp.sum(jnp.sum(g_hn * hp, axis=1, keepdims=True), axis=0, keepdims=True) * cd
            d_w = lax.dot_general(b16, g_hn16, (((1,), (1,)), ((), ())), preferred_element_type=F32)
            db_acc = db_acc + jnp.dot((xg * decay_s).astype(BF16), g_hn16, preferred_element_type=F32)
            d_xg = d_xg + d_w * decay_s
            d_ds = jnp.sum(d_w * xg, axis=1, keepdims=True) * decay_s
            d_last = d_last + jnp.sum(d_ds, axis=0, keepdims=True)
            d_acs = d_acs - d_ds + d_last * last_row
            dx_ref[0, j] = d_j * g_y + d_xg * dt_j
            ddt_mat = ddt_mat + jnp.sum(d_xg * x, axis=1, keepdims=True) * onehot
            dacs_mat = dacs_mat + d_acs * onehot
            dstate[hidx] = d_hp
        dcb16 = dcb.astype(BF16)
        dc_ref[0, 0] = dc_acc + jnp.dot(dcb16, b16, preferred_element_type=F32)
        db_ref[0, 0] = db_acc + lax.dot_general(dcb16, c16, (((0,), (0,)), ((), ())), preferred_element_type=F32)
        d_a = _dot01_left(_tri(False).astype(BF16), dacs_mat)
        ddt_mat = ddt_mat + d_a * aneg
        d_aneg = jnp.sum(d_a * dt, axis=0, keepdims=True)
        d_raw = ddt_mat * jax.nn.sigmoid(z)

        @pl.when(g == 0)
        def _():
            ddtr_ref[0] = d_raw

        @pl.when(g != 0)
        def _():
            ddtr_ref[0] += d_raw

        dpar_ref[0:1, :] += jnp.sum(d_raw, axis=0, keepdims=True)
        dpar_ref[1:2, :] += d_aneg * aneg
        dpar_ref[2:3, :] += ddsk_row

    return pl.pallas_call(
        body, name=name, grid=(b, nc, SSM_GROUPS),
        in_specs=[x_spec, bc_spec, bc_spec, dt_spec, par_spec, par_spec, par_spec, h_spec, x_spec],
        out_specs=[x_spec, bc_spec, bc_spec, dt_spec, dpar_spec],
        out_shape=[jax.ShapeDtypeStruct(xs.shape, F32), jax.ShapeDtypeStruct(bm.shape, F32),
                   jax.ShapeDtypeStruct(cm.shape, F32), jax.ShapeDtypeStruct(dtr.shape, F32),
                   jax.ShapeDtypeStruct((8, LANE), F32)],
        scratch_shapes=[pltpu.VMEM((SSM_HEADS, SSM_P, D_STATE), F32)],
        compiler_params=_params(("arbitrary", "arbitrary", "arbitrary")),
    )(xs, bm, cm, dtr, dtb, alog, dsk, hprev, dy)


SSD_INTERLEAVE = 8


def _each(f, *lists):
    return [f(*a) for a in zip(*lists)]


def _nt(a, b):
    return lax.dot_general(a, b, (((1,), (1,)), ((), ())), preferred_element_type=F32)


def _tn(a, b):
    return lax.dot_general(a, b, (((0,), (0,)), ((), ())), preferred_element_type=F32)


def _nn(a, b):
    return jnp.dot(a, b, preferred_element_type=F32)


def _rowsum(a):
    return jnp.sum(a, axis=1, keepdims=True)


def _colsum(a):
    return jnp.sum(a, axis=0, keepdims=True)


def _bf(a):
    return a.astype(BF16)


def _head_batches(g):
    first = g * HEADS_PER_GROUP
    return [list(range(first + k, first + k + SSD_INTERLEAVE)) for k in range(0, HEADS_PER_GROUP, SSD_INTERLEAVE)]


def _decay_matrix(acs_j, acs_row, tri_mask):
    dm = jnp.broadcast_to(acs_j, (CHUNK, CHUNK)) - jnp.broadcast_to(acs_row, (CHUNK, CHUNK))
    return jnp.where(tri_mask, jnp.exp(jnp.where(tri_mask, dm, 0.0)), 0.0)


def ssd_fwd(act3, proj3, dtb, alog, dsk, *, name):
    b, s, _ = act3.shape
    nc = s // CHUNK
    act_spec, y_spec, dt_in_spec, _, par_spec, h_spec = _ssd_specs(nc, False)

    def body(act_ref, dtr_ref, dtb_ref, alog_ref, dsk_ref, y_ref, hp_ref, state):
        c = pl.program_id(1)

        @pl.when(c == 0)
        def _():
            state[...] = jnp.zeros_like(state)

        _, dt, _, acs = _ssd_common(dtr_ref, dtb_ref, alog_ref)
        acs_t = acs.T
        tri_mask = _tri(True)
        last_row = (lax.broadcasted_iota(jnp.int32, (CHUNK, 1), 0) == CHUNK - 1).astype(F32)
        for g in range(SSM_GROUPS):
            b16 = _bf(act_ref[0, :, _group_cols(g, 0)])
            c16 = _bf(act_ref[0, :, _group_cols(g, 1)])
            cb = _nt(c16, b16)
            for hs in _head_batches(g):
                x = [act_ref[0, :, _head_cols(h)] for h in hs]
                dt_j = [dt[:, h:h + 1] for h in hs]
                acs_j = [acs[:, h:h + 1] for h in hs]
                acs_last = [_colsum(a * last_row) for a in acs_j]
                xg = _each(lambda xv, d: xv * d, x, dt_j)
                mm = [cb * _decay_matrix(a, acs_t[h:h + 1, :], tri_mask) for a, h in zip(acs_j, hs)]
                decay_s = _each(lambda al, a: jnp.exp(al - a), acs_last, acs_j)
                y_diag = _each(lambda m_, v: _nn(_bf(m_), _bf(v)), mm, xg)
                st = _each(lambda v, d: _tn(_bf(v * d), b16), xg, decay_s)
                hp = [state[h] for h in hs]
                for h, v in zip(hs, hp):
                    hp_ref[0, h, 0] = v
                y_off = [_nt(c16, _bf(v)) for v in hp]
                for h, yd, yo, a, xv in zip(hs, y_diag, y_off, acs_j, x):
                    y_ref[0, :, _head_cols(h)] = yd + yo * jnp.exp(a) + dsk_ref[:, h:h + 1] * xv
                for h, v, al, sv in zip(hs, hp, acs_last, st):
                    state[h] = v * jnp.exp(al) + sv

    return pl.pallas_call(
        body, name=name, grid=(b, nc),
        in_specs=[act_spec, dt_in_spec, par_spec, par_spec, par_spec],
        out_specs=[y_spec, h_spec],
        out_shape=[jax.ShapeDtypeStruct((b, s, SSM_INNER), F32),
                   jax.ShapeDtypeStruct((b, SSM_HEADS, nc, SSM_P, D_STATE), F32)],
        scratch_shapes=[pltpu.VMEM((SSM_HEADS, SSM_P, D_STATE), F32)],
        compiler_params=_params(("arbitrary", "arbitrary")),
    )(act3, proj3, dtb, alog, dsk)


def ssd_bwd(act3, proj3, dtb, alog, dsk, hprev, dy3, *, name):
    b, s, _ = act3.shape
    nc = s // CHUNK
    act_spec, y_spec, dt_in_spec, dt_out_spec, par_spec, h_spec = _ssd_specs(nc, True)
    dpar_spec = pl.BlockSpec((8, LANE), lambda bi, c: (0, 0))

    def body(act_ref, dtr_ref, dtb_ref, alog_ref, dsk_ref, hp_ref, dy_ref, dact_ref, ddtr_ref, dpar_ref, dstate):
        bi, c = pl.program_id(0), pl.program_id(1)

        @pl.when(c == 0)
        def _():
            dstate[...] = jnp.zeros_like(dstate)

        @pl.when((bi == 0) & (c == 0))
        def _():
            dpar_ref[...] = jnp.zeros_like(dpar_ref)

        z, dt, aneg, acs = _ssd_common(dtr_ref, dtb_ref, alog_ref)
        acs_t = acs.T
        tri_mask = _tri(True)
        last_row = (lax.broadcasted_iota(jnp.int32, (CHUNK, 1), 0) == CHUNK - 1).astype(F32)
        lanes = lax.broadcasted_iota(jnp.int32, (1, LANE), 1)
        sublanes = lax.broadcasted_iota(jnp.int32, (LANE, 1), 0)
        ddt_mat = jnp.zeros((CHUNK, LANE), F32)
        dacs_mat = jnp.zeros((CHUNK, LANE), F32)
        dacs_rows = jnp.zeros((LANE, CHUNK), F32)
        ddsk_row = jnp.zeros((1, LANE), F32)
        for g in range(SSM_GROUPS):
            b16 = _bf(act_ref[0, :, _group_cols(g, 0)])
            c16 = _bf(act_ref[0, :, _group_cols(g, 1)])
            cb = _nt(c16, b16)
            dcb = jnp.zeros((CHUNK, CHUNK), F32)
            db_acc = jnp.zeros((CHUNK, D_STATE), F32)
            dc_acc = jnp.zeros((CHUNK, D_STATE), F32)
            for hs in _head_batches(g):
                x = [act_ref[0, :, _head_cols(h)] for h in hs]
                g_y = [dy_ref[0, :, _head_cols(h)] for h in hs]
                hp = [hp_ref[0, h, 0] for h in hs]
                g_hn = [dstate[h] for h in hs]
                dt_j = [dt[:, h:h + 1] for h in hs]
                acs_j = [acs[:, h:h + 1] for h in hs]
                acs_last = [_colsum(a * last_row) for a in acs_j]
                xg = _each(lambda xv, d: xv * d, x, dt_j)
                lm = [_decay_matrix(a, acs_t[h:h + 1, :], tri_mask) for a, h in zip(acs_j, hs)]
                mm = [cb * l for l in lm]
                decay_s = _each(lambda al, a: jnp.exp(al - a), acs_last, acs_j)
                ea = [jnp.exp(a) for a in acs_j]
                cd = [jnp.exp(al) for al in acs_last]
                g_y16, xg16, hp16, g_hn16 = [[_bf(v) for v in vs] for vs in (g_y, xg, hp, g_hn)]
                d_mm = _each(_nt, g_y16, xg16)
                d_xg = _each(lambda m_, gy: _tn(_bf(m_), gy), mm, g_y16)
                d_dm = _each(lambda a, m_: a * m_, d_mm, mm)
                d_acs = [_rowsum(v) for v in d_dm]
                t_off = [_nt(c16, v) for v in hp16]
                d_t16 = _each(lambda gy, e: _bf(gy * e), g_y, ea)
                d_acs = _each(lambda da, gy, t, e: da + _rowsum(gy * t) * e, d_acs, g_y, t_off, ea)
                d_hp = _each(lambda dtv, gh, cdv: _tn(dtv, c16) + gh * cdv, d_t16, g_hn, cd)
                d_w = [_nt(b16, v) for v in g_hn16]
                d_xg = _each(lambda dx, dw, ds: dx + dw * ds, d_xg, d_w, decay_s)
                d_ds = _each(lambda dw, v, ds: _rowsum(dw * v) * ds, d_w, xg, decay_s)
                d_last = _each(lambda gh, hv, cdv, dd: _colsum(_rowsum(gh * hv)) * cdv + _colsum(dd), g_hn, hp, cd, d_ds)
                d_acs = _each(lambda da, dd, dl: da - dd + dl * last_row, d_acs, d_ds, d_last)
                for h, gy, dx, d, xv in zip(hs, g_y, d_xg, dt_j, x):
                    dact_ref[0, :, _head_cols(h)] = dsk_ref[:, h:h + 1] * gy + dx * d
                for h, v in zip(hs, d_hp):
                    dstate[h] = v
                for k, h in enumerate(hs):
                    onehot = (lanes == h).astype(F32)
                    dcb = dcb + d_mm[k] * lm[k]
                    dc_acc = dc_acc + _nn(d_t16[k], hp16[k])
                    db_acc = db_acc + _nn(_bf(xg[k] * decay_s[k]), g_hn16[k])
                    ddsk_row = ddsk_row + _colsum(_rowsum(g_y[k] * x[k])) * onehot
                    ddt_mat = ddt_mat + _rowsum(d_xg[k] * x[k]) * onehot
                    dacs_mat = dacs_mat + d_acs[k] * onehot
                    dacs_rows = dacs_rows + (sublanes == h).astype(F32) * _colsum(d_dm[k])
            dcb16 = _bf(dcb)
            dact_ref[0, :, _group_cols(g, 1)] = dc_acc + _nn(dcb16, b16)
            dact_ref[0, :, _group_cols(g, 0)] = db_acc + _tn(dcb16, c16)
        d_a = _dot01_left(_tri(False).astype(BF16), dacs_mat - dacs_rows.T)
        ddt_mat = ddt_mat + d_a * aneg
        d_raw = ddt_mat * jax.nn.sigmoid(z)
        ddtr_ref[0] = d_raw
        dpar_ref[0:1, :] += _colsum(d_raw)
        dpar_ref[1:2, :] += _colsum(d_a * dt) * aneg
        dpar_ref[2:3, :] += ddsk_row

    return pl.pallas_call(
        body, name=name, grid=(b, nc),
        in_specs=[act_spec, dt_in_spec, par_spec, par_spec, par_spec, h_spec, y_spec],
        out_specs=[act_spec, dt_out_spec, dpar_spec],
        out_shape=[jax.ShapeDtypeStruct(act3.shape, F32), jax.ShapeDtypeStruct((b, s, LANE), F32),
                   jax.ShapeDtypeStruct((8, LANE), F32)],
        scratch_shapes=[pltpu.VMEM((SSM_HEADS, SSM_P, D_STATE), F32)],
        compiler_params=_params(("arbitrary", "arbitrary")),
    )(act3, proj3, dtb, alog, dsk, hprev, dy3)


def to_heads(x, b, s, h):
    return x.reshape(b, s, h, -1).transpose(0, 2, 1, 3)


def from_heads(x):
    b, h, s, c = x.shape
    return x.transpose(0, 2, 1, 3).reshape(b * s, h * c)


def dilate_q(q, d):
    b, _, s, c = q.shape
    x = q.reshape(b, N_KV_HEADS, GQA, s // d, d, c).transpose(0, 1, 4, 2, 3, 5)
    return x.reshape(b * N_KV_HEADS * d, GQA, s // d, c)


def undilate_q(x, b, d):
    _, _, l, c = x.shape
    y = x.reshape(b, N_KV_HEADS, d, GQA, l, c).transpose(0, 1, 3, 4, 2, 5)
    return y.reshape(b, N_Q_HEADS, l * d, c)


def dilate_kv(k, d):
    b, h, s, c = k.shape
    return k.reshape(b, h, s // d, d, c).transpose(0, 1, 3, 2, 4).reshape(b * h * d, s // d, c)


def undilate_kv(x, b, d):
    _, l, c = x.shape
    return x.reshape(b, N_KV_HEADS, d, l, c).transpose(0, 1, 3, 2, 4).reshape(b, N_KV_HEADS, l * d, c)


def rotary_tables(positions):
    inv_freq = ROPE_THETA ** (-jnp.arange(0, ROPE_DIM, 2, dtype=F32) / ROPE_DIM)
    ang = positions.astype(F32)[..., None] * inv_freq
    cos, sin = jnp.cos(ang), jnp.sin(ang)
    rest = HEAD_DIM - ROPE_DIM
    cosf = jnp.concatenate([cos, cos, jnp.ones(cos.shape[:2] + (rest,), F32)], axis=-1)
    sinf = jnp.concatenate([-sin, sin, jnp.zeros(sin.shape[:2] + (rest,), F32)], axis=-1)
    return cosf, sinf


def w_in_columns(w):
    pad = jnp.zeros((w.shape[0], IN_PAD - IN_PROJ), w.dtype)
    return jnp.concatenate([w[:, :Q_END], w[:, V_END:XBC_END], w[:, Q_END:V_END], w[:, XBC_END:], pad], axis=1)


def w_in_grad_columns(g):
    return jnp.concatenate([g[:, :Z_COL], g[:, K_COL:DT_COL], g[:, Z_COL:K_COL], g[:, DT_COL:DT_COL + SSM_HEADS]], axis=1)


def lane_pad(v):
    return jnp.pad(v.reshape(1, -1), ((0, 0), (0, LANE - v.shape[-1])))


def layer_fwd(h, wts, small, rope_tab, b, s, tag, side=None):
    w_in, w_out, w_gate, w_up, w_down = wts
    t = b * s
    sv = {"h": h}
    hn = rowwise_fwd(rms_fn, [h], [small["norm_mix"]], [BF16], name=f"rms_mix_{tag}")[0]
    proj = matmul(hn, w_in, name=f"in_proj_{tag}")
    sv["hn"], sv["proj"] = hn, proj
    proj3 = proj.reshape(b, s, IN_PAD)
    attn3, lse3, *side_out = attn_fwd(proj3, rope_tab, name=f"attn_{tag}", side=side)
    sv["attn3"], sv["lse3"] = attn3, lse3
    attn = attn3.reshape(t, ATTN_WIDTH)
    act3 = conv_silu_fwd(proj3, small["conv_w"], small["conv_b"], name=f"conv_{tag}")
    y3, hprev = ssd_fwd(act3, proj3, small["dt_bias"], small["a_log"], small["d_skip"], name=f"ssd_{tag}")
    y = y3.reshape(t, SSM_INNER)
    sv["act3"], sv["hprev"], sv["y"] = act3, hprev, y
    gn = rowwise_fwd(gated_norm_fn, [y, proj], [small["ssm_norm"]], [F32], name=f"gated_norm_{tag}", groups=SSM_GROUPS,
                     windows=[None, (Z_COL, SSM_INNER)])[0]
    cat = jnp.concatenate([attn, gn], axis=1).astype(BF16)
    sv["cat"] = cat
    h1 = matmul(cat, w_out, name=f"out_proj_{tag}", residual=h)
    sv["h1"] = h1
    hn2 = rowwise_fwd(rms_fn, [h1], [small["norm_ffn"]], [BF16], name=f"rms_ffn_{tag}")[0]
    gate = matmul(hn2, w_gate, out_dtype=BF16, name=f"ffn_gate_{tag}")
    up = matmul(hn2, w_up, out_dtype=BF16, name=f"ffn_up_{tag}")
    act2 = rowwise_fwd(swiglu_fn, [gate, up], [], [BF16], name=f"swiglu_{tag}")[0]
    sv["hn2"], sv["gate"], sv["up"], sv["act2"] = hn2, gate, up, act2
    h2 = matmul(act2, w_down, name=f"ffn_down_{tag}", residual=h1)
    return h2, sv, (side_out[0] if side_out else None)


def layer_bwd(dh2, sv, wts, small, rope_tab, b, s, tag, side=None):
    w_in, w_out, w_gate, w_up, w_down = wts
    t = b * s
    gr = {}
    dh2_16 = dh2.astype(BF16)
    d_act2 = matmul(dh2_16, w_down, tb=True, out_dtype=BF16, name=f"ffn_down_dx_{tag}")
    gr["w_down"] = matmul(sv["act2"], dh2_16, ta=True, out_dtype=BF16, name=f"ffn_down_dw_{tag}")
    d_gate, d_up = rowwise_bwd(swiglu_fn, [sv["gate"], sv["up"]], [], [d_act2], [BF16, BF16], name=f"swiglu_bwd_{tag}")
    gr["w_gate"] = matmul(sv["hn2"], d_gate, ta=True, out_dtype=BF16, name=f"ffn_gate_dw_{tag}")
    gr["w_up"] = matmul(sv["hn2"], d_up, ta=True, out_dtype=BF16, name=f"ffn_up_dw_{tag}")
    d_hn2 = matmul(d_gate, w_gate, tb=True, name=f"ffn_gate_dx_{tag}")
    d_hn2 = matmul(d_up, w_up, tb=True, residual=d_hn2, name=f"ffn_up_dx_{tag}")
    dh1, gr["norm_ffn"] = rowwise_bwd(rms_fn, [sv["h1"]], [small["norm_ffn"]], [d_hn2], [F32],
                                      name=f"rms_ffn_bwd_{tag}", add_to_first=dh2)
    dh1_16 = dh1.astype(BF16)
    d_cat = matmul(dh1_16, w_out, tb=True, name=f"out_proj_dx_{tag}")
    gr["w_out"] = matmul(sv["cat"], dh1_16, ta=True, out_dtype=BF16, name=f"out_proj_dw_{tag}")
    d_attn, d_gn = d_cat[:, :ATTN_WIDTH], d_cat[:, ATTN_WIDTH:]
    d_y, d_z, gr["ssm_norm"] = rowwise_bwd(gated_norm_fn, [sv["y"], sv["proj"]], [small["ssm_norm"]], [d_gn], [F32, F32],
                                           name=f"gated_norm_bwd_{tag}", groups=SSM_GROUPS,
                                           windows=[None, (Z_COL, SSM_INNER)])
    proj3 = sv["proj"].reshape(b, s, IN_PAD)
    d_act3, d_dtr, d_par = ssd_bwd(sv["act3"], proj3, small["dt_bias"], small["a_log"], small["d_skip"], sv["hprev"],
                                   d_y.reshape(b, s, SSM_INNER), name=f"ssd_bwd_{tag}")
    gr["dt_bias"], gr["a_log"], gr["d_skip"] = d_par[0, :SSM_HEADS], d_par[1, :SSM_HEADS], d_par[2, :SSM_HEADS]
    d_xbc, gr["conv_w"], gr["conv_b"] = conv_silu_bwd(proj3, small["conv_w"], small["conv_b"], d_act3,
                                                      name=f"conv_bwd_{tag}")
    d_q3, d_k4, d_v4, *side_out = attn_bwd(proj3, rope_tab, sv["attn3"], sv["lse3"], d_attn.reshape(b, s, ATTN_WIDTH),
                                           name=f"attn_bwd_{tag}", side=side)
    d_proj = jnp.concatenate([d_q3.reshape(t, ATTN_WIDTH), d_z, d_xbc.reshape(t, CONV_CH), from_heads(d_k4),
                              from_heads(d_v4), d_dtr.reshape(t, LANE)], axis=1).astype(BF16)
    d_hn = matmul(d_proj, w_in, tb=True, name=f"in_proj_dx_{tag}")
    gr["w_in"] = w_in_grad_columns(matmul(sv["hn"], d_proj, ta=True, out_dtype=BF16, name=f"in_proj_dw_{tag}"))
    dh, gr["norm_mix"] = rowwise_bwd(rms_fn, [sv["h"]], [small["norm_mix"]], [d_hn], [F32],
                                     name=f"rms_mix_bwd_{tag}", add_to_first=dh1)
    return dh, gr, (side_out[0] if side_out else None)


def local_step(x, positions, big, small_all, final_norm, loss_target, *, late_weights=None, early_grads=None):
    b, s, _ = x.shape
    t = b * s
    rope_tab = jnp.concatenate(rotary_tables(positions), axis=-1)
    h = x.reshape(t, D_MODEL)
    saved, big = [], list(big)
    for l in range(DEPTH):
        side = (late_weights[0], False) if late_weights is not None and l == 0 else None
        h, sv, got = layer_fwd(h, big[l], small_all[l], rope_tab, b, s, f"l{l}", side=side)
        if got is not None:
            big[DEPTH - 1] = late_weights[1](got)
        saved.append(sv)
    dh, d_final, loss = loss_and_grad(h, loss_target.reshape(t, D_MODEL), final_norm.reshape(1, D_MODEL))
    grads, received = [None] * DEPTH, None
    for l in reversed(range(DEPTH)):
        side = (early_grads(grads[DEPTH - 1]), True) if early_grads is not None and l == 0 else None
        dh, grads[l], got = layer_bwd(dh, saved[l], big[l], small_all[l], rope_tab, b, s, f"l{l}", side=side)
        received = got if got is not None else received
    return loss, dh.reshape(b, s, D_MODEL), grads, d_final, received


def _slab_rows(r):
    return r if r <= 512 else _pick(r, (512, 256))


def cast_bf16(x, *, name):
    def fn(v):
        return (v,)
    return rowwise_fwd(fn, [x], [], [BF16], name=name, tr=_slab_rows(x.shape[0]))[0]


def sum_slots(x, *, name):
    n, r, c = x.shape
    tr = _slab_rows(r)

    def body(x_ref, o_ref):
        acc = x_ref[0].astype(F32)
        for i in range(1, n):
            acc = acc + x_ref[i].astype(F32)
        o_ref[...] = acc

    return pl.pallas_call(
        body, name=name, grid=(r // tr,), in_specs=[pl.BlockSpec((n, tr, c), lambda i: (0, i, 0))],
        out_specs=pl.BlockSpec((tr, c), lambda i: (i, 0)), out_shape=jax.ShapeDtypeStruct((r, c), F32),
        compiler_params=_params(("parallel",)),
    )(x)


def adamw(g_parts, w, m, v, *, name):
    r, c = w.shape
    tr = _slab_rows(r)
    n_g = len(g_parts)
    bc1 = 1.0 / (1.0 - ADAM_B1 ** ADAM_STEP)
    bc2 = 1.0 / (1.0 - ADAM_B2 ** ADAM_STEP)

    def body(*refs):
        g = refs[0][...]
        for r_ in refs[1:n_g]:
            g = g + r_[...]
        w_ref, m_ref, v_ref, g_out, d_out, m_out, v_out = refs[n_g:]
        m_new = ADAM_B1 * m_ref[...] + (1.0 - ADAM_B1) * g
        v_new = ADAM_B2 * v_ref[...] + (1.0 - ADAM_B2) * (g * g)
        g_out[...] = g
        m_out[...] = m_new
        v_out[...] = v_new
        d_out[...] = -ADAM_LR * ((m_new * bc1) / (jnp.sqrt(v_new * bc2) + ADAM_EPS) + ADAM_WD * w_ref[...])

    spec = pl.BlockSpec((tr, c), lambda i: (i, 0))
    return pl.pallas_call(
        body, name=name, grid=(r // tr,), in_specs=[spec] * (n_g + 3), out_specs=[spec] * 4,
        out_shape=[jax.ShapeDtypeStruct((r, c), F32)] * 4, compiler_params=_params(("parallel",)),
    )(*g_parts, w, m, v)


def _other_chips(x, y):
    return [(1 - x, y), (x, 1 - y), (1 - x, 1 - y)]


def allgather_chips(shards):
    n_arr = len(shards)

    def body(*refs):
        in_refs, out_refs = refs[:n_arr], refs[n_arr:2 * n_arr]
        send_sems, recv_sems, local_sems = refs[2 * n_arr:]
        x, y, c = lax.axis_index("x"), lax.axis_index("y"), lax.axis_index("c")
        chip = 2 * x + y
        started = []
        for a, (in_ref, out_ref) in enumerate(zip(in_refs, out_refs)):
            mine = pltpu.make_async_copy(in_ref, out_ref.at[chip], local_sems.at[a])
            mine.start()
            started.append(mine.wait)
            for k, (px, py) in enumerate(_other_chips(x, y)):
                cp = pltpu.make_async_remote_copy(src_ref=in_ref, dst_ref=out_ref.at[chip], send_sem=send_sems.at[3 * a + k],
                                                  recv_sem=recv_sems.at[3 * a + k], device_id=(px, py, c), device_id_type=MESH)
                cp.start()
                started.append(cp.wait_send)
        for a, (in_ref, out_ref) in enumerate(zip(in_refs, out_refs)):
            for k, (px, py) in enumerate(_other_chips(x, y)):
                pltpu.make_async_remote_copy(src_ref=in_ref, dst_ref=out_ref.at[2 * px + py], send_sem=send_sems.at[3 * a + k],
                                             recv_sem=recv_sems.at[3 * a + k], device_id=(px, py, c),
                                             device_id_type=MESH).wait_recv()
        for wait in started:
            wait()

    hbm = pl.BlockSpec(memory_space=pltpu.HBM)
    return pl.pallas_call(
        body, name="allgather_weights", in_specs=[hbm] * n_arr, out_specs=[hbm] * n_arr,
        out_shape=[jax.ShapeDtypeStruct((N_CHIPS,) + s.shape, s.dtype) for s in shards],
        scratch_shapes=[pltpu.SemaphoreType.DMA((3 * n_arr,)), pltpu.SemaphoreType.DMA((3 * n_arr,)),
                        pltpu.SemaphoreType.DMA((n_arr,))],
    )(*shards)


def exchange_grads(big, small):
    def body(big_ref, small_ref, big_out, small_out, send_sems, recv_sems, local_sems):
        x, y, c = lax.axis_index("x"), lax.axis_index("y"), lax.axis_index("c")
        chip = 2 * x + y
        dev = 4 * x + 2 * y + c
        own_big = pltpu.make_async_copy(big_ref.at[chip], big_out.at[chip], local_sems.at[0])
        own_small = pltpu.make_async_copy(small_ref, small_out.at[dev], local_sems.at[1])
        own_big.start()
        own_small.start()
        sends = []
        for k, (px, py) in enumerate(_other_chips(x, y)):
            cp = pltpu.make_async_remote_copy(src_ref=big_ref.at[2 * px + py], dst_ref=big_out.at[chip],
                                              send_sem=send_sems.at[k], recv_sem=recv_sems.at[k],
                                              device_id=(px, py, c), device_id_type=MESH)
            cp.start()
            sends.append(cp)
        peers = []
        for r in range(1, N_DEV):
            fx, fy, fc = (r >> 2) & 1, (r >> 1) & 1, r & 1
            px, py, pc = (x + fx) % 2, (y + fy) % 2, (c + fc) % 2
            peers.append((px, py, pc))
            cp = pltpu.make_async_remote_copy(src_ref=small_ref, dst_ref=small_out.at[dev], send_sem=send_sems.at[2 + r],
                                              recv_sem=recv_sems.at[2 + r], device_id=(px, py, pc), device_id_type=MESH)
            cp.start()
            sends.append(cp)
        for k, (px, py) in enumerate(_other_chips(x, y)):
            pltpu.make_async_remote_copy(src_ref=big_ref.at[chip], dst_ref=big_out.at[2 * px + py],
                                         send_sem=send_sems.at[k], recv_sem=recv_sems.at[k],
                                         device_id=(px, py, c), device_id_type=MESH).wait_recv()
        for r, (px, py, pc) in zip(range(1, N_DEV), peers):
            pltpu.make_async_remote_copy(src_ref=small_ref, dst_ref=small_out.at[4 * px + 2 * py + pc],
                                         send_sem=send_sems.at[2 + r], recv_sem=recv_sems.at[2 + r],
                                         device_id=(px, py, pc), device_id_type=MESH).wait_recv()
        for cp in sends:
            cp.wait_send()
        own_big.wait()
        own_small.wait()

    hbm = pl.BlockSpec(memory_space=pltpu.HBM)
    n_sem = 3 + N_DEV - 1
    return pl.pallas_call(
        body, name="exchange_grads", in_specs=[hbm, hbm], out_specs=[hbm, hbm],
        out_shape=[jax.ShapeDtypeStruct(big.shape, big.dtype), jax.ShapeDtypeStruct((N_DEV,) + small.shape, small.dtype)],
        scratch_shapes=[pltpu.SemaphoreType.DMA((n_sem,)), pltpu.SemaphoreType.DMA((n_sem,)), pltpu.SemaphoreType.DMA((2,))],
    )(big, small)


SWAP_CHUNKS = 28


def swap_cores(mine):
    rows = mine.shape[0] // SWAP_CHUNKS
    assert rows * SWAP_CHUNKS == mine.shape[0] and rows % 8 == 0

    def body(in_ref, out_ref, send_sems, recv_sems):
        x, y, c = lax.axis_index("x"), lax.axis_index("y"), lax.axis_index("c")

        def chunk(k):
            part = pl.ds(k * rows, rows)
            return pltpu.make_async_remote_copy(src_ref=in_ref.at[part], dst_ref=out_ref.at[part],
                                                send_sem=send_sems.at[k], recv_sem=recv_sems.at[k],
                                                device_id=(x, y, 1 - c), device_id_type=MESH)

        for k in range(SWAP_CHUNKS):
            chunk(k).start()
        for k in range(SWAP_CHUNKS):
            chunk(k).wait_recv()
        for k in range(SWAP_CHUNKS):
            chunk(k).wait_send()

    hbm = pl.BlockSpec(memory_space=pltpu.HBM)
    return pl.pallas_call(
        body, name="swap_cores", in_specs=[hbm], out_specs=hbm,
        out_shape=jax.ShapeDtypeStruct(mine.shape, mine.dtype),
        scratch_shapes=[pltpu.SemaphoreType.DMA((SWAP_CHUNKS,)), pltpu.SemaphoreType.DMA((SWAP_CHUNKS,))],
    )(mine)


BIG_NAMES = ("w_in", "w_out", "w_gate", "w_up", "w_down")
BIG_SHARD_AXIS = {"w_in": 1, "w_out": 0, "w_gate": 1, "w_up": 1, "w_down": 0}
PACK_COLS = 1024
SMALL_NAMES = ("norm_mix", "conv_w", "conv_b", "dt_bias", "a_log", "d_skip", "ssm_norm", "norm_ffn")


PACK_ROW_TILE = 256


def pack_big(shards):
    flat = jnp.concatenate([shards[n].reshape(-1) for n in BIG_NAMES])
    unit = PACK_ROW_TILE * PACK_COLS
    total = -(-flat.size // unit) * unit
    return jnp.pad(flat, (0, total - flat.size)).reshape(-1, PACK_COLS)


def unpack_big(packed, like):
    out, off = {}, 0
    flat = packed.reshape(-1)
    for n in BIG_NAMES:
        size = like[n].size
        out[n] = flat[off:off + size].reshape(like[n].shape)
        off += size
    return out


def pack_small(parts):
    flat = jnp.concatenate([p.reshape(-1).astype(F32) for p in parts])
    rows = -(-flat.size // LANE)
    rows = -(-rows // 8) * 8
    return jnp.pad(flat, (0, rows * LANE - flat.size)).reshape(rows, LANE)


def unpack_small(packed, like):
    out, off = [], 0
    flat = packed.reshape(-1)
    for a in like:
        out.append(flat[off:off + a.size].reshape(a.shape))
        off += a.size
    return out


def kernel(x, positions, norm_mix, w_in, conv_w, conv_b, dt_bias, a_log, d_skip, ssm_norm, w_out, norm_ffn, w_gate, w_up, w_down, final_norm, loss_target, m_norm_mix, m_w_in, m_conv_w, m_conv_b, m_dt_bias, m_a_log, m_d_skip, m_ssm_norm, m_w_out, m_norm_ffn, m_w_gate, m_w_up, m_w_down, m_final_norm, v_norm_mix, v_w_in, v_conv_w, v_conv_b, v_dt_bias, v_a_log, v_d_skip, v_ssm_norm, v_w_out, v_norm_ffn, v_w_gate, v_w_up, v_w_down, v_final_norm):
    chip = 2 * lax.axis_index("x") + lax.axis_index("y")
    w_sh = {"w_in": w_in, "w_out": w_out, "w_gate": w_gate, "w_up": w_up, "w_down": w_down}
    m_sh = {"w_in": m_w_in, "w_out": m_w_out, "w_gate": m_w_gate, "w_up": m_w_up, "w_down": m_w_down}
    v_sh = {"w_in": v_w_in, "w_out": v_w_out, "w_gate": v_w_gate, "w_up": v_w_up, "w_down": v_w_down}

    assert DEPTH == 2
    layer_of = lambda d, l: {n: d[n][l] for n in BIG_NAMES}
    pack_layers = lambda d: jnp.concatenate([pack_big(layer_of(d, l)) for l in range(DEPTH)])
    layer_rows = pack_big(layer_of(w_sh, 0)).shape[0]

    def unpack_layers(packed):
        per_layer = [unpack_big(packed[l * layer_rows:(l + 1) * layer_rows], layer_of(w_sh, l)) for l in range(DEPTH)]
        return {n: jnp.stack([p[n] for p in per_layer]) for n in BIG_NAMES}

    def full_weights(gathered, l):
        pieces = [unpack_big(gathered[j], layer_of(w_sh, l)) for j in range(N_CHIPS)]
        full = {n: jnp.concatenate([p[n] for p in pieces], axis=BIG_SHARD_AXIS[n]) for n in BIG_NAMES}
        return (w_in_columns(full["w_in"]), full["w_out"], full["w_gate"], full["w_up"], full["w_down"])

    w_packed = pack_layers(w_sh)
    w_packed16 = cast_bf16(w_packed, name="cast_weights")
    conv_cols = CONV_CH // N_CHIPS
    gathered0, conv_g = allgather_chips([w_packed16[:layer_rows], conv_w.reshape(-1, LANE)])
    big = [full_weights(gathered0, 0), None]
    late_weights = (w_packed16[layer_rows:], lambda gathered: full_weights(gathered, DEPTH - 1))
    conv_w_full = jnp.concatenate([conv_g[j].reshape(DEPTH, CONV_WIDTH, conv_cols) for j in range(N_CHIPS)], axis=2)

    small_all = []
    for l in range(DEPTH):
        small_all.append({
            "norm_mix": norm_mix[l].reshape(1, -1), "conv_w": conv_w_full[l], "conv_b": conv_b[l].reshape(1, -1),
            "dt_bias": lane_pad(dt_bias[l]), "a_log": lane_pad(a_log[l]), "d_skip": lane_pad(d_skip[l]),
            "ssm_norm": ssm_norm[l].reshape(1, -1), "norm_ffn": norm_ffn[l].reshape(1, -1)})

    def shard_of(name, g, j):
        n = g.shape[BIG_SHARD_AXIS[name]] // N_CHIPS
        return lax.slice_in_dim(g, j * n, (j + 1) * n, axis=BIG_SHARD_AXIS[name])

    def per_chip(layer_grads):
        return jnp.stack([pack_big({n: shard_of(n, layer_grads[n], j) for n in BIG_NAMES}) for j in range(N_CHIPS)])

    loss_part, grad_x, grads, d_final, recv_last = local_step(x, positions, big, small_all, final_norm, loss_target,
                                                              late_weights=late_weights, early_grads=per_chip)

    small_parts = [jnp.stack([grads[l][n].reshape(-1) for l in range(DEPTH)]) for n in SMALL_NAMES]
    small_parts += [d_final.reshape(-1), loss_part.reshape(-1)]
    recv_first, recv_small = exchange_grads(per_chip(grads[0]), pack_small(small_parts))
    plane_sum = jnp.concatenate([sum_slots(recv_first, name="sum_chip_partials_l0"),
                                 sum_slots(recv_last, name="sum_chip_partials_l1")])
    other_plane = swap_cores(plane_sum)

    g_big, d_big, m_big, v_big = adamw([plane_sum, other_plane], w_packed, pack_layers(m_sh), pack_layers(v_sh), name="adamw_big")
    g_big, d_big, m_big, v_big = (unpack_layers(a) for a in (g_big, d_big, m_big, v_big))

    small_sum = sum_slots(recv_small, name="sum_small")
    like = [norm_mix, conv_w_full, conv_b, dt_bias, a_log, d_skip, ssm_norm, norm_ffn, final_norm, loss_part.reshape(-1)]
    g_small = unpack_small(small_sum, like)
    loss = g_small[-1][0]
    g_small = dict(zip(SMALL_NAMES + ("final_norm",), g_small[:-1]))
    g_small["conv_w"] = lax.dynamic_slice_in_dim(g_small["conv_w"], chip * conv_cols, conv_cols, axis=2)
    w_small = {"norm_mix": norm_mix, "conv_w": conv_w, "conv_b": conv_b, "dt_bias": dt_bias, "a_log": a_log, "d_skip": d_skip,
               "ssm_norm": ssm_norm, "norm_ffn": norm_ffn, "final_norm": final_norm}
    m_small = {"norm_mix": m_norm_mix, "conv_w": m_conv_w, "conv_b": m_conv_b, "dt_bias": m_dt_bias, "a_log": m_a_log,
               "d_skip": m_d_skip, "ssm_norm": m_ssm_norm, "norm_ffn": m_norm_ffn, "final_norm": m_final_norm}
    v_small = {"norm_mix": v_norm_mix, "conv_w": v_conv_w, "conv_b": v_conv_b, "dt_bias": v_dt_bias, "a_log": v_a_log,
               "d_skip": v_d_skip, "ssm_norm": v_ssm_norm, "norm_ffn": v_norm_ffn, "final_norm": v_final_norm}
    names = SMALL_NAMES + ("final_norm",)
    order = [w_small[n] for n in names]
    res = adamw([pack_small([g_small[n] for n in names])], pack_small(order), pack_small([m_small[n] for n in names]),
                pack_small([v_small[n] for n in names]), name="adamw_small")
    g_s, d_s, m_s, v_s = (dict(zip(names, unpack_small(a, order))) for a in res)

    all_names = ("norm_mix", "w_in", "conv_w", "conv_b", "dt_bias", "a_log", "d_skip", "ssm_norm", "w_out", "norm_ffn",
                 "w_gate", "w_up", "w_down", "final_norm")
    outs = [loss, grad_x]
    for src_big, src_small in ((g_big, g_s), (d_big, d_s), (m_big, m_s), (v_big, v_s)):
        outs += [src_big[n] if n in BIG_NAMES else src_small[n] for n in all_names]
    return tuple(outs)
```

```python
import functools

import jax
import jax.numpy as jnp
from jax import lax
from jax.experimental import pallas as pl
from jax.experimental.pallas import tpu as pltpu

F32 = jnp.float32
BF16 = jnp.bfloat16
MESH = pl.DeviceIdType.MESH

D_MODEL = 1024
DEPTH = 2
HEAD_DIM = 64
N_Q_HEADS = 8
N_KV_HEADS = 2
GQA = N_Q_HEADS // N_KV_HEADS
ATTN_WIDTH = N_Q_HEADS * HEAD_DIM
ROPE_DIM = HEAD_DIM // 4
ROPE_HALF = ROPE_DIM // 2
ROPE_THETA = 500000.0
DILATIONS = (1, 4, 16)
ATTN_BLOCK = 128
SSM_P = 64
SSM_HEADS = 16
SSM_INNER = SSM_HEADS * SSM_P
SSM_GROUPS = 2
HEADS_PER_GROUP = SSM_HEADS // SSM_GROUPS
D_STATE = 128
CONV_WIDTH = 4
CHUNK = 128
CONV_CH = SSM_INNER + 2 * SSM_GROUPS * D_STATE
MIX_WIDTH = ATTN_WIDTH + SSM_INNER
Q_END = ATTN_WIDTH
K_END = Q_END + N_KV_HEADS * HEAD_DIM
V_END = K_END + N_KV_HEADS * HEAD_DIM
Z_END = V_END + SSM_INNER
XBC_END = Z_END + CONV_CH
IN_PROJ = XBC_END + SSM_HEADS
LANE = 128
IN_PAD = XBC_END + LANE
Q_COL, Z_COL, XBC_COL, K_COL, V_COL, DT_COL = 0, 512, 1536, 3072, 3200, 3328
FFN_HIDDEN = 2816
EPS = 1e-5
ADAM_LR, ADAM_B1, ADAM_B2, ADAM_EPS, ADAM_WD, ADAM_STEP = 0.001, 0.9, 0.999, 1e-8, 0.01, 10
N_CHIPS = 4
N_DEV = 8
VMEM_LIMIT = 48 * 1024 * 1024
NEG_BIG = -1e30


def _params(sem=None):
    return pltpu.CompilerParams(dimension_semantics=sem, vmem_limit_bytes=VMEM_LIMIT)


def _pick(n, prefs):
    for p in prefs:
        if n % p == 0:
            return p
    return n


def matmul(a, b, *, name, ta=False, tb=False, out_dtype=F32, residual=None):
    if ta:
        assert not tb and residual is None
        return _matmul_over_rows(a, b, name=name, out_dtype=out_dtype)
    return _matmul_full_k(a, b, name=name, tb=tb, out_dtype=out_dtype, residual=residual)


def _matmul_full_k(a, b, *, name, tb, out_dtype, residual):
    a_parts = list(a) if isinstance(a, (list, tuple)) else [a]
    n_a = len(a_parts)
    m = a_parts[0].shape[0]
    kdim = sum(p.shape[1] for p in a_parts)
    wide = kdim > 1536 or any(p.dtype == F32 for p in a_parts)
    n = b.shape[0] if tb else b.shape[1]
    tm = _pick(m, (512, 256)) if wide else _pick(m, (1024, 512, 256))
    tn = _pick(n, (1152, 1408, 1536, 1024, 768, 512, 384, 256, 128))
    b_spec = pl.BlockSpec((tn, kdim), lambda i, j: (j, 0)) if tb else pl.BlockSpec((kdim, tn), lambda i, j: (0, j))
    o_spec = pl.BlockSpec((tm, tn), lambda i, j: (i, j))
    dims = (((1,), (1 if tb else 0,)), ((), ()))
    has_res = residual is not None

    def body(*refs):
        b_ref, o_ref = refs[n_a], refs[-1]
        pieces = [r[...].astype(BF16) for r in refs[:n_a]]
        av = pieces[0] if n_a == 1 else jnp.concatenate(pieces, axis=1)
        r = lax.dot_general(av, b_ref[...].astype(BF16), dims, preferred_element_type=F32)
        if has_res:
            r = r + refs[n_a + 1][...]
        o_ref[...] = r.astype(out_dtype)

    in_specs = ([pl.BlockSpec((tm, p.shape[1]), lambda i, j: (i, 0)) for p in a_parts] + [b_spec]
                + ([o_spec] if has_res else []))
    args = tuple(a_parts) + (b,) + ((residual,) if has_res else ())
    return pl.pallas_call(
        body, name=name, grid=(m // tm, n // tn), in_specs=in_specs, out_specs=o_spec,
        out_shape=jax.ShapeDtypeStruct((m, n), out_dtype),
        compiler_params=_params(("parallel", "parallel")),
    )(*args)


def _matmul_over_rows(a, b, *, name, out_dtype):
    t, m = a.shape
    n = b.shape[1]
    tm = _pick(m, (1024, 1408, 768, 512, 256, 128))
    tn = _pick(n, (1152, 1408, 1024, 768, 512, 256, 128))
    tk = _pick(t, (1024, 512, 256, 128))
    nk = t // tk

    def body(a_ref, b_ref, o_ref, acc):
        k = pl.program_id(2)
        part = lax.dot_general(a_ref[...].astype(BF16), b_ref[...].astype(BF16), (((0,), (0,)), ((), ())),
                               preferred_element_type=F32)

        @pl.when(k == 0)
        def _():
            acc[...] = part

        @pl.when(k > 0)
        def _():
            acc[...] += part

        @pl.when(k == nk - 1)
        def _():
            o_ref[...] = acc[...].astype(out_dtype)

    return pl.pallas_call(
        body, name=name, grid=(m // tm, n // tn, nk),
        in_specs=[pl.BlockSpec((tk, tm), lambda i, j, k: (k, i)), pl.BlockSpec((tk, tn), lambda i, j, k: (k, j))],
        out_specs=pl.BlockSpec((tm, tn), lambda i, j, k: (i, j)),
        out_shape=jax.ShapeDtypeStruct((m, n), out_dtype),
        scratch_shapes=[pltpu.VMEM((tm, tn), F32)],
        compiler_params=_params(("parallel", "parallel", "arbitrary")),
    )(a, b)


ROW_BLOCK_BYTES = 16 * 1024 * 1024


def _row_tile(t, tr, widths, n_copies):
    lanes = sum(-(-wd // LANE) * LANE for wd in widths) * n_copies
    tr = min(tr, t)
    while tr > 8 and tr * lanes * 4 > ROW_BLOCK_BYTES:
        tr //= 2
    return tr


def _row_widths(rows, groups, windows):
    windows = windows or [None] * len(rows)
    widths = [(w[1] if w else a.shape[1]) // groups for a, w in zip(rows, windows)]
    assert all(w is None or w[0] % wd == 0 for w, wd in zip(windows, widths))
    return widths, [(w[0] // wd if w else 0) for w, wd in zip(windows, widths)]


def _row_specs(tr, widths, offs):
    return [pl.BlockSpec((tr, wd), functools.partial(lambda g, i, off: (i, g + off), off=off)) for wd, off in zip(widths, offs)]


def rowwise_fwd(fn, rows, params, out_dtypes, *, name, tr=512, groups=1, windows=None):
    t = rows[0].shape[0]
    widths, offs = _row_widths(rows, groups, windows)
    tr = _row_tile(t, tr, widths, 2)
    row_specs = _row_specs(tr, widths, offs)
    par_spec = lambda p: pl.BlockSpec((1, p.shape[1] // groups), lambda g, i: (0, g))
    n_in = len(rows) + len(params)
    out_cols = [o.shape[1] for o in jax.eval_shape(
        fn, *[jax.ShapeDtypeStruct((tr, wd), F32) for wd in widths],
        *[jax.ShapeDtypeStruct((1, p.shape[1] // groups), F32) for p in params])]

    def body(*refs):
        vals = [r[...].astype(F32) for r in refs[:n_in]]
        outs = fn(*vals)
        for o_ref, o in zip(refs[n_in:], outs):
            o_ref[...] = o.astype(o_ref.dtype)

    return pl.pallas_call(
        body, name=name, grid=(groups, t // tr),
        in_specs=row_specs + [par_spec(p) for p in params],
        out_specs=[pl.BlockSpec((tr, c), lambda g, i: (i, g)) for c in out_cols],
        out_shape=[jax.ShapeDtypeStruct((t, c * groups), d) for c, d in zip(out_cols, out_dtypes)],
        compiler_params=_params(("arbitrary", "arbitrary")),
    )(*rows, *params)


def rowwise_bwd(fn, rows, params, cts, drow_dtypes, *, name, tr=512, groups=1, add_to_first=None, windows=None,
                ct_windows=None):
    t = rows[0].shape[0]
    widths, offs = _row_widths(rows, groups, windows)
    ct_widths, ct_offs = _row_widths(cts, groups, ct_windows)
    tr = _row_tile(t, tr, widths + ct_widths, 2)
    row_spec = lambda a: pl.BlockSpec((tr, a.shape[1] // groups), lambda g, i: (i, g))
    row_specs = _row_specs(tr, widths, offs)
    par_spec = lambda p: pl.BlockSpec((1, p.shape[1] // groups), lambda g, i: (0, g))
    n_rows, n_par, n_ct = len(rows), len(params), len(cts)
    has_add = add_to_first is not None
    n_in = n_rows + n_par + n_ct + (1 if has_add else 0)

    def body(*refs):
        i = pl.program_id(1)
        vals = [r[...].astype(F32) for r in refs[:n_rows + n_par]]
        ct_vals = tuple(r[...].astype(F32) for r in refs[n_rows + n_par:n_rows + n_par + n_ct])
        _, vjp = jax.vjp(fn, *vals)
        grads = vjp(ct_vals)
        out_refs = refs[n_in:]
        for idx in range(n_rows):
            g = grads[idx]
            if idx == 0 and has_add:
                g = g + refs[n_in - 1][...]
            out_refs[idx][...] = g.astype(out_refs[idx].dtype)
        for idx in range(n_par):
            p_ref = out_refs[n_rows + idx]

            @pl.when(i == 0)
            def _():
                p_ref[...] = jnp.zeros_like(p_ref)

            p_ref[...] += grads[n_rows + idx]

    ins = list(rows) + list(params) + list(cts) + ([add_to_first] if has_add else [])
    in_specs = (row_specs + [par_spec(p) for p in params] + _row_specs(tr, ct_widths, ct_offs)
                + ([row_spec(add_to_first)] if has_add else []))
    return pl.pallas_call(
        body, name=name, grid=(groups, t // tr), in_specs=in_specs,
        out_specs=[pl.BlockSpec((tr, wd), lambda g, i: (i, g)) for wd in widths] + [par_spec(p) for p in params],
        out_shape=[jax.ShapeDtypeStruct((t, wd * groups), d) for wd, d in zip(widths, drow_dtypes)]
        + [jax.ShapeDtypeStruct(p.shape, F32) for p in params],
        compiler_params=_params(("arbitrary", "arbitrary")),
    )(*ins)


def rms_fn(x, w):
    return (x * lax.rsqrt(jnp.mean(x * x, axis=-1, keepdims=True) + EPS) * w,)


def swiglu_fn(g, u):
    return (g * jax.nn.sigmoid(g) * u,)


def gated_norm_fn(y, z, w):
    v = y * (z * jax.nn.sigmoid(z))
    return (v * lax.rsqrt(jnp.mean(v * v, axis=-1, keepdims=True) + EPS) * w,)


def combine_fn(o1, o2, o3, l1, l2, l3):
    m = jnp.maximum(jnp.maximum(l1, l2), l3)
    e1, e2, e3 = jnp.exp(l1 - m), jnp.exp(l2 - m), jnp.exp(l3 - m)
    inv = 1.0 / (e1 + e2 + e3)
    return ((e1 * inv) * o1 + (e2 * inv) * o2 + (e3 * inv) * o3,)


def loss_and_grad(h, target, w, *, tr=512):
    t, d = h.shape

    def loss_fn(hv, wv, tv):
        err = rms_fn(hv, wv)[0] - tv
        per_row = jnp.mean(err * err, axis=-1, keepdims=True)
        return 0.5 * jnp.sum(per_row, axis=0, keepdims=True)

    def body(h_ref, t_ref, w_ref, dh_ref, dw_ref, loss_ref):
        i = pl.program_id(0)

        @pl.when(i == 0)
        def _():
            dw_ref[...] = jnp.zeros_like(dw_ref)
            loss_ref[...] = jnp.zeros_like(loss_ref)

        tv = t_ref[...]
        val, vjp = jax.vjp(lambda hv, wv: loss_fn(hv, wv, tv), h_ref[...], w_ref[...])
        dh, dw = vjp(jnp.ones((1, 1), F32))
        dh_ref[...] = dh
        dw_ref[...] += dw
        loss_ref[...] += jnp.broadcast_to(val, loss_ref.shape)

    row = pl.BlockSpec((tr, d), lambda i: (i, 0))
    par = pl.BlockSpec((1, d), lambda i: (0, 0))
    return pl.pallas_call(
        body, name="loss_and_grad", grid=(t // tr,), in_specs=[row, row, par],
        out_specs=[row, par, pl.BlockSpec((1, LANE), lambda i: (0, 0))],
        out_shape=[jax.ShapeDtypeStruct((t, d), F32), jax.ShapeDtypeStruct((1, d), F32),
                   jax.ShapeDtypeStruct((1, LANE), F32)],
        compiler_params=_params(("arbitrary",)),
    )(h, target, w)


def _split3(x):
    hi = x.astype(BF16)
    r1 = x - hi.astype(F32)
    mid = r1.astype(BF16)
    lo = (r1 - mid.astype(F32)).astype(BF16)
    return hi, mid, lo


def _dot01_left(m01, x):
    return sum(jnp.dot(m01, p, preferred_element_type=F32) for p in _split3(x))


def _dot01_right(x, m01):
    return sum(jnp.dot(p, m01, preferred_element_type=F32) for p in _split3(x))


def rotary(xs_list, cosf, sinf, scale, *, adjoint, name, ts=512):
    b, h, s, c = xs_list[0].shape
    n_x = len(xs_list)

    def body(*refs):
        x = refs[0][0, 0]
        for r in refs[1:n_x]:
            x = x + r[0, 0]
        cos_v, sin_v = refs[n_x][0], refs[n_x + 1][0]
        o_ref = refs[n_x + 2]
        ci = lax.broadcasted_iota(jnp.int32, (c, c), 0)
        cj = lax.broadcasted_iota(jnp.int32, (c, c), 1)
        swap = ((cj == ci + ROPE_HALF) & (ci < ROPE_HALF)) | ((cj == ci - ROPE_HALF) & (ci >= ROPE_HALF) & (ci < ROPE_DIM))
        swap = swap.astype(BF16)
        if adjoint:
            out = x * cos_v + _dot01_right(x * sin_v, swap)
        else:
            out = x * cos_v + _dot01_right(x, swap) * sin_v
        o_ref[0, 0] = out * scale

    x_spec = pl.BlockSpec((1, 1, ts, c), lambda bi, hi, si: (bi, hi, si, 0))
    t_spec = pl.BlockSpec((1, ts, c), lambda bi, hi, si: (bi, si, 0))
    return pl.pallas_call(
        body, name=name, grid=(b, h, s // ts), in_specs=[x_spec] * n_x + [t_spec, t_spec], out_specs=x_spec,
        out_shape=jax.ShapeDtypeStruct((b, h, s, c), F32),
        compiler_params=_params(("parallel", "parallel", "parallel")),
    )(*xs_list, cosf, sinf)


def add3(a, b, c, *, name, tr=1024):
    def fn(x, y, z):
        return (x + y + z,)
    return rowwise_fwd(fn, [a, b, c], [], [F32], name=name, tr=tr)[0]


def _attn_mask(n):
    rows = GQA * ATTN_BLOCK
    qi = lax.broadcasted_iota(jnp.int32, (rows, 2 * ATTN_BLOCK), 0) % ATTN_BLOCK
    ki = lax.broadcasted_iota(jnp.int32, (rows, 2 * ATTN_BLOCK), 1)
    delta = qi + ATTN_BLOCK - ki
    return (delta >= 0) & (delta <= ATTN_BLOCK) & ((n - 1) * ATTN_BLOCK + ki >= 0)


def _attn_specs(l):
    q_spec = pl.BlockSpec((1, GQA, ATTN_BLOCK, HEAD_DIM), lambda p, n: (p, 0, n, 0))
    l_spec = pl.BlockSpec((1, GQA, ATTN_BLOCK, 1), lambda p, n: (p, 0, n, 0))
    kprev = pl.BlockSpec((1, ATTN_BLOCK, HEAD_DIM), lambda p, n: (p, jnp.maximum(n - 1, 0), 0))
    kcur = pl.BlockSpec((1, ATTN_BLOCK, HEAD_DIM), lambda p, n: (p, n, 0))
    kfull = pl.BlockSpec((1, l, HEAD_DIM), lambda p, n: (p, 0, 0))
    return q_spec, l_spec, kprev, kcur, kfull


def attn_branch_fwd(q, k, v, *, name):
    p_cnt, _, l, _ = q.shape
    rows = GQA * ATTN_BLOCK
    q_spec, l_spec, kprev, kcur, _ = _attn_specs(l)

    def body(q_ref, kp_ref, kc_ref, vp_ref, vc_ref, o_ref, lse_ref):
        n = pl.program_id(1)
        qv = q_ref[0].reshape(rows, HEAD_DIM).astype(BF16)
        kk = jnp.concatenate([kp_ref[0], kc_ref[0]], axis=0).astype(BF16)
        vv = jnp.concatenate([vp_ref[0], vc_ref[0]], axis=0).astype(BF16)
        s = lax.dot_general(qv, kk, (((1,), (1,)), ((), ())), preferred_element_type=F32)
        s = jnp.where(_attn_mask(n), s, NEG_BIG)
        m = jnp.max(s, axis=-1, keepdims=True)
        pr = jnp.exp(s - m)
        den = jnp.sum(pr, axis=-1, keepdims=True)
        o = jnp.dot(pr.astype(BF16), vv, preferred_element_type=F32) / den
        o_ref[0] = o.reshape(GQA, ATTN_BLOCK, HEAD_DIM)
        lse_ref[0] = (m + jnp.log(den)).reshape(GQA, ATTN_BLOCK, 1)

    return pl.pallas_call(
        body, name=name, grid=(p_cnt, l // ATTN_BLOCK), in_specs=[q_spec, kprev, kcur, kprev, kcur],
        out_specs=[q_spec, l_spec],
        out_shape=[jax.ShapeDtypeStruct(q.shape, F32), jax.ShapeDtypeStruct(q.shape[:3] + (1,), F32)],
        compiler_params=_params(("parallel", "arbitrary")),
    )(q, k, k, v, v)


def attn_branch_bwd(q, k, v, o, lse, do, dlse, *, name):
    p_cnt, _, l, _ = q.shape
    rows = GQA * ATTN_BLOCK
    q_spec, l_spec, kprev, kcur, kfull = _attn_specs(l)

    def body(q_ref, kp_ref, kc_ref, vp_ref, vc_ref, o_ref, lse_ref, do_ref, dlse_ref, dq_ref, dk_ref, dv_ref):
        n = pl.program_id(1)

        @pl.when(n == 0)
        def _():
            dk_ref[...] = jnp.zeros_like(dk_ref)
            dv_ref[...] = jnp.zeros_like(dv_ref)

        qv = q_ref[0].reshape(rows, HEAD_DIM).astype(BF16)
        kk = jnp.concatenate([kp_ref[0], kc_ref[0]], axis=0).astype(BF16)
        vv = jnp.concatenate([vp_ref[0], vc_ref[0]], axis=0).astype(BF16)
        ov = o_ref[0].reshape(rows, HEAD_DIM)
        dov = do_ref[0].reshape(rows, HEAD_DIM)
        lsev = lse_ref[0].reshape(rows, 1)
        dlsev = dlse_ref[0].reshape(rows, 1)
        s = lax.dot_general(qv, kk, (((1,), (1,)), ((), ())), preferred_element_type=F32)
        pr = jnp.where(_attn_mask(n), jnp.exp(s - lsev), 0.0)
        do16 = dov.astype(BF16)
        dv = lax.dot_general(pr.astype(BF16), do16, (((0,), (0,)), ((), ())), preferred_element_type=F32)
        dp = lax.dot_general(do16, vv, (((1,), (1,)), ((), ())), preferred_element_type=F32)
        delta = jnp.sum(dov * ov, axis=-1, keepdims=True)
        ds = (pr * (dp - delta + dlsev)).astype(BF16)
        dq = jnp.dot(ds, kk, preferred_element_type=F32)
        dk = lax.dot_general(ds, qv, (((0,), (0,)), ((), ())), preferred_element_type=F32)
        dq_ref[0] = dq.reshape(GQA, ATTN_BLOCK, HEAD_DIM)
        cur = pl.ds(pl.multiple_of(n * ATTN_BLOCK, ATTN_BLOCK), ATTN_BLOCK)
        dk_ref[0, cur, :] += dk[ATTN_BLOCK:]
        dv_ref[0, cur, :] += dv[ATTN_BLOCK:]

        @pl.when(n > 0)
        def _():
            prev = pl.ds(pl.multiple_of((n - 1) * ATTN_BLOCK, ATTN_BLOCK), ATTN_BLOCK)
            dk_ref[0, prev, :] += dk[:ATTN_BLOCK]
            dv_ref[0, prev, :] += dv[:ATTN_BLOCK]

    return pl.pallas_call(
        body, name=name, grid=(p_cnt, l // ATTN_BLOCK),
        in_specs=[q_spec, kprev, kcur, kprev, kcur, q_spec, l_spec, q_spec, l_spec],
        out_specs=[q_spec, kfull, kfull],
        out_shape=[jax.ShapeDtypeStruct(q.shape, F32), jax.ShapeDtypeStruct(k.shape, F32),
                   jax.ShapeDtypeStruct(v.shape, F32)],
        compiler_params=_params(("parallel", "arbitrary")),
    )(q, k, k, v, v, o, lse, do, dlse)


ATTN_PAD = ATTN_BLOCK * DILATIONS[-1]
Q_GROUP_W = GQA * HEAD_DIM
ATTN_VMEM_LIMIT = 56 * 1024 * 1024


def _rope(x, cos_v, sin_v, swap, scale, adjoint):
    if adjoint:
        return (x * cos_v + _dot01_right(x * sin_v, swap)) * scale
    return (x * cos_v + _dot01_right(x, swap) * sin_v) * scale


def _swap_matrix():
    c = HEAD_DIM
    ci = lax.broadcasted_iota(jnp.int32, (c, c), 0)
    cj = lax.broadcasted_iota(jnp.int32, (c, c), 1)
    swap = ((cj == ci + ROPE_HALF) & (ci < ROPE_HALF)) | ((cj == ci - ROPE_HALF) & (ci >= ROPE_HALF) & (ci < ROPE_DIM))
    return swap.astype(BF16)


def _attn_prologue(q_ref, kv_ref, tab_ref, q_s, k_s, v_s, hk, s_len):
    swap = _swap_matrix()
    cos_v, sin_v = tab_ref[0, :, :HEAD_DIM], tab_ref[0, :, HEAD_DIM:]
    for g in range(GQA):
        cols = slice(g * HEAD_DIM, (g + 1) * HEAD_DIM)
        q_s[:, cols] = _rope(q_ref[0, :, cols], cos_v, sin_v, swap, HEAD_DIM ** -0.5, False)
    zeros = jnp.zeros((ATTN_PAD, HEAD_DIM), F32)
    k_s[0:ATTN_PAD, :] = zeros
    v_s[0:ATTN_PAD, :] = zeros
    for h in range(N_KV_HEADS):
        @pl.when(hk == h)
        def _():
            k_s[ATTN_PAD:ATTN_PAD + s_len, :] = _rope(kv_ref[0, :, h * HEAD_DIM:(h + 1) * HEAD_DIM], cos_v, sin_v, swap, 1.0, False)
            v_s[ATTN_PAD:ATTN_PAD + s_len, :] = kv_ref[0, :, LANE + h * HEAD_DIM:LANE + (h + 1) * HEAD_DIM]


def _attn_blocks(s_len):
    out = []
    for i, d in enumerate(DILATIONS):
        nb = s_len // (ATTN_BLOCK * d)
        for r in range(d):
            for n in range(nb):
                start = r + d * ATTN_BLOCK * n
                out.append((i, d, start, ATTN_PAD + start - d * ATTN_BLOCK, n))
    return out


def _rows(start, size, d):
    return pl.ds(start, size, stride=d) if d > 1 else pl.ds(start, size)


def _stack_heads(blk):
    return jnp.concatenate([blk[:, g * HEAD_DIM:(g + 1) * HEAD_DIM] for g in range(GQA)], axis=0)


def _stack_stats(blk):
    return jnp.concatenate([jnp.max(blk[:, g * HEAD_DIM:(g + 1) * HEAD_DIM], axis=1, keepdims=True) for g in range(GQA)], axis=0)


def _attn_in_specs(s_len):
    assert K_COL % (2 * LANE) == 0 and V_COL == K_COL + LANE
    q_spec = pl.BlockSpec((1, s_len, Q_GROUP_W), lambda b, h: (b, 0, Q_COL // Q_GROUP_W + h))
    kv_spec = pl.BlockSpec((1, s_len, 2 * LANE), lambda b, h: (b, 0, K_COL // (2 * LANE)))
    t_spec = pl.BlockSpec((1, s_len, 2 * HEAD_DIM), lambda b, h: (b, 0, 0))
    o_spec = pl.BlockSpec((1, s_len, Q_GROUP_W), lambda b, h: (b, 0, h))
    return q_spec, kv_spec, t_spec, o_spec


def attn_fwd(proj3, rope_tab, *, name):
    b, s_len, _ = proj3.shape
    q_spec, kv_spec, t_spec, o_spec = _attn_in_specs(s_len)
    n_br = len(DILATIONS)

    def body(q_ref, kv_ref, tab_ref, o_ref, lse_ref, q_s, k_s, v_s, *branch_s):
        o_s, l_s = branch_s[:n_br], branch_s[n_br:]
        _attn_prologue(q_ref, kv_ref, tab_ref, q_s, k_s, v_s, pl.program_id(1), s_len)
        for i, d, q0, k0, n in _attn_blocks(s_len):
            qv = _stack_heads(q_s[_rows(q0, ATTN_BLOCK, d), :]).astype(BF16)
            kk = k_s[_rows(k0, 2 * ATTN_BLOCK, d), :].astype(BF16)
            vv = v_s[_rows(k0, 2 * ATTN_BLOCK, d), :].astype(BF16)
            sc = lax.dot_general(qv, kk, (((1,), (1,)), ((), ())), preferred_element_type=F32)
            sc = jnp.where(_attn_mask(n), sc, NEG_BIG)
            m = jnp.max(sc, axis=-1, keepdims=True)
            pr = jnp.exp(sc - m)
            den = jnp.sum(pr, axis=-1, keepdims=True)
            o = jnp.dot(pr.astype(BF16), vv, preferred_element_type=F32) / den
            lse = m + jnp.log(den)
            for g in range(GQA):
                part = slice(g * ATTN_BLOCK, (g + 1) * ATTN_BLOCK)
                o_s[i][_rows(q0, ATTN_BLOCK, d), g * HEAD_DIM:(g + 1) * HEAD_DIM] = o[part]
                l_s[i][_rows(q0, ATTN_BLOCK, d), g * HEAD_DIM:(g + 1) * HEAD_DIM] = jnp.broadcast_to(lse[part], (ATTN_BLOCK, HEAD_DIM))
        step = 256
        for t0 in range(0, s_len, step):
            rs = pl.ds(t0, step)
            for g in range(GQA):
                ls = [l_s[i][rs, g * HEAD_DIM:(g + 1) * HEAD_DIM] for i in range(n_br)]
                m = functools.reduce(jnp.maximum, ls)
                es = [jnp.exp(l - m) for l in ls]
                tot = functools.reduce(lambda a, c: a + c, es)
                inv = 1.0 / tot
                acc = None
                for i in range(n_br):
                    term = (es[i] * inv) * o_s[i][rs, g * HEAD_DIM:(g + 1) * HEAD_DIM]
                    acc = term if acc is None else acc + term
                o_ref[0, rs, g * HEAD_DIM:(g + 1) * HEAD_DIM] = acc
                lse_ref[0, rs, g * HEAD_DIM:(g + 1) * HEAD_DIM] = m + jnp.log(tot)

    return pl.pallas_call(
        body, name=name, grid=(b, N_KV_HEADS), in_specs=[q_spec, kv_spec, t_spec],
        out_specs=[o_spec, o_spec],
        out_shape=[jax.ShapeDtypeStruct((b, s_len, ATTN_WIDTH), F32)] * 2,
        scratch_shapes=[pltpu.VMEM((s_len, Q_GROUP_W), F32), pltpu.VMEM((ATTN_PAD + s_len, HEAD_DIM), F32),
                        pltpu.VMEM((ATTN_PAD + s_len, HEAD_DIM), F32)] + [pltpu.VMEM((s_len, Q_GROUP_W), F32)] * (2 * n_br),
        compiler_params=pltpu.CompilerParams(dimension_semantics=("arbitrary", "arbitrary"), vmem_limit_bytes=ATTN_VMEM_LIMIT),
    )(proj3, proj3, rope_tab)


def attn_bwd(proj3, rope_tab, attn3, lse3, d_attn3, *, name):
    b, s_len, _ = proj3.shape
    q_spec, kv_spec, t_spec, o_spec = _attn_in_specs(s_len)
    kv_out = pl.BlockSpec((1, 1, s_len, HEAD_DIM), lambda bi, h: (bi, h, 0, 0))

    def body(q_ref, kv_ref, tab_ref, o_ref, lse_ref, do_ref, dq_ref, dk_ref, dv_ref,
             q_s, k_s, v_s, dl_s, dq_s, dk_s, dv_s):
        _attn_prologue(q_ref, kv_ref, tab_ref, q_s, k_s, v_s, pl.program_id(1), s_len)
        dq_s[...] = jnp.zeros_like(dq_s)
        dk_s[...] = jnp.zeros_like(dk_s)
        dv_s[...] = jnp.zeros_like(dv_s)
        for g in range(GQA):
            cols = slice(g * HEAD_DIM, (g + 1) * HEAD_DIM)
            delta = jnp.sum(do_ref[0, :, cols] * o_ref[0, :, cols], axis=1, keepdims=True)
            dl_s[:, cols] = jnp.broadcast_to(delta, (s_len, HEAD_DIM))
        for i, d, q0, k0, n in _attn_blocks(s_len):
            qrows, krows = _rows(q0, ATTN_BLOCK, d), _rows(k0, 2 * ATTN_BLOCK, d)
            qv = _stack_heads(q_s[qrows, :]).astype(BF16)
            kk = k_s[krows, :].astype(BF16)
            vv = v_s[krows, :].astype(BF16)
            do16 = _stack_heads(do_ref.at[0][qrows, :]).astype(BF16)
            lse = _stack_stats(lse_ref.at[0][qrows, :])
            delta = _stack_stats(dl_s[qrows, :])
            sc = lax.dot_general(qv, kk, (((1,), (1,)), ((), ())), preferred_element_type=F32)
            pr = jnp.where(_attn_mask(n), jnp.exp(sc - lse), 0.0)
            dv = lax.dot_general(pr.astype(BF16), do16, (((0,), (0,)), ((), ())), preferred_element_type=F32)
            dp = lax.dot_general(do16, vv, (((1,), (1,)), ((), ())), preferred_element_type=F32)
            ds = (pr * (dp - delta)).astype(BF16)
            dq = jnp.dot(ds, kk, preferred_element_type=F32)
            dk = lax.dot_general(ds, qv, (((0,), (0,)), ((), ())), preferred_element_type=F32)
            for g in range(GQA):
                cols = slice(g * HEAD_DIM, (g + 1) * HEAD_DIM)
                dq_s[qrows, cols] += dq[g * ATTN_BLOCK:(g + 1) * ATTN_BLOCK]
            dk_s[krows, :] += dk
            dv_s[krows, :] += dv
        swap = _swap_matrix()
        cos_v, sin_v = tab_ref[0, :, :HEAD_DIM], tab_ref[0, :, HEAD_DIM:]
        for g in range(GQA):
            cols = slice(g * HEAD_DIM, (g + 1) * HEAD_DIM)
            dq_ref[0, :, cols] = _rope(dq_s[:, cols], cos_v, sin_v, swap, HEAD_DIM ** -0.5, True)
        dk_ref[0, 0] = _rope(dk_s[ATTN_PAD:ATTN_PAD + s_len, :], cos_v, sin_v, swap, 1.0, True)
        dv_ref[0, 0] = dv_s[ATTN_PAD:ATTN_PAD + s_len, :]

    kv_shape = jax.ShapeDtypeStruct((b, N_KV_HEADS, s_len, HEAD_DIM), F32)
    return pl.pallas_call(
        body, name=name, grid=(b, N_KV_HEADS),
        in_specs=[q_spec, kv_spec, t_spec, o_spec, o_spec, o_spec],
        out_specs=[o_spec, kv_out, kv_out],
        out_shape=[jax.ShapeDtypeStruct((b, s_len, ATTN_WIDTH), F32), kv_shape, kv_shape],
        scratch_shapes=[pltpu.VMEM((s_len, Q_GROUP_W), F32), pltpu.VMEM((ATTN_PAD + s_len, HEAD_DIM), F32),
                        pltpu.VMEM((ATTN_PAD + s_len, HEAD_DIM), F32), pltpu.VMEM((s_len, Q_GROUP_W), F32),
                        pltpu.VMEM((s_len, Q_GROUP_W), F32), pltpu.VMEM((ATTN_PAD + s_len, HEAD_DIM), F32),
                        pltpu.VMEM((ATTN_PAD + s_len, HEAD_DIM), F32)],
        compiler_params=pltpu.CompilerParams(dimension_semantics=("arbitrary", "arbitrary"), vmem_limit_bytes=ATTN_VMEM_LIMIT),
    )(proj3, proj3, rope_tab, attn3, lse3, d_attn3)


HALF_W = 2 * HEAD_DIM
N_HALF = Q_GROUP_W // HALF_W
_ATTN_BIAS_BUF = pltpu.VMEM((2, GQA * ATTN_BLOCK, 2 * ATTN_BLOCK), F32)


def _attn_bias(bias_s):
    for first in (0, 1):
        bias_s[first] = jnp.where(_attn_mask(first), 0.0, NEG_BIG)


def _attn_prologue(q_refs, kv_ref, tab_ref, q_s, kv_s, hk, s_len):
    swap = _swap_matrix()
    cos_v, sin_v = tab_ref[0, :, :HEAD_DIM], tab_ref[0, :, HEAD_DIM:]
    for j in range(N_HALF):
        for e in range(2):
            cols = slice(e * HEAD_DIM, (e + 1) * HEAD_DIM)
            q_s[j][:, cols] = _rope(q_refs[j][0, :, cols], cos_v, sin_v, swap, HEAD_DIM ** -0.5, False)
    kv_s[0:ATTN_PAD, :] = jnp.zeros((ATTN_PAD, HALF_W), F32)
    for h in range(N_KV_HEADS):
        @pl.when(hk == h)
        def _():
            kv_s[ATTN_PAD:ATTN_PAD + s_len, :HEAD_DIM] = _rope(kv_ref[0, :, h * HEAD_DIM:(h + 1) * HEAD_DIM], cos_v, sin_v,
                                                               swap, 1.0, False)
            kv_s[ATTN_PAD:ATTN_PAD + s_len, HEAD_DIM:] = kv_ref[0, :, LANE + h * HEAD_DIM:LANE + (h + 1) * HEAD_DIM]


def _stack_heads(halves):
    return jnp.concatenate([h[:, e * HEAD_DIM:(e + 1) * HEAD_DIM] for h in halves for e in range(2)], axis=0)


def _unstack_heads(x, j):
    return jnp.concatenate([x[(2 * j + e) * ATTN_BLOCK:(2 * j + e + 1) * ATTN_BLOCK] for e in range(2)], axis=1)


def _stack_stats(halves):
    return jnp.concatenate([jnp.max(h[:, e * HEAD_DIM:(e + 1) * HEAD_DIM], axis=1, keepdims=True)
                            for h in halves for e in range(2)], axis=0)


def _attn_in_specs(s_len):
    assert K_COL % (2 * LANE) == 0 and V_COL == K_COL + LANE

    def halves(first_tile):
        return [pl.BlockSpec((1, s_len, HALF_W), functools.partial(lambda b, h, j: (b, 0, first_tile + N_HALF * h + j), j=j))
                for j in range(N_HALF)]

    kv_spec = pl.BlockSpec((1, s_len, 2 * LANE), lambda b, h: (b, 0, K_COL // (2 * LANE)))
    t_spec = pl.BlockSpec((1, s_len, 2 * HEAD_DIM), lambda b, h: (b, 0, 0))
    o_spec = pl.BlockSpec((1, s_len, Q_GROUP_W), lambda b, h: (b, 0, h))
    return halves(Q_COL // HALF_W), kv_spec, t_spec, o_spec, halves(0)


class SideCopy:
    def __init__(self, side, *, n_in, n_out, grid):
        self.side, self.n_in, self.n_out, self.grid = side, n_in, n_out, grid
        hbm = pl.BlockSpec(memory_space=pltpu.HBM)
        if side is None:
            self.in_specs, self.out_specs, self.out_shape, self.scratch, self.args = [], [], [], [], []
            return
        src, per_dest = side
        shape = src.shape if per_dest else (N_CHIPS,) + src.shape
        self.in_specs, self.out_specs, self.args = [hbm], [hbm], [src]
        self.out_shape = [jax.ShapeDtypeStruct(shape, src.dtype)]
        self.scratch = [pltpu.SemaphoreType.DMA((N_CHIPS - 1,)), pltpu.SemaphoreType.DMA((N_CHIPS - 1,)), pltpu.SemaphoreType.DMA]

    def wrap(self, body):
        if self.side is None:
            return body
        n_in, n_out, grid, per_dest = self.n_in, self.n_out, self.grid, self.side[1]

        def wrapped(*refs):
            ins, src = refs[:n_in], refs[n_in]
            outs, dst = refs[n_in + 1:n_in + 1 + n_out], refs[n_in + 1 + n_out]
            scratch, sems = refs[n_in + 2 + n_out:-3], refs[-3:]
            ids = [pl.program_id(a) for a in range(len(grid))]
            first = functools.reduce(lambda p, q: p & q, [i == 0 for i in ids])
            last = functools.reduce(lambda p, q: p & q, [i == g - 1 for i, g in zip(ids, grid)])

            @pl.when(first)
            def _():
                local, sends, _ = _chip_copies(src, dst, *sems, per_dest)
                local.start()
                for cp in sends:
                    cp.start()

            body(*ins, *outs, *scratch)

            @pl.when(last)
            def _():
                local, sends, recvs = _chip_copies(src, dst, *sems, per_dest)
                for cp in recvs:
                    cp.wait_recv()
                for cp in sends:
                    cp.wait_send()
                local.wait()

        return wrapped


def _chip_copies(src_ref, dst_ref, send_sems, recv_sems, local_sem, per_dest):
    x, y, c = lax.axis_index("x"), lax.axis_index("y"), lax.axis_index("c")
    chip = 2 * x + y
    own = src_ref.at[chip] if per_dest else src_ref
    local = pltpu.make_async_copy(own, dst_ref.at[chip], local_sem)
    sends, recvs = [], []
    for k, (px, py) in enumerate([(1 - x, y), (x, 1 - y), (1 - x, 1 - y)]):
        peer = dict(send_sem=send_sems.at[k], recv_sem=recv_sems.at[k], device_id=(px, py, c), device_id_type=MESH)
        sends.append(pltpu.make_async_remote_copy(src_ref=src_ref.at[2 * px + py] if per_dest else src_ref,
                                                  dst_ref=dst_ref.at[chip], **peer))
        recvs.append(pltpu.make_async_remote_copy(src_ref=own, dst_ref=dst_ref.at[2 * px + py], **peer))
    return local, sends, recvs


def attn_fwd(proj3, rope_tab, *, name, side=None):
    b, s_len, _ = proj3.shape
    q_specs, kv_spec, t_spec, o_spec, _ = _attn_in_specs(s_len)
    n_br = len(DILATIONS)

    def body(*refs):
        q_refs, (kv_ref, tab_ref, o_ref, lse_ref) = refs[:N_HALF], refs[N_HALF:N_HALF + 4]
        scratch = refs[N_HALF + 4:]
        q_s, kv_s = scratch[:N_HALF], scratch[N_HALF]
        o_s = [scratch[N_HALF + 1 + i * N_HALF:N_HALF + 1 + (i + 1) * N_HALF] for i in range(n_br)]
        l_s = [scratch[N_HALF + 1 + (n_br + i) * N_HALF:N_HALF + 1 + (n_br + i + 1) * N_HALF] for i in range(n_br)]
        bias_s = scratch[-1]
        _attn_prologue(q_refs, kv_ref, tab_ref, q_s, kv_s, pl.program_id(1), s_len)
        _attn_bias(bias_s)
        for i, d, q0, k0, n in _attn_blocks(s_len):
            qrows = _rows(q0, ATTN_BLOCK, d)
            qv = _stack_heads([q_s[j][qrows, :] for j in range(N_HALF)]).astype(BF16)
            kvb = kv_s[_rows(k0, 2 * ATTN_BLOCK, d), :].astype(BF16)
            kk, vv = kvb[:, :HEAD_DIM], kvb[:, HEAD_DIM:]
            sc = lax.dot_general(qv, kk, (((1,), (1,)), ((), ())), preferred_element_type=F32)
            sc = sc + bias_s[min(n, 1)]
            m = jnp.max(sc, axis=-1, keepdims=True)
            pr = jnp.exp(sc - m)
            den = jnp.sum(pr, axis=-1, keepdims=True)
            o = jnp.dot(pr.astype(BF16), vv, preferred_element_type=F32) / den
            lse_b = jnp.broadcast_to(m + jnp.log(den), (GQA * ATTN_BLOCK, HEAD_DIM))
            for j in range(N_HALF):
                o_s[i][j][qrows, :] = _unstack_heads(o, j)
                l_s[i][j][qrows, :] = _unstack_heads(lse_b, j)
        step = 256
        for t0 in range(0, s_len, step):
            rs = pl.ds(t0, step)
            for j in range(N_HALF):
                ls = [l_s[i][j][rs, :] for i in range(n_br)]
                m = functools.reduce(jnp.maximum, ls)
                es = [jnp.exp(l - m) for l in ls]
                tot = functools.reduce(lambda a, c: a + c, es)
                inv = 1.0 / tot
                acc = None
                for i in range(n_br):
                    term = (es[i] * inv) * o_s[i][j][rs, :]
                    acc = term if acc is None else acc + term
                o_ref[0, rs, j * HALF_W:(j + 1) * HALF_W] = acc
                lse_ref[0, rs, j * HALF_W:(j + 1) * HALF_W] = m + jnp.log(tot)

    half_buf = pltpu.VMEM((s_len, HALF_W), F32)
    call = SideCopy(side, n_in=N_HALF + 2, n_out=2, grid=(b, N_KV_HEADS))
    return pl.pallas_call(
        call.wrap(body), name=name, grid=(b, N_KV_HEADS), in_specs=q_specs + [kv_spec, t_spec] + call.in_specs,
        out_specs=[o_spec, o_spec] + call.out_specs,
        out_shape=[jax.ShapeDtypeStruct((b, s_len, ATTN_WIDTH), F32)] * 2 + call.out_shape,
        scratch_shapes=[half_buf] * N_HALF + [pltpu.VMEM((ATTN_PAD + s_len, HALF_W), F32)] + [half_buf] * (2 * n_br * N_HALF)
        + [_ATTN_BIAS_BUF] + call.scratch,
        compiler_params=pltpu.CompilerParams(dimension_semantics=("arbitrary", "arbitrary"), vmem_limit_bytes=ATTN_VMEM_LIMIT),
    )(*([proj3] * (N_HALF + 1)), rope_tab, *call.args)


def attn_bwd(proj3, rope_tab, attn3, lse3, d_attn3, *, name, side=None):
    b, s_len, _ = proj3.shape
    q_specs, kv_spec, t_spec, o_spec, half_specs = _attn_in_specs(s_len)
    kv_out = pl.BlockSpec((1, 1, s_len, HEAD_DIM), lambda bi, h: (bi, h, 0, 0))

    def body(*refs):
        q_refs = refs[:N_HALF]
        kv_ref, tab_ref, o_ref = refs[N_HALF:N_HALF + 3]
        lse_refs = refs[N_HALF + 3:2 * N_HALF + 3]
        do_refs = refs[2 * N_HALF + 3:3 * N_HALF + 3]
        dq_ref, dk_ref, dv_ref = refs[3 * N_HALF + 3:3 * N_HALF + 6]
        scratch = refs[3 * N_HALF + 6:]
        q_s, kv_s = scratch[:N_HALF], scratch[N_HALF]
        dl_s = scratch[N_HALF + 1:2 * N_HALF + 1]
        dq_s = scratch[2 * N_HALF + 1:3 * N_HALF + 1]
        dkv_s = scratch[3 * N_HALF + 1]
        bias_s = scratch[-1]
        _attn_prologue(q_refs, kv_ref, tab_ref, q_s, kv_s, pl.program_id(1), s_len)
        _attn_bias(bias_s)
        dkv_s[...] = jnp.zeros_like(dkv_s)
        for j in range(N_HALF):
            dq_s[j][...] = jnp.zeros_like(dq_s[j])
            for e in range(2):
                cols = slice(e * HEAD_DIM, (e + 1) * HEAD_DIM)
                ocols = slice(j * HALF_W + e * HEAD_DIM, j * HALF_W + (e + 1) * HEAD_DIM)
                delta = jnp.sum(do_refs[j][0, :, cols] * o_ref[0, :, ocols], axis=1, keepdims=True)
                dl_s[j][:, cols] = jnp.broadcast_to(delta, (s_len, HEAD_DIM))
        for i, d, q0, k0, n in _attn_blocks(s_len):
            qrows, krows = _rows(q0, ATTN_BLOCK, d), _rows(k0, 2 * ATTN_BLOCK, d)
            qv = _stack_heads([q_s[j][qrows, :] for j in range(N_HALF)]).astype(BF16)
            kvb = kv_s[krows, :].astype(BF16)
            kk, vv = kvb[:, :HEAD_DIM], kvb[:, HEAD_DIM:]
            do16 = _stack_heads([do_refs[j].at[0][qrows, :] for j in range(N_HALF)]).astype(BF16)
            lse = _stack_stats([lse_refs[j].at[0][qrows, :] for j in range(N_HALF)])
            delta = _stack_stats([dl_s[j][qrows, :] for j in range(N_HALF)])
            sc = lax.dot_general(qv, kk, (((1,), (1,)), ((), ())), preferred_element_type=F32)
            pr = jnp.exp(sc + bias_s[min(n, 1)] - lse)
            dv = lax.dot_general(pr.astype(BF16), do16, (((0,), (0,)), ((), ())), preferred_element_type=F32)
            dp = lax.dot_general(do16, vv, (((1,), (1,)), ((), ())), preferred_element_type=F32)
            ds = (pr * (dp - delta)).astype(BF16)
            dq = jnp.dot(ds, kk, preferred_element_type=F32)
            dk = lax.dot_general(ds, qv, (((0,), (0,)), ((), ())), preferred_element_type=F32)
            for j in range(N_HALF):
                dq_s[j][qrows, :] += _unstack_heads(dq, j)
            dkv_s[krows, :] += jnp.concatenate([dk, dv], axis=1)
        swap = _swap_matrix()
        cos_v, sin_v = tab_ref[0, :, :HEAD_DIM], tab_ref[0, :, HEAD_DIM:]
        for j in range(N_HALF):
            for e in range(2):
                cols = slice(e * HEAD_DIM, (e + 1) * HEAD_DIM)
                ocols = slice(j * HALF_W + e * HEAD_DIM, j * HALF_W + (e + 1) * HEAD_DIM)
                dq_ref[0, :, ocols] = _rope(dq_s[j][:, cols], cos_v, sin_v, swap, HEAD_DIM ** -0.5, True)
        dk_ref[0, 0] = _rope(dkv_s[ATTN_PAD:ATTN_PAD + s_len, :HEAD_DIM], cos_v, sin_v, swap, 1.0, True)
        dv_ref[0, 0] = dkv_s[ATTN_PAD:ATTN_PAD + s_len, HEAD_DIM:]

    kv_shape = jax.ShapeDtypeStruct((b, N_KV_HEADS, s_len, HEAD_DIM), F32)
    half_buf = pltpu.VMEM((s_len, HALF_W), F32)
    pad_buf = pltpu.VMEM((ATTN_PAD + s_len, HALF_W), F32)
    call = SideCopy(side, n_in=3 * N_HALF + 3, n_out=3, grid=(b, N_KV_HEADS))
    return pl.pallas_call(
        call.wrap(body), name=name, grid=(b, N_KV_HEADS),
        in_specs=q_specs + [kv_spec, t_spec, o_spec] + half_specs + half_specs + call.in_specs,
        out_specs=[o_spec, kv_out, kv_out] + call.out_specs,
        out_shape=[jax.ShapeDtypeStruct((b, s_len, ATTN_WIDTH), F32), kv_shape, kv_shape] + call.out_shape,
        scratch_shapes=[half_buf] * N_HALF + [pad_buf] + [half_buf] * (2 * N_HALF) + [pad_buf, _ATTN_BIAS_BUF] + call.scratch,
        compiler_params=pltpu.CompilerParams(dimension_semantics=("arbitrary", "arbitrary"), vmem_limit_bytes=ATTN_VMEM_LIMIT),
    )(*([proj3] * (N_HALF + 1)), rope_tab, attn3, *([lse3] * N_HALF), *([d_attn3] * N_HALF), *call.args)


CONV_TC = 256
CONV_COL0 = XBC_COL // CONV_TC


def _shift_down(u, s):
    if s == 0:
        return u
    rows = lax.broadcasted_iota(jnp.int32, u.shape, 0)
    return jnp.where(rows >= s, pltpu.roll(u, s, 0), 0.0)


def _shift_up(u, s):
    if s == 0:
        return u
    n = u.shape[0]
    rows = lax.broadcasted_iota(jnp.int32, u.shape, 0)
    return jnp.where(rows < n - s, pltpu.roll(u, n - s, 0), 0.0)


def conv_silu_fwd(proj3, w, bias, *, name):
    b, s, _ = proj3.shape
    u_spec = pl.BlockSpec((1, s, CONV_TC), lambda j, bi: (bi, 0, CONV_COL0 + j))
    o_spec = pl.BlockSpec((1, s, CONV_TC), lambda j, bi: (bi, 0, j))
    w_spec = pl.BlockSpec((CONV_WIDTH, CONV_TC), lambda j, bi: (0, j))
    b_spec = pl.BlockSpec((1, CONV_TC), lambda j, bi: (0, j))

    def body(u_ref, w_ref, b_ref, o_ref):
        u = u_ref[0]
        y = jnp.broadcast_to(b_ref[...], u.shape)
        for k in range(CONV_WIDTH):
            y = y + w_ref[k:k + 1, :] * _shift_down(u, CONV_WIDTH - 1 - k)
        o_ref[0] = y * jax.nn.sigmoid(y)

    return pl.pallas_call(
        body, name=name, grid=(CONV_CH // CONV_TC, b), in_specs=[u_spec, w_spec, b_spec], out_specs=o_spec,
        out_shape=jax.ShapeDtypeStruct((b, s, CONV_CH), F32),
        compiler_params=_params(("parallel", "arbitrary")),
    )(proj3, w, bias)


def conv_silu_bwd(proj3, w, bias, dact, *, name):
    b, s, _ = proj3.shape
    u_spec = pl.BlockSpec((1, s, CONV_TC), lambda j, bi: (bi, 0, CONV_COL0 + j))
    o_spec = pl.BlockSpec((1, s, CONV_TC), lambda j, bi: (bi, 0, j))
    w_spec = pl.BlockSpec((CONV_WIDTH, CONV_TC), lambda j, bi: (0, j))
    b_spec = pl.BlockSpec((1, CONV_TC), lambda j, bi: (0, j))

    def body(u_ref, w_ref, b_ref, g_ref, du_ref, dw_ref, db_ref):
        bi = pl.program_id(1)

        @pl.when(bi == 0)
        def _():
            dw_ref[...] = jnp.zeros_like(dw_ref)
            db_ref[...] = jnp.zeros_like(db_ref)

        u = u_ref[0]
        y = jnp.broadcast_to(b_ref[...], u.shape)
        shifted = [_shift_down(u, CONV_WIDTH - 1 - k) for k in range(CONV_WIDTH)]
        for k in range(CONV_WIDTH):
            y = y + w_ref[k:k + 1, :] * shifted[k]
        sig = jax.nn.sigmoid(y)
        dy = g_ref[0] * (sig * (1.0 + y * (1.0 - sig)))
        du = jnp.zeros_like(u)
        for k in range(CONV_WIDTH):
            du = du + w_ref[k:k + 1, :] * _shift_up(dy, CONV_WIDTH - 1 - k)
            dw_ref[k:k + 1, :] += jnp.sum(dy * shifted[k], axis=0, keepdims=True)
        du_ref[0] = du
        db_ref[...] += jnp.sum(dy, axis=0, keepdims=True)

    return pl.pallas_call(
        body, name=name, grid=(CONV_CH // CONV_TC, b), in_specs=[u_spec, w_spec, b_spec, o_spec],
        out_specs=[o_spec, w_spec, b_spec],
        out_shape=[jax.ShapeDtypeStruct((b, s, CONV_CH), F32), jax.ShapeDtypeStruct((CONV_WIDTH, CONV_CH), F32),
                   jax.ShapeDtypeStruct((1, CONV_CH), F32)],
        compiler_params=_params(("parallel", "arbitrary")),
    )(proj3, w, bias, dact)


def _softplus(z):
    e = jnp.exp(-jnp.abs(z))
    u = 1.0 + e
    log1p = jnp.where(u == 1.0, e, jnp.log(u) * e / jnp.where(u == 1.0, 1.0, u - 1.0))
    return jnp.maximum(z, 0.0) + log1p


def _tri(lower):
    r = lax.broadcasted_iota(jnp.int32, (CHUNK, CHUNK), 0)
    c = lax.broadcasted_iota(jnp.int32, (CHUNK, CHUNK), 1)
    return (r >= c) if lower else (r <= c)


def _ssd_common(dtr_ref, dtb_ref, alog_ref):
    z = dtr_ref[0] + dtb_ref[...]
    dt = _softplus(z)
    aneg = -jnp.exp(alog_ref[...])
    acs = _dot01_left(_tri(True).astype(BF16), dt * aneg)
    return z, dt, aneg, acs


def _col(mat, onehot):
    return jnp.sum(mat * onehot, axis=1, keepdims=True)


def _ssd_head(x, dt_j, acs_j, cb, tri_mask, last_row, acs_row=None):
    acs_last = jnp.sum(acs_j * last_row, axis=0, keepdims=True)
    xg = x * dt_j
    bc = jnp.broadcast_to(acs_j, (CHUNK, CHUNK))
    dm = bc - (bc.T if acs_row is None else jnp.broadcast_to(acs_row, (CHUNK, CHUNK)))
    lm = jnp.where(tri_mask, jnp.exp(jnp.where(tri_mask, dm, 0.0)), 0.0)
    mm = cb * lm
    decay_s = jnp.exp(acs_last - acs_j)
    return acs_last, xg, lm, mm, decay_s


def _ssd_specs(nc, reverse):
    cidx = (lambda c: nc - 1 - c) if reverse else (lambda c: c)
    act_spec = pl.BlockSpec((1, CHUNK, CONV_CH), lambda b, c: (b, cidx(c), 0))
    y_spec = pl.BlockSpec((1, CHUNK, SSM_INNER), lambda b, c: (b, cidx(c), 0))
    dt_in_spec = pl.BlockSpec((1, CHUNK, LANE), lambda b, c: (b, cidx(c), DT_COL // LANE))
    dt_out_spec = pl.BlockSpec((1, CHUNK, LANE), lambda b, c: (b, cidx(c), 0))
    par_spec = pl.BlockSpec((1, LANE), lambda b, c: (0, 0))
    h_spec = pl.BlockSpec((1, SSM_HEADS, 1, SSM_P, D_STATE), lambda b, c: (b, 0, cidx(c), 0, 0))
    return act_spec, y_spec, dt_in_spec, dt_out_spec, par_spec, h_spec


def _head_cols(h):
    return slice(h * SSM_P, (h + 1) * SSM_P)


def _group_cols(g, which):
    start = SSM_INNER + which * SSM_GROUPS * D_STATE + g * D_STATE
    return slice(start, start + D_STATE)


def ssd_fwd(act3, proj3, dtb, alog, dsk, *, name):
    b, s, _ = act3.shape
    nc = s // CHUNK
    act_spec, y_spec, dt_in_spec, _, par_spec, h_spec = _ssd_specs(nc, False)

    def body(act_ref, dtr_ref, dtb_ref, alog_ref, dsk_ref, y_ref, hp_ref, state):
        c = pl.program_id(1)

        @pl.when(c == 0)
        def _():
            state[...] = jnp.zeros_like(state)

        _, dt, _, acs = _ssd_common(dtr_ref, dtb_ref, alog_ref)
        acs_t = acs.T
        tri_mask = _tri(True)
        last_row = (lax.broadcasted_iota(jnp.int32, (CHUNK, 1), 0) == CHUNK - 1).astype(F32)
        for g in range(SSM_GROUPS):
            b16 = act_ref[0, :, _group_cols(g, 0)].astype(BF16)
            c16 = act_ref[0, :, _group_cols(g, 1)].astype(BF16)
            cb = lax.dot_general(c16, b16, (((1,), (1,)), ((), ())), preferred_element_type=F32)
            for j in range(HEADS_PER_GROUP):
                hidx = g * HEADS_PER_GROUP + j
                x = act_ref[0, :, _head_cols(hidx)]
                dt_j, acs_j = dt[:, hidx:hidx + 1], acs[:, hidx:hidx + 1]
                acs_last, xg, _, mm, decay_s = _ssd_head(x, dt_j, acs_j, cb, tri_mask, last_row, acs_t[hidx:hidx + 1, :])
                y_diag = jnp.dot(mm.astype(BF16), xg.astype(BF16), preferred_element_type=F32)
                st = lax.dot_general((xg * decay_s).astype(BF16), b16, (((0,), (0,)), ((), ())), preferred_element_type=F32)
                hp = state[hidx]
                hp_ref[0, hidx, 0] = hp
                y_off = lax.dot_general(c16, hp.astype(BF16), (((1,), (1,)), ((), ())), preferred_element_type=F32)
                d_j = dsk_ref[:, hidx:hidx + 1]
                y_ref[0, :, _head_cols(hidx)] = y_diag + y_off * jnp.exp(acs_j) + d_j * x
                state[hidx] = hp * jnp.exp(acs_last) + st

    return pl.pallas_call(
        body, name=name, grid=(b, nc),
        in_specs=[act_spec, dt_in_spec, par_spec, par_spec, par_spec],
        out_specs=[y_spec, h_spec],
        out_shape=[jax.ShapeDtypeStruct((b, s, SSM_INNER), F32),
                   jax.ShapeDtypeStruct((b, SSM_HEADS, nc, SSM_P, D_STATE), F32)],
        scratch_shapes=[pltpu.VMEM((SSM_HEADS, SSM_P, D_STATE), F32)],
        compiler_params=_params(("arbitrary", "arbitrary")),
    )(act3, proj3, dtb, alog, dsk)


def ssd_bwd(act3, proj3, dtb, alog, dsk, hprev, dy3, *, name):
    b, s, _ = act3.shape
    nc = s // CHUNK
    act_spec, y_spec, dt_in_spec, dt_out_spec, par_spec, h_spec = _ssd_specs(nc, True)
    dpar_spec = pl.BlockSpec((8, LANE), lambda bi, c: (0, 0))

    def body(act_ref, dtr_ref, dtb_ref, alog_ref, dsk_ref, hp_ref, dy_ref, dact_ref, ddtr_ref, dpar_ref, dstate):
        bi, c = pl.program_id(0), pl.program_id(1)

        @pl.when(c == 0)
        def _():
            dstate[...] = jnp.zeros_like(dstate)

        @pl.when((bi == 0) & (c == 0))
        def _():
            dpar_ref[...] = jnp.zeros_like(dpar_ref)

        z, dt, aneg, acs = _ssd_common(dtr_ref, dtb_ref, alog_ref)
        acs_t = acs.T
        tri_mask = _tri(True)
        last_row = (lax.broadcasted_iota(jnp.int32, (CHUNK, 1), 0) == CHUNK - 1).astype(F32)
        lanes = lax.broadcasted_iota(jnp.int32, (1, LANE), 1)
        sublanes = lax.broadcasted_iota(jnp.int32, (LANE, 1), 0)
        ddt_mat = jnp.zeros((CHUNK, LANE), F32)
        dacs_mat = jnp.zeros((CHUNK, LANE), F32)
        dacs_rows = jnp.zeros((LANE, CHUNK), F32)
        ddsk_row = jnp.zeros((1, LANE), F32)
        for g in range(SSM_GROUPS):
            b16 = act_ref[0, :, _group_cols(g, 0)].astype(BF16)
            c16 = act_ref[0, :, _group_cols(g, 1)].astype(BF16)
            cb = lax.dot_general(c16, b16, (((1,), (1,)), ((), ())), preferred_element_type=F32)
            dcb = jnp.zeros((CHUNK, CHUNK), F32)
            db_acc = jnp.zeros((CHUNK, D_STATE), F32)
            dc_acc = jnp.zeros((CHUNK, D_STATE), F32)
            for j in range(HEADS_PER_GROUP):
                hidx = g * HEADS_PER_GROUP + j
                onehot = (lanes == hidx).astype(F32)
                x = act_ref[0, :, _head_cols(hidx)]
                dt_j, acs_j = dt[:, hidx:hidx + 1], acs[:, hidx:hidx + 1]
                acs_last, xg, lm, mm, decay_s = _ssd_head(x, dt_j, acs_j, cb, tri_mask, last_row, acs_t[hidx:hidx + 1, :])
                ea = jnp.exp(acs_j)
                cd = jnp.exp(acs_last)
                d_j = dsk_ref[:, hidx:hidx + 1]
                hp = hp_ref[0, hidx, 0]
                hp16 = hp.astype(BF16)
                g_y = dy_ref[0, :, _head_cols(hidx)]
                g_y16 = g_y.astype(BF16)
                g_hn = dstate[hidx]
                g_hn16 = g_hn.astype(BF16)
                xg16 = xg.astype(BF16)
                ddsk_row = ddsk_row + jnp.sum(jnp.sum(g_y * x, axis=1, keepdims=True), axis=0, keepdims=True) * onehot
                d_mm = lax.dot_general(g_y16, xg16, (((1,), (1,)), ((), ())), preferred_element_type=F32)
                d_xg = lax.dot_general(mm.astype(BF16), g_y16, (((0,), (0,)), ((), ())), preferred_element_type=F32)
                dcb = dcb + d_mm * lm
                d_dm = d_mm * mm
                d_acs = jnp.sum(d_dm, axis=1, keepdims=True)
                dacs_rows = dacs_rows + (sublanes == hidx).astype(F32) * jnp.sum(d_dm, axis=0, keepdims=True)
                t_off = lax.dot_general(c16, hp16, (((1,), (1,)), ((), ())), preferred_element_type=F32)
                d_t16 = (g_y * ea).astype(BF16)
                d_acs = d_acs + jnp.sum(g_y * t_off, axis=1, keepdims=True) * ea
                dc_acc = dc_acc + jnp.dot(d_t16, hp16, preferred_element_type=F32)
                d_hp = lax.dot_general(d_t16, c16, (((0,), (0,)), ((), ())), preferred_element_type=F32) + g_hn * cd
                d_last = jnp.sum(jnp.sum(g_hn * hp, axis=1, keepdims=True), axis=0, keepdims=True) * cd
                d_w = lax.dot_general(b16, g_hn16, (((1,), (1,)), ((), ())), preferred_element_type=F32)
                db_acc = db_acc + jnp.dot((xg * decay_s).astype(BF16), g_hn16, preferred_element_type=F32)
                d_xg = d_xg + d_w * decay_s
                d_ds = jnp.sum(d_w * xg, axis=1, keepdims=True) * decay_s
                d_last = d_last + jnp.sum(d_ds, axis=0, keepdims=True)
                d_acs = d_acs - d_ds + d_last * last_row
                dact_ref[0, :, _head_cols(hidx)] = d_j * g_y + d_xg * dt_j
                ddt_mat = ddt_mat + jnp.sum(d_xg * x, axis=1, keepdims=True) * onehot
                dacs_mat = dacs_mat + d_acs * onehot
                dstate[hidx] = d_hp
            dcb16 = dcb.astype(BF16)
            dact_ref[0, :, _group_cols(g, 1)] = dc_acc + jnp.dot(dcb16, b16, preferred_element_type=F32)
            dact_ref[0, :, _group_cols(g, 0)] = db_acc + lax.dot_general(dcb16, c16, (((0,), (0,)), ((), ())),
                                                                         preferred_element_type=F32)
        d_a = _dot01_left(_tri(False).astype(BF16), dacs_mat - dacs_rows.T)
        ddt_mat = ddt_mat + d_a * aneg
        d_raw = ddt_mat * jax.nn.sigmoid(z)
        ddtr_ref[0] = d_raw
        dpar_ref[0:1, :] += jnp.sum(d_raw, axis=0, keepdims=True)
        dpar_ref[1:2, :] += jnp.sum(d_a * dt, axis=0, keepdims=True) * aneg
        dpar_ref[2:3, :] += ddsk_row

    return pl.pallas_call(
        body, name=name, grid=(b, nc),
        in_specs=[act_spec, dt_in_spec, par_spec, par_spec, par_spec, h_spec, y_spec],
        out_specs=[act_spec, dt_out_spec, dpar_spec],
        out_shape=[jax.ShapeDtypeStruct(act3.shape, F32), jax.ShapeDtypeStruct((b, s, LANE), F32),
                   jax.ShapeDtypeStruct((8, LANE), F32)],
        scratch_shapes=[pltpu.VMEM((SSM_HEADS, SSM_P, D_STATE), F32)],
        compiler_params=_params(("arbitrary", "arbitrary")),
    )(act3, proj3, dtb, alog, dsk, hprev, dy3)


def _unused_ssd_specs(nc, reverse):
    cidx = (lambda c: nc - 1 - c) if reverse else (lambda c: c)
    x_spec = pl.BlockSpec((1, HEADS_PER_GROUP, CHUNK, SSM_P), lambda b, c, g: (b, g, cidx(c), 0))
    bc_spec = pl.BlockSpec((1, 1, CHUNK, D_STATE), lambda b, c, g: (b, g, cidx(c), 0))
    dt_spec = pl.BlockSpec((1, CHUNK, LANE), lambda b, c, g: (b, cidx(c), 0))
    par_spec = pl.BlockSpec((1, LANE), lambda b, c, g: (0, 0))
    h_spec = pl.BlockSpec((1, HEADS_PER_GROUP, 1, SSM_P, D_STATE), lambda b, c, g: (b, g, cidx(c), 0, 0))
    return x_spec, bc_spec, dt_spec, par_spec, h_spec


def _unused_ssd_fwd(xs, bm, cm, dtr, dtb, alog, dsk, *, name):
    b, _, s, _ = xs.shape
    nc = s // CHUNK
    x_spec, bc_spec, dt_spec, par_spec, h_spec = _ssd_specs(nc, False)

    def body(x_ref, b_ref, c_ref, dtr_ref, dtb_ref, alog_ref, dsk_ref, y_ref, hp_ref, state):
        c, g = pl.program_id(1), pl.program_id(2)

        @pl.when(c == 0)
        def _():
            state[pl.ds(g * HEADS_PER_GROUP, HEADS_PER_GROUP)] = jnp.zeros((HEADS_PER_GROUP, SSM_P, D_STATE), F32)

        _, dt, _, acs = _ssd_common(dtr_ref, dtb_ref, alog_ref)
        b16, c16 = b_ref[0, 0].astype(BF16), c_ref[0, 0].astype(BF16)
        cb = lax.dot_general(c16, b16, (((1,), (1,)), ((), ())), preferred_element_type=F32)
        tri_mask = _tri(True)
        last_row = (lax.broadcasted_iota(jnp.int32, (CHUNK, 1), 0) == CHUNK - 1).astype(F32)
        lanes = lax.broadcasted_iota(jnp.int32, (1, LANE), 1)
        for j in range(HEADS_PER_GROUP):
            hidx = g * HEADS_PER_GROUP + j
            onehot = (lanes == hidx).astype(F32)
            x = x_ref[0, j]
            dt_j, acs_j = _col(dt, onehot), _col(acs, onehot)
            acs_last, xg, _, mm, decay_s = _ssd_head(x, dt_j, acs_j, cb, tri_mask, last_row)
            xg16 = xg.astype(BF16)
            y_diag = jnp.dot(mm.astype(BF16), xg16, preferred_element_type=F32)
            st = lax.dot_general((xg * decay_s).astype(BF16), b16, (((0,), (0,)), ((), ())), preferred_element_type=F32)
            hp = state[hidx]
            hp_ref[0, j, 0] = hp
            y_off = lax.dot_general(c16, hp.astype(BF16), (((1,), (1,)), ((), ())), preferred_element_type=F32)
            d_j = jnp.sum(dsk_ref[...] * onehot, axis=1, keepdims=True)
            y_ref[0, j] = y_diag + y_off * jnp.exp(acs_j) + d_j * x
            state[hidx] = hp * jnp.exp(acs_last) + st

    return pl.pallas_call(
        body, name=name, grid=(b, nc, SSM_GROUPS),
        in_specs=[x_spec, bc_spec, bc_spec, dt_spec, par_spec, par_spec, par_spec],
        out_specs=[x_spec, h_spec],
        out_shape=[jax.ShapeDtypeStruct(xs.shape, F32),
                   jax.ShapeDtypeStruct((b, SSM_HEADS, nc, SSM_P, D_STATE), F32)],
        scratch_shapes=[pltpu.VMEM((SSM_HEADS, SSM_P, D_STATE), F32)],
        compiler_params=_params(("arbitrary", "arbitrary", "arbitrary")),
    )(xs, bm, cm, dtr, dtb, alog, dsk)


def _unused_ssd_bwd(xs, bm, cm, dtr, dtb, alog, dsk, hprev, dy, *, name):
    b, _, s, _ = xs.shape
    nc = s // CHUNK
    x_spec, bc_spec, dt_spec, par_spec, h_spec = _ssd_specs(nc, True)
    dpar_spec = pl.BlockSpec((8, LANE), lambda bi, c, g: (0, 0))

    def body(x_ref, b_ref, c_ref, dtr_ref, dtb_ref, alog_ref, dsk_ref, hp_ref, dy_ref,
             dx_ref, db_ref, dc_ref, ddtr_ref, dpar_ref, dstate):
        bi, c, g = pl.program_id(0), pl.program_id(1), pl.program_id(2)

        @pl.when(c == 0)
        def _():
            dstate[pl.ds(g * HEADS_PER_GROUP, HEADS_PER_GROUP)] = jnp.zeros((HEADS_PER_GROUP, SSM_P, D_STATE), F32)

        @pl.when((bi == 0) & (c == 0) & (g == 0))
        def _():
            dpar_ref[...] = jnp.zeros_like(dpar_ref)

        z, dt, aneg, acs = _ssd_common(dtr_ref, dtb_ref, alog_ref)
        bv, cv = b_ref[0, 0], c_ref[0, 0]
        b16, c16 = bv.astype(BF16), cv.astype(BF16)
        cb = lax.dot_general(c16, b16, (((1,), (1,)), ((), ())), preferred_element_type=F32)
        tri_mask = _tri(True)
        last_row = (lax.broadcasted_iota(jnp.int32, (CHUNK, 1), 0) == CHUNK - 1).astype(F32)
        lanes = lax.broadcasted_iota(jnp.int32, (1, LANE), 1)
        dcb = jnp.zeros((CHUNK, CHUNK), F32)
        db_acc = jnp.zeros((CHUNK, D_STATE), F32)
        dc_acc = jnp.zeros((CHUNK, D_STATE), F32)
        ddt_mat = jnp.zeros((CHUNK, LANE), F32)
        dacs_mat = jnp.zeros((CHUNK, LANE), F32)
        ddsk_row = jnp.zeros((1, LANE), F32)
        for j in range(HEADS_PER_GROUP):
            hidx = g * HEADS_PER_GROUP + j
            onehot = (lanes == hidx).astype(F32)
            x = x_ref[0, j]
            dt_j, acs_j = _col(dt, onehot), _col(acs, onehot)
            acs_last, xg, lm, mm, decay_s = _ssd_head(x, dt_j, acs_j, cb, tri_mask, last_row)
            ea = jnp.exp(acs_j)
            cd = jnp.exp(acs_last)
            d_j = jnp.sum(dsk_ref[...] * onehot, axis=1, keepdims=True)
            hp = hp_ref[0, j, 0]
            hp16 = hp.astype(BF16)
            g_y = dy_ref[0, j]
            g_y16 = g_y.astype(BF16)
            g_hn = dstate[hidx]
            g_hn16 = g_hn.astype(BF16)
            xg16 = xg.astype(BF16)
            ddsk_row = ddsk_row + jnp.sum(jnp.sum(g_y * x, axis=1, keepdims=True), axis=0, keepdims=True) * onehot
            d_mm = lax.dot_general(g_y16, xg16, (((1,), (1,)), ((), ())), preferred_element_type=F32)
            d_xg = lax.dot_general(mm.astype(BF16), g_y16, (((0,), (0,)), ((), ())), preferred_element_type=F32)
            dcb = dcb + d_mm * lm
            d_dm = d_mm * mm
            d_acs = jnp.sum(d_dm, axis=1, keepdims=True) - jnp.sum(d_dm.T, axis=1, keepdims=True)
            t_off = lax.dot_general(c16, hp16, (((1,), (1,)), ((), ())), preferred_element_type=F32)
            d_t16 = (g_y * ea).astype(BF16)
            d_acs = d_acs + jnp.sum(g_y * t_off, axis=1, keepdims=True) * ea
            dc_acc = dc_acc + jnp.dot(d_t16, hp16, preferred_element_type=F32)
            d_hp = lax.dot_general(d_t16, c16, (((0,), (0,)), ((), ())), preferred_element_type=F32) + g_hn * cd
            d_last = jnp.sum(jnp.sum(g_hn * hp, axis=1, keepdims=True), axis=0, keepdims=True) * cd
            d_w = lax.dot_general(b16, g_hn16, (((1,), (1,)), ((), ())), preferred_element_type=F32)
            db_acc = db_acc + jnp.dot((xg * decay_s).astype(BF16), g_hn16, preferred_element_type=F32)
            d_xg = d_xg + d_w * decay_s
            d_ds = jnp.sum(d_w * xg, axis=1, keepdims=True) * decay_s
            d_last = d_last + jnp.sum(d_ds, axis=0, keepdims=True)
            d_acs = d_acs - d_ds + d_last * last_row
            dx_ref[0, j] = d_j * g_y + d_xg * dt_j
            ddt_mat = ddt_mat + jnp.sum(d_xg * x, axis=1, keepdims=True) * onehot
            dacs_mat = dacs_mat + d_acs * onehot
            dstate[hidx] = d_hp
        dcb16 = dcb.astype(BF16)
        dc_ref[0, 0] = dc_acc + jnp.dot(dcb16, b16, preferred_element_type=F32)
        db_ref[0, 0] = db_acc + lax.dot_general(dcb16, c16, (((0,), (0,)), ((), ())), preferred_element_type=F32)
        d_a = _dot01_left(_tri(False).astype(BF16), dacs_mat)
        ddt_mat = ddt_mat + d_a * aneg
        d_aneg = jnp.sum(d_a * dt, axis=0, keepdims=True)
        d_raw = ddt_mat * jax.nn.sigmoid(z)

        @pl.when(g == 0)
        def _():
            ddtr_ref[0] = d_raw

        @pl.when(g != 0)
        def _():
            ddtr_ref[0] += d_raw

        dpar_ref[0:1, :] += jnp.sum(d_raw, axis=0, keepdims=True)
        dpar_ref[1:2, :] += d_aneg * aneg
        dpar_ref[2:3, :] += ddsk_row

    return pl.pallas_call(
        body, name=name, grid=(b, nc, SSM_GROUPS),
        in_specs=[x_spec, bc_spec, bc_spec, dt_spec, par_spec, par_spec, par_spec, h_spec, x_spec],
        out_specs=[x_spec, bc_spec, bc_spec, dt_spec, dpar_spec],
        out_shape=[jax.ShapeDtypeStruct(xs.shape, F32), jax.ShapeDtypeStruct(bm.shape, F32),
                   jax.ShapeDtypeStruct(cm.shape, F32), jax.ShapeDtypeStruct(dtr.shape, F32),
                   jax.ShapeDtypeStruct((8, LANE), F32)],
        scratch_shapes=[pltpu.VMEM((SSM_HEADS, SSM_P, D_STATE), F32)],
        compiler_params=_params(("arbitrary", "arbitrary", "arbitrary")),
    )(xs, bm, cm, dtr, dtb, alog, dsk, hprev, dy)


SSD_INTERLEAVE = 8


def _each(f, *lists):
    return [f(*a) for a in zip(*lists)]


def _nt(a, b):
    return lax.dot_general(a, b, (((1,), (1,)), ((), ())), preferred_element_type=F32)


def _tn(a, b):
    return lax.dot_general(a, b, (((0,), (0,)), ((), ())), preferred_element_type=F32)


def _nn(a, b):
    return jnp.dot(a, b, preferred_element_type=F32)


def _rowsum(a):
    return jnp.sum(a, axis=1, keepdims=True)


def _colsum(a):
    return jnp.sum(a, axis=0, keepdims=True)


def _bf(a):
    return a.astype(BF16)


def _head_batches(g):
    first = g * HEADS_PER_GROUP
    return [list(range(first + k, first + k + SSD_INTERLEAVE)) for k in range(0, HEADS_PER_GROUP, SSD_INTERLEAVE)]


def _decay_matrix(acs_j, acs_row, tri_mask):
    dm = jnp.broadcast_to(acs_j, (CHUNK, CHUNK)) - jnp.broadcast_to(acs_row, (CHUNK, CHUNK))
    return jnp.where(tri_mask, jnp.exp(jnp.where(tri_mask, dm, 0.0)), 0.0)


def ssd_fwd(act3, proj3, dtb, alog, dsk, *, name):
    b, s, _ = act3.shape
    nc = s // CHUNK
    act_spec, y_spec, dt_in_spec, _, par_spec, h_spec = _ssd_specs(nc, False)

    def body(act_ref, dtr_ref, dtb_ref, alog_ref, dsk_ref, y_ref, hp_ref, state):
        c = pl.program_id(1)

        @pl.when(c == 0)
        def _():
            state[...] = jnp.zeros_like(state)

        _, dt, _, acs = _ssd_common(dtr_ref, dtb_ref, alog_ref)
        acs_t = acs.T
        tri_mask = _tri(True)
        last_row = (lax.broadcasted_iota(jnp.int32, (CHUNK, 1), 0) == CHUNK - 1).astype(F32)
        for g in range(SSM_GROUPS):
            b16 = _bf(act_ref[0, :, _group_cols(g, 0)])
            c16 = _bf(act_ref[0, :, _group_cols(g, 1)])
            cb = _nt(c16, b16)
            for hs in _head_batches(g):
                x = [act_ref[0, :, _head_cols(h)] for h in hs]
                dt_j = [dt[:, h:h + 1] for h in hs]
                acs_j = [acs[:, h:h + 1] for h in hs]
                acs_last = [_colsum(a * last_row) for a in acs_j]
                xg = _each(lambda xv, d: xv * d, x, dt_j)
                mm = [cb * _decay_matrix(a, acs_t[h:h + 1, :], tri_mask) for a, h in zip(acs_j, hs)]
                decay_s = _each(lambda al, a: jnp.exp(al - a), acs_last, acs_j)
                y_diag = _each(lambda m_, v: _nn(_bf(m_), _bf(v)), mm, xg)
                st = _each(lambda v, d: _tn(_bf(v * d), b16), xg, decay_s)
                hp = [state[h] for h in hs]
                for h, v in zip(hs, hp):
                    hp_ref[0, h, 0] = v
                y_off = [_nt(c16, _bf(v)) for v in hp]
                for h, yd, yo, a, xv in zip(hs, y_diag, y_off, acs_j, x):
                    y_ref[0, :, _head_cols(h)] = yd + yo * jnp.exp(a) + dsk_ref[:, h:h + 1] * xv
                for h, v, al, sv in zip(hs, hp, acs_last, st):
                    state[h] = v * jnp.exp(al) + sv

    return pl.pallas_call(
        body, name=name, grid=(b, nc),
        in_specs=[act_spec, dt_in_spec, par_spec, par_spec, par_spec],
        out_specs=[y_spec, h_spec],
        out_shape=[jax.ShapeDtypeStruct((b, s, SSM_INNER), F32),
                   jax.ShapeDtypeStruct((b, SSM_HEADS, nc, SSM_P, D_STATE), F32)],
        scratch_shapes=[pltpu.VMEM((SSM_HEADS, SSM_P, D_STATE), F32)],
        compiler_params=_params(("arbitrary", "arbitrary")),
    )(act3, proj3, dtb, alog, dsk)


def ssd_bwd(act3, proj3, dtb, alog, dsk, hprev, dy3, *, name):
    b, s, _ = act3.shape
    nc = s // CHUNK
    act_spec, y_spec, dt_in_spec, dt_out_spec, par_spec, h_spec = _ssd_specs(nc, True)
    dpar_spec = pl.BlockSpec((8, LANE), lambda bi, c: (0, 0))

    def body(act_ref, dtr_ref, dtb_ref, alog_ref, dsk_ref, hp_ref, dy_ref, dact_ref, ddtr_ref, dpar_ref, dstate):
        bi, c = pl.program_id(0), pl.program_id(1)

        @pl.when(c == 0)
        def _():
            dstate[...] = jnp.zeros_like(dstate)

        @pl.when((bi == 0) & (c == 0))
        def _():
            dpar_ref[...] = jnp.zeros_like(dpar_ref)

        z, dt, aneg, acs = _ssd_common(dtr_ref, dtb_ref, alog_ref)
        acs_t = acs.T
        tri_mask = _tri(True)
        last_row = (lax.broadcasted_iota(jnp.int32, (CHUNK, 1), 0) == CHUNK - 1).astype(F32)
        lanes = lax.broadcasted_iota(jnp.int32, (1, LANE), 1)
        sublanes = lax.broadcasted_iota(jnp.int32, (LANE, 1), 0)
        ddt_mat = jnp.zeros((CHUNK, LANE), F32)
        dacs_mat = jnp.zeros((CHUNK, LANE), F32)
        dacs_rows = jnp.zeros((LANE, CHUNK), F32)
        ddsk_row = jnp.zeros((1, LANE), F32)
        for g in range(SSM_GROUPS):
            b16 = _bf(act_ref[0, :, _group_cols(g, 0)])
            c16 = _bf(act_ref[0, :, _group_cols(g, 1)])
            cb = _nt(c16, b16)
            dcb = jnp.zeros((CHUNK, CHUNK), F32)
            db_acc = jnp.zeros((CHUNK, D_STATE), F32)
            dc_acc = jnp.zeros((CHUNK, D_STATE), F32)
            for hs in _head_batches(g):
                x = [act_ref[0, :, _head_cols(h)] for h in hs]
                g_y = [dy_ref[0, :, _head_cols(h)] for h in hs]
                hp = [hp_ref[0, h, 0] for h in hs]
                g_hn = [dstate[h] for h in hs]
                dt_j = [dt[:, h:h + 1] for h in hs]
                acs_j = [acs[:, h:h + 1] for h in hs]
                acs_last = [_colsum(a * last_row) for a in acs_j]
                xg = _each(lambda xv, d: xv * d, x, dt_j)
                lm = [_decay_matrix(a, acs_t[h:h + 1, :], tri_mask) for a, h in zip(acs_j, hs)]
                mm = [cb * l for l in lm]
                decay_s = _each(lambda al, a: jnp.exp(al - a), acs_last, acs_j)
                ea = [jnp.exp(a) for a in acs_j]
                cd = [jnp.exp(al) for al in acs_last]
                g_y16, xg16, hp16, g_hn16 = [[_bf(v) for v in vs] for vs in (g_y, xg, hp, g_hn)]
                d_mm = _each(_nt, g_y16, xg16)
                d_xg = _each(lambda m_, gy: _tn(_bf(m_), gy), mm, g_y16)
                d_dm = _each(lambda a, m_: a * m_, d_mm, mm)
                d_acs = [_rowsum(v) for v in d_dm]
                t_off = [_nt(c16, v) for v in hp16]
                d_t16 = _each(lambda gy, e: _bf(gy * e), g_y, ea)
                d_acs = _each(lambda da, gy, t, e: da + _rowsum(gy * t) * e, d_acs, g_y, t_off, ea)
                d_hp = _each(lambda dtv, gh, cdv: _tn(dtv, c16) + gh * cdv, d_t16, g_hn, cd)
                d_w = [_nt(b16, v) for v in g_hn16]
                d_xg = _each(lambda dx, dw, ds: dx + dw * ds, d_xg, d_w, decay_s)
                d_ds = _each(lambda dw, v, ds: _rowsum(dw * v) * ds, d_w, xg, decay_s)
                d_last = _each(lambda gh, hv, cdv, dd: _colsum(_rowsum(gh * hv)) * cdv + _colsum(dd), g_hn, hp, cd, d_ds)
                d_acs = _each(lambda da, dd, dl: da - dd + dl * last_row, d_acs, d_ds, d_last)
                for h, gy, dx, d, xv in zip(hs, g_y, d_xg, dt_j, x):
                    dact_ref[0, :, _head_cols(h)] = dsk_ref[:, h:h + 1] * gy + dx * d
                for h, v in zip(hs, d_hp):
                    dstate[h] = v
                for k, h in enumerate(hs):
                    onehot = (lanes == h).astype(F32)
                    dcb = dcb + d_mm[k] * lm[k]
                    dc_acc = dc_acc + _nn(d_t16[k], hp16[k])
                    db_acc = db_acc + _nn(_bf(xg[k] * decay_s[k]), g_hn16[k])
                    ddsk_row = ddsk_row + _colsum(_rowsum(g_y[k] * x[k])) * onehot
                    ddt_mat = ddt_mat + _rowsum(d_xg[k] * x[k]) * onehot
                    dacs_mat = dacs_mat + d_acs[k] * onehot
                    dacs_rows = dacs_rows + (sublanes == h).astype(F32) * _colsum(d_dm[k])
            dcb16 = _bf(dcb)
            dact_ref[0, :, _group_cols(g, 1)] = dc_acc + _nn(dcb16, b16)
            dact_ref[0, :, _group_cols(g, 0)] = db_acc + _tn(dcb16, c16)
        d_a = _dot01_left(_tri(False).astype(BF16), dacs_mat - dacs_rows.T)
        ddt_mat = ddt_mat + d_a * aneg
        d_raw = ddt_mat * jax.nn.sigmoid(z)
        ddtr_ref[0] = d_raw
        dpar_ref[0:1, :] += _colsum(d_raw)
        dpar_ref[1:2, :] += _colsum(d_a * dt) * aneg
        dpar_ref[2:3, :] += ddsk_row

    return pl.pallas_call(
        body, name=name, grid=(b, nc),
        in_specs=[act_spec, dt_in_spec, par_spec, par_spec, par_spec, h_spec, y_spec],
        out_specs=[act_spec, dt_out_spec, dpar_spec],
        out_shape=[jax.ShapeDtypeStruct(act3.shape, F32), jax.ShapeDtypeStruct((b, s, LANE), F32),
                   jax.ShapeDtypeStruct((8, LANE), F32)],
        scratch_shapes=[pltpu.VMEM((SSM_HEADS, SSM_P, D_STATE), F32)],
        compiler_params=_params(("arbitrary", "arbitrary")),
    )(act3, proj3, dtb, alog, dsk, hprev, dy3)


def to_heads(x, b, s, h):
    return x.reshape(b, s, h, -1).transpose(0, 2, 1, 3)


def from_heads(x):
    b, h, s, c = x.shape
    return x.transpose(0, 2, 1, 3).reshape(b * s, h * c)


def dilate_q(q, d):
    b, _, s, c = q.shape
    x = q.reshape(b, N_KV_HEADS, GQA, s // d, d, c).transpose(0, 1, 4, 2, 3, 5)
    return x.reshape(b * N_KV_HEADS * d, GQA, s // d, c)


def undilate_q(x, b, d):
    _, _, l, c = x.shape
    y = x.reshape(b, N_KV_HEADS, d, GQA, l, c).transpose(0, 1, 3, 4, 2, 5)
    return y.reshape(b, N_Q_HEADS, l * d, c)


def dilate_kv(k, d):
    b, h, s, c = k.shape
    return k.reshape(b, h, s // d, d, c).transpose(0, 1, 3, 2, 4).reshape(b * h * d, s // d, c)


def undilate_kv(x, b, d):
    _, l, c = x.shape
    return x.reshape(b, N_KV_HEADS, d, l, c).transpose(0, 1, 3, 2, 4).reshape(b, N_KV_HEADS, l * d, c)


def rotary_tables(positions):
    inv_freq = ROPE_THETA ** (-jnp.arange(0, ROPE_DIM, 2, dtype=F32) / ROPE_DIM)
    ang = positions.astype(F32)[..., None] * inv_freq
    cos, sin = jnp.cos(ang), jnp.sin(ang)
    rest = HEAD_DIM - ROPE_DIM
    cosf = jnp.concatenate([cos, cos, jnp.ones(cos.shape[:2] + (rest,), F32)], axis=-1)
    sinf = jnp.concatenate([-sin, sin, jnp.zeros(sin.shape[:2] + (rest,), F32)], axis=-1)
    return cosf, sinf


def w_in_columns(w):
    pad = jnp.zeros((w.shape[0], IN_PAD - IN_PROJ), w.dtype)
    return jnp.concatenate([w[:, :Q_END], w[:, V_END:XBC_END], w[:, Q_END:V_END], w[:, XBC_END:], pad], axis=1)


def w_in_grad_columns(g):
    return jnp.concatenate([g[:, :Z_COL], g[:, K_COL:DT_COL], g[:, Z_COL:K_COL], g[:, DT_COL:DT_COL + SSM_HEADS]], axis=1)


def lane_pad(v):
    return jnp.pad(v.reshape(1, -1), ((0, 0), (0, LANE - v.shape[-1])))


def layer_fwd(h, wts, small, rope_tab, b, s, tag, side=None):
    w_in, w_out, w_gate, w_up, w_down = wts
    t = b * s
    sv = {"h": h}
    hn = rowwise_fwd(rms_fn, [h], [small["norm_mix"]], [BF16], name=f"rms_mix_{tag}")[0]
    proj = matmul(hn, w_in, name=f"in_proj_{tag}")
    sv["hn"], sv["proj"] = hn, proj
    proj3 = proj.reshape(b, s, IN_PAD)
    attn3, lse3, *side_out = attn_fwd(proj3, rope_tab, name=f"attn_{tag}", side=side)
    sv["attn3"], sv["lse3"] = attn3, lse3
    attn = attn3.reshape(t, ATTN_WIDTH)
    act3 = conv_silu_fwd(proj3, small["conv_w"], small["conv_b"], name=f"conv_{tag}")
    y3, hprev = ssd_fwd(act3, proj3, small["dt_bias"], small["a_log"], small["d_skip"], name=f"ssd_{tag}")
    y = y3.reshape(t, SSM_INNER)
    sv["act3"], sv["hprev"], sv["y"] = act3, hprev, y
    gn = rowwise_fwd(gated_norm_fn, [y, proj], [small["ssm_norm"]], [F32], name=f"gated_norm_{tag}", groups=SSM_GROUPS,
                     windows=[None, (Z_COL, SSM_INNER)])[0]
    sv["gn"] = gn
    h1 = matmul([attn, gn], w_out, name=f"out_proj_{tag}", residual=h)
    sv["h1"] = h1
    hn2 = rowwise_fwd(rms_fn, [h1], [small["norm_ffn"]], [BF16], name=f"rms_ffn_{tag}")[0]
    gate = matmul(hn2, w_gate, out_dtype=BF16, name=f"ffn_gate_{tag}")
    up = matmul(hn2, w_up, out_dtype=BF16, name=f"ffn_up_{tag}")
    act2 = rowwise_fwd(swiglu_fn, [gate, up], [], [BF16], name=f"swiglu_{tag}")[0]
    sv["hn2"], sv["gate"], sv["up"], sv["act2"] = hn2, gate, up, act2
    h2 = matmul(act2, w_down, name=f"ffn_down_{tag}", residual=h1)
    return h2, sv, (side_out[0] if side_out else None)


def layer_bwd(dh2, sv, wts, small, rope_tab, b, s, tag, side=None):
    w_in, w_out, w_gate, w_up, w_down = wts
    t = b * s
    gr = {}
    d_act2 = matmul(dh2, w_down, tb=True, out_dtype=BF16, name=f"ffn_down_dx_{tag}")
    gr["w_down"] = matmul(sv["act2"], dh2, ta=True, out_dtype=BF16, name=f"ffn_down_dw_{tag}")
    d_gate, d_up = rowwise_bwd(swiglu_fn, [sv["gate"], sv["up"]], [], [d_act2], [BF16, BF16], name=f"swiglu_bwd_{tag}")
    gr["w_gate"] = matmul(sv["hn2"], d_gate, ta=True, out_dtype=BF16, name=f"ffn_gate_dw_{tag}")
    gr["w_up"] = matmul(sv["hn2"], d_up, ta=True, out_dtype=BF16, name=f"ffn_up_dw_{tag}")
    d_hn2 = matmul(d_gate, w_gate, tb=True, name=f"ffn_gate_dx_{tag}")
    d_hn2 = matmul(d_up, w_up, tb=True, residual=d_hn2, name=f"ffn_up_dx_{tag}")
    dh1, gr["norm_ffn"] = rowwise_bwd(rms_fn, [sv["h1"]], [small["norm_ffn"]], [d_hn2], [F32],
                                      name=f"rms_ffn_bwd_{tag}", add_to_first=dh2)
    d_cat = matmul(dh1, w_out, tb=True, name=f"out_proj_dx_{tag}")
    gr["w_out"] = jnp.concatenate([
        matmul(sv["attn3"].reshape(t, ATTN_WIDTH), dh1, ta=True, out_dtype=BF16, name=f"out_proj_dw_attn_{tag}"),
        matmul(sv["gn"], dh1, ta=True, out_dtype=BF16, name=f"out_proj_dw_ssd_{tag}")], axis=0)
    d_y, d_z, gr["ssm_norm"] = rowwise_bwd(gated_norm_fn, [sv["y"], sv["proj"]], [small["ssm_norm"]], [d_cat], [F32, F32],
                                           name=f"gated_norm_bwd_{tag}", groups=SSM_GROUPS,
                                           windows=[None, (Z_COL, SSM_INNER)], ct_windows=[(ATTN_WIDTH, SSM_INNER)])
    proj3 = sv["proj"].reshape(b, s, IN_PAD)
    d_act3, d_dtr, d_par = ssd_bwd(sv["act3"], proj3, small["dt_bias"], small["a_log"], small["d_skip"], sv["hprev"],
                                   d_y.reshape(b, s, SSM_INNER), name=f"ssd_bwd_{tag}")
    gr["dt_bias"], gr["a_log"], gr["d_skip"] = d_par[0, :SSM_HEADS], d_par[1, :SSM_HEADS], d_par[2, :SSM_HEADS]
    d_xbc, gr["conv_w"], gr["conv_b"] = conv_silu_bwd(proj3, small["conv_w"], small["conv_b"], d_act3,
                                                      name=f"conv_bwd_{tag}")
    d_q3, d_k4, d_v4, *side_out = attn_bwd(proj3, rope_tab, sv["attn3"], sv["lse3"], d_cat.reshape(b, s, MIX_WIDTH),
                                           name=f"attn_bwd_{tag}", side=side)
    d_tail = jnp.concatenate([from_heads(d_k4), from_heads(d_v4), d_dtr.reshape(t, LANE)], axis=1)
    d_proj = [d_q3.reshape(t, ATTN_WIDTH), d_z, d_xbc.reshape(t, CONV_CH), d_tail]
    d_hn = matmul(d_proj, w_in, tb=True, name=f"in_proj_dx_{tag}")
    gr["w_in"] = w_in_grad_columns(jnp.concatenate(
        [matmul(sv["hn"], part, ta=True, out_dtype=BF16, name=f"in_proj_dw_{k}_{tag}") for k, part in enumerate(d_proj)],
        axis=1))
    dh, gr["norm_mix"] = rowwise_bwd(rms_fn, [sv["h"]], [small["norm_mix"]], [d_hn], [F32],
                                     name=f"rms_mix_bwd_{tag}", add_to_first=dh1)
    return dh, gr, (side_out[0] if side_out else None)


def local_step(x, positions, big, small_all, final_norm, loss_target, *, late_weights=None, early_grads=None):
    b, s, _ = x.shape
    t = b * s
    rope_tab = jnp.concatenate(rotary_tables(positions), axis=-1)
    h = x.reshape(t, D_MODEL)
    saved, big = [], list(big)
    for l in range(DEPTH):
        side = (late_weights[0], False) if late_weights is not None and l == 0 else None
        h, sv, got = layer_fwd(h, big[l], small_all[l], rope_tab, b, s, f"l{l}", side=side)
        if got is not None:
            big[DEPTH - 1] = late_weights[1](got)
        saved.append(sv)
    dh, d_final, loss = loss_and_grad(h, loss_target.reshape(t, D_MODEL), final_norm.reshape(1, D_MODEL))
    grads, received = [None] * DEPTH, None
    for l in reversed(range(DEPTH)):
        side = (early_grads(grads[DEPTH - 1]), True) if early_grads is not None and l == 0 else None
        dh, grads[l], got = layer_bwd(dh, saved[l], big[l], small_all[l], rope_tab, b, s, f"l{l}", side=side)
        received = got if got is not None else received
    return loss, dh.reshape(b, s, D_MODEL), grads, d_final, received


def _slab_rows(r):
    return r if r <= 512 else _pick(r, (512, 256, 128, 8))


def cast_bf16(x, *, name):
    def fn(v):
        return (v,)
    return rowwise_fwd(fn, [x], [], [BF16], name=name, tr=_slab_rows(x.shape[0]))[0]


def sum_slots(x, *, name):
    n, r, c = x.shape
    tr = _slab_rows(r)

    def body(x_ref, o_ref):
        acc = x_ref[0].astype(F32)
        for i in range(1, n):
            acc = acc + x_ref[i].astype(F32)
        o_ref[...] = acc

    return pl.pallas_call(
        body, name=name, grid=(r // tr,), in_specs=[pl.BlockSpec((n, tr, c), lambda i: (0, i, 0))],
        out_specs=pl.BlockSpec((tr, c), lambda i: (i, 0)), out_shape=jax.ShapeDtypeStruct((r, c), F32),
        compiler_params=_params(("parallel",)),
    )(x)


def adamw(g_parts, w, m, v, *, name, with_grad=True):
    r, c = w.shape
    tr = _slab_rows(r)
    n_g = len(g_parts)
    n_out = 4 if with_grad else 3
    bc1 = 1.0 / (1.0 - ADAM_B1 ** ADAM_STEP)
    bc2 = 1.0 / (1.0 - ADAM_B2 ** ADAM_STEP)

    def body(*refs):
        g = refs[0][...]
        for r_ in refs[1:n_g]:
            g = g + r_[...]
        w_ref, m_ref, v_ref = refs[n_g:n_g + 3]
        d_out, m_out, v_out = refs[-3:]
        m_new = ADAM_B1 * m_ref[...] + (1.0 - ADAM_B1) * g
        v_new = ADAM_B2 * v_ref[...] + (1.0 - ADAM_B2) * (g * g)
        if with_grad:
            refs[n_g + 3][...] = g
        m_out[...] = m_new
        v_out[...] = v_new
        d_out[...] = -ADAM_LR * ((m_new * bc1) / (jnp.sqrt(v_new * bc2) + ADAM_EPS) + ADAM_WD * w_ref[...])

    spec = pl.BlockSpec((tr, c), lambda i: (i, 0))
    return pl.pallas_call(
        body, name=name, grid=(r // tr,), in_specs=[spec] * (n_g + 3), out_specs=[spec] * n_out,
        out_shape=[jax.ShapeDtypeStruct((r, c), F32)] * n_out, compiler_params=_params(("parallel",)),
    )(*g_parts, w, m, v)


def _other_chips(x, y):
    return [(1 - x, y), (x, 1 - y), (1 - x, 1 - y)]


def allgather_chips(shards):
    n_arr = len(shards)

    def body(*refs):
        in_refs, out_refs = refs[:n_arr], refs[n_arr:2 * n_arr]
        send_sems, recv_sems, local_sems = refs[2 * n_arr:]
        x, y, c = lax.axis_index("x"), lax.axis_index("y"), lax.axis_index("c")
        chip = 2 * x + y
        started = []
        for a, (in_ref, out_ref) in enumerate(zip(in_refs, out_refs)):
            mine = pltpu.make_async_copy(in_ref, out_ref.at[chip], local_sems.at[a])
            mine.start()
            started.append(mine.wait)
            for k, (px, py) in enumerate(_other_chips(x, y)):
                cp = pltpu.make_async_remote_copy(src_ref=in_ref, dst_ref=out_ref.at[chip], send_sem=send_sems.at[3 * a + k],
                                                  recv_sem=recv_sems.at[3 * a + k], device_id=(px, py, c), device_id_type=MESH)
                cp.start()
                started.append(cp.wait_send)
        for a, (in_ref, out_ref) in enumerate(zip(in_refs, out_refs)):
            for k, (px, py) in enumerate(_other_chips(x, y)):
                pltpu.make_async_remote_copy(src_ref=in_ref, dst_ref=out_ref.at[2 * px + py], send_sem=send_sems.at[3 * a + k],
                                             recv_sem=recv_sems.at[3 * a + k], device_id=(px, py, c),
                                             device_id_type=MESH).wait_recv()
        for wait in started:
            wait()

    hbm = pl.BlockSpec(memory_space=pltpu.HBM)
    return pl.pallas_call(
        body, name="allgather_weights", in_specs=[hbm] * n_arr, out_specs=[hbm] * n_arr,
        out_shape=[jax.ShapeDtypeStruct((N_CHIPS,) + s.shape, s.dtype) for s in shards],
        scratch_shapes=[pltpu.SemaphoreType.DMA((3 * n_arr,)), pltpu.SemaphoreType.DMA((3 * n_arr,)),
                        pltpu.SemaphoreType.DMA((n_arr,))],
    )(*shards)


def exchange_grads(big, small):
    def body(big_ref, small_ref, big_out, small_out, send_sems, recv_sems, local_sems):
        x, y, c = lax.axis_index("x"), lax.axis_index("y"), lax.axis_index("c")
        chip = 2 * x + y
        dev = 4 * x + 2 * y + c
        own_big = pltpu.make_async_copy(big_ref.at[chip], big_out.at[chip], local_sems.at[0])
        own_small = pltpu.make_async_copy(small_ref, small_out.at[dev], local_sems.at[1])
        own_big.start()
        own_small.start()
        sends = []
        for k, (px, py) in enumerate(_other_chips(x, y)):
            cp = pltpu.make_async_remote_copy(src_ref=big_ref.at[2 * px + py], dst_ref=big_out.at[chip],
                                              send_sem=send_sems.at[k], recv_sem=recv_sems.at[k],
                                              device_id=(px, py, c), device_id_type=MESH)
            cp.start()
            sends.append(cp)
        peers = []
        for r in range(1, N_DEV):
            fx, fy, fc = (r >> 2) & 1, (r >> 1) & 1, r & 1
            px, py, pc = (x + fx) % 2, (y + fy) % 2, (c + fc) % 2
            peers.append((px, py, pc))
            cp = pltpu.make_async_remote_copy(src_ref=small_ref, dst_ref=small_out.at[dev], send_sem=send_sems.at[2 + r],
                                              recv_sem=recv_sems.at[2 + r], device_id=(px, py, pc), device_id_type=MESH)
            cp.start()
            sends.append(cp)
        for k, (px, py) in enumerate(_other_chips(x, y)):
            pltpu.make_async_remote_copy(src_ref=big_ref.at[chip], dst_ref=big_out.at[2 * px + py],
                                         send_sem=send_sems.at[k], recv_sem=recv_sems.at[k],
                                         device_id=(px, py, c), device_id_type=MESH).wait_recv()
        for r, (px, py, pc) in zip(range(1, N_DEV), peers):
            pltpu.make_async_remote_copy(src_ref=small_ref, dst_ref=small_out.at[4 * px + 2 * py + pc],
                                         send_sem=send_sems.at[2 + r], recv_sem=recv_sems.at[2 + r],
                                         device_id=(px, py, pc), device_id_type=MESH).wait_recv()
        for cp in sends:
            cp.wait_send()
        own_big.wait()
        own_small.wait()

    hbm = pl.BlockSpec(memory_space=pltpu.HBM)
    n_sem = 3 + N_DEV - 1
    return pl.pallas_call(
        body, name="exchange_grads", in_specs=[hbm, hbm], out_specs=[hbm, hbm],
        out_shape=[jax.ShapeDtypeStruct(big.shape, big.dtype), jax.ShapeDtypeStruct((N_DEV,) + small.shape, small.dtype)],
        scratch_shapes=[pltpu.SemaphoreType.DMA((n_sem,)), pltpu.SemaphoreType.DMA((n_sem,)), pltpu.SemaphoreType.DMA((2,))],
    )(big, small)


SWAP_CHUNKS = 28


def swap_cores(mine):
    rows = mine.shape[0] // SWAP_CHUNKS
    assert rows * SWAP_CHUNKS == mine.shape[0] and rows % 8 == 0

    def body(in_ref, out_ref, send_sems, recv_sems):
        x, y, c = lax.axis_index("x"), lax.axis_index("y"), lax.axis_index("c")

        def chunk(k):
            part = pl.ds(k * rows, rows)
            return pltpu.make_async_remote_copy(src_ref=in_ref.at[part], dst_ref=out_ref.at[part],
                                                send_sem=send_sems.at[k], recv_sem=recv_sems.at[k],
                                                device_id=(x, y, 1 - c), device_id_type=MESH)

        for k in range(SWAP_CHUNKS):
            chunk(k).start()
        for k in range(SWAP_CHUNKS):
            chunk(k).wait_recv()
        for k in range(SWAP_CHUNKS):
            chunk(k).wait_send()

    hbm = pl.BlockSpec(memory_space=pltpu.HBM)
    return pl.pallas_call(
        body, name="swap_cores", in_specs=[hbm], out_specs=hbm,
        out_shape=jax.ShapeDtypeStruct(mine.shape, mine.dtype),
        scratch_shapes=[pltpu.SemaphoreType.DMA((SWAP_CHUNKS,)), pltpu.SemaphoreType.DMA((SWAP_CHUNKS,))],
    )(mine)


BIG_NAMES = ("w_in", "w_out", "w_gate", "w_up", "w_down")
BIG_SHARD_AXIS = {"w_in": 1, "w_out": 0, "w_gate": 1, "w_up": 1, "w_down": 0}
PACK_COLS = 1024
SMALL_NAMES = ("norm_mix", "conv_w", "conv_b", "dt_bias", "a_log", "d_skip", "ssm_norm", "norm_ffn")


PACK_ROW_TILE = 256


def pack_big(shards):
    flat = jnp.concatenate([shards[n].reshape(-1) for n in BIG_NAMES])
    unit = PACK_ROW_TILE * PACK_COLS
    total = -(-flat.size // unit) * unit
    return jnp.pad(flat, (0, total - flat.size)).reshape(-1, PACK_COLS)


def unpack_big(packed, like):
    out, off = {}, 0
    flat = packed.reshape(-1)
    for n in BIG_NAMES:
        size = like[n].size
        out[n] = flat[off:off + size].reshape(like[n].shape)
        off += size
    return out


def pack_small(parts):
    flat = jnp.concatenate([p.reshape(-1).astype(F32) for p in parts])
    rows = -(-flat.size // LANE)
    rows = -(-rows // 8) * 8
    return jnp.pad(flat, (0, rows * LANE - flat.size)).reshape(rows, LANE)


def unpack_small(packed, like):
    out, off = [], 0
    flat = packed.reshape(-1)
    for a in like:
        out.append(flat[off:off + a.size].reshape(a.shape))
        off += a.size
    return out


def kernel(x, positions, norm_mix, w_in, conv_w, conv_b, dt_bias, a_log, d_skip, ssm_norm, w_out, norm_ffn, w_gate, w_up, w_down, final_norm, loss_target, m_norm_mix, m_w_in, m_conv_w, m_conv_b, m_dt_bias, m_a_log, m_d_skip, m_ssm_norm, m_w_out, m_norm_ffn, m_w_gate, m_w_up, m_w_down, m_final_norm, v_norm_mix, v_w_in, v_conv_w, v_conv_b, v_dt_bias, v_a_log, v_d_skip, v_ssm_norm, v_w_out, v_norm_ffn, v_w_gate, v_w_up, v_w_down, v_final_norm):
    chip = 2 * lax.axis_index("x") + lax.axis_index("y")
    w_sh = {"w_in": w_in, "w_out": w_out, "w_gate": w_gate, "w_up": w_up, "w_down": w_down}
    m_sh = {"w_in": m_w_in, "w_out": m_w_out, "w_gate": m_w_gate, "w_up": m_w_up, "w_down": m_w_down}
    v_sh = {"w_in": v_w_in, "w_out": v_w_out, "w_gate": v_w_gate, "w_up": v_w_up, "w_down": v_w_down}

    assert DEPTH == 2
    layer_of = lambda d, l: {n: d[n][l] for n in BIG_NAMES}
    pack_layers = lambda d: jnp.concatenate([pack_big(layer_of(d, l)) for l in range(DEPTH)])
    layer_rows = pack_big(layer_of(w_sh, 0)).shape[0]

    def unpack_layers(packed):
        per_layer = [unpack_big(packed[l * layer_rows:(l + 1) * layer_rows], layer_of(w_sh, l)) for l in range(DEPTH)]
        return {n: jnp.stack([p[n] for p in per_layer]) for n in BIG_NAMES}

    def full_weights(gathered, l):
        pieces = [unpack_big(gathered[j], layer_of(w_sh, l)) for j in range(N_CHIPS)]
        full = {n: jnp.concatenate([p[n] for p in pieces], axis=BIG_SHARD_AXIS[n]) for n in BIG_NAMES}
        return (w_in_columns(full["w_in"]), full["w_out"], full["w_gate"], full["w_up"], full["w_down"])

    w_packed16 = pack_layers({n: cast_bf16(w_sh[n].reshape(-1, w_sh[n].shape[-1]), name=f"cast_{n}").reshape(w_sh[n].shape)
                              for n in BIG_NAMES})
    conv_cols = CONV_CH // N_CHIPS
    gathered0, conv_g = allgather_chips([w_packed16[:layer_rows], conv_w.reshape(-1, LANE)])
    big = [full_weights(gathered0, 0), None]
    late_weights = (w_packed16[layer_rows:], lambda gathered: full_weights(gathered, DEPTH - 1))
    conv_w_full = jnp.concatenate([conv_g[j].reshape(DEPTH, CONV_WIDTH, conv_cols) for j in range(N_CHIPS)], axis=2)

    small_all = []
    for l in range(DEPTH):
        small_all.append({
            "norm_mix": norm_mix[l].reshape(1, -1), "conv_w": conv_w_full[l], "conv_b": conv_b[l].reshape(1, -1),
            "dt_bias": lane_pad(dt_bias[l]), "a_log": lane_pad(a_log[l]), "d_skip": lane_pad(d_skip[l]),
            "ssm_norm": ssm_norm[l].reshape(1, -1), "norm_ffn": norm_ffn[l].reshape(1, -1)})

    def shard_of(name, g, j):
        n = g.shape[BIG_SHARD_AXIS[name]] // N_CHIPS
        return lax.slice_in_dim(g, j * n, (j + 1) * n, axis=BIG_SHARD_AXIS[name])

    def per_chip(layer_grads):
        return jnp.stack([pack_big({n: shard_of(n, layer_grads[n], j) for n in BIG_NAMES}) for j in range(N_CHIPS)])

    loss_part, grad_x, grads, d_final, recv_last = local_step(x, positions, big, small_all, final_norm, loss_target,
                                                              late_weights=late_weights, early_grads=per_chip)

    small_parts = [jnp.stack([grads[l][n].reshape(-1) for l in range(DEPTH)]) for n in SMALL_NAMES]
    small_parts += [d_final.reshape(-1), loss_part.reshape(-1)]
    recv_first, recv_small = exchange_grads(per_chip(grads[0]), pack_small(small_parts))
    plane_sum = jnp.concatenate([sum_slots(recv_first, name="sum_chip_partials_l0"),
                                 sum_slots(recv_last, name="sum_chip_partials_l1")])
    other_plane = swap_cores(plane_sum)

    g_packed = rowwise_fwd(lambda p, q: (p + q,), [plane_sum, other_plane], [], [F32], name="sum_planes")[0]
    g_big = unpack_layers(g_packed)
    d_big, m_big, v_big = {}, {}, {}
    for n in BIG_NAMES:
        flat = lambda a: a.reshape(-1, a.shape[-1])
        res = adamw([flat(g_big[n])], flat(w_sh[n]), flat(m_sh[n]), flat(v_sh[n]), name=f"adamw_{n}", with_grad=False)
        d_big[n], m_big[n], v_big[n] = (a.reshape(w_sh[n].shape) for a in res)

    small_sum = sum_slots(recv_small, name="sum_small")
    like = [norm_mix, conv_w_full, conv_b, dt_bias, a_log, d_skip, ssm_norm, norm_ffn, final_norm, loss_part.reshape(-1)]
    g_small = unpack_small(small_sum, like)
    loss = g_small[-1][0]
    g_small = dict(zip(SMALL_NAMES + ("final_norm",), g_small[:-1]))
    g_small["conv_w"] = lax.dynamic_slice_in_dim(g_small["conv_w"], chip * conv_cols, conv_cols, axis=2)
    w_small = {"norm_mix": norm_mix, "conv_w": conv_w, "conv_b": conv_b, "dt_bias": dt_bias, "a_log": a_log, "d_skip": d_skip,
               "ssm_norm": ssm_norm, "norm_ffn": norm_ffn, "final_norm": final_norm}
    m_small = {"norm_mix": m_norm_mix, "conv_w": m_conv_w, "conv_b": m_conv_b, "dt_bias": m_dt_bias, "a_log": m_a_log,
               "d_skip": m_d_skip, "ssm_norm": m_ssm_norm, "norm_ffn": m_norm_ffn, "final_norm": m_final_norm}
    v_small = {"norm_mix": v_norm_mix, "conv_w": v_conv_w, "conv_b": v_conv_b, "dt_bias": v_dt_bias, "a_log": v_a_log,
               "d_skip": v_d_skip, "ssm_norm": v_ssm_norm, "norm_ffn": v_norm_ffn, "final_norm": v_final_norm}
    names = SMALL_NAMES + ("final_norm",)
    order = [w_small[n] for n in names]
    res = adamw([pack_small([g_small[n] for n in names])], pack_small(order), pack_small([m_small[n] for n in names]),
                pack_small([v_small[n] for n in names]), name="adamw_small")
    g_s, d_s, m_s, v_s = (dict(zip(names, unpack_small(a, order))) for a in res)

    all_names = ("norm_mix", "w_in", "conv_w", "conv_b", "dt_bias", "a_log", "d_skip", "ssm_norm", "w_out", "norm_ffn",
                 "w_gate", "w_up", "w_down", "final_norm")
    outs = [loss, grad_x]
    for src_big, src_small in ((g_big, g_s), (d_big, d_s), (m_big, m_s), (v_big, v_s)):
        outs += [src_big[n] if n in BIG_NAMES else src_small[n] for n in all_names]
    return tuple(outs)
```

```python
import functools

import jax
import jax.numpy as jnp
from jax import lax
from jax.experimental import pallas as pl
from jax.experimental.pallas import tpu as pltpu

F32 = jnp.float32
BF16 = jnp.bfloat16
MESH = pl.DeviceIdType.MESH

D_MODEL = 1024
DEPTH = 2
HEAD_DIM = 64
N_Q_HEADS = 8
N_KV_HEADS = 2
GQA = N_Q_HEADS // N_KV_HEADS
ATTN_WIDTH = N_Q_HEADS * HEAD_DIM
ROPE_DIM = HEAD_DIM // 4
ROPE_HALF = ROPE_DIM // 2
ROPE_THETA = 500000.0
DILATIONS = (1, 4, 16)
ATTN_BLOCK = 128
SSM_P = 64
SSM_HEADS = 16
SSM_INNER = SSM_HEADS * SSM_P
SSM_GROUPS = 2
HEADS_PER_GROUP = SSM_HEADS // SSM_GROUPS
D_STATE = 128
CONV_WIDTH = 4
CHUNK = 128
CONV_CH = SSM_INNER + 2 * SSM_GROUPS * D_STATE
MIX_WIDTH = ATTN_WIDTH + SSM_INNER
Q_END = ATTN_WIDTH
K_END = Q_END + N_KV_HEADS * HEAD_DIM
V_END = K_END + N_KV_HEADS * HEAD_DIM
Z_END = V_END + SSM_INNER
XBC_END = Z_END + CONV_CH
IN_PROJ = XBC_END + SSM_HEADS
LANE = 128
IN_PAD = XBC_END + LANE
Q_COL, Z_COL, XBC_COL, K_COL, V_COL, DT_COL = 0, 512, 1536, 3072, 3200, 3328
FFN_HIDDEN = 2816
EPS = 1e-5
ADAM_LR, ADAM_B1, ADAM_B2, ADAM_EPS, ADAM_WD, ADAM_STEP = 0.001, 0.9, 0.999, 1e-8, 0.01, 10
N_CHIPS = 4
N_DEV = 8
VMEM_LIMIT = 48 * 1024 * 1024
NEG_BIG = -1e30


def _params(sem=None):
    return pltpu.CompilerParams(dimension_semantics=sem, vmem_limit_bytes=VMEM_LIMIT)


def _pick(n, prefs):
    for p in prefs:
        if n % p == 0:
            return p
    return n


def matmul(a, b, *, name, ta=False, tb=False, out_dtype=F32, residual=None):
    if ta:
        assert not tb and residual is None
        return _matmul_over_rows(a, b, name=name, out_dtype=out_dtype)
    return _matmul_full_k(a, b, name=name, tb=tb, out_dtype=out_dtype, residual=residual)


def _matmul_full_k(a, b, *, name, tb, out_dtype, residual):
    a_parts = list(a) if isinstance(a, (list, tuple)) else [a]
    n_a = len(a_parts)
    m = a_parts[0].shape[0]
    kdim = sum(p.shape[1] for p in a_parts)
    wide = kdim > 1536 or any(p.dtype == F32 for p in a_parts)
    n = b.shape[0] if tb else b.shape[1]
    tm = _pick(m, (512, 256)) if wide else _pick(m, (1024, 512, 256))
    tn = _pick(n, (1152, 1408, 1536, 1024, 768, 512, 384, 256, 128))
    b_spec = pl.BlockSpec((tn, kdim), lambda i, j: (j, 0)) if tb else pl.BlockSpec((kdim, tn), lambda i, j: (0, j))
    o_spec = pl.BlockSpec((tm, tn), lambda i, j: (i, j))
    dims = (((1,), (1 if tb else 0,)), ((), ()))
    has_res = residual is not None

    def body(*refs):
        b_ref, o_ref = refs[n_a], refs[-1]
        pieces = [r[...].astype(BF16) for r in refs[:n_a]]
        av = pieces[0] if n_a == 1 else jnp.concatenate(pieces, axis=1)
        r = lax.dot_general(av, b_ref[...].astype(BF16), dims, preferred_element_type=F32)
        if has_res:
            r = r + refs[n_a + 1][...]
        o_ref[...] = r.astype(out_dtype)

    in_specs = ([pl.BlockSpec((tm, p.shape[1]), lambda i, j: (i, 0)) for p in a_parts] + [b_spec]
                + ([o_spec] if has_res else []))
    args = tuple(a_parts) + (b,) + ((residual,) if has_res else ())
    return pl.pallas_call(
        body, name=name, grid=(m // tm, n // tn), in_specs=in_specs, out_specs=o_spec,
        out_shape=jax.ShapeDtypeStruct((m, n), out_dtype),
        compiler_params=_params(("parallel", "parallel")),
    )(*args)


def _matmul_over_rows(a, b, *, name, out_dtype):
    t, m = a.shape
    n = b.shape[1]
    tm = _pick(m, (1024, 1408, 768, 512, 256, 128))
    tn = _pick(n, (1152, 1408, 1024, 768, 512, 256, 128))
    tk = _pick(t, (1024, 512, 256, 128))
    nk = t // tk

    def body(a_ref, b_ref, o_ref, acc):
        k = pl.program_id(2)
        part = lax.dot_general(a_ref[...].astype(BF16), b_ref[...].astype(BF16), (((0,), (0,)), ((), ())),
                               preferred_element_type=F32)

        @pl.when(k == 0)
        def _():
            acc[...] = part

        @pl.when(k > 0)
        def _():
            acc[...] += part

        @pl.when(k == nk - 1)
        def _():
            o_ref[...] = acc[...].astype(out_dtype)

    return pl.pallas_call(
        body, name=name, grid=(m // tm, n // tn, nk),
        in_specs=[pl.BlockSpec((tk, tm), lambda i, j, k: (k, i)), pl.BlockSpec((tk, tn), lambda i, j, k: (k, j))],
        out_specs=pl.BlockSpec((tm, tn), lambda i, j, k: (i, j)),
        out_shape=jax.ShapeDtypeStruct((m, n), out_dtype),
        scratch_shapes=[pltpu.VMEM((tm, tn), F32)],
        compiler_params=_params(("parallel", "parallel", "arbitrary")),
    )(a, b)


ROW_BLOCK_BYTES = 16 * 1024 * 1024


def _row_tile(t, tr, widths, n_copies):
    lanes = sum(-(-wd // LANE) * LANE for wd in widths) * n_copies
    tr = min(tr, t)
    while tr > 8 and tr * lanes * 4 > ROW_BLOCK_BYTES:
        tr //= 2
    return tr


def _row_widths(rows, groups, windows):
    windows = windows or [None] * len(rows)
    widths = [(w[1] if w else a.shape[1]) // groups for a, w in zip(rows, windows)]
    assert all(w is None or w[0] % wd == 0 for w, wd in zip(windows, widths))
    return widths, [(w[0] // wd if w else 0) for w, wd in zip(windows, widths)]


def _row_specs(tr, widths, offs):
    return [pl.BlockSpec((tr, wd), functools.partial(lambda g, i, off: (i, g + off), off=off)) for wd, off in zip(widths, offs)]


def rowwise_fwd(fn, rows, params, out_dtypes, *, name, tr=512, groups=1, windows=None):
    t = rows[0].shape[0]
    widths, offs = _row_widths(rows, groups, windows)
    tr = _row_tile(t, tr, widths, 2)
    row_specs = _row_specs(tr, widths, offs)
    par_spec = lambda p: pl.BlockSpec((1, p.shape[1] // groups), lambda g, i: (0, g))
    n_in = len(rows) + len(params)
    out_cols = [o.shape[1] for o in jax.eval_shape(
        fn, *[jax.ShapeDtypeStruct((tr, wd), F32) for wd in widths],
        *[jax.ShapeDtypeStruct((1, p.shape[1] // groups), F32) for p in params])]

    def body(*refs):
        vals = [r[...].astype(F32) for r in refs[:n_in]]
        outs = fn(*vals)
        for o_ref, o in zip(refs[n_in:], outs):
            o_ref[...] = o.astype(o_ref.dtype)

    return pl.pallas_call(
        body, name=name, grid=(groups, t // tr),
        in_specs=row_specs + [par_spec(p) for p in params],
        out_specs=[pl.BlockSpec((tr, c), lambda g, i: (i, g)) for c in out_cols],
        out_shape=[jax.ShapeDtypeStruct((t, c * groups), d) for c, d in zip(out_cols, out_dtypes)],
        compiler_params=_params(("arbitrary", "arbitrary")),
    )(*rows, *params)


def rowwise_bwd(fn, rows, params, cts, drow_dtypes, *, name, tr=512, groups=1, add_to_first=None, windows=None,
                ct_windows=None):
    t = rows[0].shape[0]
    widths, offs = _row_widths(rows, groups, windows)
    ct_widths, ct_offs = _row_widths(cts, groups, ct_windows)
    tr = _row_tile(t, tr, widths + ct_widths, 2)
    row_spec = lambda a: pl.BlockSpec((tr, a.shape[1] // groups), lambda g, i: (i, g))
    row_specs = _row_specs(tr, widths, offs)
    par_spec = lambda p: pl.BlockSpec((1, p.shape[1] // groups), lambda g, i: (0, g))
    n_rows, n_par, n_ct = len(rows), len(params), len(cts)
    has_add = add_to_first is not None
    n_in = n_rows + n_par + n_ct + (1 if has_add else 0)

    def body(*refs):
        i = pl.program_id(1)
        vals = [r[...].astype(F32) for r in refs[:n_rows + n_par]]
        ct_vals = tuple(r[...].astype(F32) for r in refs[n_rows + n_par:n_rows + n_par + n_ct])
        _, vjp = jax.vjp(fn, *vals)
        grads = vjp(ct_vals)
        out_refs = refs[n_in:]
        for idx in range(n_rows):
            g = grads[idx]
            if idx == 0 and has_add:
                g = g + refs[n_in - 1][...]
            out_refs[idx][...] = g.astype(out_refs[idx].dtype)
        for idx in range(n_par):
            p_ref = out_refs[n_rows + idx]

            @pl.when(i == 0)
            def _():
                p_ref[...] = jnp.zeros_like(p_ref)

            p_ref[...] += grads[n_rows + idx]

    ins = list(rows) + list(params) + list(cts) + ([add_to_first] if has_add else [])
    in_specs = (row_specs + [par_spec(p) for p in params] + _row_specs(tr, ct_widths, ct_offs)
                + ([row_spec(add_to_first)] if has_add else []))
    return pl.pallas_call(
        body, name=name, grid=(groups, t // tr), in_specs=in_specs,
        out_specs=[pl.BlockSpec((tr, wd), lambda g, i: (i, g)) for wd in widths] + [par_spec(p) for p in params],
        out_shape=[jax.ShapeDtypeStruct((t, wd * groups), d) for wd, d in zip(widths, drow_dtypes)]
        + [jax.ShapeDtypeStruct(p.shape, F32) for p in params],
        compiler_params=_params(("arbitrary", "arbitrary")),
    )(*ins)


def rms_fn(x, w):
    return (x * lax.rsqrt(jnp.mean(x * x, axis=-1, keepdims=True) + EPS) * w,)


def swiglu_fn(g, u):
    return (g * jax.nn.sigmoid(g) * u,)


def gated_norm_fn(y, z, w):
    v = y * (z * jax.nn.sigmoid(z))
    return (v * lax.rsqrt(jnp.mean(v * v, axis=-1, keepdims=True) + EPS) * w,)


def combine_fn(o1, o2, o3, l1, l2, l3):
    m = jnp.maximum(jnp.maximum(l1, l2), l3)
    e1, e2, e3 = jnp.exp(l1 - m), jnp.exp(l2 - m), jnp.exp(l3 - m)
    inv = 1.0 / (e1 + e2 + e3)
    return ((e1 * inv) * o1 + (e2 * inv) * o2 + (e3 * inv) * o3,)


def loss_and_grad(h, target, w, *, tr=512):
    t, d = h.shape

    def loss_fn(hv, wv, tv):
        err = rms_fn(hv, wv)[0] - tv
        per_row = jnp.mean(err * err, axis=-1, keepdims=True)
        return 0.5 * jnp.sum(per_row, axis=0, keepdims=True)

    def body(h_ref, t_ref, w_ref, dh_ref, dw_ref, loss_ref):
        i = pl.program_id(0)

        @pl.when(i == 0)
        def _():
            dw_ref[...] = jnp.zeros_like(dw_ref)
            loss_ref[...] = jnp.zeros_like(loss_ref)

        tv = t_ref[...]
        val, vjp = jax.vjp(lambda hv, wv: loss_fn(hv, wv, tv), h_ref[...], w_ref[...])
        dh, dw = vjp(jnp.ones((1, 1), F32))
        dh_ref[...] = dh
        dw_ref[...] += dw
        loss_ref[...] += jnp.broadcast_to(val, loss_ref.shape)

    row = pl.BlockSpec((tr, d), lambda i: (i, 0))
    par = pl.BlockSpec((1, d), lambda i: (0, 0))
    return pl.pallas_call(
        body, name="loss_and_grad", grid=(t // tr,), in_specs=[row, row, par],
        out_specs=[row, par, pl.BlockSpec((1, LANE), lambda i: (0, 0))],
        out_shape=[jax.ShapeDtypeStruct((t, d), F32), jax.ShapeDtypeStruct((1, d), F32),
                   jax.ShapeDtypeStruct((1, LANE), F32)],
        compiler_params=_params(("arbitrary",)),
    )(h, target, w)


def _split3(x):
    hi = x.astype(BF16)
    r1 = x - hi.astype(F32)
    mid = r1.astype(BF16)
    lo = (r1 - mid.astype(F32)).astype(BF16)
    return hi, mid, lo


def _dot01_left(m01, x):
    return sum(jnp.dot(m01, p, preferred_element_type=F32) for p in _split3(x))


def _dot01_right(x, m01):
    return sum(jnp.dot(p, m01, preferred_element_type=F32) for p in _split3(x))


def rotary(xs_list, cosf, sinf, scale, *, adjoint, name, ts=512):
    b, h, s, c = xs_list[0].shape
    n_x = len(xs_list)

    def body(*refs):
        x = refs[0][0, 0]
        for r in refs[1:n_x]:
            x = x + r[0, 0]
        cos_v, sin_v = refs[n_x][0], refs[n_x + 1][0]
        o_ref = refs[n_x + 2]
        ci = lax.broadcasted_iota(jnp.int32, (c, c), 0)
        cj = lax.broadcasted_iota(jnp.int32, (c, c), 1)
        swap = ((cj == ci + ROPE_HALF) & (ci < ROPE_HALF)) | ((cj == ci - ROPE_HALF) & (ci >= ROPE_HALF) & (ci < ROPE_DIM))
        swap = swap.astype(BF16)
        if adjoint:
            out = x * cos_v + _dot01_right(x * sin_v, swap)
        else:
            out = x * cos_v + _dot01_right(x, swap) * sin_v
        o_ref[0, 0] = out * scale

    x_spec = pl.BlockSpec((1, 1, ts, c), lambda bi, hi, si: (bi, hi, si, 0))
    t_spec = pl.BlockSpec((1, ts, c), lambda bi, hi, si: (bi, si, 0))
    return pl.pallas_call(
        body, name=name, grid=(b, h, s // ts), in_specs=[x_spec] * n_x + [t_spec, t_spec], out_specs=x_spec,
        out_shape=jax.ShapeDtypeStruct((b, h, s, c), F32),
        compiler_params=_params(("parallel", "parallel", "parallel")),
    )(*xs_list, cosf, sinf)


def add3(a, b, c, *, name, tr=1024):
    def fn(x, y, z):
        return (x + y + z,)
    return rowwise_fwd(fn, [a, b, c], [], [F32], name=name, tr=tr)[0]


def _attn_mask(n):
    rows = GQA * ATTN_BLOCK
    qi = lax.broadcasted_iota(jnp.int32, (rows, 2 * ATTN_BLOCK), 0) % ATTN_BLOCK
    ki = lax.broadcasted_iota(jnp.int32, (rows, 2 * ATTN_BLOCK), 1)
    delta = qi + ATTN_BLOCK - ki
    return (delta >= 0) & (delta <= ATTN_BLOCK) & ((n - 1) * ATTN_BLOCK + ki >= 0)


def _attn_specs(l):
    q_spec = pl.BlockSpec((1, GQA, ATTN_BLOCK, HEAD_DIM), lambda p, n: (p, 0, n, 0))
    l_spec = pl.BlockSpec((1, GQA, ATTN_BLOCK, 1), lambda p, n: (p, 0, n, 0))
    kprev = pl.BlockSpec((1, ATTN_BLOCK, HEAD_DIM), lambda p, n: (p, jnp.maximum(n - 1, 0), 0))
    kcur = pl.BlockSpec((1, ATTN_BLOCK, HEAD_DIM), lambda p, n: (p, n, 0))
    kfull = pl.BlockSpec((1, l, HEAD_DIM), lambda p, n: (p, 0, 0))
    return q_spec, l_spec, kprev, kcur, kfull


def attn_branch_fwd(q, k, v, *, name):
    p_cnt, _, l, _ = q.shape
    rows = GQA * ATTN_BLOCK
    q_spec, l_spec, kprev, kcur, _ = _attn_specs(l)

    def body(q_ref, kp_ref, kc_ref, vp_ref, vc_ref, o_ref, lse_ref):
        n = pl.program_id(1)
        qv = q_ref[0].reshape(rows, HEAD_DIM).astype(BF16)
        kk = jnp.concatenate([kp_ref[0], kc_ref[0]], axis=0).astype(BF16)
        vv = jnp.concatenate([vp_ref[0], vc_ref[0]], axis=0).astype(BF16)
        s = lax.dot_general(qv, kk, (((1,), (1,)), ((), ())), preferred_element_type=F32)
        s = jnp.where(_attn_mask(n), s, NEG_BIG)
        m = jnp.max(s, axis=-1, keepdims=True)
        pr = jnp.exp(s - m)
        den = jnp.sum(pr, axis=-1, keepdims=True)
        o = jnp.dot(pr.astype(BF16), vv, preferred_element_type=F32) / den
        o_ref[0] = o.reshape(GQA, ATTN_BLOCK, HEAD_DIM)
        lse_ref[0] = (m + jnp.log(den)).reshape(GQA, ATTN_BLOCK, 1)

    return pl.pallas_call(
        body, name=name, grid=(p_cnt, l // ATTN_BLOCK), in_specs=[q_spec, kprev, kcur, kprev, kcur],
        out_specs=[q_spec, l_spec],
        out_shape=[jax.ShapeDtypeStruct(q.shape, F32), jax.ShapeDtypeStruct(q.shape[:3] + (1,), F32)],
        compiler_params=_params(("parallel", "arbitrary")),
    )(q, k, k, v, v)


def attn_branch_bwd(q, k, v, o, lse, do, dlse, *, name):
    p_cnt, _, l, _ = q.shape
    rows = GQA * ATTN_BLOCK
    q_spec, l_spec, kprev, kcur, kfull = _attn_specs(l)

    def body(q_ref, kp_ref, kc_ref, vp_ref, vc_ref, o_ref, lse_ref, do_ref, dlse_ref, dq_ref, dk_ref, dv_ref):
        n = pl.program_id(1)

        @pl.when(n == 0)
        def _():
            dk_ref[...] = jnp.zeros_like(dk_ref)
            dv_ref[...] = jnp.zeros_like(dv_ref)

        qv = q_ref[0].reshape(rows, HEAD_DIM).astype(BF16)
        kk = jnp.concatenate([kp_ref[0], kc_ref[0]], axis=0).astype(BF16)
        vv = jnp.concatenate([vp_ref[0], vc_ref[0]], axis=0).astype(BF16)
        ov = o_ref[0].reshape(rows, HEAD_DIM)
        dov = do_ref[0].reshape(rows, HEAD_DIM)
        lsev = lse_ref[0].reshape(rows, 1)
        dlsev = dlse_ref[0].reshape(rows, 1)
        s = lax.dot_general(qv, kk, (((1,), (1,)), ((), ())), preferred_element_type=F32)
        pr = jnp.where(_attn_mask(n), jnp.exp(s - lsev), 0.0)
        do16 = dov.astype(BF16)
        dv = lax.dot_general(pr.astype(BF16), do16, (((0,), (0,)), ((), ())), preferred_element_type=F32)
        dp = lax.dot_general(do16, vv, (((1,), (1,)), ((), ())), preferred_element_type=F32)
        delta = jnp.sum(dov * ov, axis=-1, keepdims=True)
        ds = (pr * (dp - delta + dlsev)).astype(BF16)
        dq = jnp.dot(ds, kk, preferred_element_type=F32)
        dk = lax.dot_general(ds, qv, (((0,), (0,)), ((), ())), preferred_element_type=F32)
        dq_ref[0] = dq.reshape(GQA, ATTN_BLOCK, HEAD_DIM)
        cur = pl.ds(pl.multiple_of(n * ATTN_BLOCK, ATTN_BLOCK), ATTN_BLOCK)
        dk_ref[0, cur, :] += dk[ATTN_BLOCK:]
        dv_ref[0, cur, :] += dv[ATTN_BLOCK:]

        @pl.when(n > 0)
        def _():
            prev = pl.ds(pl.multiple_of((n - 1) * ATTN_BLOCK, ATTN_BLOCK), ATTN_BLOCK)
            dk_ref[0, prev, :] += dk[:ATTN_BLOCK]
            dv_ref[0, prev, :] += dv[:ATTN_BLOCK]

    return pl.pallas_call(
        body, name=name, grid=(p_cnt, l // ATTN_BLOCK),
        in_specs=[q_spec, kprev, kcur, kprev, kcur, q_spec, l_spec, q_spec, l_spec],
        out_specs=[q_spec, kfull, kfull],
        out_shape=[jax.ShapeDtypeStruct(q.shape, F32), jax.ShapeDtypeStruct(k.shape, F32),
                   jax.ShapeDtypeStruct(v.shape, F32)],
        compiler_params=_params(("parallel", "arbitrary")),
    )(q, k, k, v, v, o, lse, do, dlse)


ATTN_PAD = ATTN_BLOCK * DILATIONS[-1]
Q_GROUP_W = GQA * HEAD_DIM
ATTN_VMEM_LIMIT = 56 * 1024 * 1024


def _rope(x, cos_v, sin_v, swap, scale, adjoint):
    if adjoint:
        return (x * cos_v + _dot01_right(x * sin_v, swap)) * scale
    return (x * cos_v + _dot01_right(x, swap) * sin_v) * scale


def _swap_matrix():
    c = HEAD_DIM
    ci = lax.broadcasted_iota(jnp.int32, (c, c), 0)
    cj = lax.broadcasted_iota(jnp.int32, (c, c), 1)
    swap = ((cj == ci + ROPE_HALF) & (ci < ROPE_HALF)) | ((cj == ci - ROPE_HALF) & (ci >= ROPE_HALF) & (ci < ROPE_DIM))
    return swap.astype(BF16)


def _attn_prologue(q_ref, kv_ref, tab_ref, q_s, k_s, v_s, hk, s_len):
    swap = _swap_matrix()
    cos_v, sin_v = tab_ref[0, :, :HEAD_DIM], tab_ref[0, :, HEAD_DIM:]
    for g in range(GQA):
        cols = slice(g * HEAD_DIM, (g + 1) * HEAD_DIM)
        q_s[:, cols] = _rope(q_ref[0, :, cols], cos_v, sin_v, swap, HEAD_DIM ** -0.5, False)
    zeros = jnp.zeros((ATTN_PAD, HEAD_DIM), F32)
    k_s[0:ATTN_PAD, :] = zeros
    v_s[0:ATTN_PAD, :] = zeros
    for h in range(N_KV_HEADS):
        @pl.when(hk == h)
        def _():
            k_s[ATTN_PAD:ATTN_PAD + s_len, :] = _rope(kv_ref[0, :, h * HEAD_DIM:(h + 1) * HEAD_DIM], cos_v, sin_v, swap, 1.0, False)
            v_s[ATTN_PAD:ATTN_PAD + s_len, :] = kv_ref[0, :, LANE + h * HEAD_DIM:LANE + (h + 1) * HEAD_DIM]


def _attn_blocks(s_len):
    out = []
    for i, d in enumerate(DILATIONS):
        nb = s_len // (ATTN_BLOCK * d)
        for r in range(d):
            for n in range(nb):
                start = r + d * ATTN_BLOCK * n
                out.append((i, d, start, ATTN_PAD + start - d * ATTN_BLOCK, n))
    return out


def _rows(start, size, d):
    return pl.ds(start, size, stride=d) if d > 1 else pl.ds(start, size)


def _stack_heads(blk):
    return jnp.concatenate([blk[:, g * HEAD_DIM:(g + 1) * HEAD_DIM] for g in range(GQA)], axis=0)


def _stack_stats(blk):
    return jnp.concatenate([jnp.max(blk[:, g * HEAD_DIM:(g + 1) * HEAD_DIM], axis=1, keepdims=True) for g in range(GQA)], axis=0)


def _attn_in_specs(s_len):
    assert K_COL % (2 * LANE) == 0 and V_COL == K_COL + LANE
    q_spec = pl.BlockSpec((1, s_len, Q_GROUP_W), lambda b, h: (b, 0, Q_COL // Q_GROUP_W + h))
    kv_spec = pl.BlockSpec((1, s_len, 2 * LANE), lambda b, h: (b, 0, K_COL // (2 * LANE)))
    t_spec = pl.BlockSpec((1, s_len, 2 * HEAD_DIM), lambda b, h: (b, 0, 0))
    o_spec = pl.BlockSpec((1, s_len, Q_GROUP_W), lambda b, h: (b, 0, h))
    return q_spec, kv_spec, t_spec, o_spec


def attn_fwd(proj3, rope_tab, *, name):
    b, s_len, _ = proj3.shape
    q_spec, kv_spec, t_spec, o_spec = _attn_in_specs(s_len)
    n_br = len(DILATIONS)

    def body(q_ref, kv_ref, tab_ref, o_ref, lse_ref, q_s, k_s, v_s, *branch_s):
        o_s, l_s = branch_s[:n_br], branch_s[n_br:]
        _attn_prologue(q_ref, kv_ref, tab_ref, q_s, k_s, v_s, pl.program_id(1), s_len)
        for i, d, q0, k0, n in _attn_blocks(s_len):
            qv = _stack_heads(q_s[_rows(q0, ATTN_BLOCK, d), :]).astype(BF16)
            kk = k_s[_rows(k0, 2 * ATTN_BLOCK, d), :].astype(BF16)
            vv = v_s[_rows(k0, 2 * ATTN_BLOCK, d), :].astype(BF16)
            sc = lax.dot_general(qv, kk, (((1,), (1,)), ((), ())), preferred_element_type=F32)
            sc = jnp.where(_attn_mask(n), sc, NEG_BIG)
            m = jnp.max(sc, axis=-1, keepdims=True)
            pr = jnp.exp(sc - m)
            den = jnp.sum(pr, axis=-1, keepdims=True)
            o = jnp.dot(pr.astype(BF16), vv, preferred_element_type=F32) / den
            lse = m + jnp.log(den)
            for g in range(GQA):
                part = slice(g * ATTN_BLOCK, (g + 1) * ATTN_BLOCK)
                o_s[i][_rows(q0, ATTN_BLOCK, d), g * HEAD_DIM:(g + 1) * HEAD_DIM] = o[part]
                l_s[i][_rows(q0, ATTN_BLOCK, d), g * HEAD_DIM:(g + 1) * HEAD_DIM] = jnp.broadcast_to(lse[part], (ATTN_BLOCK, HEAD_DIM))
        step = 256
        for t0 in range(0, s_len, step):
            rs = pl.ds(t0, step)
            for g in range(GQA):
                ls = [l_s[i][rs, g * HEAD_DIM:(g + 1) * HEAD_DIM] for i in range(n_br)]
                m = functools.reduce(jnp.maximum, ls)
                es = [jnp.exp(l - m) for l in ls]
                tot = functools.reduce(lambda a, c: a + c, es)
                inv = 1.0 / tot
                acc = None
                for i in range(n_br):
                    term = (es[i] * inv) * o_s[i][rs, g * HEAD_DIM:(g + 1) * HEAD_DIM]
                    acc = term if acc is None else acc + term
                o_ref[0, rs, g * HEAD_DIM:(g + 1) * HEAD_DIM] = acc
                lse_ref[0, rs, g * HEAD_DIM:(g + 1) * HEAD_DIM] = m + jnp.log(tot)

    return pl.pallas_call(
        body, name=name, grid=(b, N_KV_HEADS), in_specs=[q_spec, kv_spec, t_spec],
        out_specs=[o_spec, o_spec],
        out_shape=[jax.ShapeDtypeStruct((b, s_len, ATTN_WIDTH), F32)] * 2,
        scratch_shapes=[pltpu.VMEM((s_len, Q_GROUP_W), F32), pltpu.VMEM((ATTN_PAD + s_len, HEAD_DIM), F32),
                        pltpu.VMEM((ATTN_PAD + s_len, HEAD_DIM), F32)] + [pltpu.VMEM((s_len, Q_GROUP_W), F32)] * (2 * n_br),
        compiler_params=pltpu.CompilerParams(dimension_semantics=("arbitrary", "arbitrary"), vmem_limit_bytes=ATTN_VMEM_LIMIT),
    )(proj3, proj3, rope_tab)


def attn_bwd(proj3, rope_tab, attn3, lse3, d_attn3, *, name):
    b, s_len, _ = proj3.shape
    q_spec, kv_spec, t_spec, o_spec = _attn_in_specs(s_len)
    kv_out = pl.BlockSpec((1, 1, s_len, HEAD_DIM), lambda bi, h: (bi, h, 0, 0))

    def body(q_ref, kv_ref, tab_ref, o_ref, lse_ref, do_ref, dq_ref, dk_ref, dv_ref,
             q_s, k_s, v_s, dl_s, dq_s, dk_s, dv_s):
        _attn_prologue(q_ref, kv_ref, tab_ref, q_s, k_s, v_s, pl.program_id(1), s_len)
        dq_s[...] = jnp.zeros_like(dq_s)
        dk_s[...] = jnp.zeros_like(dk_s)
        dv_s[...] = jnp.zeros_like(dv_s)
        for g in range(GQA):
            cols = slice(g * HEAD_DIM, (g + 1) * HEAD_DIM)
            delta = jnp.sum(do_ref[0, :, cols] * o_ref[0, :, cols], axis=1, keepdims=True)
            dl_s[:, cols] = jnp.broadcast_to(delta, (s_len, HEAD_DIM))
        for i, d, q0, k0, n in _attn_blocks(s_len):
            qrows, krows = _rows(q0, ATTN_BLOCK, d), _rows(k0, 2 * ATTN_BLOCK, d)
            qv = _stack_heads(q_s[qrows, :]).astype(BF16)
            kk = k_s[krows, :].astype(BF16)
            vv = v_s[krows, :].astype(BF16)
            do16 = _stack_heads(do_ref.at[0][qrows, :]).astype(BF16)
            lse = _stack_stats(lse_ref.at[0][qrows, :])
            delta = _stack_stats(dl_s[qrows, :])
            sc = lax.dot_general(qv, kk, (((1,), (1,)), ((), ())), preferred_element_type=F32)
            pr = jnp.where(_attn_mask(n), jnp.exp(sc - lse), 0.0)
            dv = lax.dot_general(pr.astype(BF16), do16, (((0,), (0,)), ((), ())), preferred_element_type=F32)
            dp = lax.dot_general(do16, vv, (((1,), (1,)), ((), ())), preferred_element_type=F32)
            ds = (pr * (dp - delta)).astype(BF16)
            dq = jnp.dot(ds, kk, preferred_element_type=F32)
            dk = lax.dot_general(ds, qv, (((0,), (0,)), ((), ())), preferred_element_type=F32)
            for g in range(GQA):
                cols = slice(g * HEAD_DIM, (g + 1) * HEAD_DIM)
                dq_s[qrows, cols] += dq[g * ATTN_BLOCK:(g + 1) * ATTN_BLOCK]
            dk_s[krows, :] += dk
            dv_s[krows, :] += dv
        swap = _swap_matrix()
        cos_v, sin_v = tab_ref[0, :, :HEAD_DIM], tab_ref[0, :, HEAD_DIM:]
        for g in range(GQA):
            cols = slice(g * HEAD_DIM, (g + 1) * HEAD_DIM)
            dq_ref[0, :, cols] = _rope(dq_s[:, cols], cos_v, sin_v, swap, HEAD_DIM ** -0.5, True)
        dk_ref[0, 0] = _rope(dk_s[ATTN_PAD:ATTN_PAD + s_len, :], cos_v, sin_v, swap, 1.0, True)
        dv_ref[0, 0] = dv_s[ATTN_PAD:ATTN_PAD + s_len, :]

    kv_shape = jax.ShapeDtypeStruct((b, N_KV_HEADS, s_len, HEAD_DIM), F32)
    return pl.pallas_call(
        body, name=name, grid=(b, N_KV_HEADS),
        in_specs=[q_spec, kv_spec, t_spec, o_spec, o_spec, o_spec],
        out_specs=[o_spec, kv_out, kv_out],
        out_shape=[jax.ShapeDtypeStruct((b, s_len, ATTN_WIDTH), F32), kv_shape, kv_shape],
        scratch_shapes=[pltpu.VMEM((s_len, Q_GROUP_W), F32), pltpu.VMEM((ATTN_PAD + s_len, HEAD_DIM), F32),
                        pltpu.VMEM((ATTN_PAD + s_len, HEAD_DIM), F32), pltpu.VMEM((s_len, Q_GROUP_W), F32),
                        pltpu.VMEM((s_len, Q_GROUP_W), F32), pltpu.VMEM((ATTN_PAD + s_len, HEAD_DIM), F32),
                        pltpu.VMEM((ATTN_PAD + s_len, HEAD_DIM), F32)],
        compiler_params=pltpu.CompilerParams(dimension_semantics=("arbitrary", "arbitrary"), vmem_limit_bytes=ATTN_VMEM_LIMIT),
    )(proj3, proj3, rope_tab, attn3, lse3, d_attn3)


HALF_W = 2 * HEAD_DIM
N_HALF = Q_GROUP_W // HALF_W
_ATTN_BIAS_BUF = pltpu.VMEM((2, GQA * ATTN_BLOCK, 2 * ATTN_BLOCK), F32)


def _attn_bias(bias_s):
    for first in (0, 1):
        bias_s[first] = jnp.where(_attn_mask(first), 0.0, NEG_BIG)


def _attn_prologue(q_refs, kv_ref, tab_ref, q_s, kv_s, hk, s_len):
    swap = _swap_matrix()
    cos_v, sin_v = tab_ref[0, :, :HEAD_DIM], tab_ref[0, :, HEAD_DIM:]
    for j in range(N_HALF):
        for e in range(2):
            cols = slice(e * HEAD_DIM, (e + 1) * HEAD_DIM)
            q_s[j][:, cols] = _rope(q_refs[j][0, :, cols], cos_v, sin_v, swap, HEAD_DIM ** -0.5, False)
    kv_s[0:ATTN_PAD, :] = jnp.zeros((ATTN_PAD, HALF_W), F32)
    for h in range(N_KV_HEADS):
        @pl.when(hk == h)
        def _():
            kv_s[ATTN_PAD:ATTN_PAD + s_len, :HEAD_DIM] = _rope(kv_ref[0, :, h * HEAD_DIM:(h + 1) * HEAD_DIM], cos_v, sin_v,
                                                               swap, 1.0, False)
            kv_s[ATTN_PAD:ATTN_PAD + s_len, HEAD_DIM:] = kv_ref[0, :, LANE + h * HEAD_DIM:LANE + (h + 1) * HEAD_DIM]


def _stack_heads(halves):
    return jnp.concatenate([h[:, e * HEAD_DIM:(e + 1) * HEAD_DIM] for h in halves for e in range(2)], axis=0)


def _unstack_heads(x, j):
    return jnp.concatenate([x[(2 * j + e) * ATTN_BLOCK:(2 * j + e + 1) * ATTN_BLOCK] for e in range(2)], axis=1)


def _stack_stats(halves):
    return jnp.concatenate([jnp.max(h[:, e * HEAD_DIM:(e + 1) * HEAD_DIM], axis=1, keepdims=True)
                            for h in halves for e in range(2)], axis=0)


def _attn_in_specs(s_len):
    assert K_COL % (2 * LANE) == 0 and V_COL == K_COL + LANE

    def halves(first_tile):
        return [pl.BlockSpec((1, s_len, HALF_W), functools.partial(lambda b, h, j: (b, 0, first_tile + N_HALF * h + j), j=j))
                for j in range(N_HALF)]

    kv_spec = pl.BlockSpec((1, s_len, 2 * LANE), lambda b, h: (b, 0, K_COL // (2 * LANE)))
    t_spec = pl.BlockSpec((1, s_len, 2 * HEAD_DIM), lambda b, h: (b, 0, 0))
    o_spec = pl.BlockSpec((1, s_len, Q_GROUP_W), lambda b, h: (b, 0, h))
    return halves(Q_COL // HALF_W), kv_spec, t_spec, o_spec, halves(0)


class SideCopy:
    def __init__(self, side, *, n_in, n_out, grid):
        self.side, self.n_in, self.n_out, self.grid = side, n_in, n_out, grid
        hbm = pl.BlockSpec(memory_space=pltpu.HBM)
        if side is None:
            self.in_specs, self.out_specs, self.out_shape, self.scratch, self.args = [], [], [], [], []
            return
        src, per_dest = side
        shape = src.shape if per_dest else (N_CHIPS,) + src.shape
        self.in_specs, self.out_specs, self.args = [hbm], [hbm], [src]
        self.out_shape = [jax.ShapeDtypeStruct(shape, src.dtype)]
        self.scratch = [pltpu.SemaphoreType.DMA((N_CHIPS - 1,)), pltpu.SemaphoreType.DMA((N_CHIPS - 1,)), pltpu.SemaphoreType.DMA]

    def wrap(self, body):
        if self.side is None:
            return body
        n_in, n_out, grid, per_dest = self.n_in, self.n_out, self.grid, self.side[1]

        def wrapped(*refs):
            ins, src = refs[:n_in], refs[n_in]
            outs, dst = refs[n_in + 1:n_in + 1 + n_out], refs[n_in + 1 + n_out]
            scratch, sems = refs[n_in + 2 + n_out:-3], refs[-3:]
            ids = [pl.program_id(a) for a in range(len(grid))]
            first = functools.reduce(lambda p, q: p & q, [i == 0 for i in ids])
            last = functools.reduce(lambda p, q: p & q, [i == g - 1 for i, g in zip(ids, grid)])

            @pl.when(first)
            def _():
                local, sends, _ = _chip_copies(src, dst, *sems, per_dest)
                local.start()
                for cp in sends:
                    cp.start()

            body(*ins, *outs, *scratch)

            @pl.when(last)
            def _():
                local, sends, recvs = _chip_copies(src, dst, *sems, per_dest)
                for cp in recvs:
                    cp.wait_recv()
                for cp in sends:
                    cp.wait_send()
                local.wait()

        return wrapped


def _chip_copies(src_ref, dst_ref, send_sems, recv_sems, local_sem, per_dest):
    x, y, c = lax.axis_index("x"), lax.axis_index("y"), lax.axis_index("c")
    chip = 2 * x + y
    own = src_ref.at[chip] if per_dest else src_ref
    local = pltpu.make_async_copy(own, dst_ref.at[chip], local_sem)
    sends, recvs = [], []
    for k, (px, py) in enumerate([(1 - x, y), (x, 1 - y), (1 - x, 1 - y)]):
        peer = dict(send_sem=send_sems.at[k], recv_sem=recv_sems.at[k], device_id=(px, py, c), device_id_type=MESH)
        sends.append(pltpu.make_async_remote_copy(src_ref=src_ref.at[2 * px + py] if per_dest else src_ref,
                                                  dst_ref=dst_ref.at[chip], **peer))
        recvs.append(pltpu.make_async_remote_copy(src_ref=own, dst_ref=dst_ref.at[2 * px + py], **peer))
    return local, sends, recvs


def attn_fwd(proj3, rope_tab, *, name, side=None):
    b, s_len, _ = proj3.shape
    q_specs, kv_spec, t_spec, o_spec, _ = _attn_in_specs(s_len)
    n_br = len(DILATIONS)

    def body(*refs):
        q_refs, (kv_ref, tab_ref, o_ref, lse_ref) = refs[:N_HALF], refs[N_HALF:N_HALF + 4]
        scratch = refs[N_HALF + 4:]
        q_s, kv_s = scratch[:N_HALF], scratch[N_HALF]
        o_s = [scratch[N_HALF + 1 + i * N_HALF:N_HALF + 1 + (i + 1) * N_HALF] for i in range(n_br)]
        l_s = [scratch[N_HALF + 1 + (n_br + i) * N_HALF:N_HALF + 1 + (n_br + i + 1) * N_HALF] for i in range(n_br)]
        bias_s = scratch[-1]
        _attn_prologue(q_refs, kv_ref, tab_ref, q_s, kv_s, pl.program_id(1), s_len)
        _attn_bias(bias_s)
        for i, d, q0, k0, n in _attn_blocks(s_len):
            qrows = _rows(q0, ATTN_BLOCK, d)
            qv = _stack_heads([q_s[j][qrows, :] for j in range(N_HALF)]).astype(BF16)
            kvb = kv_s[_rows(k0, 2 * ATTN_BLOCK, d), :].astype(BF16)
            kk, vv = kvb[:, :HEAD_DIM], kvb[:, HEAD_DIM:]
            sc = lax.dot_general(qv, kk, (((1,), (1,)), ((), ())), preferred_element_type=F32)
            sc = sc + bias_s[min(n, 1)]
            m = jnp.max(sc, axis=-1, keepdims=True)
            pr = jnp.exp(sc - m)
            den = jnp.sum(pr, axis=-1, keepdims=True)
            o = jnp.dot(pr.astype(BF16), vv, preferred_element_type=F32) / den
            lse_b = jnp.broadcast_to(m + jnp.log(den), (GQA * ATTN_BLOCK, HEAD_DIM))
            for j in range(N_HALF):
                o_s[i][j][qrows, :] = _unstack_heads(o, j)
                l_s[i][j][qrows, :] = _unstack_heads(lse_b, j)
        step = 256
        for t0 in range(0, s_len, step):
            rs = pl.ds(t0, step)
            for j in range(N_HALF):
                ls = [l_s[i][j][rs, :] for i in range(n_br)]
                m = functools.reduce(jnp.maximum, ls)
                es = [jnp.exp(l - m) for l in ls]
                tot = functools.reduce(lambda a, c: a + c, es)
                inv = 1.0 / tot
                acc = None
                for i in range(n_br):
                    term = (es[i] * inv) * o_s[i][j][rs, :]
                    acc = term if acc is None else acc + term
                o_ref[0, rs, j * HALF_W:(j + 1) * HALF_W] = acc
                lse_ref[0, rs, j * HALF_W:(j + 1) * HALF_W] = m + jnp.log(tot)

    half_buf = pltpu.VMEM((s_len, HALF_W), F32)
    call = SideCopy(side, n_in=N_HALF + 2, n_out=2, grid=(b, N_KV_HEADS))
    return pl.pallas_call(
        call.wrap(body), name=name, grid=(b, N_KV_HEADS), in_specs=q_specs + [kv_spec, t_spec] + call.in_specs,
        out_specs=[o_spec, o_spec] + call.out_specs,
        out_shape=[jax.ShapeDtypeStruct((b, s_len, ATTN_WIDTH), F32)] * 2 + call.out_shape,
        scratch_shapes=[half_buf] * N_HALF + [pltpu.VMEM((ATTN_PAD + s_len, HALF_W), F32)] + [half_buf] * (2 * n_br * N_HALF)
        + [_ATTN_BIAS_BUF] + call.scratch,
        compiler_params=pltpu.CompilerParams(dimension_semantics=("arbitrary", "arbitrary"), vmem_limit_bytes=ATTN_VMEM_LIMIT),
    )(*([proj3] * (N_HALF + 1)), rope_tab, *call.args)


def attn_bwd(proj3, rope_tab, attn3, lse3, d_attn3, *, name, side=None):
    b, s_len, _ = proj3.shape
    q_specs, kv_spec, t_spec, o_spec, half_specs = _attn_in_specs(s_len)
    kv_out = pl.BlockSpec((1, 1, s_len, HEAD_DIM), lambda bi, h: (bi, h, 0, 0))

    def body(*refs):
        q_refs = refs[:N_HALF]
        kv_ref, tab_ref, o_ref = refs[N_HALF:N_HALF + 3]
        lse_refs = refs[N_HALF + 3:2 * N_HALF + 3]
        do_refs = refs[2 * N_HALF + 3:3 * N_HALF + 3]
        dq_ref, dk_ref, dv_ref = refs[3 * N_HALF + 3:3 * N_HALF + 6]
        scratch = refs[3 * N_HALF + 6:]
        q_s, kv_s = scratch[:N_HALF], scratch[N_HALF]
        dl_s = scratch[N_HALF + 1:2 * N_HALF + 1]
        dq_s = scratch[2 * N_HALF + 1:3 * N_HALF + 1]
        dkv_s = scratch[3 * N_HALF + 1]
        bias_s = scratch[-1]
        _attn_prologue(q_refs, kv_ref, tab_ref, q_s, kv_s, pl.program_id(1), s_len)
        _attn_bias(bias_s)
        dkv_s[...] = jnp.zeros_like(dkv_s)
        for j in range(N_HALF):
            dq_s[j][...] = jnp.zeros_like(dq_s[j])
            for e in range(2):
                cols = slice(e * HEAD_DIM, (e + 1) * HEAD_DIM)
                ocols = slice(j * HALF_W + e * HEAD_DIM, j * HALF_W + (e + 1) * HEAD_DIM)
                delta = jnp.sum(do_refs[j][0, :, cols] * o_ref[0, :, ocols], axis=1, keepdims=True)
                dl_s[j][:, cols] = jnp.broadcast_to(delta, (s_len, HEAD_DIM))
        for i, d, q0, k0, n in _attn_blocks(s_len):
            qrows, krows = _rows(q0, ATTN_BLOCK, d), _rows(k0, 2 * ATTN_BLOCK, d)
            qv = _stack_heads([q_s[j][qrows, :] for j in range(N_HALF)]).astype(BF16)
            kvb = kv_s[krows, :].astype(BF16)
            kk, vv = kvb[:, :HEAD_DIM], kvb[:, HEAD_DIM:]
            do16 = _stack_heads([do_refs[j].at[0][qrows, :] for j in range(N_HALF)]).astype(BF16)
            lse = _stack_stats([lse_refs[j].at[0][qrows, :] for j in range(N_HALF)])
            delta = _stack_stats([dl_s[j][qrows, :] for j in range(N_HALF)])
            sc = lax.dot_general(qv, kk, (((1,), (1,)), ((), ())), preferred_element_type=F32)
            pr = jnp.exp(sc + bias_s[min(n, 1)] - lse)
            dv = lax.dot_general(pr.astype(BF16), do16, (((0,), (0,)), ((), ())), preferred_element_type=F32)
            dp = lax.dot_general(do16, vv, (((1,), (1,)), ((), ())), preferred_element_type=F32)
            ds = (pr * (dp - delta)).astype(BF16)
            dq = jnp.dot(ds, kk, preferred_element_type=F32)
            dk = lax.dot_general(ds, qv, (((0,), (0,)), ((), ())), preferred_element_type=F32)
            for j in range(N_HALF):
                dq_s[j][qrows, :] += _unstack_heads(dq, j)
            dkv_s[krows, :] += jnp.concatenate([dk, dv], axis=1)
        swap = _swap_matrix()
        cos_v, sin_v = tab_ref[0, :, :HEAD_DIM], tab_ref[0, :, HEAD_DIM:]
        for j in range(N_HALF):
            for e in range(2):
                cols = slice(e * HEAD_DIM, (e + 1) * HEAD_DIM)
                ocols = slice(j * HALF_W + e * HEAD_DIM, j * HALF_W + (e + 1) * HEAD_DIM)
                dq_ref[0, :, ocols] = _rope(dq_s[j][:, cols], cos_v, sin_v, swap, HEAD_DIM ** -0.5, True)
        dk_ref[0, 0] = _rope(dkv_s[ATTN_PAD:ATTN_PAD + s_len, :HEAD_DIM], cos_v, sin_v, swap, 1.0, True)
        dv_ref[0, 0] = dkv_s[ATTN_PAD:ATTN_PAD + s_len, HEAD_DIM:]

    kv_shape = jax.ShapeDtypeStruct((b, N_KV_HEADS, s_len, HEAD_DIM), F32)
    half_buf = pltpu.VMEM((s_len, HALF_W), F32)
    pad_buf = pltpu.VMEM((ATTN_PAD + s_len, HALF_W), F32)
    call = SideCopy(side, n_in=3 * N_HALF + 3, n_out=3, grid=(b, N_KV_HEADS))
    return pl.pallas_call(
        call.wrap(body), name=name, grid=(b, N_KV_HEADS),
        in_specs=q_specs + [kv_spec, t_spec, o_spec] + half_specs + half_specs + call.in_specs,
        out_specs=[o_spec, kv_out, kv_out] + call.out_specs,
        out_shape=[jax.ShapeDtypeStruct((b, s_len, ATTN_WIDTH), F32), kv_shape, kv_shape] + call.out_shape,
        scratch_shapes=[half_buf] * N_HALF + [pad_buf] + [half_buf] * (2 * N_HALF) + [pad_buf, _ATTN_BIAS_BUF] + call.scratch,
        compiler_params=pltpu.CompilerParams(dimension_semantics=("arbitrary", "arbitrary"), vmem_limit_bytes=ATTN_VMEM_LIMIT),
    )(*([proj3] * (N_HALF + 1)), rope_tab, attn3, *([lse3] * N_HALF), *([d_attn3] * N_HALF), *call.args)


CONV_TC = 256
CONV_COL0 = XBC_COL // CONV_TC


def _shift_down(u, s):
    if s == 0:
        return u
    rows = lax.broadcasted_iota(jnp.int32, u.shape, 0)
    return jnp.where(rows >= s, pltpu.roll(u, s, 0), 0.0)


def _shift_up(u, s):
    if s == 0:
        return u
    n = u.shape[0]
    rows = lax.broadcasted_iota(jnp.int32, u.shape, 0)
    return jnp.where(rows < n - s, pltpu.roll(u, n - s, 0), 0.0)


def conv_silu_fwd(proj3, w, bias, *, name):
    b, s, _ = proj3.shape
    u_spec = pl.BlockSpec((1, s, CONV_TC), lambda j, bi: (bi, 0, CONV_COL0 + j))
    o_spec = pl.BlockSpec((1, s, CONV_TC), lambda j, bi: (bi, 0, j))
    w_spec = pl.BlockSpec((CONV_WIDTH, CONV_TC), lambda j, bi: (0, j))
    b_spec = pl.BlockSpec((1, CONV_TC), lambda j, bi: (0, j))

    def body(u_ref, w_ref, b_ref, o_ref):
        u = u_ref[0]
        y = jnp.broadcast_to(b_ref[...], u.shape)
        for k in range(CONV_WIDTH):
            y = y + w_ref[k:k + 1, :] * _shift_down(u, CONV_WIDTH - 1 - k)
        o_ref[0] = y * jax.nn.sigmoid(y)

    return pl.pallas_call(
        body, name=name, grid=(CONV_CH // CONV_TC, b), in_specs=[u_spec, w_spec, b_spec], out_specs=o_spec,
        out_shape=jax.ShapeDtypeStruct((b, s, CONV_CH), F32),
        compiler_params=_params(("parallel", "arbitrary")),
    )(proj3, w, bias)


def conv_silu_bwd(proj3, w, bias, dact, *, name):
    b, s, _ = proj3.shape
    u_spec = pl.BlockSpec((1, s, CONV_TC), lambda j, bi: (bi, 0, CONV_COL0 + j))
    o_spec = pl.BlockSpec((1, s, CONV_TC), lambda j, bi: (bi, 0, j))
    w_spec = pl.BlockSpec((CONV_WIDTH, CONV_TC), lambda j, bi: (0, j))
    b_spec = pl.BlockSpec((1, CONV_TC), lambda j, bi: (0, j))

    def body(u_ref, w_ref, b_ref, g_ref, du_ref, dw_ref, db_ref):
        bi = pl.program_id(1)

        @pl.when(bi == 0)
        def _():
            dw_ref[...] = jnp.zeros_like(dw_ref)
            db_ref[...] = jnp.zeros_like(db_ref)

        u = u_ref[0]
        y = jnp.broadcast_to(b_ref[...], u.shape)
        shifted = [_shift_down(u, CONV_WIDTH - 1 - k) for k in range(CONV_WIDTH)]
        for k in range(CONV_WIDTH):
            y = y + w_ref[k:k + 1, :] * shifted[k]
        sig = jax.nn.sigmoid(y)
        dy = g_ref[0] * (sig * (1.0 + y * (1.0 - sig)))
        du = jnp.zeros_like(u)
        for k in range(CONV_WIDTH):
            du = du + w_ref[k:k + 1, :] * _shift_up(dy, CONV_WIDTH - 1 - k)
            dw_ref[k:k + 1, :] += jnp.sum(dy * shifted[k], axis=0, keepdims=True)
        du_ref[0] = du
        db_ref[...] += jnp.sum(dy, axis=0, keepdims=True)

    return pl.pallas_call(
        body, name=name, grid=(CONV_CH // CONV_TC, b), in_specs=[u_spec, w_spec, b_spec, o_spec],
        out_specs=[o_spec, w_spec, b_spec],
        out_shape=[jax.ShapeDtypeStruct((b, s, CONV_CH), F32), jax.ShapeDtypeStruct((CONV_WIDTH, CONV_CH), F32),
                   jax.ShapeDtypeStruct((1, CONV_CH), F32)],
        compiler_params=_params(("parallel", "arbitrary")),
    )(proj3, w, bias, dact)


def _softplus(z):
    e = jnp.exp(-jnp.abs(z))
    u = 1.0 + e
    log1p = jnp.where(u == 1.0, e, jnp.log(u) * e / jnp.where(u == 1.0, 1.0, u - 1.0))
    return jnp.maximum(z, 0.0) + log1p


def _tri(lower):
    r = lax.broadcasted_iota(jnp.int32, (CHUNK, CHUNK), 0)
    c = lax.broadcasted_iota(jnp.int32, (CHUNK, CHUNK), 1)
    return (r >= c) if lower else (r <= c)


def _ssd_common(dtr_ref, dtb_ref, alog_ref):
    z = dtr_ref[0] + dtb_ref[...]
    dt = _softplus(z)
    aneg = -jnp.exp(alog_ref[...])
    acs = _dot01_left(_tri(True).astype(BF16), dt * aneg)
    return z, dt, aneg, acs


def _col(mat, onehot):
    return jnp.sum(mat * onehot, axis=1, keepdims=True)


def _ssd_head(x, dt_j, acs_j, cb, tri_mask, last_row, acs_row=None):
    acs_last = jnp.sum(acs_j * last_row, axis=0, keepdims=True)
    xg = x * dt_j
    bc = jnp.broadcast_to(acs_j, (CHUNK, CHUNK))
    dm = bc - (bc.T if acs_row is None else jnp.broadcast_to(acs_row, (CHUNK, CHUNK)))
    lm = jnp.where(tri_mask, jnp.exp(jnp.where(tri_mask, dm, 0.0)), 0.0)
    mm = cb * lm
    decay_s = jnp.exp(acs_last - acs_j)
    return acs_last, xg, lm, mm, decay_s


def _ssd_specs(nc, reverse):
    cidx = (lambda c: nc - 1 - c) if reverse else (lambda c: c)
    act_spec = pl.BlockSpec((1, CHUNK, CONV_CH), lambda b, c: (b, cidx(c), 0))
    y_spec = pl.BlockSpec((1, CHUNK, SSM_INNER), lambda b, c: (b, cidx(c), 0))
    dt_in_spec = pl.BlockSpec((1, CHUNK, LANE), lambda b, c: (b, cidx(c), DT_COL // LANE))
    dt_out_spec = pl.BlockSpec((1, CHUNK, LANE), lambda b, c: (b, cidx(c), 0))
    par_spec = pl.BlockSpec((1, LANE), lambda b, c: (0, 0))
    h_spec = pl.BlockSpec((1, SSM_HEADS, 1, SSM_P, D_STATE), lambda b, c: (b, 0, cidx(c), 0, 0))
    return act_spec, y_spec, dt_in_spec, dt_out_spec, par_spec, h_spec


def _head_cols(h):
    return slice(h * SSM_P, (h + 1) * SSM_P)


def _group_cols(g, which):
    start = SSM_INNER + which * SSM_GROUPS * D_STATE + g * D_STATE
    return slice(start, start + D_STATE)


def ssd_fwd(act3, proj3, dtb, alog, dsk, *, name):
    b, s, _ = act3.shape
    nc = s // CHUNK
    act_spec, y_spec, dt_in_spec, _, par_spec, h_spec = _ssd_specs(nc, False)

    def body(act_ref, dtr_ref, dtb_ref, alog_ref, dsk_ref, y_ref, hp_ref, state):
        c = pl.program_id(1)

        @pl.when(c == 0)
        def _():
            state[...] = jnp.zeros_like(state)

        _, dt, _, acs = _ssd_common(dtr_ref, dtb_ref, alog_ref)
        acs_t = acs.T
        tri_mask = _tri(True)
        last_row = (lax.broadcasted_iota(jnp.int32, (CHUNK, 1), 0) == CHUNK - 1).astype(F32)
        for g in range(SSM_GROUPS):
            b16 = act_ref[0, :, _group_cols(g, 0)].astype(BF16)
            c16 = act_ref[0, :, _group_cols(g, 1)].astype(BF16)
            cb = lax.dot_general(c16, b16, (((1,), (1,)), ((), ())), preferred_element_type=F32)
            for j in range(HEADS_PER_GROUP):
                hidx = g * HEADS_PER_GROUP + j
                x = act_ref[0, :, _head_cols(hidx)]
                dt_j, acs_j = dt[:, hidx:hidx + 1], acs[:, hidx:hidx + 1]
                acs_last, xg, _, mm, decay_s = _ssd_head(x, dt_j, acs_j, cb, tri_mask, last_row, acs_t[hidx:hidx + 1, :])
                y_diag = jnp.dot(mm.astype(BF16), xg.astype(BF16), preferred_element_type=F32)
                st = lax.dot_general((xg * decay_s).astype(BF16), b16, (((0,), (0,)), ((), ())), preferred_element_type=F32)
                hp = state[hidx]
                hp_ref[0, hidx, 0] = hp
                y_off = lax.dot_general(c16, hp.astype(BF16), (((1,), (1,)), ((), ())), preferred_element_type=F32)
                d_j = dsk_ref[:, hidx:hidx + 1]
                y_ref[0, :, _head_cols(hidx)] = y_diag + y_off * jnp.exp(acs_j) + d_j * x
                state[hidx] = hp * jnp.exp(acs_last) + st

    return pl.pallas_call(
        body, name=name, grid=(b, nc),
        in_specs=[act_spec, dt_in_spec, par_spec, par_spec, par_spec],
        out_specs=[y_spec, h_spec],
        out_shape=[jax.ShapeDtypeStruct((b, s, SSM_INNER), F32),
                   jax.ShapeDtypeStruct((b, SSM_HEADS, nc, SSM_P, D_STATE), F32)],
        scratch_shapes=[pltpu.VMEM((SSM_HEADS, SSM_P, D_STATE), F32)],
        compiler_params=_params(("arbitrary", "arbitrary")),
    )(act3, proj3, dtb, alog, dsk)


def ssd_bwd(act3, proj3, dtb, alog, dsk, hprev, dy3, *, name):
    b, s, _ = act3.shape
    nc = s // CHUNK
    act_spec, y_spec, dt_in_spec, dt_out_spec, par_spec, h_spec = _ssd_specs(nc, True)
    dpar_spec = pl.BlockSpec((8, LANE), lambda bi, c: (0, 0))

    def body(act_ref, dtr_ref, dtb_ref, alog_ref, dsk_ref, hp_ref, dy_ref, dact_ref, ddtr_ref, dpar_ref, dstate):
        bi, c = pl.program_id(0), pl.program_id(1)

        @pl.when(c == 0)
        def _():
            dstate[...] = jnp.zeros_like(dstate)

        @pl.when((bi == 0) & (c == 0))
        def _():
            dpar_ref[...] = jnp.zeros_like(dpar_ref)

        z, dt, aneg, acs = _ssd_common(dtr_ref, dtb_ref, alog_ref)
        acs_t = acs.T
        tri_mask = _tri(True)
        last_row = (lax.broadcasted_iota(jnp.int32, (CHUNK, 1), 0) == CHUNK - 1).astype(F32)
        lanes = lax.broadcasted_iota(jnp.int32, (1, LANE), 1)
        sublanes = lax.broadcasted_iota(jnp.int32, (LANE, 1), 0)
        ddt_mat = jnp.zeros((CHUNK, LANE), F32)
        dacs_mat = jnp.zeros((CHUNK, LANE), F32)
        dacs_rows = jnp.zeros((LANE, CHUNK), F32)
        ddsk_row = jnp.zeros((1, LANE), F32)
        for g in range(SSM_GROUPS):
            b16 = act_ref[0, :, _group_cols(g, 0)].astype(BF16)
            c16 = act_ref[0, :, _group_cols(g, 1)].astype(BF16)
            cb = lax.dot_general(c16, b16, (((1,), (1,)), ((), ())), preferred_element_type=F32)
            dcb = jnp.zeros((CHUNK, CHUNK), F32)
            db_acc = jnp.zeros((CHUNK, D_STATE), F32)
            dc_acc = jnp.zeros((CHUNK, D_STATE), F32)
            for j in range(HEADS_PER_GROUP):
                hidx = g * HEADS_PER_GROUP + j
                onehot = (lanes == hidx).astype(F32)
                x = act_ref[0, :, _head_cols(hidx)]
                dt_j, acs_j = dt[:, hidx:hidx + 1], acs[:, hidx:hidx + 1]
                acs_last, xg, lm, mm, decay_s = _ssd_head(x, dt_j, acs_j, cb, tri_mask, last_row, acs_t[hidx:hidx + 1, :])
                ea = jnp.exp(acs_j)
                cd = jnp.exp(acs_last)
                d_j = dsk_ref[:, hidx:hidx + 1]
                hp = hp_ref[0, hidx, 0]
                hp16 = hp.astype(BF16)
                g_y = dy_ref[0, :, _head_cols(hidx)]
                g_y16 = g_y.astype(BF16)
                g_hn = dstate[hidx]
                g_hn16 = g_hn.astype(BF16)
                xg16 = xg.astype(BF16)
                ddsk_row = ddsk_row + jnp.sum(jnp.sum(g_y * x, axis=1, keepdims=True), axis=0, keepdims=True) * onehot
                d_mm = lax.dot_general(g_y16, xg16, (((1,), (1,)), ((), ())), preferred_element_type=F32)
                d_xg = lax.dot_general(mm.astype(BF16), g_y16, (((0,), (0,)), ((), ())), preferred_element_type=F32)
                dcb = dcb + d_mm * lm
                d_dm = d_mm * mm
                d_acs = jnp.sum(d_dm, axis=1, keepdims=True)
                dacs_rows = dacs_rows + (sublanes == hidx).astype(F32) * jnp.sum(d_dm, axis=0, keepdims=True)
                t_off = lax.dot_general(c16, hp16, (((1,), (1,)), ((), ())), preferred_element_type=F32)
                d_t16 = (g_y * ea).astype(BF16)
                d_acs = d_acs + jnp.sum(g_y * t_off, axis=1, keepdims=True) * ea
                dc_acc = dc_acc + jnp.dot(d_t16, hp16, preferred_element_type=F32)
                d_hp = lax.dot_general(d_t16, c16, (((0,), (0,)), ((), ())), preferred_element_type=F32) + g_hn * cd
                d_last = jnp.sum(jnp.sum(g_hn * hp, axis=1, keepdims=True), axis=0, keepdims=True) * cd
                d_w = lax.dot_general(b16, g_hn16, (((1,), (1,)), ((), ())), preferred_element_type=F32)
                db_acc = db_acc + jnp.dot((xg * decay_s).astype(BF16), g_hn16, preferred_element_type=F32)
                d_xg = d_xg + d_w * decay_s
                d_ds = jnp.sum(d_w * xg, axis=1, keepdims=True) * decay_s
                d_last = d_last + jnp.sum(d_ds, axis=0, keepdims=True)
                d_acs = d_acs - d_ds + d_last * last_row
                dact_ref[0, :, _head_cols(hidx)] = d_j * g_y + d_xg * dt_j
                ddt_mat = ddt_mat + jnp.sum(d_xg * x, axis=1, keepdims=True) * onehot
                dacs_mat = dacs_mat + d_acs * onehot
                dstate[hidx] = d_hp
            dcb16 = dcb.astype(BF16)
            dact_ref[0, :, _group_cols(g, 1)] = dc_acc + jnp.dot(dcb16, b16, preferred_element_type=F32)
            dact_ref[0, :, _group_cols(g, 0)] = db_acc + lax.dot_general(dcb16, c16, (((0,), (0,)), ((), ())),
                                                                         preferred_element_type=F32)
        d_a = _dot01_left(_tri(False).astype(BF16), dacs_mat - dacs_rows.T)
        ddt_mat = ddt_mat + d_a * aneg
        d_raw = ddt_mat * jax.nn.sigmoid(z)
        ddtr_ref[0] = d_raw
        dpar_ref[0:1, :] += jnp.sum(d_raw, axis=0, keepdims=True)
        dpar_ref[1:2, :] += jnp.sum(d_a * dt, axis=0, keepdims=True) * aneg
        dpar_ref[2:3, :] += ddsk_row

    return pl.pallas_call(
        body, name=name, grid=(b, nc),
        in_specs=[act_spec, dt_in_spec, par_spec, par_spec, par_spec, h_spec, y_spec],
        out_specs=[act_spec, dt_out_spec, dpar_spec],
        out_shape=[jax.ShapeDtypeStruct(act3.shape, F32), jax.ShapeDtypeStruct((b, s, LANE), F32),
                   jax.ShapeDtypeStruct((8, LANE), F32)],
        scratch_shapes=[pltpu.VMEM((SSM_HEADS, SSM_P, D_STATE), F32)],
        compiler_params=_params(("arbitrary", "arbitrary")),
    )(act3, proj3, dtb, alog, dsk, hprev, dy3)


def _unused_ssd_specs(nc, reverse):
    cidx = (lambda c: nc - 1 - c) if reverse else (lambda c: c)
    x_spec = pl.BlockSpec((1, HEADS_PER_GROUP, CHUNK, SSM_P), lambda b, c, g: (b, g, cidx(c), 0))
    bc_spec = pl.BlockSpec((1, 1, CHUNK, D_STATE), lambda b, c, g: (b, g, cidx(c), 0))
    dt_spec = pl.BlockSpec((1, CHUNK, LANE), lambda b, c, g: (b, cidx(c), 0))
    par_spec = pl.BlockSpec((1, LANE), lambda b, c, g: (0, 0))
    h_spec = pl.BlockSpec((1, HEADS_PER_GROUP, 1, SSM_P, D_STATE), lambda b, c, g: (b, g, cidx(c), 0, 0))
    return x_spec, bc_spec, dt_spec, par_spec, h_spec


def _unused_ssd_fwd(xs, bm, cm, dtr, dtb, alog, dsk, *, name):
    b, _, s, _ = xs.shape
    nc = s // CHUNK
    x_spec, bc_spec, dt_spec, par_spec, h_spec = _ssd_specs(nc, False)

    def body(x_ref, b_ref, c_ref, dtr_ref, dtb_ref, alog_ref, dsk_ref, y_ref, hp_ref, state):
        c, g = pl.program_id(1), pl.program_id(2)

        @pl.when(c == 0)
        def _():
            state[pl.ds(g * HEADS_PER_GROUP, HEADS_PER_GROUP)] = jnp.zeros((HEADS_PER_GROUP, SSM_P, D_STATE), F32)

        _, dt, _, acs = _ssd_common(dtr_ref, dtb_ref, alog_ref)
        b16, c16 = b_ref[0, 0].astype(BF16), c_ref[0, 0].astype(BF16)
        cb = lax.dot_general(c16, b16, (((1,), (1,)), ((), ())), preferred_element_type=F32)
        tri_mask = _tri(True)
        last_row = (lax.broadcasted_iota(jnp.int32, (CHUNK, 1), 0) == CHUNK - 1).astype(F32)
        lanes = lax.broadcasted_iota(jnp.int32, (1, LANE), 1)
        for j in range(HEADS_PER_GROUP):
            hidx = g * HEADS_PER_GROUP + j
            onehot = (lanes == hidx).astype(F32)
            x = x_ref[0, j]
            dt_j, acs_j = _col(dt, onehot), _col(acs, onehot)
            acs_last, xg, _, mm, decay_s = _ssd_head(x, dt_j, acs_j, cb, tri_mask, last_row)
            xg16 = xg.astype(BF16)
            y_diag = jnp.dot(mm.astype(BF16), xg16, preferred_element_type=F32)
            st = lax.dot_general((xg * decay_s).astype(BF16), b16, (((0,), (0,)), ((), ())), preferred_element_type=F32)
            hp = state[hidx]
            hp_ref[0, j, 0] = hp
            y_off = lax.dot_general(c16, hp.astype(BF16), (((1,), (1,)), ((), ())), preferred_element_type=F32)
            d_j = jnp.sum(dsk_ref[...] * onehot, axis=1, keepdims=True)
            y_ref[0, j] = y_diag + y_off * jnp.exp(acs_j) + d_j * x
            state[hidx] = hp * jnp.exp(acs_last) + st

    return pl.pallas_call(
        body, name=name, grid=(b, nc, SSM_GROUPS),
        in_specs=[x_spec, bc_spec, bc_spec, dt_spec, par_spec, par_spec, par_spec],
        out_specs=[x_spec, h_spec],
        out_shape=[jax.ShapeDtypeStruct(xs.shape, F32),
                   jax.ShapeDtypeStruct((b, SSM_HEADS, nc, SSM_P, D_STATE), F32)],
        scratch_shapes=[pltpu.VMEM((SSM_HEADS, SSM_P, D_STATE), F32)],
        compiler_params=_params(("arbitrary", "arbitrary", "arbitrary")),
    )(xs, bm, cm, dtr, dtb, alog, dsk)


def _unused_ssd_bwd(xs, bm, cm, dtr, dtb, alog, dsk, hprev, dy, *, name):
    b, _, s, _ = xs.shape
    nc = s // CHUNK
    x_spec, bc_spec, dt_spec, par_spec, h_spec = _ssd_specs(nc, True)
    dpar_spec = pl.BlockSpec((8, LANE), lambda bi, c, g: (0, 0))

    def body(x_ref, b_ref, c_ref, dtr_ref, dtb_ref, alog_ref, dsk_ref, hp_ref, dy_ref,
             dx_ref, db_ref, dc_ref, ddtr_ref, dpar_ref, dstate):
        bi, c, g = pl.program_id(0), pl.program_id(1), pl.program_id(2)

        @pl.when(c == 0)
        def _():
            dstate[pl.ds(g * HEADS_PER_GROUP, HEADS_PER_GROUP)] = jnp.zeros((HEADS_PER_GROUP, SSM_P, D_STATE), F32)

        @pl.when((bi == 0) & (c == 0) & (g == 0))
        def _():
            dpar_ref[...] = jnp.zeros_like(dpar_ref)

        z, dt, aneg, acs = _ssd_common(dtr_ref, dtb_ref, alog_ref)
        bv, cv = b_ref[0, 0], c_ref[0, 0]
        b16, c16 = bv.astype(BF16), cv.astype(BF16)
        cb = lax.dot_general(c16, b16, (((1,), (1,)), ((), ())), preferred_element_type=F32)
        tri_mask = _tri(True)
        last_row = (lax.broadcasted_iota(jnp.int32, (CHUNK, 1), 0) == CHUNK - 1).astype(F32)
        lanes = lax.broadcasted_iota(jnp.int32, (1, LANE), 1)
        dcb = jnp.zeros((CHUNK, CHUNK), F32)
        db_acc = jnp.zeros((CHUNK, D_STATE), F32)
        dc_acc = jnp.zeros((CHUNK, D_STATE), F32)
        ddt_mat = jnp.zeros((CHUNK, LANE), F32)
        dacs_mat = jnp.zeros((CHUNK, LANE), F32)
        ddsk_row = jnp.zeros((1, LANE), F32)
        for j in range(HEADS_PER_GROUP):
            hidx = g * HEADS_PER_GROUP + j
            onehot = (lanes == hidx).astype(F32)
            x = x_ref[0, j]
            dt_j, acs_j = _col(dt, onehot), _col(acs, onehot)
            acs_last, xg, lm, mm, decay_s = _ssd_head(x, dt_j, acs_j, cb, tri_mask, last_row)
            ea = jnp.exp(acs_j)
            cd = jnp.exp(acs_last)
            d_j = jnp.sum(dsk_ref[...] * onehot, axis=1, keepdims=True)
            hp = hp_ref[0, j, 0]
            hp16 = hp.astype(BF16)
            g_y = dy_ref[0, j]
            g_y16 = g_y.astype(BF16)
            g_hn = dstate[hidx]
            g_hn16 = g_hn.astype(BF16)
            xg16 = xg.astype(BF16)
            ddsk_row = ddsk_row + jnp.sum(jnp.sum(g_y * x, axis=1, keepdims=True), axis=0, keepdims=True) * onehot
            d_mm = lax.dot_general(g_y16, xg16, (((1,), (1,)), ((), ())), preferred_element_type=F32)
            d_xg = lax.dot_general(mm.astype(BF16), g_y16, (((0,), (0,)), ((), ())), preferred_element_type=F32)
            dcb = dcb + d_mm * lm
            d_dm = d_mm * mm
            d_acs = jnp.sum(d_dm, axis=1, keepdims=True) - jnp.sum(d_dm.T, axis=1, keepdims=True)
            t_off = lax.dot_general(c16, hp16, (((1,), (1,)), ((), ())), preferred_element_type=F32)
            d_t16 = (g_y * ea).astype(BF16)
            d_acs = d_acs + jnp.sum(g_y * t_off, axis=1, keepdims=True) * ea
            dc_acc = dc_acc + jnp.dot(d_t16, hp16, preferred_element_type=F32)
            d_hp = lax.dot_general(d_t16, c16, (((0,), (0,)), ((), ())), preferred_element_type=F32) + g_hn * cd
            d_last = jnp.sum(jnp.sum(g_hn * hp, axis=1, keepdims=True), axis=0, keepdims=True) * cd
            d_w = lax.dot_general(b16, g_hn16, (((1,), (1,)), ((), ())), preferred_element_type=F32)
            db_acc = db_acc + jnp.dot((xg * decay_s).astype(BF16), g_hn16, preferred_element_type=F32)
            d_xg = d_xg + d_w * decay_s
            d_ds = jnp.sum(d_w * xg, axis=1, keepdims=True) * decay_s
            d_last = d_last + jnp.sum(d_ds, axis=0, keepdims=True)
            d_acs = d_acs - d_ds + d_last * last_row
            dx_ref[0, j] = d_j * g_y + d_xg * dt_j
            ddt_mat = ddt_mat + jnp.sum(d_xg * x, axis=1, keepdims=True) * onehot
            dacs_mat = dacs_mat + d_acs * onehot
            dstate[hidx] = d_hp
        dcb16 = dcb.astype(BF16)
        dc_ref[0, 0] = dc_acc + jnp.dot(dcb16, b16, preferred_element_type=F32)
        db_ref[0, 0] = db_acc + lax.dot_general(dcb16, c16, (((0,), (0,)), ((), ())), preferred_element_type=F32)
        d_a = _dot01_left(_tri(False).astype(BF16), dacs_mat)
        ddt_mat = ddt_mat + d_a * aneg
        d_aneg = jnp.sum(d_a * dt, axis=0, keepdims=True)
        d_raw = ddt_mat * jax.nn.sigmoid(z)

        @pl.when(g == 0)
        def _():
            ddtr_ref[0] = d_raw

        @pl.when(g != 0)
        def _():
            ddtr_ref[0] += d_raw

        dpar_ref[0:1, :] += jnp.sum(d_raw, axis=0, keepdims=True)
        dpar_ref[1:2, :] += d_aneg * aneg
        dpar_ref[2:3, :] += ddsk_row

    return pl.pallas_call(
        body, name=name, grid=(b, nc, SSM_GROUPS),
        in_specs=[x_spec, bc_spec, bc_spec, dt_spec, par_spec, par_spec, par_spec, h_spec, x_spec],
        out_specs=[x_spec, bc_spec, bc_spec, dt_spec, dpar_spec],
        out_shape=[jax.ShapeDtypeStruct(xs.shape, F32), jax.ShapeDtypeStruct(bm.shape, F32),
                   jax.ShapeDtypeStruct(cm.shape, F32), jax.ShapeDtypeStruct(dtr.shape, F32),
                   jax.ShapeDtypeStruct((8, LANE), F32)],
        scratch_shapes=[pltpu.VMEM((SSM_HEADS, SSM_P, D_STATE), F32)],
        compiler_params=_params(("arbitrary", "arbitrary", "arbitrary")),
    )(xs, bm, cm, dtr, dtb, alog, dsk, hprev, dy)


SSD_INTERLEAVE = 8


def _each(f, *lists):
    return [f(*a) for a in zip(*lists)]


def _nt(a, b):
    return lax.dot_general(a, b, (((1,), (1,)), ((), ())), preferred_element_type=F32)


def _tn(a, b):
    return lax.dot_general(a, b, (((0,), (0,)), ((), ())), preferred_element_type=F32)


def _nn(a, b):
    return jnp.dot(a, b, preferred_element_type=F32)


def _rowsum(a):
    return jnp.sum(a, axis=1, keepdims=True)


def _colsum(a):
    return jnp.sum(a, axis=0, keepdims=True)


def _bf(a):
    return a.astype(BF16)


def _head_batches(g):
    first = g * HEADS_PER_GROUP
    return [list(range(first + k, first + k + SSD_INTERLEAVE)) for k in range(0, HEADS_PER_GROUP, SSD_INTERLEAVE)]


def _decay_matrix(acs_j, acs_row, tri_mask):
    dm = jnp.broadcast_to(acs_j, (CHUNK, CHUNK)) - jnp.broadcast_to(acs_row, (CHUNK, CHUNK))
    return jnp.where(tri_mask, jnp.exp(jnp.where(tri_mask, dm, 0.0)), 0.0)


def ssd_fwd(act3, proj3, dtb, alog, dsk, *, name, side=None):
    b, s, _ = act3.shape
    nc = s // CHUNK
    act_spec, y_spec, dt_in_spec, _, par_spec, h_spec = _ssd_specs(nc, False)

    def body(act_ref, dtr_ref, dtb_ref, alog_ref, dsk_ref, y_ref, hp_ref, state):
        c = pl.program_id(1)

        @pl.when(c == 0)
        def _():
            state[...] = jnp.zeros_like(state)

        _, dt, _, acs = _ssd_common(dtr_ref, dtb_ref, alog_ref)
        acs_t = acs.T
        tri_mask = _tri(True)
        last_row = (lax.broadcasted_iota(jnp.int32, (CHUNK, 1), 0) == CHUNK - 1).astype(F32)
        for g in range(SSM_GROUPS):
            b16 = _bf(act_ref[0, :, _group_cols(g, 0)])
            c16 = _bf(act_ref[0, :, _group_cols(g, 1)])
            cb = _nt(c16, b16)
            for hs in _head_batches(g):
                x = [act_ref[0, :, _head_cols(h)] for h in hs]
                dt_j = [dt[:, h:h + 1] for h in hs]
                acs_j = [acs[:, h:h + 1] for h in hs]
                acs_last = [_colsum(a * last_row) for a in acs_j]
                xg = _each(lambda xv, d: xv * d, x, dt_j)
                mm = [cb * _decay_matrix(a, acs_t[h:h + 1, :], tri_mask) for a, h in zip(acs_j, hs)]
                decay_s = _each(lambda al, a: jnp.exp(al - a), acs_last, acs_j)
                y_diag = _each(lambda m_, v: _nn(_bf(m_), _bf(v)), mm, xg)
                st = _each(lambda v, d: _tn(_bf(v * d), b16), xg, decay_s)
                hp = [state[h] for h in hs]
                for h, v in zip(hs, hp):
                    hp_ref[0, h, 0] = v
                y_off = [_nt(c16, _bf(v)) for v in hp]
                for h, yd, yo, a, xv in zip(hs, y_diag, y_off, acs_j, x):
                    y_ref[0, :, _head_cols(h)] = yd + yo * jnp.exp(a) + dsk_ref[:, h:h + 1] * xv
                for h, v, al, sv in zip(hs, hp, acs_last, st):
                    state[h] = v * jnp.exp(al) + sv

    call = SideCopy(side, n_in=5, n_out=2, grid=(b, nc))
    return pl.pallas_call(
        call.wrap(body), name=name, grid=(b, nc),
        in_specs=[act_spec, dt_in_spec, par_spec, par_spec, par_spec] + call.in_specs,
        out_specs=[y_spec, h_spec] + call.out_specs,
        out_shape=[jax.ShapeDtypeStruct((b, s, SSM_INNER), F32),
                   jax.ShapeDtypeStruct((b, SSM_HEADS, nc, SSM_P, D_STATE), F32)] + call.out_shape,
        scratch_shapes=[pltpu.VMEM((SSM_HEADS, SSM_P, D_STATE), F32)] + call.scratch,
        compiler_params=_params(("arbitrary", "arbitrary")),
    )(act3, proj3, dtb, alog, dsk, *call.args)


def ssd_bwd(act3, proj3, dtb, alog, dsk, hprev, dy3, *, name, side=None):
    b, s, _ = act3.shape
    nc = s // CHUNK
    act_spec, y_spec, dt_in_spec, dt_out_spec, par_spec, h_spec = _ssd_specs(nc, True)
    dpar_spec = pl.BlockSpec((8, LANE), lambda bi, c: (0, 0))

    def body(act_ref, dtr_ref, dtb_ref, alog_ref, dsk_ref, hp_ref, dy_ref, dact_ref, ddtr_ref, dpar_ref, dstate):
        bi, c = pl.program_id(0), pl.program_id(1)

        @pl.when(c == 0)
        def _():
            dstate[...] = jnp.zeros_like(dstate)

        @pl.when((bi == 0) & (c == 0))
        def _():
            dpar_ref[...] = jnp.zeros_like(dpar_ref)

        z, dt, aneg, acs = _ssd_common(dtr_ref, dtb_ref, alog_ref)
        acs_t = acs.T
        tri_mask = _tri(True)
        last_row = (lax.broadcasted_iota(jnp.int32, (CHUNK, 1), 0) == CHUNK - 1).astype(F32)
        lanes = lax.broadcasted_iota(jnp.int32, (1, LANE), 1)
        sublanes = lax.broadcasted_iota(jnp.int32, (LANE, 1), 0)
        ddt_mat = jnp.zeros((CHUNK, LANE), F32)
        dacs_mat = jnp.zeros((CHUNK, LANE), F32)
        dacs_rows = jnp.zeros((LANE, CHUNK), F32)
        ddsk_row = jnp.zeros((1, LANE), F32)
        for g in range(SSM_GROUPS):
            b16 = _bf(act_ref[0, :, _group_cols(g, 0)])
            c16 = _bf(act_ref[0, :, _group_cols(g, 1)])
            cb = _nt(c16, b16)
            dcb = jnp.zeros((CHUNK, CHUNK), F32)
            db_acc = jnp.zeros((CHUNK, D_STATE), F32)
            dc_acc = jnp.zeros((CHUNK, D_STATE), F32)
            for hs in _head_batches(g):
                x = [act_ref[0, :, _head_cols(h)] for h in hs]
                g_y = [dy_ref[0, :, _head_cols(h)] for h in hs]
                hp = [hp_ref[0, h, 0] for h in hs]
                g_hn = [dstate[h] for h in hs]
                dt_j = [dt[:, h:h + 1] for h in hs]
                acs_j = [acs[:, h:h + 1] for h in hs]
                acs_last = [_colsum(a * last_row) for a in acs_j]
                xg = _each(lambda xv, d: xv * d, x, dt_j)
                lm = [_decay_matrix(a, acs_t[h:h + 1, :], tri_mask) for a, h in zip(acs_j, hs)]
                mm = [cb * l for l in lm]
                decay_s = _each(lambda al, a: jnp.exp(al - a), acs_last, acs_j)
                ea = [jnp.exp(a) for a in acs_j]
                cd = [jnp.exp(al) for al in acs_last]
                g_y16, xg16, hp16, g_hn16 = [[_bf(v) for v in vs] for vs in (g_y, xg, hp, g_hn)]
                d_mm = _each(_nt, g_y16, xg16)
                d_xg = _each(lambda m_, gy: _tn(_bf(m_), gy), mm, g_y16)
                d_dm = _each(lambda a, m_: a * m_, d_mm, mm)
                d_acs = [_rowsum(v) for v in d_dm]
                t_off = [_nt(c16, v) for v in hp16]
                d_t16 = _each(lambda gy, e: _bf(gy * e), g_y, ea)
                d_acs = _each(lambda da, gy, t, e: da + _rowsum(gy * t) * e, d_acs, g_y, t_off, ea)
                d_hp = _each(lambda dtv, gh, cdv: _tn(dtv, c16) + gh * cdv, d_t16, g_hn, cd)
                d_w = [_nt(b16, v) for v in g_hn16]
                d_xg = _each(lambda dx, dw, ds: dx + dw * ds, d_xg, d_w, decay_s)
                d_ds = _each(lambda dw, v, ds: _rowsum(dw * v) * ds, d_w, xg, decay_s)
                d_last = _each(lambda gh, hv, cdv, dd: _colsum(_rowsum(gh * hv)) * cdv + _colsum(dd), g_hn, hp, cd, d_ds)
                d_acs = _each(lambda da, dd, dl: da - dd + dl * last_row, d_acs, d_ds, d_last)
                for h, gy, dx, d, xv in zip(hs, g_y, d_xg, dt_j, x):
                    dact_ref[0, :, _head_cols(h)] = dsk_ref[:, h:h + 1] * gy + dx * d
                for h, v in zip(hs, d_hp):
                    dstate[h] = v
                for k, h in enumerate(hs):
                    onehot = (lanes == h).astype(F32)
                    dcb = dcb + d_mm[k] * lm[k]
                    dc_acc = dc_acc + _nn(d_t16[k], hp16[k])
                    db_acc = db_acc + _nn(_bf(xg[k] * decay_s[k]), g_hn16[k])
                    ddsk_row = ddsk_row + _colsum(_rowsum(g_y[k] * x[k])) * onehot
                    ddt_mat = ddt_mat + _rowsum(d_xg[k] * x[k]) * onehot
                    dacs_mat = dacs_mat + d_acs[k] * onehot
                    dacs_rows = dacs_rows + (sublanes == h).astype(F32) * _colsum(d_dm[k])
            dcb16 = _bf(dcb)
            dact_ref[0, :, _group_cols(g, 1)] = dc_acc + _nn(dcb16, b16)
            dact_ref[0, :, _group_cols(g, 0)] = db_acc + _tn(dcb16, c16)
        d_a = _dot01_left(_tri(False).astype(BF16), dacs_mat - dacs_rows.T)
        ddt_mat = ddt_mat + d_a * aneg
        d_raw = ddt_mat * jax.nn.sigmoid(z)
        ddtr_ref[0] = d_raw
        dpar_ref[0:1, :] += _colsum(d_raw)
        dpar_ref[1:2, :] += _colsum(d_a * dt) * aneg
        dpar_ref[2:3, :] += ddsk_row

    call = SideCopy(side, n_in=7, n_out=3, grid=(b, nc))
    return pl.pallas_call(
        call.wrap(body), name=name, grid=(b, nc),
        in_specs=[act_spec, dt_in_spec, par_spec, par_spec, par_spec, h_spec, y_spec] + call.in_specs,
        out_specs=[act_spec, dt_out_spec, dpar_spec] + call.out_specs,
        out_shape=[jax.ShapeDtypeStruct(act3.shape, F32), jax.ShapeDtypeStruct((b, s, LANE), F32),
                   jax.ShapeDtypeStruct((8, LANE), F32)] + call.out_shape,
        scratch_shapes=[pltpu.VMEM((SSM_HEADS, SSM_P, D_STATE), F32)] + call.scratch,
        compiler_params=_params(("arbitrary", "arbitrary")),
    )(act3, proj3, dtb, alog, dsk, hprev, dy3, *call.args)


def to_heads(x, b, s, h):
    return x.reshape(b, s, h, -1).transpose(0, 2, 1, 3)


def from_heads(x):
    b, h, s, c = x.shape
    return x.transpose(0, 2, 1, 3).reshape(b * s, h * c)


def dilate_q(q, d):
    b, _, s, c = q.shape
    x = q.reshape(b, N_KV_HEADS, GQA, s // d, d, c).transpose(0, 1, 4, 2, 3, 5)
    return x.reshape(b * N_KV_HEADS * d, GQA, s // d, c)


def undilate_q(x, b, d):
    _, _, l, c = x.shape
    y = x.reshape(b, N_KV_HEADS, d, GQA, l, c).transpose(0, 1, 3, 4, 2, 5)
    return y.reshape(b, N_Q_HEADS, l * d, c)


def dilate_kv(k, d):
    b, h, s, c = k.shape
    return k.reshape(b, h, s // d, d, c).transpose(0, 1, 3, 2, 4).reshape(b * h * d, s // d, c)


def undilate_kv(x, b, d):
    _, l, c = x.shape
    return x.reshape(b, N_KV_HEADS, d, l, c).transpose(0, 1, 3, 2, 4).reshape(b, N_KV_HEADS, l * d, c)


def rotary_tables(positions):
    inv_freq = ROPE_THETA ** (-jnp.arange(0, ROPE_DIM, 2, dtype=F32) / ROPE_DIM)
    ang = positions.astype(F32)[..., None] * inv_freq
    cos, sin = jnp.cos(ang), jnp.sin(ang)
    rest = HEAD_DIM - ROPE_DIM
    cosf = jnp.concatenate([cos, cos, jnp.ones(cos.shape[:2] + (rest,), F32)], axis=-1)
    sinf = jnp.concatenate([-sin, sin, jnp.zeros(sin.shape[:2] + (rest,), F32)], axis=-1)
    return cosf, sinf


def w_in_columns(w):
    pad = jnp.zeros((w.shape[0], IN_PAD - IN_PROJ), w.dtype)
    return jnp.concatenate([w[:, :Q_END], w[:, V_END:XBC_END], w[:, Q_END:V_END], w[:, XBC_END:], pad], axis=1)


def w_in_grad_columns(g):
    return jnp.concatenate([g[:, :Z_COL], g[:, K_COL:DT_COL], g[:, Z_COL:K_COL], g[:, DT_COL:DT_COL + SSM_HEADS]], axis=1)


def lane_pad(v):
    return jnp.pad(v.reshape(1, -1), ((0, 0), (0, LANE - v.shape[-1])))


def layer_fwd(h, wts, small, rope_tab, b, s, tag, attn_side=None, rest_from=None, ssd_side=None):
    w_in = wts[0]
    t = b * s
    sv = {"h": h}
    hn = rowwise_fwd(rms_fn, [h], [small["norm_mix"]], [BF16], name=f"rms_mix_{tag}")[0]
    proj = matmul(hn, w_in, name=f"in_proj_{tag}")
    sv["hn"], sv["proj"] = hn, proj
    proj3 = proj.reshape(b, s, IN_PAD)
    attn3, lse3, *attn_out = attn_fwd(proj3, rope_tab, name=f"attn_{tag}", side=attn_side)
    if rest_from is not None:
        wts = (w_in,) + tuple(rest_from(attn_out[0]))
    _, w_out, w_gate, w_up, w_down = wts
    sv["attn3"], sv["lse3"] = attn3, lse3
    attn = attn3.reshape(t, ATTN_WIDTH)
    act3 = conv_silu_fwd(proj3, small["conv_w"], small["conv_b"], name=f"conv_{tag}")
    y3, hprev, *ssd_out = ssd_fwd(act3, proj3, small["dt_bias"], small["a_log"], small["d_skip"], name=f"ssd_{tag}",
                                  side=ssd_side)
    y = y3.reshape(t, SSM_INNER)
    sv["act3"], sv["hprev"], sv["y"] = act3, hprev, y
    gn = rowwise_fwd(gated_norm_fn, [y, proj], [small["ssm_norm"]], [F32], name=f"gated_norm_{tag}", groups=SSM_GROUPS,
                     windows=[None, (Z_COL, SSM_INNER)])[0]
    sv["gn"] = gn
    h1 = matmul([attn, gn], w_out, name=f"out_proj_{tag}", residual=h)
    sv["h1"] = h1
    hn2 = rowwise_fwd(rms_fn, [h1], [small["norm_ffn"]], [BF16], name=f"rms_ffn_{tag}")[0]
    gate = matmul(hn2, w_gate, out_dtype=BF16, name=f"ffn_gate_{tag}")
    up = matmul(hn2, w_up, out_dtype=BF16, name=f"ffn_up_{tag}")
    act2 = rowwise_fwd(swiglu_fn, [gate, up], [], [BF16], name=f"swiglu_{tag}")[0]
    sv["hn2"], sv["gate"], sv["up"], sv["act2"] = hn2, gate, up, act2
    h2 = matmul(act2, w_down, name=f"ffn_down_{tag}", residual=h1)
    return h2, sv, wts, (ssd_out[0] if ssd_out else None)


def layer_bwd(dh2, sv, wts, small, rope_tab, b, s, tag, ssd_side=None, attn_side_fn=None):
    w_in, w_out, w_gate, w_up, w_down = wts
    t = b * s
    gr = {}
    d_act2 = matmul(dh2, w_down, tb=True, out_dtype=BF16, name=f"ffn_down_dx_{tag}")
    gr["w_down"] = matmul(sv["act2"], dh2, ta=True, out_dtype=BF16, name=f"ffn_down_dw_{tag}")
    d_gate, d_up = rowwise_bwd(swiglu_fn, [sv["gate"], sv["up"]], [], [d_act2], [BF16, BF16], name=f"swiglu_bwd_{tag}")
    gr["w_gate"] = matmul(sv["hn2"], d_gate, ta=True, out_dtype=BF16, name=f"ffn_gate_dw_{tag}")
    gr["w_up"] = matmul(sv["hn2"], d_up, ta=True, out_dtype=BF16, name=f"ffn_up_dw_{tag}")
    d_hn2 = matmul(d_gate, w_gate, tb=True, name=f"ffn_gate_dx_{tag}")
    d_hn2 = matmul(d_up, w_up, tb=True, residual=d_hn2, name=f"ffn_up_dx_{tag}")
    dh1, gr["norm_ffn"] = rowwise_bwd(rms_fn, [sv["h1"]], [small["norm_ffn"]], [d_hn2], [F32],
                                      name=f"rms_ffn_bwd_{tag}", add_to_first=dh2)
    d_cat = matmul(dh1, w_out, tb=True, name=f"out_proj_dx_{tag}")
    gr["w_out"] = jnp.concatenate([
        matmul(sv["attn3"].reshape(t, ATTN_WIDTH), dh1, ta=True, out_dtype=BF16, name=f"out_proj_dw_attn_{tag}"),
        matmul(sv["gn"], dh1, ta=True, out_dtype=BF16, name=f"out_proj_dw_ssd_{tag}")], axis=0)
    d_y, d_z, gr["ssm_norm"] = rowwise_bwd(gated_norm_fn, [sv["y"], sv["proj"]], [small["ssm_norm"]], [d_cat], [F32, F32],
                                           name=f"gated_norm_bwd_{tag}", groups=SSM_GROUPS,
                                           windows=[None, (Z_COL, SSM_INNER)], ct_windows=[(ATTN_WIDTH, SSM_INNER)])
    proj3 = sv["proj"].reshape(b, s, IN_PAD)
    d_act3, d_dtr, d_par, *ssd_out = ssd_bwd(sv["act3"], proj3, small["dt_bias"], small["a_log"], small["d_skip"],
                                             sv["hprev"], d_y.reshape(b, s, SSM_INNER), name=f"ssd_bwd_{tag}", side=ssd_side)
    gr["dt_bias"], gr["a_log"], gr["d_skip"] = d_par[0, :SSM_HEADS], d_par[1, :SSM_HEADS], d_par[2, :SSM_HEADS]
    d_xbc, gr["conv_w"], gr["conv_b"] = conv_silu_bwd(proj3, small["conv_w"], small["conv_b"], d_act3,
                                                      name=f"conv_bwd_{tag}")
    attn_side = attn_side_fn(gr) if attn_side_fn is not None else None
    d_q3, d_k4, d_v4, *attn_out = attn_bwd(proj3, rope_tab, sv["attn3"], sv["lse3"], d_cat.reshape(b, s, MIX_WIDTH),
                                           name=f"attn_bwd_{tag}", side=attn_side)
    d_tail = jnp.concatenate([from_heads(d_k4), from_heads(d_v4), d_dtr.reshape(t, LANE)], axis=1)
    d_proj = [d_q3.reshape(t, ATTN_WIDTH), d_z, d_xbc.reshape(t, CONV_CH), d_tail]
    d_hn = matmul(d_proj, w_in, tb=True, name=f"in_proj_dx_{tag}")
    gr["w_in"] = w_in_grad_columns(jnp.concatenate(
        [matmul(sv["hn"], part, ta=True, out_dtype=BF16, name=f"in_proj_dw_{k}_{tag}") for k, part in enumerate(d_proj)],
        axis=1))
    dh, gr["norm_mix"] = rowwise_bwd(rms_fn, [sv["h"]], [small["norm_mix"]], [d_hn], [F32],
                                     name=f"rms_mix_bwd_{tag}", add_to_first=dh1)
    return dh, gr, (ssd_out[0] if ssd_out else None), (attn_out[0] if attn_out else None)


def local_step(x, positions, big, small_all, final_norm, loss_target, *, plan=None):
    b, s, _ = x.shape
    t = b * s
    rope_tab = jnp.concatenate(rotary_tables(positions), axis=-1)
    h = x.reshape(t, D_MODEL)
    saved, big = [], list(big)
    for l in range(DEPTH):
        kw = {}
        if plan is not None and l == 0:
            kw = dict(attn_side=(plan["rest0"], False), rest_from=plan["make_rest0"], ssd_side=(plan["late"], False))
        h, sv, big[l], got = layer_fwd(h, big[l], small_all[l], rope_tab, b, s, f"l{l}", **kw)
        if got is not None:
            big[DEPTH - 1] = plan["make_late"](got)
        saved.append(sv)
    dh, d_final, loss = loss_and_grad(h, loss_target.reshape(t, D_MODEL), final_norm.reshape(1, D_MODEL))
    grads, received = [None] * DEPTH, {}
    for l in reversed(range(DEPTH)):
        kw = {}
        if plan is not None and l == 0:
            kw = dict(ssd_side=(plan["grads_late"](grads[DEPTH - 1]), True),
                      attn_side_fn=lambda gr: (plan["grads_rest0"](gr), True))
        dh, grads[l], got_ssd, got_attn = layer_bwd(dh, saved[l], big[l], small_all[l], rope_tab, b, s, f"l{l}", **kw)
        if got_ssd is not None:
            received["late"] = got_ssd
        if got_attn is not None:
            received["rest0"] = got_attn
    return loss, dh.reshape(b, s, D_MODEL), grads, d_final, received


def _slab_rows(r):
    return r if r <= 512 else _pick(r, (512, 256, 128, 8))


def cast_bf16(x, *, name):
    def fn(v):
        return (v,)
    return rowwise_fwd(fn, [x], [], [BF16], name=name, tr=_slab_rows(x.shape[0]))[0]


def sum_slots(x, *, name):
    n, r, c = x.shape
    tr = _slab_rows(r)

    def body(x_ref, o_ref):
        acc = x_ref[0].astype(F32)
        for i in range(1, n):
            acc = acc + x_ref[i].astype(F32)
        o_ref[...] = acc

    return pl.pallas_call(
        body, name=name, grid=(r // tr,), in_specs=[pl.BlockSpec((n, tr, c), lambda i: (0, i, 0))],
        out_specs=pl.BlockSpec((tr, c), lambda i: (i, 0)), out_shape=jax.ShapeDtypeStruct((r, c), F32),
        compiler_params=_params(("parallel",)),
    )(x)


def adamw(g_parts, w, m, v, *, name, with_grad=True):
    r, c = w.shape
    tr = _slab_rows(r)
    n_g = len(g_parts)
    n_out = 4 if with_grad else 3
    bc1 = 1.0 / (1.0 - ADAM_B1 ** ADAM_STEP)
    bc2 = 1.0 / (1.0 - ADAM_B2 ** ADAM_STEP)

    def body(*refs):
        g = refs[0][...]
        for r_ in refs[1:n_g]:
            g = g + r_[...]
        w_ref, m_ref, v_ref = refs[n_g:n_g + 3]
        d_out, m_out, v_out = refs[-3:]
        m_new = ADAM_B1 * m_ref[...] + (1.0 - ADAM_B1) * g
        v_new = ADAM_B2 * v_ref[...] + (1.0 - ADAM_B2) * (g * g)
        if with_grad:
            refs[n_g + 3][...] = g
        m_out[...] = m_new
        v_out[...] = v_new
        d_out[...] = -ADAM_LR * ((m_new * bc1) / (jnp.sqrt(v_new * bc2) + ADAM_EPS) + ADAM_WD * w_ref[...])

    spec = pl.BlockSpec((tr, c), lambda i: (i, 0))
    return pl.pallas_call(
        body, name=name, grid=(r // tr,), in_specs=[spec] * (n_g + 3), out_specs=[spec] * n_out,
        out_shape=[jax.ShapeDtypeStruct((r, c), F32)] * n_out, compiler_params=_params(("parallel",)),
    )(*g_parts, w, m, v)


def _other_chips(x, y):
    return [(1 - x, y), (x, 1 - y), (1 - x, 1 - y)]


def allgather_chips(shards):
    n_arr = len(shards)

    def body(*refs):
        in_refs, out_refs = refs[:n_arr], refs[n_arr:2 * n_arr]
        send_sems, recv_sems, local_sems = refs[2 * n_arr:]
        x, y, c = lax.axis_index("x"), lax.axis_index("y"), lax.axis_index("c")
        chip = 2 * x + y
        started = []
        for a, (in_ref, out_ref) in enumerate(zip(in_refs, out_refs)):
            mine = pltpu.make_async_copy(in_ref, out_ref.at[chip], local_sems.at[a])
            mine.start()
            started.append(mine.wait)
            for k, (px, py) in enumerate(_other_chips(x, y)):
                cp = pltpu.make_async_remote_copy(src_ref=in_ref, dst_ref=out_ref.at[chip], send_sem=send_sems.at[3 * a + k],
                                                  recv_sem=recv_sems.at[3 * a + k], device_id=(px, py, c), device_id_type=MESH)
                cp.start()
                started.append(cp.wait_send)
        for a, (in_ref, out_ref) in enumerate(zip(in_refs, out_refs)):
            for k, (px, py) in enumerate(_other_chips(x, y)):
                pltpu.make_async_remote_copy(src_ref=in_ref, dst_ref=out_ref.at[2 * px + py], send_sem=send_sems.at[3 * a + k],
                                             recv_sem=recv_sems.at[3 * a + k], device_id=(px, py, c),
                                             device_id_type=MESH).wait_recv()
        for wait in started:
            wait()

    hbm = pl.BlockSpec(memory_space=pltpu.HBM)
    return pl.pallas_call(
        body, name="allgather_weights", in_specs=[hbm] * n_arr, out_specs=[hbm] * n_arr,
        out_shape=[jax.ShapeDtypeStruct((N_CHIPS,) + s.shape, s.dtype) for s in shards],
        scratch_shapes=[pltpu.SemaphoreType.DMA((3 * n_arr,)), pltpu.SemaphoreType.DMA((3 * n_arr,)),
                        pltpu.SemaphoreType.DMA((n_arr,))],
    )(*shards)


def exchange_grads(big, small):
    def body(big_ref, small_ref, big_out, small_out, send_sems, recv_sems, local_sems):
        x, y, c = lax.axis_index("x"), lax.axis_index("y"), lax.axis_index("c")
        chip = 2 * x + y
        dev = 4 * x + 2 * y + c
        own_big = pltpu.make_async_copy(big_ref.at[chip], big_out.at[chip], local_sems.at[0])
        own_small = pltpu.make_async_copy(small_ref, small_out.at[dev], local_sems.at[1])
        own_big.start()
        own_small.start()
        sends = []
        for k, (px, py) in enumerate(_other_chips(x, y)):
            cp = pltpu.make_async_remote_copy(src_ref=big_ref.at[2 * px + py], dst_ref=big_out.at[chip],
                                              send_sem=send_sems.at[k], recv_sem=recv_sems.at[k],
                                              device_id=(px, py, c), device_id_type=MESH)
            cp.start()
            sends.append(cp)
        peers = []
        for r in range(1, N_DEV):
            fx, fy, fc = (r >> 2) & 1, (r >> 1) & 1, r & 1
            px, py, pc = (x + fx) % 2, (y + fy) % 2, (c + fc) % 2
            peers.append((px, py, pc))
            cp = pltpu.make_async_remote_copy(src_ref=small_ref, dst_ref=small_out.at[dev], send_sem=send_sems.at[2 + r],
                                              recv_sem=recv_sems.at[2 + r], device_id=(px, py, pc), device_id_type=MESH)
            cp.start()
            sends.append(cp)
        for k, (px, py) in enumerate(_other_chips(x, y)):
            pltpu.make_async_remote_copy(src_ref=big_ref.at[chip], dst_ref=big_out.at[2 * px + py],
                                         send_sem=send_sems.at[k], recv_sem=recv_sems.at[k],
                                         device_id=(px, py, c), device_id_type=MESH).wait_recv()
        for r, (px, py, pc) in zip(range(1, N_DEV), peers):
            pltpu.make_async_remote_copy(src_ref=small_ref, dst_ref=small_out.at[4 * px + 2 * py + pc],
                                         send_sem=send_sems.at[2 + r], recv_sem=recv_sems.at[2 + r],
                                         device_id=(px, py, pc), device_id_type=MESH).wait_recv()
        for cp in sends:
            cp.wait_send()
        own_big.wait()
        own_small.wait()

    hbm = pl.BlockSpec(memory_space=pltpu.HBM)
    n_sem = 3 + N_DEV - 1
    return pl.pallas_call(
        body, name="exchange_grads", in_specs=[hbm, hbm], out_specs=[hbm, hbm],
        out_shape=[jax.ShapeDtypeStruct(big.shape, big.dtype), jax.ShapeDtypeStruct((N_DEV,) + small.shape, small.dtype)],
        scratch_shapes=[pltpu.SemaphoreType.DMA((n_sem,)), pltpu.SemaphoreType.DMA((n_sem,)), pltpu.SemaphoreType.DMA((2,))],
    )(big, small)


SWAP_CHUNKS = 28


def swap_cores(mine):
    rows = mine.shape[0] // SWAP_CHUNKS
    assert rows * SWAP_CHUNKS == mine.shape[0] and rows % 8 == 0

    def body(in_ref, out_ref, send_sems, recv_sems):
        x, y, c = lax.axis_index("x"), lax.axis_index("y"), lax.axis_index("c")

        def chunk(k):
            part = pl.ds(k * rows, rows)
            return pltpu.make_async_remote_copy(src_ref=in_ref.at[part], dst_ref=out_ref.at[part],
                                                send_sem=send_sems.at[k], recv_sem=recv_sems.at[k],
                                                device_id=(x, y, 1 - c), device_id_type=MESH)

        for k in range(SWAP_CHUNKS):
            chunk(k).start()
        for k in range(SWAP_CHUNKS):
            chunk(k).wait_recv()
        for k in range(SWAP_CHUNKS):
            chunk(k).wait_send()

    hbm = pl.BlockSpec(memory_space=pltpu.HBM)
    return pl.pallas_call(
        body, name="swap_cores", in_specs=[hbm], out_specs=hbm,
        out_shape=jax.ShapeDtypeStruct(mine.shape, mine.dtype),
        scratch_shapes=[pltpu.SemaphoreType.DMA((SWAP_CHUNKS,)), pltpu.SemaphoreType.DMA((SWAP_CHUNKS,))],
    )(mine)


BIG_NAMES = ("w_in", "w_out", "w_gate", "w_up", "w_down")
BIG_SHARD_AXIS = {"w_in": 1, "w_out": 0, "w_gate": 1, "w_up": 1, "w_down": 0}
PACK_COLS = 1024
SMALL_NAMES = ("norm_mix", "conv_w", "conv_b", "dt_bias", "a_log", "d_skip", "ssm_norm", "norm_ffn")


PACK_ROW_TILE = 256


def pack_big(shards, names=BIG_NAMES):
    flat = jnp.concatenate([shards[n].reshape(-1) for n in names])
    unit = PACK_ROW_TILE * PACK_COLS
    total = -(-flat.size // unit) * unit
    return jnp.pad(flat, (0, total - flat.size)).reshape(-1, PACK_COLS)


def unpack_big(packed, like, names=BIG_NAMES):
    out, off = {}, 0
    flat = packed.reshape(-1)
    for n in names:
        size = like[n].size
        out[n] = flat[off:off + size].reshape(like[n].shape)
        off += size
    return out


def pack_small(parts):
    flat = jnp.concatenate([p.reshape(-1).astype(F32) for p in parts])
    rows = -(-flat.size // LANE)
    rows = -(-rows // 8) * 8
    return jnp.pad(flat, (0, rows * LANE - flat.size)).reshape(rows, LANE)


def unpack_small(packed, like):
    out, off = [], 0
    flat = packed.reshape(-1)
    for a in like:
        out.append(flat[off:off + a.size].reshape(a.shape))
        off += a.size
    return out


def kernel(x, positions, norm_mix, w_in, conv_w, conv_b, dt_bias, a_log, d_skip, ssm_norm, w_out, norm_ffn, w_gate, w_up, w_down, final_norm, loss_target, m_norm_mix, m_w_in, m_conv_w, m_conv_b, m_dt_bias, m_a_log, m_d_skip, m_ssm_norm, m_w_out, m_norm_ffn, m_w_gate, m_w_up, m_w_down, m_final_norm, v_norm_mix, v_w_in, v_conv_w, v_conv_b, v_dt_bias, v_a_log, v_d_skip, v_ssm_norm, v_w_out, v_norm_ffn, v_w_gate, v_w_up, v_w_down, v_final_norm):
    chip = 2 * lax.axis_index("x") + lax.axis_index("y")
    w_sh = {"w_in": w_in, "w_out": w_out, "w_gate": w_gate, "w_up": w_up, "w_down": w_down}
    m_sh = {"w_in": m_w_in, "w_out": m_w_out, "w_gate": m_w_gate, "w_up": m_w_up, "w_down": m_w_down}
    v_sh = {"w_in": v_w_in, "w_out": v_w_out, "w_gate": v_w_gate, "w_up": v_w_up, "w_down": v_w_down}

    assert DEPTH == 2
    first, rest = BIG_NAMES[:1], BIG_NAMES[1:]
    layer_of = lambda d, l: {n: d[n][l] for n in BIG_NAMES}
    pack_layer = lambda d: jnp.concatenate([pack_big(d, first), pack_big(d, rest)])
    pack_layers = lambda d: jnp.concatenate([pack_layer(layer_of(d, l)) for l in range(DEPTH)])
    first_rows = pack_big(layer_of(w_sh, 0), first).shape[0]
    layer_rows = pack_layer(layer_of(w_sh, 0)).shape[0]

    def unpack_layer(packed, l):
        like = layer_of(w_sh, l)
        return {**unpack_big(packed[:first_rows], like, first), **unpack_big(packed[first_rows:], like, rest)}

    def unpack_layers(packed):
        per_layer = [unpack_layer(packed[l * layer_rows:(l + 1) * layer_rows], l) for l in range(DEPTH)]
        return {n: jnp.stack([p[n] for p in per_layer]) for n in BIG_NAMES}

    def full_weights(gathered, l, names, unpack):
        pieces = [unpack(gathered[j]) for j in range(N_CHIPS)]
        full = {n: jnp.concatenate([p[n] for p in pieces], axis=BIG_SHARD_AXIS[n]) for n in names}
        return tuple(w_in_columns(full[n]) if n == "w_in" else full[n] for n in names)

    w_packed16 = pack_layers({n: cast_bf16(w_sh[n].reshape(-1, w_sh[n].shape[-1]), name=f"cast_{n}").reshape(w_sh[n].shape)
                              for n in BIG_NAMES})
    conv_cols = CONV_CH // N_CHIPS
    gathered_in0, conv_g = allgather_chips([w_packed16[:first_rows], conv_w.reshape(-1, LANE)])
    big = [full_weights(gathered_in0, 0, first, lambda p: unpack_big(p, layer_of(w_sh, 0), first)) + (None,) * len(rest), None]
    plan = {
        "rest0": w_packed16[first_rows:layer_rows],
        "make_rest0": lambda g: full_weights(g, 0, rest, lambda p: unpack_big(p, layer_of(w_sh, 0), rest)),
        "late": w_packed16[layer_rows:],
        "make_late": lambda g: full_weights(g, DEPTH - 1, BIG_NAMES, lambda p: unpack_layer(p, DEPTH - 1)),
    }
    conv_w_full = jnp.concatenate([conv_g[j].reshape(DEPTH, CONV_WIDTH, conv_cols) for j in range(N_CHIPS)], axis=2)

    small_all = []
    for l in range(DEPTH):
        small_all.append({
            "norm_mix": norm_mix[l].reshape(1, -1), "conv_w": conv_w_full[l], "conv_b": conv_b[l].reshape(1, -1),
            "dt_bias": lane_pad(dt_bias[l]), "a_log": lane_pad(a_log[l]), "d_skip": lane_pad(d_skip[l]),
            "ssm_norm": ssm_norm[l].reshape(1, -1), "norm_ffn": norm_ffn[l].reshape(1, -1)})

    def shard_of(name, g, j):
        n = g.shape[BIG_SHARD_AXIS[name]] // N_CHIPS
        return lax.slice_in_dim(g, j * n, (j + 1) * n, axis=BIG_SHARD_AXIS[name])

    def per_chip(layer_grads, names):
        packs = [[pack_big({n: shard_of(n, layer_grads[n], j) for n in group}, group) for j in range(N_CHIPS)]
                 for group in ((first, rest) if names == BIG_NAMES else (names,))]
        return jnp.stack([jnp.concatenate([p[j] for p in packs]) for j in range(N_CHIPS)])

    plan["grads_late"] = lambda gr: per_chip(gr, BIG_NAMES)
    plan["grads_rest0"] = lambda gr: per_chip(gr, rest)
    loss_part, grad_x, grads, d_final, received = local_step(x, positions, big, small_all, final_norm, loss_target, plan=plan)

    small_parts = [jnp.stack([grads[l][n].reshape(-1) for l in range(DEPTH)]) for n in SMALL_NAMES]
    small_parts += [d_final.reshape(-1), loss_part.reshape(-1)]
    recv_first, recv_small = exchange_grads(per_chip(grads[0], first), pack_small(small_parts))
    plane_sum = jnp.concatenate([sum_slots(recv_first, name="sum_chip_partials_in0"),
                                 sum_slots(received["rest0"], name="sum_chip_partials_rest0"),
                                 sum_slots(received["late"], name="sum_chip_partials_l1")])
    other_plane = swap_cores(plane_sum)

    g_packed = rowwise_fwd(lambda p, q: (p + q,), [plane_sum, other_plane], [], [F32], name="sum_planes")[0]
    g_big = unpack_layers(g_packed)
    d_big, m_big, v_big = {}, {}, {}
    for n in BIG_NAMES:
        flat = lambda a: a.reshape(-1, a.shape[-1])
        res = adamw([flat(g_big[n])], flat(w_sh[n]), flat(m_sh[n]), flat(v_sh[n]), name=f"adamw_{n}", with_grad=False)
        d_big[n], m_big[n], v_big[n] = (a.reshape(w_sh[n].shape) for a in res)

    small_sum = sum_slots(recv_small, name="sum_small")
    like = [norm_mix, conv_w_full, conv_b, dt_bias, a_log, d_skip, ssm_norm, norm_ffn, final_norm, loss_part.reshape(-1)]
    g_small = unpack_small(small_sum, like)
    loss = g_small[-1][0]
    g_small = dict(zip(SMALL_NAMES + ("final_norm",), g_small[:-1]))
    g_small["conv_w"] = lax.dynamic_slice_in_dim(g_small["conv_w"], chip * conv_cols, conv_cols, axis=2)
    w_small = {"norm_mix": norm_mix, "conv_w": conv_w, "conv_b": conv_b, "dt_bias": dt_bias, "a_log": a_log, "d_skip": d_skip,
               "ssm_norm": ssm_norm, "norm_ffn": norm_ffn, "final_norm": final_norm}
    m_small = {"norm_mix": m_norm_mix, "conv_w": m_conv_w, "conv_b": m_conv_b, "dt_bias": m_dt_bias, "a_log": m_a_log,
               "d_skip": m_d_skip, "ssm_norm": m_ssm_norm, "norm_ffn": m_norm_ffn, "final_norm": m_final_norm}
    v_small = {"norm_mix": v_norm_mix, "conv_w": v_conv_w, "conv_b": v_conv_b, "dt_bias": v_dt_bias, "a_log": v_a_log,
               "d_skip": v_d_skip, "ssm_norm": v_ssm_norm, "norm_ffn": v_norm_ffn, "final_norm": v_final_norm}
    names = SMALL_NAMES + ("final_norm",)
    order = [w_small[n] for n in names]
    res = adamw([pack_small([g_small[n] for n in names])], pack_small(order), pack_small([m_small[n] for n in names]),
                pack_small([v_small[n] for n in names]), name="adamw_small")
    g_s, d_s, m_s, v_s = (dict(zip(names, unpack_small(a, order))) for a in res)

    all_names = ("norm_mix", "w_in", "conv_w", "conv_b", "dt_bias", "a_log", "d_skip", "ssm_norm", "w_out", "norm_ffn",
                 "w_gate", "w_up", "w_down", "final_norm")
    outs = [loss, grad_x]
    for src_big, src_small in ((g_big, g_s), (d_big, d_s), (m_big, m_s), (v_big, v_s)):
        outs += [src_big[n] if n in BIG_NAMES else src_small[n] for n in all_names]
    return tuple(outs)
```

```python
import functools

import jax
import jax.numpy as jnp
from jax import lax
from jax.experimental import pallas as pl
from jax.experimental.pallas import tpu as pltpu

F32 = jnp.float32
BF16 = jnp.bfloat16
MESH = pl.DeviceIdType.MESH

D_MODEL = 1024
DEPTH = 2
HEAD_DIM = 64
N_Q_HEADS = 8
N_KV_HEADS = 2
GQA = N_Q_HEADS // N_KV_HEADS
ATTN_WIDTH = N_Q_HEADS * HEAD_DIM
ROPE_DIM = HEAD_DIM // 4
ROPE_HALF = ROPE_DIM // 2
ROPE_THETA = 500000.0
DILATIONS = (1, 4, 16)
ATTN_BLOCK = 128
SSM_P = 64
SSM_HEADS = 16
SSM_INNER = SSM_HEADS * SSM_P
SSM_GROUPS = 2
HEADS_PER_GROUP = SSM_HEADS // SSM_GROUPS
D_STATE = 128
CONV_WIDTH = 4
CHUNK = 128
CONV_CH = SSM_INNER + 2 * SSM_GROUPS * D_STATE
MIX_WIDTH = ATTN_WIDTH + SSM_INNER
Q_END = ATTN_WIDTH
K_END = Q_END + N_KV_HEADS * HEAD_DIM
V_END = K_END + N_KV_HEADS * HEAD_DIM
Z_END = V_END + SSM_INNER
XBC_END = Z_END + CONV_CH
IN_PROJ = XBC_END + SSM_HEADS
LANE = 128
IN_PAD = XBC_END + LANE
Q_COL, Z_COL, XBC_COL, K_COL, V_COL, DT_COL = 0, 512, 1536, 3072, 3200, 3328
FFN_HIDDEN = 2816
EPS = 1e-5
ADAM_LR, ADAM_B1, ADAM_B2, ADAM_EPS, ADAM_WD, ADAM_STEP = 0.001, 0.9, 0.999, 1e-8, 0.01, 10
N_CHIPS = 4
N_DEV = 8
VMEM_LIMIT = 48 * 1024 * 1024
NEG_BIG = -1e30


def _params(sem=None):
    return pltpu.CompilerParams(dimension_semantics=sem, vmem_limit_bytes=VMEM_LIMIT)


def _pick(n, prefs):
    for p in prefs:
        if n % p == 0:
            return p
    return n


def matmul(a, b, *, name, ta=False, tb=False, out_dtype=F32, residual=None):
    if ta:
        assert not tb and residual is None
        return _matmul_over_rows(a, b, name=name, out_dtype=out_dtype)
    return _matmul_full_k(a, b, name=name, tb=tb, out_dtype=out_dtype, residual=residual)


def _matmul_full_k(a, b, *, name, tb, out_dtype, residual):
    a_parts = list(a) if isinstance(a, (list, tuple)) else [a]
    n_a = len(a_parts)
    m = a_parts[0].shape[0]
    kdim = sum(p.shape[1] for p in a_parts)
    wide = kdim > 1536 or any(p.dtype == F32 for p in a_parts)
    n = b.shape[0] if tb else b.shape[1]
    tm = _pick(m, (512, 256)) if wide else _pick(m, (1024, 512, 256))
    tn = _pick(n, (1152, 1408, 1536, 1024, 768, 512, 384, 256, 128))
    b_spec = pl.BlockSpec((tn, kdim), lambda i, j: (j, 0)) if tb else pl.BlockSpec((kdim, tn), lambda i, j: (0, j))
    o_spec = pl.BlockSpec((tm, tn), lambda i, j: (i, j))
    dims = (((1,), (1 if tb else 0,)), ((), ()))
    has_res = residual is not None

    def body(*refs):
        b_ref, o_ref = refs[n_a], refs[-1]
        pieces = [r[...].astype(BF16) for r in refs[:n_a]]
        av = pieces[0] if n_a == 1 else jnp.concatenate(pieces, axis=1)
        r = lax.dot_general(av, b_ref[...].astype(BF16), dims, preferred_element_type=F32)
        if has_res:
            r = r + refs[n_a + 1][...]
        o_ref[...] = r.astype(out_dtype)

    in_specs = ([pl.BlockSpec((tm, p.shape[1]), lambda i, j: (i, 0)) for p in a_parts] + [b_spec]
                + ([o_spec] if has_res else []))
    args = tuple(a_parts) + (b,) + ((residual,) if has_res else ())
    return pl.pallas_call(
        body, name=name, grid=(m // tm, n // tn), in_specs=in_specs, out_specs=o_spec,
        out_shape=jax.ShapeDtypeStruct((m, n), out_dtype),
        compiler_params=_params(("parallel", "parallel")),
    )(*args)


def _matmul_over_rows(a, b, *, name, out_dtype):
    t, m = a.shape
    n = b.shape[1]
    tm = _pick(m, (1024, 1408, 768, 512, 256, 128))
    tn = _pick(n, (1152, 1408, 1024, 768, 512, 256, 128))
    tk = _pick(t, (1024, 512, 256, 128))
    nk = t // tk

    def body(a_ref, b_ref, o_ref, acc):
        k = pl.program_id(2)
        part = lax.dot_general(a_ref[...].astype(BF16), b_ref[...].astype(BF16), (((0,), (0,)), ((), ())),
                               preferred_element_type=F32)

        @pl.when(k == 0)
        def _():
            acc[...] = part

        @pl.when(k > 0)
        def _():
            acc[...] += part

        @pl.when(k == nk - 1)
        def _():
            o_ref[...] = acc[...].astype(out_dtype)

    return pl.pallas_call(
        body, name=name, grid=(m // tm, n // tn, nk),
        in_specs=[pl.BlockSpec((tk, tm), lambda i, j, k: (k, i)), pl.BlockSpec((tk, tn), lambda i, j, k: (k, j))],
        out_specs=pl.BlockSpec((tm, tn), lambda i, j, k: (i, j)),
        out_shape=jax.ShapeDtypeStruct((m, n), out_dtype),
        scratch_shapes=[pltpu.VMEM((tm, tn), F32)],
        compiler_params=_params(("parallel", "parallel", "arbitrary")),
    )(a, b)


ROW_BLOCK_BYTES = 16 * 1024 * 1024


def _row_tile(t, tr, widths, n_copies):
    lanes = sum(-(-wd // LANE) * LANE for wd in widths) * n_copies
    tr = min(tr, t)
    while tr > 8 and tr * lanes * 4 > ROW_BLOCK_BYTES:
        tr //= 2
    return tr


def _row_widths(rows, groups, windows):
    windows = windows or [None] * len(rows)
    widths = [(w[1] if w else a.shape[1]) // groups for a, w in zip(rows, windows)]
    assert all(w is None or w[0] % wd == 0 for w, wd in zip(windows, widths))
    return widths, [(w[0] // wd if w else 0) for w, wd in zip(windows, widths)]


def _row_specs(tr, widths, offs):
    return [pl.BlockSpec((tr, wd), functools.partial(lambda g, i, off: (i, g + off), off=off)) for wd, off in zip(widths, offs)]


def rowwise_fwd(fn, rows, params, out_dtypes, *, name, tr=512, groups=1, windows=None):
    t = rows[0].shape[0]
    widths, offs = _row_widths(rows, groups, windows)
    tr = _row_tile(t, tr, widths, 2)
    row_specs = _row_specs(tr, widths, offs)
    par_spec = lambda p: pl.BlockSpec((1, p.shape[1] // groups), lambda g, i: (0, g))
    n_in = len(rows) + len(params)
    out_cols = [o.shape[1] for o in jax.eval_shape(
        fn, *[jax.ShapeDtypeStruct((tr, wd), F32) for wd in widths],
        *[jax.ShapeDtypeStruct((1, p.shape[1] // groups), F32) for p in params])]

    def body(*refs):
        vals = [r[...].astype(F32) for r in refs[:n_in]]
        outs = fn(*vals)
        for o_ref, o in zip(refs[n_in:], outs):
            o_ref[...] = o.astype(o_ref.dtype)

    return pl.pallas_call(
        body, name=name, grid=(groups, t // tr),
        in_specs=row_specs + [par_spec(p) for p in params],
        out_specs=[pl.BlockSpec((tr, c), lambda g, i: (i, g)) for c in out_cols],
        out_shape=[jax.ShapeDtypeStruct((t, c * groups), d) for c, d in zip(out_cols, out_dtypes)],
        compiler_params=_params(("arbitrary", "arbitrary")),
    )(*rows, *params)


def rowwise_bwd(fn, rows, params, cts, drow_dtypes, *, name, tr=512, groups=1, add_to_first=None, windows=None,
                ct_windows=None):
    t = rows[0].shape[0]
    widths, offs = _row_widths(rows, groups, windows)
    ct_widths, ct_offs = _row_widths(cts, groups, ct_windows)
    tr = _row_tile(t, tr, widths + ct_widths, 2)
    row_spec = lambda a: pl.BlockSpec((tr, a.shape[1] // groups), lambda g, i: (i, g))
    row_specs = _row_specs(tr, widths, offs)
    par_spec = lambda p: pl.BlockSpec((1, p.shape[1] // groups), lambda g, i: (0, g))
    n_rows, n_par, n_ct = len(rows), len(params), len(cts)
    has_add = add_to_first is not None
    n_in = n_rows + n_par + n_ct + (1 if has_add else 0)

    def body(*refs):
        i = pl.program_id(1)
        vals = [r[...].astype(F32) for r in refs[:n_rows + n_par]]
        ct_vals = tuple(r[...].astype(F32) for r in refs[n_rows + n_par:n_rows + n_par + n_ct])
        _, vjp = jax.vjp(fn, *vals)
        grads = vjp(ct_vals)
        out_refs = refs[n_in:]
        for idx in range(n_rows):
            g = grads[idx]
            if idx == 0 and has_add:
                g = g + refs[n_in - 1][...]
            out_refs[idx][...] = g.astype(out_refs[idx].dtype)
        for idx in range(n_par):
            p_ref = out_refs[n_rows + idx]

            @pl.when(i == 0)
            def _():
                p_ref[...] = jnp.zeros_like(p_ref)

            p_ref[...] += grads[n_rows + idx]

    ins = list(rows) + list(params) + list(cts) + ([add_to_first] if has_add else [])
    in_specs = (row_specs + [par_spec(p) for p in params] + _row_specs(tr, ct_widths, ct_offs)
                + ([row_spec(add_to_first)] if has_add else []))
    return pl.pallas_call(
        body, name=name, grid=(groups, t // tr), in_specs=in_specs,
        out_specs=[pl.BlockSpec((tr, wd), lambda g, i: (i, g)) for wd in widths] + [par_spec(p) for p in params],
        out_shape=[jax.ShapeDtypeStruct((t, wd * groups), d) for wd, d in zip(widths, drow_dtypes)]
        + [jax.ShapeDtypeStruct(p.shape, F32) for p in params],
        compiler_params=_params(("arbitrary", "arbitrary")),
    )(*ins)


def rms_fn(x, w):
    return (x * lax.rsqrt(jnp.mean(x * x, axis=-1, keepdims=True) + EPS) * w,)


def swiglu_fn(g, u):
    return (g * jax.nn.sigmoid(g) * u,)


def gated_norm_fn(y, z, w):
    v = y * (z * jax.nn.sigmoid(z))
    return (v * lax.rsqrt(jnp.mean(v * v, axis=-1, keepdims=True) + EPS) * w,)


def combine_fn(o1, o2, o3, l1, l2, l3):
    m = jnp.maximum(jnp.maximum(l1, l2), l3)
    e1, e2, e3 = jnp.exp(l1 - m), jnp.exp(l2 - m), jnp.exp(l3 - m)
    inv = 1.0 / (e1 + e2 + e3)
    return ((e1 * inv) * o1 + (e2 * inv) * o2 + (e3 * inv) * o3,)


def loss_and_grad(h, target, w, *, tr=512):
    t, d = h.shape

    def loss_fn(hv, wv, tv):
        err = rms_fn(hv, wv)[0] - tv
        per_row = jnp.mean(err * err, axis=-1, keepdims=True)
        return 0.5 * jnp.sum(per_row, axis=0, keepdims=True)

    def body(h_ref, t_ref, w_ref, dh_ref, dw_ref, loss_ref):
        i = pl.program_id(0)

        @pl.when(i == 0)
        def _():
            dw_ref[...] = jnp.zeros_like(dw_ref)
            loss_ref[...] = jnp.zeros_like(loss_ref)

        tv = t_ref[...]
        val, vjp = jax.vjp(lambda hv, wv: loss_fn(hv, wv, tv), h_ref[...], w_ref[...])
        dh, dw = vjp(jnp.ones((1, 1), F32))
        dh_ref[...] = dh
        dw_ref[...] += dw
        loss_ref[...] += jnp.broadcast_to(val, loss_ref.shape)

    row = pl.BlockSpec((tr, d), lambda i: (i, 0))
    par = pl.BlockSpec((1, d), lambda i: (0, 0))
    return pl.pallas_call(
        body, name="loss_and_grad", grid=(t // tr,), in_specs=[row, row, par],
        out_specs=[row, par, pl.BlockSpec((1, LANE), lambda i: (0, 0))],
        out_shape=[jax.ShapeDtypeStruct((t, d), F32), jax.ShapeDtypeStruct((1, d), F32),
                   jax.ShapeDtypeStruct((1, LANE), F32)],
        compiler_params=_params(("arbitrary",)),
    )(h, target, w)


def _split3(x):
    hi = x.astype(BF16)
    r1 = x - hi.astype(F32)
    mid = r1.astype(BF16)
    lo = (r1 - mid.astype(F32)).astype(BF16)
    return hi, mid, lo


def _dot01_left(m01, x):
    return sum(jnp.dot(m01, p, preferred_element_type=F32) for p in _split3(x))


def _dot01_right(x, m01):
    return sum(jnp.dot(p, m01, preferred_element_type=F32) for p in _split3(x))


def rotary(xs_list, cosf, sinf, scale, *, adjoint, name, ts=512):
    b, h, s, c = xs_list[0].shape
    n_x = len(xs_list)

    def body(*refs):
        x = refs[0][0, 0]
        for r in refs[1:n_x]:
            x = x + r[0, 0]
        cos_v, sin_v = refs[n_x][0], refs[n_x + 1][0]
        o_ref = refs[n_x + 2]
        ci = lax.broadcasted_iota(jnp.int32, (c, c), 0)
        cj = lax.broadcasted_iota(jnp.int32, (c, c), 1)
        swap = ((cj == ci + ROPE_HALF) & (ci < ROPE_HALF)) | ((cj == ci - ROPE_HALF) & (ci >= ROPE_HALF) & (ci < ROPE_DIM))
        swap = swap.astype(BF16)
        if adjoint:
            out = x * cos_v + _dot01_right(x * sin_v, swap)
        else:
            out = x * cos_v + _dot01_right(x, swap) * sin_v
        o_ref[0, 0] = out * scale

    x_spec = pl.BlockSpec((1, 1, ts, c), lambda bi, hi, si: (bi, hi, si, 0))
    t_spec = pl.BlockSpec((1, ts, c), lambda bi, hi, si: (bi, si, 0))
    return pl.pallas_call(
        body, name=name, grid=(b, h, s // ts), in_specs=[x_spec] * n_x + [t_spec, t_spec], out_specs=x_spec,
        out_shape=jax.ShapeDtypeStruct((b, h, s, c), F32),
        compiler_params=_params(("parallel", "parallel", "parallel")),
    )(*xs_list, cosf, sinf)


def add3(a, b, c, *, name, tr=1024):
    def fn(x, y, z):
        return (x + y + z,)
    return rowwise_fwd(fn, [a, b, c], [], [F32], name=name, tr=tr)[0]


def _attn_mask(n):
    rows = GQA * ATTN_BLOCK
    qi = lax.broadcasted_iota(jnp.int32, (rows, 2 * ATTN_BLOCK), 0) % ATTN_BLOCK
    ki = lax.broadcasted_iota(jnp.int32, (rows, 2 * ATTN_BLOCK), 1)
    delta = qi + ATTN_BLOCK - ki
    return (delta >= 0) & (delta <= ATTN_BLOCK) & ((n - 1) * ATTN_BLOCK + ki >= 0)


def _attn_specs(l):
    q_spec = pl.BlockSpec((1, GQA, ATTN_BLOCK, HEAD_DIM), lambda p, n: (p, 0, n, 0))
    l_spec = pl.BlockSpec((1, GQA, ATTN_BLOCK, 1), lambda p, n: (p, 0, n, 0))
    kprev = pl.BlockSpec((1, ATTN_BLOCK, HEAD_DIM), lambda p, n: (p, jnp.maximum(n - 1, 0), 0))
    kcur = pl.BlockSpec((1, ATTN_BLOCK, HEAD_DIM), lambda p, n: (p, n, 0))
    kfull = pl.BlockSpec((1, l, HEAD_DIM), lambda p, n: (p, 0, 0))
    return q_spec, l_spec, kprev, kcur, kfull


def attn_branch_fwd(q, k, v, *, name):
    p_cnt, _, l, _ = q.shape
    rows = GQA * ATTN_BLOCK
    q_spec, l_spec, kprev, kcur, _ = _attn_specs(l)

    def body(q_ref, kp_ref, kc_ref, vp_ref, vc_ref, o_ref, lse_ref):
        n = pl.program_id(1)
        qv = q_ref[0].reshape(rows, HEAD_DIM).astype(BF16)
        kk = jnp.concatenate([kp_ref[0], kc_ref[0]], axis=0).astype(BF16)
        vv = jnp.concatenate([vp_ref[0], vc_ref[0]], axis=0).astype(BF16)
        s = lax.dot_general(qv, kk, (((1,), (1,)), ((), ())), preferred_element_type=F32)
        s = jnp.where(_attn_mask(n), s, NEG_BIG)
        m = jnp.max(s, axis=-1, keepdims=True)
        pr = jnp.exp(s - m)
        den = jnp.sum(pr, axis=-1, keepdims=True)
        o = jnp.dot(pr.astype(BF16), vv, preferred_element_type=F32) / den
        o_ref[0] = o.reshape(GQA, ATTN_BLOCK, HEAD_DIM)
        lse_ref[0] = (m + jnp.log(den)).reshape(GQA, ATTN_BLOCK, 1)

    return pl.pallas_call(
        body, name=name, grid=(p_cnt, l // ATTN_BLOCK), in_specs=[q_spec, kprev, kcur, kprev, kcur],
        out_specs=[q_spec, l_spec],
        out_shape=[jax.ShapeDtypeStruct(q.shape, F32), jax.ShapeDtypeStruct(q.shape[:3] + (1,), F32)],
        compiler_params=_params(("parallel", "arbitrary")),
    )(q, k, k, v, v)


def attn_branch_bwd(q, k, v, o, lse, do, dlse, *, name):
    p_cnt, _, l, _ = q.shape
    rows = GQA * ATTN_BLOCK
    q_spec, l_spec, kprev, kcur, kfull = _attn_specs(l)

    def body(q_ref, kp_ref, kc_ref, vp_ref, vc_ref, o_ref, lse_ref, do_ref, dlse_ref, dq_ref, dk_ref, dv_ref):
        n = pl.program_id(1)

        @pl.when(n == 0)
        def _():
            dk_ref[...] = jnp.zeros_like(dk_ref)
            dv_ref[...] = jnp.zeros_like(dv_ref)

        qv = q_ref[0].reshape(rows, HEAD_DIM).astype(BF16)
        kk = jnp.concatenate([kp_ref[0], kc_ref[0]], axis=0).astype(BF16)
        vv = jnp.concatenate([vp_ref[0], vc_ref[0]], axis=0).astype(BF16)
        ov = o_ref[0].reshape(rows, HEAD_DIM)
        dov = do_ref[0].reshape(rows, HEAD_DIM)
        lsev = lse_ref[0].reshape(rows, 1)
        dlsev = dlse_ref[0].reshape(rows, 1)
        s = lax.dot_general(qv, kk, (((1,), (1,)), ((), ())), preferred_element_type=F32)
        pr = jnp.where(_attn_mask(n), jnp.exp(s - lsev), 0.0)
        do16 = dov.astype(BF16)
        dv = lax.dot_general(pr.astype(BF16), do16, (((0,), (0,)), ((), ())), preferred_element_type=F32)
        dp = lax.dot_general(do16, vv, (((1,), (1,)), ((), ())), preferred_element_type=F32)
        delta = jnp.sum(dov * ov, axis=-1, keepdims=True)
        ds = (pr * (dp - delta + dlsev)).astype(BF16)
        dq = jnp.dot(ds, kk, preferred_element_type=F32)
        dk = lax.dot_general(ds, qv, (((0,), (0,)), ((), ())), preferred_element_type=F32)
        dq_ref[0] = dq.reshape(GQA, ATTN_BLOCK, HEAD_DIM)
        cur = pl.ds(pl.multiple_of(n * ATTN_BLOCK, ATTN_BLOCK), ATTN_BLOCK)
        dk_ref[0, cur, :] += dk[ATTN_BLOCK:]
        dv_ref[0, cur, :] += dv[ATTN_BLOCK:]

        @pl.when(n > 0)
        def _():
            prev = pl.ds(pl.multiple_of((n - 1) * ATTN_BLOCK, ATTN_BLOCK), ATTN_BLOCK)
            dk_ref[0, prev, :] += dk[:ATTN_BLOCK]
            dv_ref[0, prev, :] += dv[:ATTN_BLOCK]

    return pl.pallas_call(
        body, name=name, grid=(p_cnt, l // ATTN_BLOCK),
        in_specs=[q_spec, kprev, kcur, kprev, kcur, q_spec, l_spec, q_spec, l_spec],
        out_specs=[q_spec, kfull, kfull],
        out_shape=[jax.ShapeDtypeStruct(q.shape, F32), jax.ShapeDtypeStruct(k.shape, F32),
                   jax.ShapeDtypeStruct(v.shape, F32)],
        compiler_params=_params(("parallel", "arbitrary")),
    )(q, k, k, v, v, o, lse, do, dlse)


ATTN_PAD = ATTN_BLOCK * DILATIONS[-1]
Q_GROUP_W = GQA * HEAD_DIM
ATTN_VMEM_LIMIT = 56 * 1024 * 1024


def _rope(x, cos_v, sin_v, swap, scale, adjoint):
    if adjoint:
        return (x * cos_v + _dot01_right(x * sin_v, swap)) * scale
    return (x * cos_v + _dot01_right(x, swap) * sin_v) * scale


def _swap_matrix():
    c = HEAD_DIM
    ci = lax.broadcasted_iota(jnp.int32, (c, c), 0)
    cj = lax.broadcasted_iota(jnp.int32, (c, c), 1)
    swap = ((cj == ci + ROPE_HALF) & (ci < ROPE_HALF)) | ((cj == ci - ROPE_HALF) & (ci >= ROPE_HALF) & (ci < ROPE_DIM))
    return swap.astype(BF16)


def _attn_prologue(q_ref, kv_ref, tab_ref, q_s, k_s, v_s, hk, s_len):
    swap = _swap_matrix()
    cos_v, sin_v = tab_ref[0, :, :HEAD_DIM], tab_ref[0, :, HEAD_DIM:]
    for g in range(GQA):
        cols = slice(g * HEAD_DIM, (g + 1) * HEAD_DIM)
        q_s[:, cols] = _rope(q_ref[0, :, cols], cos_v, sin_v, swap, HEAD_DIM ** -0.5, False)
    zeros = jnp.zeros((ATTN_PAD, HEAD_DIM), F32)
    k_s[0:ATTN_PAD, :] = zeros
    v_s[0:ATTN_PAD, :] = zeros
    for h in range(N_KV_HEADS):
        @pl.when(hk == h)
        def _():
            k_s[ATTN_PAD:ATTN_PAD + s_len, :] = _rope(kv_ref[0, :, h * HEAD_DIM:(h + 1) * HEAD_DIM], cos_v, sin_v, swap, 1.0, False)
            v_s[ATTN_PAD:ATTN_PAD + s_len, :] = kv_ref[0, :, LANE + h * HEAD_DIM:LANE + (h + 1) * HEAD_DIM]


def _attn_blocks(s_len):
    out = []
    for i, d in enumerate(DILATIONS):
        nb = s_len // (ATTN_BLOCK * d)
        for r in range(d):
            for n in range(nb):
                start = r + d * ATTN_BLOCK * n
                out.append((i, d, start, ATTN_PAD + start - d * ATTN_BLOCK, n))
    return out


def _rows(start, size, d):
    return pl.ds(start, size, stride=d) if d > 1 else pl.ds(start, size)


def _stack_heads(blk):
    return jnp.concatenate([blk[:, g * HEAD_DIM:(g + 1) * HEAD_DIM] for g in range(GQA)], axis=0)


def _stack_stats(blk):
    return jnp.concatenate([jnp.max(blk[:, g * HEAD_DIM:(g + 1) * HEAD_DIM], axis=1, keepdims=True) for g in range(GQA)], axis=0)


def _attn_in_specs(s_len):
    assert K_COL % (2 * LANE) == 0 and V_COL == K_COL + LANE
    q_spec = pl.BlockSpec((1, s_len, Q_GROUP_W), lambda b, h: (b, 0, Q_COL // Q_GROUP_W + h))
    kv_spec = pl.BlockSpec((1, s_len, 2 * LANE), lambda b, h: (b, 0, K_COL // (2 * LANE)))
    t_spec = pl.BlockSpec((1, s_len, 2 * HEAD_DIM), lambda b, h: (b, 0, 0))
    o_spec = pl.BlockSpec((1, s_len, Q_GROUP_W), lambda b, h: (b, 0, h))
    return q_spec, kv_spec, t_spec, o_spec


def attn_fwd(proj3, rope_tab, *, name):
    b, s_len, _ = proj3.shape
    q_spec, kv_spec, t_spec, o_spec = _attn_in_specs(s_len)
    n_br = len(DILATIONS)

    def body(q_ref, kv_ref, tab_ref, o_ref, lse_ref, q_s, k_s, v_s, *branch_s):
        o_s, l_s = branch_s[:n_br], branch_s[n_br:]
        _attn_prologue(q_ref, kv_ref, tab_ref, q_s, k_s, v_s, pl.program_id(1), s_len)
        for i, d, q0, k0, n in _attn_blocks(s_len):
            qv = _stack_heads(q_s[_rows(q0, ATTN_BLOCK, d), :]).astype(BF16)
            kk = k_s[_rows(k0, 2 * ATTN_BLOCK, d), :].astype(BF16)
            vv = v_s[_rows(k0, 2 * ATTN_BLOCK, d), :].astype(BF16)
            sc = lax.dot_general(qv, kk, (((1,), (1,)), ((), ())), preferred_element_type=F32)
            sc = jnp.where(_attn_mask(n), sc, NEG_BIG)
            m = jnp.max(sc, axis=-1, keepdims=True)
            pr = jnp.exp(sc - m)
            den = jnp.sum(pr, axis=-1, keepdims=True)
            o = jnp.dot(pr.astype(BF16), vv, preferred_element_type=F32) / den
            lse = m + jnp.log(den)
            for g in range(GQA):
                part = slice(g * ATTN_BLOCK, (g + 1) * ATTN_BLOCK)
                o_s[i][_rows(q0, ATTN_BLOCK, d), g * HEAD_DIM:(g + 1) * HEAD_DIM] = o[part]
                l_s[i][_rows(q0, ATTN_BLOCK, d), g * HEAD_DIM:(g + 1) * HEAD_DIM] = jnp.broadcast_to(lse[part], (ATTN_BLOCK, HEAD_DIM))
        step = 256
        for t0 in range(0, s_len, step):
            rs = pl.ds(t0, step)
            for g in range(GQA):
                ls = [l_s[i][rs, g * HEAD_DIM:(g + 1) * HEAD_DIM] for i in range(n_br)]
                m = functools.reduce(jnp.maximum, ls)
                es = [jnp.exp(l - m) for l in ls]
                tot = functools.reduce(lambda a, c: a + c, es)
                inv = 1.0 / tot
                acc = None
                for i in range(n_br):
                    term = (es[i] * inv) * o_s[i][rs, g * HEAD_DIM:(g + 1) * HEAD_DIM]
                    acc = term if acc is None else acc + term
                o_ref[0, rs, g * HEAD_DIM:(g + 1) * HEAD_DIM] = acc
                lse_ref[0, rs, g * HEAD_DIM:(g + 1) * HEAD_DIM] = m + jnp.log(tot)

    return pl.pallas_call(
        body, name=name, grid=(b, N_KV_HEADS), in_specs=[q_spec, kv_spec, t_spec],
        out_specs=[o_spec, o_spec],
        out_shape=[jax.ShapeDtypeStruct((b, s_len, ATTN_WIDTH), F32)] * 2,
        scratch_shapes=[pltpu.VMEM((s_len, Q_GROUP_W), F32), pltpu.VMEM((ATTN_PAD + s_len, HEAD_DIM), F32),
                        pltpu.VMEM((ATTN_PAD + s_len, HEAD_DIM), F32)] + [pltpu.VMEM((s_len, Q_GROUP_W), F32)] * (2 * n_br),
        compiler_params=pltpu.CompilerParams(dimension_semantics=("arbitrary", "arbitrary"), vmem_limit_bytes=ATTN_VMEM_LIMIT),
    )(proj3, proj3, rope_tab)


def attn_bwd(proj3, rope_tab, attn3, lse3, d_attn3, *, name):
    b, s_len, _ = proj3.shape
    q_spec, kv_spec, t_spec, o_spec = _attn_in_specs(s_len)
    kv_out = pl.BlockSpec((1, 1, s_len, HEAD_DIM), lambda bi, h: (bi, h, 0, 0))

    def body(q_ref, kv_ref, tab_ref, o_ref, lse_ref, do_ref, dq_ref, dk_ref, dv_ref,
             q_s, k_s, v_s, dl_s, dq_s, dk_s, dv_s):
        _attn_prologue(q_ref, kv_ref, tab_ref, q_s, k_s, v_s, pl.program_id(1), s_len)
        dq_s[...] = jnp.zeros_like(dq_s)
        dk_s[...] = jnp.zeros_like(dk_s)
        dv_s[...] = jnp.zeros_like(dv_s)
        for g in range(GQA):
            cols = slice(g * HEAD_DIM, (g + 1) * HEAD_DIM)
            delta = jnp.sum(do_ref[0, :, cols] * o_ref[0, :, cols], axis=1, keepdims=True)
            dl_s[:, cols] = jnp.broadcast_to(delta, (s_len, HEAD_DIM))
        for i, d, q0, k0, n in _attn_blocks(s_len):
            qrows, krows = _rows(q0, ATTN_BLOCK, d), _rows(k0, 2 * ATTN_BLOCK, d)
            qv = _stack_heads(q_s[qrows, :]).astype(BF16)
            kk = k_s[krows, :].astype(BF16)
            vv = v_s[krows, :].astype(BF16)
            do16 = _stack_heads(do_ref.at[0][qrows, :]).astype(BF16)
            lse = _stack_stats(lse_ref.at[0][qrows, :])
            delta = _stack_stats(dl_s[qrows, :])
            sc = lax.dot_general(qv, kk, (((1,), (1,)), ((), ())), preferred_element_type=F32)
            pr = jnp.where(_attn_mask(n), jnp.exp(sc - lse), 0.0)
            dv = lax.dot_general(pr.astype(BF16), do16, (((0,), (0,)), ((), ())), preferred_element_type=F32)
            dp = lax.dot_general(do16, vv, (((1,), (1,)), ((), ())), preferred_element_type=F32)
            ds = (pr * (dp - delta)).astype(BF16)
            dq = jnp.dot(ds, kk, preferred_element_type=F32)
            dk = lax.dot_general(ds, qv, (((0,), (0,)), ((), ())), preferred_element_type=F32)
            for g in range(GQA):
                cols = slice(g * HEAD_DIM, (g + 1) * HEAD_DIM)
                dq_s[qrows, cols] += dq[g * ATTN_BLOCK:(g + 1) * ATTN_BLOCK]
            dk_s[krows, :] += dk
            dv_s[krows, :] += dv
        swap = _swap_matrix()
        cos_v, sin_v = tab_ref[0, :, :HEAD_DIM], tab_ref[0, :, HEAD_DIM:]
        for g in range(GQA):
            cols = slice(g * HEAD_DIM, (g + 1) * HEAD_DIM)
            dq_ref[0, :, cols] = _rope(dq_s[:, cols], cos_v, sin_v, swap, HEAD_DIM ** -0.5, True)
        dk_ref[0, 0] = _rope(dk_s[ATTN_PAD:ATTN_PAD + s_len, :], cos_v, sin_v, swap, 1.0, True)
        dv_ref[0, 0] = dv_s[ATTN_PAD:ATTN_PAD + s_len, :]

    kv_shape = jax.ShapeDtypeStruct((b, N_KV_HEADS, s_len, HEAD_DIM), F32)
    return pl.pallas_call(
        body, name=name, grid=(b, N_KV_HEADS),
        in_specs=[q_spec, kv_spec, t_spec, o_spec, o_spec, o_spec],
        out_specs=[o_spec, kv_out, kv_out],
        out_shape=[jax.ShapeDtypeStruct((b, s_len, ATTN_WIDTH), F32), kv_shape, kv_shape],
        scratch_shapes=[pltpu.VMEM((s_len, Q_GROUP_W), F32), pltpu.VMEM((ATTN_PAD + s_len, HEAD_DIM), F32),
                        pltpu.VMEM((ATTN_PAD + s_len, HEAD_DIM), F32), pltpu.VMEM((s_len, Q_GROUP_W), F32),
                        pltpu.VMEM((s_len, Q_GROUP_W), F32), pltpu.VMEM((ATTN_PAD + s_len, HEAD_DIM), F32),
                        pltpu.VMEM((ATTN_PAD + s_len, HEAD_DIM), F32)],
        compiler_params=pltpu.CompilerParams(dimension_semantics=("arbitrary", "arbitrary"), vmem_limit_bytes=ATTN_VMEM_LIMIT),
    )(proj3, proj3, rope_tab, attn3, lse3, d_attn3)


HALF_W = 2 * HEAD_DIM
N_HALF = Q_GROUP_W // HALF_W
_ATTN_BIAS_BUF = pltpu.VMEM((2, GQA * ATTN_BLOCK, 2 * ATTN_BLOCK), F32)


def _attn_bias(bias_s):
    for first in (0, 1):
        bias_s[first] = jnp.where(_attn_mask(first), 0.0, NEG_BIG)


def _attn_prologue(q_refs, kv_ref, tab_ref, q_s, kv_s, hk, s_len):
    swap = _swap_matrix()
    cos_v, sin_v = tab_ref[0, :, :HEAD_DIM], tab_ref[0, :, HEAD_DIM:]
    for j in range(N_HALF):
        for e in range(2):
            cols = slice(e * HEAD_DIM, (e + 1) * HEAD_DIM)
            q_s[j][:, cols] = _rope(q_refs[j][0, :, cols], cos_v, sin_v, swap, HEAD_DIM ** -0.5, False)
    kv_s[0:ATTN_PAD, :] = jnp.zeros((ATTN_PAD, HALF_W), F32)
    for h in range(N_KV_HEADS):
        @pl.when(hk == h)
        def _():
            kv_s[ATTN_PAD:ATTN_PAD + s_len, :HEAD_DIM] = _rope(kv_ref[0, :, h * HEAD_DIM:(h + 1) * HEAD_DIM], cos_v, sin_v,
                                                               swap, 1.0, False)
            kv_s[ATTN_PAD:ATTN_PAD + s_len, HEAD_DIM:] = kv_ref[0, :, LANE + h * HEAD_DIM:LANE + (h + 1) * HEAD_DIM]


def _stack_heads(halves):
    return jnp.concatenate([h[:, e * HEAD_DIM:(e + 1) * HEAD_DIM] for h in halves for e in range(2)], axis=0)


def _unstack_heads(x, j):
    return jnp.concatenate([x[(2 * j + e) * ATTN_BLOCK:(2 * j + e + 1) * ATTN_BLOCK] for e in range(2)], axis=1)


def _stack_stats(halves):
    return jnp.concatenate([jnp.max(h[:, e * HEAD_DIM:(e + 1) * HEAD_DIM], axis=1, keepdims=True)
                            for h in halves for e in range(2)], axis=0)


def _attn_in_specs(s_len):
    assert K_COL % (2 * LANE) == 0 and V_COL == K_COL + LANE

    def halves(first_tile):
        return [pl.BlockSpec((1, s_len, HALF_W), functools.partial(lambda b, h, j: (b, 0, first_tile + N_HALF * h + j), j=j))
                for j in range(N_HALF)]

    kv_spec = pl.BlockSpec((1, s_len, 2 * LANE), lambda b, h: (b, 0, K_COL // (2 * LANE)))
    t_spec = pl.BlockSpec((1, s_len, 2 * HEAD_DIM), lambda b, h: (b, 0, 0))
    o_spec = pl.BlockSpec((1, s_len, Q_GROUP_W), lambda b, h: (b, 0, h))
    return halves(Q_COL // HALF_W), kv_spec, t_spec, o_spec, halves(0)


class SideCopy:
    def __init__(self, side, *, n_in, n_out, grid):
        self.side, self.n_in, self.n_out, self.grid = side, n_in, n_out, grid
        hbm = pl.BlockSpec(memory_space=pltpu.HBM)
        if side is None:
            self.in_specs, self.out_specs, self.out_shape, self.scratch, self.args = [], [], [], [], []
            return
        srcs, per_dest = side
        n = len(srcs)
        self.in_specs, self.out_specs, self.args = [hbm] * n, [hbm] * n, list(srcs)
        self.out_shape = [jax.ShapeDtypeStruct(s.shape if per_dest else (N_CHIPS,) + s.shape, s.dtype) for s in srcs]
        self.scratch = [pltpu.SemaphoreType.DMA(((N_CHIPS - 1) * n,)), pltpu.SemaphoreType.DMA(((N_CHIPS - 1) * n,)),
                        pltpu.SemaphoreType.DMA((n,))]

    def wrap(self, body):
        if self.side is None:
            return body
        n_in, n_out, grid, per_dest, n = self.n_in, self.n_out, self.grid, self.side[1], len(self.side[0])

        def wrapped(*refs):
            ins, srcs = refs[:n_in], refs[n_in:n_in + n]
            outs, dsts = refs[n_in + n:n_in + n + n_out], refs[n_in + n + n_out:n_in + 2 * n + n_out]
            scratch, sems = refs[n_in + 2 * n + n_out:-3], refs[-3:]
            ids = [pl.program_id(a) for a in range(len(grid))]
            first = functools.reduce(lambda p, q: p & q, [i == 0 for i in ids])
            last = functools.reduce(lambda p, q: p & q, [i == g - 1 for i, g in zip(ids, grid)])

            @pl.when(first)
            def _():
                for a in range(n):
                    local, sends, _ = _chip_copies(srcs[a], dsts[a], *sems, per_dest, a)
                    local.start()
                    for cp in sends:
                        cp.start()

            body(*ins, *outs, *scratch)

            @pl.when(last)
            def _():
                for a in range(n):
                    local, sends, recvs = _chip_copies(srcs[a], dsts[a], *sems, per_dest, a)
                    for cp in recvs:
                        cp.wait_recv()
                    for cp in sends:
                        cp.wait_send()
                    local.wait()

        return wrapped


def _chip_copies(src_ref, dst_ref, send_sems, recv_sems, local_sems, per_dest, a=0):
    x, y, c = lax.axis_index("x"), lax.axis_index("y"), lax.axis_index("c")
    chip = 2 * x + y
    own = src_ref.at[chip] if per_dest else src_ref
    local = pltpu.make_async_copy(own, dst_ref.at[chip], local_sems.at[a])
    sends, recvs = [], []
    for k, (px, py) in enumerate([(1 - x, y), (x, 1 - y), (1 - x, 1 - y)]):
        k = (N_CHIPS - 1) * a + k
        peer = dict(send_sem=send_sems.at[k], recv_sem=recv_sems.at[k], device_id=(px, py, c), device_id_type=MESH)
        sends.append(pltpu.make_async_remote_copy(src_ref=src_ref.at[2 * px + py] if per_dest else src_ref,
                                                  dst_ref=dst_ref.at[chip], **peer))
        recvs.append(pltpu.make_async_remote_copy(src_ref=own, dst_ref=dst_ref.at[2 * px + py], **peer))
    return local, sends, recvs


def attn_fwd(proj3, rope_tab, *, name, side=None):
    b, s_len, _ = proj3.shape
    q_specs, kv_spec, t_spec, o_spec, _ = _attn_in_specs(s_len)
    n_br = len(DILATIONS)

    def body(*refs):
        q_refs, (kv_ref, tab_ref, o_ref, lse_ref) = refs[:N_HALF], refs[N_HALF:N_HALF + 4]
        scratch = refs[N_HALF + 4:]
        q_s, kv_s = scratch[:N_HALF], scratch[N_HALF]
        o_s = [scratch[N_HALF + 1 + i * N_HALF:N_HALF + 1 + (i + 1) * N_HALF] for i in range(n_br)]
        l_s = [scratch[N_HALF + 1 + (n_br + i) * N_HALF:N_HALF + 1 + (n_br + i + 1) * N_HALF] for i in range(n_br)]
        bias_s = scratch[-1]
        _attn_prologue(q_refs, kv_ref, tab_ref, q_s, kv_s, pl.program_id(1), s_len)
        _attn_bias(bias_s)
        for i, d, q0, k0, n in _attn_blocks(s_len):
            qrows = _rows(q0, ATTN_BLOCK, d)
            qv = _stack_heads([q_s[j][qrows, :] for j in range(N_HALF)]).astype(BF16)
            kvb = kv_s[_rows(k0, 2 * ATTN_BLOCK, d), :].astype(BF16)
            kk, vv = kvb[:, :HEAD_DIM], kvb[:, HEAD_DIM:]
            sc = lax.dot_general(qv, kk, (((1,), (1,)), ((), ())), preferred_element_type=F32)
            sc = sc + bias_s[min(n, 1)]
            m = jnp.max(sc, axis=-1, keepdims=True)
            pr = jnp.exp(sc - m)
            den = jnp.sum(pr, axis=-1, keepdims=True)
            o = jnp.dot(pr.astype(BF16), vv, preferred_element_type=F32) / den
            lse_b = jnp.broadcast_to(m + jnp.log(den), (GQA * ATTN_BLOCK, HEAD_DIM))
            for j in range(N_HALF):
                o_s[i][j][qrows, :] = _unstack_heads(o, j)
                l_s[i][j][qrows, :] = _unstack_heads(lse_b, j)
        step = 256
        for t0 in range(0, s_len, step):
            rs = pl.ds(t0, step)
            for j in range(N_HALF):
                ls = [l_s[i][j][rs, :] for i in range(n_br)]
                m = functools.reduce(jnp.maximum, ls)
                es = [jnp.exp(l - m) for l in ls]
                tot = functools.reduce(lambda a, c: a + c, es)
                inv = 1.0 / tot
                acc = None
                for i in range(n_br):
                    term = (es[i] * inv) * o_s[i][j][rs, :]
                    acc = term if acc is None else acc + term
                o_ref[0, rs, j * HALF_W:(j + 1) * HALF_W] = acc
                lse_ref[0, rs, j * HALF_W:(j + 1) * HALF_W] = m + jnp.log(tot)

    half_buf = pltpu.VMEM((s_len, HALF_W), F32)
    call = SideCopy(side, n_in=N_HALF + 2, n_out=2, grid=(b, N_KV_HEADS))
    return pl.pallas_call(
        call.wrap(body), name=name, grid=(b, N_KV_HEADS), in_specs=q_specs + [kv_spec, t_spec] + call.in_specs,
        out_specs=[o_spec, o_spec] + call.out_specs,
        out_shape=[jax.ShapeDtypeStruct((b, s_len, ATTN_WIDTH), F32)] * 2 + call.out_shape,
        scratch_shapes=[half_buf] * N_HALF + [pltpu.VMEM((ATTN_PAD + s_len, HALF_W), F32)] + [half_buf] * (2 * n_br * N_HALF)
        + [_ATTN_BIAS_BUF] + call.scratch,
        compiler_params=pltpu.CompilerParams(dimension_semantics=("arbitrary", "arbitrary"), vmem_limit_bytes=ATTN_VMEM_LIMIT),
    )(*([proj3] * (N_HALF + 1)), rope_tab, *call.args)


def attn_bwd(proj3, rope_tab, attn3, lse3, d_attn3, *, name, side=None):
    b, s_len, _ = proj3.shape
    q_specs, kv_spec, t_spec, o_spec, half_specs = _attn_in_specs(s_len)
    kv_out = pl.BlockSpec((1, 1, s_len, HEAD_DIM), lambda bi, h: (bi, h, 0, 0))

    def body(*refs):
        q_refs = refs[:N_HALF]
        kv_ref, tab_ref, o_ref = refs[N_HALF:N_HALF + 3]
        lse_refs = refs[N_HALF + 3:2 * N_HALF + 3]
        do_refs = refs[2 * N_HALF + 3:3 * N_HALF + 3]
        dq_ref, dk_ref, dv_ref = refs[3 * N_HALF + 3:3 * N_HALF + 6]
        scratch = refs[3 * N_HALF + 6:]
        q_s, kv_s = scratch[:N_HALF], scratch[N_HALF]
        dl_s = scratch[N_HALF + 1:2 * N_HALF + 1]
        dq_s = scratch[2 * N_HALF + 1:3 * N_HALF + 1]
        dkv_s = scratch[3 * N_HALF + 1]
        bias_s = scratch[-1]
        _attn_prologue(q_refs, kv_ref, tab_ref, q_s, kv_s, pl.program_id(1), s_len)
        _attn_bias(bias_s)
        dkv_s[...] = jnp.zeros_like(dkv_s)
        for j in range(N_HALF):
            dq_s[j][...] = jnp.zeros_like(dq_s[j])
            for e in range(2):
                cols = slice(e * HEAD_DIM, (e + 1) * HEAD_DIM)
                ocols = slice(j * HALF_W + e * HEAD_DIM, j * HALF_W + (e + 1) * HEAD_DIM)
                delta = jnp.sum(do_refs[j][0, :, cols] * o_ref[0, :, ocols], axis=1, keepdims=True)
                dl_s[j][:, cols] = jnp.broadcast_to(delta, (s_len, HEAD_DIM))
        for i, d, q0, k0, n in _attn_blocks(s_len):
            qrows, krows = _rows(q0, ATTN_BLOCK, d), _rows(k0, 2 * ATTN_BLOCK, d)
            qv = _stack_heads([q_s[j][qrows, :] for j in range(N_HALF)]).astype(BF16)
            kvb = kv_s[krows, :].astype(BF16)
            kk, vv = kvb[:, :HEAD_DIM], kvb[:, HEAD_DIM:]
            do16 = _stack_heads([do_refs[j].at[0][qrows, :] for j in range(N_HALF)]).astype(BF16)
            lse = _stack_stats([lse_refs[j].at[0][qrows, :] for j in range(N_HALF)])
            delta = _stack_stats([dl_s[j][qrows, :] for j in range(N_HALF)])
            sc = lax.dot_general(qv, kk, (((1,), (1,)), ((), ())), preferred_element_type=F32)
            pr = jnp.exp(sc + bias_s[min(n, 1)] - lse)
            dv = lax.dot_general(pr.astype(BF16), do16, (((0,), (0,)), ((), ())), preferred_element_type=F32)
            dp = lax.dot_general(do16, vv, (((1,), (1,)), ((), ())), preferred_element_type=F32)
            ds = (pr * (dp - delta)).astype(BF16)
            dq = jnp.dot(ds, kk, preferred_element_type=F32)
            dk = lax.dot_general(ds, qv, (((0,), (0,)), ((), ())), preferred_element_type=F32)
            for j in range(N_HALF):
                dq_s[j][qrows, :] += _unstack_heads(dq, j)
            dkv_s[krows, :] += jnp.concatenate([dk, dv], axis=1)
        swap = _swap_matrix()
        cos_v, sin_v = tab_ref[0, :, :HEAD_DIM], tab_ref[0, :, HEAD_DIM:]
        for j in range(N_HALF):
            for e in range(2):
                cols = slice(e * HEAD_DIM, (e + 1) * HEAD_DIM)
                ocols = slice(j * HALF_W + e * HEAD_DIM, j * HALF_W + (e + 1) * HEAD_DIM)
                dq_ref[0, :, ocols] = _rope(dq_s[j][:, cols], cos_v, sin_v, swap, HEAD_DIM ** -0.5, True)
        dk_ref[0, 0] = _rope(dkv_s[ATTN_PAD:ATTN_PAD + s_len, :HEAD_DIM], cos_v, sin_v, swap, 1.0, True)
        dv_ref[0, 0] = dkv_s[ATTN_PAD:ATTN_PAD + s_len, HEAD_DIM:]

    kv_shape = jax.ShapeDtypeStruct((b, N_KV_HEADS, s_len, HEAD_DIM), F32)
    half_buf = pltpu.VMEM((s_len, HALF_W), F32)
    pad_buf = pltpu.VMEM((ATTN_PAD + s_len, HALF_W), F32)
    call = SideCopy(side, n_in=3 * N_HALF + 3, n_out=3, grid=(b, N_KV_HEADS))
    return pl.pallas_call(
        call.wrap(body), name=name, grid=(b, N_KV_HEADS),
        in_specs=q_specs + [kv_spec, t_spec, o_spec] + half_specs + half_specs + call.in_specs,
        out_specs=[o_spec, kv_out, kv_out] + call.out_specs,
        out_shape=[jax.ShapeDtypeStruct((b, s_len, ATTN_WIDTH), F32), kv_shape, kv_shape] + call.out_shape,
        scratch_shapes=[half_buf] * N_HALF + [pad_buf] + [half_buf] * (2 * N_HALF) + [pad_buf, _ATTN_BIAS_BUF] + call.scratch,
        compiler_params=pltpu.CompilerParams(dimension_semantics=("arbitrary", "arbitrary"), vmem_limit_bytes=ATTN_VMEM_LIMIT),
    )(*([proj3] * (N_HALF + 1)), rope_tab, attn3, *([lse3] * N_HALF), *([d_attn3] * N_HALF), *call.args)


CONV_TC = 256
CONV_COL0 = XBC_COL // CONV_TC


def _shift_down(u, s):
    if s == 0:
        return u
    rows = lax.broadcasted_iota(jnp.int32, u.shape, 0)
    return jnp.where(rows >= s, pltpu.roll(u, s, 0), 0.0)


def _shift_up(u, s):
    if s == 0:
        return u
    n = u.shape[0]
    rows = lax.broadcasted_iota(jnp.int32, u.shape, 0)
    return jnp.where(rows < n - s, pltpu.roll(u, n - s, 0), 0.0)


def conv_silu_fwd(proj3, w, bias, *, name):
    b, s, _ = proj3.shape
    u_spec = pl.BlockSpec((1, s, CONV_TC), lambda j, bi: (bi, 0, CONV_COL0 + j))
    o_spec = pl.BlockSpec((1, s, CONV_TC), lambda j, bi: (bi, 0, j))
    w_spec = pl.BlockSpec((CONV_WIDTH, CONV_TC), lambda j, bi: (0, j))
    b_spec = pl.BlockSpec((1, CONV_TC), lambda j, bi: (0, j))

    def body(u_ref, w_ref, b_ref, o_ref):
        u = u_ref[0]
        y = jnp.broadcast_to(b_ref[...], u.shape)
        for k in range(CONV_WIDTH):
            y = y + w_ref[k:k + 1, :] * _shift_down(u, CONV_WIDTH - 1 - k)
        o_ref[0] = y * jax.nn.sigmoid(y)

    return pl.pallas_call(
        body, name=name, grid=(CONV_CH // CONV_TC, b), in_specs=[u_spec, w_spec, b_spec], out_specs=o_spec,
        out_shape=jax.ShapeDtypeStruct((b, s, CONV_CH), F32),
        compiler_params=_params(("parallel", "arbitrary")),
    )(proj3, w, bias)


def conv_silu_bwd(proj3, w, bias, dact, *, name):
    b, s, _ = proj3.shape
    u_spec = pl.BlockSpec((1, s, CONV_TC), lambda j, bi: (bi, 0, CONV_COL0 + j))
    o_spec = pl.BlockSpec((1, s, CONV_TC), lambda j, bi: (bi, 0, j))
    w_spec = pl.BlockSpec((CONV_WIDTH, CONV_TC), lambda j, bi: (0, j))
    b_spec = pl.BlockSpec((1, CONV_TC), lambda j, bi: (0, j))

    def body(u_ref, w_ref, b_ref, g_ref, du_ref, dw_ref, db_ref):
        bi = pl.program_id(1)

        @pl.when(bi == 0)
        def _():
            dw_ref[...] = jnp.zeros_like(dw_ref)
            db_ref[...] = jnp.zeros_like(db_ref)

        u = u_ref[0]
        y = jnp.broadcast_to(b_ref[...], u.shape)
        shifted = [_shift_down(u, CONV_WIDTH - 1 - k) for k in range(CONV_WIDTH)]
        for k in range(CONV_WIDTH):
            y = y + w_ref[k:k + 1, :] * shifted[k]
        sig = jax.nn.sigmoid(y)
        dy = g_ref[0] * (sig * (1.0 + y * (1.0 - sig)))
        du = jnp.zeros_like(u)
        for k in range(CONV_WIDTH):
            du = du + w_ref[k:k + 1, :] * _shift_up(dy, CONV_WIDTH - 1 - k)
            dw_ref[k:k + 1, :] += jnp.sum(dy * shifted[k], axis=0, keepdims=True)
        du_ref[0] = du
        db_ref[...] += jnp.sum(dy, axis=0, keepdims=True)

    return pl.pallas_call(
        body, name=name, grid=(CONV_CH // CONV_TC, b), in_specs=[u_spec, w_spec, b_spec, o_spec],
        out_specs=[o_spec, w_spec, b_spec],
        out_shape=[jax.ShapeDtypeStruct((b, s, CONV_CH), F32), jax.ShapeDtypeStruct((CONV_WIDTH, CONV_CH), F32),
                   jax.ShapeDtypeStruct((1, CONV_CH), F32)],
        compiler_params=_params(("parallel", "arbitrary")),
    )(proj3, w, bias, dact)


def _softplus(z):
    e = jnp.exp(-jnp.abs(z))
    u = 1.0 + e
    log1p = jnp.where(u == 1.0, e, jnp.log(u) * e / jnp.where(u == 1.0, 1.0, u - 1.0))
    return jnp.maximum(z, 0.0) + log1p


def _tri(lower):
    r = lax.broadcasted_iota(jnp.int32, (CHUNK, CHUNK), 0)
    c = lax.broadcasted_iota(jnp.int32, (CHUNK, CHUNK), 1)
    return (r >= c) if lower else (r <= c)


def _ssd_common(dtr_ref, dtb_ref, alog_ref):
    z = dtr_ref[0] + dtb_ref[...]
    dt = _softplus(z)
    aneg = -jnp.exp(alog_ref[...])
    acs = _dot01_left(_tri(True).astype(BF16), dt * aneg)
    return z, dt, aneg, acs


def _col(mat, onehot):
    return jnp.sum(mat * onehot, axis=1, keepdims=True)


def _ssd_head(x, dt_j, acs_j, cb, tri_mask, last_row, acs_row=None):
    acs_last = jnp.sum(acs_j * last_row, axis=0, keepdims=True)
    xg = x * dt_j
    bc = jnp.broadcast_to(acs_j, (CHUNK, CHUNK))
    dm = bc - (bc.T if acs_row is None else jnp.broadcast_to(acs_row, (CHUNK, CHUNK)))
    lm = jnp.where(tri_mask, jnp.exp(jnp.where(tri_mask, dm, 0.0)), 0.0)
    mm = cb * lm
    decay_s = jnp.exp(acs_last - acs_j)
    return acs_last, xg, lm, mm, decay_s


def _ssd_specs(nc, reverse):
    cidx = (lambda c: nc - 1 - c) if reverse else (lambda c: c)
    act_spec = pl.BlockSpec((1, CHUNK, CONV_CH), lambda b, c: (b, cidx(c), 0))
    y_spec = pl.BlockSpec((1, CHUNK, SSM_INNER), lambda b, c: (b, cidx(c), 0))
    dt_in_spec = pl.BlockSpec((1, CHUNK, LANE), lambda b, c: (b, cidx(c), DT_COL // LANE))
    dt_out_spec = pl.BlockSpec((1, CHUNK, LANE), lambda b, c: (b, cidx(c), 0))
    par_spec = pl.BlockSpec((1, LANE), lambda b, c: (0, 0))
    h_spec = pl.BlockSpec((1, SSM_HEADS, 1, SSM_P, D_STATE), lambda b, c: (b, 0, cidx(c), 0, 0))
    return act_spec, y_spec, dt_in_spec, dt_out_spec, par_spec, h_spec


def _head_cols(h):
    return slice(h * SSM_P, (h + 1) * SSM_P)


def _group_cols(g, which):
    start = SSM_INNER + which * SSM_GROUPS * D_STATE + g * D_STATE
    return slice(start, start + D_STATE)


def ssd_fwd(act3, proj3, dtb, alog, dsk, *, name):
    b, s, _ = act3.shape
    nc = s // CHUNK
    act_spec, y_spec, dt_in_spec, _, par_spec, h_spec = _ssd_specs(nc, False)

    def body(act_ref, dtr_ref, dtb_ref, alog_ref, dsk_ref, y_ref, hp_ref, state):
        c = pl.program_id(1)

        @pl.when(c == 0)
        def _():
            state[...] = jnp.zeros_like(state)

        _, dt, _, acs = _ssd_common(dtr_ref, dtb_ref, alog_ref)
        acs_t = acs.T
        tri_mask = _tri(True)
        last_row = (lax.broadcasted_iota(jnp.int32, (CHUNK, 1), 0) == CHUNK - 1).astype(F32)
        for g in range(SSM_GROUPS):
            b16 = act_ref[0, :, _group_cols(g, 0)].astype(BF16)
            c16 = act_ref[0, :, _group_cols(g, 1)].astype(BF16)
            cb = lax.dot_general(c16, b16, (((1,), (1,)), ((), ())), preferred_element_type=F32)
            for j in range(HEADS_PER_GROUP):
                hidx = g * HEADS_PER_GROUP + j
                x = act_ref[0, :, _head_cols(hidx)]
                dt_j, acs_j = dt[:, hidx:hidx + 1], acs[:, hidx:hidx + 1]
                acs_last, xg, _, mm, decay_s = _ssd_head(x, dt_j, acs_j, cb, tri_mask, last_row, acs_t[hidx:hidx + 1, :])
                y_diag = jnp.dot(mm.astype(BF16), xg.astype(BF16), preferred_element_type=F32)
                st = lax.dot_general((xg * decay_s).astype(BF16), b16, (((0,), (0,)), ((), ())), preferred_element_type=F32)
                hp = state[hidx]
                hp_ref[0, hidx, 0] = hp
                y_off = lax.dot_general(c16, hp.astype(BF16), (((1,), (1,)), ((), ())), preferred_element_type=F32)
                d_j = dsk_ref[:, hidx:hidx + 1]
                y_ref[0, :, _head_cols(hidx)] = y_diag + y_off * jnp.exp(acs_j) + d_j * x
                state[hidx] = hp * jnp.exp(acs_last) + st

    return pl.pallas_call(
        body, name=name, grid=(b, nc),
        in_specs=[act_spec, dt_in_spec, par_spec, par_spec, par_spec],
        out_specs=[y_spec, h_spec],
        out_shape=[jax.ShapeDtypeStruct((b, s, SSM_INNER), F32),
                   jax.ShapeDtypeStruct((b, SSM_HEADS, nc, SSM_P, D_STATE), F32)],
        scratch_shapes=[pltpu.VMEM((SSM_HEADS, SSM_P, D_STATE), F32)],
        compiler_params=_params(("arbitrary", "arbitrary")),
    )(act3, proj3, dtb, alog, dsk)


def ssd_bwd(act3, proj3, dtb, alog, dsk, hprev, dy3, *, name):
    b, s, _ = act3.shape
    nc = s // CHUNK
    act_spec, y_spec, dt_in_spec, dt_out_spec, par_spec, h_spec = _ssd_specs(nc, True)
    dpar_spec = pl.BlockSpec((8, LANE), lambda bi, c: (0, 0))

    def body(act_ref, dtr_ref, dtb_ref, alog_ref, dsk_ref, hp_ref, dy_ref, dact_ref, ddtr_ref, dpar_ref, dstate):
        bi, c = pl.program_id(0), pl.program_id(1)

        @pl.when(c == 0)
        def _():
            dstate[...] = jnp.zeros_like(dstate)

        @pl.when((bi == 0) & (c == 0))
        def _():
            dpar_ref[...] = jnp.zeros_like(dpar_ref)

        z, dt, aneg, acs = _ssd_common(dtr_ref, dtb_ref, alog_ref)
        acs_t = acs.T
        tri_mask = _tri(True)
        last_row = (lax.broadcasted_iota(jnp.int32, (CHUNK, 1), 0) == CHUNK - 1).astype(F32)
        lanes = lax.broadcasted_iota(jnp.int32, (1, LANE), 1)
        sublanes = lax.broadcasted_iota(jnp.int32, (LANE, 1), 0)
        ddt_mat = jnp.zeros((CHUNK, LANE), F32)
        dacs_mat = jnp.zeros((CHUNK, LANE), F32)
        dacs_rows = jnp.zeros((LANE, CHUNK), F32)
        ddsk_row = jnp.zeros((1, LANE), F32)
        for g in range(SSM_GROUPS):
            b16 = act_ref[0, :, _group_cols(g, 0)].astype(BF16)
            c16 = act_ref[0, :, _group_cols(g, 1)].astype(BF16)
            cb = lax.dot_general(c16, b16, (((1,), (1,)), ((), ())), preferred_element_type=F32)
            dcb = jnp.zeros((CHUNK, CHUNK), F32)
            db_acc = jnp.zeros((CHUNK, D_STATE), F32)
            dc_acc = jnp.zeros((CHUNK, D_STATE), F32)
            for j in range(HEADS_PER_GROUP):
                hidx = g * HEADS_PER_GROUP + j
                onehot = (lanes == hidx).astype(F32)
                x = act_ref[0, :, _head_cols(hidx)]
                dt_j, acs_j = dt[:, hidx:hidx + 1], acs[:, hidx:hidx + 1]
                acs_last, xg, lm, mm, decay_s = _ssd_head(x, dt_j, acs_j, cb, tri_mask, last_row, acs_t[hidx:hidx + 1, :])
                ea = jnp.exp(acs_j)
                cd = jnp.exp(acs_last)
                d_j = dsk_ref[:, hidx:hidx + 1]
                hp = hp_ref[0, hidx, 0]
                hp16 = hp.astype(BF16)
                g_y = dy_ref[0, :, _head_cols(hidx)]
                g_y16 = g_y.astype(BF16)
                g_hn = dstate[hidx]
                g_hn16 = g_hn.astype(BF16)
                xg16 = xg.astype(BF16)
                ddsk_row = ddsk_row + jnp.sum(jnp.sum(g_y * x, axis=1, keepdims=True), axis=0, keepdims=True) * onehot
                d_mm = lax.dot_general(g_y16, xg16, (((1,), (1,)), ((), ())), preferred_element_type=F32)
                d_xg = lax.dot_general(mm.astype(BF16), g_y16, (((0,), (0,)), ((), ())), preferred_element_type=F32)
                dcb = dcb + d_mm * lm
                d_dm = d_mm * mm
                d_acs = jnp.sum(d_dm, axis=1, keepdims=True)
                dacs_rows = dacs_rows + (sublanes == hidx).astype(F32) * jnp.sum(d_dm, axis=0, keepdims=True)
                t_off = lax.dot_general(c16, hp16, (((1,), (1,)), ((), ())), preferred_element_type=F32)
                d_t16 = (g_y * ea).astype(BF16)
                d_acs = d_acs + jnp.sum(g_y * t_off, axis=1, keepdims=True) * ea
                dc_acc = dc_acc + jnp.dot(d_t16, hp16, preferred_element_type=F32)
                d_hp = lax.dot_general(d_t16, c16, (((0,), (0,)), ((), ())), preferred_element_type=F32) + g_hn * cd
                d_last = jnp.sum(jnp.sum(g_hn * hp, axis=1, keepdims=True), axis=0, keepdims=True) * cd
                d_w = lax.dot_general(b16, g_hn16, (((1,), (1,)), ((), ())), preferred_element_type=F32)
                db_acc = db_acc + jnp.dot((xg * decay_s).astype(BF16), g_hn16, preferred_element_type=F32)
                d_xg = d_xg + d_w * decay_s
                d_ds = jnp.sum(d_w * xg, axis=1, keepdims=True) * decay_s
                d_last = d_last + jnp.sum(d_ds, axis=0, keepdims=True)
                d_acs = d_acs - d_ds + d_last * last_row
                dact_ref[0, :, _head_cols(hidx)] = d_j * g_y + d_xg * dt_j
                ddt_mat = ddt_mat + jnp.sum(d_xg * x, axis=1, keepdims=True) * onehot
                dacs_mat = dacs_mat + d_acs * onehot
                dstate[hidx] = d_hp
            dcb16 = dcb.astype(BF16)
            dact_ref[0, :, _group_cols(g, 1)] = dc_acc + jnp.dot(dcb16, b16, preferred_element_type=F32)
            dact_ref[0, :, _group_cols(g, 0)] = db_acc + lax.dot_general(dcb16, c16, (((0,), (0,)), ((), ())),
                                                                         preferred_element_type=F32)
        d_a = _dot01_left(_tri(False).astype(BF16), dacs_mat - dacs_rows.T)
        ddt_mat = ddt_mat + d_a * aneg
        d_raw = ddt_mat * jax.nn.sigmoid(z)
        ddtr_ref[0] = d_raw
        dpar_ref[0:1, :] += jnp.sum(d_raw, axis=0, keepdims=True)
        dpar_ref[1:2, :] += jnp.sum(d_a * dt, axis=0, keepdims=True) * aneg
        dpar_ref[2:3, :] += ddsk_row

    return pl.pallas_call(
        body, name=name, grid=(b, nc),
        in_specs=[act_spec, dt_in_spec, par_spec, par_spec, par_spec, h_spec, y_spec],
        out_specs=[act_spec, dt_out_spec, dpar_spec],
        out_shape=[jax.ShapeDtypeStruct(act3.shape, F32), jax.ShapeDtypeStruct((b, s, LANE), F32),
                   jax.ShapeDtypeStruct((8, LANE), F32)],
        scratch_shapes=[pltpu.VMEM((SSM_HEADS, SSM_P, D_STATE), F32)],
        compiler_params=_params(("arbitrary", "arbitrary")),
    )(act3, proj3, dtb, alog, dsk, hprev, dy3)


def _unused_ssd_specs(nc, reverse):
    cidx = (lambda c: nc - 1 - c) if reverse else (lambda c: c)
    x_spec = pl.BlockSpec((1, HEADS_PER_GROUP, CHUNK, SSM_P), lambda b, c, g: (b, g, cidx(c), 0))
    bc_spec = pl.BlockSpec((1, 1, CHUNK, D_STATE), lambda b, c, g: (b, g, cidx(c), 0))
    dt_spec = pl.BlockSpec((1, CHUNK, LANE), lambda b, c, g: (b, cidx(c), 0))
    par_spec = pl.BlockSpec((1, LANE), lambda b, c, g: (0, 0))
    h_spec = pl.BlockSpec((1, HEADS_PER_GROUP, 1, SSM_P, D_STATE), lambda b, c, g: (b, g, cidx(c), 0, 0))
    return x_spec, bc_spec, dt_spec, par_spec, h_spec


def _unused_ssd_fwd(xs, bm, cm, dtr, dtb, alog, dsk, *, name):
    b, _, s, _ = xs.shape
    nc = s // CHUNK
    x_spec, bc_spec, dt_spec, par_spec, h_spec = _ssd_specs(nc, False)

    def body(x_ref, b_ref, c_ref, dtr_ref, dtb_ref, alog_ref, dsk_ref, y_ref, hp_ref, state):
        c, g = pl.program_id(1), pl.program_id(2)

        @pl.when(c == 0)
        def _():
            state[pl.ds(g * HEADS_PER_GROUP, HEADS_PER_GROUP)] = jnp.zeros((HEADS_PER_GROUP, SSM_P, D_STATE), F32)

        _, dt, _, acs = _ssd_common(dtr_ref, dtb_ref, alog_ref)
        b16, c16 = b_ref[0, 0].astype(BF16), c_ref[0, 0].astype(BF16)
        cb = lax.dot_general(c16, b16, (((1,), (1,)), ((), ())), preferred_element_type=F32)
        tri_mask = _tri(True)
        last_row = (lax.broadcasted_iota(jnp.int32, (CHUNK, 1), 0) == CHUNK - 1).astype(F32)
        lanes = lax.broadcasted_iota(jnp.int32, (1, LANE), 1)
        for j in range(HEADS_PER_GROUP):
            hidx = g * HEADS_PER_GROUP + j
            onehot = (lanes == hidx).astype(F32)
            x = x_ref[0, j]
            dt_j, acs_j = _col(dt, onehot), _col(acs, onehot)
            acs_last, xg, _, mm, decay_s = _ssd_head(x, dt_j, acs_j, cb, tri_mask, last_row)
            xg16 = xg.astype(BF16)
            y_diag = jnp.dot(mm.astype(BF16), xg16, preferred_element_type=F32)
            st = lax.dot_general((xg * decay_s).astype(BF16), b16, (((0,), (0,)), ((), ())), preferred_element_type=F32)
            hp = state[hidx]
            hp_ref[0, j, 0] = hp
            y_off = lax.dot_general(c16, hp.astype(BF16), (((1,), (1,)), ((), ())), preferred_element_type=F32)
            d_j = jnp.sum(dsk_ref[...] * onehot, axis=1, keepdims=True)
            y_ref[0, j] = y_diag + y_off * jnp.exp(acs_j) + d_j * x
            state[hidx] = hp * jnp.exp(acs_last) + st

    return pl.pallas_call(
        body, name=name, grid=(b, nc, SSM_GROUPS),
        in_specs=[x_spec, bc_spec, bc_spec, dt_spec, par_spec, par_spec, par_spec],
        out_specs=[x_spec, h_spec],
        out_shape=[jax.ShapeDtypeStruct(xs.shape, F32),
                   jax.ShapeDtypeStruct((b, SSM_HEADS, nc, SSM_P, D_STATE), F32)],
        scratch_shapes=[pltpu.VMEM((SSM_HEADS, SSM_P, D_STATE), F32)],
        compiler_params=_params(("arbitrary", "arbitrary", "arbitrary")),
    )(xs, bm, cm, dtr, dtb, alog, dsk)


def _unused_ssd_bwd(xs, bm, cm, dtr, dtb, alog, dsk, hprev, dy, *, name):
    b, _, s, _ = xs.shape
    nc = s // CHUNK
    x_spec, bc_spec, dt_spec, par_spec, h_spec = _ssd_specs(nc, True)
    dpar_spec = pl.BlockSpec((8, LANE), lambda bi, c, g: (0, 0))

    def body(x_ref, b_ref, c_ref, dtr_ref, dtb_ref, alog_ref, dsk_ref, hp_ref, dy_ref,
             dx_ref, db_ref, dc_ref, ddtr_ref, dpar_ref, dstate):
        bi, c, g = pl.program_id(0), pl.program_id(1), pl.program_id(2)

        @pl.when(c == 0)
        def _():
            dstate[pl.ds(g * HEADS_PER_GROUP, HEADS_PER_GROUP)] = jnp.zeros((HEADS_PER_GROUP, SSM_P, D_STATE), F32)

        @pl.when((bi == 0) & (c == 0) & (g == 0))
        def _():
            dpar_ref[...] = jnp.zeros_like(dpar_ref)

        z, dt, aneg, acs = _ssd_common(dtr_ref, dtb_ref, alog_ref)
        bv, cv = b_ref[0, 0], c_ref[0, 0]
        b16, c16 = bv.astype(BF16), cv.astype(BF16)
        cb = lax.dot_general(c16, b16, (((1,), (1,)), ((), ())), preferred_element_type=F32)
        tri_mask = _tri(True)
        last_row = (lax.broadcasted_iota(jnp.int32, (CHUNK, 1), 0) == CHUNK - 1).astype(F32)
        lanes = lax.broadcasted_iota(jnp.int32, (1, LANE), 1)
        dcb = jnp.zeros((CHUNK, CHUNK), F32)
        db_acc = jnp.zeros((CHUNK, D_STATE), F32)
        dc_acc = jnp.zeros((CHUNK, D_STATE), F32)
        ddt_mat = jnp.zeros((CHUNK, LANE), F32)
        dacs_mat = jnp.zeros((CHUNK, LANE), F32)
        ddsk_row = jnp.zeros((1, LANE), F32)
        for j in range(HEADS_PER_GROUP):
            hidx = g * HEADS_PER_GROUP + j
            onehot = (lanes == hidx).astype(F32)
            x = x_ref[0, j]
            dt_j, acs_j = _col(dt, onehot), _col(acs, onehot)
            acs_last, xg, lm, mm, decay_s = _ssd_head(x, dt_j, acs_j, cb, tri_mask, last_row)
            ea = jnp.exp(acs_j)
            cd = jnp.exp(acs_last)
            d_j = jnp.sum(dsk_ref[...] * onehot, axis=1, keepdims=True)
            hp = hp_ref[0, j, 0]
            hp16 = hp.astype(BF16)
            g_y = dy_ref[0, j]
            g_y16 = g_y.astype(BF16)
            g_hn = dstate[hidx]
            g_hn16 = g_hn.astype(BF16)
            xg16 = xg.astype(BF16)
            ddsk_row = ddsk_row + jnp.sum(jnp.sum(g_y * x, axis=1, keepdims=True), axis=0, keepdims=True) * onehot
            d_mm = lax.dot_general(g_y16, xg16, (((1,), (1,)), ((), ())), preferred_element_type=F32)
            d_xg = lax.dot_general(mm.astype(BF16), g_y16, (((0,), (0,)), ((), ())), preferred_element_type=F32)
            dcb = dcb + d_mm * lm
            d_dm = d_mm * mm
            d_acs = jnp.sum(d_dm, axis=1, keepdims=True) - jnp.sum(d_dm.T, axis=1, keepdims=True)
            t_off = lax.dot_general(c16, hp16, (((1,), (1,)), ((), ())), preferred_element_type=F32)
            d_t16 = (g_y * ea).astype(BF16)
            d_acs = d_acs + jnp.sum(g_y * t_off, axis=1, keepdims=True) * ea
            dc_acc = dc_acc + jnp.dot(d_t16, hp16, preferred_element_type=F32)
            d_hp = lax.dot_general(d_t16, c16, (((0,), (0,)), ((), ())), preferred_element_type=F32) + g_hn * cd
            d_last = jnp.sum(jnp.sum(g_hn * hp, axis=1, keepdims=True), axis=0, keepdims=True) * cd
            d_w = lax.dot_general(b16, g_hn16, (((1,), (1,)), ((), ())), preferred_element_type=F32)
            db_acc = db_acc + jnp.dot((xg * decay_s).astype(BF16), g_hn16, preferred_element_type=F32)
            d_xg = d_xg + d_w * decay_s
            d_ds = jnp.sum(d_w * xg, axis=1, keepdims=True) * decay_s
            d_last = d_last + jnp.sum(d_ds, axis=0, keepdims=True)
            d_acs = d_acs - d_ds + d_last * last_row
            dx_ref[0, j] = d_j * g_y + d_xg * dt_j
            ddt_mat = ddt_mat + jnp.sum(d_xg * x, axis=1, keepdims=True) * onehot
            dacs_mat = dacs_mat + d_acs * onehot
            dstate[hidx] = d_hp
        dcb16 = dcb.astype(BF16)
        dc_ref[0, 0] = dc_acc + jnp.dot(dcb16, b16, preferred_element_type=F32)
        db_ref[0, 0] = db_acc + lax.dot_general(dcb16, c16, (((0,), (0,)), ((), ())), preferred_element_type=F32)
        d_a = _dot01_left(_tri(False).astype(BF16), dacs_mat)
        ddt_mat = ddt_mat + d_a * aneg
        d_aneg = jnp.sum(d_a * dt, axis=0, keepdims=True)
        d_raw = ddt_mat * jax.nn.sigmoid(z)

        @pl.when(g == 0)
        def _():
            ddtr_ref[0] = d_raw

        @pl.when(g != 0)
        def _():
            ddtr_ref[0] += d_raw

        dpar_ref[0:1, :] += jnp.sum(d_raw, axis=0, keepdims=True)
        dpar_ref[1:2, :] += d_aneg * aneg
        dpar_ref[2:3, :] += ddsk_row

    return pl.pallas_call(
        body, name=name, grid=(b, nc, SSM_GROUPS),
        in_specs=[x_spec, bc_spec, bc_spec, dt_spec, par_spec, par_spec, par_spec, h_spec, x_spec],
        out_specs=[x_spec, bc_spec, bc_spec, dt_spec, dpar_spec],
        out_shape=[jax.ShapeDtypeStruct(xs.shape, F32), jax.ShapeDtypeStruct(bm.shape, F32),
                   jax.ShapeDtypeStruct(cm.shape, F32), jax.ShapeDtypeStruct(dtr.shape, F32),
                   jax.ShapeDtypeStruct((8, LANE), F32)],
        scratch_shapes=[pltpu.VMEM((SSM_HEADS, SSM_P, D_STATE), F32)],
        compiler_params=_params(("arbitrary", "arbitrary", "arbitrary")),
    )(xs, bm, cm, dtr, dtb, alog, dsk, hprev, dy)


SSD_INTERLEAVE = 8


def _each(f, *lists):
    return [f(*a) for a in zip(*lists)]


def _nt(a, b):
    return lax.dot_general(a, b, (((1,), (1,)), ((), ())), preferred_element_type=F32)


def _tn(a, b):
    return lax.dot_general(a, b, (((0,), (0,)), ((), ())), preferred_element_type=F32)


def _nn(a, b):
    return jnp.dot(a, b, preferred_element_type=F32)


def _rowsum(a):
    return jnp.sum(a, axis=1, keepdims=True)


def _colsum(a):
    return jnp.sum(a, axis=0, keepdims=True)


def _bf(a):
    return a.astype(BF16)


def _head_batches(g):
    first = g * HEADS_PER_GROUP
    return [list(range(first + k, first + k + SSD_INTERLEAVE)) for k in range(0, HEADS_PER_GROUP, SSD_INTERLEAVE)]


def _decay_matrix(acs_j, acs_row, tri_mask):
    dm = jnp.broadcast_to(acs_j, (CHUNK, CHUNK)) - jnp.broadcast_to(acs_row, (CHUNK, CHUNK))
    return jnp.where(tri_mask, jnp.exp(jnp.where(tri_mask, dm, 0.0)), 0.0)


def ssd_fwd(act3, proj3, dtb, alog, dsk, *, name, side=None):
    b, s, _ = act3.shape
    nc = s // CHUNK
    act_spec, y_spec, dt_in_spec, _, par_spec, h_spec = _ssd_specs(nc, False)

    def body(act_ref, dtr_ref, dtb_ref, alog_ref, dsk_ref, y_ref, hp_ref, state):
        c = pl.program_id(1)

        @pl.when(c == 0)
        def _():
            state[...] = jnp.zeros_like(state)

        _, dt, _, acs = _ssd_common(dtr_ref, dtb_ref, alog_ref)
        acs_t = acs.T
        tri_mask = _tri(True)
        last_row = (lax.broadcasted_iota(jnp.int32, (CHUNK, 1), 0) == CHUNK - 1).astype(F32)
        for g in range(SSM_GROUPS):
            b16 = _bf(act_ref[0, :, _group_cols(g, 0)])
            c16 = _bf(act_ref[0, :, _group_cols(g, 1)])
            cb = _nt(c16, b16)
            for hs in _head_batches(g):
                x = [act_ref[0, :, _head_cols(h)] for h in hs]
                dt_j = [dt[:, h:h + 1] for h in hs]
                acs_j = [acs[:, h:h + 1] for h in hs]
                acs_last = [_colsum(a * last_row) for a in acs_j]
                xg = _each(lambda xv, d: xv * d, x, dt_j)
                mm = [cb * _decay_matrix(a, acs_t[h:h + 1, :], tri_mask) for a, h in zip(acs_j, hs)]
                decay_s = _each(lambda al, a: jnp.exp(al - a), acs_last, acs_j)
                y_diag = _each(lambda m_, v: _nn(_bf(m_), _bf(v)), mm, xg)
                st = _each(lambda v, d: _tn(_bf(v * d), b16), xg, decay_s)
                hp = [state[h] for h in hs]
                for h, v in zip(hs, hp):
                    hp_ref[0, h, 0] = v
                y_off = [_nt(c16, _bf(v)) for v in hp]
                for h, yd, yo, a, xv in zip(hs, y_diag, y_off, acs_j, x):
                    y_ref[0, :, _head_cols(h)] = yd + yo * jnp.exp(a) + dsk_ref[:, h:h + 1] * xv
                for h, v, al, sv in zip(hs, hp, acs_last, st):
                    state[h] = v * jnp.exp(al) + sv

    call = SideCopy(side, n_in=5, n_out=2, grid=(b, nc))
    return pl.pallas_call(
        call.wrap(body), name=name, grid=(b, nc),
        in_specs=[act_spec, dt_in_spec, par_spec, par_spec, par_spec] + call.in_specs,
        out_specs=[y_spec, h_spec] + call.out_specs,
        out_shape=[jax.ShapeDtypeStruct((b, s, SSM_INNER), F32),
                   jax.ShapeDtypeStruct((b, SSM_HEADS, nc, SSM_P, D_STATE), F32)] + call.out_shape,
        scratch_shapes=[pltpu.VMEM((SSM_HEADS, SSM_P, D_STATE), F32)] + call.scratch,
        compiler_params=_params(("arbitrary", "arbitrary")),
    )(act3, proj3, dtb, alog, dsk, *call.args)


def ssd_bwd(act3, proj3, dtb, alog, dsk, hprev, dy3, *, name, side=None):
    b, s, _ = act3.shape
    nc = s // CHUNK
    act_spec, y_spec, dt_in_spec, dt_out_spec, par_spec, h_spec = _ssd_specs(nc, True)
    dpar_spec = pl.BlockSpec((8, LANE), lambda bi, c: (0, 0))

    def body(act_ref, dtr_ref, dtb_ref, alog_ref, dsk_ref, hp_ref, dy_ref, dact_ref, ddtr_ref, dpar_ref, dstate):
        bi, c = pl.program_id(0), pl.program_id(1)

        @pl.when(c == 0)
        def _():
            dstate[...] = jnp.zeros_like(dstate)

        @pl.when((bi == 0) & (c == 0))
        def _():
            dpar_ref[...] = jnp.zeros_like(dpar_ref)

        z, dt, aneg, acs = _ssd_common(dtr_ref, dtb_ref, alog_ref)
        acs_t = acs.T
        tri_mask = _tri(True)
        last_row = (lax.broadcasted_iota(jnp.int32, (CHUNK, 1), 0) == CHUNK - 1).astype(F32)
        lanes = lax.broadcasted_iota(jnp.int32, (1, LANE), 1)
        sublanes = lax.broadcasted_iota(jnp.int32, (LANE, 1), 0)
        ddt_mat = jnp.zeros((CHUNK, LANE), F32)
        dacs_mat = jnp.zeros((CHUNK, LANE), F32)
        dacs_rows = jnp.zeros((LANE, CHUNK), F32)
        ddsk_row = jnp.zeros((1, LANE), F32)
        for g in range(SSM_GROUPS):
            b16 = _bf(act_ref[0, :, _group_cols(g, 0)])
            c16 = _bf(act_ref[0, :, _group_cols(g, 1)])
            cb = _nt(c16, b16)
            dcb = jnp.zeros((CHUNK, CHUNK), F32)
            db_acc = jnp.zeros((CHUNK, D_STATE), F32)
            dc_acc = jnp.zeros((CHUNK, D_STATE), F32)
            for hs in _head_batches(g):
                x = [act_ref[0, :, _head_cols(h)] for h in hs]
                g_y = [dy_ref[0, :, _head_cols(h)] for h in hs]
                hp = [hp_ref[0, h, 0] for h in hs]
                g_hn = [dstate[h] for h in hs]
                dt_j = [dt[:, h:h + 1] for h in hs]
                acs_j = [acs[:, h:h + 1] for h in hs]
                acs_last = [_colsum(a * last_row) for a in acs_j]
                xg = _each(lambda xv, d: xv * d, x, dt_j)
                lm = [_decay_matrix(a, acs_t[h:h + 1, :], tri_mask) for a, h in zip(acs_j, hs)]
                mm = [cb * l for l in lm]
                decay_s = _each(lambda al, a: jnp.exp(al - a), acs_last, acs_j)
                ea = [jnp.exp(a) for a in acs_j]
                cd = [jnp.exp(al) for al in acs_last]
                g_y16, xg16, hp16, g_hn16 = [[_bf(v) for v in vs] for vs in (g_y, xg, hp, g_hn)]
                d_mm = _each(_nt, g_y16, xg16)
                d_xg = _each(lambda m_, gy: _tn(_bf(m_), gy), mm, g_y16)
                d_dm = _each(lambda a, m_: a * m_, d_mm, mm)
                d_acs = [_rowsum(v) for v in d_dm]
                t_off = [_nt(c16, v) for v in hp16]
                d_t16 = _each(lambda gy, e: _bf(gy * e), g_y, ea)
                d_acs = _each(lambda da, gy, t, e: da + _rowsum(gy * t) * e, d_acs, g_y, t_off, ea)
                d_hp = _each(lambda dtv, gh, cdv: _tn(dtv, c16) + gh * cdv, d_t16, g_hn, cd)
                d_w = [_nt(b16, v) for v in g_hn16]
                d_xg = _each(lambda dx, dw, ds: dx + dw * ds, d_xg, d_w, decay_s)
                d_ds = _each(lambda dw, v, ds: _rowsum(dw * v) * ds, d_w, xg, decay_s)
                d_last = _each(lambda gh, hv, cdv, dd: _colsum(_rowsum(gh * hv)) * cdv + _colsum(dd), g_hn, hp, cd, d_ds)
                d_acs = _each(lambda da, dd, dl: da - dd + dl * last_row, d_acs, d_ds, d_last)
                for h, gy, dx, d, xv in zip(hs, g_y, d_xg, dt_j, x):
                    dact_ref[0, :, _head_cols(h)] = dsk_ref[:, h:h + 1] * gy + dx * d
                for h, v in zip(hs, d_hp):
                    dstate[h] = v
                for k, h in enumerate(hs):
                    onehot = (lanes == h).astype(F32)
                    dcb = dcb + d_mm[k] * lm[k]
                    dc_acc = dc_acc + _nn(d_t16[k], hp16[k])
                    db_acc = db_acc + _nn(_bf(xg[k] * decay_s[k]), g_hn16[k])
                    ddsk_row = ddsk_row + _colsum(_rowsum(g_y[k] * x[k])) * onehot
                    ddt_mat = ddt_mat + _rowsum(d_xg[k] * x[k]) * onehot
                    dacs_mat = dacs_mat + d_acs[k] * onehot
                    dacs_rows = dacs_rows + (sublanes == h).astype(F32) * _colsum(d_dm[k])
            dcb16 = _bf(dcb)
            dact_ref[0, :, _group_cols(g, 1)] = dc_acc + _nn(dcb16, b16)
            dact_ref[0, :, _group_cols(g, 0)] = db_acc + _tn(dcb16, c16)
        d_a = _dot01_left(_tri(False).astype(BF16), dacs_mat - dacs_rows.T)
        ddt_mat = ddt_mat + d_a * aneg
        d_raw = ddt_mat * jax.nn.sigmoid(z)
        ddtr_ref[0] = d_raw
        dpar_ref[0:1, :] += _colsum(d_raw)
        dpar_ref[1:2, :] += _colsum(d_a * dt) * aneg
        dpar_ref[2:3, :] += ddsk_row

    call = SideCopy(side, n_in=7, n_out=3, grid=(b, nc))
    return pl.pallas_call(
        call.wrap(body), name=name, grid=(b, nc),
        in_specs=[act_spec, dt_in_spec, par_spec, par_spec, par_spec, h_spec, y_spec] + call.in_specs,
        out_specs=[act_spec, dt_out_spec, dpar_spec] + call.out_specs,
        out_shape=[jax.ShapeDtypeStruct(act3.shape, F32), jax.ShapeDtypeStruct((b, s, LANE), F32),
                   jax.ShapeDtypeStruct((8, LANE), F32)] + call.out_shape,
        scratch_shapes=[pltpu.VMEM((SSM_HEADS, SSM_P, D_STATE), F32)] + call.scratch,
        compiler_params=_params(("arbitrary", "arbitrary")),
    )(act3, proj3, dtb, alog, dsk, hprev, dy3, *call.args)


def to_heads(x, b, s, h):
    return x.reshape(b, s, h, -1).transpose(0, 2, 1, 3)


def from_heads(x):
    b, h, s, c = x.shape
    return x.transpose(0, 2, 1, 3).reshape(b * s, h * c)


def dilate_q(q, d):
    b, _, s, c = q.shape
    x = q.reshape(b, N_KV_HEADS, GQA, s // d, d, c).transpose(0, 1, 4, 2, 3, 5)
    return x.reshape(b * N_KV_HEADS * d, GQA, s // d, c)


def undilate_q(x, b, d):
    _, _, l, c = x.shape
    y = x.reshape(b, N_KV_HEADS, d, GQA, l, c).transpose(0, 1, 3, 4, 2, 5)
    return y.reshape(b, N_Q_HEADS, l * d, c)


def dilate_kv(k, d):
    b, h, s, c = k.shape
    return k.reshape(b, h, s // d, d, c).transpose(0, 1, 3, 2, 4).reshape(b * h * d, s // d, c)


def undilate_kv(x, b, d):
    _, l, c = x.shape
    return x.reshape(b, N_KV_HEADS, d, l, c).transpose(0, 1, 3, 2, 4).reshape(b, N_KV_HEADS, l * d, c)


def rotary_tables(positions):
    inv_freq = ROPE_THETA ** (-jnp.arange(0, ROPE_DIM, 2, dtype=F32) / ROPE_DIM)
    ang = positions.astype(F32)[..., None] * inv_freq
    cos, sin = jnp.cos(ang), jnp.sin(ang)
    rest = HEAD_DIM - ROPE_DIM
    cosf = jnp.concatenate([cos, cos, jnp.ones(cos.shape[:2] + (rest,), F32)], axis=-1)
    sinf = jnp.concatenate([-sin, sin, jnp.zeros(sin.shape[:2] + (rest,), F32)], axis=-1)
    return cosf, sinf


def w_in_columns(w):
    pad = jnp.zeros((w.shape[0], IN_PAD - IN_PROJ), w.dtype)
    return jnp.concatenate([w[:, :Q_END], w[:, V_END:XBC_END], w[:, Q_END:V_END], w[:, XBC_END:], pad], axis=1)


def w_in_grad_columns(g):
    return jnp.concatenate([g[:, :Z_COL], g[:, K_COL:DT_COL], g[:, Z_COL:K_COL], g[:, DT_COL:DT_COL + SSM_HEADS]], axis=1)


def lane_pad(v):
    return jnp.pad(v.reshape(1, -1), ((0, 0), (0, LANE - v.shape[-1])))


def layer_fwd(h, wts, small, rope_tab, b, s, tag, attn_side=None, rest_from=None, ssd_side=None):
    w_in = wts[0]
    t = b * s
    sv = {"h": h}
    hn = rowwise_fwd(rms_fn, [h], [small["norm_mix"]], [BF16], name=f"rms_mix_{tag}")[0]
    proj = matmul(hn, w_in, name=f"in_proj_{tag}")
    sv["hn"], sv["proj"] = hn, proj
    proj3 = proj.reshape(b, s, IN_PAD)
    attn3, lse3, *attn_out = attn_fwd(proj3, rope_tab, name=f"attn_{tag}", side=attn_side)
    if rest_from is not None:
        wts = (w_in,) + tuple(rest_from(attn_out))
    _, w_out, w_gate, w_up, w_down = wts
    sv["attn3"], sv["lse3"] = attn3, lse3
    attn = attn3.reshape(t, ATTN_WIDTH)
    act3 = conv_silu_fwd(proj3, small["conv_w"], small["conv_b"], name=f"conv_{tag}")
    y3, hprev, *ssd_out = ssd_fwd(act3, proj3, small["dt_bias"], small["a_log"], small["d_skip"], name=f"ssd_{tag}",
                                  side=ssd_side)
    y = y3.reshape(t, SSM_INNER)
    sv["act3"], sv["hprev"], sv["y"] = act3, hprev, y
    gn = rowwise_fwd(gated_norm_fn, [y, proj], [small["ssm_norm"]], [F32], name=f"gated_norm_{tag}", groups=SSM_GROUPS,
                     windows=[None, (Z_COL, SSM_INNER)])[0]
    sv["gn"] = gn
    h1 = matmul([attn, gn], w_out, name=f"out_proj_{tag}", residual=h)
    sv["h1"] = h1
    hn2 = rowwise_fwd(rms_fn, [h1], [small["norm_ffn"]], [BF16], name=f"rms_ffn_{tag}")[0]
    gate = matmul(hn2, w_gate, out_dtype=BF16, name=f"ffn_gate_{tag}")
    up = matmul(hn2, w_up, out_dtype=BF16, name=f"ffn_up_{tag}")
    act2 = rowwise_fwd(swiglu_fn, [gate, up], [], [BF16], name=f"swiglu_{tag}")[0]
    sv["hn2"], sv["gate"], sv["up"], sv["act2"] = hn2, gate, up, act2
    h2 = matmul(act2, w_down, name=f"ffn_down_{tag}", residual=h1)
    return h2, sv, wts, (ssd_out or None)


def layer_bwd(dh2, sv, wts, small, rope_tab, b, s, tag, ssd_side=None, attn_side_fn=None):
    w_in, w_out, w_gate, w_up, w_down = wts
    t = b * s
    gr = {}
    d_act2 = matmul(dh2, w_down, tb=True, out_dtype=BF16, name=f"ffn_down_dx_{tag}")
    gr["w_down"] = matmul(sv["act2"], dh2, ta=True, out_dtype=BF16, name=f"ffn_down_dw_{tag}")
    d_gate, d_up = rowwise_bwd(swiglu_fn, [sv["gate"], sv["up"]], [], [d_act2], [BF16, BF16], name=f"swiglu_bwd_{tag}")
    gr["w_gate"] = matmul(sv["hn2"], d_gate, ta=True, out_dtype=BF16, name=f"ffn_gate_dw_{tag}")
    gr["w_up"] = matmul(sv["hn2"], d_up, ta=True, out_dtype=BF16, name=f"ffn_up_dw_{tag}")
    d_hn2 = matmul(d_gate, w_gate, tb=True, name=f"ffn_gate_dx_{tag}")
    d_hn2 = matmul(d_up, w_up, tb=True, residual=d_hn2, name=f"ffn_up_dx_{tag}")
    dh1, gr["norm_ffn"] = rowwise_bwd(rms_fn, [sv["h1"]], [small["norm_ffn"]], [d_hn2], [F32],
                                      name=f"rms_ffn_bwd_{tag}", add_to_first=dh2)
    d_cat = matmul(dh1, w_out, tb=True, name=f"out_proj_dx_{tag}")
    gr["w_out"] = jnp.concatenate([
        matmul(sv["attn3"].reshape(t, ATTN_WIDTH), dh1, ta=True, out_dtype=BF16, name=f"out_proj_dw_attn_{tag}"),
        matmul(sv["gn"], dh1, ta=True, out_dtype=BF16, name=f"out_proj_dw_ssd_{tag}")], axis=0)
    d_y, d_z, gr["ssm_norm"] = rowwise_bwd(gated_norm_fn, [sv["y"], sv["proj"]], [small["ssm_norm"]], [d_cat], [F32, F32],
                                           name=f"gated_norm_bwd_{tag}", groups=SSM_GROUPS,
                                           windows=[None, (Z_COL, SSM_INNER)], ct_windows=[(ATTN_WIDTH, SSM_INNER)])
    proj3 = sv["proj"].reshape(b, s, IN_PAD)
    d_act3, d_dtr, d_par, *ssd_out = ssd_bwd(sv["act3"], proj3, small["dt_bias"], small["a_log"], small["d_skip"],
                                             sv["hprev"], d_y.reshape(b, s, SSM_INNER), name=f"ssd_bwd_{tag}", side=ssd_side)
    gr["dt_bias"], gr["a_log"], gr["d_skip"] = d_par[0, :SSM_HEADS], d_par[1, :SSM_HEADS], d_par[2, :SSM_HEADS]
    d_xbc, gr["conv_w"], gr["conv_b"] = conv_silu_bwd(proj3, small["conv_w"], small["conv_b"], d_act3,
                                                      name=f"conv_bwd_{tag}")
    attn_side = attn_side_fn(gr) if attn_side_fn is not None else None
    d_q3, d_k4, d_v4, *attn_out = attn_bwd(proj3, rope_tab, sv["attn3"], sv["lse3"], d_cat.reshape(b, s, MIX_WIDTH),
                                           name=f"attn_bwd_{tag}", side=attn_side)
    d_tail = jnp.concatenate([from_heads(d_k4), from_heads(d_v4), d_dtr.reshape(t, LANE)], axis=1)
    d_proj = [d_q3.reshape(t, ATTN_WIDTH), d_z, d_xbc.reshape(t, CONV_CH), d_tail]
    d_hn = matmul(d_proj, w_in, tb=True, name=f"in_proj_dx_{tag}")
    gr["w_in"] = w_in_grad_columns(jnp.concatenate(
        [matmul(sv["hn"], part, ta=True, out_dtype=BF16, name=f"in_proj_dw_{k}_{tag}") for k, part in enumerate(d_proj)],
        axis=1))
    dh, gr["norm_mix"] = rowwise_bwd(rms_fn, [sv["h"]], [small["norm_mix"]], [d_hn], [F32],
                                     name=f"rms_mix_bwd_{tag}", add_to_first=dh1)
    return dh, gr, (ssd_out or None), (attn_out or None)


def local_step(x, positions, big, small_all, final_norm, loss_target, *, plan=None):
    b, s, _ = x.shape
    t = b * s
    rope_tab = jnp.concatenate(rotary_tables(positions), axis=-1)
    h = x.reshape(t, D_MODEL)
    saved, big = [], list(big)
    for l in range(DEPTH):
        kw = {}
        if plan is not None and l == 0:
            kw = dict(attn_side=(plan["rest0"], False), rest_from=plan["make_rest0"], ssd_side=(plan["late"], False))
        h, sv, big[l], got = layer_fwd(h, big[l], small_all[l], rope_tab, b, s, f"l{l}", **kw)
        if got is not None:
            big[DEPTH - 1] = plan["make_late"](got)
        saved.append(sv)
    dh, d_final, loss = loss_and_grad(h, loss_target.reshape(t, D_MODEL), final_norm.reshape(1, D_MODEL))
    grads, received = [None] * DEPTH, {}
    for l in reversed(range(DEPTH)):
        kw = {}
        if plan is not None and l == 0:
            kw = dict(ssd_side=(plan["grads_late"](grads[DEPTH - 1]), True),
                      attn_side_fn=lambda gr: (plan["grads_rest0"](gr), True))
        dh, grads[l], got_ssd, got_attn = layer_bwd(dh, saved[l], big[l], small_all[l], rope_tab, b, s, f"l{l}", **kw)
        if got_ssd is not None:
            received["late"] = got_ssd
        if got_attn is not None:
            received["rest0"] = got_attn
    return loss, dh.reshape(b, s, D_MODEL), grads, d_final, received


def _slab_rows(r):
    return r if r <= 512 else _pick(r, (512, 352, 256, 128, 8))


def cast_bf16(x, *, name):
    def fn(v):
        return (v,)
    return rowwise_fwd(fn, [x], [], [BF16], name=name, tr=_slab_rows(x.shape[0]))[0]


def sum_slots(x, *, name):
    n, r, c = x.shape
    tr = _slab_rows(r)

    def body(x_ref, o_ref):
        acc = x_ref[0].astype(F32)
        for i in range(1, n):
            acc = acc + x_ref[i].astype(F32)
        o_ref[...] = acc

    return pl.pallas_call(
        body, name=name, grid=(r // tr,), in_specs=[pl.BlockSpec((n, tr, c), lambda i: (0, i, 0))],
        out_specs=pl.BlockSpec((tr, c), lambda i: (i, 0)), out_shape=jax.ShapeDtypeStruct((r, c), F32),
        compiler_params=_params(("parallel",)),
    )(x)


def adamw(g_parts, w, m, v, *, name, with_grad=True):
    r, c = w.shape
    tr = _slab_rows(r)
    n_g = len(g_parts)
    n_out = 4 if with_grad else 3
    bc1 = 1.0 / (1.0 - ADAM_B1 ** ADAM_STEP)
    bc2 = 1.0 / (1.0 - ADAM_B2 ** ADAM_STEP)

    def body(*refs):
        g = refs[0][...]
        for r_ in refs[1:n_g]:
            g = g + r_[...]
        w_ref, m_ref, v_ref = refs[n_g:n_g + 3]
        d_out, m_out, v_out = refs[-3:]
        m_new = ADAM_B1 * m_ref[...] + (1.0 - ADAM_B1) * g
        v_new = ADAM_B2 * v_ref[...] + (1.0 - ADAM_B2) * (g * g)
        if with_grad:
            refs[n_g + 3][...] = g
        m_out[...] = m_new
        v_out[...] = v_new
        d_out[...] = -ADAM_LR * ((m_new * bc1) / (jnp.sqrt(v_new * bc2) + ADAM_EPS) + ADAM_WD * w_ref[...])

    spec = pl.BlockSpec((tr, c), lambda i: (i, 0))
    return pl.pallas_call(
        body, name=name, grid=(r // tr,), in_specs=[spec] * (n_g + 3), out_specs=[spec] * n_out,
        out_shape=[jax.ShapeDtypeStruct((r, c), F32)] * n_out, compiler_params=_params(("parallel",)),
    )(*g_parts, w, m, v)


def _other_chips(x, y):
    return [(1 - x, y), (x, 1 - y), (1 - x, 1 - y)]


def allgather_chips(shards):
    n_arr = len(shards)

    def body(*refs):
        in_refs, out_refs = refs[:n_arr], refs[n_arr:2 * n_arr]
        send_sems, recv_sems, local_sems = refs[2 * n_arr:]
        x, y, c = lax.axis_index("x"), lax.axis_index("y"), lax.axis_index("c")
        chip = 2 * x + y
        started = []
        for a, (in_ref, out_ref) in enumerate(zip(in_refs, out_refs)):
            mine = pltpu.make_async_copy(in_ref, out_ref.at[chip], local_sems.at[a])
            mine.start()
            started.append(mine.wait)
            for k, (px, py) in enumerate(_other_chips(x, y)):
                cp = pltpu.make_async_remote_copy(src_ref=in_ref, dst_ref=out_ref.at[chip], send_sem=send_sems.at[3 * a + k],
                                                  recv_sem=recv_sems.at[3 * a + k], device_id=(px, py, c), device_id_type=MESH)
                cp.start()
                started.append(cp.wait_send)
        for a, (in_ref, out_ref) in enumerate(zip(in_refs, out_refs)):
            for k, (px, py) in enumerate(_other_chips(x, y)):
                pltpu.make_async_remote_copy(src_ref=in_ref, dst_ref=out_ref.at[2 * px + py], send_sem=send_sems.at[3 * a + k],
                                             recv_sem=recv_sems.at[3 * a + k], device_id=(px, py, c),
                                             device_id_type=MESH).wait_recv()
        for wait in started:
            wait()

    hbm = pl.BlockSpec(memory_space=pltpu.HBM)
    return pl.pallas_call(
        body, name="allgather_weights", in_specs=[hbm] * n_arr, out_specs=[hbm] * n_arr,
        out_shape=[jax.ShapeDtypeStruct((N_CHIPS,) + s.shape, s.dtype) for s in shards],
        scratch_shapes=[pltpu.SemaphoreType.DMA((3 * n_arr,)), pltpu.SemaphoreType.DMA((3 * n_arr,)),
                        pltpu.SemaphoreType.DMA((n_arr,))],
    )(*shards)


def exchange_grads(big, small):
    def body(big_ref, small_ref, big_out, small_out, send_sems, recv_sems, local_sems):
        x, y, c = lax.axis_index("x"), lax.axis_index("y"), lax.axis_index("c")
        chip = 2 * x + y
        dev = 4 * x + 2 * y + c
        own_big = pltpu.make_async_copy(big_ref.at[chip], big_out.at[chip], local_sems.at[0])
        own_small = pltpu.make_async_copy(small_ref, small_out.at[dev], local_sems.at[1])
        own_big.start()
        own_small.start()
        sends = []
        for k, (px, py) in enumerate(_other_chips(x, y)):
            cp = pltpu.make_async_remote_copy(src_ref=big_ref.at[2 * px + py], dst_ref=big_out.at[chip],
                                              send_sem=send_sems.at[k], recv_sem=recv_sems.at[k],
                                              device_id=(px, py, c), device_id_type=MESH)
            cp.start()
            sends.append(cp)
        peers = []
        for r in range(1, N_DEV):
            fx, fy, fc = (r >> 2) & 1, (r >> 1) & 1, r & 1
            px, py, pc = (x + fx) % 2, (y + fy) % 2, (c + fc) % 2
            peers.append((px, py, pc))
            cp = pltpu.make_async_remote_copy(src_ref=small_ref, dst_ref=small_out.at[dev], send_sem=send_sems.at[2 + r],
                                              recv_sem=recv_sems.at[2 + r], device_id=(px, py, pc), device_id_type=MESH)
            cp.start()
            sends.append(cp)
        for k, (px, py) in enumerate(_other_chips(x, y)):
            pltpu.make_async_remote_copy(src_ref=big_ref.at[chip], dst_ref=big_out.at[2 * px + py],
                                         send_sem=send_sems.at[k], recv_sem=recv_sems.at[k],
                                         device_id=(px, py, c), device_id_type=MESH).wait_recv()
        for r, (px, py, pc) in zip(range(1, N_DEV), peers):
            pltpu.make_async_remote_copy(src_ref=small_ref, dst_ref=small_out.at[4 * px + 2 * py + pc],
                                         send_sem=send_sems.at[2 + r], recv_sem=recv_sems.at[2 + r],
                                         device_id=(px, py, pc), device_id_type=MESH).wait_recv()
        for cp in sends:
            cp.wait_send()
        own_big.wait()
        own_small.wait()

    hbm = pl.BlockSpec(memory_space=pltpu.HBM)
    n_sem = 3 + N_DEV - 1
    return pl.pallas_call(
        body, name="exchange_grads", in_specs=[hbm, hbm], out_specs=[hbm, hbm],
        out_shape=[jax.ShapeDtypeStruct(big.shape, big.dtype), jax.ShapeDtypeStruct((N_DEV,) + small.shape, small.dtype)],
        scratch_shapes=[pltpu.SemaphoreType.DMA((n_sem,)), pltpu.SemaphoreType.DMA((n_sem,)), pltpu.SemaphoreType.DMA((2,))],
    )(big, small)


SWAP_CHUNKS = 28


def swap_cores(mine):
    rows = mine.shape[0] // SWAP_CHUNKS
    assert rows * SWAP_CHUNKS == mine.shape[0] and rows % 8 == 0

    def body(in_ref, out_ref, send_sems, recv_sems):
        x, y, c = lax.axis_index("x"), lax.axis_index("y"), lax.axis_index("c")

        def chunk(k):
            part = pl.ds(k * rows, rows)
            return pltpu.make_async_remote_copy(src_ref=in_ref.at[part], dst_ref=out_ref.at[part],
                                                send_sem=send_sems.at[k], recv_sem=recv_sems.at[k],
                                                device_id=(x, y, 1 - c), device_id_type=MESH)

        for k in range(SWAP_CHUNKS):
            chunk(k).start()
        for k in range(SWAP_CHUNKS):
            chunk(k).wait_recv()
        for k in range(SWAP_CHUNKS):
            chunk(k).wait_send()

    hbm = pl.BlockSpec(memory_space=pltpu.HBM)
    return pl.pallas_call(
        body, name="swap_cores", in_specs=[hbm], out_specs=hbm,
        out_shape=jax.ShapeDtypeStruct(mine.shape, mine.dtype),
        scratch_shapes=[pltpu.SemaphoreType.DMA((SWAP_CHUNKS,)), pltpu.SemaphoreType.DMA((SWAP_CHUNKS,))],
    )(mine)


BIG_NAMES = ("w_in", "w_out", "w_gate", "w_up", "w_down")
BIG_SHARD_AXIS = {"w_in": 1, "w_out": 0, "w_gate": 1, "w_up": 1, "w_down": 0}
PACK_COLS = 1024
SMALL_NAMES = ("norm_mix", "conv_w", "conv_b", "dt_bias", "a_log", "d_skip", "ssm_norm", "norm_ffn")


PACK_ROW_TILE = 256


def pack_big(shards, names=BIG_NAMES):
    flat = jnp.concatenate([shards[n].reshape(-1) for n in names])
    unit = PACK_ROW_TILE * PACK_COLS
    total = -(-flat.size // unit) * unit
    return jnp.pad(flat, (0, total - flat.size)).reshape(-1, PACK_COLS)


def unpack_big(packed, like, names=BIG_NAMES):
    out, off = {}, 0
    flat = packed.reshape(-1)
    for n in names:
        size = like[n].size
        out[n] = flat[off:off + size].reshape(like[n].shape)
        off += size
    return out


def pack_small(parts):
    flat = jnp.concatenate([p.reshape(-1).astype(F32) for p in parts])
    rows = -(-flat.size // LANE)
    rows = -(-rows // 8) * 8
    return jnp.pad(flat, (0, rows * LANE - flat.size)).reshape(rows, LANE)


def unpack_small(packed, like):
    out, off = [], 0
    flat = packed.reshape(-1)
    for a in like:
        out.append(flat[off:off + a.size].reshape(a.shape))
        off += a.size
    return out


def _unused_kernel_packed(x, positions, norm_mix, w_in, conv_w, conv_b, dt_bias, a_log, d_skip, ssm_norm, w_out, norm_ffn, w_gate, w_up, w_down, final_norm, loss_target, m_norm_mix, m_w_in, m_conv_w, m_conv_b, m_dt_bias, m_a_log, m_d_skip, m_ssm_norm, m_w_out, m_norm_ffn, m_w_gate, m_w_up, m_w_down, m_final_norm, v_norm_mix, v_w_in, v_conv_w, v_conv_b, v_dt_bias, v_a_log, v_d_skip, v_ssm_norm, v_w_out, v_norm_ffn, v_w_gate, v_w_up, v_w_down, v_final_norm):
    chip = 2 * lax.axis_index("x") + lax.axis_index("y")
    w_sh = {"w_in": w_in, "w_out": w_out, "w_gate": w_gate, "w_up": w_up, "w_down": w_down}
    m_sh = {"w_in": m_w_in, "w_out": m_w_out, "w_gate": m_w_gate, "w_up": m_w_up, "w_down": m_w_down}
    v_sh = {"w_in": v_w_in, "w_out": v_w_out, "w_gate": v_w_gate, "w_up": v_w_up, "w_down": v_w_down}

    assert DEPTH == 2
    first, rest = BIG_NAMES[:1], BIG_NAMES[1:]
    layer_of = lambda d, l: {n: d[n][l] for n in BIG_NAMES}
    pack_layer = lambda d: jnp.concatenate([pack_big(d, first), pack_big(d, rest)])
    pack_layers = lambda d: jnp.concatenate([pack_layer(layer_of(d, l)) for l in range(DEPTH)])
    first_rows = pack_big(layer_of(w_sh, 0), first).shape[0]
    layer_rows = pack_layer(layer_of(w_sh, 0)).shape[0]

    def unpack_layer(packed, l):
        like = layer_of(w_sh, l)
        return {**unpack_big(packed[:first_rows], like, first), **unpack_big(packed[first_rows:], like, rest)}

    def unpack_layers(packed):
        per_layer = [unpack_layer(packed[l * layer_rows:(l + 1) * layer_rows], l) for l in range(DEPTH)]
        return {n: jnp.stack([p[n] for p in per_layer]) for n in BIG_NAMES}

    def full_weights(gathered, l, names, unpack):
        pieces = [unpack(gathered[j]) for j in range(N_CHIPS)]
        full = {n: jnp.concatenate([p[n] for p in pieces], axis=BIG_SHARD_AXIS[n]) for n in names}
        return tuple(w_in_columns(full[n]) if n == "w_in" else full[n] for n in names)

    w_packed16 = pack_layers({n: cast_bf16(w_sh[n].reshape(-1, w_sh[n].shape[-1]), name=f"cast_{n}").reshape(w_sh[n].shape)
                              for n in BIG_NAMES})
    conv_cols = CONV_CH // N_CHIPS
    gathered_in0, conv_g = allgather_chips([w_packed16[:first_rows], conv_w.reshape(-1, LANE)])
    big = [full_weights(gathered_in0, 0, first, lambda p: unpack_big(p, layer_of(w_sh, 0), first)) + (None,) * len(rest), None]
    plan = {
        "rest0": w_packed16[first_rows:layer_rows],
        "make_rest0": lambda g: full_weights(g, 0, rest, lambda p: unpack_big(p, layer_of(w_sh, 0), rest)),
        "late": w_packed16[layer_rows:],
        "make_late": lambda g: full_weights(g, DEPTH - 1, BIG_NAMES, lambda p: unpack_layer(p, DEPTH - 1)),
    }
    conv_w_full = jnp.concatenate([conv_g[j].reshape(DEPTH, CONV_WIDTH, conv_cols) for j in range(N_CHIPS)], axis=2)

    small_all = []
    for l in range(DEPTH):
        small_all.append({
            "norm_mix": norm_mix[l].reshape(1, -1), "conv_w": conv_w_full[l], "conv_b": conv_b[l].reshape(1, -1),
            "dt_bias": lane_pad(dt_bias[l]), "a_log": lane_pad(a_log[l]), "d_skip": lane_pad(d_skip[l]),
            "ssm_norm": ssm_norm[l].reshape(1, -1), "norm_ffn": norm_ffn[l].reshape(1, -1)})

    def shard_of(name, g, j):
        n = g.shape[BIG_SHARD_AXIS[name]] // N_CHIPS
        return lax.slice_in_dim(g, j * n, (j + 1) * n, axis=BIG_SHARD_AXIS[name])

    def per_chip(layer_grads, names):
        packs = [[pack_big({n: shard_of(n, layer_grads[n], j) for n in group}, group) for j in range(N_CHIPS)]
                 for group in ((first, rest) if names == BIG_NAMES else (names,))]
        return jnp.stack([jnp.concatenate([p[j] for p in packs]) for j in range(N_CHIPS)])

    plan["grads_late"] = lambda gr: per_chip(gr, BIG_NAMES)
    plan["grads_rest0"] = lambda gr: per_chip(gr, rest)
    loss_part, grad_x, grads, d_final, received = local_step(x, positions, big, small_all, final_norm, loss_target, plan=plan)

    small_parts = [jnp.stack([grads[l][n].reshape(-1) for l in range(DEPTH)]) for n in SMALL_NAMES]
    small_parts += [d_final.reshape(-1), loss_part.reshape(-1)]
    recv_first, recv_small = exchange_grads(per_chip(grads[0], first), pack_small(small_parts))
    plane_sum = jnp.concatenate([sum_slots(recv_first, name="sum_chip_partials_in0"),
                                 sum_slots(received["rest0"], name="sum_chip_partials_rest0"),
                                 sum_slots(received["late"], name="sum_chip_partials_l1")])
    other_plane = swap_cores(plane_sum)

    g_packed = rowwise_fwd(lambda p, q: (p + q,), [plane_sum, other_plane], [], [F32], name="sum_planes")[0]
    g_big = unpack_layers(g_packed)
    d_big, m_big, v_big = {}, {}, {}
    for n in BIG_NAMES:
        flat = lambda a: a.reshape(-1, a.shape[-1])
        res = adamw([flat(g_big[n])], flat(w_sh[n]), flat(m_sh[n]), flat(v_sh[n]), name=f"adamw_{n}", with_grad=False)
        d_big[n], m_big[n], v_big[n] = (a.reshape(w_sh[n].shape) for a in res)

    small_sum = sum_slots(recv_small, name="sum_small")
    like = [norm_mix, conv_w_full, conv_b, dt_bias, a_log, d_skip, ssm_norm, norm_ffn, final_norm, loss_part.reshape(-1)]
    g_small = unpack_small(small_sum, like)
    loss = g_small[-1][0]
    g_small = dict(zip(SMALL_NAMES + ("final_norm",), g_small[:-1]))
    g_small["conv_w"] = lax.dynamic_slice_in_dim(g_small["conv_w"], chip * conv_cols, conv_cols, axis=2)
    w_small = {"norm_mix": norm_mix, "conv_w": conv_w, "conv_b": conv_b, "dt_bias": dt_bias, "a_log": a_log, "d_skip": d_skip,
               "ssm_norm": ssm_norm, "norm_ffn": norm_ffn, "final_norm": final_norm}
    m_small = {"norm_mix": m_norm_mix, "conv_w": m_conv_w, "conv_b": m_conv_b, "dt_bias": m_dt_bias, "a_log": m_a_log,
               "d_skip": m_d_skip, "ssm_norm": m_ssm_norm, "norm_ffn": m_norm_ffn, "final_norm": m_final_norm}
    v_small = {"norm_mix": v_norm_mix, "conv_w": v_conv_w, "conv_b": v_conv_b, "dt_bias": v_dt_bias, "a_log": v_a_log,
               "d_skip": v_d_skip, "ssm_norm": v_ssm_norm, "norm_ffn": v_norm_ffn, "final_norm": v_final_norm}
    names = SMALL_NAMES + ("final_norm",)
    order = [w_small[n] for n in names]
    res = adamw([pack_small([g_small[n] for n in names])], pack_small(order), pack_small([m_small[n] for n in names]),
                pack_small([v_small[n] for n in names]), name="adamw_small")
    g_s, d_s, m_s, v_s = (dict(zip(names, unpack_small(a, order))) for a in res)

    all_names = ("norm_mix", "w_in", "conv_w", "conv_b", "dt_bias", "a_log", "d_skip", "ssm_norm", "w_out", "norm_ffn",
                 "w_gate", "w_up", "w_down", "final_norm")
    outs = [loss, grad_x]
    for src_big, src_small in ((g_big, g_s), (d_big, d_s), (m_big, m_s), (v_big, v_s)):
        outs += [src_big[n] if n in BIG_NAMES else src_small[n] for n in all_names]
    return tuple(outs)


SWAP_PIECES = 4


def swap_cores_list(arrays):
    n = len(arrays)

    def body(*refs):
        ins, outs, (send_sems, recv_sems) = refs[:n], refs[n:2 * n], refs[2 * n:]
        x, y, c = lax.axis_index("x"), lax.axis_index("y"), lax.axis_index("c")
        copies = []
        for k in range(n):
            rows = ins[k].shape[0] // SWAP_PIECES
            for p in range(SWAP_PIECES):
                part = pl.ds(p * rows, rows)
                copies.append(pltpu.make_async_remote_copy(
                    src_ref=ins[k].at[part], dst_ref=outs[k].at[part], send_sem=send_sems.at[k * SWAP_PIECES + p],
                    recv_sem=recv_sems.at[k * SWAP_PIECES + p], device_id=(x, y, 1 - c), device_id_type=MESH))
        for cp in copies:
            cp.start()
        for cp in copies:
            cp.wait_recv()
        for cp in copies:
            cp.wait_send()

    assert all(a.shape[0] % (8 * SWAP_PIECES) == 0 for a in arrays)
    hbm = pl.BlockSpec(memory_space=pltpu.HBM)
    return pl.pallas_call(
        body, name="swap_cores", in_specs=[hbm] * n, out_specs=[hbm] * n,
        out_shape=[jax.ShapeDtypeStruct(a.shape, a.dtype) for a in arrays],
        scratch_shapes=[pltpu.SemaphoreType.DMA((n * SWAP_PIECES,)), pltpu.SemaphoreType.DMA((n * SWAP_PIECES,))],
    )(*arrays)


def adamw_layers(g_parts, w, m, v, *, name):
    depth, a, b = w.shape
    tr = _pick(a, (256, 352, 192, 128, 8))
    counts = [len(p) for p in g_parts]
    flat_parts = [q for p in g_parts for q in p]
    bc1 = 1.0 / (1.0 - ADAM_B1 ** ADAM_STEP)
    bc2 = 1.0 / (1.0 - ADAM_B2 ** ADAM_STEP)

    def body(*refs):
        layer = pl.program_id(0)
        g, off = None, 0
        for l, cnt in enumerate(counts):
            g_l = refs[off][...]
            for r_ in refs[off + 1:off + cnt]:
                g_l = g_l + r_[...]
            off += cnt
            g = g_l if g is None else jnp.where(layer == l, g_l, g)
        w_ref, m_ref, v_ref, g_out, d_out, m_out, v_out = refs[off:]
        m_new = ADAM_B1 * m_ref[0] + (1.0 - ADAM_B1) * g
        v_new = ADAM_B2 * v_ref[0] + (1.0 - ADAM_B2) * (g * g)
        g_out[0] = g
        m_out[0] = m_new
        v_out[0] = v_new
        d_out[0] = -ADAM_LR * ((m_new * bc1) / (jnp.sqrt(v_new * bc2) + ADAM_EPS) + ADAM_WD * w_ref[0])

    g_spec = pl.BlockSpec((tr, b), lambda l, i: (i, 0))
    spec = pl.BlockSpec((1, tr, b), lambda l, i: (l, i, 0))
    return pl.pallas_call(
        body, name=name, grid=(depth, a // tr), in_specs=[g_spec] * len(flat_parts) + [spec] * 3, out_specs=[spec] * 4,
        out_shape=[jax.ShapeDtypeStruct(w.shape, F32)] * 4, compiler_params=_params(("parallel", "parallel")),
    )(*flat_parts, w, m, v)


def kernel(x, positions, norm_mix, w_in, conv_w, conv_b, dt_bias, a_log, d_skip, ssm_norm, w_out, norm_ffn, w_gate, w_up, w_down, final_norm, loss_target, m_norm_mix, m_w_in, m_conv_w, m_conv_b, m_dt_bias, m_a_log, m_d_skip, m_ssm_norm, m_w_out, m_norm_ffn, m_w_gate, m_w_up, m_w_down, m_final_norm, v_norm_mix, v_w_in, v_conv_w, v_conv_b, v_dt_bias, v_a_log, v_d_skip, v_ssm_norm, v_w_out, v_norm_ffn, v_w_gate, v_w_up, v_w_down, v_final_norm):
    chip = 2 * lax.axis_index("x") + lax.axis_index("y")
    w_sh = {"w_in": w_in, "w_out": w_out, "w_gate": w_gate, "w_up": w_up, "w_down": w_down}
    m_sh = {"w_in": m_w_in, "w_out": m_w_out, "w_gate": m_w_gate, "w_up": m_w_up, "w_down": m_w_down}
    v_sh = {"w_in": v_w_in, "w_out": v_w_out, "w_gate": v_w_gate, "w_up": v_w_up, "w_down": v_w_down}
    assert DEPTH == 2
    first, rest = BIG_NAMES[:1], BIG_NAMES[1:]

    w16 = {n: cast_bf16(w_sh[n].reshape(-1, w_sh[n].shape[-1]), name=f"cast_{n}").reshape(w_sh[n].shape) for n in BIG_NAMES}

    def joined(n, gathered):
        if BIG_SHARD_AXIS[n] == 0:
            full = gathered.reshape(-1, gathered.shape[-1])
        else:
            full = jnp.concatenate([gathered[j] for j in range(N_CHIPS)], axis=1)
        return w_in_columns(full) if n == "w_in" else full

    def per_chip(n, g):
        if BIG_SHARD_AXIS[n] == 0:
            return g.reshape(N_CHIPS, -1, g.shape[-1])
        return jnp.stack(jnp.split(g, N_CHIPS, axis=1))

    conv_cols = CONV_CH // N_CHIPS
    gathered_in0, conv_g = allgather_chips([w16["w_in"][0], conv_w.reshape(-1, LANE)])
    big = [(joined("w_in", gathered_in0),) + (None,) * len(rest), None]
    plan = {
        "rest0": [w16[n][0] for n in rest],
        "make_rest0": lambda gs: tuple(joined(n, g) for n, g in zip(rest, gs)),
        "late": [w16[n][DEPTH - 1] for n in BIG_NAMES],
        "make_late": lambda gs: tuple(joined(n, g) for n, g in zip(BIG_NAMES, gs)),
        "grads_late": lambda gr: [per_chip(n, gr[n]) for n in BIG_NAMES],
        "grads_rest0": lambda gr: [per_chip(n, gr[n]) for n in rest],
    }
    conv_w_full = jnp.concatenate([conv_g[j].reshape(DEPTH, CONV_WIDTH, conv_cols) for j in range(N_CHIPS)], axis=2)
    small_all = []
    for l in range(DEPTH):
        small_all.append({
            "norm_mix": norm_mix[l].reshape(1, -1), "conv_w": conv_w_full[l], "conv_b": conv_b[l].reshape(1, -1),
            "dt_bias": lane_pad(dt_bias[l]), "a_log": lane_pad(a_log[l]), "d_skip": lane_pad(d_skip[l]),
            "ssm_norm": ssm_norm[l].reshape(1, -1), "norm_ffn": norm_ffn[l].reshape(1, -1)})

    loss_part, grad_x, grads, d_final, received = local_step(x, positions, big, small_all, final_norm, loss_target, plan=plan)

    small_parts = [jnp.stack([grads[l][n].reshape(-1) for l in range(DEPTH)]) for n in SMALL_NAMES]
    small_parts += [d_final.reshape(-1), loss_part.reshape(-1)]
    recv_in0, recv_small = exchange_grads(per_chip("w_in", grads[0]["w_in"]), pack_small(small_parts))
    recv = [dict(zip(BIG_NAMES, [recv_in0] + list(received["rest0"]))), dict(zip(BIG_NAMES, received["late"]))]
    keys = [(l, n) for l in range(DEPTH) for n in BIG_NAMES]
    mine = {(l, n): sum_slots(recv[l][n], name=f"sum_partials_{n}_l{l}") for l, n in keys}
    other = dict(zip(keys, swap_cores_list([mine[k] for k in keys])))

    g_big, d_big, m_big, v_big = {}, {}, {}, {}
    for n in BIG_NAMES:
        g_big[n], d_big[n], m_big[n], v_big[n] = adamw_layers([[mine[(l, n)], other[(l, n)]] for l in range(DEPTH)],
                                                              w_sh[n], m_sh[n], v_sh[n], name=f"adamw_{n}")

    small_sum = sum_slots(recv_small, name="sum_small")
    like = [norm_mix, conv_w_full, conv_b, dt_bias, a_log, d_skip, ssm_norm, norm_ffn, final_norm, loss_part.reshape(-1)]
    g_small = unpack_small(small_sum, like)
    loss = g_small[-1][0]
    g_small = dict(zip(SMALL_NAMES + ("final_norm",), g_small[:-1]))
    g_small["conv_w"] = lax.dynamic_slice_in_dim(g_small["conv_w"], chip * conv_cols, conv_cols, axis=2)
    w_small = {"norm_mix": norm_mix, "conv_w": conv_w, "conv_b": conv_b, "dt_bias": dt_bias, "a_log": a_log, "d_skip": d_skip,
               "ssm_norm": ssm_norm, "norm_ffn": norm_ffn, "final_norm": final_norm}
    m_small = {"norm_mix": m_norm_mix, "conv_w": m_conv_w, "conv_b": m_conv_b, "dt_bias": m_dt_bias, "a_log": m_a_log,
               "d_skip": m_d_skip, "ssm_norm": m_ssm_norm, "norm_ffn": m_norm_ffn, "final_norm": m_final_norm}
    v_small = {"norm_mix": v_norm_mix, "conv_w": v_conv_w, "conv_b": v_conv_b, "dt_bias": v_dt_bias, "a_log": v_a_log,
               "d_skip": v_d_skip, "ssm_norm": v_ssm_norm, "norm_ffn": v_norm_ffn, "final_norm": v_final_norm}
    names = SMALL_NAMES + ("final_norm",)
    order = [w_small[n] for n in names]
    res = adamw([pack_small([g_small[n] for n in names])], pack_small(order), pack_small([m_small[n] for n in names]),
                pack_small([v_small[n] for n in names]), name="adamw_small")
    g_s, d_s, m_s, v_s = (dict(zip(names, unpack_small(a, order))) for a in res)

    all_names = ("norm_mix", "w_in", "conv_w", "conv_b", "dt_bias", "a_log", "d_skip", "ssm_norm", "w_out", "norm_ffn",
                 "w_gate", "w_up", "w_down", "final_norm")
    outs = [loss, grad_x]
    for src_big, src_small in ((g_big, g_s), (d_big, d_s), (m_big, m_s), (v_big, v_s)):
        outs += [src_big[n] if n in BIG_NAMES else src_small[n] for n in all_names]
    return tuple(outs)
```

```python
import functools

import jax
import jax.numpy as jnp
from jax import lax
from jax.experimental import pallas as pl
from jax.experimental.pallas import tpu as pltpu

F32 = jnp.float32
BF16 = jnp.bfloat16
MESH = pl.DeviceIdType.MESH

D_MODEL = 1024
DEPTH = 2
HEAD_DIM = 64
N_Q_HEADS = 8
N_KV_HEADS = 2
GQA = N_Q_HEADS // N_KV_HEADS
ATTN_WIDTH = N_Q_HEADS * HEAD_DIM
ROPE_DIM = HEAD_DIM // 4
ROPE_HALF = ROPE_DIM // 2
ROPE_THETA = 500000.0
DILATIONS = (1, 4, 16)
ATTN_BLOCK = 128
SSM_P = 64
SSM_HEADS = 16
SSM_INNER = SSM_HEADS * SSM_P
SSM_GROUPS = 2
HEADS_PER_GROUP = SSM_HEADS // SSM_GROUPS
D_STATE = 128
CONV_WIDTH = 4
CHUNK = 128
CONV_CH = SSM_INNER + 2 * SSM_GROUPS * D_STATE
MIX_WIDTH = ATTN_WIDTH + SSM_INNER
Q_END = ATTN_WIDTH
K_END = Q_END + N_KV_HEADS * HEAD_DIM
V_END = K_END + N_KV_HEADS * HEAD_DIM
Z_END = V_END + SSM_INNER
XBC_END = Z_END + CONV_CH
IN_PROJ = XBC_END + SSM_HEADS
LANE = 128
IN_PAD = XBC_END + LANE
Q_COL, Z_COL, XBC_COL, K_COL, V_COL, DT_COL = 0, 512, 1536, 3072, 3200, 3328
FFN_HIDDEN = 2816
EPS = 1e-5
ADAM_LR, ADAM_B1, ADAM_B2, ADAM_EPS, ADAM_WD, ADAM_STEP = 0.001, 0.9, 0.999, 1e-8, 0.01, 10
N_CHIPS = 4
N_DEV = 8
VMEM_LIMIT = 48 * 1024 * 1024
NEG_BIG = -1e30


def _params(sem=None):
    return pltpu.CompilerParams(dimension_semantics=sem, vmem_limit_bytes=VMEM_LIMIT)


def _pick(n, prefs):
    for p in prefs:
        if n % p == 0:
            return p
    return n


def matmul(a, b, *, name, ta=False, tb=False, out_dtype=F32, residual=None):
    if ta:
        assert not tb and residual is None
        return _matmul_over_rows(a, b, name=name, out_dtype=out_dtype)
    return _matmul_full_k(a, b, name=name, tb=tb, out_dtype=out_dtype, residual=residual)


def _matmul_full_k(a, b, *, name, tb, out_dtype, residual):
    a_parts = list(a) if isinstance(a, (list, tuple)) else [a]
    n_a = len(a_parts)
    m = a_parts[0].shape[0]
    kdim = sum(p.shape[1] for p in a_parts)
    wide = kdim > 1536 or any(p.dtype == F32 for p in a_parts)
    n = b.shape[0] if tb else b.shape[1]
    tm = _pick(m, (512, 256)) if wide else _pick(m, (1024, 512, 256))
    tn = _pick(n, (1152, 1408, 1536, 1024, 768, 512, 384, 256, 128))
    b_spec = pl.BlockSpec((tn, kdim), lambda i, j: (j, 0)) if tb else pl.BlockSpec((kdim, tn), lambda i, j: (0, j))
    o_spec = pl.BlockSpec((tm, tn), lambda i, j: (i, j))
    dims = (((1,), (1 if tb else 0,)), ((), ()))
    has_res = residual is not None

    def body(*refs):
        b_ref, o_ref = refs[n_a], refs[-1]
        pieces = [r[...].astype(BF16) for r in refs[:n_a]]
        av = pieces[0] if n_a == 1 else jnp.concatenate(pieces, axis=1)
        r = lax.dot_general(av, b_ref[...].astype(BF16), dims, preferred_element_type=F32)
        if has_res:
            r = r + refs[n_a + 1][...]
        o_ref[...] = r.astype(out_dtype)

    in_specs = ([pl.BlockSpec((tm, p.shape[1]), lambda i, j: (i, 0)) for p in a_parts] + [b_spec]
                + ([o_spec] if has_res else []))
    args = tuple(a_parts) + (b,) + ((residual,) if has_res else ())
    return pl.pallas_call(
        body, name=name, grid=(m // tm, n // tn), in_specs=in_specs, out_specs=o_spec,
        out_shape=jax.ShapeDtypeStruct((m, n), out_dtype),
        compiler_params=_params(("parallel", "parallel")),
    )(*args)


def _matmul_over_rows(a, b, *, name, out_dtype):
    t, m = a.shape
    n = b.shape[1]
    tm = _pick(m, (1024, 1408, 768, 512, 256, 128))
    tn = _pick(n, (1152, 1408, 1024, 768, 512, 384, 256, 128))
    tk = _pick(t, (1024, 512, 256, 128))
    nk = t // tk

    def body(a_ref, b_ref, o_ref, acc):
        k = pl.program_id(2)
        part = lax.dot_general(a_ref[...].astype(BF16), b_ref[...].astype(BF16), (((0,), (0,)), ((), ())),
                               preferred_element_type=F32)

        @pl.when(k == 0)
        def _():
            acc[...] = part

        @pl.when(k > 0)
        def _():
            acc[...] += part

        @pl.when(k == nk - 1)
        def _():
            o_ref[...] = acc[...].astype(out_dtype)

    return pl.pallas_call(
        body, name=name, grid=(m // tm, n // tn, nk),
        in_specs=[pl.BlockSpec((tk, tm), lambda i, j, k: (k, i)), pl.BlockSpec((tk, tn), lambda i, j, k: (k, j))],
        out_specs=pl.BlockSpec((tm, tn), lambda i, j, k: (i, j)),
        out_shape=jax.ShapeDtypeStruct((m, n), out_dtype),
        scratch_shapes=[pltpu.VMEM((tm, tn), F32)],
        compiler_params=_params(("parallel", "parallel", "arbitrary")),
    )(a, b)


ROW_BLOCK_BYTES = 16 * 1024 * 1024


def _row_tile(t, tr, widths, n_copies):
    lanes = sum(-(-wd // LANE) * LANE for wd in widths) * n_copies
    tr = min(tr, t)
    while tr > 8 and tr * lanes * 4 > ROW_BLOCK_BYTES:
        tr //= 2
    return tr


def _row_widths(rows, groups, windows):
    windows = windows or [None] * len(rows)
    widths = [(w[1] if w else a.shape[1]) // groups for a, w in zip(rows, windows)]
    assert all(w is None or w[0] % wd == 0 for w, wd in zip(windows, widths))
    return widths, [(w[0] // wd if w else 0) for w, wd in zip(windows, widths)]


def _row_specs(tr, widths, offs):
    return [pl.BlockSpec((tr, wd), functools.partial(lambda g, i, off: (i, g + off), off=off)) for wd, off in zip(widths, offs)]


def rowwise_fwd(fn, rows, params, out_dtypes, *, name, tr=512, groups=1, windows=None):
    t = rows[0].shape[0]
    widths, offs = _row_widths(rows, groups, windows)
    tr = _row_tile(t, tr, widths, 2)
    row_specs = _row_specs(tr, widths, offs)
    par_spec = lambda p: pl.BlockSpec((1, p.shape[1] // groups), lambda g, i: (0, g))
    n_in = len(rows) + len(params)
    out_cols = [o.shape[1] for o in jax.eval_shape(
        fn, *[jax.ShapeDtypeStruct((tr, wd), F32) for wd in widths],
        *[jax.ShapeDtypeStruct((1, p.shape[1] // groups), F32) for p in params])]

    def body(*refs):
        vals = [r[...].astype(F32) for r in refs[:n_in]]
        outs = fn(*vals)
        for o_ref, o in zip(refs[n_in:], outs):
            o_ref[...] = o.astype(o_ref.dtype)

    return pl.pallas_call(
        body, name=name, grid=(groups, t // tr),
        in_specs=row_specs + [par_spec(p) for p in params],
        out_specs=[pl.BlockSpec((tr, c), lambda g, i: (i, g)) for c in out_cols],
        out_shape=[jax.ShapeDtypeStruct((t, c * groups), d) for c, d in zip(out_cols, out_dtypes)],
        compiler_params=_params(("arbitrary", "arbitrary")),
    )(*rows, *params)


def rowwise_bwd(fn, rows, params, cts, drow_dtypes, *, name, tr=512, groups=1, add_to_first=None, windows=None,
                ct_windows=None):
    t = rows[0].shape[0]
    widths, offs = _row_widths(rows, groups, windows)
    ct_widths, ct_offs = _row_widths(cts, groups, ct_windows)
    tr = _row_tile(t, tr, widths + ct_widths, 2)
    row_spec = lambda a: pl.BlockSpec((tr, a.shape[1] // groups), lambda g, i: (i, g))
    row_specs = _row_specs(tr, widths, offs)
    par_spec = lambda p: pl.BlockSpec((1, p.shape[1] // groups), lambda g, i: (0, g))
    n_rows, n_par, n_ct = len(rows), len(params), len(cts)
    has_add = add_to_first is not None
    n_in = n_rows + n_par + n_ct + (1 if has_add else 0)

    def body(*refs):
        i = pl.program_id(1)
        vals = [r[...].astype(F32) for r in refs[:n_rows + n_par]]
        ct_vals = tuple(r[...].astype(F32) for r in refs[n_rows + n_par:n_rows + n_par + n_ct])
        _, vjp = jax.vjp(fn, *vals)
        grads = vjp(ct_vals)
        out_refs = refs[n_in:]
        for idx in range(n_rows):
            g = grads[idx]
            if idx == 0 and has_add:
                g = g + refs[n_in - 1][...]
            out_refs[idx][...] = g.astype(out_refs[idx].dtype)
        for idx in range(n_par):
            p_ref = out_refs[n_rows + idx]

            @pl.when(i == 0)
            def _():
                p_ref[...] = jnp.zeros_like(p_ref)

            p_ref[...] += grads[n_rows + idx]

    ins = list(rows) + list(params) + list(cts) + ([add_to_first] if has_add else [])
    in_specs = (row_specs + [par_spec(p) for p in params] + _row_specs(tr, ct_widths, ct_offs)
                + ([row_spec(add_to_first)] if has_add else []))
    return pl.pallas_call(
        body, name=name, grid=(groups, t // tr), in_specs=in_specs,
        out_specs=[pl.BlockSpec((tr, wd), lambda g, i: (i, g)) for wd in widths] + [par_spec(p) for p in params],
        out_shape=[jax.ShapeDtypeStruct((t, wd * groups), d) for wd, d in zip(widths, drow_dtypes)]
        + [jax.ShapeDtypeStruct(p.shape, F32) for p in params],
        compiler_params=_params(("arbitrary", "arbitrary")),
    )(*ins)


def rms_fn(x, w):
    return (x * lax.rsqrt(jnp.mean(x * x, axis=-1, keepdims=True) + EPS) * w,)


def swiglu_fn(g, u):
    return (g * jax.nn.sigmoid(g) * u,)


def gated_norm_fn(y, z, w):
    v = y * (z * jax.nn.sigmoid(z))
    return (v * lax.rsqrt(jnp.mean(v * v, axis=-1, keepdims=True) + EPS) * w,)


def combine_fn(o1, o2, o3, l1, l2, l3):
    m = jnp.maximum(jnp.maximum(l1, l2), l3)
    e1, e2, e3 = jnp.exp(l1 - m), jnp.exp(l2 - m), jnp.exp(l3 - m)
    inv = 1.0 / (e1 + e2 + e3)
    return ((e1 * inv) * o1 + (e2 * inv) * o2 + (e3 * inv) * o3,)


def loss_and_grad(h, target, w, *, tr=512):
    t, d = h.shape

    def loss_fn(hv, wv, tv):
        err = rms_fn(hv, wv)[0] - tv
        per_row = jnp.mean(err * err, axis=-1, keepdims=True)
        return 0.5 * jnp.sum(per_row, axis=0, keepdims=True)

    def body(h_ref, t_ref, w_ref, dh_ref, dw_ref, loss_ref):
        i = pl.program_id(0)

        @pl.when(i == 0)
        def _():
            dw_ref[...] = jnp.zeros_like(dw_ref)
            loss_ref[...] = jnp.zeros_like(loss_ref)

        tv = t_ref[...]
        val, vjp = jax.vjp(lambda hv, wv: loss_fn(hv, wv, tv), h_ref[...], w_ref[...])
        dh, dw = vjp(jnp.ones((1, 1), F32))
        dh_ref[...] = dh
        dw_ref[...] += dw
        loss_ref[...] += jnp.broadcast_to(val, loss_ref.shape)

    row = pl.BlockSpec((tr, d), lambda i: (i, 0))
    par = pl.BlockSpec((1, d), lambda i: (0, 0))
    return pl.pallas_call(
        body, name="loss_and_grad", grid=(t // tr,), in_specs=[row, row, par],
        out_specs=[row, par, pl.BlockSpec((1, LANE), lambda i: (0, 0))],
        out_shape=[jax.ShapeDtypeStruct((t, d), F32), jax.ShapeDtypeStruct((1, d), F32),
                   jax.ShapeDtypeStruct((1, LANE), F32)],
        compiler_params=_params(("arbitrary",)),
    )(h, target, w)


def _split3(x):
    hi = x.astype(BF16)
    r1 = x - hi.astype(F32)
    mid = r1.astype(BF16)
    lo = (r1 - mid.astype(F32)).astype(BF16)
    return hi, mid, lo


def _dot01_left(m01, x):
    return sum(jnp.dot(m01, p, preferred_element_type=F32) for p in _split3(x))


def _dot01_right(x, m01):
    return sum(jnp.dot(p, m01, preferred_element_type=F32) for p in _split3(x))


def rotary(xs_list, cosf, sinf, scale, *, adjoint, name, ts=512):
    b, h, s, c = xs_list[0].shape
    n_x = len(xs_list)

    def body(*refs):
        x = refs[0][0, 0]
        for r in refs[1:n_x]:
            x = x + r[0, 0]
        cos_v, sin_v = refs[n_x][0], refs[n_x + 1][0]
        o_ref = refs[n_x + 2]
        ci = lax.broadcasted_iota(jnp.int32, (c, c), 0)
        cj = lax.broadcasted_iota(jnp.int32, (c, c), 1)
        swap = ((cj == ci + ROPE_HALF) & (ci < ROPE_HALF)) | ((cj == ci - ROPE_HALF) & (ci >= ROPE_HALF) & (ci < ROPE_DIM))
        swap = swap.astype(BF16)
        if adjoint:
            out = x * cos_v + _dot01_right(x * sin_v, swap)
        else:
            out = x * cos_v + _dot01_right(x, swap) * sin_v
        o_ref[0, 0] = out * scale

    x_spec = pl.BlockSpec((1, 1, ts, c), lambda bi, hi, si: (bi, hi, si, 0))
    t_spec = pl.BlockSpec((1, ts, c), lambda bi, hi, si: (bi, si, 0))
    return pl.pallas_call(
        body, name=name, grid=(b, h, s // ts), in_specs=[x_spec] * n_x + [t_spec, t_spec], out_specs=x_spec,
        out_shape=jax.ShapeDtypeStruct((b, h, s, c), F32),
        compiler_params=_params(("parallel", "parallel", "parallel")),
    )(*xs_list, cosf, sinf)


def add3(a, b, c, *, name, tr=1024):
    def fn(x, y, z):
        return (x + y + z,)
    return rowwise_fwd(fn, [a, b, c], [], [F32], name=name, tr=tr)[0]


def _attn_mask(n):
    rows = GQA * ATTN_BLOCK
    qi = lax.broadcasted_iota(jnp.int32, (rows, 2 * ATTN_BLOCK), 0) % ATTN_BLOCK
    ki = lax.broadcasted_iota(jnp.int32, (rows, 2 * ATTN_BLOCK), 1)
    delta = qi + ATTN_BLOCK - ki
    return (delta >= 0) & (delta <= ATTN_BLOCK) & ((n - 1) * ATTN_BLOCK + ki >= 0)


def _attn_specs(l):
    q_spec = pl.BlockSpec((1, GQA, ATTN_BLOCK, HEAD_DIM), lambda p, n: (p, 0, n, 0))
    l_spec = pl.BlockSpec((1, GQA, ATTN_BLOCK, 1), lambda p, n: (p, 0, n, 0))
    kprev = pl.BlockSpec((1, ATTN_BLOCK, HEAD_DIM), lambda p, n: (p, jnp.maximum(n - 1, 0), 0))
    kcur = pl.BlockSpec((1, ATTN_BLOCK, HEAD_DIM), lambda p, n: (p, n, 0))
    kfull = pl.BlockSpec((1, l, HEAD_DIM), lambda p, n: (p, 0, 0))
    return q_spec, l_spec, kprev, kcur, kfull


def attn_branch_fwd(q, k, v, *, name):
    p_cnt, _, l, _ = q.shape
    rows = GQA * ATTN_BLOCK
    q_spec, l_spec, kprev, kcur, _ = _attn_specs(l)

    def body(q_ref, kp_ref, kc_ref, vp_ref, vc_ref, o_ref, lse_ref):
        n = pl.program_id(1)
        qv = q_ref[0].reshape(rows, HEAD_DIM).astype(BF16)
        kk = jnp.concatenate([kp_ref[0], kc_ref[0]], axis=0).astype(BF16)
        vv = jnp.concatenate([vp_ref[0], vc_ref[0]], axis=0).astype(BF16)
        s = lax.dot_general(qv, kk, (((1,), (1,)), ((), ())), preferred_element_type=F32)
        s = jnp.where(_attn_mask(n), s, NEG_BIG)
        m = jnp.max(s, axis=-1, keepdims=True)
        pr = jnp.exp(s - m)
        den = jnp.sum(pr, axis=-1, keepdims=True)
        o = jnp.dot(pr.astype(BF16), vv, preferred_element_type=F32) / den
        o_ref[0] = o.reshape(GQA, ATTN_BLOCK, HEAD_DIM)
        lse_ref[0] = (m + jnp.log(den)).reshape(GQA, ATTN_BLOCK, 1)

    return pl.pallas_call(
        body, name=name, grid=(p_cnt, l // ATTN_BLOCK), in_specs=[q_spec, kprev, kcur, kprev, kcur],
        out_specs=[q_spec, l_spec],
        out_shape=[jax.ShapeDtypeStruct(q.shape, F32), jax.ShapeDtypeStruct(q.shape[:3] + (1,), F32)],
        compiler_params=_params(("parallel", "arbitrary")),
    )(q, k, k, v, v)


def attn_branch_bwd(q, k, v, o, lse, do, dlse, *, name):
    p_cnt, _, l, _ = q.shape
    rows = GQA * ATTN_BLOCK
    q_spec, l_spec, kprev, kcur, kfull = _attn_specs(l)

    def body(q_ref, kp_ref, kc_ref, vp_ref, vc_ref, o_ref, lse_ref, do_ref, dlse_ref, dq_ref, dk_ref, dv_ref):
        n = pl.program_id(1)

        @pl.when(n == 0)
        def _():
            dk_ref[...] = jnp.zeros_like(dk_ref)
            dv_ref[...] = jnp.zeros_like(dv_ref)

        qv = q_ref[0].reshape(rows, HEAD_DIM).astype(BF16)
        kk = jnp.concatenate([kp_ref[0], kc_ref[0]], axis=0).astype(BF16)
        vv = jnp.concatenate([vp_ref[0], vc_ref[0]], axis=0).astype(BF16)
        ov = o_ref[0].reshape(rows, HEAD_DIM)
        dov = do_ref[0].reshape(rows, HEAD_DIM)
        lsev = lse_ref[0].reshape(rows, 1)
        dlsev = dlse_ref[0].reshape(rows, 1)
        s = lax.dot_general(qv, kk, (((1,), (1,)), ((), ())), preferred_element_type=F32)
        pr = jnp.where(_attn_mask(n), jnp.exp(s - lsev), 0.0)
        do16 = dov.astype(BF16)
        dv = lax.dot_general(pr.astype(BF16), do16, (((0,), (0,)), ((), ())), preferred_element_type=F32)
        dp = lax.dot_general(do16, vv, (((1,), (1,)), ((), ())), preferred_element_type=F32)
        delta = jnp.sum(dov * ov, axis=-1, keepdims=True)
        ds = (pr * (dp - delta + dlsev)).astype(BF16)
        dq = jnp.dot(ds, kk, preferred_element_type=F32)
        dk = lax.dot_general(ds, qv, (((0,), (0,)), ((), ())), preferred_element_type=F32)
        dq_ref[0] = dq.reshape(GQA, ATTN_BLOCK, HEAD_DIM)
        cur = pl.ds(pl.multiple_of(n * ATTN_BLOCK, ATTN_BLOCK), ATTN_BLOCK)
        dk_ref[0, cur, :] += dk[ATTN_BLOCK:]
        dv_ref[0, cur, :] += dv[ATTN_BLOCK:]

        @pl.when(n > 0)
        def _():
            prev = pl.ds(pl.multiple_of((n - 1) * ATTN_BLOCK, ATTN_BLOCK), ATTN_BLOCK)
            dk_ref[0, prev, :] += dk[:ATTN_BLOCK]
            dv_ref[0, prev, :] += dv[:ATTN_BLOCK]

    return pl.pallas_call(
        body, name=name, grid=(p_cnt, l // ATTN_BLOCK),
        in_specs=[q_spec, kprev, kcur, kprev, kcur, q_spec, l_spec, q_spec, l_spec],
        out_specs=[q_spec, kfull, kfull],
        out_shape=[jax.ShapeDtypeStruct(q.shape, F32), jax.ShapeDtypeStruct(k.shape, F32),
                   jax.ShapeDtypeStruct(v.shape, F32)],
        compiler_params=_params(("parallel", "arbitrary")),
    )(q, k, k, v, v, o, lse, do, dlse)


ATTN_PAD = ATTN_BLOCK * DILATIONS[-1]
Q_GROUP_W = GQA * HEAD_DIM
ATTN_VMEM_LIMIT = 56 * 1024 * 1024


def _rope(x, cos_v, sin_v, swap, scale, adjoint):
    if adjoint:
        return (x * cos_v + _dot01_right(x * sin_v, swap)) * scale
    return (x * cos_v + _dot01_right(x, swap) * sin_v) * scale


def _swap_matrix():
    c = HEAD_DIM
    ci = lax.broadcasted_iota(jnp.int32, (c, c), 0)
    cj = lax.broadcasted_iota(jnp.int32, (c, c), 1)
    swap = ((cj == ci + ROPE_HALF) & (ci < ROPE_HALF)) | ((cj == ci - ROPE_HALF) & (ci >= ROPE_HALF) & (ci < ROPE_DIM))
    return swap.astype(BF16)


def _attn_prologue(q_ref, kv_ref, tab_ref, q_s, k_s, v_s, hk, s_len):
    swap = _swap_matrix()
    cos_v, sin_v = tab_ref[0, :, :HEAD_DIM], tab_ref[0, :, HEAD_DIM:]
    for g in range(GQA):
        cols = slice(g * HEAD_DIM, (g + 1) * HEAD_DIM)
        q_s[:, cols] = _rope(q_ref[0, :, cols], cos_v, sin_v, swap, HEAD_DIM ** -0.5, False)
    zeros = jnp.zeros((ATTN_PAD, HEAD_DIM), F32)
    k_s[0:ATTN_PAD, :] = zeros
    v_s[0:ATTN_PAD, :] = zeros
    for h in range(N_KV_HEADS):
        @pl.when(hk == h)
        def _():
            k_s[ATTN_PAD:ATTN_PAD + s_len, :] = _rope(kv_ref[0, :, h * HEAD_DIM:(h + 1) * HEAD_DIM], cos_v, sin_v, swap, 1.0, False)
            v_s[ATTN_PAD:ATTN_PAD + s_len, :] = kv_ref[0, :, LANE + h * HEAD_DIM:LANE + (h + 1) * HEAD_DIM]


def _attn_blocks(s_len):
    out = []
    for i, d in enumerate(DILATIONS):
        nb = s_len // (ATTN_BLOCK * d)
        for r in range(d):
            for n in range(nb):
                start = r + d * ATTN_BLOCK * n
                out.append((i, d, start, ATTN_PAD + start - d * ATTN_BLOCK, n))
    return out


def _rows(start, size, d):
    return pl.ds(start, size, stride=d) if d > 1 else pl.ds(start, size)


def _stack_heads(blk):
    return jnp.concatenate([blk[:, g * HEAD_DIM:(g + 1) * HEAD_DIM] for g in range(GQA)], axis=0)


def _stack_stats(blk):
    return jnp.concatenate([jnp.max(blk[:, g * HEAD_DIM:(g + 1) * HEAD_DIM], axis=1, keepdims=True) for g in range(GQA)], axis=0)


def _attn_in_specs(s_len):
    assert K_COL % (2 * LANE) == 0 and V_COL == K_COL + LANE
    q_spec = pl.BlockSpec((1, s_len, Q_GROUP_W), lambda b, h: (b, 0, Q_COL // Q_GROUP_W + h))
    kv_spec = pl.BlockSpec((1, s_len, 2 * LANE), lambda b, h: (b, 0, K_COL // (2 * LANE)))
    t_spec = pl.BlockSpec((1, s_len, 2 * HEAD_DIM), lambda b, h: (b, 0, 0))
    o_spec = pl.BlockSpec((1, s_len, Q_GROUP_W), lambda b, h: (b, 0, h))
    return q_spec, kv_spec, t_spec, o_spec


def attn_fwd(proj3, rope_tab, *, name):
    b, s_len, _ = proj3.shape
    q_spec, kv_spec, t_spec, o_spec = _attn_in_specs(s_len)
    n_br = len(DILATIONS)

    def body(q_ref, kv_ref, tab_ref, o_ref, lse_ref, q_s, k_s, v_s, *branch_s):
        o_s, l_s = branch_s[:n_br], branch_s[n_br:]
        _attn_prologue(q_ref, kv_ref, tab_ref, q_s, k_s, v_s, pl.program_id(1), s_len)
        for i, d, q0, k0, n in _attn_blocks(s_len):
            qv = _stack_heads(q_s[_rows(q0, ATTN_BLOCK, d), :]).astype(BF16)
            kk = k_s[_rows(k0, 2 * ATTN_BLOCK, d), :].astype(BF16)
            vv = v_s[_rows(k0, 2 * ATTN_BLOCK, d), :].astype(BF16)
            sc = lax.dot_general(qv, kk, (((1,), (1,)), ((), ())), preferred_element_type=F32)
            sc = jnp.where(_attn_mask(n), sc, NEG_BIG)
            m = jnp.max(sc, axis=-1, keepdims=True)
            pr = jnp.exp(sc - m)
            den = jnp.sum(pr, axis=-1, keepdims=True)
            o = jnp.dot(pr.astype(BF16), vv, preferred_element_type=F32) / den
            lse = m + jnp.log(den)
            for g in range(GQA):
                part = slice(g * ATTN_BLOCK, (g + 1) * ATTN_BLOCK)
                o_s[i][_rows(q0, ATTN_BLOCK, d), g * HEAD_DIM:(g + 1) * HEAD_DIM] = o[part]
                l_s[i][_rows(q0, ATTN_BLOCK, d), g * HEAD_DIM:(g + 1) * HEAD_DIM] = jnp.broadcast_to(lse[part], (ATTN_BLOCK, HEAD_DIM))
        step = 256
        for t0 in range(0, s_len, step):
            rs = pl.ds(t0, step)
            for g in range(GQA):
                ls = [l_s[i][rs, g * HEAD_DIM:(g + 1) * HEAD_DIM] for i in range(n_br)]
                m = functools.reduce(jnp.maximum, ls)
                es = [jnp.exp(l - m) for l in ls]
                tot = functools.reduce(lambda a, c: a + c, es)
                inv = 1.0 / tot
                acc = None
                for i in range(n_br):
                    term = (es[i] * inv) * o_s[i][rs, g * HEAD_DIM:(g + 1) * HEAD_DIM]
                    acc = term if acc is None else acc + term
                o_ref[0, rs, g * HEAD_DIM:(g + 1) * HEAD_DIM] = acc
                lse_ref[0, rs, g * HEAD_DIM:(g + 1) * HEAD_DIM] = m + jnp.log(tot)

    return pl.pallas_call(
        body, name=name, grid=(b, N_KV_HEADS), in_specs=[q_spec, kv_spec, t_spec],
        out_specs=[o_spec, o_spec],
        out_shape=[jax.ShapeDtypeStruct((b, s_len, ATTN_WIDTH), F32)] * 2,
        scratch_shapes=[pltpu.VMEM((s_len, Q_GROUP_W), F32), pltpu.VMEM((ATTN_PAD + s_len, HEAD_DIM), F32),
                        pltpu.VMEM((ATTN_PAD + s_len, HEAD_DIM), F32)] + [pltpu.VMEM((s_len, Q_GROUP_W), F32)] * (2 * n_br),
        compiler_params=pltpu.CompilerParams(dimension_semantics=("arbitrary", "arbitrary"), vmem_limit_bytes=ATTN_VMEM_LIMIT),
    )(proj3, proj3, rope_tab)


def attn_bwd(proj3, rope_tab, attn3, lse3, d_attn3, *, name):
    b, s_len, _ = proj3.shape
    q_spec, kv_spec, t_spec, o_spec = _attn_in_specs(s_len)
    kv_out = pl.BlockSpec((1, 1, s_len, HEAD_DIM), lambda bi, h: (bi, h, 0, 0))

    def body(q_ref, kv_ref, tab_ref, o_ref, lse_ref, do_ref, dq_ref, dk_ref, dv_ref,
             q_s, k_s, v_s, dl_s, dq_s, dk_s, dv_s):
        _attn_prologue(q_ref, kv_ref, tab_ref, q_s, k_s, v_s, pl.program_id(1), s_len)
        dq_s[...] = jnp.zeros_like(dq_s)
        dk_s[...] = jnp.zeros_like(dk_s)
        dv_s[...] = jnp.zeros_like(dv_s)
        for g in range(GQA):
            cols = slice(g * HEAD_DIM, (g + 1) * HEAD_DIM)
            delta = jnp.sum(do_ref[0, :, cols] * o_ref[0, :, cols], axis=1, keepdims=True)
            dl_s[:, cols] = jnp.broadcast_to(delta, (s_len, HEAD_DIM))
        for i, d, q0, k0, n in _attn_blocks(s_len):
            qrows, krows = _rows(q0, ATTN_BLOCK, d), _rows(k0, 2 * ATTN_BLOCK, d)
            qv = _stack_heads(q_s[qrows, :]).astype(BF16)
            kk = k_s[krows, :].astype(BF16)
            vv = v_s[krows, :].astype(BF16)
            do16 = _stack_heads(do_ref.at[0][qrows, :]).astype(BF16)
            lse = _stack_stats(lse_ref.at[0][qrows, :])
            delta = _stack_stats(dl_s[qrows, :])
            sc = lax.dot_general(qv, kk, (((1,), (1,)), ((), ())), preferred_element_type=F32)
            pr = jnp.where(_attn_mask(n), jnp.exp(sc - lse), 0.0)
            dv = lax.dot_general(pr.astype(BF16), do16, (((0,), (0,)), ((), ())), preferred_element_type=F32)
            dp = lax.dot_general(do16, vv, (((1,), (1,)), ((), ())), preferred_element_type=F32)
            ds = (pr * (dp - delta)).astype(BF16)
            dq = jnp.dot(ds, kk, preferred_element_type=F32)
            dk = lax.dot_general(ds, qv, (((0,), (0,)), ((), ())), preferred_element_type=F32)
            for g in range(GQA):
                cols = slice(g * HEAD_DIM, (g + 1) * HEAD_DIM)
                dq_s[qrows, cols] += dq[g * ATTN_BLOCK:(g + 1) * ATTN_BLOCK]
            dk_s[krows, :] += dk
            dv_s[krows, :] += dv
        swap = _swap_matrix()
        cos_v, sin_v = tab_ref[0, :, :HEAD_DIM], tab_ref[0, :, HEAD_DIM:]
        for g in range(GQA):
            cols = slice(g * HEAD_DIM, (g + 1) * HEAD_DIM)
            dq_ref[0, :, cols] = _rope(dq_s[:, cols], cos_v, sin_v, swap, HEAD_DIM ** -0.5, True)
        dk_ref[0, 0] = _rope(dk_s[ATTN_PAD:ATTN_PAD + s_len, :], cos_v, sin_v, swap, 1.0, True)
        dv_ref[0, 0] = dv_s[ATTN_PAD:ATTN_PAD + s_len, :]

    kv_shape = jax.ShapeDtypeStruct((b, N_KV_HEADS, s_len, HEAD_DIM), F32)
    return pl.pallas_call(
        body, name=name, grid=(b, N_KV_HEADS),
        in_specs=[q_spec, kv_spec, t_spec, o_spec, o_spec, o_spec],
        out_specs=[o_spec, kv_out, kv_out],
        out_shape=[jax.ShapeDtypeStruct((b, s_len, ATTN_WIDTH), F32), kv_shape, kv_shape],
        scratch_shapes=[pltpu.VMEM((s_len, Q_GROUP_W), F32), pltpu.VMEM((ATTN_PAD + s_len, HEAD_DIM), F32),
                        pltpu.VMEM((ATTN_PAD + s_len, HEAD_DIM), F32), pltpu.VMEM((s_len, Q_GROUP_W), F32),
                        pltpu.VMEM((s_len, Q_GROUP_W), F32), pltpu.VMEM((ATTN_PAD + s_len, HEAD_DIM), F32),
                        pltpu.VMEM((ATTN_PAD + s_len, HEAD_DIM), F32)],
        compiler_params=pltpu.CompilerParams(dimension_semantics=("arbitrary", "arbitrary"), vmem_limit_bytes=ATTN_VMEM_LIMIT),
    )(proj3, proj3, rope_tab, attn3, lse3, d_attn3)


HALF_W = 2 * HEAD_DIM
N_HALF = Q_GROUP_W // HALF_W
_ATTN_BIAS_BUF = pltpu.VMEM((2, GQA * ATTN_BLOCK, 2 * ATTN_BLOCK), F32)


def _attn_bias(bias_s):
    for first in (0, 1):
        bias_s[first] = jnp.where(_attn_mask(first), 0.0, NEG_BIG)


def _attn_prologue(q_refs, kv_ref, tab_ref, q_s, kv_s, hk, s_len):
    swap = _swap_matrix()
    cos_v, sin_v = tab_ref[0, :, :HEAD_DIM], tab_ref[0, :, HEAD_DIM:]
    for j in range(N_HALF):
        for e in range(2):
            cols = slice(e * HEAD_DIM, (e + 1) * HEAD_DIM)
            q_s[j][:, cols] = _rope(q_refs[j][0, :, cols], cos_v, sin_v, swap, HEAD_DIM ** -0.5, False)
    kv_s[0:ATTN_PAD, :] = jnp.zeros((ATTN_PAD, HALF_W), F32)
    for h in range(N_KV_HEADS):
        @pl.when(hk == h)
        def _():
            kv_s[ATTN_PAD:ATTN_PAD + s_len, :HEAD_DIM] = _rope(kv_ref[0, :, h * HEAD_DIM:(h + 1) * HEAD_DIM], cos_v, sin_v,
                                                               swap, 1.0, False)
            kv_s[ATTN_PAD:ATTN_PAD + s_len, HEAD_DIM:] = kv_ref[0, :, LANE + h * HEAD_DIM:LANE + (h + 1) * HEAD_DIM]


def _stack_heads(halves):
    return jnp.concatenate([h[:, e * HEAD_DIM:(e + 1) * HEAD_DIM] for h in halves for e in range(2)], axis=0)


def _unstack_heads(x, j):
    return jnp.concatenate([x[(2 * j + e) * ATTN_BLOCK:(2 * j + e + 1) * ATTN_BLOCK] for e in range(2)], axis=1)


def _stack_stats(halves):
    return jnp.concatenate([jnp.max(h[:, e * HEAD_DIM:(e + 1) * HEAD_DIM], axis=1, keepdims=True)
                            for h in halves for e in range(2)], axis=0)


def _attn_in_specs(s_len):
    assert K_COL % (2 * LANE) == 0 and V_COL == K_COL + LANE

    def halves(first_tile):
        return [pl.BlockSpec((1, s_len, HALF_W), functools.partial(lambda b, h, j: (b, 0, first_tile + N_HALF * h + j), j=j))
                for j in range(N_HALF)]

    kv_spec = pl.BlockSpec((1, s_len, 2 * LANE), lambda b, h: (b, 0, K_COL // (2 * LANE)))
    t_spec = pl.BlockSpec((1, s_len, 2 * HEAD_DIM), lambda b, h: (b, 0, 0))
    o_spec = pl.BlockSpec((1, s_len, Q_GROUP_W), lambda b, h: (b, 0, h))
    return halves(Q_COL // HALF_W), kv_spec, t_spec, o_spec, halves(0)


class SideCopy:
    def __init__(self, side, *, n_in, n_out, grid):
        self.side, self.n_in, self.n_out, self.grid = side, n_in, n_out, grid
        hbm = pl.BlockSpec(memory_space=pltpu.HBM)
        if side is None:
            self.in_specs, self.out_specs, self.out_shape, self.scratch, self.args = [], [], [], [], []
            return
        srcs, per_dest = side
        n = len(srcs)
        self.in_specs, self.out_specs, self.args = [hbm] * n, [hbm] * n, list(srcs)
        self.out_shape = [jax.ShapeDtypeStruct(s.shape if per_dest else (N_CHIPS,) + s.shape, s.dtype) for s in srcs]
        self.scratch = [pltpu.SemaphoreType.DMA(((N_CHIPS - 1) * n,)), pltpu.SemaphoreType.DMA(((N_CHIPS - 1) * n,)),
                        pltpu.SemaphoreType.DMA((n,))]

    def wrap(self, body):
        if self.side is None:
            return body
        n_in, n_out, grid, per_dest, n = self.n_in, self.n_out, self.grid, self.side[1], len(self.side[0])

        def wrapped(*refs):
            ins, srcs = refs[:n_in], refs[n_in:n_in + n]
            outs, dsts = refs[n_in + n:n_in + n + n_out], refs[n_in + n + n_out:n_in + 2 * n + n_out]
            scratch, sems = refs[n_in + 2 * n + n_out:-3], refs[-3:]
            ids = [pl.program_id(a) for a in range(len(grid))]
            first = functools.reduce(lambda p, q: p & q, [i == 0 for i in ids])
            last = functools.reduce(lambda p, q: p & q, [i == g - 1 for i, g in zip(ids, grid)])

            @pl.when(first)
            def _():
                for a in range(n):
                    local, sends, _ = _chip_copies(srcs[a], dsts[a], *sems, per_dest, a)
                    local.start()
                    for cp in sends:
                        cp.start()

            body(*ins, *outs, *scratch)

            @pl.when(last)
            def _():
                for a in range(n):
                    local, sends, recvs = _chip_copies(srcs[a], dsts[a], *sems, per_dest, a)
                    for cp in recvs:
                        cp.wait_recv()
                    for cp in sends:
                        cp.wait_send()
                    local.wait()

        return wrapped


def _chip_copies(src_ref, dst_ref, send_sems, recv_sems, local_sems, per_dest, a=0):
    x, y, c = lax.axis_index("x"), lax.axis_index("y"), lax.axis_index("c")
    chip = 2 * x + y
    own = src_ref.at[chip] if per_dest else src_ref
    local = pltpu.make_async_copy(own, dst_ref.at[chip], local_sems.at[a])
    sends, recvs = [], []
    for k, (px, py) in enumerate([(1 - x, y), (x, 1 - y), (1 - x, 1 - y)]):
        k = (N_CHIPS - 1) * a + k
        peer = dict(send_sem=send_sems.at[k], recv_sem=recv_sems.at[k], device_id=(px, py, c), device_id_type=MESH)
        sends.append(pltpu.make_async_remote_copy(src_ref=src_ref.at[2 * px + py] if per_dest else src_ref,
                                                  dst_ref=dst_ref.at[chip], **peer))
        recvs.append(pltpu.make_async_remote_copy(src_ref=own, dst_ref=dst_ref.at[2 * px + py], **peer))
    return local, sends, recvs


def attn_fwd(proj3, rope_tab, *, name, side=None):
    b, s_len, _ = proj3.shape
    q_specs, kv_spec, t_spec, o_spec, _ = _attn_in_specs(s_len)
    n_br = len(DILATIONS)

    def body(*refs):
        q_refs, (kv_ref, tab_ref, o_ref, lse_ref) = refs[:N_HALF], refs[N_HALF:N_HALF + 4]
        scratch = refs[N_HALF + 4:]
        q_s, kv_s = scratch[:N_HALF], scratch[N_HALF]
        o_s = [scratch[N_HALF + 1 + i * N_HALF:N_HALF + 1 + (i + 1) * N_HALF] for i in range(n_br)]
        l_s = [scratch[N_HALF + 1 + (n_br + i) * N_HALF:N_HALF + 1 + (n_br + i + 1) * N_HALF] for i in range(n_br)]
        bias_s = scratch[-1]
        _attn_prologue(q_refs, kv_ref, tab_ref, q_s, kv_s, pl.program_id(1), s_len)
        _attn_bias(bias_s)
        for i, d, q0, k0, n in _attn_blocks(s_len):
            qrows = _rows(q0, ATTN_BLOCK, d)
            qv = _stack_heads([q_s[j][qrows, :] for j in range(N_HALF)]).astype(BF16)
            kvb = kv_s[_rows(k0, 2 * ATTN_BLOCK, d), :].astype(BF16)
            kk, vv = kvb[:, :HEAD_DIM], kvb[:, HEAD_DIM:]
            sc = lax.dot_general(qv, kk, (((1,), (1,)), ((), ())), preferred_element_type=F32)
            sc = sc + bias_s[min(n, 1)]
            m = jnp.max(sc, axis=-1, keepdims=True)
            pr = jnp.exp(sc - m)
            den = jnp.sum(pr, axis=-1, keepdims=True)
            o = jnp.dot(pr.astype(BF16), vv, preferred_element_type=F32) / den
            lse_b = jnp.broadcast_to(m + jnp.log(den), (GQA * ATTN_BLOCK, HEAD_DIM))
            for j in range(N_HALF):
                o_s[i][j][qrows, :] = _unstack_heads(o, j)
                l_s[i][j][qrows, :] = _unstack_heads(lse_b, j)
        step = 256
        for t0 in range(0, s_len, step):
            rs = pl.ds(t0, step)
            for j in range(N_HALF):
                ls = [l_s[i][j][rs, :] for i in range(n_br)]
                m = functools.reduce(jnp.maximum, ls)
                es = [jnp.exp(l - m) for l in ls]
                tot = functools.reduce(lambda a, c: a + c, es)
                inv = 1.0 / tot
                acc = None
                for i in range(n_br):
                    term = (es[i] * inv) * o_s[i][j][rs, :]
                    acc = term if acc is None else acc + term
                o_ref[0, rs, j * HALF_W:(j + 1) * HALF_W] = acc
                lse_ref[0, rs, j * HALF_W:(j + 1) * HALF_W] = m + jnp.log(tot)

    half_buf = pltpu.VMEM((s_len, HALF_W), F32)
    call = SideCopy(side, n_in=N_HALF + 2, n_out=2, grid=(b, N_KV_HEADS))
    return pl.pallas_call(
        call.wrap(body), name=name, grid=(b, N_KV_HEADS), in_specs=q_specs + [kv_spec, t_spec] + call.in_specs,
        out_specs=[o_spec, o_spec] + call.out_specs,
        out_shape=[jax.ShapeDtypeStruct((b, s_len, ATTN_WIDTH), F32)] * 2 + call.out_shape,
        scratch_shapes=[half_buf] * N_HALF + [pltpu.VMEM((ATTN_PAD + s_len, HALF_W), F32)] + [half_buf] * (2 * n_br * N_HALF)
        + [_ATTN_BIAS_BUF] + call.scratch,
        compiler_params=pltpu.CompilerParams(dimension_semantics=("arbitrary", "arbitrary"), vmem_limit_bytes=ATTN_VMEM_LIMIT),
    )(*([proj3] * (N_HALF + 1)), rope_tab, *call.args)


def attn_bwd(proj3, rope_tab, attn3, lse3, d_attn3, *, name, side=None):
    b, s_len, _ = proj3.shape
    q_specs, kv_spec, t_spec, o_spec, half_specs = _attn_in_specs(s_len)
    kv_out = pl.BlockSpec((1, 1, s_len, HEAD_DIM), lambda bi, h: (bi, h, 0, 0))

    def body(*refs):
        q_refs = refs[:N_HALF]
        kv_ref, tab_ref, o_ref = refs[N_HALF:N_HALF + 3]
        lse_refs = refs[N_HALF + 3:2 * N_HALF + 3]
        do_refs = refs[2 * N_HALF + 3:3 * N_HALF + 3]
        dq_ref, dk_ref, dv_ref = refs[3 * N_HALF + 3:3 * N_HALF + 6]
        scratch = refs[3 * N_HALF + 6:]
        q_s, kv_s = scratch[:N_HALF], scratch[N_HALF]
        dl_s = scratch[N_HALF + 1:2 * N_HALF + 1]
        dq_s = scratch[2 * N_HALF + 1:3 * N_HALF + 1]
        dkv_s = scratch[3 * N_HALF + 1]
        bias_s = scratch[-1]
        _attn_prologue(q_refs, kv_ref, tab_ref, q_s, kv_s, pl.program_id(1), s_len)
        _attn_bias(bias_s)
        dkv_s[...] = jnp.zeros_like(dkv_s)
        for j in range(N_HALF):
            dq_s[j][...] = jnp.zeros_like(dq_s[j])
            for e in range(2):
                cols = slice(e * HEAD_DIM, (e + 1) * HEAD_DIM)
                ocols = slice(j * HALF_W + e * HEAD_DIM, j * HALF_W + (e + 1) * HEAD_DIM)
                delta = jnp.sum(do_refs[j][0, :, cols] * o_ref[0, :, ocols], axis=1, keepdims=True)
                dl_s[j][:, cols] = jnp.broadcast_to(delta, (s_len, HEAD_DIM))
        for i, d, q0, k0, n in _attn_blocks(s_len):
            qrows, krows = _rows(q0, ATTN_BLOCK, d), _rows(k0, 2 * ATTN_BLOCK, d)
            qv = _stack_heads([q_s[j][qrows, :] for j in range(N_HALF)]).astype(BF16)
            kvb = kv_s[krows, :].astype(BF16)
            kk, vv = kvb[:, :HEAD_DIM], kvb[:, HEAD_DIM:]
            do16 = _stack_heads([do_refs[j].at[0][qrows, :] for j in range(N_HALF)]).astype(BF16)
            lse = _stack_stats([lse_refs[j].at[0][qrows, :] for j in range(N_HALF)])
            delta = _stack_stats([dl_s[j][qrows, :] for j in range(N_HALF)])
            sc = lax.dot_general(qv, kk, (((1,), (1,)), ((), ())), preferred_element_type=F32)
            pr = jnp.exp(sc + bias_s[min(n, 1)] - lse)
            dv = lax.dot_general(pr.astype(BF16), do16, (((0,), (0,)), ((), ())), preferred_element_type=F32)
            dp = lax.dot_general(do16, vv, (((1,), (1,)), ((), ())), preferred_element_type=F32)
            ds = (pr * (dp - delta)).astype(BF16)
            dq = jnp.dot(ds, kk, preferred_element_type=F32)
            dk = lax.dot_general(ds, qv, (((0,), (0,)), ((), ())), preferred_element_type=F32)
            for j in range(N_HALF):
                dq_s[j][qrows, :] += _unstack_heads(dq, j)
            dkv_s[krows, :] += jnp.concatenate([dk, dv], axis=1)
        swap = _swap_matrix()
        cos_v, sin_v = tab_ref[0, :, :HEAD_DIM], tab_ref[0, :, HEAD_DIM:]
        for j in range(N_HALF):
            for e in range(2):
                cols = slice(e * HEAD_DIM, (e + 1) * HEAD_DIM)
                ocols = slice(j * HALF_W + e * HEAD_DIM, j * HALF_W + (e + 1) * HEAD_DIM)
                dq_ref[0, :, ocols] = _rope(dq_s[j][:, cols], cos_v, sin_v, swap, HEAD_DIM ** -0.5, True).astype(dq_ref.dtype)
        dk_ref[0, 0] = _rope(dkv_s[ATTN_PAD:ATTN_PAD + s_len, :HEAD_DIM], cos_v, sin_v, swap, 1.0, True)
        dv_ref[0, 0] = dkv_s[ATTN_PAD:ATTN_PAD + s_len, HEAD_DIM:]

    kv_shape = jax.ShapeDtypeStruct((b, N_KV_HEADS, s_len, HEAD_DIM), F32)
    half_buf = pltpu.VMEM((s_len, HALF_W), F32)
    pad_buf = pltpu.VMEM((ATTN_PAD + s_len, HALF_W), F32)
    call = SideCopy(side, n_in=3 * N_HALF + 3, n_out=3, grid=(b, N_KV_HEADS))
    return pl.pallas_call(
        call.wrap(body), name=name, grid=(b, N_KV_HEADS),
        in_specs=q_specs + [kv_spec, t_spec, o_spec] + half_specs + half_specs + call.in_specs,
        out_specs=[o_spec, kv_out, kv_out] + call.out_specs,
        out_shape=[jax.ShapeDtypeStruct((b, s_len, ATTN_WIDTH), BF16), kv_shape, kv_shape] + call.out_shape,
        scratch_shapes=[half_buf] * N_HALF + [pad_buf] + [half_buf] * (2 * N_HALF) + [pad_buf, _ATTN_BIAS_BUF] + call.scratch,
        compiler_params=pltpu.CompilerParams(dimension_semantics=("arbitrary", "arbitrary"), vmem_limit_bytes=ATTN_VMEM_LIMIT),
    )(*([proj3] * (N_HALF + 1)), rope_tab, attn3, *([lse3] * N_HALF), *([d_attn3] * N_HALF), *call.args)


CONV_TC = 256
CONV_COL0 = XBC_COL // CONV_TC


def _shift_down(u, s):
    if s == 0:
        return u
    rows = lax.broadcasted_iota(jnp.int32, u.shape, 0)
    return jnp.where(rows >= s, pltpu.roll(u, s, 0), 0.0)


def _shift_up(u, s):
    if s == 0:
        return u
    n = u.shape[0]
    rows = lax.broadcasted_iota(jnp.int32, u.shape, 0)
    return jnp.where(rows < n - s, pltpu.roll(u, n - s, 0), 0.0)


def conv_silu_fwd(proj3, w, bias, *, name):
    b, s, _ = proj3.shape
    u_spec = pl.BlockSpec((1, s, CONV_TC), lambda j, bi: (bi, 0, CONV_COL0 + j))
    o_spec = pl.BlockSpec((1, s, CONV_TC), lambda j, bi: (bi, 0, j))
    w_spec = pl.BlockSpec((CONV_WIDTH, CONV_TC), lambda j, bi: (0, j))
    b_spec = pl.BlockSpec((1, CONV_TC), lambda j, bi: (0, j))

    def body(u_ref, w_ref, b_ref, o_ref):
        u = u_ref[0]
        y = jnp.broadcast_to(b_ref[...], u.shape)
        for k in range(CONV_WIDTH):
            y = y + w_ref[k:k + 1, :] * _shift_down(u, CONV_WIDTH - 1 - k)
        o_ref[0] = y * jax.nn.sigmoid(y)

    return pl.pallas_call(
        body, name=name, grid=(CONV_CH // CONV_TC, b), in_specs=[u_spec, w_spec, b_spec], out_specs=o_spec,
        out_shape=jax.ShapeDtypeStruct((b, s, CONV_CH), F32),
        compiler_params=_params(("parallel", "arbitrary")),
    )(proj3, w, bias)


def conv_silu_bwd(proj3, w, bias, dact, *, name):
    b, s, _ = proj3.shape
    u_spec = pl.BlockSpec((1, s, CONV_TC), lambda j, bi: (bi, 0, CONV_COL0 + j))
    o_spec = pl.BlockSpec((1, s, CONV_TC), lambda j, bi: (bi, 0, j))
    w_spec = pl.BlockSpec((CONV_WIDTH, CONV_TC), lambda j, bi: (0, j))
    b_spec = pl.BlockSpec((1, CONV_TC), lambda j, bi: (0, j))

    def body(u_ref, w_ref, b_ref, g_ref, du_ref, dw_ref, db_ref):
        bi = pl.program_id(1)

        @pl.when(bi == 0)
        def _():
            dw_ref[...] = jnp.zeros_like(dw_ref)
            db_ref[...] = jnp.zeros_like(db_ref)

        u = u_ref[0]
        y = jnp.broadcast_to(b_ref[...], u.shape)
        shifted = [_shift_down(u, CONV_WIDTH - 1 - k) for k in range(CONV_WIDTH)]
        for k in range(CONV_WIDTH):
            y = y + w_ref[k:k + 1, :] * shifted[k]
        sig = jax.nn.sigmoid(y)
        dy = g_ref[0] * (sig * (1.0 + y * (1.0 - sig)))
        du = jnp.zeros_like(u)
        for k in range(CONV_WIDTH):
            du = du + w_ref[k:k + 1, :] * _shift_up(dy, CONV_WIDTH - 1 - k)
            dw_ref[k:k + 1, :] += jnp.sum(dy * shifted[k], axis=0, keepdims=True)
        du_ref[0] = du.astype(du_ref.dtype)
        db_ref[...] += jnp.sum(dy, axis=0, keepdims=True)

    return pl.pallas_call(
        body, name=name, grid=(CONV_CH // CONV_TC, b), in_specs=[u_spec, w_spec, b_spec, o_spec],
        out_specs=[o_spec, w_spec, b_spec],
        out_shape=[jax.ShapeDtypeStruct((b, s, CONV_CH), BF16), jax.ShapeDtypeStruct((CONV_WIDTH, CONV_CH), F32),
                   jax.ShapeDtypeStruct((1, CONV_CH), F32)],
        compiler_params=_params(("parallel", "arbitrary")),
    )(proj3, w, bias, dact)


def _softplus(z):
    e = jnp.exp(-jnp.abs(z))
    u = 1.0 + e
    log1p = jnp.where(u == 1.0, e, jnp.log(u) * e / jnp.where(u == 1.0, 1.0, u - 1.0))
    return jnp.maximum(z, 0.0) + log1p


def _tri(lower):
    r = lax.broadcasted_iota(jnp.int32, (CHUNK, CHUNK), 0)
    c = lax.broadcasted_iota(jnp.int32, (CHUNK, CHUNK), 1)
    return (r >= c) if lower else (r <= c)


def _ssd_common(dtr_ref, dtb_ref, alog_ref):
    z = dtr_ref[0] + dtb_ref[...]
    dt = _softplus(z)
    aneg = -jnp.exp(alog_ref[...])
    acs = _dot01_left(_tri(True).astype(BF16), dt * aneg)
    return z, dt, aneg, acs


def _col(mat, onehot):
    return jnp.sum(mat * onehot, axis=1, keepdims=True)


def _ssd_head(x, dt_j, acs_j, cb, tri_mask, last_row, acs_row=None):
    acs_last = jnp.sum(acs_j * last_row, axis=0, keepdims=True)
    xg = x * dt_j
    bc = jnp.broadcast_to(acs_j, (CHUNK, CHUNK))
    dm = bc - (bc.T if acs_row is None else jnp.broadcast_to(acs_row, (CHUNK, CHUNK)))
    lm = jnp.where(tri_mask, jnp.exp(jnp.where(tri_mask, dm, 0.0)), 0.0)
    mm = cb * lm
    decay_s = jnp.exp(acs_last - acs_j)
    return acs_last, xg, lm, mm, decay_s


def _ssd_specs(nc, reverse):
    cidx = (lambda c: nc - 1 - c) if reverse else (lambda c: c)
    act_spec = pl.BlockSpec((1, CHUNK, CONV_CH), lambda b, c: (b, cidx(c), 0))
    y_spec = pl.BlockSpec((1, CHUNK, SSM_INNER), lambda b, c: (b, cidx(c), 0))
    dt_in_spec = pl.BlockSpec((1, CHUNK, LANE), lambda b, c: (b, cidx(c), DT_COL // LANE))
    dt_out_spec = pl.BlockSpec((1, CHUNK, LANE), lambda b, c: (b, cidx(c), 0))
    par_spec = pl.BlockSpec((1, LANE), lambda b, c: (0, 0))
    h_spec = pl.BlockSpec((1, SSM_HEADS, 1, SSM_P, D_STATE), lambda b, c: (b, 0, cidx(c), 0, 0))
    return act_spec, y_spec, dt_in_spec, dt_out_spec, par_spec, h_spec


def _head_cols(h):
    return slice(h * SSM_P, (h + 1) * SSM_P)


def _group_cols(g, which):
    start = SSM_INNER + which * SSM_GROUPS * D_STATE + g * D_STATE
    return slice(start, start + D_STATE)


def ssd_fwd(act3, proj3, dtb, alog, dsk, *, name):
    b, s, _ = act3.shape
    nc = s // CHUNK
    act_spec, y_spec, dt_in_spec, _, par_spec, h_spec = _ssd_specs(nc, False)

    def body(act_ref, dtr_ref, dtb_ref, alog_ref, dsk_ref, y_ref, hp_ref, state):
        c = pl.program_id(1)

        @pl.when(c == 0)
        def _():
            state[...] = jnp.zeros_like(state)

        _, dt, _, acs = _ssd_common(dtr_ref, dtb_ref, alog_ref)
        acs_t = acs.T
        tri_mask = _tri(True)
        last_row = (lax.broadcasted_iota(jnp.int32, (CHUNK, 1), 0) == CHUNK - 1).astype(F32)
        for g in range(SSM_GROUPS):
            b16 = act_ref[0, :, _group_cols(g, 0)].astype(BF16)
            c16 = act_ref[0, :, _group_cols(g, 1)].astype(BF16)
            cb = lax.dot_general(c16, b16, (((1,), (1,)), ((), ())), preferred_element_type=F32)
            for j in range(HEADS_PER_GROUP):
                hidx = g * HEADS_PER_GROUP + j
                x = act_ref[0, :, _head_cols(hidx)]
                dt_j, acs_j = dt[:, hidx:hidx + 1], acs[:, hidx:hidx + 1]
                acs_last, xg, _, mm, decay_s = _ssd_head(x, dt_j, acs_j, cb, tri_mask, last_row, acs_t[hidx:hidx + 1, :])
                y_diag = jnp.dot(mm.astype(BF16), xg.astype(BF16), preferred_element_type=F32)
                st = lax.dot_general((xg * decay_s).astype(BF16), b16, (((0,), (0,)), ((), ())), preferred_element_type=F32)
                hp = state[hidx]
                hp_ref[0, hidx, 0] = hp
                y_off = lax.dot_general(c16, hp.astype(BF16), (((1,), (1,)), ((), ())), preferred_element_type=F32)
                d_j = dsk_ref[:, hidx:hidx + 1]
                y_ref[0, :, _head_cols(hidx)] = y_diag + y_off * jnp.exp(acs_j) + d_j * x
                state[hidx] = hp * jnp.exp(acs_last) + st

    return pl.pallas_call(
        body, name=name, grid=(b, nc),
        in_specs=[act_spec, dt_in_spec, par_spec, par_spec, par_spec],
        out_specs=[y_spec, h_spec],
        out_shape=[jax.ShapeDtypeStruct((b, s, SSM_INNER), F32),
                   jax.ShapeDtypeStruct((b, SSM_HEADS, nc, SSM_P, D_STATE), F32)],
        scratch_shapes=[pltpu.VMEM((SSM_HEADS, SSM_P, D_STATE), F32)],
        compiler_params=_params(("arbitrary", "arbitrary")),
    )(act3, proj3, dtb, alog, dsk)


def ssd_bwd(act3, proj3, dtb, alog, dsk, hprev, dy3, *, name):
    b, s, _ = act3.shape
    nc = s // CHUNK
    act_spec, y_spec, dt_in_spec, dt_out_spec, par_spec, h_spec = _ssd_specs(nc, True)
    dpar_spec = pl.BlockSpec((8, LANE), lambda bi, c: (0, 0))

    def body(act_ref, dtr_ref, dtb_ref, alog_ref, dsk_ref, hp_ref, dy_ref, dact_ref, ddtr_ref, dpar_ref, dstate):
        bi, c = pl.program_id(0), pl.program_id(1)

        @pl.when(c == 0)
        def _():
            dstate[...] = jnp.zeros_like(dstate)

        @pl.when((bi == 0) & (c == 0))
        def _():
            dpar_ref[...] = jnp.zeros_like(dpar_ref)

        z, dt, aneg, acs = _ssd_common(dtr_ref, dtb_ref, alog_ref)
        acs_t = acs.T
        tri_mask = _tri(True)
        last_row = (lax.broadcasted_iota(jnp.int32, (CHUNK, 1), 0) == CHUNK - 1).astype(F32)
        lanes = lax.broadcasted_iota(jnp.int32, (1, LANE), 1)
        sublanes = lax.broadcasted_iota(jnp.int32, (LANE, 1), 0)
        ddt_mat = jnp.zeros((CHUNK, LANE), F32)
        dacs_mat = jnp.zeros((CHUNK, LANE), F32)
        dacs_rows = jnp.zeros((LANE, CHUNK), F32)
        ddsk_row = jnp.zeros((1, LANE), F32)
        for g in range(SSM_GROUPS):
            b16 = act_ref[0, :, _group_cols(g, 0)].astype(BF16)
            c16 = act_ref[0, :, _group_cols(g, 1)].astype(BF16)
            cb = lax.dot_general(c16, b16, (((1,), (1,)), ((), ())), preferred_element_type=F32)
            dcb = jnp.zeros((CHUNK, CHUNK), F32)
            db_acc = jnp.zeros((CHUNK, D_STATE), F32)
            dc_acc = jnp.zeros((CHUNK, D_STATE), F32)
            for j in range(HEADS_PER_GROUP):
                hidx = g * HEADS_PER_GROUP + j
                onehot = (lanes == hidx).astype(F32)
                x = act_ref[0, :, _head_cols(hidx)]
                dt_j, acs_j = dt[:, hidx:hidx + 1], acs[:, hidx:hidx + 1]
                acs_last, xg, lm, mm, decay_s = _ssd_head(x, dt_j, acs_j, cb, tri_mask, last_row, acs_t[hidx:hidx + 1, :])
                ea = jnp.exp(acs_j)
                cd = jnp.exp(acs_last)
                d_j = dsk_ref[:, hidx:hidx + 1]
                hp = hp_ref[0, hidx, 0]
                hp16 = hp.astype(BF16)
                g_y = dy_ref[0, :, _head_cols(hidx)]
                g_y16 = g_y.astype(BF16)
                g_hn = dstate[hidx]
                g_hn16 = g_hn.astype(BF16)
                xg16 = xg.astype(BF16)
                ddsk_row = ddsk_row + jnp.sum(jnp.sum(g_y * x, axis=1, keepdims=True), axis=0, keepdims=True) * onehot
                d_mm = lax.dot_general(g_y16, xg16, (((1,), (1,)), ((), ())), preferred_element_type=F32)
                d_xg = lax.dot_general(mm.astype(BF16), g_y16, (((0,), (0,)), ((), ())), preferred_element_type=F32)
                dcb = dcb + d_mm * lm
                d_dm = d_mm * mm
                d_acs = jnp.sum(d_dm, axis=1, keepdims=True)
                dacs_rows = dacs_rows + (sublanes == hidx).astype(F32) * jnp.sum(d_dm, axis=0, keepdims=True)
                t_off = lax.dot_general(c16, hp16, (((1,), (1,)), ((), ())), preferred_element_type=F32)
                d_t16 = (g_y * ea).astype(BF16)
                d_acs = d_acs + jnp.sum(g_y * t_off, axis=1, keepdims=True) * ea
                dc_acc = dc_acc + jnp.dot(d_t16, hp16, preferred_element_type=F32)
                d_hp = lax.dot_general(d_t16, c16, (((0,), (0,)), ((), ())), preferred_element_type=F32) + g_hn * cd
                d_last = jnp.sum(jnp.sum(g_hn * hp, axis=1, keepdims=True), axis=0, keepdims=True) * cd
                d_w = lax.dot_general(b16, g_hn16, (((1,), (1,)), ((), ())), preferred_element_type=F32)
                db_acc = db_acc + jnp.dot((xg * decay_s).astype(BF16), g_hn16, preferred_element_type=F32)
                d_xg = d_xg + d_w * decay_s
                d_ds = jnp.sum(d_w * xg, axis=1, keepdims=True) * decay_s
                d_last = d_last + jnp.sum(d_ds, axis=0, keepdims=True)
                d_acs = d_acs - d_ds + d_last * last_row
                dact_ref[0, :, _head_cols(hidx)] = d_j * g_y + d_xg * dt_j
                ddt_mat = ddt_mat + jnp.sum(d_xg * x, axis=1, keepdims=True) * onehot
                dacs_mat = dacs_mat + d_acs * onehot
                dstate[hidx] = d_hp
            dcb16 = dcb.astype(BF16)
            dact_ref[0, :, _group_cols(g, 1)] = dc_acc + jnp.dot(dcb16, b16, preferred_element_type=F32)
            dact_ref[0, :, _group_cols(g, 0)] = db_acc + lax.dot_general(dcb16, c16, (((0,), (0,)), ((), ())),
                                                                         preferred_element_type=F32)
        d_a = _dot01_left(_tri(False).astype(BF16), dacs_mat - dacs_rows.T)
        ddt_mat = ddt_mat + d_a * aneg
        d_raw = ddt_mat * jax.nn.sigmoid(z)
        ddtr_ref[0] = d_raw
        dpar_ref[0:1, :] += jnp.sum(d_raw, axis=0, keepdims=True)
        dpar_ref[1:2, :] += jnp.sum(d_a * dt, axis=0, keepdims=True) * aneg
        dpar_ref[2:3, :] += ddsk_row

    return pl.pallas_call(
        body, name=name, grid=(b, nc),
        in_specs=[act_spec, dt_in_spec, par_spec, par_spec, par_spec, h_spec, y_spec],
        out_specs=[act_spec, dt_out_spec, dpar_spec],
        out_shape=[jax.ShapeDtypeStruct(act3.shape, F32), jax.ShapeDtypeStruct((b, s, LANE), F32),
                   jax.ShapeDtypeStruct((8, LANE), F32)],
        scratch_shapes=[pltpu.VMEM((SSM_HEADS, SSM_P, D_STATE), F32)],
        compiler_params=_params(("arbitrary", "arbitrary")),
    )(act3, proj3, dtb, alog, dsk, hprev, dy3)


def _unused_ssd_specs(nc, reverse):
    cidx = (lambda c: nc - 1 - c) if reverse else (lambda c: c)
    x_spec = pl.BlockSpec((1, HEADS_PER_GROUP, CHUNK, SSM_P), lambda b, c, g: (b, g, cidx(c), 0))
    bc_spec = pl.BlockSpec((1, 1, CHUNK, D_STATE), lambda b, c, g: (b, g, cidx(c), 0))
    dt_spec = pl.BlockSpec((1, CHUNK, LANE), lambda b, c, g: (b, cidx(c), 0))
    par_spec = pl.BlockSpec((1, LANE), lambda b, c, g: (0, 0))
    h_spec = pl.BlockSpec((1, HEADS_PER_GROUP, 1, SSM_P, D_STATE), lambda b, c, g: (b, g, cidx(c), 0, 0))
    return x_spec, bc_spec, dt_spec, par_spec, h_spec


def _unused_ssd_fwd(xs, bm, cm, dtr, dtb, alog, dsk, *, name):
    b, _, s, _ = xs.shape
    nc = s // CHUNK
    x_spec, bc_spec, dt_spec, par_spec, h_spec = _ssd_specs(nc, False)

    def body(x_ref, b_ref, c_ref, dtr_ref, dtb_ref, alog_ref, dsk_ref, y_ref, hp_ref, state):
        c, g = pl.program_id(1), pl.program_id(2)

        @pl.when(c == 0)
        def _():
            state[pl.ds(g * HEADS_PER_GROUP, HEADS_PER_GROUP)] = jnp.zeros((HEADS_PER_GROUP, SSM_P, D_STATE), F32)

        _, dt, _, acs = _ssd_common(dtr_ref, dtb_ref, alog_ref)
        b16, c16 = b_ref[0, 0].astype(BF16), c_ref[0, 0].astype(BF16)
        cb = lax.dot_general(c16, b16, (((1,), (1,)), ((), ())), preferred_element_type=F32)
        tri_mask = _tri(True)
        last_row = (lax.broadcasted_iota(jnp.int32, (CHUNK, 1), 0) == CHUNK - 1).astype(F32)
        lanes = lax.broadcasted_iota(jnp.int32, (1, LANE), 1)
        for j in range(HEADS_PER_GROUP):
            hidx = g * HEADS_PER_GROUP + j
            onehot = (lanes == hidx).astype(F32)
            x = x_ref[0, j]
            dt_j, acs_j = _col(dt, onehot), _col(acs, onehot)
            acs_last, xg, _, mm, decay_s = _ssd_head(x, dt_j, acs_j, cb, tri_mask, last_row)
            xg16 = xg.astype(BF16)
            y_diag = jnp.dot(mm.astype(BF16), xg16, preferred_element_type=F32)
            st = lax.dot_general((xg * decay_s).astype(BF16), b16, (((0,), (0,)), ((), ())), preferred_element_type=F32)
            hp = state[hidx]
            hp_ref[0, j, 0] = hp
            y_off = lax.dot_general(c16, hp.astype(BF16), (((1,), (1,)), ((), ())), preferred_element_type=F32)
            d_j = jnp.sum(dsk_ref[...] * onehot, axis=1, keepdims=True)
            y_ref[0, j] = y_diag + y_off * jnp.exp(acs_j) + d_j * x
            state[hidx] = hp * jnp.exp(acs_last) + st

    return pl.pallas_call(
        body, name=name, grid=(b, nc, SSM_GROUPS),
        in_specs=[x_spec, bc_spec, bc_spec, dt_spec, par_spec, par_spec, par_spec],
        out_specs=[x_spec, h_spec],
        out_shape=[jax.ShapeDtypeStruct(xs.shape, F32),
                   jax.ShapeDtypeStruct((b, SSM_HEADS, nc, SSM_P, D_STATE), F32)],
        scratch_shapes=[pltpu.VMEM((SSM_HEADS, SSM_P, D_STATE), F32)],
        compiler_params=_params(("arbitrary", "arbitrary", "arbitrary")),
    )(xs, bm, cm, dtr, dtb, alog, dsk)


def _unused_ssd_bwd(xs, bm, cm, dtr, dtb, alog, dsk, hprev, dy, *, name):
    b, _, s, _ = xs.shape
    nc = s // CHUNK
    x_spec, bc_spec, dt_spec, par_spec, h_spec = _ssd_specs(nc, True)
    dpar_spec = pl.BlockSpec((8, LANE), lambda bi, c, g: (0, 0))

    def body(x_ref, b_ref, c_ref, dtr_ref, dtb_ref, alog_ref, dsk_ref, hp_ref, dy_ref,
             dx_ref, db_ref, dc_ref, ddtr_ref, dpar_ref, dstate):
        bi, c, g = pl.program_id(0), pl.program_id(1), pl.program_id(2)

        @pl.when(c == 0)
        def _():
            dstate[pl.ds(g * HEADS_PER_GROUP, HEADS_PER_GROUP)] = jnp.zeros((HEADS_PER_GROUP, SSM_P, D_STATE), F32)

        @pl.when((bi == 0) & (c == 0) & (g == 0))
        def _():
            dpar_ref[...] = jnp.zeros_like(dpar_ref)

        z, dt, aneg, acs = _ssd_common(dtr_ref, dtb_ref, alog_ref)
        bv, cv = b_ref[0, 0], c_ref[0, 0]
        b16, c16 = bv.astype(BF16), cv.astype(BF16)
        cb = lax.dot_general(c16, b16, (((1,), (1,)), ((), ())), preferred_element_type=F32)
        tri_mask = _tri(True)
        last_row = (lax.broadcasted_iota(jnp.int32, (CHUNK, 1), 0) == CHUNK - 1).astype(F32)
        lanes = lax.broadcasted_iota(jnp.int32, (1, LANE), 1)
        dcb = jnp.zeros((CHUNK, CHUNK), F32)
        db_acc = jnp.zeros((CHUNK, D_STATE), F32)
        dc_acc = jnp.zeros((CHUNK, D_STATE), F32)
        ddt_mat = jnp.zeros((CHUNK, LANE), F32)
        dacs_mat = jnp.zeros((CHUNK, LANE), F32)
        ddsk_row = jnp.zeros((1, LANE), F32)
        for j in range(HEADS_PER_GROUP):
            hidx = g * HEADS_PER_GROUP + j
            onehot = (lanes == hidx).astype(F32)
            x = x_ref[0, j]
            dt_j, acs_j = _col(dt, onehot), _col(acs, onehot)
            acs_last, xg, lm, mm, decay_s = _ssd_head(x, dt_j, acs_j, cb, tri_mask, last_row)
            ea = jnp.exp(acs_j)
            cd = jnp.exp(acs_last)
            d_j = jnp.sum(dsk_ref[...] * onehot, axis=1, keepdims=True)
            hp = hp_ref[0, j, 0]
            hp16 = hp.astype(BF16)
            g_y = dy_ref[0, j]
            g_y16 = g_y.astype(BF16)
            g_hn = dstate[hidx]
            g_hn16 = g_hn.astype(BF16)
            xg16 = xg.astype(BF16)
            ddsk_row = ddsk_row + jnp.sum(jnp.sum(g_y * x, axis=1, keepdims=True), axis=0, keepdims=True) * onehot
            d_mm = lax.dot_general(g_y16, xg16, (((1,), (1,)), ((), ())), preferred_element_type=F32)
            d_xg = lax.dot_general(mm.astype(BF16), g_y16, (((0,), (0,)), ((), ())), preferred_element_type=F32)
            dcb = dcb + d_mm * lm
            d_dm = d_mm * mm
            d_acs = jnp.sum(d_dm, axis=1, keepdims=True) - jnp.sum(d_dm.T, axis=1, keepdims=True)
            t_off = lax.dot_general(c16, hp16, (((1,), (1,)), ((), ())), preferred_element_type=F32)
            d_t16 = (g_y * ea).astype(BF16)
            d_acs = d_acs + jnp.sum(g_y * t_off, axis=1, keepdims=True) * ea
            dc_acc = dc_acc + jnp.dot(d_t16, hp16, preferred_element_type=F32)
            d_hp = lax.dot_general(d_t16, c16, (((0,), (0,)), ((), ())), preferred_element_type=F32) + g_hn * cd
            d_last = jnp.sum(jnp.sum(g_hn * hp, axis=1, keepdims=True), axis=0, keepdims=True) * cd
            d_w = lax.dot_general(b16, g_hn16, (((1,), (1,)), ((), ())), preferred_element_type=F32)
            db_acc = db_acc + jnp.dot((xg * decay_s).astype(BF16), g_hn16, preferred_element_type=F32)
            d_xg = d_xg + d_w * decay_s
            d_ds = jnp.sum(d_w * xg, axis=1, keepdims=True) * decay_s
            d_last = d_last + jnp.sum(d_ds, axis=0, keepdims=True)
            d_acs = d_acs - d_ds + d_last * last_row
            dx_ref[0, j] = d_j * g_y + d_xg * dt_j
            ddt_mat = ddt_mat + jnp.sum(d_xg * x, axis=1, keepdims=True) * onehot
            dacs_mat = dacs_mat + d_acs * onehot
            dstate[hidx] = d_hp
        dcb16 = dcb.astype(BF16)
        dc_ref[0, 0] = dc_acc + jnp.dot(dcb16, b16, preferred_element_type=F32)
        db_ref[0, 0] = db_acc + lax.dot_general(dcb16, c16, (((0,), (0,)), ((), ())), preferred_element_type=F32)
        d_a = _dot01_left(_tri(False).astype(BF16), dacs_mat)
        ddt_mat = ddt_mat + d_a * aneg
        d_aneg = jnp.sum(d_a * dt, axis=0, keepdims=True)
        d_raw = ddt_mat * jax.nn.sigmoid(z)

        @pl.when(g == 0)
        def _():
            ddtr_ref[0] = d_raw

        @pl.when(g != 0)
        def _():
            ddtr_ref[0] += d_raw

        dpar_ref[0:1, :] += jnp.sum(d_raw, axis=0, keepdims=True)
        dpar_ref[1:2, :] += d_aneg * aneg
        dpar_ref[2:3, :] += ddsk_row

    return pl.pallas_call(
        body, name=name, grid=(b, nc, SSM_GROUPS),
        in_specs=[x_spec, bc_spec, bc_spec, dt_spec, par_spec, par_spec, par_spec, h_spec, x_spec],
        out_specs=[x_spec, bc_spec, bc_spec, dt_spec, dpar_spec],
        out_shape=[jax.ShapeDtypeStruct(xs.shape, F32), jax.ShapeDtypeStruct(bm.shape, F32),
                   jax.ShapeDtypeStruct(cm.shape, F32), jax.ShapeDtypeStruct(dtr.shape, F32),
                   jax.ShapeDtypeStruct((8, LANE), F32)],
        scratch_shapes=[pltpu.VMEM((SSM_HEADS, SSM_P, D_STATE), F32)],
        compiler_params=_params(("arbitrary", "arbitrary", "arbitrary")),
    )(xs, bm, cm, dtr, dtb, alog, dsk, hprev, dy)


SSD_INTERLEAVE = 8


def _each(f, *lists):
    return [f(*a) for a in zip(*lists)]


def _nt(a, b):
    return lax.dot_general(a, b, (((1,), (1,)), ((), ())), preferred_element_type=F32)


def _tn(a, b):
    return lax.dot_general(a, b, (((0,), (0,)), ((), ())), preferred_element_type=F32)


def _nn(a, b):
    return jnp.dot(a, b, preferred_element_type=F32)


def _rowsum(a):
    return jnp.sum(a, axis=1, keepdims=True)


def _colsum(a):
    return jnp.sum(a, axis=0, keepdims=True)


def _bf(a):
    return a.astype(BF16)


def _head_batches(g):
    first = g * HEADS_PER_GROUP
    return [list(range(first + k, first + k + SSD_INTERLEAVE)) for k in range(0, HEADS_PER_GROUP, SSD_INTERLEAVE)]


def _decay_matrix(acs_j, acs_row, tri_mask):
    dm = jnp.broadcast_to(acs_j, (CHUNK, CHUNK)) - jnp.broadcast_to(acs_row, (CHUNK, CHUNK))
    return jnp.where(tri_mask, jnp.exp(jnp.where(tri_mask, dm, 0.0)), 0.0)


def ssd_fwd(act3, proj3, dtb, alog, dsk, *, name, side=None):
    b, s, _ = act3.shape
    nc = s // CHUNK
    act_spec, y_spec, dt_in_spec, _, par_spec, h_spec = _ssd_specs(nc, False)

    def body(act_ref, dtr_ref, dtb_ref, alog_ref, dsk_ref, y_ref, hp_ref, state):
        c = pl.program_id(1)

        @pl.when(c == 0)
        def _():
            state[...] = jnp.zeros_like(state)

        _, dt, _, acs = _ssd_common(dtr_ref, dtb_ref, alog_ref)
        acs_t = acs.T
        tri_mask = _tri(True)
        last_row = (lax.broadcasted_iota(jnp.int32, (CHUNK, 1), 0) == CHUNK - 1).astype(F32)
        for g in range(SSM_GROUPS):
            b16 = _bf(act_ref[0, :, _group_cols(g, 0)])
            c16 = _bf(act_ref[0, :, _group_cols(g, 1)])
            cb = _nt(c16, b16)
            for hs in _head_batches(g):
                x = [act_ref[0, :, _head_cols(h)] for h in hs]
                dt_j = [dt[:, h:h + 1] for h in hs]
                acs_j = [acs[:, h:h + 1] for h in hs]
                acs_last = [_colsum(a * last_row) for a in acs_j]
                xg = _each(lambda xv, d: xv * d, x, dt_j)
                mm = [cb * _decay_matrix(a, acs_t[h:h + 1, :], tri_mask) for a, h in zip(acs_j, hs)]
                decay_s = _each(lambda al, a: jnp.exp(al - a), acs_last, acs_j)
                y_diag = _each(lambda m_, v: _nn(_bf(m_), _bf(v)), mm, xg)
                st = _each(lambda v, d: _tn(_bf(v * d), b16), xg, decay_s)
                hp = [state[h] for h in hs]
                for h, v in zip(hs, hp):
                    hp_ref[0, h, 0] = v
                y_off = [_nt(c16, _bf(v)) for v in hp]
                for h, yd, yo, a, xv in zip(hs, y_diag, y_off, acs_j, x):
                    y_ref[0, :, _head_cols(h)] = yd + yo * jnp.exp(a) + dsk_ref[:, h:h + 1] * xv
                for h, v, al, sv in zip(hs, hp, acs_last, st):
                    state[h] = v * jnp.exp(al) + sv

    call = SideCopy(side, n_in=5, n_out=2, grid=(b, nc))
    return pl.pallas_call(
        call.wrap(body), name=name, grid=(b, nc),
        in_specs=[act_spec, dt_in_spec, par_spec, par_spec, par_spec] + call.in_specs,
        out_specs=[y_spec, h_spec] + call.out_specs,
        out_shape=[jax.ShapeDtypeStruct((b, s, SSM_INNER), F32),
                   jax.ShapeDtypeStruct((b, SSM_HEADS, nc, SSM_P, D_STATE), F32)] + call.out_shape,
        scratch_shapes=[pltpu.VMEM((SSM_HEADS, SSM_P, D_STATE), F32)] + call.scratch,
        compiler_params=_params(("arbitrary", "arbitrary")),
    )(act3, proj3, dtb, alog, dsk, *call.args)


def ssd_bwd(act3, proj3, dtb, alog, dsk, hprev, dy3, *, name, side=None):
    b, s, _ = act3.shape
    nc = s // CHUNK
    act_spec, y_spec, dt_in_spec, dt_out_spec, par_spec, h_spec = _ssd_specs(nc, True)
    dpar_spec = pl.BlockSpec((8, LANE), lambda bi, c: (0, 0))

    def body(act_ref, dtr_ref, dtb_ref, alog_ref, dsk_ref, hp_ref, dy_ref, dact_ref, ddtr_ref, dpar_ref, dstate):
        bi, c = pl.program_id(0), pl.program_id(1)

        @pl.when(c == 0)
        def _():
            dstate[...] = jnp.zeros_like(dstate)

        @pl.when((bi == 0) & (c == 0))
        def _():
            dpar_ref[...] = jnp.zeros_like(dpar_ref)

        z, dt, aneg, acs = _ssd_common(dtr_ref, dtb_ref, alog_ref)
        acs_t = acs.T
        tri_mask = _tri(True)
        last_row = (lax.broadcasted_iota(jnp.int32, (CHUNK, 1), 0) == CHUNK - 1).astype(F32)
        lanes = lax.broadcasted_iota(jnp.int32, (1, LANE), 1)
        sublanes = lax.broadcasted_iota(jnp.int32, (LANE, 1), 0)
        ddt_mat = jnp.zeros((CHUNK, LANE), F32)
        dacs_mat = jnp.zeros((CHUNK, LANE), F32)
        dacs_rows = jnp.zeros((LANE, CHUNK), F32)
        ddsk_row = jnp.zeros((1, LANE), F32)
        for g in range(SSM_GROUPS):
            b16 = _bf(act_ref[0, :, _group_cols(g, 0)])
            c16 = _bf(act_ref[0, :, _group_cols(g, 1)])
            cb = _nt(c16, b16)
            dcb = jnp.zeros((CHUNK, CHUNK), F32)
            db_acc = jnp.zeros((CHUNK, D_STATE), F32)
            dc_acc = jnp.zeros((CHUNK, D_STATE), F32)
            for hs in _head_batches(g):
                x = [act_ref[0, :, _head_cols(h)] for h in hs]
                g_y = [dy_ref[0, :, _head_cols(h)] for h in hs]
                hp = [hp_ref[0, h, 0] for h in hs]
                g_hn = [dstate[h] for h in hs]
                dt_j = [dt[:, h:h + 1] for h in hs]
                acs_j = [acs[:, h:h + 1] for h in hs]
                acs_last = [_colsum(a * last_row) for a in acs_j]
                xg = _each(lambda xv, d: xv * d, x, dt_j)
                lm = [_decay_matrix(a, acs_t[h:h + 1, :], tri_mask) for a, h in zip(acs_j, hs)]
                mm = [cb * l for l in lm]
                decay_s = _each(lambda al, a: jnp.exp(al - a), acs_last, acs_j)
                ea = [jnp.exp(a) for a in acs_j]
                cd = [jnp.exp(al) for al in acs_last]
                g_y16, xg16, hp16, g_hn16 = [[_bf(v) for v in vs] for vs in (g_y, xg, hp, g_hn)]
                d_mm = _each(_nt, g_y16, xg16)
                d_xg = _each(lambda m_, gy: _tn(_bf(m_), gy), mm, g_y16)
                d_dm = _each(lambda a, m_: a * m_, d_mm, mm)
                d_acs = [_rowsum(v) for v in d_dm]
                t_off = [_nt(c16, v) for v in hp16]
                d_t16 = _each(lambda gy, e: _bf(gy * e), g_y, ea)
                d_acs = _each(lambda da, gy, t, e: da + _rowsum(gy * t) * e, d_acs, g_y, t_off, ea)
                d_hp = _each(lambda dtv, gh, cdv: _tn(dtv, c16) + gh * cdv, d_t16, g_hn, cd)
                d_w = [_nt(b16, v) for v in g_hn16]
                d_xg = _each(lambda dx, dw, ds: dx + dw * ds, d_xg, d_w, decay_s)
                d_ds = _each(lambda dw, v, ds: _rowsum(dw * v) * ds, d_w, xg, decay_s)
                d_last = _each(lambda gh, hv, cdv, dd: _colsum(_rowsum(gh * hv)) * cdv + _colsum(dd), g_hn, hp, cd, d_ds)
                d_acs = _each(lambda da, dd, dl: da - dd + dl * last_row, d_acs, d_ds, d_last)
                for h, gy, dx, d, xv in zip(hs, g_y, d_xg, dt_j, x):
                    dact_ref[0, :, _head_cols(h)] = dsk_ref[:, h:h + 1] * gy + dx * d
                for h, v in zip(hs, d_hp):
                    dstate[h] = v
                for k, h in enumerate(hs):
                    onehot = (lanes == h).astype(F32)
                    dcb = dcb + d_mm[k] * lm[k]
                    dc_acc = dc_acc + _nn(d_t16[k], hp16[k])
                    db_acc = db_acc + _nn(_bf(xg[k] * decay_s[k]), g_hn16[k])
                    ddsk_row = ddsk_row + _colsum(_rowsum(g_y[k] * x[k])) * onehot
                    ddt_mat = ddt_mat + _rowsum(d_xg[k] * x[k]) * onehot
                    dacs_mat = dacs_mat + d_acs[k] * onehot
                    dacs_rows = dacs_rows + (sublanes == h).astype(F32) * _colsum(d_dm[k])
            dcb16 = _bf(dcb)
            dact_ref[0, :, _group_cols(g, 1)] = dc_acc + _nn(dcb16, b16)
            dact_ref[0, :, _group_cols(g, 0)] = db_acc + _tn(dcb16, c16)
        d_a = _dot01_left(_tri(False).astype(BF16), dacs_mat - dacs_rows.T)
        ddt_mat = ddt_mat + d_a * aneg
        d_raw = ddt_mat * jax.nn.sigmoid(z)
        ddtr_ref[0] = d_raw
        dpar_ref[0:1, :] += _colsum(d_raw)
        dpar_ref[1:2, :] += _colsum(d_a * dt) * aneg
        dpar_ref[2:3, :] += ddsk_row

    call = SideCopy(side, n_in=7, n_out=3, grid=(b, nc))
    return pl.pallas_call(
        call.wrap(body), name=name, grid=(b, nc),
        in_specs=[act_spec, dt_in_spec, par_spec, par_spec, par_spec, h_spec, y_spec] + call.in_specs,
        out_specs=[act_spec, dt_out_spec, dpar_spec] + call.out_specs,
        out_shape=[jax.ShapeDtypeStruct(act3.shape, F32), jax.ShapeDtypeStruct((b, s, LANE), F32),
                   jax.ShapeDtypeStruct((8, LANE), F32)] + call.out_shape,
        scratch_shapes=[pltpu.VMEM((SSM_HEADS, SSM_P, D_STATE), F32)] + call.scratch,
        compiler_params=_params(("arbitrary", "arbitrary")),
    )(act3, proj3, dtb, alog, dsk, hprev, dy3, *call.args)


def to_heads(x, b, s, h):
    return x.reshape(b, s, h, -1).transpose(0, 2, 1, 3)


def from_heads(x):
    b, h, s, c = x.shape
    return x.transpose(0, 2, 1, 3).reshape(b * s, h * c)


def dilate_q(q, d):
    b, _, s, c = q.shape
    x = q.reshape(b, N_KV_HEADS, GQA, s // d, d, c).transpose(0, 1, 4, 2, 3, 5)
    return x.reshape(b * N_KV_HEADS * d, GQA, s // d, c)


def undilate_q(x, b, d):
    _, _, l, c = x.shape
    y = x.reshape(b, N_KV_HEADS, d, GQA, l, c).transpose(0, 1, 3, 4, 2, 5)
    return y.reshape(b, N_Q_HEADS, l * d, c)


def dilate_kv(k, d):
    b, h, s, c = k.shape
    return k.reshape(b, h, s // d, d, c).transpose(0, 1, 3, 2, 4).reshape(b * h * d, s // d, c)


def undilate_kv(x, b, d):
    _, l, c = x.shape
    return x.reshape(b, N_KV_HEADS, d, l, c).transpose(0, 1, 3, 2, 4).reshape(b, N_KV_HEADS, l * d, c)


def rotary_tables(positions):
    inv_freq = ROPE_THETA ** (-jnp.arange(0, ROPE_DIM, 2, dtype=F32) / ROPE_DIM)
    ang = positions.astype(F32)[..., None] * inv_freq
    cos, sin = jnp.cos(ang), jnp.sin(ang)
    rest = HEAD_DIM - ROPE_DIM
    cosf = jnp.concatenate([cos, cos, jnp.ones(cos.shape[:2] + (rest,), F32)], axis=-1)
    sinf = jnp.concatenate([-sin, sin, jnp.zeros(sin.shape[:2] + (rest,), F32)], axis=-1)
    return cosf, sinf


def w_in_columns(w):
    pad = jnp.zeros((w.shape[0], IN_PAD - IN_PROJ), w.dtype)
    return jnp.concatenate([w[:, :Q_END], w[:, V_END:XBC_END], w[:, Q_END:V_END], w[:, XBC_END:], pad], axis=1)


def w_in_grad_columns(g):
    return jnp.concatenate([g[:, :Z_COL], g[:, K_COL:DT_COL], g[:, Z_COL:K_COL], g[:, DT_COL:DT_COL + SSM_HEADS]], axis=1)


def lane_pad(v):
    return jnp.pad(v.reshape(1, -1), ((0, 0), (0, LANE - v.shape[-1])))


def layer_fwd(h, wts, small, rope_tab, b, s, tag, attn_side=None, rest_from=None, ssd_side=None):
    w_in = wts[0]
    t = b * s
    sv = {"h": h}
    hn = rowwise_fwd(rms_fn, [h], [small["norm_mix"]], [BF16], name=f"rms_mix_{tag}")[0]
    proj = matmul(hn, w_in, name=f"in_proj_{tag}")
    sv["hn"], sv["proj"] = hn, proj
    proj3 = proj.reshape(b, s, IN_PAD)
    attn3, lse3, *attn_out = attn_fwd(proj3, rope_tab, name=f"attn_{tag}", side=attn_side)
    if rest_from is not None:
        wts = (w_in,) + tuple(rest_from(attn_out))
    _, w_out, w_gate, w_up, w_down = wts
    sv["attn3"], sv["lse3"] = attn3, lse3
    attn = attn3.reshape(t, ATTN_WIDTH)
    act3 = conv_silu_fwd(proj3, small["conv_w"], small["conv_b"], name=f"conv_{tag}")
    y3, hprev, *ssd_out = ssd_fwd(act3, proj3, small["dt_bias"], small["a_log"], small["d_skip"], name=f"ssd_{tag}",
                                  side=ssd_side)
    y = y3.reshape(t, SSM_INNER)
    sv["act3"], sv["hprev"], sv["y"] = act3, hprev, y
    gn = rowwise_fwd(gated_norm_fn, [y, proj], [small["ssm_norm"]], [BF16], name=f"gated_norm_{tag}", groups=SSM_GROUPS,
                     windows=[None, (Z_COL, SSM_INNER)])[0]
    sv["gn"] = gn
    h1 = matmul([attn, gn], w_out, name=f"out_proj_{tag}", residual=h)
    sv["h1"] = h1
    hn2 = rowwise_fwd(rms_fn, [h1], [small["norm_ffn"]], [BF16], name=f"rms_ffn_{tag}")[0]
    gate = matmul(hn2, w_gate, out_dtype=BF16, name=f"ffn_gate_{tag}")
    up = matmul(hn2, w_up, out_dtype=BF16, name=f"ffn_up_{tag}")
    act2 = rowwise_fwd(swiglu_fn, [gate, up], [], [BF16], name=f"swiglu_{tag}")[0]
    sv["hn2"], sv["gate"], sv["up"], sv["act2"] = hn2, gate, up, act2
    h2 = matmul(act2, w_down, name=f"ffn_down_{tag}", residual=h1)
    return h2, sv, wts, (ssd_out or None)


def layer_bwd(dh2, sv, wts, small, rope_tab, b, s, tag, ssd_side=None, attn_side_fn=None):
    w_in, w_out, w_gate, w_up, w_down = wts
    t = b * s
    gr = {}
    d_act2 = matmul(dh2, w_down, tb=True, out_dtype=BF16, name=f"ffn_down_dx_{tag}")
    gr["w_down"] = matmul(sv["act2"], dh2, ta=True, out_dtype=BF16, name=f"ffn_down_dw_{tag}")
    d_gate, d_up = rowwise_bwd(swiglu_fn, [sv["gate"], sv["up"]], [], [d_act2], [BF16, BF16], name=f"swiglu_bwd_{tag}")
    gr["w_gate"] = matmul(sv["hn2"], d_gate, ta=True, out_dtype=BF16, name=f"ffn_gate_dw_{tag}")
    gr["w_up"] = matmul(sv["hn2"], d_up, ta=True, out_dtype=BF16, name=f"ffn_up_dw_{tag}")
    d_hn2 = matmul(d_gate, w_gate, tb=True, name=f"ffn_gate_dx_{tag}")
    d_hn2 = matmul(d_up, w_up, tb=True, residual=d_hn2, name=f"ffn_up_dx_{tag}")
    dh1, gr["norm_ffn"] = rowwise_bwd(rms_fn, [sv["h1"]], [small["norm_ffn"]], [d_hn2], [F32],
                                      name=f"rms_ffn_bwd_{tag}", add_to_first=dh2)
    d_cat = matmul(dh1, w_out, tb=True, name=f"out_proj_dx_{tag}")
    gr["w_out"] = jnp.concatenate([
        matmul(sv["attn3"].reshape(t, ATTN_WIDTH), dh1, ta=True, out_dtype=BF16, name=f"out_proj_dw_attn_{tag}"),
        matmul(sv["gn"], dh1, ta=True, out_dtype=BF16, name=f"out_proj_dw_ssd_{tag}")], axis=0)
    d_y, d_z, gr["ssm_norm"] = rowwise_bwd(gated_norm_fn, [sv["y"], sv["proj"]], [small["ssm_norm"]], [d_cat], [F32, BF16],
                                           name=f"gated_norm_bwd_{tag}", groups=SSM_GROUPS,
                                           windows=[None, (Z_COL, SSM_INNER)], ct_windows=[(ATTN_WIDTH, SSM_INNER)])
    proj3 = sv["proj"].reshape(b, s, IN_PAD)
    d_act3, d_dtr, d_par, *ssd_out = ssd_bwd(sv["act3"], proj3, small["dt_bias"], small["a_log"], small["d_skip"],
                                             sv["hprev"], d_y.reshape(b, s, SSM_INNER), name=f"ssd_bwd_{tag}", side=ssd_side)
    gr["dt_bias"], gr["a_log"], gr["d_skip"] = d_par[0, :SSM_HEADS], d_par[1, :SSM_HEADS], d_par[2, :SSM_HEADS]
    d_xbc, gr["conv_w"], gr["conv_b"] = conv_silu_bwd(proj3, small["conv_w"], small["conv_b"], d_act3,
                                                      name=f"conv_bwd_{tag}")
    attn_side = attn_side_fn(gr) if attn_side_fn is not None else None
    d_q3, d_k4, d_v4, *attn_out = attn_bwd(proj3, rope_tab, sv["attn3"], sv["lse3"], d_cat.reshape(b, s, MIX_WIDTH),
                                           name=f"attn_bwd_{tag}", side=attn_side)
    d_tail = jnp.concatenate([from_heads(d_k4), from_heads(d_v4), d_dtr.reshape(t, LANE)], axis=1).astype(BF16)
    d_proj = [d_q3.reshape(t, ATTN_WIDTH), d_z, d_xbc.reshape(t, CONV_CH), d_tail]
    d_hn = matmul(d_proj, w_in, tb=True, name=f"in_proj_dx_{tag}")
    gr["w_in"] = w_in_grad_columns(jnp.concatenate(
        [matmul(sv["hn"], part, ta=True, out_dtype=BF16, name=f"in_proj_dw_{k}_{tag}") for k, part in enumerate(d_proj)],
        axis=1))
    dh, gr["norm_mix"] = rowwise_bwd(rms_fn, [sv["h"]], [small["norm_mix"]], [d_hn], [F32],
                                     name=f"rms_mix_bwd_{tag}", add_to_first=dh1)
    return dh, gr, (ssd_out or None), (attn_out or None)


def local_step(x, positions, big, small_all, final_norm, loss_target, *, plan=None):
    b, s, _ = x.shape
    t = b * s
    rope_tab = jnp.concatenate(rotary_tables(positions), axis=-1)
    h = x.reshape(t, D_MODEL)
    saved, big = [], list(big)
    for l in range(DEPTH):
        kw = {}
        if plan is not None and l == 0:
            kw = dict(attn_side=(plan["rest0"], False), rest_from=plan["make_rest0"], ssd_side=(plan["late"], False))
        h, sv, big[l], got = layer_fwd(h, big[l], small_all[l], rope_tab, b, s, f"l{l}", **kw)
        if got is not None:
            big[DEPTH - 1] = plan["make_late"](got)
        saved.append(sv)
    dh, d_final, loss = loss_and_grad(h, loss_target.reshape(t, D_MODEL), final_norm.reshape(1, D_MODEL))
    grads, received = [None] * DEPTH, {}
    for l in reversed(range(DEPTH)):
        kw = {}
        if plan is not None and l == 0:
            kw = dict(ssd_side=(plan["grads_late"](grads[DEPTH - 1]), True),
                      attn_side_fn=lambda gr: (plan["grads_rest0"](gr), True))
        dh, grads[l], got_ssd, got_attn = layer_bwd(dh, saved[l], big[l], small_all[l], rope_tab, b, s, f"l{l}", **kw)
        if got_ssd is not None:
            received["late"] = got_ssd
        if got_attn is not None:
            received["rest0"] = got_attn
    return loss, dh.reshape(b, s, D_MODEL), grads, d_final, received


def _slab_rows(r):
    return r if r <= 512 else _pick(r, (512, 352, 256, 128, 8))


def cast_bf16(x, *, name):
    def fn(v):
        return (v,)
    return rowwise_fwd(fn, [x], [], [BF16], name=name, tr=_slab_rows(x.shape[0]))[0]


def sum_slots(x, *, name):
    n, r, c = x.shape
    tr = _slab_rows(r)

    def body(x_ref, o_ref):
        acc = x_ref[0].astype(F32)
        for i in range(1, n):
            acc = acc + x_ref[i].astype(F32)
        o_ref[...] = acc

    return pl.pallas_call(
        body, name=name, grid=(r // tr,), in_specs=[pl.BlockSpec((n, tr, c), lambda i: (0, i, 0))],
        out_specs=pl.BlockSpec((tr, c), lambda i: (i, 0)), out_shape=jax.ShapeDtypeStruct((r, c), F32),
        compiler_params=_params(("parallel",)),
    )(x)


def adamw(g_parts, w, m, v, *, name, with_grad=True):
    r, c = w.shape
    tr = _slab_rows(r)
    n_g = len(g_parts)
    n_out = 4 if with_grad else 3
    bc1 = 1.0 / (1.0 - ADAM_B1 ** ADAM_STEP)
    bc2 = 1.0 / (1.0 - ADAM_B2 ** ADAM_STEP)

    def body(*refs):
        g = refs[0][...]
        for r_ in refs[1:n_g]:
            g = g + r_[...]
        w_ref, m_ref, v_ref = refs[n_g:n_g + 3]
        d_out, m_out, v_out = refs[-3:]
        m_new = ADAM_B1 * m_ref[...] + (1.0 - ADAM_B1) * g
        v_new = ADAM_B2 * v_ref[...] + (1.0 - ADAM_B2) * (g * g)
        if with_grad:
            refs[n_g + 3][...] = g
        m_out[...] = m_new
        v_out[...] = v_new
        d_out[...] = -ADAM_LR * ((m_new * bc1) / (jnp.sqrt(v_new * bc2) + ADAM_EPS) + ADAM_WD * w_ref[...])

    spec = pl.BlockSpec((tr, c), lambda i: (i, 0))
    return pl.pallas_call(
        body, name=name, grid=(r // tr,), in_specs=[spec] * (n_g + 3), out_specs=[spec] * n_out,
        out_shape=[jax.ShapeDtypeStruct((r, c), F32)] * n_out, compiler_params=_params(("parallel",)),
    )(*g_parts, w, m, v)


def _other_chips(x, y):
    return [(1 - x, y), (x, 1 - y), (1 - x, 1 - y)]


def allgather_chips(shards):
    n_arr = len(shards)

    def body(*refs):
        in_refs, out_refs = refs[:n_arr], refs[n_arr:2 * n_arr]
        send_sems, recv_sems, local_sems = refs[2 * n_arr:]
        x, y, c = lax.axis_index("x"), lax.axis_index("y"), lax.axis_index("c")
        chip = 2 * x + y
        started = []
        for a, (in_ref, out_ref) in enumerate(zip(in_refs, out_refs)):
            mine = pltpu.make_async_copy(in_ref, out_ref.at[chip], local_sems.at[a])
            mine.start()
            started.append(mine.wait)
            for k, (px, py) in enumerate(_other_chips(x, y)):
                cp = pltpu.make_async_remote_copy(src_ref=in_ref, dst_ref=out_ref.at[chip], send_sem=send_sems.at[3 * a + k],
                                                  recv_sem=recv_sems.at[3 * a + k], device_id=(px, py, c), device_id_type=MESH)
                cp.start()
                started.append(cp.wait_send)
        for a, (in_ref, out_ref) in enumerate(zip(in_refs, out_refs)):
            for k, (px, py) in enumerate(_other_chips(x, y)):
                pltpu.make_async_remote_copy(src_ref=in_ref, dst_ref=out_ref.at[2 * px + py], send_sem=send_sems.at[3 * a + k],
                                             recv_sem=recv_sems.at[3 * a + k], device_id=(px, py, c),
                                             device_id_type=MESH).wait_recv()
        for wait in started:
            wait()

    hbm = pl.BlockSpec(memory_space=pltpu.HBM)
    return pl.pallas_call(
        body, name="allgather_weights", in_specs=[hbm] * n_arr, out_specs=[hbm] * n_arr,
        out_shape=[jax.ShapeDtypeStruct((N_CHIPS,) + s.shape, s.dtype) for s in shards],
        scratch_shapes=[pltpu.SemaphoreType.DMA((3 * n_arr,)), pltpu.SemaphoreType.DMA((3 * n_arr,)),
                        pltpu.SemaphoreType.DMA((n_arr,))],
    )(*shards)


def exchange_grads(big, small):
    def body(big_ref, small_ref, big_out, small_out, send_sems, recv_sems, local_sems):
        x, y, c = lax.axis_index("x"), lax.axis_index("y"), lax.axis_index("c")
        chip = 2 * x + y
        dev = 4 * x + 2 * y + c
        own_big = pltpu.make_async_copy(big_ref.at[chip], big_out.at[chip], local_sems.at[0])
        own_small = pltpu.make_async_copy(small_ref, small_out.at[dev], local_sems.at[1])
        own_big.start()
        own_small.start()
        sends = []
        for k, (px, py) in enumerate(_other_chips(x, y)):
            cp = pltpu.make_async_remote_copy(src_ref=big_ref.at[2 * px + py], dst_ref=big_out.at[chip],
                                              send_sem=send_sems.at[k], recv_sem=recv_sems.at[k],
                                              device_id=(px, py, c), device_id_type=MESH)
            cp.start()
            sends.append(cp)
        peers = []
        for r in range(1, N_DEV):
            fx, fy, fc = (r >> 2) & 1, (r >> 1) & 1, r & 1
            px, py, pc = (x + fx) % 2, (y + fy) % 2, (c + fc) % 2
            peers.append((px, py, pc))
            cp = pltpu.make_async_remote_copy(src_ref=small_ref, dst_ref=small_out.at[dev], send_sem=send_sems.at[2 + r],
                                              recv_sem=recv_sems.at[2 + r], device_id=(px, py, pc), device_id_type=MESH)
            cp.start()
            sends.append(cp)
        for k, (px, py) in enumerate(_other_chips(x, y)):
            pltpu.make_async_remote_copy(src_ref=big_ref.at[chip], dst_ref=big_out.at[2 * px + py],
                                         send_sem=send_sems.at[k], recv_sem=recv_sems.at[k],
                                         device_id=(px, py, c), device_id_type=MESH).wait_recv()
        for r, (px, py, pc) in zip(range(1, N_DEV), peers):
            pltpu.make_async_remote_copy(src_ref=small_ref, dst_ref=small_out.at[4 * px + 2 * py + pc],
                                         send_sem=send_sems.at[2 + r], recv_sem=recv_sems.at[2 + r],
                                         device_id=(px, py, pc), device_id_type=MESH).wait_recv()
        for cp in sends:
            cp.wait_send()
        own_big.wait()
        own_small.wait()

    hbm = pl.BlockSpec(memory_space=pltpu.HBM)
    n_sem = 3 + N_DEV - 1
    return pl.pallas_call(
        body, name="exchange_grads", in_specs=[hbm, hbm], out_specs=[hbm, hbm],
        out_shape=[jax.ShapeDtypeStruct(big.shape, big.dtype), jax.ShapeDtypeStruct((N_DEV,) + small.shape, small.dtype)],
        scratch_shapes=[pltpu.SemaphoreType.DMA((n_sem,)), pltpu.SemaphoreType.DMA((n_sem,)), pltpu.SemaphoreType.DMA((2,))],
    )(big, small)


SWAP_CHUNKS = 28


def swap_cores(mine):
    rows = mine.shape[0] // SWAP_CHUNKS
    assert rows * SWAP_CHUNKS == mine.shape[0] and rows % 8 == 0

    def body(in_ref, out_ref, send_sems, recv_sems):
        x, y, c = lax.axis_index("x"), lax.axis_index("y"), lax.axis_index("c")

        def chunk(k):
            part = pl.ds(k * rows, rows)
            return pltpu.make_async_remote_copy(src_ref=in_ref.at[part], dst_ref=out_ref.at[part],
                                                send_sem=send_sems.at[k], recv_sem=recv_sems.at[k],
                                                device_id=(x, y, 1 - c), device_id_type=MESH)

        for k in range(SWAP_CHUNKS):
            chunk(k).start()
        for k in range(SWAP_CHUNKS):
            chunk(k).wait_recv()
        for k in range(SWAP_CHUNKS):
            chunk(k).wait_send()

    hbm = pl.BlockSpec(memory_space=pltpu.HBM)
    return pl.pallas_call(
        body, name="swap_cores", in_specs=[hbm], out_specs=hbm,
        out_shape=jax.ShapeDtypeStruct(mine.shape, mine.dtype),
        scratch_shapes=[pltpu.SemaphoreType.DMA((SWAP_CHUNKS,)), pltpu.SemaphoreType.DMA((SWAP_CHUNKS,))],
    )(mine)


BIG_NAMES = ("w_in", "w_out", "w_gate", "w_up", "w_down")
BIG_SHARD_AXIS = {"w_in": 1, "w_out": 0, "w_gate": 1, "w_up": 1, "w_down": 0}
PACK_COLS = 1024
SMALL_NAMES = ("norm_mix", "conv_w", "conv_b", "dt_bias", "a_log", "d_skip", "ssm_norm", "norm_ffn")


PACK_ROW_TILE = 256


def pack_big(shards, names=BIG_NAMES):
    flat = jnp.concatenate([shards[n].reshape(-1) for n in names])
    unit = PACK_ROW_TILE * PACK_COLS
    total = -(-flat.size // unit) * unit
    return jnp.pad(flat, (0, total - flat.size)).reshape(-1, PACK_COLS)


def unpack_big(packed, like, names=BIG_NAMES):
    out, off = {}, 0
    flat = packed.reshape(-1)
    for n in names:
        size = like[n].size
        out[n] = flat[off:off + size].reshape(like[n].shape)
        off += size
    return out


def pack_small(parts):
    flat = jnp.concatenate([p.reshape(-1).astype(F32) for p in parts])
    rows = -(-flat.size // LANE)
    rows = -(-rows // 8) * 8
    return jnp.pad(flat, (0, rows * LANE - flat.size)).reshape(rows, LANE)


def unpack_small(packed, like):
    out, off = [], 0
    flat = packed.reshape(-1)
    for a in like:
        out.append(flat[off:off + a.size].reshape(a.shape))
        off += a.size
    return out


def _unused_kernel_packed(x, positions, norm_mix, w_in, conv_w, conv_b, dt_bias, a_log, d_skip, ssm_norm, w_out, norm_ffn, w_gate, w_up, w_down, final_norm, loss_target, m_norm_mix, m_w_in, m_conv_w, m_conv_b, m_dt_bias, m_a_log, m_d_skip, m_ssm_norm, m_w_out, m_norm_ffn, m_w_gate, m_w_up, m_w_down, m_final_norm, v_norm_mix, v_w_in, v_conv_w, v_conv_b, v_dt_bias, v_a_log, v_d_skip, v_ssm_norm, v_w_out, v_norm_ffn, v_w_gate, v_w_up, v_w_down, v_final_norm):
    chip = 2 * lax.axis_index("x") + lax.axis_index("y")
    w_sh = {"w_in": w_in, "w_out": w_out, "w_gate": w_gate, "w_up": w_up, "w_down": w_down}
    m_sh = {"w_in": m_w_in, "w_out": m_w_out, "w_gate": m_w_gate, "w_up": m_w_up, "w_down": m_w_down}
    v_sh = {"w_in": v_w_in, "w_out": v_w_out, "w_gate": v_w_gate, "w_up": v_w_up, "w_down": v_w_down}

    assert DEPTH == 2
    first, rest = BIG_NAMES[:1], BIG_NAMES[1:]
    layer_of = lambda d, l: {n: d[n][l] for n in BIG_NAMES}
    pack_layer = lambda d: jnp.concatenate([pack_big(d, first), pack_big(d, rest)])
    pack_layers = lambda d: jnp.concatenate([pack_layer(layer_of(d, l)) for l in range(DEPTH)])
    first_rows = pack_big(layer_of(w_sh, 0), first).shape[0]
    layer_rows = pack_layer(layer_of(w_sh, 0)).shape[0]

    def unpack_layer(packed, l):
        like = layer_of(w_sh, l)
        return {**unpack_big(packed[:first_rows], like, first), **unpack_big(packed[first_rows:], like, rest)}

    def unpack_layers(packed):
        per_layer = [unpack_layer(packed[l * layer_rows:(l + 1) * layer_rows], l) for l in range(DEPTH)]
        return {n: jnp.stack([p[n] for p in per_layer]) for n in BIG_NAMES}

    def full_weights(gathered, l, names, unpack):
        pieces = [unpack(gathered[j]) for j in range(N_CHIPS)]
        full = {n: jnp.concatenate([p[n] for p in pieces], axis=BIG_SHARD_AXIS[n]) for n in names}
        return tuple(w_in_columns(full[n]) if n == "w_in" else full[n] for n in names)

    w_packed16 = pack_layers({n: cast_bf16(w_sh[n].reshape(-1, w_sh[n].shape[-1]), name=f"cast_{n}").reshape(w_sh[n].shape)
                              for n in BIG_NAMES})
    conv_cols = CONV_CH // N_CHIPS
    gathered_in0, conv_g = allgather_chips([w_packed16[:first_rows], conv_w.reshape(-1, LANE)])
    big = [full_weights(gathered_in0, 0, first, lambda p: unpack_big(p, layer_of(w_sh, 0), first)) + (None,) * len(rest), None]
    plan = {
        "rest0": w_packed16[first_rows:layer_rows],
        "make_rest0": lambda g: full_weights(g, 0, rest, lambda p: unpack_big(p, layer_of(w_sh, 0), rest)),
        "late": w_packed16[layer_rows:],
        "make_late": lambda g: full_weights(g, DEPTH - 1, BIG_NAMES, lambda p: unpack_layer(p, DEPTH - 1)),
    }
    conv_w_full = jnp.concatenate([conv_g[j].reshape(DEPTH, CONV_WIDTH, conv_cols) for j in range(N_CHIPS)], axis=2)

    small_all = []
    for l in range(DEPTH):
        small_all.append({
            "norm_mix": norm_mix[l].reshape(1, -1), "conv_w": conv_w_full[l], "conv_b": conv_b[l].reshape(1, -1),
            "dt_bias": lane_pad(dt_bias[l]), "a_log": lane_pad(a_log[l]), "d_skip": lane_pad(d_skip[l]),
            "ssm_norm": ssm_norm[l].reshape(1, -1), "norm_ffn": norm_ffn[l].reshape(1, -1)})

    def shard_of(name, g, j):
        n = g.shape[BIG_SHARD_AXIS[name]] // N_CHIPS
        return lax.slice_in_dim(g, j * n, (j + 1) * n, axis=BIG_SHARD_AXIS[name])

    def per_chip(layer_grads, names):
        packs = [[pack_big({n: shard_of(n, layer_grads[n], j) for n in group}, group) for j in range(N_CHIPS)]
                 for group in ((first, rest) if names == BIG_NAMES else (names,))]
        return jnp.stack([jnp.concatenate([p[j] for p in packs]) for j in range(N_CHIPS)])

    plan["grads_late"] = lambda gr: per_chip(gr, BIG_NAMES)
    plan["grads_rest0"] = lambda gr: per_chip(gr, rest)
    loss_part, grad_x, grads, d_final, received = local_step(x, positions, big, small_all, final_norm, loss_target, plan=plan)

    small_parts = [jnp.stack([grads[l][n].reshape(-1) for l in range(DEPTH)]) for n in SMALL_NAMES]
    small_parts += [d_final.reshape(-1), loss_part.reshape(-1)]
    recv_first, recv_small = exchange_grads(per_chip(grads[0], first), pack_small(small_parts))
    plane_sum = jnp.concatenate([sum_slots(recv_first, name="sum_chip_partials_in0"),
                                 sum_slots(received["rest0"], name="sum_chip_partials_rest0"),
                                 sum_slots(received["late"], name="sum_chip_partials_l1")])
    other_plane = swap_cores(plane_sum)

    g_packed = rowwise_fwd(lambda p, q: (p + q,), [plane_sum, other_plane], [], [F32], name="sum_planes")[0]
    g_big = unpack_layers(g_packed)
    d_big, m_big, v_big = {}, {}, {}
    for n in BIG_NAMES:
        flat = lambda a: a.reshape(-1, a.shape[-1])
        res = adamw([flat(g_big[n])], flat(w_sh[n]), flat(m_sh[n]), flat(v_sh[n]), name=f"adamw_{n}", with_grad=False)
        d_big[n], m_big[n], v_big[n] = (a.reshape(w_sh[n].shape) for a in res)

    small_sum = sum_slots(recv_small, name="sum_small")
    like = [norm_mix, conv_w_full, conv_b, dt_bias, a_log, d_skip, ssm_norm, norm_ffn, final_norm, loss_part.reshape(-1)]
    g_small = unpack_small(small_sum, like)
    loss = g_small[-1][0]
    g_small = dict(zip(SMALL_NAMES + ("final_norm",), g_small[:-1]))
    g_small["conv_w"] = lax.dynamic_slice_in_dim(g_small["conv_w"], chip * conv_cols, conv_cols, axis=2)
    w_small = {"norm_mix": norm_mix, "conv_w": conv_w, "conv_b": conv_b, "dt_bias": dt_bias, "a_log": a_log, "d_skip": d_skip,
               "ssm_norm": ssm_norm, "norm_ffn": norm_ffn, "final_norm": final_norm}
    m_small = {"norm_mix": m_norm_mix, "conv_w": m_conv_w, "conv_b": m_conv_b, "dt_bias": m_dt_bias, "a_log": m_a_log,
               "d_skip": m_d_skip, "ssm_norm": m_ssm_norm, "norm_ffn": m_norm_ffn, "final_norm": m_final_norm}
    v_small = {"norm_mix": v_norm_mix, "conv_w": v_conv_w, "conv_b": v_conv_b, "dt_bias": v_dt_bias, "a_log": v_a_log,
               "d_skip": v_d_skip, "ssm_norm": v_ssm_norm, "norm_ffn": v_norm_ffn, "final_norm": v_final_norm}
    names = SMALL_NAMES + ("final_norm",)
    order = [w_small[n] for n in names]
    res = adamw([pack_small([g_small[n] for n in names])], pack_small(order), pack_small([m_small[n] for n in names]),
                pack_small([v_small[n] for n in names]), name="adamw_small")
    g_s, d_s, m_s, v_s = (dict(zip(names, unpack_small(a, order))) for a in res)

    all_names = ("norm_mix", "w_in", "conv_w", "conv_b", "dt_bias", "a_log", "d_skip", "ssm_norm", "w_out", "norm_ffn",
                 "w_gate", "w_up", "w_down", "final_norm")
    outs = [loss, grad_x]
    for src_big, src_small in ((g_big, g_s), (d_big, d_s), (m_big, m_s), (v_big, v_s)):
        outs += [src_big[n] if n in BIG_NAMES else src_small[n] for n in all_names]
    return tuple(outs)


SWAP_PIECES = 4


def swap_cores_list(arrays):
    n = len(arrays)

    def body(*refs):
        ins, outs, (send_sems, recv_sems) = refs[:n], refs[n:2 * n], refs[2 * n:]
        x, y, c = lax.axis_index("x"), lax.axis_index("y"), lax.axis_index("c")
        copies = []
        for k in range(n):
            rows = ins[k].shape[0] // SWAP_PIECES
            for p in range(SWAP_PIECES):
                part = pl.ds(p * rows, rows)
                copies.append(pltpu.make_async_remote_copy(
                    src_ref=ins[k].at[part], dst_ref=outs[k].at[part], send_sem=send_sems.at[k * SWAP_PIECES + p],
                    recv_sem=recv_sems.at[k * SWAP_PIECES + p], device_id=(x, y, 1 - c), device_id_type=MESH))
        for cp in copies:
            cp.start()
        for cp in copies:
            cp.wait_recv()
        for cp in copies:
            cp.wait_send()

    assert all(a.shape[0] % (8 * SWAP_PIECES) == 0 for a in arrays)
    hbm = pl.BlockSpec(memory_space=pltpu.HBM)
    return pl.pallas_call(
        body, name="swap_cores", in_specs=[hbm] * n, out_specs=[hbm] * n,
        out_shape=[jax.ShapeDtypeStruct(a.shape, a.dtype) for a in arrays],
        scratch_shapes=[pltpu.SemaphoreType.DMA((n * SWAP_PIECES,)), pltpu.SemaphoreType.DMA((n * SWAP_PIECES,))],
    )(*arrays)


def adamw_layers(g_parts, w, m, v, *, name):
    depth, a, b = w.shape
    tr = _pick(a, (256, 352, 192, 128, 8))
    counts = [len(p) for p in g_parts]
    flat_parts = [q for p in g_parts for q in p]
    bc1 = 1.0 / (1.0 - ADAM_B1 ** ADAM_STEP)
    bc2 = 1.0 / (1.0 - ADAM_B2 ** ADAM_STEP)

    def body(*refs):
        layer = pl.program_id(0)
        g, off = None, 0
        for l, cnt in enumerate(counts):
            g_l = refs[off][...]
            for r_ in refs[off + 1:off + cnt]:
                g_l = g_l + r_[...]
            off += cnt
            g = g_l if g is None else jnp.where(layer == l, g_l, g)
        w_ref, m_ref, v_ref, g_out, d_out, m_out, v_out = refs[off:]
        m_new = ADAM_B1 * m_ref[0] + (1.0 - ADAM_B1) * g
        v_new = ADAM_B2 * v_ref[0] + (1.0 - ADAM_B2) * (g * g)
        g_out[0] = g
        m_out[0] = m_new
        v_out[0] = v_new
        d_out[0] = -ADAM_LR * ((m_new * bc1) / (jnp.sqrt(v_new * bc2) + ADAM_EPS) + ADAM_WD * w_ref[0])

    g_spec = pl.BlockSpec((tr, b), lambda l, i: (i, 0))
    spec = pl.BlockSpec((1, tr, b), lambda l, i: (l, i, 0))
    return pl.pallas_call(
        body, name=name, grid=(depth, a // tr), in_specs=[g_spec] * len(flat_parts) + [spec] * 3, out_specs=[spec] * 4,
        out_shape=[jax.ShapeDtypeStruct(w.shape, F32)] * 4, compiler_params=_params(("parallel", "parallel")),
    )(*flat_parts, w, m, v)


def kernel(x, positions, norm_mix, w_in, conv_w, conv_b, dt_bias, a_log, d_skip, ssm_norm, w_out, norm_ffn, w_gate, w_up, w_down, final_norm, loss_target, m_norm_mix, m_w_in, m_conv_w, m_conv_b, m_dt_bias, m_a_log, m_d_skip, m_ssm_norm, m_w_out, m_norm_ffn, m_w_gate, m_w_up, m_w_down, m_final_norm, v_norm_mix, v_w_in, v_conv_w, v_conv_b, v_dt_bias, v_a_log, v_d_skip, v_ssm_norm, v_w_out, v_norm_ffn, v_w_gate, v_w_up, v_w_down, v_final_norm):
    chip = 2 * lax.axis_index("x") + lax.axis_index("y")
    w_sh = {"w_in": w_in, "w_out": w_out, "w_gate": w_gate, "w_up": w_up, "w_down": w_down}
    m_sh = {"w_in": m_w_in, "w_out": m_w_out, "w_gate": m_w_gate, "w_up": m_w_up, "w_down": m_w_down}
    v_sh = {"w_in": v_w_in, "w_out": v_w_out, "w_gate": v_w_gate, "w_up": v_w_up, "w_down": v_w_down}
    assert DEPTH == 2
    first, rest = BIG_NAMES[:1], BIG_NAMES[1:]

    w16 = {n: cast_bf16(w_sh[n].reshape(-1, w_sh[n].shape[-1]), name=f"cast_{n}").reshape(w_sh[n].shape) for n in BIG_NAMES}

    def joined(n, gathered):
        if BIG_SHARD_AXIS[n] == 0:
            full = gathered.reshape(-1, gathered.shape[-1])
        else:
            full = jnp.concatenate([gathered[j] for j in range(N_CHIPS)], axis=1)
        return w_in_columns(full) if n == "w_in" else full

    def per_chip(n, g):
        if BIG_SHARD_AXIS[n] == 0:
            return g.reshape(N_CHIPS, -1, g.shape[-1])
        return jnp.stack(jnp.split(g, N_CHIPS, axis=1))

    conv_cols = CONV_CH // N_CHIPS
    gathered_in0, conv_g = allgather_chips([w16["w_in"][0], conv_w.reshape(-1, LANE)])
    big = [(joined("w_in", gathered_in0),) + (None,) * len(rest), None]
    plan = {
        "rest0": [w16[n][0] for n in rest],
        "make_rest0": lambda gs: tuple(joined(n, g) for n, g in zip(rest, gs)),
        "late": [w16[n][DEPTH - 1] for n in BIG_NAMES],
        "make_late": lambda gs: tuple(joined(n, g) for n, g in zip(BIG_NAMES, gs)),
        "grads_late": lambda gr: [per_chip(n, gr[n]) for n in BIG_NAMES],
        "grads_rest0": lambda gr: [per_chip(n, gr[n]) for n in rest],
    }
    conv_w_full = jnp.concatenate([conv_g[j].reshape(DEPTH, CONV_WIDTH, conv_cols) for j in range(N_CHIPS)], axis=2)
    small_all = []
    for l in range(DEPTH):
        small_all.append({
            "norm_mix": norm_mix[l].reshape(1, -1), "conv_w": conv_w_full[l], "conv_b": conv_b[l].reshape(1, -1),
            "dt_bias": lane_pad(dt_bias[l]), "a_log": lane_pad(a_log[l]), "d_skip": lane_pad(d_skip[l]),
            "ssm_norm": ssm_norm[l].reshape(1, -1), "norm_ffn": norm_ffn[l].reshape(1, -1)})

    loss_part, grad_x, grads, d_final, received = local_step(x, positions, big, small_all, final_norm, loss_target, plan=plan)

    small_parts = [jnp.stack([grads[l][n].reshape(-1) for l in range(DEPTH)]) for n in SMALL_NAMES]
    small_parts += [d_final.reshape(-1), loss_part.reshape(-1)]
    recv_in0, recv_small = exchange_grads(per_chip("w_in", grads[0]["w_in"]), pack_small(small_parts))
    recv = [dict(zip(BIG_NAMES, [recv_in0] + list(received["rest0"]))), dict(zip(BIG_NAMES, received["late"]))]
    keys = [(l, n) for l in range(DEPTH) for n in BIG_NAMES]
    mine = {(l, n): sum_slots(recv[l][n], name=f"sum_partials_{n}_l{l}") for l, n in keys}
    other = dict(zip(keys, swap_cores_list([mine[k] for k in keys])))

    g_big, d_big, m_big, v_big = {}, {}, {}, {}
    for n in BIG_NAMES:
        g_big[n], d_big[n], m_big[n], v_big[n] = adamw_layers([[mine[(l, n)], other[(l, n)]] for l in range(DEPTH)],
                                                              w_sh[n], m_sh[n], v_sh[n], name=f"adamw_{n}")

    small_sum = sum_slots(recv_small, name="sum_small")
    like = [norm_mix, conv_w_full, conv_b, dt_bias, a_log, d_skip, ssm_norm, norm_ffn, final_norm, loss_part.reshape(-1)]
    g_small = unpack_small(small_sum, like)
    loss = g_small[-1][0]
    g_small = dict(zip(SMALL_NAMES + ("final_norm",), g_small[:-1]))
    g_small["conv_w"] = lax.dynamic_slice_in_dim(g_small["conv_w"], chip * conv_cols, conv_cols, axis=2)
    w_small = {"norm_mix": norm_mix, "conv_w": conv_w, "conv_b": conv_b, "dt_bias": dt_bias, "a_log": a_log, "d_skip": d_skip,
               "ssm_norm": ssm_norm, "norm_ffn": norm_ffn, "final_norm": final_norm}
    m_small = {"norm_mix": m_norm_mix, "conv_w": m_conv_w, "conv_b": m_conv_b, "dt_bias": m_dt_bias, "a_log": m_a_log,
               "d_skip": m_d_skip, "ssm_norm": m_ssm_norm, "norm_ffn": m_norm_ffn, "final_norm": m_final_norm}
    v_small = {"norm_mix": v_norm_mix, "conv_w": v_conv_w, "conv_b": v_conv_b, "dt_bias": v_dt_bias, "a_log": v_a_log,
               "d_skip": v_d_skip, "ssm_norm": v_ssm_norm, "norm_ffn": v_norm_ffn, "final_norm": v_final_norm}
    names = SMALL_NAMES + ("final_norm",)
    order = [w_small[n] for n in names]
    res = adamw([pack_small([g_small[n] for n in names])], pack_small(order), pack_small([m_small[n] for n in names]),
                pack_small([v_small[n] for n in names]), name="adamw_small")
    g_s, d_s, m_s, v_s = (dict(zip(names, unpack_small(a, order))) for a in res)

    all_names = ("norm_mix", "w_in", "conv_w", "conv_b", "dt_bias", "a_log", "d_skip", "ssm_norm", "w_out", "norm_ffn",
                 "w_gate", "w_up", "w_down", "final_norm")
    outs = [loss, grad_x]
    for src_big, src_small in ((g_big, g_s), (d_big, d_s), (m_big, m_s), (v_big, v_s)):
        outs += [src_big[n] if n in BIG_NAMES else src_small[n] for n in all_names]
    return tuple(outs)
```

```python
import functools

import jax
import jax.numpy as jnp
from jax import lax
from jax.experimental import pallas as pl
from jax.experimental.pallas import tpu as pltpu

F32 = jnp.float32
BF16 = jnp.bfloat16
MESH = pl.DeviceIdType.MESH

D_MODEL = 1024
DEPTH = 2
HEAD_DIM = 64
N_Q_HEADS = 8
N_KV_HEADS = 2
GQA = N_Q_HEADS // N_KV_HEADS
ATTN_WIDTH = N_Q_HEADS * HEAD_DIM
ROPE_DIM = HEAD_DIM // 4
ROPE_HALF = ROPE_DIM // 2
ROPE_THETA = 500000.0
DILATIONS = (1, 4, 16)
ATTN_BLOCK = 128
SSM_P = 64
SSM_HEADS = 16
SSM_INNER = SSM_HEADS * SSM_P
SSM_GROUPS = 2
HEADS_PER_GROUP = SSM_HEADS // SSM_GROUPS
D_STATE = 128
CONV_WIDTH = 4
CHUNK = 128
CONV_CH = SSM_INNER + 2 * SSM_GROUPS * D_STATE
MIX_WIDTH = ATTN_WIDTH + SSM_INNER
Q_END = ATTN_WIDTH
K_END = Q_END + N_KV_HEADS * HEAD_DIM
V_END = K_END + N_KV_HEADS * HEAD_DIM
Z_END = V_END + SSM_INNER
XBC_END = Z_END + CONV_CH
IN_PROJ = XBC_END + SSM_HEADS
LANE = 128
IN_PAD = XBC_END + LANE
Q_COL, Z_COL, XBC_COL, K_COL, V_COL, DT_COL = 0, 512, 1536, 3072, 3200, 3328
FFN_HIDDEN = 2816
EPS = 1e-5
ADAM_LR, ADAM_B1, ADAM_B2, ADAM_EPS, ADAM_WD, ADAM_STEP = 0.001, 0.9, 0.999, 1e-8, 0.01, 10
N_CHIPS = 4
N_DEV = 8
VMEM_LIMIT = 48 * 1024 * 1024
NEG_BIG = -1e30


def _params(sem=None):
    return pltpu.CompilerParams(dimension_semantics=sem, vmem_limit_bytes=VMEM_LIMIT)


def _pick(n, prefs):
    for p in prefs:
        if n % p == 0:
            return p
    return n


def matmul(a, b, *, name, ta=False, tb=False, out_dtype=F32, residual=None):
    if ta:
        assert not tb and residual is None
        return _matmul_over_rows(a, b, name=name, out_dtype=out_dtype)
    return _matmul_full_k(a, b, name=name, tb=tb, out_dtype=out_dtype, residual=residual)


def _matmul_full_k(a, b, *, name, tb, out_dtype, residual):
    a_parts = list(a) if isinstance(a, (list, tuple)) else [a]
    n_a = len(a_parts)
    m = a_parts[0].shape[0]
    kdim = sum(p.shape[1] for p in a_parts)
    wide = kdim > 1536 or any(p.dtype == F32 for p in a_parts)
    n = b.shape[0] if tb else b.shape[1]
    tm = _pick(m, (512, 256)) if wide else _pick(m, (1024, 512, 256))
    tn = _pick(n, (1152, 1408, 1536, 1024, 768, 512, 384, 256, 128))
    b_spec = pl.BlockSpec((tn, kdim), lambda i, j: (j, 0)) if tb else pl.BlockSpec((kdim, tn), lambda i, j: (0, j))
    o_spec = pl.BlockSpec((tm, tn), lambda i, j: (i, j))
    dims = (((1,), (1 if tb else 0,)), ((), ()))
    has_res = residual is not None

    def body(*refs):
        b_ref, o_ref = refs[n_a], refs[-1]
        pieces = [r[...].astype(BF16) for r in refs[:n_a]]
        av = pieces[0] if n_a == 1 else jnp.concatenate(pieces, axis=1)
        r = lax.dot_general(av, b_ref[...].astype(BF16), dims, preferred_element_type=F32)
        if has_res:
            r = r + refs[n_a + 1][...]
        o_ref[...] = r.astype(out_dtype)

    in_specs = ([pl.BlockSpec((tm, p.shape[1]), lambda i, j: (i, 0)) for p in a_parts] + [b_spec]
                + ([o_spec] if has_res else []))
    args = tuple(a_parts) + (b,) + ((residual,) if has_res else ())
    return pl.pallas_call(
        body, name=name, grid=(m // tm, n // tn), in_specs=in_specs, out_specs=o_spec,
        out_shape=jax.ShapeDtypeStruct((m, n), out_dtype),
        compiler_params=_params(("parallel", "parallel")),
    )(*args)


def _matmul_over_rows(a, b, *, name, out_dtype):
    t, m = a.shape
    n = b.shape[1]
    tm = _pick(m, (1024, 1408, 768, 512, 256, 128))
    tn = _pick(n, (1152, 1408, 1024, 768, 512, 384, 256, 128))
    tk = _pick(t, (1024, 512, 256, 128))
    nk = t // tk

    def body(a_ref, b_ref, o_ref, acc):
        k = pl.program_id(2)
        part = lax.dot_general(a_ref[...].astype(BF16), b_ref[...].astype(BF16), (((0,), (0,)), ((), ())),
                               preferred_element_type=F32)

        @pl.when(k == 0)
        def _():
            acc[...] = part

        @pl.when(k > 0)
        def _():
            acc[...] += part

        @pl.when(k == nk - 1)
        def _():
            o_ref[...] = acc[...].astype(out_dtype)

    return pl.pallas_call(
        body, name=name, grid=(m // tm, n // tn, nk),
        in_specs=[pl.BlockSpec((tk, tm), lambda i, j, k: (k, i)), pl.BlockSpec((tk, tn), lambda i, j, k: (k, j))],
        out_specs=pl.BlockSpec((tm, tn), lambda i, j, k: (i, j)),
        out_shape=jax.ShapeDtypeStruct((m, n), out_dtype),
        scratch_shapes=[pltpu.VMEM((tm, tn), F32)],
        compiler_params=_params(("parallel", "parallel", "arbitrary")),
    )(a, b)


ROW_BLOCK_BYTES = 16 * 1024 * 1024


def _row_tile(t, tr, widths, n_copies):
    lanes = sum(-(-wd // LANE) * LANE for wd in widths) * n_copies
    tr = min(tr, t)
    while tr > 8 and tr * lanes * 4 > ROW_BLOCK_BYTES:
        tr //= 2
    return tr


def _row_widths(rows, groups, windows):
    windows = windows or [None] * len(rows)
    widths = [(w[1] if w else a.shape[1]) // groups for a, w in zip(rows, windows)]
    assert all(w is None or w[0] % wd == 0 for w, wd in zip(windows, widths))
    return widths, [(w[0] // wd if w else 0) for w, wd in zip(windows, widths)]


def _row_specs(tr, widths, offs):
    return [pl.BlockSpec((tr, wd), functools.partial(lambda g, i, off: (i, g + off), off=off)) for wd, off in zip(widths, offs)]


def rowwise_fwd(fn, rows, params, out_dtypes, *, name, tr=512, groups=1, windows=None):
    t = rows[0].shape[0]
    widths, offs = _row_widths(rows, groups, windows)
    tr = _row_tile(t, tr, widths, 2)
    row_specs = _row_specs(tr, widths, offs)
    par_spec = lambda p: pl.BlockSpec((1, p.shape[1] // groups), lambda g, i: (0, g))
    n_in = len(rows) + len(params)
    out_cols = [o.shape[1] for o in jax.eval_shape(
        fn, *[jax.ShapeDtypeStruct((tr, wd), F32) for wd in widths],
        *[jax.ShapeDtypeStruct((1, p.shape[1] // groups), F32) for p in params])]

    def body(*refs):
        vals = [r[...].astype(F32) for r in refs[:n_in]]
        outs = fn(*vals)
        for o_ref, o in zip(refs[n_in:], outs):
            o_ref[...] = o.astype(o_ref.dtype)

    return pl.pallas_call(
        body, name=name, grid=(groups, t // tr),
        in_specs=row_specs + [par_spec(p) for p in params],
        out_specs=[pl.BlockSpec((tr, c), lambda g, i: (i, g)) for c in out_cols],
        out_shape=[jax.ShapeDtypeStruct((t, c * groups), d) for c, d in zip(out_cols, out_dtypes)],
        compiler_params=_params(("arbitrary", "arbitrary")),
    )(*rows, *params)


def rowwise_bwd(fn, rows, params, cts, drow_dtypes, *, name, tr=512, groups=1, add_to_first=None, windows=None,
                ct_windows=None):
    t = rows[0].shape[0]
    widths, offs = _row_widths(rows, groups, windows)
    ct_widths, ct_offs = _row_widths(cts, groups, ct_windows)
    tr = _row_tile(t, tr, widths + ct_widths, 2)
    row_spec = lambda a: pl.BlockSpec((tr, a.shape[1] // groups), lambda g, i: (i, g))
    row_specs = _row_specs(tr, widths, offs)
    par_spec = lambda p: pl.BlockSpec((1, p.shape[1] // groups), lambda g, i: (0, g))
    n_rows, n_par, n_ct = len(rows), len(params), len(cts)
    has_add = add_to_first is not None
    n_in = n_rows + n_par + n_ct + (1 if has_add else 0)

    def body(*refs):
        i = pl.program_id(1)
        vals = [r[...].astype(F32) for r in refs[:n_rows + n_par]]
        ct_vals = tuple(r[...].astype(F32) for r in refs[n_rows + n_par:n_rows + n_par + n_ct])
        _, vjp = jax.vjp(fn, *vals)
        grads = vjp(ct_vals)
        out_refs = refs[n_in:]
        for idx in range(n_rows):
            g = grads[idx]
            if idx == 0 and has_add:
                g = g + refs[n_in - 1][...]
            out_refs[idx][...] = g.astype(out_refs[idx].dtype)
        for idx in range(n_par):
            p_ref = out_refs[n_rows + idx]

            @pl.when(i == 0)
            def _():
                p_ref[...] = jnp.zeros_like(p_ref)

            p_ref[...] += grads[n_rows + idx]

    ins = list(rows) + list(params) + list(cts) + ([add_to_first] if has_add else [])
    in_specs = (row_specs + [par_spec(p) for p in params] + _row_specs(tr, ct_widths, ct_offs)
                + ([row_spec(add_to_first)] if has_add else []))
    return pl.pallas_call(
        body, name=name, grid=(groups, t // tr), in_specs=in_specs,
        out_specs=[pl.BlockSpec((tr, wd), lambda g, i: (i, g)) for wd in widths] + [par_spec(p) for p in params],
        out_shape=[jax.ShapeDtypeStruct((t, wd * groups), d) for wd, d in zip(widths, drow_dtypes)]
        + [jax.ShapeDtypeStruct(p.shape, F32) for p in params],
        compiler_params=_params(("arbitrary", "arbitrary")),
    )(*ins)


def rms_fn(x, w):
    return (x * lax.rsqrt(jnp.mean(x * x, axis=-1, keepdims=True) + EPS) * w,)


def swiglu_fn(g, u):
    return (g * jax.nn.sigmoid(g) * u,)


def gated_norm_fn(y, z, w):
    v = y * (z * jax.nn.sigmoid(z))
    return (v * lax.rsqrt(jnp.mean(v * v, axis=-1, keepdims=True) + EPS) * w,)


def combine_fn(o1, o2, o3, l1, l2, l3):
    m = jnp.maximum(jnp.maximum(l1, l2), l3)
    e1, e2, e3 = jnp.exp(l1 - m), jnp.exp(l2 - m), jnp.exp(l3 - m)
    inv = 1.0 / (e1 + e2 + e3)
    return ((e1 * inv) * o1 + (e2 * inv) * o2 + (e3 * inv) * o3,)


def loss_and_grad(h, target, w, *, tr=512):
    t, d = h.shape

    def loss_fn(hv, wv, tv):
        err = rms_fn(hv, wv)[0] - tv
        per_row = jnp.mean(err * err, axis=-1, keepdims=True)
        return 0.5 * jnp.sum(per_row, axis=0, keepdims=True)

    def body(h_ref, t_ref, w_ref, dh_ref, dw_ref, loss_ref):
        i = pl.program_id(0)

        @pl.when(i == 0)
        def _():
            dw_ref[...] = jnp.zeros_like(dw_ref)
            loss_ref[...] = jnp.zeros_like(loss_ref)

        tv = t_ref[...]
        val, vjp = jax.vjp(lambda hv, wv: loss_fn(hv, wv, tv), h_ref[...], w_ref[...])
        dh, dw = vjp(jnp.ones((1, 1), F32))
        dh_ref[...] = dh
        dw_ref[...] += dw
        loss_ref[...] += jnp.broadcast_to(val, loss_ref.shape)

    row = pl.BlockSpec((tr, d), lambda i: (i, 0))
    par = pl.BlockSpec((1, d), lambda i: (0, 0))
    return pl.pallas_call(
        body, name="loss_and_grad", grid=(t // tr,), in_specs=[row, row, par],
        out_specs=[row, par, pl.BlockSpec((1, LANE), lambda i: (0, 0))],
        out_shape=[jax.ShapeDtypeStruct((t, d), F32), jax.ShapeDtypeStruct((1, d), F32),
                   jax.ShapeDtypeStruct((1, LANE), F32)],
        compiler_params=_params(("arbitrary",)),
    )(h, target, w)


def _split3(x):
    hi = x.astype(BF16)
    r1 = x - hi.astype(F32)
    mid = r1.astype(BF16)
    lo = (r1 - mid.astype(F32)).astype(BF16)
    return hi, mid, lo


def _dot01_left(m01, x):
    return sum(jnp.dot(m01, p, preferred_element_type=F32) for p in _split3(x))


def _dot01_right(x, m01):
    return sum(jnp.dot(p, m01, preferred_element_type=F32) for p in _split3(x))


def rotary(xs_list, cosf, sinf, scale, *, adjoint, name, ts=512):
    b, h, s, c = xs_list[0].shape
    n_x = len(xs_list)

    def body(*refs):
        x = refs[0][0, 0]
        for r in refs[1:n_x]:
            x = x + r[0, 0]
        cos_v, sin_v = refs[n_x][0], refs[n_x + 1][0]
        o_ref = refs[n_x + 2]
        ci = lax.broadcasted_iota(jnp.int32, (c, c), 0)
        cj = lax.broadcasted_iota(jnp.int32, (c, c), 1)
        swap = ((cj == ci + ROPE_HALF) & (ci < ROPE_HALF)) | ((cj == ci - ROPE_HALF) & (ci >= ROPE_HALF) & (ci < ROPE_DIM))
        swap = swap.astype(BF16)
        if adjoint:
            out = x * cos_v + _dot01_right(x * sin_v, swap)
        else:
            out = x * cos_v + _dot01_right(x, swap) * sin_v
        o_ref[0, 0] = out * scale

    x_spec = pl.BlockSpec((1, 1, ts, c), lambda bi, hi, si: (bi, hi, si, 0))
    t_spec = pl.BlockSpec((1, ts, c), lambda bi, hi, si: (bi, si, 0))
    return pl.pallas_call(
        body, name=name, grid=(b, h, s // ts), in_specs=[x_spec] * n_x + [t_spec, t_spec], out_specs=x_spec,
        out_shape=jax.ShapeDtypeStruct((b, h, s, c), F32),
        compiler_params=_params(("parallel", "parallel", "parallel")),
    )(*xs_list, cosf, sinf)


def add3(a, b, c, *, name, tr=1024):
    def fn(x, y, z):
        return (x + y + z,)
    return rowwise_fwd(fn, [a, b, c], [], [F32], name=name, tr=tr)[0]


def _attn_mask(n):
    rows = GQA * ATTN_BLOCK
    qi = lax.broadcasted_iota(jnp.int32, (rows, 2 * ATTN_BLOCK), 0) % ATTN_BLOCK
    ki = lax.broadcasted_iota(jnp.int32, (rows, 2 * ATTN_BLOCK), 1)
    delta = qi + ATTN_BLOCK - ki
    return (delta >= 0) & (delta <= ATTN_BLOCK) & ((n - 1) * ATTN_BLOCK + ki >= 0)


def _attn_specs(l):
    q_spec = pl.BlockSpec((1, GQA, ATTN_BLOCK, HEAD_DIM), lambda p, n: (p, 0, n, 0))
    l_spec = pl.BlockSpec((1, GQA, ATTN_BLOCK, 1), lambda p, n: (p, 0, n, 0))
    kprev = pl.BlockSpec((1, ATTN_BLOCK, HEAD_DIM), lambda p, n: (p, jnp.maximum(n - 1, 0), 0))
    kcur = pl.BlockSpec((1, ATTN_BLOCK, HEAD_DIM), lambda p, n: (p, n, 0))
    kfull = pl.BlockSpec((1, l, HEAD_DIM), lambda p, n: (p, 0, 0))
    return q_spec, l_spec, kprev, kcur, kfull


def attn_branch_fwd(q, k, v, *, name):
    p_cnt, _, l, _ = q.shape
    rows = GQA * ATTN_BLOCK
    q_spec, l_spec, kprev, kcur, _ = _attn_specs(l)

    def body(q_ref, kp_ref, kc_ref, vp_ref, vc_ref, o_ref, lse_ref):
        n = pl.program_id(1)
        qv = q_ref[0].reshape(rows, HEAD_DIM).astype(BF16)
        kk = jnp.concatenate([kp_ref[0], kc_ref[0]], axis=0).astype(BF16)
        vv = jnp.concatenate([vp_ref[0], vc_ref[0]], axis=0).astype(BF16)
        s = lax.dot_general(qv, kk, (((1,), (1,)), ((), ())), preferred_element_type=F32)
        s = jnp.where(_attn_mask(n), s, NEG_BIG)
        m = jnp.max(s, axis=-1, keepdims=True)
        pr = jnp.exp(s - m)
        den = jnp.sum(pr, axis=-1, keepdims=True)
        o = jnp.dot(pr.astype(BF16), vv, preferred_element_type=F32) / den
        o_ref[0] = o.reshape(GQA, ATTN_BLOCK, HEAD_DIM)
        lse_ref[0] = (m + jnp.log(den)).reshape(GQA, ATTN_BLOCK, 1)

    return pl.pallas_call(
        body, name=name, grid=(p_cnt, l // ATTN_BLOCK), in_specs=[q_spec, kprev, kcur, kprev, kcur],
        out_specs=[q_spec, l_spec],
        out_shape=[jax.ShapeDtypeStruct(q.shape, F32), jax.ShapeDtypeStruct(q.shape[:3] + (1,), F32)],
        compiler_params=_params(("parallel", "arbitrary")),
    )(q, k, k, v, v)


def attn_branch_bwd(q, k, v, o, lse, do, dlse, *, name):
    p_cnt, _, l, _ = q.shape
    rows = GQA * ATTN_BLOCK
    q_spec, l_spec, kprev, kcur, kfull = _attn_specs(l)

    def body(q_ref, kp_ref, kc_ref, vp_ref, vc_ref, o_ref, lse_ref, do_ref, dlse_ref, dq_ref, dk_ref, dv_ref):
        n = pl.program_id(1)

        @pl.when(n == 0)
        def _():
            dk_ref[...] = jnp.zeros_like(dk_ref)
            dv_ref[...] = jnp.zeros_like(dv_ref)

        qv = q_ref[0].reshape(rows, HEAD_DIM).astype(BF16)
        kk = jnp.concatenate([kp_ref[0], kc_ref[0]], axis=0).astype(BF16)
        vv = jnp.concatenate([vp_ref[0], vc_ref[0]], axis=0).astype(BF16)
        ov = o_ref[0].reshape(rows, HEAD_DIM)
        dov = do_ref[0].reshape(rows, HEAD_DIM)
        lsev = lse_ref[0].reshape(rows, 1)
        dlsev = dlse_ref[0].reshape(rows, 1)
        s = lax.dot_general(qv, kk, (((1,), (1,)), ((), ())), preferred_element_type=F32)
        pr = jnp.where(_attn_mask(n), jnp.exp(s - lsev), 0.0)
        do16 = dov.astype(BF16)
        dv = lax.dot_general(pr.astype(BF16), do16, (((0,), (0,)), ((), ())), preferred_element_type=F32)
        dp = lax.dot_general(do16, vv, (((1,), (1,)), ((), ())), preferred_element_type=F32)
        delta = jnp.sum(dov * ov, axis=-1, keepdims=True)
        ds = (pr * (dp - delta + dlsev)).astype(BF16)
        dq = jnp.dot(ds, kk, preferred_element_type=F32)
        dk = lax.dot_general(ds, qv, (((0,), (0,)), ((), ())), preferred_element_type=F32)
        dq_ref[0] = dq.reshape(GQA, ATTN_BLOCK, HEAD_DIM)
        cur = pl.ds(pl.multiple_of(n * ATTN_BLOCK, ATTN_BLOCK), ATTN_BLOCK)
        dk_ref[0, cur, :] += dk[ATTN_BLOCK:]
        dv_ref[0, cur, :] += dv[ATTN_BLOCK:]

        @pl.when(n > 0)
        def _():
            prev = pl.ds(pl.multiple_of((n - 1) * ATTN_BLOCK, ATTN_BLOCK), ATTN_BLOCK)
            dk_ref[0, prev, :] += dk[:ATTN_BLOCK]
            dv_ref[0, prev, :] += dv[:ATTN_BLOCK]

    return pl.pallas_call(
        body, name=name, grid=(p_cnt, l // ATTN_BLOCK),
        in_specs=[q_spec, kprev, kcur, kprev, kcur, q_spec, l_spec, q_spec, l_spec],
        out_specs=[q_spec, kfull, kfull],
        out_shape=[jax.ShapeDtypeStruct(q.shape, F32), jax.ShapeDtypeStruct(k.shape, F32),
                   jax.ShapeDtypeStruct(v.shape, F32)],
        compiler_params=_params(("parallel", "arbitrary")),
    )(q, k, k, v, v, o, lse, do, dlse)


ATTN_PAD = ATTN_BLOCK * DILATIONS[-1]
Q_GROUP_W = GQA * HEAD_DIM
ATTN_VMEM_LIMIT = 56 * 1024 * 1024


def _rope(x, cos_v, sin_v, swap, scale, adjoint):
    if adjoint:
        return (x * cos_v + _dot01_right(x * sin_v, swap)) * scale
    return (x * cos_v + _dot01_right(x, swap) * sin_v) * scale


def _swap_matrix():
    c = HEAD_DIM
    ci = lax.broadcasted_iota(jnp.int32, (c, c), 0)
    cj = lax.broadcasted_iota(jnp.int32, (c, c), 1)
    swap = ((cj == ci + ROPE_HALF) & (ci < ROPE_HALF)) | ((cj == ci - ROPE_HALF) & (ci >= ROPE_HALF) & (ci < ROPE_DIM))
    return swap.astype(BF16)


def _attn_prologue(q_ref, kv_ref, tab_ref, q_s, k_s, v_s, hk, s_len):
    swap = _swap_matrix()
    cos_v, sin_v = tab_ref[0, :, :HEAD_DIM], tab_ref[0, :, HEAD_DIM:]
    for g in range(GQA):
        cols = slice(g * HEAD_DIM, (g + 1) * HEAD_DIM)
        q_s[:, cols] = _rope(q_ref[0, :, cols], cos_v, sin_v, swap, HEAD_DIM ** -0.5, False)
    zeros = jnp.zeros((ATTN_PAD, HEAD_DIM), F32)
    k_s[0:ATTN_PAD, :] = zeros
    v_s[0:ATTN_PAD, :] = zeros
    for h in range(N_KV_HEADS):
        @pl.when(hk == h)
        def _():
            k_s[ATTN_PAD:ATTN_PAD + s_len, :] = _rope(kv_ref[0, :, h * HEAD_DIM:(h + 1) * HEAD_DIM], cos_v, sin_v, swap, 1.0, False)
            v_s[ATTN_PAD:ATTN_PAD + s_len, :] = kv_ref[0, :, LANE + h * HEAD_DIM:LANE + (h + 1) * HEAD_DIM]


def _attn_blocks(s_len):
    out = []
    for i, d in enumerate(DILATIONS):
        nb = s_len // (ATTN_BLOCK * d)
        for r in range(d):
            for n in range(nb):
                start = r + d * ATTN_BLOCK * n
                out.append((i, d, start, ATTN_PAD + start - d * ATTN_BLOCK, n))
    return out


def _rows(start, size, d):
    return pl.ds(start, size, stride=d) if d > 1 else pl.ds(start, size)


def _stack_heads(blk):
    return jnp.concatenate([blk[:, g * HEAD_DIM:(g + 1) * HEAD_DIM] for g in range(GQA)], axis=0)


def _stack_stats(blk):
    return jnp.concatenate([jnp.max(blk[:, g * HEAD_DIM:(g + 1) * HEAD_DIM], axis=1, keepdims=True) for g in range(GQA)], axis=0)


def _attn_in_specs(s_len):
    assert K_COL % (2 * LANE) == 0 and V_COL == K_COL + LANE
    q_spec = pl.BlockSpec((1, s_len, Q_GROUP_W), lambda b, h: (b, 0, Q_COL // Q_GROUP_W + h))
    kv_spec = pl.BlockSpec((1, s_len, 2 * LANE), lambda b, h: (b, 0, K_COL // (2 * LANE)))
    t_spec = pl.BlockSpec((1, s_len, 2 * HEAD_DIM), lambda b, h: (b, 0, 0))
    o_spec = pl.BlockSpec((1, s_len, Q_GROUP_W), lambda b, h: (b, 0, h))
    return q_spec, kv_spec, t_spec, o_spec


def attn_fwd(proj3, rope_tab, *, name):
    b, s_len, _ = proj3.shape
    q_spec, kv_spec, t_spec, o_spec = _attn_in_specs(s_len)
    n_br = len(DILATIONS)

    def body(q_ref, kv_ref, tab_ref, o_ref, lse_ref, q_s, k_s, v_s, *branch_s):
        o_s, l_s = branch_s[:n_br], branch_s[n_br:]
        _attn_prologue(q_ref, kv_ref, tab_ref, q_s, k_s, v_s, pl.program_id(1), s_len)
        for i, d, q0, k0, n in _attn_blocks(s_len):
            qv = _stack_heads(q_s[_rows(q0, ATTN_BLOCK, d), :]).astype(BF16)
            kk = k_s[_rows(k0, 2 * ATTN_BLOCK, d), :].astype(BF16)
            vv = v_s[_rows(k0, 2 * ATTN_BLOCK, d), :].astype(BF16)
            sc = lax.dot_general(qv, kk, (((1,), (1,)), ((), ())), preferred_element_type=F32)
            sc = jnp.where(_attn_mask(n), sc, NEG_BIG)
            m = jnp.max(sc, axis=-1, keepdims=True)
            pr = jnp.exp(sc - m)
            den = jnp.sum(pr, axis=-1, keepdims=True)
            o = jnp.dot(pr.astype(BF16), vv, preferred_element_type=F32) / den
            lse = m + jnp.log(den)
            for g in range(GQA):
                part = slice(g * ATTN_BLOCK, (g + 1) * ATTN_BLOCK)
                o_s[i][_rows(q0, ATTN_BLOCK, d), g * HEAD_DIM:(g + 1) * HEAD_DIM] = o[part]
                l_s[i][_rows(q0, ATTN_BLOCK, d), g * HEAD_DIM:(g + 1) * HEAD_DIM] = jnp.broadcast_to(lse[part], (ATTN_BLOCK, HEAD_DIM))
        step = 256
        for t0 in range(0, s_len, step):
            rs = pl.ds(t0, step)
            for g in range(GQA):
                ls = [l_s[i][rs, g * HEAD_DIM:(g + 1) * HEAD_DIM] for i in range(n_br)]
                m = functools.reduce(jnp.maximum, ls)
                es = [jnp.exp(l - m) for l in ls]
                tot = functools.reduce(lambda a, c: a + c, es)
                inv = 1.0 / tot
                acc = None
                for i in range(n_br):
                    term = (es[i] * inv) * o_s[i][rs, g * HEAD_DIM:(g + 1) * HEAD_DIM]
                    acc = term if acc is None else acc + term
                o_ref[0, rs, g * HEAD_DIM:(g + 1) * HEAD_DIM] = acc
                lse_ref[0, rs, g * HEAD_DIM:(g + 1) * HEAD_DIM] = m + jnp.log(tot)

    return pl.pallas_call(
        body, name=name, grid=(b, N_KV_HEADS), in_specs=[q_spec, kv_spec, t_spec],
        out_specs=[o_spec, o_spec],
        out_shape=[jax.ShapeDtypeStruct((b, s_len, ATTN_WIDTH), F32)] * 2,
        scratch_shapes=[pltpu.VMEM((s_len, Q_GROUP_W), F32), pltpu.VMEM((ATTN_PAD + s_len, HEAD_DIM), F32),
                        pltpu.VMEM((ATTN_PAD + s_len, HEAD_DIM), F32)] + [pltpu.VMEM((s_len, Q_GROUP_W), F32)] * (2 * n_br),
        compiler_params=pltpu.CompilerParams(dimension_semantics=("arbitrary", "arbitrary"), vmem_limit_bytes=ATTN_VMEM_LIMIT),
    )(proj3, proj3, rope_tab)


def attn_bwd(proj3, rope_tab, attn3, lse3, d_attn3, *, name):
    b, s_len, _ = proj3.shape
    q_spec, kv_spec, t_spec, o_spec = _attn_in_specs(s_len)
    kv_out = pl.BlockSpec((1, 1, s_len, HEAD_DIM), lambda bi, h: (bi, h, 0, 0))

    def body(q_ref, kv_ref, tab_ref, o_ref, lse_ref, do_ref, dq_ref, dk_ref, dv_ref,
             q_s, k_s, v_s, dl_s, dq_s, dk_s, dv_s):
        _attn_prologue(q_ref, kv_ref, tab_ref, q_s, k_s, v_s, pl.program_id(1), s_len)
        dq_s[...] = jnp.zeros_like(dq_s)
        dk_s[...] = jnp.zeros_like(dk_s)
        dv_s[...] = jnp.zeros_like(dv_s)
        for g in range(GQA):
            cols = slice(g * HEAD_DIM, (g + 1) * HEAD_DIM)
            delta = jnp.sum(do_ref[0, :, cols] * o_ref[0, :, cols], axis=1, keepdims=True)
            dl_s[:, cols] = jnp.broadcast_to(delta, (s_len, HEAD_DIM))
        for i, d, q0, k0, n in _attn_blocks(s_len):
            qrows, krows = _rows(q0, ATTN_BLOCK, d), _rows(k0, 2 * ATTN_BLOCK, d)
            qv = _stack_heads(q_s[qrows, :]).astype(BF16)
            kk = k_s[krows, :].astype(BF16)
            vv = v_s[krows, :].astype(BF16)
            do16 = _stack_heads(do_ref.at[0][qrows, :]).astype(BF16)
            lse = _stack_stats(lse_ref.at[0][qrows, :])
            delta = _stack_stats(dl_s[qrows, :])
            sc = lax.dot_general(qv, kk, (((1,), (1,)), ((), ())), preferred_element_type=F32)
            pr = jnp.where(_attn_mask(n), jnp.exp(sc - lse), 0.0)
            dv = lax.dot_general(pr.astype(BF16), do16, (((0,), (0,)), ((), ())), preferred_element_type=F32)
            dp = lax.dot_general(do16, vv, (((1,), (1,)), ((), ())), preferred_element_type=F32)
            ds = (pr * (dp - delta)).astype(BF16)
            dq = jnp.dot(ds, kk, preferred_element_type=F32)
            dk = lax.dot_general(ds, qv, (((0,), (0,)), ((), ())), preferred_element_type=F32)
            for g in range(GQA):
                cols = slice(g * HEAD_DIM, (g + 1) * HEAD_DIM)
                dq_s[qrows, cols] += dq[g * ATTN_BLOCK:(g + 1) * ATTN_BLOCK]
            dk_s[krows, :] += dk
            dv_s[krows, :] += dv
        swap = _swap_matrix()
        cos_v, sin_v = tab_ref[0, :, :HEAD_DIM], tab_ref[0, :, HEAD_DIM:]
        for g in range(GQA):
            cols = slice(g * HEAD_DIM, (g + 1) * HEAD_DIM)
            dq_ref[0, :, cols] = _rope(dq_s[:, cols], cos_v, sin_v, swap, HEAD_DIM ** -0.5, True)
        dk_ref[0, 0] = _rope(dk_s[ATTN_PAD:ATTN_PAD + s_len, :], cos_v, sin_v, swap, 1.0, True)
        dv_ref[0, 0] = dv_s[ATTN_PAD:ATTN_PAD + s_len, :]

    kv_shape = jax.ShapeDtypeStruct((b, N_KV_HEADS, s_len, HEAD_DIM), F32)
    return pl.pallas_call(
        body, name=name, grid=(b, N_KV_HEADS),
        in_specs=[q_spec, kv_spec, t_spec, o_spec, o_spec, o_spec],
        out_specs=[o_spec, kv_out, kv_out],
        out_shape=[jax.ShapeDtypeStruct((b, s_len, ATTN_WIDTH), F32), kv_shape, kv_shape],
        scratch_shapes=[pltpu.VMEM((s_len, Q_GROUP_W), F32), pltpu.VMEM((ATTN_PAD + s_len, HEAD_DIM), F32),
                        pltpu.VMEM((ATTN_PAD + s_len, HEAD_DIM), F32), pltpu.VMEM((s_len, Q_GROUP_W), F32),
                        pltpu.VMEM((s_len, Q_GROUP_W), F32), pltpu.VMEM((ATTN_PAD + s_len, HEAD_DIM), F32),
                        pltpu.VMEM((ATTN_PAD + s_len, HEAD_DIM), F32)],
        compiler_params=pltpu.CompilerParams(dimension_semantics=("arbitrary", "arbitrary"), vmem_limit_bytes=ATTN_VMEM_LIMIT),
    )(proj3, proj3, rope_tab, attn3, lse3, d_attn3)


HALF_W = 2 * HEAD_DIM
N_HALF = Q_GROUP_W // HALF_W
_ATTN_BIAS_BUF = pltpu.VMEM((2, GQA * ATTN_BLOCK, 2 * ATTN_BLOCK), F32)


def _attn_bias(bias_s):
    for first in (0, 1):
        bias_s[first] = jnp.where(_attn_mask(first), 0.0, NEG_BIG)


def _attn_prologue(q_refs, kv_ref, tab_ref, q_s, kv_s, hk, s_len):
    swap = _swap_matrix()
    cos_v, sin_v = tab_ref[0, :, :HEAD_DIM], tab_ref[0, :, HEAD_DIM:]
    for j in range(N_HALF):
        for e in range(2):
            cols = slice(e * HEAD_DIM, (e + 1) * HEAD_DIM)
            q_s[j][:, cols] = _rope(q_refs[j][0, :, cols], cos_v, sin_v, swap, HEAD_DIM ** -0.5, False)
    kv_s[0:ATTN_PAD, :] = jnp.zeros((ATTN_PAD, HALF_W), F32)
    for h in range(N_KV_HEADS):
        @pl.when(hk == h)
        def _():
            kv_s[ATTN_PAD:ATTN_PAD + s_len, :HEAD_DIM] = _rope(kv_ref[0, :, h * HEAD_DIM:(h + 1) * HEAD_DIM], cos_v, sin_v,
                                                               swap, 1.0, False)
            kv_s[ATTN_PAD:ATTN_PAD + s_len, HEAD_DIM:] = kv_ref[0, :, LANE + h * HEAD_DIM:LANE + (h + 1) * HEAD_DIM]


def _stack_heads(halves):
    return jnp.concatenate([h[:, e * HEAD_DIM:(e + 1) * HEAD_DIM] for h in halves for e in range(2)], axis=0)


def _unstack_heads(x, j):
    return jnp.concatenate([x[(2 * j + e) * ATTN_BLOCK:(2 * j + e + 1) * ATTN_BLOCK] for e in range(2)], axis=1)


def _stack_stats(halves):
    return jnp.concatenate([jnp.max(h[:, e * HEAD_DIM:(e + 1) * HEAD_DIM], axis=1, keepdims=True)
                            for h in halves for e in range(2)], axis=0)


def _attn_in_specs(s_len):
    assert K_COL % (2 * LANE) == 0 and V_COL == K_COL + LANE

    def halves(first_tile):
        return [pl.BlockSpec((1, s_len, HALF_W), functools.partial(lambda b, h, j: (b, 0, first_tile + N_HALF * h + j), j=j))
                for j in range(N_HALF)]

    kv_spec = pl.BlockSpec((1, s_len, 2 * LANE), lambda b, h: (b, 0, K_COL // (2 * LANE)))
    t_spec = pl.BlockSpec((1, s_len, 2 * HEAD_DIM), lambda b, h: (b, 0, 0))
    o_spec = pl.BlockSpec((1, s_len, Q_GROUP_W), lambda b, h: (b, 0, h))
    return halves(Q_COL // HALF_W), kv_spec, t_spec, o_spec, halves(0)


class SideCopy:
    def __init__(self, side, *, n_in, n_out, grid):
        self.side, self.n_in, self.n_out, self.grid = side, n_in, n_out, grid
        hbm = pl.BlockSpec(memory_space=pltpu.HBM)
        if side is None:
            self.in_specs, self.out_specs, self.out_shape, self.scratch, self.args = [], [], [], [], []
            return
        srcs, per_dest = side
        n = len(srcs)
        self.in_specs, self.out_specs, self.args = [hbm] * n, [hbm] * n, list(srcs)
        self.out_shape = [jax.ShapeDtypeStruct(s.shape if per_dest else (N_CHIPS,) + s.shape, s.dtype) for s in srcs]
        self.scratch = [pltpu.SemaphoreType.DMA(((N_CHIPS - 1) * n,)), pltpu.SemaphoreType.DMA(((N_CHIPS - 1) * n,)),
                        pltpu.SemaphoreType.DMA((n,))]

    def wrap(self, body):
        if self.side is None:
            return body
        n_in, n_out, grid, per_dest, n = self.n_in, self.n_out, self.grid, self.side[1], len(self.side[0])

        def wrapped(*refs):
            ins, srcs = refs[:n_in], refs[n_in:n_in + n]
            outs, dsts = refs[n_in + n:n_in + n + n_out], refs[n_in + n + n_out:n_in + 2 * n + n_out]
            scratch, sems = refs[n_in + 2 * n + n_out:-3], refs[-3:]
            ids = [pl.program_id(a) for a in range(len(grid))]
            first = functools.reduce(lambda p, q: p & q, [i == 0 for i in ids])
            last = functools.reduce(lambda p, q: p & q, [i == g - 1 for i, g in zip(ids, grid)])

            @pl.when(first)
            def _():
                for a in range(n):
                    local, sends, _ = _chip_copies(srcs[a], dsts[a], *sems, per_dest, a)
                    local.start()
                    for cp in sends:
                        cp.start()

            body(*ins, *outs, *scratch)

            @pl.when(last)
            def _():
                for a in range(n):
                    local, sends, recvs = _chip_copies(srcs[a], dsts[a], *sems, per_dest, a)
                    for cp in recvs:
                        cp.wait_recv()
                    for cp in sends:
                        cp.wait_send()
                    local.wait()

        return wrapped


def _chip_copies(src_ref, dst_ref, send_sems, recv_sems, local_sems, per_dest, a=0):
    x, y, c = lax.axis_index("x"), lax.axis_index("y"), lax.axis_index("c")
    chip = 2 * x + y
    own = src_ref.at[chip] if per_dest else src_ref
    local = pltpu.make_async_copy(own, dst_ref.at[chip], local_sems.at[a])
    sends, recvs = [], []
    for k, (px, py) in enumerate([(1 - x, y), (x, 1 - y), (1 - x, 1 - y)]):
        k = (N_CHIPS - 1) * a + k
        peer = dict(send_sem=send_sems.at[k], recv_sem=recv_sems.at[k], device_id=(px, py, c), device_id_type=MESH)
        sends.append(pltpu.make_async_remote_copy(src_ref=src_ref.at[2 * px + py] if per_dest else src_ref,
                                                  dst_ref=dst_ref.at[chip], **peer))
        recvs.append(pltpu.make_async_remote_copy(src_ref=own, dst_ref=dst_ref.at[2 * px + py], **peer))
    return local, sends, recvs


def attn_fwd(proj3, rope_tab, *, name, side=None):
    b, s_len, _ = proj3.shape
    q_specs, kv_spec, t_spec, o_spec, _ = _attn_in_specs(s_len)
    n_br = len(DILATIONS)

    def body(*refs):
        q_refs, (kv_ref, tab_ref, o_ref, lse_ref) = refs[:N_HALF], refs[N_HALF:N_HALF + 4]
        scratch = refs[N_HALF + 4:]
        q_s, kv_s = scratch[:N_HALF], scratch[N_HALF]
        o_s = [scratch[N_HALF + 1 + i * N_HALF:N_HALF + 1 + (i + 1) * N_HALF] for i in range(n_br)]
        l_s = [scratch[N_HALF + 1 + (n_br + i) * N_HALF:N_HALF + 1 + (n_br + i + 1) * N_HALF] for i in range(n_br)]
        bias_s = scratch[-1]
        _attn_prologue(q_refs, kv_ref, tab_ref, q_s, kv_s, pl.program_id(1), s_len)
        _attn_bias(bias_s)
        for i, d, q0, k0, n in _attn_blocks(s_len):
            qrows = _rows(q0, ATTN_BLOCK, d)
            qv = _stack_heads([q_s[j][qrows, :] for j in range(N_HALF)]).astype(BF16)
            kvb = kv_s[_rows(k0, 2 * ATTN_BLOCK, d), :].astype(BF16)
            kk, vv = kvb[:, :HEAD_DIM], kvb[:, HEAD_DIM:]
            sc = lax.dot_general(qv, kk, (((1,), (1,)), ((), ())), preferred_element_type=F32)
            sc = sc + bias_s[min(n, 1)]
            m = jnp.max(sc, axis=-1, keepdims=True)
            pr = jnp.exp(sc - m)
            den = jnp.sum(pr, axis=-1, keepdims=True)
            o = jnp.dot(pr.astype(BF16), vv, preferred_element_type=F32) / den
            lse_b = jnp.broadcast_to(m + jnp.log(den), (GQA * ATTN_BLOCK, HEAD_DIM))
            for j in range(N_HALF):
                o_s[i][j][qrows, :] = _unstack_heads(o, j)
                l_s[i][j][qrows, :] = _unstack_heads(lse_b, j)
        step = 256
        for t0 in range(0, s_len, step):
            rs = pl.ds(t0, step)
            for j in range(N_HALF):
                ls = [l_s[i][j][rs, :] for i in range(n_br)]
                m = functools.reduce(jnp.maximum, ls)
                es = [jnp.exp(l - m) for l in ls]
                tot = functools.reduce(lambda a, c: a + c, es)
                inv = 1.0 / tot
                acc = None
                for i in range(n_br):
                    term = (es[i] * inv) * o_s[i][j][rs, :]
                    acc = term if acc is None else acc + term
                o_ref[0, rs, j * HALF_W:(j + 1) * HALF_W] = acc
                lse_ref[0, rs, j * HALF_W:(j + 1) * HALF_W] = m + jnp.log(tot)

    half_buf = pltpu.VMEM((s_len, HALF_W), F32)
    call = SideCopy(side, n_in=N_HALF + 2, n_out=2, grid=(b, N_KV_HEADS))
    return pl.pallas_call(
        call.wrap(body), name=name, grid=(b, N_KV_HEADS), in_specs=q_specs + [kv_spec, t_spec] + call.in_specs,
        out_specs=[o_spec, o_spec] + call.out_specs,
        out_shape=[jax.ShapeDtypeStruct((b, s_len, ATTN_WIDTH), F32)] * 2 + call.out_shape,
        scratch_shapes=[half_buf] * N_HALF + [pltpu.VMEM((ATTN_PAD + s_len, HALF_W), F32)] + [half_buf] * (2 * n_br * N_HALF)
        + [_ATTN_BIAS_BUF] + call.scratch,
        compiler_params=pltpu.CompilerParams(dimension_semantics=("arbitrary", "arbitrary"), vmem_limit_bytes=ATTN_VMEM_LIMIT),
    )(*([proj3] * (N_HALF + 1)), rope_tab, *call.args)


def attn_bwd(proj3, rope_tab, attn3, lse3, d_attn3, *, name, side=None):
    b, s_len, _ = proj3.shape
    q_specs, kv_spec, t_spec, o_spec, half_specs = _attn_in_specs(s_len)
    kv_out = pl.BlockSpec((1, 1, s_len, HEAD_DIM), lambda bi, h: (bi, h, 0, 0))

    def body(*refs):
        q_refs = refs[:N_HALF]
        kv_ref, tab_ref, o_ref = refs[N_HALF:N_HALF + 3]
        lse_refs = refs[N_HALF + 3:2 * N_HALF + 3]
        do_refs = refs[2 * N_HALF + 3:3 * N_HALF + 3]
        dq_ref, dkv_ref = refs[3 * N_HALF + 3:3 * N_HALF + 5]
        scratch = refs[3 * N_HALF + 5:]
        q_s, kv_s = scratch[:N_HALF], scratch[N_HALF]
        dl_s = scratch[N_HALF + 1:2 * N_HALF + 1]
        dq_s = scratch[2 * N_HALF + 1:3 * N_HALF + 1]
        dkv_s = scratch[3 * N_HALF + 1]
        bias_s = scratch[-1]
        _attn_prologue(q_refs, kv_ref, tab_ref, q_s, kv_s, pl.program_id(1), s_len)
        _attn_bias(bias_s)
        dkv_s[...] = jnp.zeros_like(dkv_s)
        for j in range(N_HALF):
            dq_s[j][...] = jnp.zeros_like(dq_s[j])
            for e in range(2):
                cols = slice(e * HEAD_DIM, (e + 1) * HEAD_DIM)
                ocols = slice(j * HALF_W + e * HEAD_DIM, j * HALF_W + (e + 1) * HEAD_DIM)
                delta = jnp.sum(do_refs[j][0, :, cols] * o_ref[0, :, ocols], axis=1, keepdims=True)
                dl_s[j][:, cols] = jnp.broadcast_to(delta, (s_len, HEAD_DIM))
        for i, d, q0, k0, n in _attn_blocks(s_len):
            qrows, krows = _rows(q0, ATTN_BLOCK, d), _rows(k0, 2 * ATTN_BLOCK, d)
            qv = _stack_heads([q_s[j][qrows, :] for j in range(N_HALF)]).astype(BF16)
            kvb = kv_s[krows, :].astype(BF16)
            kk, vv = kvb[:, :HEAD_DIM], kvb[:, HEAD_DIM:]
            do16 = _stack_heads([do_refs[j].at[0][qrows, :] for j in range(N_HALF)]).astype(BF16)
            lse = _stack_stats([lse_refs[j].at[0][qrows, :] for j in range(N_HALF)])
            delta = _stack_stats([dl_s[j][qrows, :] for j in range(N_HALF)])
            sc = lax.dot_general(qv, kk, (((1,), (1,)), ((), ())), preferred_element_type=F32)
            pr = jnp.exp(sc + bias_s[min(n, 1)] - lse)
            dv = lax.dot_general(pr.astype(BF16), do16, (((0,), (0,)), ((), ())), preferred_element_type=F32)
            dp = lax.dot_general(do16, vv, (((1,), (1,)), ((), ())), preferred_element_type=F32)
            ds = (pr * (dp - delta)).astype(BF16)
            dq = jnp.dot(ds, kk, preferred_element_type=F32)
            dk = lax.dot_general(ds, qv, (((0,), (0,)), ((), ())), preferred_element_type=F32)
            for j in range(N_HALF):
                dq_s[j][qrows, :] += _unstack_heads(dq, j)
            dkv_s[krows, :] += jnp.concatenate([dk, dv], axis=1)
        swap = _swap_matrix()
        cos_v, sin_v = tab_ref[0, :, :HEAD_DIM], tab_ref[0, :, HEAD_DIM:]
        for j in range(N_HALF):
            for e in range(2):
                cols = slice(e * HEAD_DIM, (e + 1) * HEAD_DIM)
                ocols = slice(j * HALF_W + e * HEAD_DIM, j * HALF_W + (e + 1) * HEAD_DIM)
                dq_ref[0, :, ocols] = _rope(dq_s[j][:, cols], cos_v, sin_v, swap, HEAD_DIM ** -0.5, True).astype(dq_ref.dtype)
        d_k = _rope(dkv_s[ATTN_PAD:ATTN_PAD + s_len, :HEAD_DIM], cos_v, sin_v, swap, 1.0, True)
        d_v = dkv_s[ATTN_PAD:ATTN_PAD + s_len, HEAD_DIM:]
        for h in range(N_KV_HEADS):
            @pl.when(pl.program_id(1) == h)
            def _():
                dkv_ref[0, :, h * HEAD_DIM:(h + 1) * HEAD_DIM] = d_k.astype(dkv_ref.dtype)
                dkv_ref[0, :, LANE + h * HEAD_DIM:LANE + (h + 1) * HEAD_DIM] = d_v.astype(dkv_ref.dtype)

    kv_out = pl.BlockSpec((1, s_len, 2 * LANE), lambda bi, h: (bi, 0, 0))
    kv_shape = jax.ShapeDtypeStruct((b, s_len, 2 * LANE), BF16)
    half_buf = pltpu.VMEM((s_len, HALF_W), F32)
    pad_buf = pltpu.VMEM((ATTN_PAD + s_len, HALF_W), F32)
    call = SideCopy(side, n_in=3 * N_HALF + 3, n_out=2, grid=(b, N_KV_HEADS))
    return pl.pallas_call(
        call.wrap(body), name=name, grid=(b, N_KV_HEADS),
        in_specs=q_specs + [kv_spec, t_spec, o_spec] + half_specs + half_specs + call.in_specs,
        out_specs=[o_spec, kv_out] + call.out_specs,
        out_shape=[jax.ShapeDtypeStruct((b, s_len, ATTN_WIDTH), BF16), kv_shape] + call.out_shape,
        scratch_shapes=[half_buf] * N_HALF + [pad_buf] + [half_buf] * (2 * N_HALF) + [pad_buf, _ATTN_BIAS_BUF] + call.scratch,
        compiler_params=pltpu.CompilerParams(dimension_semantics=("arbitrary", "arbitrary"), vmem_limit_bytes=ATTN_VMEM_LIMIT),
    )(*([proj3] * (N_HALF + 1)), rope_tab, attn3, *([lse3] * N_HALF), *([d_attn3] * N_HALF), *call.args)


CONV_TC = 256
CONV_COL0 = XBC_COL // CONV_TC


def _shift_down(u, s):
    if s == 0:
        return u
    rows = lax.broadcasted_iota(jnp.int32, u.shape, 0)
    return jnp.where(rows >= s, pltpu.roll(u, s, 0), 0.0)


def _shift_up(u, s):
    if s == 0:
        return u
    n = u.shape[0]
    rows = lax.broadcasted_iota(jnp.int32, u.shape, 0)
    return jnp.where(rows < n - s, pltpu.roll(u, n - s, 0), 0.0)


def conv_silu_fwd(proj3, w, bias, *, name):
    b, s, _ = proj3.shape
    u_spec = pl.BlockSpec((1, s, CONV_TC), lambda j, bi: (bi, 0, CONV_COL0 + j))
    o_spec = pl.BlockSpec((1, s, CONV_TC), lambda j, bi: (bi, 0, j))
    w_spec = pl.BlockSpec((CONV_WIDTH, CONV_TC), lambda j, bi: (0, j))
    b_spec = pl.BlockSpec((1, CONV_TC), lambda j, bi: (0, j))

    def body(u_ref, w_ref, b_ref, o_ref):
        u = u_ref[0]
        y = jnp.broadcast_to(b_ref[...], u.shape)
        for k in range(CONV_WIDTH):
            y = y + w_ref[k:k + 1, :] * _shift_down(u, CONV_WIDTH - 1 - k)
        o_ref[0] = y * jax.nn.sigmoid(y)

    return pl.pallas_call(
        body, name=name, grid=(CONV_CH // CONV_TC, b), in_specs=[u_spec, w_spec, b_spec], out_specs=o_spec,
        out_shape=jax.ShapeDtypeStruct((b, s, CONV_CH), F32),
        compiler_params=_params(("parallel", "arbitrary")),
    )(proj3, w, bias)


def conv_silu_bwd(proj3, w, bias, dact, *, name):
    b, s, _ = proj3.shape
    u_spec = pl.BlockSpec((1, s, CONV_TC), lambda j, bi: (bi, 0, CONV_COL0 + j))
    o_spec = pl.BlockSpec((1, s, CONV_TC), lambda j, bi: (bi, 0, j))
    w_spec = pl.BlockSpec((CONV_WIDTH, CONV_TC), lambda j, bi: (0, j))
    b_spec = pl.BlockSpec((1, CONV_TC), lambda j, bi: (0, j))

    def body(u_ref, w_ref, b_ref, g_ref, du_ref, dw_ref, db_ref):
        bi = pl.program_id(1)

        @pl.when(bi == 0)
        def _():
            dw_ref[...] = jnp.zeros_like(dw_ref)
            db_ref[...] = jnp.zeros_like(db_ref)

        u = u_ref[0]
        y = jnp.broadcast_to(b_ref[...], u.shape)
        shifted = [_shift_down(u, CONV_WIDTH - 1 - k) for k in range(CONV_WIDTH)]
        for k in range(CONV_WIDTH):
            y = y + w_ref[k:k + 1, :] * shifted[k]
        sig = jax.nn.sigmoid(y)
        dy = g_ref[0] * (sig * (1.0 + y * (1.0 - sig)))
        du = jnp.zeros_like(u)
        for k in range(CONV_WIDTH):
            du = du + w_ref[k:k + 1, :] * _shift_up(dy, CONV_WIDTH - 1 - k)
            dw_ref[k:k + 1, :] += jnp.sum(dy * shifted[k], axis=0, keepdims=True)
        du_ref[0] = du.astype(du_ref.dtype)
        db_ref[...] += jnp.sum(dy, axis=0, keepdims=True)

    return pl.pallas_call(
        body, name=name, grid=(CONV_CH // CONV_TC, b), in_specs=[u_spec, w_spec, b_spec, o_spec],
        out_specs=[o_spec, w_spec, b_spec],
        out_shape=[jax.ShapeDtypeStruct((b, s, CONV_CH), BF16), jax.ShapeDtypeStruct((CONV_WIDTH, CONV_CH), F32),
                   jax.ShapeDtypeStruct((1, CONV_CH), F32)],
        compiler_params=_params(("parallel", "arbitrary")),
    )(proj3, w, bias, dact)


def _softplus(z):
    e = jnp.exp(-jnp.abs(z))
    u = 1.0 + e
    log1p = jnp.where(u == 1.0, e, jnp.log(u) * e / jnp.where(u == 1.0, 1.0, u - 1.0))
    return jnp.maximum(z, 0.0) + log1p


def _tri(lower):
    r = lax.broadcasted_iota(jnp.int32, (CHUNK, CHUNK), 0)
    c = lax.broadcasted_iota(jnp.int32, (CHUNK, CHUNK), 1)
    return (r >= c) if lower else (r <= c)


def _ssd_common(dtr_ref, dtb_ref, alog_ref):
    z = dtr_ref[0] + dtb_ref[...]
    dt = _softplus(z)
    aneg = -jnp.exp(alog_ref[...])
    acs = _dot01_left(_tri(True).astype(BF16), dt * aneg)
    return z, dt, aneg, acs


def _col(mat, onehot):
    return jnp.sum(mat * onehot, axis=1, keepdims=True)


def _ssd_head(x, dt_j, acs_j, cb, tri_mask, last_row, acs_row=None):
    acs_last = jnp.sum(acs_j * last_row, axis=0, keepdims=True)
    xg = x * dt_j
    bc = jnp.broadcast_to(acs_j, (CHUNK, CHUNK))
    dm = bc - (bc.T if acs_row is None else jnp.broadcast_to(acs_row, (CHUNK, CHUNK)))
    lm = jnp.where(tri_mask, jnp.exp(jnp.where(tri_mask, dm, 0.0)), 0.0)
    mm = cb * lm
    decay_s = jnp.exp(acs_last - acs_j)
    return acs_last, xg, lm, mm, decay_s


def _ssd_specs(nc, reverse):
    cidx = (lambda c: nc - 1 - c) if reverse else (lambda c: c)
    act_spec = pl.BlockSpec((1, CHUNK, CONV_CH), lambda b, c: (b, cidx(c), 0))
    y_spec = pl.BlockSpec((1, CHUNK, SSM_INNER), lambda b, c: (b, cidx(c), 0))
    dt_in_spec = pl.BlockSpec((1, CHUNK, LANE), lambda b, c: (b, cidx(c), DT_COL // LANE))
    dt_out_spec = pl.BlockSpec((1, CHUNK, LANE), lambda b, c: (b, cidx(c), 0))
    par_spec = pl.BlockSpec((1, LANE), lambda b, c: (0, 0))
    h_spec = pl.BlockSpec((1, SSM_HEADS, 1, SSM_P, D_STATE), lambda b, c: (b, 0, cidx(c), 0, 0))
    return act_spec, y_spec, dt_in_spec, dt_out_spec, par_spec, h_spec


def _head_cols(h):
    return slice(h * SSM_P, (h + 1) * SSM_P)


def _group_cols(g, which):
    start = SSM_INNER + which * SSM_GROUPS * D_STATE + g * D_STATE
    return slice(start, start + D_STATE)


def ssd_fwd(act3, proj3, dtb, alog, dsk, *, name):
    b, s, _ = act3.shape
    nc = s // CHUNK
    act_spec, y_spec, dt_in_spec, _, par_spec, h_spec = _ssd_specs(nc, False)

    def body(act_ref, dtr_ref, dtb_ref, alog_ref, dsk_ref, y_ref, hp_ref, state):
        c = pl.program_id(1)

        @pl.when(c == 0)
        def _():
            state[...] = jnp.zeros_like(state)

        _, dt, _, acs = _ssd_common(dtr_ref, dtb_ref, alog_ref)
        acs_t = acs.T
        tri_mask = _tri(True)
        last_row = (lax.broadcasted_iota(jnp.int32, (CHUNK, 1), 0) == CHUNK - 1).astype(F32)
        for g in range(SSM_GROUPS):
            b16 = act_ref[0, :, _group_cols(g, 0)].astype(BF16)
            c16 = act_ref[0, :, _group_cols(g, 1)].astype(BF16)
            cb = lax.dot_general(c16, b16, (((1,), (1,)), ((), ())), preferred_element_type=F32)
            for j in range(HEADS_PER_GROUP):
                hidx = g * HEADS_PER_GROUP + j
                x = act_ref[0, :, _head_cols(hidx)]
                dt_j, acs_j = dt[:, hidx:hidx + 1], acs[:, hidx:hidx + 1]
                acs_last, xg, _, mm, decay_s = _ssd_head(x, dt_j, acs_j, cb, tri_mask, last_row, acs_t[hidx:hidx + 1, :])
                y_diag = jnp.dot(mm.astype(BF16), xg.astype(BF16), preferred_element_type=F32)
                st = lax.dot_general((xg * decay_s).astype(BF16), b16, (((0,), (0,)), ((), ())), preferred_element_type=F32)
                hp = state[hidx]
                hp_ref[0, hidx, 0] = hp
                y_off = lax.dot_general(c16, hp.astype(BF16), (((1,), (1,)), ((), ())), preferred_element_type=F32)
                d_j = dsk_ref[:, hidx:hidx + 1]
                y_ref[0, :, _head_cols(hidx)] = y_diag + y_off * jnp.exp(acs_j) + d_j * x
                state[hidx] = hp * jnp.exp(acs_last) + st

    return pl.pallas_call(
        body, name=name, grid=(b, nc),
        in_specs=[act_spec, dt_in_spec, par_spec, par_spec, par_spec],
        out_specs=[y_spec, h_spec],
        out_shape=[jax.ShapeDtypeStruct((b, s, SSM_INNER), F32),
                   jax.ShapeDtypeStruct((b, SSM_HEADS, nc, SSM_P, D_STATE), F32)],
        scratch_shapes=[pltpu.VMEM((SSM_HEADS, SSM_P, D_STATE), F32)],
        compiler_params=_params(("arbitrary", "arbitrary")),
    )(act3, proj3, dtb, alog, dsk)


def ssd_bwd(act3, proj3, dtb, alog, dsk, hprev, dy3, *, name):
    b, s, _ = act3.shape
    nc = s // CHUNK
    act_spec, y_spec, dt_in_spec, dt_out_spec, par_spec, h_spec = _ssd_specs(nc, True)
    dpar_spec = pl.BlockSpec((8, LANE), lambda bi, c: (0, 0))

    def body(act_ref, dtr_ref, dtb_ref, alog_ref, dsk_ref, hp_ref, dy_ref, dact_ref, ddtr_ref, dpar_ref, dstate):
        bi, c = pl.program_id(0), pl.program_id(1)

        @pl.when(c == 0)
        def _():
            dstate[...] = jnp.zeros_like(dstate)

        @pl.when((bi == 0) & (c == 0))
        def _():
            dpar_ref[...] = jnp.zeros_like(dpar_ref)

        z, dt, aneg, acs = _ssd_common(dtr_ref, dtb_ref, alog_ref)
        acs_t = acs.T
        tri_mask = _tri(True)
        last_row = (lax.broadcasted_iota(jnp.int32, (CHUNK, 1), 0) == CHUNK - 1).astype(F32)
        lanes = lax.broadcasted_iota(jnp.int32, (1, LANE), 1)
        sublanes = lax.broadcasted_iota(jnp.int32, (LANE, 1), 0)
        ddt_mat = jnp.zeros((CHUNK, LANE), F32)
        dacs_mat = jnp.zeros((CHUNK, LANE), F32)
        dacs_rows = jnp.zeros((LANE, CHUNK), F32)
        ddsk_row = jnp.zeros((1, LANE), F32)
        for g in range(SSM_GROUPS):
            b16 = act_ref[0, :, _group_cols(g, 0)].astype(BF16)
            c16 = act_ref[0, :, _group_cols(g, 1)].astype(BF16)
            cb = lax.dot_general(c16, b16, (((1,), (1,)), ((), ())), preferred_element_type=F32)
            dcb = jnp.zeros((CHUNK, CHUNK), F32)
            db_acc = jnp.zeros((CHUNK, D_STATE), F32)
            dc_acc = jnp.zeros((CHUNK, D_STATE), F32)
            for j in range(HEADS_PER_GROUP):
                hidx = g * HEADS_PER_GROUP + j
                onehot = (lanes == hidx).astype(F32)
                x = act_ref[0, :, _head_cols(hidx)]
                dt_j, acs_j = dt[:, hidx:hidx + 1], acs[:, hidx:hidx + 1]
                acs_last, xg, lm, mm, decay_s = _ssd_head(x, dt_j, acs_j, cb, tri_mask, last_row, acs_t[hidx:hidx + 1, :])
                ea = jnp.exp(acs_j)
                cd = jnp.exp(acs_last)
                d_j = dsk_ref[:, hidx:hidx + 1]
                hp = hp_ref[0, hidx, 0]
                hp16 = hp.astype(BF16)
                g_y = dy_ref[0, :, _head_cols(hidx)]
                g_y16 = g_y.astype(BF16)
                g_hn = dstate[hidx]
                g_hn16 = g_hn.astype(BF16)
                xg16 = xg.astype(BF16)
                ddsk_row = ddsk_row + jnp.sum(jnp.sum(g_y * x, axis=1, keepdims=True), axis=0, keepdims=True) * onehot
                d_mm = lax.dot_general(g_y16, xg16, (((1,), (1,)), ((), ())), preferred_element_type=F32)
                d_xg = lax.dot_general(mm.astype(BF16), g_y16, (((0,), (0,)), ((), ())), preferred_element_type=F32)
                dcb = dcb + d_mm * lm
                d_dm = d_mm * mm
                d_acs = jnp.sum(d_dm, axis=1, keepdims=True)
                dacs_rows = dacs_rows + (sublanes == hidx).astype(F32) * jnp.sum(d_dm, axis=0, keepdims=True)
                t_off = lax.dot_general(c16, hp16, (((1,), (1,)), ((), ())), preferred_element_type=F32)
                d_t16 = (g_y * ea).astype(BF16)
                d_acs = d_acs + jnp.sum(g_y * t_off, axis=1, keepdims=True) * ea
                dc_acc = dc_acc + jnp.dot(d_t16, hp16, preferred_element_type=F32)
                d_hp = lax.dot_general(d_t16, c16, (((0,), (0,)), ((), ())), preferred_element_type=F32) + g_hn * cd
                d_last = jnp.sum(jnp.sum(g_hn * hp, axis=1, keepdims=True), axis=0, keepdims=True) * cd
                d_w = lax.dot_general(b16, g_hn16, (((1,), (1,)), ((), ())), preferred_element_type=F32)
                db_acc = db_acc + jnp.dot((xg * decay_s).astype(BF16), g_hn16, preferred_element_type=F32)
                d_xg = d_xg + d_w * decay_s
                d_ds = jnp.sum(d_w * xg, axis=1, keepdims=True) * decay_s
                d_last = d_last + jnp.sum(d_ds, axis=0, keepdims=True)
                d_acs = d_acs - d_ds + d_last * last_row
                dact_ref[0, :, _head_cols(hidx)] = d_j * g_y + d_xg * dt_j
                ddt_mat = ddt_mat + jnp.sum(d_xg * x, axis=1, keepdims=True) * onehot
                dacs_mat = dacs_mat + d_acs * onehot
                dstate[hidx] = d_hp
            dcb16 = dcb.astype(BF16)
            dact_ref[0, :, _group_cols(g, 1)] = dc_acc + jnp.dot(dcb16, b16, preferred_element_type=F32)
            dact_ref[0, :, _group_cols(g, 0)] = db_acc + lax.dot_general(dcb16, c16, (((0,), (0,)), ((), ())),
                                                                         preferred_element_type=F32)
        d_a = _dot01_left(_tri(False).astype(BF16), dacs_mat - dacs_rows.T)
        ddt_mat = ddt_mat + d_a * aneg
        d_raw = ddt_mat * jax.nn.sigmoid(z)
        ddtr_ref[0] = d_raw
        dpar_ref[0:1, :] += jnp.sum(d_raw, axis=0, keepdims=True)
        dpar_ref[1:2, :] += jnp.sum(d_a * dt, axis=0, keepdims=True) * aneg
        dpar_ref[2:3, :] += ddsk_row

    return pl.pallas_call(
        body, name=name, grid=(b, nc),
        in_specs=[act_spec, dt_in_spec, par_spec, par_spec, par_spec, h_spec, y_spec],
        out_specs=[act_spec, dt_out_spec, dpar_spec],
        out_shape=[jax.ShapeDtypeStruct(act3.shape, F32), jax.ShapeDtypeStruct((b, s, LANE), F32),
                   jax.ShapeDtypeStruct((8, LANE), F32)],
        scratch_shapes=[pltpu.VMEM((SSM_HEADS, SSM_P, D_STATE), F32)],
        compiler_params=_params(("arbitrary", "arbitrary")),
    )(act3, proj3, dtb, alog, dsk, hprev, dy3)


def _unused_ssd_specs(nc, reverse):
    cidx = (lambda c: nc - 1 - c) if reverse else (lambda c: c)
    x_spec = pl.BlockSpec((1, HEADS_PER_GROUP, CHUNK, SSM_P), lambda b, c, g: (b, g, cidx(c), 0))
    bc_spec = pl.BlockSpec((1, 1, CHUNK, D_STATE), lambda b, c, g: (b, g, cidx(c), 0))
    dt_spec = pl.BlockSpec((1, CHUNK, LANE), lambda b, c, g: (b, cidx(c), 0))
    par_spec = pl.BlockSpec((1, LANE), lambda b, c, g: (0, 0))
    h_spec = pl.BlockSpec((1, HEADS_PER_GROUP, 1, SSM_P, D_STATE), lambda b, c, g: (b, g, cidx(c), 0, 0))
    return x_spec, bc_spec, dt_spec, par_spec, h_spec


def _unused_ssd_fwd(xs, bm, cm, dtr, dtb, alog, dsk, *, name):
    b, _, s, _ = xs.shape
    nc = s // CHUNK
    x_spec, bc_spec, dt_spec, par_spec, h_spec = _ssd_specs(nc, False)

    def body(x_ref, b_ref, c_ref, dtr_ref, dtb_ref, alog_ref, dsk_ref, y_ref, hp_ref, state):
        c, g = pl.program_id(1), pl.program_id(2)

        @pl.when(c == 0)
        def _():
            state[pl.ds(g * HEADS_PER_GROUP, HEADS_PER_GROUP)] = jnp.zeros((HEADS_PER_GROUP, SSM_P, D_STATE), F32)

        _, dt, _, acs = _ssd_common(dtr_ref, dtb_ref, alog_ref)
        b16, c16 = b_ref[0, 0].astype(BF16), c_ref[0, 0].astype(BF16)
        cb = lax.dot_general(c16, b16, (((1,), (1,)), ((), ())), preferred_element_type=F32)
        tri_mask = _tri(True)
        last_row = (lax.broadcasted_iota(jnp.int32, (CHUNK, 1), 0) == CHUNK - 1).astype(F32)
        lanes = lax.broadcasted_iota(jnp.int32, (1, LANE), 1)
        for j in range(HEADS_PER_GROUP):
            hidx = g * HEADS_PER_GROUP + j
            onehot = (lanes == hidx).astype(F32)
            x = x_ref[0, j]
            dt_j, acs_j = _col(dt, onehot), _col(acs, onehot)
            acs_last, xg, _, mm, decay_s = _ssd_head(x, dt_j, acs_j, cb, tri_mask, last_row)
            xg16 = xg.astype(BF16)
            y_diag = jnp.dot(mm.astype(BF16), xg16, preferred_element_type=F32)
            st = lax.dot_general((xg * decay_s).astype(BF16), b16, (((0,), (0,)), ((), ())), preferred_element_type=F32)
            hp = state[hidx]
            hp_ref[0, j, 0] = hp
            y_off = lax.dot_general(c16, hp.astype(BF16), (((1,), (1,)), ((), ())), preferred_element_type=F32)
            d_j = jnp.sum(dsk_ref[...] * onehot, axis=1, keepdims=True)
            y_ref[0, j] = y_diag + y_off * jnp.exp(acs_j) + d_j * x
            state[hidx] = hp * jnp.exp(acs_last) + st

    return pl.pallas_call(
        body, name=name, grid=(b, nc, SSM_GROUPS),
        in_specs=[x_spec, bc_spec, bc_spec, dt_spec, par_spec, par_spec, par_spec],
        out_specs=[x_spec, h_spec],
        out_shape=[jax.ShapeDtypeStruct(xs.shape, F32),
                   jax.ShapeDtypeStruct((b, SSM_HEADS, nc, SSM_P, D_STATE), F32)],
        scratch_shapes=[pltpu.VMEM((SSM_HEADS, SSM_P, D_STATE), F32)],
        compiler_params=_params(("arbitrary", "arbitrary", "arbitrary")),
    )(xs, bm, cm, dtr, dtb, alog, dsk)


def _unused_ssd_bwd(xs, bm, cm, dtr, dtb, alog, dsk, hprev, dy, *, name):
    b, _, s, _ = xs.shape
    nc = s // CHUNK
    x_spec, bc_spec, dt_spec, par_spec, h_spec = _ssd_specs(nc, True)
    dpar_spec = pl.BlockSpec((8, LANE), lambda bi, c, g: (0, 0))

    def body(x_ref, b_ref, c_ref, dtr_ref, dtb_ref, alog_ref, dsk_ref, hp_ref, dy_ref,
             dx_ref, db_ref, dc_ref, ddtr_ref, dpar_ref, dstate):
        bi, c, g = pl.program_id(0), pl.program_id(1), pl.program_id(2)

        @pl.when(c == 0)
        def _():
            dstate[pl.ds(g * HEADS_PER_GROUP, HEADS_PER_GROUP)] = jnp.zeros((HEADS_PER_GROUP, SSM_P, D_STATE), F32)

        @pl.when((bi == 0) & (c == 0) & (g == 0))
        def _():
            dpar_ref[...] = jnp.zeros_like(dpar_ref)

        z, dt, aneg, acs = _ssd_common(dtr_ref, dtb_ref, alog_ref)
        bv, cv = b_ref[0, 0], c_ref[0, 0]
        b16, c16 = bv.astype(BF16), cv.astype(BF16)
        cb = lax.dot_general(c16, b16, (((1,), (1,)), ((), ())), preferred_element_type=F32)
        tri_mask = _tri(True)
        last_row = (lax.broadcasted_iota(jnp.int32, (CHUNK, 1), 0) == CHUNK - 1).astype(F32)
        lanes = lax.broadcasted_iota(jnp.int32, (1, LANE), 1)
        dcb = jnp.zeros((CHUNK, CHUNK), F32)
        db_acc = jnp.zeros((CHUNK, D_STATE), F32)
        dc_acc = jnp.zeros((CHUNK, D_STATE), F32)
        ddt_mat = jnp.zeros((CHUNK, LANE), F32)
        dacs_mat = jnp.zeros((CHUNK, LANE), F32)
        ddsk_row = jnp.zeros((1, LANE), F32)
        for j in range(HEADS_PER_GROUP):
            hidx = g * HEADS_PER_GROUP + j
            onehot = (lanes == hidx).astype(F32)
            x = x_ref[0, j]
            dt_j, acs_j = _col(dt, onehot), _col(acs, onehot)
            acs_last, xg, lm, mm, decay_s = _ssd_head(x, dt_j, acs_j, cb, tri_mask, last_row)
            ea = jnp.exp(acs_j)
            cd = jnp.exp(acs_last)
            d_j = jnp.sum(dsk_ref[...] * onehot, axis=1, keepdims=True)
            hp = hp_ref[0, j, 0]
            hp16 = hp.astype(BF16)
            g_y = dy_ref[0, j]
            g_y16 = g_y.astype(BF16)
            g_hn = dstate[hidx]
            g_hn16 = g_hn.astype(BF16)
            xg16 = xg.astype(BF16)
            ddsk_row = ddsk_row + jnp.sum(jnp.sum(g_y * x, axis=1, keepdims=True), axis=0, keepdims=True) * onehot
            d_mm = lax.dot_general(g_y16, xg16, (((1,), (1,)), ((), ())), preferred_element_type=F32)
            d_xg = lax.dot_general(mm.astype(BF16), g_y16, (((0,), (0,)), ((), ())), preferred_element_type=F32)
            dcb = dcb + d_mm * lm
            d_dm = d_mm * mm
            d_acs = jnp.sum(d_dm, axis=1, keepdims=True) - jnp.sum(d_dm.T, axis=1, keepdims=True)
            t_off = lax.dot_general(c16, hp16, (((1,), (1,)), ((), ())), preferred_element_type=F32)
            d_t16 = (g_y * ea).astype(BF16)
            d_acs = d_acs + jnp.sum(g_y * t_off, axis=1, keepdims=True) * ea
            dc_acc = dc_acc + jnp.dot(d_t16, hp16, preferred_element_type=F32)
            d_hp = lax.dot_general(d_t16, c16, (((0,), (0,)), ((), ())), preferred_element_type=F32) + g_hn * cd
            d_last = jnp.sum(jnp.sum(g_hn * hp, axis=1, keepdims=True), axis=0, keepdims=True) * cd
            d_w = lax.dot_general(b16, g_hn16, (((1,), (1,)), ((), ())), preferred_element_type=F32)
            db_acc = db_acc + jnp.dot((xg * decay_s).astype(BF16), g_hn16, preferred_element_type=F32)
            d_xg = d_xg + d_w * decay_s
            d_ds = jnp.sum(d_w * xg, axis=1, keepdims=True) * decay_s
            d_last = d_last + jnp.sum(d_ds, axis=0, keepdims=True)
            d_acs = d_acs - d_ds + d_last * last_row
            dx_ref[0, j] = d_j * g_y + d_xg * dt_j
            ddt_mat = ddt_mat + jnp.sum(d_xg * x, axis=1, keepdims=True) * onehot
            dacs_mat = dacs_mat + d_acs * onehot
            dstate[hidx] = d_hp
        dcb16 = dcb.astype(BF16)
        dc_ref[0, 0] = dc_acc + jnp.dot(dcb16, b16, preferred_element_type=F32)
        db_ref[0, 0] = db_acc + lax.dot_general(dcb16, c16, (((0,), (0,)), ((), ())), preferred_element_type=F32)
        d_a = _dot01_left(_tri(False).astype(BF16), dacs_mat)
        ddt_mat = ddt_mat + d_a * aneg
        d_aneg = jnp.sum(d_a * dt, axis=0, keepdims=True)
        d_raw = ddt_mat * jax.nn.sigmoid(z)

        @pl.when(g == 0)
        def _():
            ddtr_ref[0] = d_raw

        @pl.when(g != 0)
        def _():
            ddtr_ref[0] += d_raw

        dpar_ref[0:1, :] += jnp.sum(d_raw, axis=0, keepdims=True)
        dpar_ref[1:2, :] += d_aneg * aneg
        dpar_ref[2:3, :] += ddsk_row

    return pl.pallas_call(
        body, name=name, grid=(b, nc, SSM_GROUPS),
        in_specs=[x_spec, bc_spec, bc_spec, dt_spec, par_spec, par_spec, par_spec, h_spec, x_spec],
        out_specs=[x_spec, bc_spec, bc_spec, dt_spec, dpar_spec],
        out_shape=[jax.ShapeDtypeStruct(xs.shape, F32), jax.ShapeDtypeStruct(bm.shape, F32),
                   jax.ShapeDtypeStruct(cm.shape, F32), jax.ShapeDtypeStruct(dtr.shape, F32),
                   jax.ShapeDtypeStruct((8, LANE), F32)],
        scratch_shapes=[pltpu.VMEM((SSM_HEADS, SSM_P, D_STATE), F32)],
        compiler_params=_params(("arbitrary", "arbitrary", "arbitrary")),
    )(xs, bm, cm, dtr, dtb, alog, dsk, hprev, dy)


SSD_INTERLEAVE = 8


def _each(f, *lists):
    return [f(*a) for a in zip(*lists)]


def _nt(a, b):
    return lax.dot_general(a, b, (((1,), (1,)), ((), ())), preferred_element_type=F32)


def _tn(a, b):
    return lax.dot_general(a, b, (((0,), (0,)), ((), ())), preferred_element_type=F32)


def _nn(a, b):
    return jnp.dot(a, b, preferred_element_type=F32)


def _rowsum(a):
    return jnp.sum(a, axis=1, keepdims=True)


def _colsum(a):
    return jnp.sum(a, axis=0, keepdims=True)


def _bf(a):
    return a.astype(BF16)


def _head_batches(g):
    first = g * HEADS_PER_GROUP
    return [list(range(first + k, first + k + SSD_INTERLEAVE)) for k in range(0, HEADS_PER_GROUP, SSD_INTERLEAVE)]


def _decay_matrix(acs_j, acs_row, tri_mask):
    dm = jnp.broadcast_to(acs_j, (CHUNK, CHUNK)) - jnp.broadcast_to(acs_row, (CHUNK, CHUNK))
    return jnp.where(tri_mask, jnp.exp(jnp.where(tri_mask, dm, 0.0)), 0.0)


def ssd_fwd(act3, proj3, dtb, alog, dsk, *, name, side=None):
    b, s, _ = act3.shape
    nc = s // CHUNK
    act_spec, y_spec, dt_in_spec, _, par_spec, h_spec = _ssd_specs(nc, False)

    def body(act_ref, dtr_ref, dtb_ref, alog_ref, dsk_ref, y_ref, hp_ref, state):
        c = pl.program_id(1)

        @pl.when(c == 0)
        def _():
            state[...] = jnp.zeros_like(state)

        _, dt, _, acs = _ssd_common(dtr_ref, dtb_ref, alog_ref)
        acs_t = acs.T
        tri_mask = _tri(True)
        last_row = (lax.broadcasted_iota(jnp.int32, (CHUNK, 1), 0) == CHUNK - 1).astype(F32)
        for g in range(SSM_GROUPS):
            b16 = _bf(act_ref[0, :, _group_cols(g, 0)])
            c16 = _bf(act_ref[0, :, _group_cols(g, 1)])
            cb = _nt(c16, b16)
            for hs in _head_batches(g):
                x = [act_ref[0, :, _head_cols(h)] for h in hs]
                dt_j = [dt[:, h:h + 1] for h in hs]
                acs_j = [acs[:, h:h + 1] for h in hs]
                acs_last = [_colsum(a * last_row) for a in acs_j]
                xg = _each(lambda xv, d: xv * d, x, dt_j)
                mm = [cb * _decay_matrix(a, acs_t[h:h + 1, :], tri_mask) for a, h in zip(acs_j, hs)]
                decay_s = _each(lambda al, a: jnp.exp(al - a), acs_last, acs_j)
                y_diag = _each(lambda m_, v: _nn(_bf(m_), _bf(v)), mm, xg)
                st = _each(lambda v, d: _tn(_bf(v * d), b16), xg, decay_s)
                hp = [state[h] for h in hs]
                for h, v in zip(hs, hp):
                    hp_ref[0, h, 0] = v
                y_off = [_nt(c16, _bf(v)) for v in hp]
                for h, yd, yo, a, xv in zip(hs, y_diag, y_off, acs_j, x):
                    y_ref[0, :, _head_cols(h)] = yd + yo * jnp.exp(a) + dsk_ref[:, h:h + 1] * xv
                for h, v, al, sv in zip(hs, hp, acs_last, st):
                    state[h] = v * jnp.exp(al) + sv

    call = SideCopy(side, n_in=5, n_out=2, grid=(b, nc))
    return pl.pallas_call(
        call.wrap(body), name=name, grid=(b, nc),
        in_specs=[act_spec, dt_in_spec, par_spec, par_spec, par_spec] + call.in_specs,
        out_specs=[y_spec, h_spec] + call.out_specs,
        out_shape=[jax.ShapeDtypeStruct((b, s, SSM_INNER), F32),
                   jax.ShapeDtypeStruct((b, SSM_HEADS, nc, SSM_P, D_STATE), F32)] + call.out_shape,
        scratch_shapes=[pltpu.VMEM((SSM_HEADS, SSM_P, D_STATE), F32)] + call.scratch,
        compiler_params=_params(("arbitrary", "arbitrary")),
    )(act3, proj3, dtb, alog, dsk, *call.args)


def ssd_bwd(act3, proj3, dtb, alog, dsk, hprev, dy3, *, name, side=None):
    b, s, _ = act3.shape
    nc = s // CHUNK
    act_spec, y_spec, dt_in_spec, dt_out_spec, par_spec, h_spec = _ssd_specs(nc, True)
    dpar_spec = pl.BlockSpec((8, LANE), lambda bi, c: (0, 0))

    def body(act_ref, dtr_ref, dtb_ref, alog_ref, dsk_ref, hp_ref, dy_ref, dact_ref, ddtr_ref, dpar_ref, dstate):
        bi, c = pl.program_id(0), pl.program_id(1)

        @pl.when(c == 0)
        def _():
            dstate[...] = jnp.zeros_like(dstate)

        @pl.when((bi == 0) & (c == 0))
        def _():
            dpar_ref[...] = jnp.zeros_like(dpar_ref)

        z, dt, aneg, acs = _ssd_common(dtr_ref, dtb_ref, alog_ref)
        acs_t = acs.T
        tri_mask = _tri(True)
        last_row = (lax.broadcasted_iota(jnp.int32, (CHUNK, 1), 0) == CHUNK - 1).astype(F32)
        lanes = lax.broadcasted_iota(jnp.int32, (1, LANE), 1)
        sublanes = lax.broadcasted_iota(jnp.int32, (LANE, 1), 0)
        ddt_mat = jnp.zeros((CHUNK, LANE), F32)
        dacs_mat = jnp.zeros((CHUNK, LANE), F32)
        dacs_rows = jnp.zeros((LANE, CHUNK), F32)
        ddsk_row = jnp.zeros((1, LANE), F32)
        for g in range(SSM_GROUPS):
            b16 = _bf(act_ref[0, :, _group_cols(g, 0)])
            c16 = _bf(act_ref[0, :, _group_cols(g, 1)])
            cb = _nt(c16, b16)
            dcb = jnp.zeros((CHUNK, CHUNK), F32)
            db_acc = jnp.zeros((CHUNK, D_STATE), F32)
            dc_acc = jnp.zeros((CHUNK, D_STATE), F32)
            for hs in _head_batches(g):
                x = [act_ref[0, :, _head_cols(h)] for h in hs]
                g_y = [dy_ref[0, :, _head_cols(h)] for h in hs]
                hp = [hp_ref[0, h, 0] for h in hs]
                g_hn = [dstate[h] for h in hs]
                dt_j = [dt[:, h:h + 1] for h in hs]
                acs_j = [acs[:, h:h + 1] for h in hs]
                acs_last = [_colsum(a * last_row) for a in acs_j]
                xg = _each(lambda xv, d: xv * d, x, dt_j)
                lm = [_decay_matrix(a, acs_t[h:h + 1, :], tri_mask) for a, h in zip(acs_j, hs)]
                mm = [cb * l for l in lm]
                decay_s = _each(lambda al, a: jnp.exp(al - a), acs_last, acs_j)
                ea = [jnp.exp(a) for a in acs_j]
                cd = [jnp.exp(al) for al in acs_last]
                g_y16, xg16, hp16, g_hn16 = [[_bf(v) for v in vs] for vs in (g_y, xg, hp, g_hn)]
                d_mm = _each(_nt, g_y16, xg16)
                d_xg = _each(lambda m_, gy: _tn(_bf(m_), gy), mm, g_y16)
                d_dm = _each(lambda a, m_: a * m_, d_mm, mm)
                d_acs = [_rowsum(v) for v in d_dm]
                t_off = [_nt(c16, v) for v in hp16]
                d_t16 = _each(lambda gy, e: _bf(gy * e), g_y, ea)
                d_acs = _each(lambda da, gy, t, e: da + _rowsum(gy * t) * e, d_acs, g_y, t_off, ea)
                d_hp = _each(lambda dtv, gh, cdv: _tn(dtv, c16) + gh * cdv, d_t16, g_hn, cd)
                d_w = [_nt(b16, v) for v in g_hn16]
                d_xg = _each(lambda dx, dw, ds: dx + dw * ds, d_xg, d_w, decay_s)
                d_ds = _each(lambda dw, v, ds: _rowsum(dw * v) * ds, d_w, xg, decay_s)
                d_last = _each(lambda gh, hv, cdv, dd: _colsum(_rowsum(gh * hv)) * cdv + _colsum(dd), g_hn, hp, cd, d_ds)
                d_acs = _each(lambda da, dd, dl: da - dd + dl * last_row, d_acs, d_ds, d_last)
                for h, gy, dx, d, xv in zip(hs, g_y, d_xg, dt_j, x):
                    dact_ref[0, :, _head_cols(h)] = dsk_ref[:, h:h + 1] * gy + dx * d
                for h, v in zip(hs, d_hp):
                    dstate[h] = v
                for k, h in enumerate(hs):
                    onehot = (lanes == h).astype(F32)
                    dcb = dcb + d_mm[k] * lm[k]
                    dc_acc = dc_acc + _nn(d_t16[k], hp16[k])
                    db_acc = db_acc + _nn(_bf(xg[k] * decay_s[k]), g_hn16[k])
                    ddsk_row = ddsk_row + _colsum(_rowsum(g_y[k] * x[k])) * onehot
                    ddt_mat = ddt_mat + _rowsum(d_xg[k] * x[k]) * onehot
                    dacs_mat = dacs_mat + d_acs[k] * onehot
                    dacs_rows = dacs_rows + (sublanes == h).astype(F32) * _colsum(d_dm[k])
            dcb16 = _bf(dcb)
            dact_ref[0, :, _group_cols(g, 1)] = dc_acc + _nn(dcb16, b16)
            dact_ref[0, :, _group_cols(g, 0)] = db_acc + _tn(dcb16, c16)
        d_a = _dot01_left(_tri(False).astype(BF16), dacs_mat - dacs_rows.T)
        ddt_mat = ddt_mat + d_a * aneg
        d_raw = ddt_mat * jax.nn.sigmoid(z)
        ddtr_ref[0] = d_raw
        dpar_ref[0:1, :] += _colsum(d_raw)
        dpar_ref[1:2, :] += _colsum(d_a * dt) * aneg
        dpar_ref[2:3, :] += ddsk_row

    call = SideCopy(side, n_in=7, n_out=3, grid=(b, nc))
    return pl.pallas_call(
        call.wrap(body), name=name, grid=(b, nc),
        in_specs=[act_spec, dt_in_spec, par_spec, par_spec, par_spec, h_spec, y_spec] + call.in_specs,
        out_specs=[act_spec, dt_out_spec, dpar_spec] + call.out_specs,
        out_shape=[jax.ShapeDtypeStruct(act3.shape, F32), jax.ShapeDtypeStruct((b, s, LANE), F32),
                   jax.ShapeDtypeStruct((8, LANE), F32)] + call.out_shape,
        scratch_shapes=[pltpu.VMEM((SSM_HEADS, SSM_P, D_STATE), F32)] + call.scratch,
        compiler_params=_params(("arbitrary", "arbitrary")),
    )(act3, proj3, dtb, alog, dsk, hprev, dy3, *call.args)


def to_heads(x, b, s, h):
    return x.reshape(b, s, h, -1).transpose(0, 2, 1, 3)


def from_heads(x):
    b, h, s, c = x.shape
    return x.transpose(0, 2, 1, 3).reshape(b * s, h * c)


def dilate_q(q, d):
    b, _, s, c = q.shape
    x = q.reshape(b, N_KV_HEADS, GQA, s // d, d, c).transpose(0, 1, 4, 2, 3, 5)
    return x.reshape(b * N_KV_HEADS * d, GQA, s // d, c)


def undilate_q(x, b, d):
    _, _, l, c = x.shape
    y = x.reshape(b, N_KV_HEADS, d, GQA, l, c).transpose(0, 1, 3, 4, 2, 5)
    return y.reshape(b, N_Q_HEADS, l * d, c)


def dilate_kv(k, d):
    b, h, s, c = k.shape
    return k.reshape(b, h, s // d, d, c).transpose(0, 1, 3, 2, 4).reshape(b * h * d, s // d, c)


def undilate_kv(x, b, d):
    _, l, c = x.shape
    return x.reshape(b, N_KV_HEADS, d, l, c).transpose(0, 1, 3, 2, 4).reshape(b, N_KV_HEADS, l * d, c)


def rotary_tables(positions):
    inv_freq = ROPE_THETA ** (-jnp.arange(0, ROPE_DIM, 2, dtype=F32) / ROPE_DIM)
    ang = positions.astype(F32)[..., None] * inv_freq
    cos, sin = jnp.cos(ang), jnp.sin(ang)
    rest = HEAD_DIM - ROPE_DIM
    cosf = jnp.concatenate([cos, cos, jnp.ones(cos.shape[:2] + (rest,), F32)], axis=-1)
    sinf = jnp.concatenate([-sin, sin, jnp.zeros(sin.shape[:2] + (rest,), F32)], axis=-1)
    return cosf, sinf


def w_in_columns(w):
    pad = jnp.zeros((w.shape[0], IN_PAD - IN_PROJ), w.dtype)
    return jnp.concatenate([w[:, :Q_END], w[:, V_END:XBC_END], w[:, Q_END:V_END], w[:, XBC_END:], pad], axis=1)


def w_in_grad_columns(g):
    return jnp.concatenate([g[:, :Z_COL], g[:, K_COL:DT_COL], g[:, Z_COL:K_COL], g[:, DT_COL:DT_COL + SSM_HEADS]], axis=1)


def lane_pad(v):
    return jnp.pad(v.reshape(1, -1), ((0, 0), (0, LANE - v.shape[-1])))


def layer_fwd(h, wts, small, rope_tab, b, s, tag, attn_side=None, rest_from=None, ssd_side=None):
    w_in = wts[0]
    t = b * s
    sv = {"h": h}
    hn = rowwise_fwd(rms_fn, [h], [small["norm_mix"]], [BF16], name=f"rms_mix_{tag}")[0]
    proj = matmul(hn, w_in, name=f"in_proj_{tag}")
    sv["hn"], sv["proj"] = hn, proj
    proj3 = proj.reshape(b, s, IN_PAD)
    attn3, lse3, *attn_out = attn_fwd(proj3, rope_tab, name=f"attn_{tag}", side=attn_side)
    if rest_from is not None:
        wts = (w_in,) + tuple(rest_from(attn_out))
    _, w_out, w_gate, w_up, w_down = wts
    sv["attn3"], sv["lse3"] = attn3, lse3
    attn = attn3.reshape(t, ATTN_WIDTH)
    act3 = conv_silu_fwd(proj3, small["conv_w"], small["conv_b"], name=f"conv_{tag}")
    y3, hprev, *ssd_out = ssd_fwd(act3, proj3, small["dt_bias"], small["a_log"], small["d_skip"], name=f"ssd_{tag}",
                                  side=ssd_side)
    y = y3.reshape(t, SSM_INNER)
    sv["act3"], sv["hprev"], sv["y"] = act3, hprev, y
    gn = rowwise_fwd(gated_norm_fn, [y, proj], [small["ssm_norm"]], [BF16], name=f"gated_norm_{tag}", groups=SSM_GROUPS,
                     windows=[None, (Z_COL, SSM_INNER)])[0]
    sv["gn"] = gn
    h1 = matmul([attn, gn], w_out, name=f"out_proj_{tag}", residual=h)
    sv["h1"] = h1
    hn2 = rowwise_fwd(rms_fn, [h1], [small["norm_ffn"]], [BF16], name=f"rms_ffn_{tag}")[0]
    gate = matmul(hn2, w_gate, out_dtype=BF16, name=f"ffn_gate_{tag}")
    up = matmul(hn2, w_up, out_dtype=BF16, name=f"ffn_up_{tag}")
    act2 = rowwise_fwd(swiglu_fn, [gate, up], [], [BF16], name=f"swiglu_{tag}")[0]
    sv["hn2"], sv["gate"], sv["up"], sv["act2"] = hn2, gate, up, act2
    h2 = matmul(act2, w_down, name=f"ffn_down_{tag}", residual=h1)
    return h2, sv, wts, (ssd_out or None)


def layer_bwd(dh2, sv, wts, small, rope_tab, b, s, tag, ssd_side=None, attn_side_fn=None):
    w_in, w_out, w_gate, w_up, w_down = wts
    t = b * s
    gr = {}
    d_act2 = matmul(dh2, w_down, tb=True, out_dtype=BF16, name=f"ffn_down_dx_{tag}")
    gr["w_down"] = matmul(sv["act2"], dh2, ta=True, out_dtype=BF16, name=f"ffn_down_dw_{tag}")
    d_gate, d_up = rowwise_bwd(swiglu_fn, [sv["gate"], sv["up"]], [], [d_act2], [BF16, BF16], name=f"swiglu_bwd_{tag}")
    gr["w_gate"] = matmul(sv["hn2"], d_gate, ta=True, out_dtype=BF16, name=f"ffn_gate_dw_{tag}")
    gr["w_up"] = matmul(sv["hn2"], d_up, ta=True, out_dtype=BF16, name=f"ffn_up_dw_{tag}")
    d_hn2 = matmul(d_gate, w_gate, tb=True, name=f"ffn_gate_dx_{tag}")
    d_hn2 = matmul(d_up, w_up, tb=True, residual=d_hn2, name=f"ffn_up_dx_{tag}")
    dh1, gr["norm_ffn"] = rowwise_bwd(rms_fn, [sv["h1"]], [small["norm_ffn"]], [d_hn2], [F32],
                                      name=f"rms_ffn_bwd_{tag}", add_to_first=dh2)
    d_cat = matmul(dh1, w_out, tb=True, name=f"out_proj_dx_{tag}")
    gr["w_out"] = jnp.concatenate([
        matmul(sv["attn3"].reshape(t, ATTN_WIDTH), dh1, ta=True, out_dtype=BF16, name=f"out_proj_dw_attn_{tag}"),
        matmul(sv["gn"], dh1, ta=True, out_dtype=BF16, name=f"out_proj_dw_ssd_{tag}")], axis=0)
    d_y, d_z, gr["ssm_norm"] = rowwise_bwd(gated_norm_fn, [sv["y"], sv["proj"]], [small["ssm_norm"]], [d_cat], [F32, BF16],
                                           name=f"gated_norm_bwd_{tag}", groups=SSM_GROUPS,
                                           windows=[None, (Z_COL, SSM_INNER)], ct_windows=[(ATTN_WIDTH, SSM_INNER)])
    proj3 = sv["proj"].reshape(b, s, IN_PAD)
    d_act3, d_dtr, d_par, *ssd_out = ssd_bwd(sv["act3"], proj3, small["dt_bias"], small["a_log"], small["d_skip"],
                                             sv["hprev"], d_y.reshape(b, s, SSM_INNER), name=f"ssd_bwd_{tag}", side=ssd_side)
    gr["dt_bias"], gr["a_log"], gr["d_skip"] = d_par[0, :SSM_HEADS], d_par[1, :SSM_HEADS], d_par[2, :SSM_HEADS]
    d_xbc, gr["conv_w"], gr["conv_b"] = conv_silu_bwd(proj3, small["conv_w"], small["conv_b"], d_act3,
                                                      name=f"conv_bwd_{tag}")
    attn_side = attn_side_fn(gr) if attn_side_fn is not None else None
    d_q3, d_kv3, *attn_out = attn_bwd(proj3, rope_tab, sv["attn3"], sv["lse3"], d_cat.reshape(b, s, MIX_WIDTH),
                                      name=f"attn_bwd_{tag}", side=attn_side)
    d_proj = [d_q3.reshape(t, ATTN_WIDTH), d_z, d_xbc.reshape(t, CONV_CH), d_kv3.reshape(t, 2 * LANE),
              d_dtr.reshape(t, LANE)]
    d_hn = matmul(d_proj, w_in, tb=True, name=f"in_proj_dx_{tag}")
    gr["w_in"] = w_in_grad_columns(jnp.concatenate(
        [matmul(sv["hn"], part, ta=True, out_dtype=BF16, name=f"in_proj_dw_{k}_{tag}") for k, part in enumerate(d_proj)],
        axis=1))
    dh, gr["norm_mix"] = rowwise_bwd(rms_fn, [sv["h"]], [small["norm_mix"]], [d_hn], [F32],
                                     name=f"rms_mix_bwd_{tag}", add_to_first=dh1)
    return dh, gr, (ssd_out or None), (attn_out or None)


def local_step(x, positions, big, small_all, final_norm, loss_target, *, plan=None):
    b, s, _ = x.shape
    t = b * s
    rope_tab = jnp.concatenate(rotary_tables(positions), axis=-1)
    h = x.reshape(t, D_MODEL)
    saved, big = [], list(big)
    for l in range(DEPTH):
        kw = {}
        if plan is not None and l == 0:
            kw = dict(attn_side=(plan["rest0"], False), rest_from=plan["make_rest0"], ssd_side=(plan["late"], False))
        h, sv, big[l], got = layer_fwd(h, big[l], small_all[l], rope_tab, b, s, f"l{l}", **kw)
        if got is not None:
            big[DEPTH - 1] = plan["make_late"](got)
        saved.append(sv)
    dh, d_final, loss = loss_and_grad(h, loss_target.reshape(t, D_MODEL), final_norm.reshape(1, D_MODEL))
    grads, received = [None] * DEPTH, {}
    for l in reversed(range(DEPTH)):
        kw = {}
        if plan is not None and l == 0:
            kw = dict(ssd_side=(plan["grads_late"](grads[DEPTH - 1]), True),
                      attn_side_fn=lambda gr: (plan["grads_rest0"](gr), True))
        dh, grads[l], got_ssd, got_attn = layer_bwd(dh, saved[l], big[l], small_all[l], rope_tab, b, s, f"l{l}", **kw)
        if got_ssd is not None:
            received["late"] = got_ssd
        if got_attn is not None:
            received["rest0"] = got_attn
    return loss, dh.reshape(b, s, D_MODEL), grads, d_final, received


def _slab_rows(r):
    return r if r <= 512 else _pick(r, (512, 352, 256, 128, 8))


def cast_bf16(x, *, name):
    def fn(v):
        return (v,)
    return rowwise_fwd(fn, [x], [], [BF16], name=name, tr=_slab_rows(x.shape[0]))[0]


def sum_slots(x, *, name):
    n, r, c = x.shape
    tr = _slab_rows(r)

    def body(x_ref, o_ref):
        acc = x_ref[0].astype(F32)
        for i in range(1, n):
            acc = acc + x_ref[i].astype(F32)
        o_ref[...] = acc

    return pl.pallas_call(
        body, name=name, grid=(r // tr,), in_specs=[pl.BlockSpec((n, tr, c), lambda i: (0, i, 0))],
        out_specs=pl.BlockSpec((tr, c), lambda i: (i, 0)), out_shape=jax.ShapeDtypeStruct((r, c), F32),
        compiler_params=_params(("parallel",)),
    )(x)


def adamw(g_parts, w, m, v, *, name, with_grad=True):
    r, c = w.shape
    tr = _slab_rows(r)
    n_g = len(g_parts)
    n_out = 4 if with_grad else 3
    bc1 = 1.0 / (1.0 - ADAM_B1 ** ADAM_STEP)
    bc2 = 1.0 / (1.0 - ADAM_B2 ** ADAM_STEP)

    def body(*refs):
        g = refs[0][...]
        for r_ in refs[1:n_g]:
            g = g + r_[...]
        w_ref, m_ref, v_ref = refs[n_g:n_g + 3]
        d_out, m_out, v_out = refs[-3:]
        m_new = ADAM_B1 * m_ref[...] + (1.0 - ADAM_B1) * g
        v_new = ADAM_B2 * v_ref[...] + (1.0 - ADAM_B2) * (g * g)
        if with_grad:
            refs[n_g + 3][...] = g
        m_out[...] = m_new
        v_out[...] = v_new
        d_out[...] = -ADAM_LR * ((m_new * bc1) / (jnp.sqrt(v_new * bc2) + ADAM_EPS) + ADAM_WD * w_ref[...])

    spec = pl.BlockSpec((tr, c), lambda i: (i, 0))
    return pl.pallas_call(
        body, name=name, grid=(r // tr,), in_specs=[spec] * (n_g + 3), out_specs=[spec] * n_out,
        out_shape=[jax.ShapeDtypeStruct((r, c), F32)] * n_out, compiler_params=_params(("parallel",)),
    )(*g_parts, w, m, v)


def _other_chips(x, y):
    return [(1 - x, y), (x, 1 - y), (1 - x, 1 - y)]


def allgather_chips(shards):
    n_arr = len(shards)

    def body(*refs):
        in_refs, out_refs = refs[:n_arr], refs[n_arr:2 * n_arr]
        send_sems, recv_sems, local_sems = refs[2 * n_arr:]
        x, y, c = lax.axis_index("x"), lax.axis_index("y"), lax.axis_index("c")
        chip = 2 * x + y
        started = []
        for a, (in_ref, out_ref) in enumerate(zip(in_refs, out_refs)):
            mine = pltpu.make_async_copy(in_ref, out_ref.at[chip], local_sems.at[a])
            mine.start()
            started.append(mine.wait)
            for k, (px, py) in enumerate(_other_chips(x, y)):
                cp = pltpu.make_async_remote_copy(src_ref=in_ref, dst_ref=out_ref.at[chip], send_sem=send_sems.at[3 * a + k],
                                                  recv_sem=recv_sems.at[3 * a + k], device_id=(px, py, c), device_id_type=MESH)
                cp.start()
                started.append(cp.wait_send)
        for a, (in_ref, out_ref) in enumerate(zip(in_refs, out_refs)):
            for k, (px, py) in enumerate(_other_chips(x, y)):
                pltpu.make_async_remote_copy(src_ref=in_ref, dst_ref=out_ref.at[2 * px + py], send_sem=send_sems.at[3 * a + k],
                                             recv_sem=recv_sems.at[3 * a + k], device_id=(px, py, c),
                                             device_id_type=MESH).wait_recv()
        for wait in started:
            wait()

    hbm = pl.BlockSpec(memory_space=pltpu.HBM)
    return pl.pallas_call(
        body, name="allgather_weights", in_specs=[hbm] * n_arr, out_specs=[hbm] * n_arr,
        out_shape=[jax.ShapeDtypeStruct((N_CHIPS,) + s.shape, s.dtype) for s in shards],
        scratch_shapes=[pltpu.SemaphoreType.DMA((3 * n_arr,)), pltpu.SemaphoreType.DMA((3 * n_arr,)),
                        pltpu.SemaphoreType.DMA((n_arr,))],
    )(*shards)


def exchange_grads(big, small):
    def body(big_ref, small_ref, big_out, small_out, send_sems, recv_sems, local_sems):
        x, y, c = lax.axis_index("x"), lax.axis_index("y"), lax.axis_index("c")
        chip = 2 * x + y
        dev = 4 * x + 2 * y + c
        own_big = pltpu.make_async_copy(big_ref.at[chip], big_out.at[chip], local_sems.at[0])
        own_small = pltpu.make_async_copy(small_ref, small_out.at[dev], local_sems.at[1])
        own_big.start()
        own_small.start()
        sends = []
        for k, (px, py) in enumerate(_other_chips(x, y)):
            cp = pltpu.make_async_remote_copy(src_ref=big_ref.at[2 * px + py], dst_ref=big_out.at[chip],
                                              send_sem=send_sems.at[k], recv_sem=recv_sems.at[k],
                                              device_id=(px, py, c), device_id_type=MESH)
            cp.start()
            sends.append(cp)
        peers = []
        for r in range(1, N_DEV):
            fx, fy, fc = (r >> 2) & 1, (r >> 1) & 1, r & 1
            px, py, pc = (x + fx) % 2, (y + fy) % 2, (c + fc) % 2
            peers.append((px, py, pc))
            cp = pltpu.make_async_remote_copy(src_ref=small_ref, dst_ref=small_out.at[dev], send_sem=send_sems.at[2 + r],
                                              recv_sem=recv_sems.at[2 + r], device_id=(px, py, pc), device_id_type=MESH)
            cp.start()
            sends.append(cp)
        for k, (px, py) in enumerate(_other_chips(x, y)):
            pltpu.make_async_remote_copy(src_ref=big_ref.at[chip], dst_ref=big_out.at[2 * px + py],
                                         send_sem=send_sems.at[k], recv_sem=recv_sems.at[k],
                                         device_id=(px, py, c), device_id_type=MESH).wait_recv()
        for r, (px, py, pc) in zip(range(1, N_DEV), peers):
            pltpu.make_async_remote_copy(src_ref=small_ref, dst_ref=small_out.at[4 * px + 2 * py + pc],
                                         send_sem=send_sems.at[2 + r], recv_sem=recv_sems.at[2 + r],
                                         device_id=(px, py, pc), device_id_type=MESH).wait_recv()
        for cp in sends:
            cp.wait_send()
        own_big.wait()
        own_small.wait()

    hbm = pl.BlockSpec(memory_space=pltpu.HBM)
    n_sem = 3 + N_DEV - 1
    return pl.pallas_call(
        body, name="exchange_grads", in_specs=[hbm, hbm], out_specs=[hbm, hbm],
        out_shape=[jax.ShapeDtypeStruct(big.shape, big.dtype), jax.ShapeDtypeStruct((N_DEV,) + small.shape, small.dtype)],
        scratch_shapes=[pltpu.SemaphoreType.DMA((n_sem,)), pltpu.SemaphoreType.DMA((n_sem,)), pltpu.SemaphoreType.DMA((2,))],
    )(big, small)


SWAP_CHUNKS = 28


def swap_cores(mine):
    rows = mine.shape[0] // SWAP_CHUNKS
    assert rows * SWAP_CHUNKS == mine.shape[0] and rows % 8 == 0

    def body(in_ref, out_ref, send_sems, recv_sems):
        x, y, c = lax.axis_index("x"), lax.axis_index("y"), lax.axis_index("c")

        def chunk(k):
            part = pl.ds(k * rows, rows)
            return pltpu.make_async_remote_copy(src_ref=in_ref.at[part], dst_ref=out_ref.at[part],
                                                send_sem=send_sems.at[k], recv_sem=recv_sems.at[k],
                                                device_id=(x, y, 1 - c), device_id_type=MESH)

        for k in range(SWAP_CHUNKS):
            chunk(k).start()
        for k in range(SWAP_CHUNKS):
            chunk(k).wait_recv()
        for k in range(SWAP_CHUNKS):
            chunk(k).wait_send()

    hbm = pl.BlockSpec(memory_space=pltpu.HBM)
    return pl.pallas_call(
        body, name="swap_cores", in_specs=[hbm], out_specs=hbm,
        out_shape=jax.ShapeDtypeStruct(mine.shape, mine.dtype),
        scratch_shapes=[pltpu.SemaphoreType.DMA((SWAP_CHUNKS,)), pltpu.SemaphoreType.DMA((SWAP_CHUNKS,))],
    )(mine)


BIG_NAMES = ("w_in", "w_out", "w_gate", "w_up", "w_down")
BIG_SHARD_AXIS = {"w_in": 1, "w_out": 0, "w_gate": 1, "w_up": 1, "w_down": 0}
PACK_COLS = 1024
SMALL_NAMES = ("norm_mix", "conv_w", "conv_b", "dt_bias", "a_log", "d_skip", "ssm_norm", "norm_ffn")


PACK_ROW_TILE = 256


def pack_big(shards, names=BIG_NAMES):
    flat = jnp.concatenate([shards[n].reshape(-1) for n in names])
    unit = PACK_ROW_TILE * PACK_COLS
    total = -(-flat.size // unit) * unit
    return jnp.pad(flat, (0, total - flat.size)).reshape(-1, PACK_COLS)


def unpack_big(packed, like, names=BIG_NAMES):
    out, off = {}, 0
    flat = packed.reshape(-1)
    for n in names:
        size = like[n].size
        out[n] = flat[off:off + size].reshape(like[n].shape)
        off += size
    return out


def pack_small(parts):
    flat = jnp.concatenate([p.reshape(-1).astype(F32) for p in parts])
    rows = -(-flat.size // LANE)
    rows = -(-rows // 8) * 8
    return jnp.pad(flat, (0, rows * LANE - flat.size)).reshape(rows, LANE)


def unpack_small(packed, like):
    out, off = [], 0
    flat = packed.reshape(-1)
    for a in like:
        out.append(flat[off:off + a.size].reshape(a.shape))
        off += a.size
    return out


def _unused_kernel_packed(x, positions, norm_mix, w_in, conv_w, conv_b, dt_bias, a_log, d_skip, ssm_norm, w_out, norm_ffn, w_gate, w_up, w_down, final_norm, loss_target, m_norm_mix, m_w_in, m_conv_w, m_conv_b, m_dt_bias, m_a_log, m_d_skip, m_ssm_norm, m_w_out, m_norm_ffn, m_w_gate, m_w_up, m_w_down, m_final_norm, v_norm_mix, v_w_in, v_conv_w, v_conv_b, v_dt_bias, v_a_log, v_d_skip, v_ssm_norm, v_w_out, v_norm_ffn, v_w_gate, v_w_up, v_w_down, v_final_norm):
    chip = 2 * lax.axis_index("x") + lax.axis_index("y")
    w_sh = {"w_in": w_in, "w_out": w_out, "w_gate": w_gate, "w_up": w_up, "w_down": w_down}
    m_sh = {"w_in": m_w_in, "w_out": m_w_out, "w_gate": m_w_gate, "w_up": m_w_up, "w_down": m_w_down}
    v_sh = {"w_in": v_w_in, "w_out": v_w_out, "w_gate": v_w_gate, "w_up": v_w_up, "w_down": v_w_down}

    assert DEPTH == 2
    first, rest = BIG_NAMES[:1], BIG_NAMES[1:]
    layer_of = lambda d, l: {n: d[n][l] for n in BIG_NAMES}
    pack_layer = lambda d: jnp.concatenate([pack_big(d, first), pack_big(d, rest)])
    pack_layers = lambda d: jnp.concatenate([pack_layer(layer_of(d, l)) for l in range(DEPTH)])
    first_rows = pack_big(layer_of(w_sh, 0), first).shape[0]
    layer_rows = pack_layer(layer_of(w_sh, 0)).shape[0]

    def unpack_layer(packed, l):
        like = layer_of(w_sh, l)
        return {**unpack_big(packed[:first_rows], like, first), **unpack_big(packed[first_rows:], like, rest)}

    def unpack_layers(packed):
        per_layer = [unpack_layer(packed[l * layer_rows:(l + 1) * layer_rows], l) for l in range(DEPTH)]
        return {n: jnp.stack([p[n] for p in per_layer]) for n in BIG_NAMES}

    def full_weights(gathered, l, names, unpack):
        pieces = [unpack(gathered[j]) for j in range(N_CHIPS)]
        full = {n: jnp.concatenate([p[n] for p in pieces], axis=BIG_SHARD_AXIS[n]) for n in names}
        return tuple(w_in_columns(full[n]) if n == "w_in" else full[n] for n in names)

    w_packed16 = pack_layers({n: cast_bf16(w_sh[n].reshape(-1, w_sh[n].shape[-1]), name=f"cast_{n}").reshape(w_sh[n].shape)
                              for n in BIG_NAMES})
    conv_cols = CONV_CH // N_CHIPS
    gathered_in0, conv_g = allgather_chips([w_packed16[:first_rows], conv_w.reshape(-1, LANE)])
    big = [full_weights(gathered_in0, 0, first, lambda p: unpack_big(p, layer_of(w_sh, 0), first)) + (None,) * len(rest), None]
    plan = {
        "rest0": w_packed16[first_rows:layer_rows],
        "make_rest0": lambda g: full_weights(g, 0, rest, lambda p: unpack_big(p, layer_of(w_sh, 0), rest)),
        "late": w_packed16[layer_rows:],
        "make_late": lambda g: full_weights(g, DEPTH - 1, BIG_NAMES, lambda p: unpack_layer(p, DEPTH - 1)),
    }
    conv_w_full = jnp.concatenate([conv_g[j].reshape(DEPTH, CONV_WIDTH, conv_cols) for j in range(N_CHIPS)], axis=2)

    small_all = []
    for l in range(DEPTH):
        small_all.append({
            "norm_mix": norm_mix[l].reshape(1, -1), "conv_w": conv_w_full[l], "conv_b": conv_b[l].reshape(1, -1),
            "dt_bias": lane_pad(dt_bias[l]), "a_log": lane_pad(a_log[l]), "d_skip": lane_pad(d_skip[l]),
            "ssm_norm": ssm_norm[l].reshape(1, -1), "norm_ffn": norm_ffn[l].reshape(1, -1)})

    def shard_of(name, g, j):
        n = g.shape[BIG_SHARD_AXIS[name]] // N_CHIPS
        return lax.slice_in_dim(g, j * n, (j + 1) * n, axis=BIG_SHARD_AXIS[name])

    def per_chip(layer_grads, names):
        packs = [[pack_big({n: shard_of(n, layer_grads[n], j) for n in group}, group) for j in range(N_CHIPS)]
                 for group in ((first, rest) if names == BIG_NAMES else (names,))]
        return jnp.stack([jnp.concatenate([p[j] for p in packs]) for j in range(N_CHIPS)])

    plan["grads_late"] = lambda gr: per_chip(gr, BIG_NAMES)
    plan["grads_rest0"] = lambda gr: per_chip(gr, rest)
    loss_part, grad_x, grads, d_final, received = local_step(x, positions, big, small_all, final_norm, loss_target, plan=plan)

    small_parts = [jnp.stack([grads[l][n].reshape(-1) for l in range(DEPTH)]) for n in SMALL_NAMES]
    small_parts += [d_final.reshape(-1), loss_part.reshape(-1)]
    recv_first, recv_small = exchange_grads(per_chip(grads[0], first), pack_small(small_parts))
    plane_sum = jnp.concatenate([sum_slots(recv_first, name="sum_chip_partials_in0"),
                                 sum_slots(received["rest0"], name="sum_chip_partials_rest0"),
                                 sum_slots(received["late"], name="sum_chip_partials_l1")])
    other_plane = swap_cores(plane_sum)

    g_packed = rowwise_fwd(lambda p, q: (p + q,), [plane_sum, other_plane], [], [F32], name="sum_planes")[0]
    g_big = unpack_layers(g_packed)
    d_big, m_big, v_big = {}, {}, {}
    for n in BIG_NAMES:
        flat = lambda a: a.reshape(-1, a.shape[-1])
        res = adamw([flat(g_big[n])], flat(w_sh[n]), flat(m_sh[n]), flat(v_sh[n]), name=f"adamw_{n}", with_grad=False)
        d_big[n], m_big[n], v_big[n] = (a.reshape(w_sh[n].shape) for a in res)

    small_sum = sum_slots(recv_small, name="sum_small")
    like = [norm_mix, conv_w_full, conv_b, dt_bias, a_log, d_skip, ssm_norm, norm_ffn, final_norm, loss_part.reshape(-1)]
    g_small = unpack_small(small_sum, like)
    loss = g_small[-1][0]
    g_small = dict(zip(SMALL_NAMES + ("final_norm",), g_small[:-1]))
    g_small["conv_w"] = lax.dynamic_slice_in_dim(g_small["conv_w"], chip * conv_cols, conv_cols, axis=2)
    w_small = {"norm_mix": norm_mix, "conv_w": conv_w, "conv_b": conv_b, "dt_bias": dt_bias, "a_log": a_log, "d_skip": d_skip,
               "ssm_norm": ssm_norm, "norm_ffn": norm_ffn, "final_norm": final_norm}
    m_small = {"norm_mix": m_norm_mix, "conv_w": m_conv_w, "conv_b": m_conv_b, "dt_bias": m_dt_bias, "a_log": m_a_log,
               "d_skip": m_d_skip, "ssm_norm": m_ssm_norm, "norm_ffn": m_norm_ffn, "final_norm": m_final_norm}
    v_small = {"norm_mix": v_norm_mix, "conv_w": v_conv_w, "conv_b": v_conv_b, "dt_bias": v_dt_bias, "a_log": v_a_log,
               "d_skip": v_d_skip, "ssm_norm": v_ssm_norm, "norm_ffn": v_norm_ffn, "final_norm": v_final_norm}
    names = SMALL_NAMES + ("final_norm",)
    order = [w_small[n] for n in names]
    res = adamw([pack_small([g_small[n] for n in names])], pack_small(order), pack_small([m_small[n] for n in names]),
                pack_small([v_small[n] for n in names]), name="adamw_small")
    g_s, d_s, m_s, v_s = (dict(zip(names, unpack_small(a, order))) for a in res)

    all_names = ("norm_mix", "w_in", "conv_w", "conv_b", "dt_bias", "a_log", "d_skip", "ssm_norm", "w_out", "norm_ffn",
                 "w_gate", "w_up", "w_down", "final_norm")
    outs = [loss, grad_x]
    for src_big, src_small in ((g_big, g_s), (d_big, d_s), (m_big, m_s), (v_big, v_s)):
        outs += [src_big[n] if n in BIG_NAMES else src_small[n] for n in all_names]
    return tuple(outs)


SWAP_PIECES = 4


def swap_cores_list(arrays):
    n = len(arrays)

    def body(*refs):
        ins, outs, (send_sems, recv_sems) = refs[:n], refs[n:2 * n], refs[2 * n:]
        x, y, c = lax.axis_index("x"), lax.axis_index("y"), lax.axis_index("c")
        copies = []
        for k in range(n):
            rows = ins[k].shape[0] // SWAP_PIECES
            for p in range(SWAP_PIECES):
                part = pl.ds(p * rows, rows)
                copies.append(pltpu.make_async_remote_copy(
                    src_ref=ins[k].at[part], dst_ref=outs[k].at[part], send_sem=send_sems.at[k * SWAP_PIECES + p],
                    recv_sem=recv_sems.at[k * SWAP_PIECES + p], device_id=(x, y, 1 - c), device_id_type=MESH))
        for cp in copies:
            cp.start()
        for cp in copies:
            cp.wait_recv()
        for cp in copies:
            cp.wait_send()

    assert all(a.shape[0] % (8 * SWAP_PIECES) == 0 for a in arrays)
    hbm = pl.BlockSpec(memory_space=pltpu.HBM)
    return pl.pallas_call(
        body, name="swap_cores", in_specs=[hbm] * n, out_specs=[hbm] * n,
        out_shape=[jax.ShapeDtypeStruct(a.shape, a.dtype) for a in arrays],
        scratch_shapes=[pltpu.SemaphoreType.DMA((n * SWAP_PIECES,)), pltpu.SemaphoreType.DMA((n * SWAP_PIECES,))],
    )(*arrays)


def adamw_layers(g_parts, w, m, v, *, name):
    depth, a, b = w.shape
    tr = _pick(a, (256, 352, 192, 128, 8))
    counts = [len(p) for p in g_parts]
    flat_parts = [q for p in g_parts for q in p]
    bc1 = 1.0 / (1.0 - ADAM_B1 ** ADAM_STEP)
    bc2 = 1.0 / (1.0 - ADAM_B2 ** ADAM_STEP)

    def body(*refs):
        layer = pl.program_id(0)
        g, off = None, 0
        for l, cnt in enumerate(counts):
            g_l = refs[off][...]
            for r_ in refs[off + 1:off + cnt]:
                g_l = g_l + r_[...]
            off += cnt
            g = g_l if g is None else jnp.where(layer == l, g_l, g)
        w_ref, m_ref, v_ref, g_out, d_out, m_out, v_out = refs[off:]
        m_new = ADAM_B1 * m_ref[0] + (1.0 - ADAM_B1) * g
        v_new = ADAM_B2 * v_ref[0] + (1.0 - ADAM_B2) * (g * g)
        g_out[0] = g
        m_out[0] = m_new
        v_out[0] = v_new
        d_out[0] = -ADAM_LR * ((m_new * bc1) / (jnp.sqrt(v_new * bc2) + ADAM_EPS) + ADAM_WD * w_ref[0])

    g_spec = pl.BlockSpec((tr, b), lambda l, i: (i, 0))
    spec = pl.BlockSpec((1, tr, b), lambda l, i: (l, i, 0))
    return pl.pallas_call(
        body, name=name, grid=(depth, a // tr), in_specs=[g_spec] * len(flat_parts) + [spec] * 3, out_specs=[spec] * 4,
        out_shape=[jax.ShapeDtypeStruct(w.shape, F32)] * 4, compiler_params=_params(("parallel", "parallel")),
    )(*flat_parts, w, m, v)


def kernel(x, positions, norm_mix, w_in, conv_w, conv_b, dt_bias, a_log, d_skip, ssm_norm, w_out, norm_ffn, w_gate, w_up, w_down, final_norm, loss_target, m_norm_mix, m_w_in, m_conv_w, m_conv_b, m_dt_bias, m_a_log, m_d_skip, m_ssm_norm, m_w_out, m_norm_ffn, m_w_gate, m_w_up, m_w_down, m_final_norm, v_norm_mix, v_w_in, v_conv_w, v_conv_b, v_dt_bias, v_a_log, v_d_skip, v_ssm_norm, v_w_out, v_norm_ffn, v_w_gate, v_w_up, v_w_down, v_final_norm):
    chip = 2 * lax.axis_index("x") + lax.axis_index("y")
    w_sh = {"w_in": w_in, "w_out": w_out, "w_gate": w_gate, "w_up": w_up, "w_down": w_down}
    m_sh = {"w_in": m_w_in, "w_out": m_w_out, "w_gate": m_w_gate, "w_up": m_w_up, "w_down": m_w_down}
    v_sh = {"w_in": v_w_in, "w_out": v_w_out, "w_gate": v_w_gate, "w_up": v_w_up, "w_down": v_w_down}
    assert DEPTH == 2
    first, rest = BIG_NAMES[:1], BIG_NAMES[1:]

    w16 = {n: cast_bf16(w_sh[n].reshape(-1, w_sh[n].shape[-1]), name=f"cast_{n}").reshape(w_sh[n].shape) for n in BIG_NAMES}

    def joined(n, gathered):
        if BIG_SHARD_AXIS[n] == 0:
            full = gathered.reshape(-1, gathered.shape[-1])
        else:
            full = jnp.concatenate([gathered[j] for j in range(N_CHIPS)], axis=1)
        return w_in_columns(full) if n == "w_in" else full

    def per_chip(n, g):
        if BIG_SHARD_AXIS[n] == 0:
            return g.reshape(N_CHIPS, -1, g.shape[-1])
        return jnp.stack(jnp.split(g, N_CHIPS, axis=1))

    conv_cols = CONV_CH // N_CHIPS
    gathered_in0, conv_g = allgather_chips([w16["w_in"][0], conv_w.reshape(-1, LANE)])
    big = [(joined("w_in", gathered_in0),) + (None,) * len(rest), None]
    plan = {
        "rest0": [w16[n][0] for n in rest],
        "make_rest0": lambda gs: tuple(joined(n, g) for n, g in zip(rest, gs)),
        "late": [w16[n][DEPTH - 1] for n in BIG_NAMES],
        "make_late": lambda gs: tuple(joined(n, g) for n, g in zip(BIG_NAMES, gs)),
        "grads_late": lambda gr: [per_chip(n, gr[n]) for n in BIG_NAMES],
        "grads_rest0": lambda gr: [per_chip(n, gr[n]) for n in rest],
    }
    conv_w_full = jnp.concatenate([conv_g[j].reshape(DEPTH, CONV_WIDTH, conv_cols) for j in range(N_CHIPS)], axis=2)
    small_all = []
    for l in range(DEPTH):
        small_all.append({
            "norm_mix": norm_mix[l].reshape(1, -1), "conv_w": conv_w_full[l], "conv_b": conv_b[l].reshape(1, -1),
            "dt_bias": lane_pad(dt_bias[l]), "a_log": lane_pad(a_log[l]), "d_skip": lane_pad(d_skip[l]),
            "ssm_norm": ssm_norm[l].reshape(1, -1), "norm_ffn": norm_ffn[l].reshape(1, -1)})

    loss_part, grad_x, grads, d_final, received = local_step(x, positions, big, small_all, final_norm, loss_target, plan=plan)

    small_parts = [jnp.stack([grads[l][n].reshape(-1) for l in range(DEPTH)]) for n in SMALL_NAMES]
    small_parts += [d_final.reshape(-1), loss_part.reshape(-1)]
    recv_in0, recv_small = exchange_grads(per_chip("w_in", grads[0]["w_in"]), pack_small(small_parts))
    recv = [dict(zip(BIG_NAMES, [recv_in0] + list(received["rest0"]))), dict(zip(BIG_NAMES, received["late"]))]
    keys = [(l, n) for l in range(DEPTH) for n in BIG_NAMES]
    mine = {(l, n): sum_slots(recv[l][n], name=f"sum_partials_{n}_l{l}") for l, n in keys}
    other = dict(zip(keys, swap_cores_list([mine[k] for k in keys])))

    g_big, d_big, m_big, v_big = {}, {}, {}, {}
    for n in BIG_NAMES:
        g_big[n], d_big[n], m_big[n], v_big[n] = adamw_layers([[mine[(l, n)], other[(l, n)]] for l in range(DEPTH)],
                                                              w_sh[n], m_sh[n], v_sh[n], name=f"adamw_{n}")

    small_sum = sum_slots(recv_small, name="sum_small")
    like = [norm_mix, conv_w_full, conv_b, dt_bias, a_log, d_skip, ssm_norm, norm_ffn, final_norm, loss_part.reshape(-1)]
    g_small = unpack_small(small_sum, like)
    loss = g_small[-1][0]
    g_small = dict(zip(SMALL_NAMES + ("final_norm",), g_small[:-1]))
    g_small["conv_w"] = lax.dynamic_slice_in_dim(g_small["conv_w"], chip * conv_cols, conv_cols, axis=2)
    w_small = {"norm_mix": norm_mix, "conv_w": conv_w, "conv_b": conv_b, "dt_bias": dt_bias, "a_log": a_log, "d_skip": d_skip,
               "ssm_norm": ssm_norm, "norm_ffn": norm_ffn, "final_norm": final_norm}
    m_small = {"norm_mix": m_norm_mix, "conv_w": m_conv_w, "conv_b": m_conv_b, "dt_bias": m_dt_bias, "a_log": m_a_log,
               "d_skip": m_d_skip, "ssm_norm": m_ssm_norm, "norm_ffn": m_norm_ffn, "final_norm": m_final_norm}
    v_small = {"norm_mix": v_norm_mix, "conv_w": v_conv_w, "conv_b": v_conv_b, "dt_bias": v_dt_bias, "a_log": v_a_log,
               "d_skip": v_d_skip, "ssm_norm": v_ssm_norm, "norm_ffn": v_norm_ffn, "final_norm": v_final_norm}
    names = SMALL_NAMES + ("final_norm",)
    order = [w_small[n] for n in names]
    res = adamw([pack_small([g_small[n] for n in names])], pack_small(order), pack_small([m_small[n] for n in names]),
                pack_small([v_small[n] for n in names]), name="adamw_small")
    g_s, d_s, m_s, v_s = (dict(zip(names, unpack_small(a, order))) for a in res)

    all_names = ("norm_mix", "w_in", "conv_w", "conv_b", "dt_bias", "a_log", "d_skip", "ssm_norm", "w_out", "norm_ffn",
                 "w_gate", "w_up", "w_down", "final_norm")
    outs = [loss, grad_x]
    for src_big, src_small in ((g_big, g_s), (d_big, d_s), (m_big, m_s), (v_big, v_s)):
        outs += [src_big[n] if n in BIG_NAMES else src_small[n] for n in all_names]
    return tuple(outs)
```

```python
import functools

import jax
import jax.numpy as jnp
from jax import lax
from jax.experimental import pallas as pl
from jax.experimental.pallas import tpu as pltpu

F32 = jnp.float32
BF16 = jnp.bfloat16
MESH = pl.DeviceIdType.MESH

D_MODEL = 1024
DEPTH = 2
HEAD_DIM = 64
N_Q_HEADS = 8
N_KV_HEADS = 2
GQA = N_Q_HEADS // N_KV_HEADS
ATTN_WIDTH = N_Q_HEADS * HEAD_DIM
ROPE_DIM = HEAD_DIM // 4
ROPE_HALF = ROPE_DIM // 2
ROPE_THETA = 500000.0
DILATIONS = (1, 4, 16)
ATTN_BLOCK = 128
SSM_P = 64
SSM_HEADS = 16
SSM_INNER = SSM_HEADS * SSM_P
SSM_GROUPS = 2
HEADS_PER_GROUP = SSM_HEADS // SSM_GROUPS
D_STATE = 128
CONV_WIDTH = 4
CHUNK = 128
CONV_CH = SSM_INNER + 2 * SSM_GROUPS * D_STATE
MIX_WIDTH = ATTN_WIDTH + SSM_INNER
Q_END = ATTN_WIDTH
K_END = Q_END + N_KV_HEADS * HEAD_DIM
V_END = K_END + N_KV_HEADS * HEAD_DIM
Z_END = V_END + SSM_INNER
XBC_END = Z_END + CONV_CH
IN_PROJ = XBC_END + SSM_HEADS
LANE = 128
IN_PAD = XBC_END + LANE
Q_COL, Z_COL, XBC_COL, K_COL, V_COL, DT_COL = 0, 512, 1536, 3072, 3200, 3328
EPS = 1e-5
ADAM_LR, ADAM_B1, ADAM_B2, ADAM_EPS, ADAM_WD, ADAM_STEP = 0.001, 0.9, 0.999, 1e-8, 0.01, 10
N_CHIPS = 4
N_DEV = 8
VMEM_LIMIT = 48 * 1024 * 1024
NEG_BIG = -1e30


def _params(sem=None):
    return pltpu.CompilerParams(dimension_semantics=sem, vmem_limit_bytes=VMEM_LIMIT)


def _pick(n, prefs):
    for p in prefs:
        if n % p == 0:
            return p
    return n


def matmul(a, b, *, name, ta=False, tb=False, out_dtype=F32, residual=None):
    if ta:
        assert not tb and residual is None
        return _matmul_over_rows(a, b, name=name, out_dtype=out_dtype)
    return _matmul_full_k(a, b, name=name, tb=tb, out_dtype=out_dtype, residual=residual)


def _matmul_full_k(a, b, *, name, tb, out_dtype, residual):
    a_parts = list(a) if isinstance(a, (list, tuple)) else [a]
    n_a = len(a_parts)
    m = a_parts[0].shape[0]
    kdim = sum(p.shape[1] for p in a_parts)
    wide = kdim > 1536 or any(p.dtype == F32 for p in a_parts)
    n = b.shape[0] if tb else b.shape[1]
    tm = _pick(m, (512, 256)) if wide else _pick(m, (1024, 512, 256))
    tn = _pick(n, (1152, 1408, 1536, 1024, 768, 512, 384, 256, 128))
    b_spec = pl.BlockSpec((tn, kdim), lambda i, j: (j, 0)) if tb else pl.BlockSpec((kdim, tn), lambda i, j: (0, j))
    o_spec = pl.BlockSpec((tm, tn), lambda i, j: (i, j))
    dims = (((1,), (1 if tb else 0,)), ((), ()))
    has_res = residual is not None

    def body(*refs):
        b_ref, o_ref = refs[n_a], refs[-1]
        pieces = [r[...].astype(BF16) for r in refs[:n_a]]
        av = pieces[0] if n_a == 1 else jnp.concatenate(pieces, axis=1)
        r = lax.dot_general(av, b_ref[...].astype(BF16), dims, preferred_element_type=F32)
        if has_res:
            r = r + refs[n_a + 1][...]
        o_ref[...] = r.astype(out_dtype)

    in_specs = ([pl.BlockSpec((tm, p.shape[1]), lambda i, j: (i, 0)) for p in a_parts] + [b_spec]
                + ([o_spec] if has_res else []))
    args = tuple(a_parts) + (b,) + ((residual,) if has_res else ())
    return pl.pallas_call(
        body, name=name, grid=(m // tm, n // tn), in_specs=in_specs, out_specs=o_spec,
        out_shape=jax.ShapeDtypeStruct((m, n), out_dtype),
        compiler_params=_params(("parallel", "parallel")),
    )(*args)


def _matmul_over_rows(a, b, *, name, out_dtype):
    t, m = a.shape
    n = b.shape[1]
    tm = _pick(m, (1024, 1408, 768, 512, 256, 128))
    tn = _pick(n, (1152, 1408, 1024, 768, 512, 384, 256, 128))
    tk = _pick(t, (1024, 512, 256, 128))
    nk = t // tk

    def body(a_ref, b_ref, o_ref, acc):
        k = pl.program_id(2)
        part = lax.dot_general(a_ref[...].astype(BF16), b_ref[...].astype(BF16), (((0,), (0,)), ((), ())),
                               preferred_element_type=F32)

        @pl.when(k == 0)
        def _():
            acc[...] = part

        @pl.when(k > 0)
        def _():
            acc[...] += part

        @pl.when(k == nk - 1)
        def _():
            o_ref[...] = acc[...].astype(out_dtype)

    return pl.pallas_call(
        body, name=name, grid=(m // tm, n // tn, nk),
        in_specs=[pl.BlockSpec((tk, tm), lambda i, j, k: (k, i)), pl.BlockSpec((tk, tn), lambda i, j, k: (k, j))],
        out_specs=pl.BlockSpec((tm, tn), lambda i, j, k: (i, j)),
        out_shape=jax.ShapeDtypeStruct((m, n), out_dtype),
        scratch_shapes=[pltpu.VMEM((tm, tn), F32)],
        compiler_params=_params(("parallel", "parallel", "arbitrary")),
    )(a, b)


ROW_BLOCK_BYTES = 32 * 1024 * 1024


def _row_tile(t, tr, widths, n_copies):
    lanes = sum(-(-wd // LANE) * LANE for wd in widths) * n_copies
    tr = min(tr, t)
    while tr > 8 and tr * lanes * 4 > ROW_BLOCK_BYTES:
        tr //= 2
    return tr


def _row_widths(rows, groups, windows):
    windows = windows or [None] * len(rows)
    widths = [(w[1] if w else a.shape[1]) // groups for a, w in zip(rows, windows)]
    assert all(w is None or w[0] % wd == 0 for w, wd in zip(windows, widths))
    return widths, [(w[0] // wd if w else 0) for w, wd in zip(windows, widths)]


def _row_specs(tr, widths, offs):
    return [pl.BlockSpec((tr, wd), functools.partial(lambda g, i, off: (i, g + off), off=off)) for wd, off in zip(widths, offs)]


def rowwise_fwd(fn, rows, params, out_dtypes, *, name, tr=512, groups=1, windows=None):
    t = rows[0].shape[0]
    widths, offs = _row_widths(rows, groups, windows)
    tr = _row_tile(t, tr, widths, 2)
    row_specs = _row_specs(tr, widths, offs)
    par_spec = lambda p: pl.BlockSpec((1, p.shape[1] // groups), lambda g, i: (0, g))
    n_in = len(rows) + len(params)
    out_cols = [o.shape[1] for o in jax.eval_shape(
        fn, *[jax.ShapeDtypeStruct((tr, wd), F32) for wd in widths],
        *[jax.ShapeDtypeStruct((1, p.shape[1] // groups), F32) for p in params])]

    def body(*refs):
        vals = [r[...].astype(F32) for r in refs[:n_in]]
        outs = fn(*vals)
        for o_ref, o in zip(refs[n_in:], outs):
            o_ref[...] = o.astype(o_ref.dtype)

    return pl.pallas_call(
        body, name=name, grid=(groups, t // tr),
        in_specs=row_specs + [par_spec(p) for p in params],
        out_specs=[pl.BlockSpec((tr, c), lambda g, i: (i, g)) for c in out_cols],
        out_shape=[jax.ShapeDtypeStruct((t, c * groups), d) for c, d in zip(out_cols, out_dtypes)],
        compiler_params=_params(("arbitrary", "arbitrary")),
    )(*rows, *params)


def rowwise_bwd(fn, rows, params, cts, drow_dtypes, *, name, tr=512, groups=1, add_to_first=None, windows=None,
                ct_windows=None):
    t = rows[0].shape[0]
    widths, offs = _row_widths(rows, groups, windows)
    ct_widths, ct_offs = _row_widths(cts, groups, ct_windows)
    tr = _row_tile(t, tr, widths + ct_widths, 2)
    row_spec = lambda a: pl.BlockSpec((tr, a.shape[1] // groups), lambda g, i: (i, g))
    row_specs = _row_specs(tr, widths, offs)
    par_spec = lambda p: pl.BlockSpec((1, p.shape[1] // groups), lambda g, i: (0, g))
    n_rows, n_par, n_ct = len(rows), len(params), len(cts)
    has_add = add_to_first is not None
    n_in = n_rows + n_par + n_ct + (1 if has_add else 0)

    def body(*refs):
        i = pl.program_id(1)
        vals = [r[...].astype(F32) for r in refs[:n_rows + n_par]]
        ct_vals = tuple(r[...].astype(F32) for r in refs[n_rows + n_par:n_rows + n_par + n_ct])
        _, vjp = jax.vjp(fn, *vals)
        grads = vjp(ct_vals)
        out_refs = refs[n_in:]
        for idx in range(n_rows):
            g = grads[idx]
            if idx == 0 and has_add:
                g = g + refs[n_in - 1][...]
            out_refs[idx][...] = g.astype(out_refs[idx].dtype)
        for idx in range(n_par):
            p_ref = out_refs[n_rows + idx]

            @pl.when(i == 0)
            def _():
                p_ref[...] = jnp.zeros_like(p_ref)

            p_ref[...] += grads[n_rows + idx]

    ins = list(rows) + list(params) + list(cts) + ([add_to_first] if has_add else [])
    in_specs = (row_specs + [par_spec(p) for p in params] + _row_specs(tr, ct_widths, ct_offs)
                + ([row_spec(add_to_first)] if has_add else []))
    return pl.pallas_call(
        body, name=name, grid=(groups, t // tr), in_specs=in_specs,
        out_specs=[pl.BlockSpec((tr, wd), lambda g, i: (i, g)) for wd in widths] + [par_spec(p) for p in params],
        out_shape=[jax.ShapeDtypeStruct((t, wd * groups), d) for wd, d in zip(widths, drow_dtypes)]
        + [jax.ShapeDtypeStruct(p.shape, F32) for p in params],
        compiler_params=_params(("arbitrary", "arbitrary")),
    )(*ins)


def rms_fn(x, w):
    return (x * lax.rsqrt(jnp.mean(x * x, axis=-1, keepdims=True) + EPS) * w,)


def swiglu_fn(g, u):
    return (g * jax.nn.sigmoid(g) * u,)


def gated_norm_fn(y, z, w):
    v = y * (z * jax.nn.sigmoid(z))
    return (v * lax.rsqrt(jnp.mean(v * v, axis=-1, keepdims=True) + EPS) * w,)


def loss_and_grad(h, target, w, *, tr=512):
    t, d = h.shape

    def loss_fn(hv, wv, tv):
        err = rms_fn(hv, wv)[0] - tv
        per_row = jnp.mean(err * err, axis=-1, keepdims=True)
        return 0.5 * jnp.sum(per_row, axis=0, keepdims=True)

    def body(h_ref, t_ref, w_ref, dh_ref, dw_ref, loss_ref):
        i = pl.program_id(0)

        @pl.when(i == 0)
        def _():
            dw_ref[...] = jnp.zeros_like(dw_ref)
            loss_ref[...] = jnp.zeros_like(loss_ref)

        tv = t_ref[...]
        val, vjp = jax.vjp(lambda hv, wv: loss_fn(hv, wv, tv), h_ref[...], w_ref[...])
        dh, dw = vjp(jnp.ones((1, 1), F32))
        dh_ref[...] = dh
        dw_ref[...] += dw
        loss_ref[...] += jnp.broadcast_to(val, loss_ref.shape)

    row = pl.BlockSpec((tr, d), lambda i: (i, 0))
    par = pl.BlockSpec((1, d), lambda i: (0, 0))
    return pl.pallas_call(
        body, name="loss_and_grad", grid=(t // tr,), in_specs=[row, row, par],
        out_specs=[row, par, pl.BlockSpec((1, LANE), lambda i: (0, 0))],
        out_shape=[jax.ShapeDtypeStruct((t, d), F32), jax.ShapeDtypeStruct((1, d), F32),
                   jax.ShapeDtypeStruct((1, LANE), F32)],
        compiler_params=_params(("arbitrary",)),
    )(h, target, w)


def _split3(x):
    hi = x.astype(BF16)
    r1 = x - hi.astype(F32)
    mid = r1.astype(BF16)
    lo = (r1 - mid.astype(F32)).astype(BF16)
    return hi, mid, lo


def _dot01_left(m01, x):
    return sum(jnp.dot(m01, p, preferred_element_type=F32) for p in _split3(x))


def _dot01_right(x, m01):
    return sum(jnp.dot(p, m01, preferred_element_type=F32) for p in _split3(x))


ATTN_PAD = ATTN_BLOCK * DILATIONS[-1]
Q_GROUP_W = GQA * HEAD_DIM
ATTN_VMEM_LIMIT = 56 * 1024 * 1024
HALF_W = 2 * HEAD_DIM
N_HALF = Q_GROUP_W // HALF_W
_ATTN_BIAS_BUF = pltpu.VMEM((2, GQA * ATTN_BLOCK, 2 * ATTN_BLOCK), F32)


def _attn_mask(n):
    rows = GQA * ATTN_BLOCK
    qi = lax.broadcasted_iota(jnp.int32, (rows, 2 * ATTN_BLOCK), 0) % ATTN_BLOCK
    ki = lax.broadcasted_iota(jnp.int32, (rows, 2 * ATTN_BLOCK), 1)
    delta = qi + ATTN_BLOCK - ki
    return (delta >= 0) & (delta <= ATTN_BLOCK) & ((n - 1) * ATTN_BLOCK + ki >= 0)


def _attn_bias(bias_s):
    for first in (0, 1):
        bias_s[first] = jnp.where(_attn_mask(first), 0.0, NEG_BIG)


def _rope(x, cos_v, sin_v, swap, scale, adjoint):
    if adjoint:
        return (x * cos_v + _dot01_right(x * sin_v, swap)) * scale
    return (x * cos_v + _dot01_right(x, swap) * sin_v) * scale


def _swap_matrix():
    c = HEAD_DIM
    ci = lax.broadcasted_iota(jnp.int32, (c, c), 0)
    cj = lax.broadcasted_iota(jnp.int32, (c, c), 1)
    swap = ((cj == ci + ROPE_HALF) & (ci < ROPE_HALF)) | ((cj == ci - ROPE_HALF) & (ci >= ROPE_HALF) & (ci < ROPE_DIM))
    return swap.astype(BF16)


def _attn_blocks(s_len):
    out = []
    for i, d in enumerate(DILATIONS):
        nb = s_len // (ATTN_BLOCK * d)
        for r in range(d):
            for n in range(nb):
                start = r + d * ATTN_BLOCK * n
                out.append((i, d, start, ATTN_PAD + start - d * ATTN_BLOCK, n))
    return out


def _rows(start, size, d):
    return pl.ds(start, size, stride=d) if d > 1 else pl.ds(start, size)


def _attn_prologue(q_refs, kv_ref, tab_ref, q_s, kv_s, hk, s_len):
    swap = _swap_matrix()
    cos_v, sin_v = tab_ref[0, :, :HEAD_DIM], tab_ref[0, :, HEAD_DIM:]
    for j in range(N_HALF):
        for e in range(2):
            cols = slice(e * HEAD_DIM, (e + 1) * HEAD_DIM)
            q_s[j][:, cols] = _rope(q_refs[j][0, :, cols], cos_v, sin_v, swap, HEAD_DIM ** -0.5, False)
    kv_s[0:ATTN_PAD, :] = jnp.zeros((ATTN_PAD, HALF_W), F32)
    for h in range(N_KV_HEADS):
        @pl.when(hk == h)
        def _():
            kv_s[ATTN_PAD:ATTN_PAD + s_len, :HEAD_DIM] = _rope(kv_ref[0, :, h * HEAD_DIM:(h + 1) * HEAD_DIM], cos_v, sin_v,
                                                               swap, 1.0, False)
            kv_s[ATTN_PAD:ATTN_PAD + s_len, HEAD_DIM:] = kv_ref[0, :, LANE + h * HEAD_DIM:LANE + (h + 1) * HEAD_DIM]


def _stack_heads(halves):
    return jnp.concatenate([h[:, e * HEAD_DIM:(e + 1) * HEAD_DIM] for h in halves for e in range(2)], axis=0)


def _unstack_heads(x, j):
    return jnp.concatenate([x[(2 * j + e) * ATTN_BLOCK:(2 * j + e + 1) * ATTN_BLOCK] for e in range(2)], axis=1)


def _stack_stats(halves):
    return jnp.concatenate([jnp.max(h[:, e * HEAD_DIM:(e + 1) * HEAD_DIM], axis=1, keepdims=True)
                            for h in halves for e in range(2)], axis=0)


def _attn_in_specs(s_len):
    assert K_COL % (2 * LANE) == 0 and V_COL == K_COL + LANE

    def halves(first_tile):
        return [pl.BlockSpec((1, s_len, HALF_W), functools.partial(lambda b, h, j: (b, 0, first_tile + N_HALF * h + j), j=j))
                for j in range(N_HALF)]

    kv_spec = pl.BlockSpec((1, s_len, 2 * LANE), lambda b, h: (b, 0, K_COL // (2 * LANE)))
    t_spec = pl.BlockSpec((1, s_len, 2 * HEAD_DIM), lambda b, h: (b, 0, 0))
    o_spec = pl.BlockSpec((1, s_len, Q_GROUP_W), lambda b, h: (b, 0, h))
    return halves(Q_COL // HALF_W), kv_spec, t_spec, o_spec, halves(0)


class SideCopy:
    def __init__(self, side, *, n_in, n_out, grid):
        self.side, self.n_in, self.n_out, self.grid = side, n_in, n_out, grid
        hbm = pl.BlockSpec(memory_space=pltpu.HBM)
        if side is None:
            self.in_specs, self.out_specs, self.out_shape, self.scratch, self.args = [], [], [], [], []
            return
        srcs, per_dest = side
        n = len(srcs)
        self.in_specs, self.out_specs, self.args = [hbm] * n, [hbm] * n, list(srcs)
        self.out_shape = [jax.ShapeDtypeStruct(s.shape if per_dest else (N_CHIPS,) + s.shape, s.dtype) for s in srcs]
        self.scratch = [pltpu.SemaphoreType.DMA(((N_CHIPS - 1) * n,)), pltpu.SemaphoreType.DMA(((N_CHIPS - 1) * n,)),
                        pltpu.SemaphoreType.DMA((n,))]

    def wrap(self, body):
        if self.side is None:
            return body
        n_in, n_out, grid, per_dest, n = self.n_in, self.n_out, self.grid, self.side[1], len(self.side[0])

        def wrapped(*refs):
            ins, srcs = refs[:n_in], refs[n_in:n_in + n]
            outs, dsts = refs[n_in + n:n_in + n + n_out], refs[n_in + n + n_out:n_in + 2 * n + n_out]
            scratch, sems = refs[n_in + 2 * n + n_out:-3], refs[-3:]
            ids = [pl.program_id(a) for a in range(len(grid))]
            first = functools.reduce(lambda p, q: p & q, [i == 0 for i in ids])
            last = functools.reduce(lambda p, q: p & q, [i == g - 1 for i, g in zip(ids, grid)])

            @pl.when(first)
            def _():
                for a in range(n):
                    local, sends, _ = _chip_copies(srcs[a], dsts[a], *sems, per_dest, a)
                    local.start()
                    for cp in sends:
                        cp.start()

            body(*ins, *outs, *scratch)

            @pl.when(last)
            def _():
                for a in range(n):
                    local, sends, recvs = _chip_copies(srcs[a], dsts[a], *sems, per_dest, a)
                    for cp in recvs:
                        cp.wait_recv()
                    for cp in sends:
                        cp.wait_send()
                    local.wait()

        return wrapped


def _chip_copies(src_ref, dst_ref, send_sems, recv_sems, local_sems, per_dest, a=0):
    x, y, c = lax.axis_index("x"), lax.axis_index("y"), lax.axis_index("c")
    chip = 2 * x + y
    own = src_ref.at[chip] if per_dest else src_ref
    local = pltpu.make_async_copy(own, dst_ref.at[chip], local_sems.at[a])
    sends, recvs = [], []
    for k, (px, py) in enumerate([(1 - x, y), (x, 1 - y), (1 - x, 1 - y)]):
        k = (N_CHIPS - 1) * a + k
        peer = dict(send_sem=send_sems.at[k], recv_sem=recv_sems.at[k], device_id=(px, py, c), device_id_type=MESH)
        sends.append(pltpu.make_async_remote_copy(src_ref=src_ref.at[2 * px + py] if per_dest else src_ref,
                                                  dst_ref=dst_ref.at[chip], **peer))
        recvs.append(pltpu.make_async_remote_copy(src_ref=own, dst_ref=dst_ref.at[2 * px + py], **peer))
    return local, sends, recvs


def attn_fwd(proj3, rope_tab, *, name, side=None):
    b, s_len, _ = proj3.shape
    q_specs, kv_spec, t_spec, o_spec, _ = _attn_in_specs(s_len)
    n_br = len(DILATIONS)

    def body(*refs):
        q_refs, (kv_ref, tab_ref, o_ref, lse_ref) = refs[:N_HALF], refs[N_HALF:N_HALF + 4]
        scratch = refs[N_HALF + 4:]
        q_s, kv_s = scratch[:N_HALF], scratch[N_HALF]
        o_s = [scratch[N_HALF + 1 + i * N_HALF:N_HALF + 1 + (i + 1) * N_HALF] for i in range(n_br)]
        l_s = [scratch[N_HALF + 1 + (n_br + i) * N_HALF:N_HALF + 1 + (n_br + i + 1) * N_HALF] for i in range(n_br)]
        bias_s = scratch[-1]
        _attn_prologue(q_refs, kv_ref, tab_ref, q_s, kv_s, pl.program_id(1), s_len)
        _attn_bias(bias_s)
        for i, d, q0, k0, n in _attn_blocks(s_len):
            qrows = _rows(q0, ATTN_BLOCK, d)
            qv = _stack_heads([q_s[j][qrows, :] for j in range(N_HALF)]).astype(BF16)
            kvb = kv_s[_rows(k0, 2 * ATTN_BLOCK, d), :].astype(BF16)
            kk, vv = kvb[:, :HEAD_DIM], kvb[:, HEAD_DIM:]
            sc = lax.dot_general(qv, kk, (((1,), (1,)), ((), ())), preferred_element_type=F32)
            sc = sc + bias_s[min(n, 1)]
            m = jnp.max(sc, axis=-1, keepdims=True)
            pr = jnp.exp(sc - m)
            den = jnp.sum(pr, axis=-1, keepdims=True)
            o = jnp.dot(pr.astype(BF16), vv, preferred_element_type=F32) / den
            lse_b = jnp.broadcast_to(m + jnp.log(den), (GQA * ATTN_BLOCK, HEAD_DIM))
            for j in range(N_HALF):
                o_s[i][j][qrows, :] = _unstack_heads(o, j)
                l_s[i][j][qrows, :] = _unstack_heads(lse_b, j)
        step = 256
        for t0 in range(0, s_len, step):
            rs = pl.ds(t0, step)
            for j in range(N_HALF):
                ls = [l_s[i][j][rs, :] for i in range(n_br)]
                m = functools.reduce(jnp.maximum, ls)
                es = [jnp.exp(l - m) for l in ls]
                tot = functools.reduce(lambda a, c: a + c, es)
                inv = 1.0 / tot
                acc = None
                for i in range(n_br):
                    term = (es[i] * inv) * o_s[i][j][rs, :]
                    acc = term if acc is None else acc + term
                o_ref[0, rs, j * HALF_W:(j + 1) * HALF_W] = acc
                lse_ref[0, rs, j * HALF_W:(j + 1) * HALF_W] = m + jnp.log(tot)

    half_buf = pltpu.VMEM((s_len, HALF_W), F32)
    call = SideCopy(side, n_in=N_HALF + 2, n_out=2, grid=(b, N_KV_HEADS))
    return pl.pallas_call(
        call.wrap(body), name=name, grid=(b, N_KV_HEADS), in_specs=q_specs + [kv_spec, t_spec] + call.in_specs,
        out_specs=[o_spec, o_spec] + call.out_specs,
        out_shape=[jax.ShapeDtypeStruct((b, s_len, ATTN_WIDTH), F32)] * 2 + call.out_shape,
        scratch_shapes=[half_buf] * N_HALF + [pltpu.VMEM((ATTN_PAD + s_len, HALF_W), F32)] + [half_buf] * (2 * n_br * N_HALF)
        + [_ATTN_BIAS_BUF] + call.scratch,
        compiler_params=pltpu.CompilerParams(dimension_semantics=("arbitrary", "arbitrary"), vmem_limit_bytes=ATTN_VMEM_LIMIT),
    )(*([proj3] * (N_HALF + 1)), rope_tab, *call.args)


def attn_bwd(proj3, rope_tab, attn3, lse3, d_attn3, *, name, side=None):
    b, s_len, _ = proj3.shape
    q_specs, kv_spec, t_spec, o_spec, half_specs = _attn_in_specs(s_len)

    def body(*refs):
        q_refs = refs[:N_HALF]
        kv_ref, tab_ref, o_ref = refs[N_HALF:N_HALF + 3]
        lse_refs = refs[N_HALF + 3:2 * N_HALF + 3]
        do_refs = refs[2 * N_HALF + 3:3 * N_HALF + 3]
        dq_ref, dkv_ref = refs[3 * N_HALF + 3:3 * N_HALF + 5]
        scratch = refs[3 * N_HALF + 5:]
        q_s, kv_s = scratch[:N_HALF], scratch[N_HALF]
        dl_s = scratch[N_HALF + 1:2 * N_HALF + 1]
        dq_s = scratch[2 * N_HALF + 1:3 * N_HALF + 1]
        dkv_s = scratch[3 * N_HALF + 1]
        bias_s = scratch[-1]
        _attn_prologue(q_refs, kv_ref, tab_ref, q_s, kv_s, pl.program_id(1), s_len)
        _attn_bias(bias_s)
        dkv_s[...] = jnp.zeros_like(dkv_s)
        for j in range(N_HALF):
            dq_s[j][...] = jnp.zeros_like(dq_s[j])
            for e in range(2):
                cols = slice(e * HEAD_DIM, (e + 1) * HEAD_DIM)
                ocols = slice(j * HALF_W + e * HEAD_DIM, j * HALF_W + (e + 1) * HEAD_DIM)
                delta = jnp.sum(do_refs[j][0, :, cols] * o_ref[0, :, ocols], axis=1, keepdims=True)
                dl_s[j][:, cols] = jnp.broadcast_to(delta, (s_len, HEAD_DIM))
        for i, d, q0, k0, n in _attn_blocks(s_len):
            qrows, krows = _rows(q0, ATTN_BLOCK, d), _rows(k0, 2 * ATTN_BLOCK, d)
            qv = _stack_heads([q_s[j][qrows, :] for j in range(N_HALF)]).astype(BF16)
            kvb = kv_s[krows, :].astype(BF16)
            kk, vv = kvb[:, :HEAD_DIM], kvb[:, HEAD_DIM:]
            do16 = _stack_heads([do_refs[j].at[0][qrows, :] for j in range(N_HALF)]).astype(BF16)
            lse = _stack_stats([lse_refs[j].at[0][qrows, :] for j in range(N_HALF)])
            delta = _stack_stats([dl_s[j][qrows, :] for j in range(N_HALF)])
            sc = lax.dot_general(qv, kk, (((1,), (1,)), ((), ())), preferred_element_type=F32)
            pr = jnp.exp(sc + bias_s[min(n, 1)] - lse)
            dv = lax.dot_general(pr.astype(BF16), do16, (((0,), (0,)), ((), ())), preferred_element_type=F32)
            dp = lax.dot_general(do16, vv, (((1,), (1,)), ((), ())), preferred_element_type=F32)
            ds = (pr * (dp - delta)).astype(BF16)
            dq = jnp.dot(ds, kk, preferred_element_type=F32)
            dk = lax.dot_general(ds, qv, (((0,), (0,)), ((), ())), preferred_element_type=F32)
            for j in range(N_HALF):
                dq_s[j][qrows, :] += _unstack_heads(dq, j)
            dkv_s[krows, :] += jnp.concatenate([dk, dv], axis=1)
        swap = _swap_matrix()
        cos_v, sin_v = tab_ref[0, :, :HEAD_DIM], tab_ref[0, :, HEAD_DIM:]
        for j in range(N_HALF):
            for e in range(2):
                cols = slice(e * HEAD_DIM, (e + 1) * HEAD_DIM)
                ocols = slice(j * HALF_W + e * HEAD_DIM, j * HALF_W + (e + 1) * HEAD_DIM)
                dq_ref[0, :, ocols] = _rope(dq_s[j][:, cols], cos_v, sin_v, swap, HEAD_DIM ** -0.5, True).astype(dq_ref.dtype)
        d_k = _rope(dkv_s[ATTN_PAD:ATTN_PAD + s_len, :HEAD_DIM], cos_v, sin_v, swap, 1.0, True)
        d_v = dkv_s[ATTN_PAD:ATTN_PAD + s_len, HEAD_DIM:]
        for h in range(N_KV_HEADS):
            @pl.when(pl.program_id(1) == h)
            def _():
                dkv_ref[0, :, h * HEAD_DIM:(h + 1) * HEAD_DIM] = d_k.astype(dkv_ref.dtype)
                dkv_ref[0, :, LANE + h * HEAD_DIM:LANE + (h + 1) * HEAD_DIM] = d_v.astype(dkv_ref.dtype)

    kv_out = pl.BlockSpec((1, s_len, 2 * LANE), lambda bi, h: (bi, 0, 0))
    kv_shape = jax.ShapeDtypeStruct((b, s_len, 2 * LANE), BF16)
    half_buf = pltpu.VMEM((s_len, HALF_W), F32)
    pad_buf = pltpu.VMEM((ATTN_PAD + s_len, HALF_W), F32)
    call = SideCopy(side, n_in=3 * N_HALF + 3, n_out=2, grid=(b, N_KV_HEADS))
    return pl.pallas_call(
        call.wrap(body), name=name, grid=(b, N_KV_HEADS),
        in_specs=q_specs + [kv_spec, t_spec, o_spec] + half_specs + half_specs + call.in_specs,
        out_specs=[o_spec, kv_out] + call.out_specs,
        out_shape=[jax.ShapeDtypeStruct((b, s_len, ATTN_WIDTH), BF16), kv_shape] + call.out_shape,
        scratch_shapes=[half_buf] * N_HALF + [pad_buf] + [half_buf] * (2 * N_HALF) + [pad_buf, _ATTN_BIAS_BUF] + call.scratch,
        compiler_params=pltpu.CompilerParams(dimension_semantics=("arbitrary", "arbitrary"), vmem_limit_bytes=ATTN_VMEM_LIMIT),
    )(*([proj3] * (N_HALF + 1)), rope_tab, attn3, *([lse3] * N_HALF), *([d_attn3] * N_HALF), *call.args)


CONV_TC = 256
CONV_COL0 = XBC_COL // CONV_TC


def _shift_down(u, s):
    if s == 0:
        return u
    rows = lax.broadcasted_iota(jnp.int32, u.shape, 0)
    return jnp.where(rows >= s, pltpu.roll(u, s, 0), 0.0)


def _shift_up(u, s):
    if s == 0:
        return u
    n = u.shape[0]
    rows = lax.broadcasted_iota(jnp.int32, u.shape, 0)
    return jnp.where(rows < n - s, pltpu.roll(u, n - s, 0), 0.0)


def conv_silu_fwd(proj3, w, bias, *, name):
    b, s, _ = proj3.shape
    u_spec = pl.BlockSpec((1, s, CONV_TC), lambda j, bi: (bi, 0, CONV_COL0 + j))
    o_spec = pl.BlockSpec((1, s, CONV_TC), lambda j, bi: (bi, 0, j))
    w_spec = pl.BlockSpec((CONV_WIDTH, CONV_TC), lambda j, bi: (0, j))
    b_spec = pl.BlockSpec((1, CONV_TC), lambda j, bi: (0, j))

    def body(u_ref, w_ref, b_ref, o_ref):
        u = u_ref[0]
        y = jnp.broadcast_to(b_ref[...], u.shape)
        for k in range(CONV_WIDTH):
            y = y + w_ref[k:k + 1, :] * _shift_down(u, CONV_WIDTH - 1 - k)
        o_ref[0] = y * jax.nn.sigmoid(y)

    return pl.pallas_call(
        body, name=name, grid=(CONV_CH // CONV_TC, b), in_specs=[u_spec, w_spec, b_spec], out_specs=o_spec,
        out_shape=jax.ShapeDtypeStruct((b, s, CONV_CH), F32),
        compiler_params=_params(("parallel", "arbitrary")),
    )(proj3, w, bias)


def conv_silu_bwd(proj3, w, bias, dact, *, name):
    b, s, _ = proj3.shape
    u_spec = pl.BlockSpec((1, s, CONV_TC), lambda j, bi: (bi, 0, CONV_COL0 + j))
    o_spec = pl.BlockSpec((1, s, CONV_TC), lambda j, bi: (bi, 0, j))
    w_spec = pl.BlockSpec((CONV_WIDTH, CONV_TC), lambda j, bi: (0, j))
    b_spec = pl.BlockSpec((1, CONV_TC), lambda j, bi: (0, j))

    def body(u_ref, w_ref, b_ref, g_ref, du_ref, dw_ref, db_ref):
        bi = pl.program_id(1)

        @pl.when(bi == 0)
        def _():
            dw_ref[...] = jnp.zeros_like(dw_ref)
            db_ref[...] = jnp.zeros_like(db_ref)

        u = u_ref[0]
        y = jnp.broadcast_to(b_ref[...], u.shape)
        shifted = [_shift_down(u, CONV_WIDTH - 1 - k) for k in range(CONV_WIDTH)]
        for k in range(CONV_WIDTH):
            y = y + w_ref[k:k + 1, :] * shifted[k]
        sig = jax.nn.sigmoid(y)
        dy = g_ref[0] * (sig * (1.0 + y * (1.0 - sig)))
        du = jnp.zeros_like(u)
        for k in range(CONV_WIDTH):
            du = du + w_ref[k:k + 1, :] * _shift_up(dy, CONV_WIDTH - 1 - k)
            dw_ref[k:k + 1, :] += jnp.sum(dy * shifted[k], axis=0, keepdims=True)
        du_ref[0] = du.astype(du_ref.dtype)
        db_ref[...] += jnp.sum(dy, axis=0, keepdims=True)

    return pl.pallas_call(
        body, name=name, grid=(CONV_CH // CONV_TC, b), in_specs=[u_spec, w_spec, b_spec, o_spec],
        out_specs=[o_spec, w_spec, b_spec],
        out_shape=[jax.ShapeDtypeStruct((b, s, CONV_CH), BF16), jax.ShapeDtypeStruct((CONV_WIDTH, CONV_CH), F32),
                   jax.ShapeDtypeStruct((1, CONV_CH), F32)],
        compiler_params=_params(("parallel", "arbitrary")),
    )(proj3, w, bias, dact)


SSD_INTERLEAVE = 8
SSD_INTERLEAVE_FWD = 1


def _softplus(z):
    e = jnp.exp(-jnp.abs(z))
    u = 1.0 + e
    log1p = jnp.where(u == 1.0, e, jnp.log(u) * e / jnp.where(u == 1.0, 1.0, u - 1.0))
    return jnp.maximum(z, 0.0) + log1p


def _tri(lower):
    r = lax.broadcasted_iota(jnp.int32, (CHUNK, CHUNK), 0)
    c = lax.broadcasted_iota(jnp.int32, (CHUNK, CHUNK), 1)
    return (r >= c) if lower else (r <= c)


def _ssd_common(dtr_ref, dtb_ref, alog_ref):
    z = dtr_ref[0] + dtb_ref[...]
    dt = _softplus(z)
    aneg = -jnp.exp(alog_ref[...])
    acs = _dot01_left(_tri(True).astype(BF16), dt * aneg)
    return z, dt, aneg, acs


def _ssd_specs(nc, reverse):
    cidx = (lambda c: nc - 1 - c) if reverse else (lambda c: c)
    act_spec = pl.BlockSpec((1, CHUNK, CONV_CH), lambda b, c: (b, cidx(c), 0))
    y_spec = pl.BlockSpec((1, CHUNK, SSM_INNER), lambda b, c: (b, cidx(c), 0))
    dt_in_spec = pl.BlockSpec((1, CHUNK, LANE), lambda b, c: (b, cidx(c), DT_COL // LANE))
    dt_out_spec = pl.BlockSpec((1, CHUNK, LANE), lambda b, c: (b, cidx(c), 0))
    par_spec = pl.BlockSpec((1, LANE), lambda b, c: (0, 0))
    h_spec = pl.BlockSpec((1, SSM_HEADS, 1, SSM_P, D_STATE), lambda b, c: (b, 0, cidx(c), 0, 0))
    return act_spec, y_spec, dt_in_spec, dt_out_spec, par_spec, h_spec


def _head_cols(h):
    return slice(h * SSM_P, (h + 1) * SSM_P)


def _group_cols(g, which):
    start = SSM_INNER + which * SSM_GROUPS * D_STATE + g * D_STATE
    return slice(start, start + D_STATE)


def _each(f, *lists):
    return [f(*a) for a in zip(*lists)]


def _nt(a, b):
    return lax.dot_general(a, b, (((1,), (1,)), ((), ())), preferred_element_type=F32)


def _tn(a, b):
    return lax.dot_general(a, b, (((0,), (0,)), ((), ())), preferred_element_type=F32)


def _nn(a, b):
    return jnp.dot(a, b, preferred_element_type=F32)


def _rowsum(a):
    return jnp.sum(a, axis=1, keepdims=True)


def _colsum(a):
    return jnp.sum(a, axis=0, keepdims=True)


def _bf(a):
    return a.astype(BF16)


def _head_batches(g, width=SSD_INTERLEAVE):
    first = g * HEADS_PER_GROUP
    return [list(range(first + k, first + k + width)) for k in range(0, HEADS_PER_GROUP, width)]


def _decay_matrix(acs_j, acs_row, tri_mask):
    dm = jnp.broadcast_to(acs_j, (CHUNK, CHUNK)) - jnp.broadcast_to(acs_row, (CHUNK, CHUNK))
    return jnp.where(tri_mask, jnp.exp(jnp.where(tri_mask, dm, 0.0)), 0.0)


def ssd_fwd(act3, proj3, dtb, alog, dsk, *, name, side=None):
    b, s, _ = act3.shape
    nc = s // CHUNK
    act_spec, y_spec, dt_in_spec, _, par_spec, h_spec = _ssd_specs(nc, False)

    def body(act_ref, dtr_ref, dtb_ref, alog_ref, dsk_ref, y_ref, hp_ref, state):
        c = pl.program_id(1)

        @pl.when(c == 0)
        def _():
            state[...] = jnp.zeros_like(state)

        _, dt, _, acs = _ssd_common(dtr_ref, dtb_ref, alog_ref)
        acs_t = acs.T
        tri_mask = _tri(True)
        last_row = (lax.broadcasted_iota(jnp.int32, (CHUNK, 1), 0) == CHUNK - 1).astype(F32)
        for g in range(SSM_GROUPS):
            b16 = _bf(act_ref[0, :, _group_cols(g, 0)])
            c16 = _bf(act_ref[0, :, _group_cols(g, 1)])
            cb = _nt(c16, b16)
            for hs in _head_batches(g, SSD_INTERLEAVE_FWD):
                x = [act_ref[0, :, _head_cols(h)] for h in hs]
                dt_j = [dt[:, h:h + 1] for h in hs]
                acs_j = [acs[:, h:h + 1] for h in hs]
                acs_last = [_colsum(a * last_row) for a in acs_j]
                xg = _each(lambda xv, d: xv * d, x, dt_j)
                mm = [cb * _decay_matrix(a, acs_t[h:h + 1, :], tri_mask) for a, h in zip(acs_j, hs)]
                decay_s = _each(lambda al, a: jnp.exp(al - a), acs_last, acs_j)
                y_diag = _each(lambda m_, v: _nn(_bf(m_), _bf(v)), mm, xg)
                st = _each(lambda v, d: _tn(_bf(v * d), b16), xg, decay_s)
                hp = [state[h] for h in hs]
                for h, v in zip(hs, hp):
                    hp_ref[0, h, 0] = v
                y_off = [_nt(c16, _bf(v)) for v in hp]
                for h, yd, yo, a, xv in zip(hs, y_diag, y_off, acs_j, x):
                    y_ref[0, :, _head_cols(h)] = yd + yo * jnp.exp(a) + dsk_ref[:, h:h + 1] * xv
                for h, v, al, sv in zip(hs, hp, acs_last, st):
                    state[h] = v * jnp.exp(al) + sv

    call = SideCopy(side, n_in=5, n_out=2, grid=(b, nc))
    return pl.pallas_call(
        call.wrap(body), name=name, grid=(b, nc),
        in_specs=[act_spec, dt_in_spec, par_spec, par_spec, par_spec] + call.in_specs,
        out_specs=[y_spec, h_spec] + call.out_specs,
        out_shape=[jax.ShapeDtypeStruct((b, s, SSM_INNER), F32),
                   jax.ShapeDtypeStruct((b, SSM_HEADS, nc, SSM_P, D_STATE), F32)] + call.out_shape,
        scratch_shapes=[pltpu.VMEM((SSM_HEADS, SSM_P, D_STATE), F32)] + call.scratch,
        compiler_params=_params(("arbitrary", "arbitrary")),
    )(act3, proj3, dtb, alog, dsk, *call.args)


def ssd_bwd(act3, proj3, dtb, alog, dsk, hprev, dy3, *, name, side=None):
    b, s, _ = act3.shape
    nc = s // CHUNK
    act_spec, y_spec, dt_in_spec, dt_out_spec, par_spec, h_spec = _ssd_specs(nc, True)
    dpar_spec = pl.BlockSpec((8, LANE), lambda bi, c: (0, 0))

    def body(act_ref, dtr_ref, dtb_ref, alog_ref, dsk_ref, hp_ref, dy_ref, dact_ref, ddtr_ref, dpar_ref, dstate):
        bi, c = pl.program_id(0), pl.program_id(1)

        @pl.when(c == 0)
        def _():
            dstate[...] = jnp.zeros_like(dstate)

        @pl.when((bi == 0) & (c == 0))
        def _():
            dpar_ref[...] = jnp.zeros_like(dpar_ref)

        z, dt, aneg, acs = _ssd_common(dtr_ref, dtb_ref, alog_ref)
        acs_t = acs.T
        tri_mask = _tri(True)
        last_row = (lax.broadcasted_iota(jnp.int32, (CHUNK, 1), 0) == CHUNK - 1).astype(F32)
        lanes = lax.broadcasted_iota(jnp.int32, (1, LANE), 1)
        sublanes = lax.broadcasted_iota(jnp.int32, (LANE, 1), 0)
        ddt_mat = jnp.zeros((CHUNK, LANE), F32)
        dacs_mat = jnp.zeros((CHUNK, LANE), F32)
        dacs_rows = jnp.zeros((LANE, CHUNK), F32)
        ddsk_row = jnp.zeros((1, LANE), F32)
        for g in range(SSM_GROUPS):
            b16 = _bf(act_ref[0, :, _group_cols(g, 0)])
            c16 = _bf(act_ref[0, :, _group_cols(g, 1)])
            cb = _nt(c16, b16)
            dcb = jnp.zeros((CHUNK, CHUNK), F32)
            db_acc = jnp.zeros((CHUNK, D_STATE), F32)
            dc_acc = jnp.zeros((CHUNK, D_STATE), F32)
            for hs in _head_batches(g):
                x = [act_ref[0, :, _head_cols(h)] for h in hs]
                g_y = [dy_ref[0, :, _head_cols(h)] for h in hs]
                hp = [hp_ref[0, h, 0] for h in hs]
                g_hn = [dstate[h] for h in hs]
                dt_j = [dt[:, h:h + 1] for h in hs]
                acs_j = [acs[:, h:h + 1] for h in hs]
                acs_last = [_colsum(a * last_row) for a in acs_j]
                xg = _each(lambda xv, d: xv * d, x, dt_j)
                lm = [_decay_matrix(a, acs_t[h:h + 1, :], tri_mask) for a, h in zip(acs_j, hs)]
                mm = [cb * l for l in lm]
                decay_s = _each(lambda al, a: jnp.exp(al - a), acs_last, acs_j)
                ea = [jnp.exp(a) for a in acs_j]
                cd = [jnp.exp(al) for al in acs_last]
                g_y16, xg16, hp16, g_hn16 = [[_bf(v) for v in vs] for vs in (g_y, xg, hp, g_hn)]
                d_mm = _each(_nt, g_y16, xg16)
                d_xg = _each(lambda m_, gy: _tn(_bf(m_), gy), mm, g_y16)
                d_dm = _each(lambda a, m_: a * m_, d_mm, mm)
                d_acs = [_rowsum(v) for v in d_dm]
                t_off = [_nt(c16, v) for v in hp16]
                d_t16 = _each(lambda gy, e: _bf(gy * e), g_y, ea)
                d_acs = _each(lambda da, gy, t, e: da + _rowsum(gy * t) * e, d_acs, g_y, t_off, ea)
                d_hp = _each(lambda dtv, gh, cdv: _tn(dtv, c16) + gh * cdv, d_t16, g_hn, cd)
                d_w = [_nt(b16, v) for v in g_hn16]
                d_xg = _each(lambda dx, dw, ds: dx + dw * ds, d_xg, d_w, decay_s)
                d_ds = _each(lambda dw, v, ds: _rowsum(dw * v) * ds, d_w, xg, decay_s)
                d_last = _each(lambda gh, hv, cdv, dd: _colsum(_rowsum(gh * hv)) * cdv + _colsum(dd), g_hn, hp, cd, d_ds)
                d_acs = _each(lambda da, dd, dl: da - dd + dl * last_row, d_acs, d_ds, d_last)
                for h, gy, dx, d, xv in zip(hs, g_y, d_xg, dt_j, x):
                    dact_ref[0, :, _head_cols(h)] = dsk_ref[:, h:h + 1] * gy + dx * d
                for h, v in zip(hs, d_hp):
                    dstate[h] = v
                for k, h in enumerate(hs):
                    onehot = (lanes == h).astype(F32)
                    dcb = dcb + d_mm[k] * lm[k]
                    dc_acc = dc_acc + _nn(d_t16[k], hp16[k])
                    db_acc = db_acc + _nn(_bf(xg[k] * decay_s[k]), g_hn16[k])
                    ddsk_row = ddsk_row + _colsum(_rowsum(g_y[k] * x[k])) * onehot
                    ddt_mat = ddt_mat + _rowsum(d_xg[k] * x[k]) * onehot
                    dacs_mat = dacs_mat + d_acs[k] * onehot
                    dacs_rows = dacs_rows + (sublanes == h).astype(F32) * _colsum(d_dm[k])
            dcb16 = _bf(dcb)
            dact_ref[0, :, _group_cols(g, 1)] = dc_acc + _nn(dcb16, b16)
            dact_ref[0, :, _group_cols(g, 0)] = db_acc + _tn(dcb16, c16)
        d_a = _dot01_left(_tri(False).astype(BF16), dacs_mat - dacs_rows.T)
        ddt_mat = ddt_mat + d_a * aneg
        d_raw = ddt_mat * jax.nn.sigmoid(z)
        ddtr_ref[0] = d_raw
        dpar_ref[0:1, :] += _colsum(d_raw)
        dpar_ref[1:2, :] += _colsum(d_a * dt) * aneg
        dpar_ref[2:3, :] += ddsk_row

    call = SideCopy(side, n_in=7, n_out=3, grid=(b, nc))
    return pl.pallas_call(
        call.wrap(body), name=name, grid=(b, nc),
        in_specs=[act_spec, dt_in_spec, par_spec, par_spec, par_spec, h_spec, y_spec] + call.in_specs,
        out_specs=[act_spec, dt_out_spec, dpar_spec] + call.out_specs,
        out_shape=[jax.ShapeDtypeStruct(act3.shape, F32), jax.ShapeDtypeStruct((b, s, LANE), F32),
                   jax.ShapeDtypeStruct((8, LANE), F32)] + call.out_shape,
        scratch_shapes=[pltpu.VMEM((SSM_HEADS, SSM_P, D_STATE), F32)] + call.scratch,
        compiler_params=_params(("arbitrary", "arbitrary")),
    )(act3, proj3, dtb, alog, dsk, hprev, dy3, *call.args)


def rotary_tables(positions):
    inv_freq = ROPE_THETA ** (-jnp.arange(0, ROPE_DIM, 2, dtype=F32) / ROPE_DIM)
    ang = positions.astype(F32)[..., None] * inv_freq
    cos, sin = jnp.cos(ang), jnp.sin(ang)
    rest = HEAD_DIM - ROPE_DIM
    cosf = jnp.concatenate([cos, cos, jnp.ones(cos.shape[:2] + (rest,), F32)], axis=-1)
    sinf = jnp.concatenate([-sin, sin, jnp.zeros(sin.shape[:2] + (rest,), F32)], axis=-1)
    return cosf, sinf


def w_in_columns(w):
    pad = jnp.zeros((w.shape[0], IN_PAD - IN_PROJ), w.dtype)
    return jnp.concatenate([w[:, :Q_END], w[:, V_END:XBC_END], w[:, Q_END:V_END], w[:, XBC_END:], pad], axis=1)


def w_in_grad_columns(g):
    return jnp.concatenate([g[:, :Z_COL], g[:, K_COL:DT_COL], g[:, Z_COL:K_COL], g[:, DT_COL:DT_COL + SSM_HEADS]], axis=1)


def lane_pad(v):
    return jnp.pad(v.reshape(1, -1), ((0, 0), (0, LANE - v.shape[-1])))


def layer_fwd(h, wts, small, rope_tab, b, s, tag, attn_side=None, rest_from=None, ssd_side=None):
    w_in = wts[0]
    t = b * s
    sv = {"h": h}
    hn = rowwise_fwd(rms_fn, [h], [small["norm_mix"]], [BF16], name=f"rms_mix_{tag}")[0]
    proj = matmul(hn, w_in, name=f"in_proj_{tag}")
    sv["hn"], sv["proj"] = hn, proj
    proj3 = proj.reshape(b, s, IN_PAD)
    attn3, lse3, *attn_out = attn_fwd(proj3, rope_tab, name=f"attn_{tag}", side=attn_side)
    if rest_from is not None:
        wts = (w_in,) + tuple(rest_from(attn_out))
    _, w_out, w_gate, w_up, w_down = wts
    sv["attn3"], sv["lse3"] = attn3, lse3
    attn = attn3.reshape(t, ATTN_WIDTH)
    act3 = conv_silu_fwd(proj3, small["conv_w"], small["conv_b"], name=f"conv_{tag}")
    y3, hprev, *ssd_out = ssd_fwd(act3, proj3, small["dt_bias"], small["a_log"], small["d_skip"], name=f"ssd_{tag}",
                                  side=ssd_side)
    y = y3.reshape(t, SSM_INNER)
    sv["act3"], sv["hprev"], sv["y"] = act3, hprev, y
    gn = rowwise_fwd(gated_norm_fn, [y, proj], [small["ssm_norm"]], [BF16], name=f"gated_norm_{tag}", groups=SSM_GROUPS,
                     windows=[None, (Z_COL, SSM_INNER)])[0]
    sv["gn"] = gn
    h1 = matmul([attn, gn], w_out, name=f"out_proj_{tag}", residual=h)
    sv["h1"] = h1
    hn2 = rowwise_fwd(rms_fn, [h1], [small["norm_ffn"]], [BF16], name=f"rms_ffn_{tag}")[0]
    gate = matmul(hn2, w_gate, out_dtype=BF16, name=f"ffn_gate_{tag}")
    up = matmul(hn2, w_up, out_dtype=BF16, name=f"ffn_up_{tag}")
    act2 = rowwise_fwd(swiglu_fn, [gate, up], [], [BF16], name=f"swiglu_{tag}")[0]
    sv["hn2"], sv["gate"], sv["up"], sv["act2"] = hn2, gate, up, act2
    h2 = matmul(act2, w_down, name=f"ffn_down_{tag}", residual=h1)
    return h2, sv, wts, (ssd_out or None)


def layer_bwd(dh2, sv, wts, small, rope_tab, b, s, tag, ssd_side=None, attn_side_fn=None):
    w_in, w_out, w_gate, w_up, w_down = wts
    t = b * s
    gr = {}
    d_act2 = matmul(dh2, w_down, tb=True, out_dtype=BF16, name=f"ffn_down_dx_{tag}")
    gr["w_down"] = matmul(sv["act2"], dh2, ta=True, out_dtype=BF16, name=f"ffn_down_dw_{tag}")
    d_gate, d_up = rowwise_bwd(swiglu_fn, [sv["gate"], sv["up"]], [], [d_act2], [BF16, BF16], name=f"swiglu_bwd_{tag}")
    gr["w_gate"] = matmul(sv["hn2"], d_gate, ta=True, out_dtype=BF16, name=f"ffn_gate_dw_{tag}")
    gr["w_up"] = matmul(sv["hn2"], d_up, ta=True, out_dtype=BF16, name=f"ffn_up_dw_{tag}")
    d_hn2 = matmul(d_gate, w_gate, tb=True, name=f"ffn_gate_dx_{tag}")
    d_hn2 = matmul(d_up, w_up, tb=True, residual=d_hn2, name=f"ffn_up_dx_{tag}")
    dh1, gr["norm_ffn"] = rowwise_bwd(rms_fn, [sv["h1"]], [small["norm_ffn"]], [d_hn2], [F32],
                                      name=f"rms_ffn_bwd_{tag}", add_to_first=dh2)
    d_cat = matmul(dh1, w_out, tb=True, name=f"out_proj_dx_{tag}")
    gr["w_out"] = jnp.concatenate([
        matmul(sv["attn3"].reshape(t, ATTN_WIDTH), dh1, ta=True, out_dtype=BF16, name=f"out_proj_dw_attn_{tag}"),
        matmul(sv["gn"], dh1, ta=True, out_dtype=BF16, name=f"out_proj_dw_ssd_{tag}")], axis=0)
    d_y, d_z, gr["ssm_norm"] = rowwise_bwd(gated_norm_fn, [sv["y"], sv["proj"]], [small["ssm_norm"]], [d_cat], [F32, BF16],
                                           name=f"gated_norm_bwd_{tag}", groups=SSM_GROUPS,
                                           windows=[None, (Z_COL, SSM_INNER)], ct_windows=[(ATTN_WIDTH, SSM_INNER)])
    proj3 = sv["proj"].reshape(b, s, IN_PAD)
    d_act3, d_dtr, d_par, *ssd_out = ssd_bwd(sv["act3"], proj3, small["dt_bias"], small["a_log"], small["d_skip"],
                                             sv["hprev"], d_y.reshape(b, s, SSM_INNER), name=f"ssd_bwd_{tag}", side=ssd_side)
    gr["dt_bias"], gr["a_log"], gr["d_skip"] = d_par[0, :SSM_HEADS], d_par[1, :SSM_HEADS], d_par[2, :SSM_HEADS]
    d_xbc, gr["conv_w"], gr["conv_b"] = conv_silu_bwd(proj3, small["conv_w"], small["conv_b"], d_act3,
                                                      name=f"conv_bwd_{tag}")
    attn_side = attn_side_fn(gr) if attn_side_fn is not None else None
    d_q3, d_kv3, *attn_out = attn_bwd(proj3, rope_tab, sv["attn3"], sv["lse3"], d_cat.reshape(b, s, MIX_WIDTH),
                                      name=f"attn_bwd_{tag}", side=attn_side)
    d_proj = [d_q3.reshape(t, ATTN_WIDTH), d_z, d_xbc.reshape(t, CONV_CH), d_kv3.reshape(t, 2 * LANE),
              d_dtr.reshape(t, LANE)]
    d_hn = matmul(d_proj, w_in, tb=True, name=f"in_proj_dx_{tag}")
    gr["w_in"] = w_in_grad_columns(jnp.concatenate(
        [matmul(sv["hn"], part, ta=True, out_dtype=BF16, name=f"in_proj_dw_{k}_{tag}") for k, part in enumerate(d_proj)],
        axis=1))
    dh, gr["norm_mix"] = rowwise_bwd(rms_fn, [sv["h"]], [small["norm_mix"]], [d_hn], [F32],
                                     name=f"rms_mix_bwd_{tag}", add_to_first=dh1)
    return dh, gr, (ssd_out or None), (attn_out or None)


def local_step(x, positions, big, small_all, final_norm, loss_target, *, plan=None):
    b, s, _ = x.shape
    t = b * s
    rope_tab = jnp.concatenate(rotary_tables(positions), axis=-1)
    h = x.reshape(t, D_MODEL)
    saved, big = [], list(big)
    for l in range(DEPTH):
        kw = {}
        if plan is not None and l == 0:
            kw = dict(attn_side=(plan["rest0"], False), rest_from=plan["make_rest0"], ssd_side=(plan["late"], False))
        h, sv, big[l], got = layer_fwd(h, big[l], small_all[l], rope_tab, b, s, f"l{l}", **kw)
        if got is not None:
            big[DEPTH - 1] = plan["make_late"](got)
        saved.append(sv)
    dh, d_final, loss = loss_and_grad(h, loss_target.reshape(t, D_MODEL), final_norm.reshape(1, D_MODEL))
    grads, received = [None] * DEPTH, {}
    for l in reversed(range(DEPTH)):
        kw = {}
        if plan is not None and l == 0:
            kw = dict(ssd_side=(plan["grads_late"](grads[DEPTH - 1]), True),
                      attn_side_fn=lambda gr: (plan["grads_rest0"](gr), True))
        dh, grads[l], got_ssd, got_attn = layer_bwd(dh, saved[l], big[l], small_all[l], rope_tab, b, s, f"l{l}", **kw)
        if got_ssd is not None:
            received["late"] = got_ssd
        if got_attn is not None:
            received["rest0"] = got_attn
    return loss, dh.reshape(b, s, D_MODEL), grads, d_final, received


def _slab_rows(r):
    return r if r <= 512 else _pick(r, (512, 352, 256, 128, 8))


def cast_bf16(x, *, name):
    def fn(v):
        return (v,)
    return rowwise_fwd(fn, [x], [], [BF16], name=name, tr=_slab_rows(x.shape[0]))[0]


def sum_slots(x, *, name):
    n, r, c = x.shape
    tr = _slab_rows(r)

    def body(x_ref, o_ref):
        acc = x_ref[0].astype(F32)
        for i in range(1, n):
            acc = acc + x_ref[i].astype(F32)
        o_ref[...] = acc

    return pl.pallas_call(
        body, name=name, grid=(r // tr,), in_specs=[pl.BlockSpec((n, tr, c), lambda i: (0, i, 0))],
        out_specs=pl.BlockSpec((tr, c), lambda i: (i, 0)), out_shape=jax.ShapeDtypeStruct((r, c), F32),
        compiler_params=_params(("parallel",)),
    )(x)


def adamw(g, w, m, v, *, name):
    r, c = w.shape
    tr = _slab_rows(r)
    bc1 = 1.0 / (1.0 - ADAM_B1 ** ADAM_STEP)
    bc2 = 1.0 / (1.0 - ADAM_B2 ** ADAM_STEP)

    def body(g_ref, w_ref, m_ref, v_ref, g_out, d_out, m_out, v_out):
        gv = g_ref[...]
        m_new = ADAM_B1 * m_ref[...] + (1.0 - ADAM_B1) * gv
        v_new = ADAM_B2 * v_ref[...] + (1.0 - ADAM_B2) * (gv * gv)
        g_out[...] = gv
        m_out[...] = m_new
        v_out[...] = v_new
        d_out[...] = -ADAM_LR * ((m_new * bc1) / (jnp.sqrt(v_new * bc2) + ADAM_EPS) + ADAM_WD * w_ref[...])

    spec = pl.BlockSpec((tr, c), lambda i: (i, 0))
    return pl.pallas_call(
        body, name=name, grid=(r // tr,), in_specs=[spec] * 4, out_specs=[spec] * 4,
        out_shape=[jax.ShapeDtypeStruct((r, c), F32)] * 4, compiler_params=_params(("parallel",)),
    )(g, w, m, v)


def adamw_layers(g_parts, w, m, v, *, name):
    depth, a, b = w.shape
    tr = _pick(a, (256, 352, 192, 128, 8))
    counts = [len(p) for p in g_parts]
    flat_parts = [q for p in g_parts for q in p]
    bc1 = 1.0 / (1.0 - ADAM_B1 ** ADAM_STEP)
    bc2 = 1.0 / (1.0 - ADAM_B2 ** ADAM_STEP)

    def body(*refs):
        layer = pl.program_id(0)
        g, off = None, 0
        for l, cnt in enumerate(counts):
            g_l = refs[off][...]
            for r_ in refs[off + 1:off + cnt]:
                g_l = g_l + r_[...]
            off += cnt
            g = g_l if g is None else jnp.where(layer == l, g_l, g)
        w_ref, m_ref, v_ref, g_out, d_out, m_out, v_out = refs[off:]
        m_new = ADAM_B1 * m_ref[0] + (1.0 - ADAM_B1) * g
        v_new = ADAM_B2 * v_ref[0] + (1.0 - ADAM_B2) * (g * g)
        g_out[0] = g
        m_out[0] = m_new
        v_out[0] = v_new
        d_out[0] = -ADAM_LR * ((m_new * bc1) / (jnp.sqrt(v_new * bc2) + ADAM_EPS) + ADAM_WD * w_ref[0])

    g_spec = pl.BlockSpec((tr, b), lambda l, i: (i, 0))
    spec = pl.BlockSpec((1, tr, b), lambda l, i: (l, i, 0))
    return pl.pallas_call(
        body, name=name, grid=(depth, a // tr), in_specs=[g_spec] * len(flat_parts) + [spec] * 3, out_specs=[spec] * 4,
        out_shape=[jax.ShapeDtypeStruct(w.shape, F32)] * 4, compiler_params=_params(("parallel", "parallel")),
    )(*flat_parts, w, m, v)


SWAP_PIECES = 4


def _other_chips(x, y):
    return [(1 - x, y), (x, 1 - y), (1 - x, 1 - y)]


def allgather_chips(shards):
    n_arr = len(shards)

    def body(*refs):
        in_refs, out_refs = refs[:n_arr], refs[n_arr:2 * n_arr]
        send_sems, recv_sems, local_sems = refs[2 * n_arr:]
        x, y, c = lax.axis_index("x"), lax.axis_index("y"), lax.axis_index("c")
        chip = 2 * x + y
        started = []
        for a, (in_ref, out_ref) in enumerate(zip(in_refs, out_refs)):
            mine = pltpu.make_async_copy(in_ref, out_ref.at[chip], local_sems.at[a])
            mine.start()
            started.append(mine.wait)
            for k, (px, py) in enumerate(_other_chips(x, y)):
                cp = pltpu.make_async_remote_copy(src_ref=in_ref, dst_ref=out_ref.at[chip], send_sem=send_sems.at[3 * a + k],
                                                  recv_sem=recv_sems.at[3 * a + k], device_id=(px, py, c), device_id_type=MESH)
                cp.start()
                started.append(cp.wait_send)
        for a, (in_ref, out_ref) in enumerate(zip(in_refs, out_refs)):
            for k, (px, py) in enumerate(_other_chips(x, y)):
                pltpu.make_async_remote_copy(src_ref=in_ref, dst_ref=out_ref.at[2 * px + py], send_sem=send_sems.at[3 * a + k],
                                             recv_sem=recv_sems.at[3 * a + k], device_id=(px, py, c),
                                             device_id_type=MESH).wait_recv()
        for wait in started:
            wait()

    hbm = pl.BlockSpec(memory_space=pltpu.HBM)
    return pl.pallas_call(
        body, name="allgather_weights", in_specs=[hbm] * n_arr, out_specs=[hbm] * n_arr,
        out_shape=[jax.ShapeDtypeStruct((N_CHIPS,) + s.shape, s.dtype) for s in shards],
        scratch_shapes=[pltpu.SemaphoreType.DMA((3 * n_arr,)), pltpu.SemaphoreType.DMA((3 * n_arr,)),
                        pltpu.SemaphoreType.DMA((n_arr,))],
    )(*shards)


def exchange_grads(big, small):
    def body(big_ref, small_ref, big_out, small_out, send_sems, recv_sems, local_sems):
        x, y, c = lax.axis_index("x"), lax.axis_index("y"), lax.axis_index("c")
        chip = 2 * x + y
        dev = 4 * x + 2 * y + c
        own_big = pltpu.make_async_copy(big_ref.at[chip], big_out.at[chip], local_sems.at[0])
        own_small = pltpu.make_async_copy(small_ref, small_out.at[dev], local_sems.at[1])
        own_big.start()
        own_small.start()
        sends = []
        for k, (px, py) in enumerate(_other_chips(x, y)):
            cp = pltpu.make_async_remote_copy(src_ref=big_ref.at[2 * px + py], dst_ref=big_out.at[chip],
                                              send_sem=send_sems.at[k], recv_sem=recv_sems.at[k],
                                              device_id=(px, py, c), device_id_type=MESH)
            cp.start()
            sends.append(cp)
        peers = []
        for r in range(1, N_DEV):
            fx, fy, fc = (r >> 2) & 1, (r >> 1) & 1, r & 1
            px, py, pc = (x + fx) % 2, (y + fy) % 2, (c + fc) % 2
            peers.append((px, py, pc))
            cp = pltpu.make_async_remote_copy(src_ref=small_ref, dst_ref=small_out.at[dev], send_sem=send_sems.at[2 + r],
                                              recv_sem=recv_sems.at[2 + r], device_id=(px, py, pc), device_id_type=MESH)
            cp.start()
            sends.append(cp)
        for k, (px, py) in enumerate(_other_chips(x, y)):
            pltpu.make_async_remote_copy(src_ref=big_ref.at[chip], dst_ref=big_out.at[2 * px + py],
                                         send_sem=send_sems.at[k], recv_sem=recv_sems.at[k],
                                         device_id=(px, py, c), device_id_type=MESH).wait_recv()
        for r, (px, py, pc) in zip(range(1, N_DEV), peers):
            pltpu.make_async_remote_copy(src_ref=small_ref, dst_ref=small_out.at[4 * px + 2 * py + pc],
                                         send_sem=send_sems.at[2 + r], recv_sem=recv_sems.at[2 + r],
                                         device_id=(px, py, pc), device_id_type=MESH).wait_recv()
        for cp in sends:
            cp.wait_send()
        own_big.wait()
        own_small.wait()

    hbm = pl.BlockSpec(memory_space=pltpu.HBM)
    n_sem = 3 + N_DEV - 1
    return pl.pallas_call(
        body, name="exchange_grads", in_specs=[hbm, hbm], out_specs=[hbm, hbm],
        out_shape=[jax.ShapeDtypeStruct(big.shape, big.dtype), jax.ShapeDtypeStruct((N_DEV,) + small.shape, small.dtype)],
        scratch_shapes=[pltpu.SemaphoreType.DMA((n_sem,)), pltpu.SemaphoreType.DMA((n_sem,)), pltpu.SemaphoreType.DMA((2,))],
    )(big, small)


def swap_cores_list(arrays):
    n = len(arrays)

    def body(*refs):
        ins, outs, (send_sems, recv_sems) = refs[:n], refs[n:2 * n], refs[2 * n:]
        x, y, c = lax.axis_index("x"), lax.axis_index("y"), lax.axis_index("c")
        copies = []
        for k in range(n):
            rows = ins[k].shape[0] // SWAP_PIECES
            for p in range(SWAP_PIECES):
                part = pl.ds(p * rows, rows)
                copies.append(pltpu.make_async_remote_copy(
                    src_ref=ins[k].at[part], dst_ref=outs[k].at[part], send_sem=send_sems.at[k * SWAP_PIECES + p],
                    recv_sem=recv_sems.at[k * SWAP_PIECES + p], device_id=(x, y, 1 - c), device_id_type=MESH))
        for cp in copies:
            cp.start()
        for cp in copies:
            cp.wait_recv()
        for cp in copies:
            cp.wait_send()

    assert all(a.shape[0] % (8 * SWAP_PIECES) == 0 for a in arrays)
    hbm = pl.BlockSpec(memory_space=pltpu.HBM)
    return pl.pallas_call(
        body, name="swap_cores", in_specs=[hbm] * n, out_specs=[hbm] * n,
        out_shape=[jax.ShapeDtypeStruct(a.shape, a.dtype) for a in arrays],
        scratch_shapes=[pltpu.SemaphoreType.DMA((n * SWAP_PIECES,)), pltpu.SemaphoreType.DMA((n * SWAP_PIECES,))],
    )(*arrays)


BIG_NAMES = ("w_in", "w_out", "w_gate", "w_up", "w_down")
BIG_SHARD_AXIS = {"w_in": 1, "w_out": 0, "w_gate": 1, "w_up": 1, "w_down": 0}
SMALL_NAMES = ("norm_mix", "conv_w", "conv_b", "dt_bias", "a_log", "d_skip", "ssm_norm", "norm_ffn")


def pack_small(parts):
    flat = jnp.concatenate([p.reshape(-1).astype(F32) for p in parts])
    rows = -(-flat.size // LANE)
    rows = -(-rows // 8) * 8
    return jnp.pad(flat, (0, rows * LANE - flat.size)).reshape(rows, LANE)


def unpack_small(packed, like):
    out, off = [], 0
    flat = packed.reshape(-1)
    for a in like:
        out.append(flat[off:off + a.size].reshape(a.shape))
        off += a.size
    return out


def kernel(x, positions, norm_mix, w_in, conv_w, conv_b, dt_bias, a_log, d_skip, ssm_norm, w_out, norm_ffn, w_gate, w_up, w_down, final_norm, loss_target, m_norm_mix, m_w_in, m_conv_w, m_conv_b, m_dt_bias, m_a_log, m_d_skip, m_ssm_norm, m_w_out, m_norm_ffn, m_w_gate, m_w_up, m_w_down, m_final_norm, v_norm_mix, v_w_in, v_conv_w, v_conv_b, v_dt_bias, v_a_log, v_d_skip, v_ssm_norm, v_w_out, v_norm_ffn, v_w_gate, v_w_up, v_w_down, v_final_norm):
    chip = 2 * lax.axis_index("x") + lax.axis_index("y")
    w_sh = {"w_in": w_in, "w_out": w_out, "w_gate": w_gate, "w_up": w_up, "w_down": w_down}
    m_sh = {"w_in": m_w_in, "w_out": m_w_out, "w_gate": m_w_gate, "w_up": m_w_up, "w_down": m_w_down}
    v_sh = {"w_in": v_w_in, "w_out": v_w_out, "w_gate": v_w_gate, "w_up": v_w_up, "w_down": v_w_down}
    assert DEPTH == 2
    rest = BIG_NAMES[1:]

    w16 = {n: cast_bf16(w_sh[n].reshape(-1, w_sh[n].shape[-1]), name=f"cast_{n}").reshape(w_sh[n].shape) for n in BIG_NAMES}

    def joined(n, gathered):
        if BIG_SHARD_AXIS[n] == 0:
            full = gathered.reshape(-1, gathered.shape[-1])
        else:
            full = jnp.concatenate([gathered[j] for j in range(N_CHIPS)], axis=1)
        return w_in_columns(full) if n == "w_in" else full

    def per_chip(n, g):
        if BIG_SHARD_AXIS[n] == 0:
            return g.reshape(N_CHIPS, -1, g.shape[-1])
        return jnp.stack(jnp.split(g, N_CHIPS, axis=1))

    conv_cols = CONV_CH // N_CHIPS
    gathered_in0, conv_g = allgather_chips([w16["w_in"][0], conv_w.reshape(-1, LANE)])
    big = [(joined("w_in", gathered_in0),) + (None,) * len(rest), None]
    plan = {
        "rest0": [w16[n][0] for n in rest],
        "make_rest0": lambda gs: tuple(joined(n, g) for n, g in zip(rest, gs)),
        "late": [w16[n][DEPTH - 1] for n in BIG_NAMES],
        "make_late": lambda gs: tuple(joined(n, g) for n, g in zip(BIG_NAMES, gs)),
        "grads_late": lambda gr: [per_chip(n, gr[n]) for n in BIG_NAMES],
        "grads_rest0": lambda gr: [per_chip(n, gr[n]) for n in rest],
    }
    conv_w_full = jnp.concatenate([conv_g[j].reshape(DEPTH, CONV_WIDTH, conv_cols) for j in range(N_CHIPS)], axis=2)
    small_all = []
    for l in range(DEPTH):
        small_all.append({
            "norm_mix": norm_mix[l].reshape(1, -1), "conv_w": conv_w_full[l], "conv_b": conv_b[l].reshape(1, -1),
            "dt_bias": lane_pad(dt_bias[l]), "a_log": lane_pad(a_log[l]), "d_skip": lane_pad(d_skip[l]),
            "ssm_norm": ssm_norm[l].reshape(1, -1), "norm_ffn": norm_ffn[l].reshape(1, -1)})

    loss_part, grad_x, grads, d_final, received = local_step(x, positions, big, small_all, final_norm, loss_target, plan=plan)

    small_parts = [jnp.stack([grads[l][n].reshape(-1) for l in range(DEPTH)]) for n in SMALL_NAMES]
    small_parts += [d_final.reshape(-1), loss_part.reshape(-1)]
    recv_in0, recv_small = exchange_grads(per_chip("w_in", grads[0]["w_in"]), pack_small(small_parts))
    recv = [dict(zip(BIG_NAMES, [recv_in0] + list(received["rest0"]))), dict(zip(BIG_NAMES, received["late"]))]
    keys = [(l, n) for l in range(DEPTH) for n in BIG_NAMES]
    mine = {(l, n): sum_slots(recv[l][n], name=f"sum_partials_{n}_l{l}") for l, n in keys}
    other = dict(zip(keys, swap_cores_list([mine[k] for k in keys])))

    g_big, d_big, m_big, v_big = {}, {}, {}, {}
    for n in BIG_NAMES:
        g_big[n], d_big[n], m_big[n], v_big[n] = adamw_layers([[mine[(l, n)], other[(l, n)]] for l in range(DEPTH)],
                                                              w_sh[n], m_sh[n], v_sh[n], name=f"adamw_{n}")

    small_sum = sum_slots(recv_small, name="sum_small")
    like = [norm_mix, conv_w_full, conv_b, dt_bias, a_log, d_skip, ssm_norm, norm_ffn, final_norm, loss_part.reshape(-1)]
    g_small = unpack_small(small_sum, like)
    loss = g_small[-1][0]
    g_small = dict(zip(SMALL_NAMES + ("final_norm",), g_small[:-1]))
    g_small["conv_w"] = lax.dynamic_slice_in_dim(g_small["conv_w"], chip * conv_cols, conv_cols, axis=2)
    w_small = {"norm_mix": norm_mix, "conv_w": conv_w, "conv_b": conv_b, "dt_bias": dt_bias, "a_log": a_log, "d_skip": d_skip,
               "ssm_norm": ssm_norm, "norm_ffn": norm_ffn, "final_norm": final_norm}
    m_small = {"norm_mix": m_norm_mix, "conv_w": m_conv_w, "conv_b": m_conv_b, "dt_bias": m_dt_bias, "a_log": m_a_log,
               "d_skip": m_d_skip, "ssm_norm": m_ssm_norm, "norm_ffn": m_norm_ffn, "final_norm": m_final_norm}
    v_small = {"norm_mix": v_norm_mix, "conv_w": v_conv_w, "conv_b": v_conv_b, "dt_bias": v_dt_bias, "a_log": v_a_log,
               "d_skip": v_d_skip, "ssm_norm": v_ssm_norm, "norm_ffn": v_norm_ffn, "final_norm": v_final_norm}
    names = SMALL_NAMES + ("final_norm",)
    order = [w_small[n] for n in names]
    res = adamw(pack_small([g_small[n] for n in names]), pack_small(order), pack_small([m_small[n] for n in names]),
                pack_small([v_small[n] for n in names]), name="adamw_small")
    g_s, d_s, m_s, v_s = (dict(zip(names, unpack_small(a, order))) for a in res)

    all_names = ("norm_mix", "w_in", "conv_w", "conv_b", "dt_bias", "a_log", "d_skip", "ssm_norm", "w_out", "norm_ffn",
                 "w_gate", "w_up", "w_down", "final_norm")
    outs = [loss, grad_x]
    for src_big, src_small in ((g_big, g_s), (d_big, d_s), (m_big, m_s), (v_big, v_s)):
        outs += [src_big[n] if n in BIG_NAMES else src_small[n] for n in all_names]
    return tuple(outs)
```

```python
import functools

import jax
import jax.numpy as jnp
from jax import lax
from jax.experimental import pallas as pl
from jax.experimental.pallas import tpu as pltpu

F32 = jnp.float32
BF16 = jnp.bfloat16
MESH = pl.DeviceIdType.MESH

D_MODEL = 1024
DEPTH = 2
HEAD_DIM = 64
N_Q_HEADS = 8
N_KV_HEADS = 2
GQA = N_Q_HEADS // N_KV_HEADS
ATTN_WIDTH = N_Q_HEADS * HEAD_DIM
ROPE_DIM = HEAD_DIM // 4
ROPE_HALF = ROPE_DIM // 2
ROPE_THETA = 500000.0
DILATIONS = (1, 4, 16)
ATTN_BLOCK = 128
SSM_P = 64
SSM_HEADS = 16
SSM_INNER = SSM_HEADS * SSM_P
SSM_GROUPS = 2
HEADS_PER_GROUP = SSM_HEADS // SSM_GROUPS
D_STATE = 128
CONV_WIDTH = 4
CHUNK = 128
CONV_CH = SSM_INNER + 2 * SSM_GROUPS * D_STATE
MIX_WIDTH = ATTN_WIDTH + SSM_INNER
Q_END = ATTN_WIDTH
K_END = Q_END + N_KV_HEADS * HEAD_DIM
V_END = K_END + N_KV_HEADS * HEAD_DIM
Z_END = V_END + SSM_INNER
XBC_END = Z_END + CONV_CH
IN_PROJ = XBC_END + SSM_HEADS
LANE = 128
IN_PAD = XBC_END + LANE
Q_COL, Z_COL, XBC_COL, K_COL, V_COL, DT_COL = 0, 512, 1536, 3072, 3200, 3328
EPS = 1e-5
ADAM_LR, ADAM_B1, ADAM_B2, ADAM_EPS, ADAM_WD, ADAM_STEP = 0.001, 0.9, 0.999, 1e-8, 0.01, 10
N_CHIPS = 4
N_DEV = 8
VMEM_LIMIT = 48 * 1024 * 1024
NEG_BIG = -1e30


def _params(sem=None):
    return pltpu.CompilerParams(dimension_semantics=sem, vmem_limit_bytes=VMEM_LIMIT)


def _pick(n, prefs):
    for p in prefs:
        if n % p == 0:
            return p
    return n


def matmul(a, b, *, name, ta=False, tb=False, out_dtype=F32, residual=None):
    if ta:
        assert not tb and residual is None
        return _matmul_over_rows(a, b, name=name, out_dtype=out_dtype)
    return _matmul_full_k(a, b, name=name, tb=tb, out_dtype=out_dtype, residual=residual)


def _matmul_full_k(a, b, *, name, tb, out_dtype, residual):
    a_parts = list(a) if isinstance(a, (list, tuple)) else [a]
    n_a = len(a_parts)
    m = a_parts[0].shape[0]
    kdim = sum(p.shape[1] for p in a_parts)
    wide = kdim > 1536 or any(p.dtype == F32 for p in a_parts)
    n = b.shape[0] if tb else b.shape[1]
    tm = _pick(m, (512, 256)) if wide else _pick(m, (1024, 512, 256))
    tn = _pick(n, (1152, 1408, 1536, 1024, 768, 512, 384, 256, 128))
    b_spec = pl.BlockSpec((tn, kdim), lambda i, j: (j, 0)) if tb else pl.BlockSpec((kdim, tn), lambda i, j: (0, j))
    o_spec = pl.BlockSpec((tm, tn), lambda i, j: (i, j))
    dims = (((1,), (1 if tb else 0,)), ((), ()))
    has_res = residual is not None

    def body(*refs):
        b_ref, o_ref = refs[n_a], refs[-1]
        pieces = [r[...].astype(BF16) for r in refs[:n_a]]
        av = pieces[0] if n_a == 1 else jnp.concatenate(pieces, axis=1)
        r = lax.dot_general(av, b_ref[...].astype(BF16), dims, preferred_element_type=F32)
        if has_res:
            r = r + refs[n_a + 1][...]
        o_ref[...] = r.astype(out_dtype)

    in_specs = ([pl.BlockSpec((tm, p.shape[1]), lambda i, j: (i, 0)) for p in a_parts] + [b_spec]
                + ([o_spec] if has_res else []))
    args = tuple(a_parts) + (b,) + ((residual,) if has_res else ())
    return pl.pallas_call(
        body, name=name, grid=(m // tm, n // tn), in_specs=in_specs, out_specs=o_spec,
        out_shape=jax.ShapeDtypeStruct((m, n), out_dtype),
        compiler_params=_params(("parallel", "parallel")),
    )(*args)


def _matmul_over_rows(a, b, *, name, out_dtype):
    t, m = a.shape
    n = b.shape[1]
    tm = _pick(m, (1024, 1408, 768, 512, 256, 128))
    tn = _pick(n, (1152, 1408, 1024, 768, 512, 384, 256, 128))
    tk = _pick(t, (1024, 512, 256, 128))
    nk = t // tk

    def body(a_ref, b_ref, o_ref, acc):
        k = pl.program_id(2)
        part = lax.dot_general(a_ref[...].astype(BF16), b_ref[...].astype(BF16), (((0,), (0,)), ((), ())),
                               preferred_element_type=F32)

        @pl.when(k == 0)
        def _():
            acc[...] = part

        @pl.when(k > 0)
        def _():
            acc[...] += part

        @pl.when(k == nk - 1)
        def _():
            o_ref[...] = acc[...].astype(out_dtype)

    return pl.pallas_call(
        body, name=name, grid=(m // tm, n // tn, nk),
        in_specs=[pl.BlockSpec((tk, tm), lambda i, j, k: (k, i)), pl.BlockSpec((tk, tn), lambda i, j, k: (k, j))],
        out_specs=pl.BlockSpec((tm, tn), lambda i, j, k: (i, j)),
        out_shape=jax.ShapeDtypeStruct((m, n), out_dtype),
        scratch_shapes=[pltpu.VMEM((tm, tn), F32)],
        compiler_params=_params(("parallel", "parallel", "arbitrary")),
    )(a, b)


ROW_BLOCK_BYTES = 32 * 1024 * 1024


def _row_tile(t, tr, widths, n_copies):
    lanes = sum(-(-wd // LANE) * LANE for wd in widths) * n_copies
    tr = min(tr, t)
    while tr > 8 and tr * lanes * 4 > ROW_BLOCK_BYTES:
        tr //= 2
    return tr


def _row_widths(rows, groups, windows):
    windows = windows or [None] * len(rows)
    widths = [(w[1] if w else a.shape[1]) // groups for a, w in zip(rows, windows)]
    assert all(w is None or w[0] % wd == 0 for w, wd in zip(windows, widths))
    return widths, [(w[0] // wd if w else 0) for w, wd in zip(windows, widths)]


def _row_specs(tr, widths, offs):
    return [pl.BlockSpec((tr, wd), functools.partial(lambda g, i, off: (i, g + off), off=off)) for wd, off in zip(widths, offs)]


def rowwise_fwd(fn, rows, params, out_dtypes, *, name, tr=512, groups=1, windows=None):
    t = rows[0].shape[0]
    widths, offs = _row_widths(rows, groups, windows)
    tr = _row_tile(t, tr, widths, 2)
    row_specs = _row_specs(tr, widths, offs)
    par_spec = lambda p: pl.BlockSpec((1, p.shape[1] // groups), lambda g, i: (0, g))
    n_in = len(rows) + len(params)
    out_cols = [o.shape[1] for o in jax.eval_shape(
        fn, *[jax.ShapeDtypeStruct((tr, wd), F32) for wd in widths],
        *[jax.ShapeDtypeStruct((1, p.shape[1] // groups), F32) for p in params])]

    def body(*refs):
        vals = [r[...].astype(F32) for r in refs[:n_in]]
        outs = fn(*vals)
        for o_ref, o in zip(refs[n_in:], outs):
            o_ref[...] = o.astype(o_ref.dtype)

    return pl.pallas_call(
        body, name=name, grid=(groups, t // tr),
        in_specs=row_specs + [par_spec(p) for p in params],
        out_specs=[pl.BlockSpec((tr, c), lambda g, i: (i, g)) for c in out_cols],
        out_shape=[jax.ShapeDtypeStruct((t, c * groups), d) for c, d in zip(out_cols, out_dtypes)],
        compiler_params=_params(("arbitrary", "arbitrary")),
    )(*rows, *params)


def rowwise_bwd(fn, rows, params, cts, drow_dtypes, *, name, tr=512, groups=1, add_to_first=None, windows=None,
                ct_windows=None):
    t = rows[0].shape[0]
    widths, offs = _row_widths(rows, groups, windows)
    ct_widths, ct_offs = _row_widths(cts, groups, ct_windows)
    tr = _row_tile(t, tr, widths + ct_widths, 2)
    row_spec = lambda a: pl.BlockSpec((tr, a.shape[1] // groups), lambda g, i: (i, g))
    row_specs = _row_specs(tr, widths, offs)
    par_spec = lambda p: pl.BlockSpec((1, p.shape[1] // groups), lambda g, i: (0, g))
    n_rows, n_par, n_ct = len(rows), len(params), len(cts)
    has_add = add_to_first is not None
    n_in = n_rows + n_par + n_ct + (1 if has_add else 0)

    def body(*refs):
        i = pl.program_id(1)
        vals = [r[...].astype(F32) for r in refs[:n_rows + n_par]]
        ct_vals = tuple(r[...].astype(F32) for r in refs[n_rows + n_par:n_rows + n_par + n_ct])
        _, vjp = jax.vjp(fn, *vals)
        grads = vjp(ct_vals)
        out_refs = refs[n_in:]
        for idx in range(n_rows):
            g = grads[idx]
            if idx == 0 and has_add:
                g = g + refs[n_in - 1][...]
            out_refs[idx][...] = g.astype(out_refs[idx].dtype)
        for idx in range(n_par):
            p_ref = out_refs[n_rows + idx]

            @pl.when(i == 0)
            def _():
                p_ref[...] = jnp.zeros_like(p_ref)

            p_ref[...] += grads[n_rows + idx]

    ins = list(rows) + list(params) + list(cts) + ([add_to_first] if has_add else [])
    in_specs = (row_specs + [par_spec(p) for p in params] + _row_specs(tr, ct_widths, ct_offs)
                + ([row_spec(add_to_first)] if has_add else []))
    return pl.pallas_call(
        body, name=name, grid=(groups, t // tr), in_specs=in_specs,
        out_specs=[pl.BlockSpec((tr, wd), lambda g, i: (i, g)) for wd in widths] + [par_spec(p) for p in params],
        out_shape=[jax.ShapeDtypeStruct((t, wd * groups), d) for wd, d in zip(widths, drow_dtypes)]
        + [jax.ShapeDtypeStruct(p.shape, F32) for p in params],
        compiler_params=_params(("arbitrary", "arbitrary")),
    )(*ins)


def rms_fn(x, w):
    return (x * lax.rsqrt(jnp.mean(x * x, axis=-1, keepdims=True) + EPS) * w,)


def swiglu_fn(g, u):
    return (g * jax.nn.sigmoid(g) * u,)


def gated_norm_fn(y, z, w):
    v = y * (z * jax.nn.sigmoid(z))
    return (v * lax.rsqrt(jnp.mean(v * v, axis=-1, keepdims=True) + EPS) * w,)


def loss_and_grad(h, target, w, *, tr=512):
    t, d = h.shape

    def loss_fn(hv, wv, tv):
        err = rms_fn(hv, wv)[0] - tv
        per_row = jnp.mean(err * err, axis=-1, keepdims=True)
        return 0.5 * jnp.sum(per_row, axis=0, keepdims=True)

    def body(h_ref, t_ref, w_ref, dh_ref, dw_ref, loss_ref):
        i = pl.program_id(0)

        @pl.when(i == 0)
        def _():
            dw_ref[...] = jnp.zeros_like(dw_ref)
            loss_ref[...] = jnp.zeros_like(loss_ref)

        tv = t_ref[...]
        val, vjp = jax.vjp(lambda hv, wv: loss_fn(hv, wv, tv), h_ref[...], w_ref[...])
        dh, dw = vjp(jnp.ones((1, 1), F32))
        dh_ref[...] = dh
        dw_ref[...] += dw
        loss_ref[...] += jnp.broadcast_to(val, loss_ref.shape)

    row = pl.BlockSpec((tr, d), lambda i: (i, 0))
    par = pl.BlockSpec((1, d), lambda i: (0, 0))
    return pl.pallas_call(
        body, name="loss_and_grad", grid=(t // tr,), in_specs=[row, row, par],
        out_specs=[row, par, pl.BlockSpec((1, LANE), lambda i: (0, 0))],
        out_shape=[jax.ShapeDtypeStruct((t, d), F32), jax.ShapeDtypeStruct((1, d), F32),
                   jax.ShapeDtypeStruct((1, LANE), F32)],
        compiler_params=_params(("arbitrary",)),
    )(h, target, w)


def _split3(x):
    hi = x.astype(BF16)
    r1 = x - hi.astype(F32)
    mid = r1.astype(BF16)
    lo = (r1 - mid.astype(F32)).astype(BF16)
    return hi, mid, lo


def _dot01_left(m01, x):
    return sum(jnp.dot(m01, p, preferred_element_type=F32) for p in _split3(x))


def _dot01_right(x, m01):
    return sum(jnp.dot(p, m01, preferred_element_type=F32) for p in _split3(x))


ATTN_PAD = ATTN_BLOCK * DILATIONS[-1]
Q_GROUP_W = GQA * HEAD_DIM
ATTN_VMEM_LIMIT = 56 * 1024 * 1024
HALF_W = 2 * HEAD_DIM
N_HALF = Q_GROUP_W // HALF_W
_ATTN_BIAS_BUF = pltpu.VMEM((2, GQA * ATTN_BLOCK, 2 * ATTN_BLOCK), F32)


def _attn_mask(n):
    rows = GQA * ATTN_BLOCK
    qi = lax.broadcasted_iota(jnp.int32, (rows, 2 * ATTN_BLOCK), 0) % ATTN_BLOCK
    ki = lax.broadcasted_iota(jnp.int32, (rows, 2 * ATTN_BLOCK), 1)
    delta = qi + ATTN_BLOCK - ki
    return (delta >= 0) & (delta <= ATTN_BLOCK) & ((n - 1) * ATTN_BLOCK + ki >= 0)


def _attn_bias(bias_s):
    for first in (0, 1):
        bias_s[first] = jnp.where(_attn_mask(first), 0.0, NEG_BIG)


def _rope(x, cos_v, sin_v, swap, scale, adjoint):
    if adjoint:
        return (x * cos_v + _dot01_right(x * sin_v, swap)) * scale
    return (x * cos_v + _dot01_right(x, swap) * sin_v) * scale


def _swap_matrix():
    c = HEAD_DIM
    ci = lax.broadcasted_iota(jnp.int32, (c, c), 0)
    cj = lax.broadcasted_iota(jnp.int32, (c, c), 1)
    swap = ((cj == ci + ROPE_HALF) & (ci < ROPE_HALF)) | ((cj == ci - ROPE_HALF) & (ci >= ROPE_HALF) & (ci < ROPE_DIM))
    return swap.astype(BF16)


def _attn_blocks(s_len):
    out = []
    for i, d in enumerate(DILATIONS):
        nb = s_len // (ATTN_BLOCK * d)
        for r in range(d):
            for n in range(nb):
                start = r + d * ATTN_BLOCK * n
                out.append((i, d, start, ATTN_PAD + start - d * ATTN_BLOCK, n))
    return out


def _rows(start, size, d):
    return pl.ds(start, size, stride=d) if d > 1 else pl.ds(start, size)


def _attn_prologue(q_refs, kv_ref, tab_ref, q_s, kv_s, hk, s_len):
    swap = _swap_matrix()
    cos_v, sin_v = tab_ref[0, :, :HEAD_DIM], tab_ref[0, :, HEAD_DIM:]
    for j in range(N_HALF):
        for e in range(2):
            cols = slice(e * HEAD_DIM, (e + 1) * HEAD_DIM)
            q_s[j][:, cols] = _rope(q_refs[j][0, :, cols], cos_v, sin_v, swap, HEAD_DIM ** -0.5, False)
    kv_s[0:ATTN_PAD, :] = jnp.zeros((ATTN_PAD, HALF_W), F32)
    for h in range(N_KV_HEADS):
        @pl.when(hk == h)
        def _():
            kv_s[ATTN_PAD:ATTN_PAD + s_len, :HEAD_DIM] = _rope(kv_ref[0, :, h * HEAD_DIM:(h + 1) * HEAD_DIM], cos_v, sin_v,
                                                               swap, 1.0, False)
            kv_s[ATTN_PAD:ATTN_PAD + s_len, HEAD_DIM:] = kv_ref[0, :, LANE + h * HEAD_DIM:LANE + (h + 1) * HEAD_DIM]


def _stack_heads(halves):
    return jnp.concatenate([h[:, e * HEAD_DIM:(e + 1) * HEAD_DIM] for h in halves for e in range(2)], axis=0)


def _unstack_heads(x, j):
    return jnp.concatenate([x[(2 * j + e) * ATTN_BLOCK:(2 * j + e + 1) * ATTN_BLOCK] for e in range(2)], axis=1)


def _stack_stats(halves):
    return jnp.concatenate([jnp.max(h[:, e * HEAD_DIM:(e + 1) * HEAD_DIM], axis=1, keepdims=True)
                            for h in halves for e in range(2)], axis=0)


def _attn_in_specs(s_len):
    assert K_COL % (2 * LANE) == 0 and V_COL == K_COL + LANE

    def halves(first_tile):
        return [pl.BlockSpec((1, s_len, HALF_W), functools.partial(lambda b, h, j: (b, 0, first_tile + N_HALF * h + j), j=j))
                for j in range(N_HALF)]

    kv_spec = pl.BlockSpec((1, s_len, 2 * LANE), lambda b, h: (b, 0, K_COL // (2 * LANE)))
    t_spec = pl.BlockSpec((1, s_len, 2 * HEAD_DIM), lambda b, h: (b, 0, 0))
    o_spec = pl.BlockSpec((1, s_len, Q_GROUP_W), lambda b, h: (b, 0, h))
    return halves(Q_COL // HALF_W), kv_spec, t_spec, o_spec, halves(0)


class SideCopy:
    def __init__(self, side, *, n_in, n_out, grid):
        self.side, self.n_in, self.n_out, self.grid = side, n_in, n_out, grid
        hbm = pl.BlockSpec(memory_space=pltpu.HBM)
        if side is None:
            self.in_specs, self.out_specs, self.out_shape, self.scratch, self.args = [], [], [], [], []
            return
        srcs, per_dest = side
        n = len(srcs)
        self.in_specs, self.out_specs, self.args = [hbm] * n, [hbm] * n, list(srcs)
        self.out_shape = [jax.ShapeDtypeStruct(s.shape if per_dest else (N_CHIPS,) + s.shape, s.dtype) for s in srcs]
        self.scratch = [pltpu.SemaphoreType.DMA(((N_CHIPS - 1) * n,)), pltpu.SemaphoreType.DMA(((N_CHIPS - 1) * n,)),
                        pltpu.SemaphoreType.DMA((n,))]

    def wrap(self, body):
        if self.side is None:
            return body
        n_in, n_out, grid, per_dest, n = self.n_in, self.n_out, self.grid, self.side[1], len(self.side[0])

        def wrapped(*refs):
            ins, srcs = refs[:n_in], refs[n_in:n_in + n]
            outs, dsts = refs[n_in + n:n_in + n + n_out], refs[n_in + n + n_out:n_in + 2 * n + n_out]
            scratch, sems = refs[n_in + 2 * n + n_out:-3], refs[-3:]
            ids = [pl.program_id(a) for a in range(len(grid))]
            first = functools.reduce(lambda p, q: p & q, [i == 0 for i in ids])
            last = functools.reduce(lambda p, q: p & q, [i == g - 1 for i, g in zip(ids, grid)])

            @pl.when(first)
            def _():
                for a in range(n):
                    local, sends, _ = _chip_copies(srcs[a], dsts[a], *sems, per_dest, a)
                    local.start()
                    for cp in sends:
                        cp.start()

            body(*ins, *outs, *scratch)

            @pl.when(last)
            def _():
                for a in range(n):
                    local, sends, recvs = _chip_copies(srcs[a], dsts[a], *sems, per_dest, a)
                    for cp in recvs:
                        cp.wait_recv()
                    for cp in sends:
                        cp.wait_send()
                    local.wait()

        return wrapped


def _chip_copies(src_ref, dst_ref, send_sems, recv_sems, local_sems, per_dest, a=0):
    x, y, c = lax.axis_index("x"), lax.axis_index("y"), lax.axis_index("c")
    chip = 2 * x + y
    own = src_ref.at[chip] if per_dest else src_ref
    local = pltpu.make_async_copy(own, dst_ref.at[chip], local_sems.at[a])
    sends, recvs = [], []
    for k, (px, py) in enumerate([(1 - x, y), (x, 1 - y), (1 - x, 1 - y)]):
        k = (N_CHIPS - 1) * a + k
        peer = dict(send_sem=send_sems.at[k], recv_sem=recv_sems.at[k], device_id=(px, py, c), device_id_type=MESH)
        sends.append(pltpu.make_async_remote_copy(src_ref=src_ref.at[2 * px + py] if per_dest else src_ref,
                                                  dst_ref=dst_ref.at[chip], **peer))
        recvs.append(pltpu.make_async_remote_copy(src_ref=own, dst_ref=dst_ref.at[2 * px + py], **peer))
    return local, sends, recvs


def attn_fwd(proj3, rope_tab, *, name, side=None):
    b, s_len, _ = proj3.shape
    q_specs, kv_spec, t_spec, o_spec, _ = _attn_in_specs(s_len)
    n_br = len(DILATIONS)

    def body(*refs):
        q_refs, (kv_ref, tab_ref, o_ref, lse_ref) = refs[:N_HALF], refs[N_HALF:N_HALF + 4]
        scratch = refs[N_HALF + 4:]
        q_s, kv_s = scratch[:N_HALF], scratch[N_HALF]
        o_s = [scratch[N_HALF + 1 + i * N_HALF:N_HALF + 1 + (i + 1) * N_HALF] for i in range(n_br)]
        l_s = [scratch[N_HALF + 1 + (n_br + i) * N_HALF:N_HALF + 1 + (n_br + i + 1) * N_HALF] for i in range(n_br)]
        bias_s = scratch[-1]
        _attn_prologue(q_refs, kv_ref, tab_ref, q_s, kv_s, pl.program_id(1), s_len)
        _attn_bias(bias_s)
        for i, d, q0, k0, n in _attn_blocks(s_len):
            qrows = _rows(q0, ATTN_BLOCK, d)
            qv = _stack_heads([q_s[j][qrows, :] for j in range(N_HALF)]).astype(BF16)
            kvb = kv_s[_rows(k0, 2 * ATTN_BLOCK, d), :].astype(BF16)
            kk, vv = kvb[:, :HEAD_DIM], kvb[:, HEAD_DIM:]
            sc = lax.dot_general(qv, kk, (((1,), (1,)), ((), ())), preferred_element_type=F32)
            sc = sc + bias_s[min(n, 1)]
            m = jnp.max(sc, axis=-1, keepdims=True)
            pr = jnp.exp(sc - m)
            den = jnp.sum(pr, axis=-1, keepdims=True)
            o = jnp.dot(pr.astype(BF16), vv, preferred_element_type=F32) / den
            lse_b = jnp.broadcast_to(m + jnp.log(den), (GQA * ATTN_BLOCK, HEAD_DIM))
            for j in range(N_HALF):
                o_s[i][j][qrows, :] = _unstack_heads(o, j)
                l_s[i][j][qrows, :] = _unstack_heads(lse_b, j)
        step = 256
        for t0 in range(0, s_len, step):
            rs = pl.ds(t0, step)
            for j in range(N_HALF):
                ls = [l_s[i][j][rs, :] for i in range(n_br)]
                m = functools.reduce(jnp.maximum, ls)
                es = [jnp.exp(l - m) for l in ls]
                tot = functools.reduce(lambda a, c: a + c, es)
                inv = 1.0 / tot
                acc = None
                for i in range(n_br):
                    term = (es[i] * inv) * o_s[i][j][rs, :]
                    acc = term if acc is None else acc + term
                o_ref[0, rs, j * HALF_W:(j + 1) * HALF_W] = acc
                lse_ref[0, rs, j * HALF_W:(j + 1) * HALF_W] = m + jnp.log(tot)

    half_buf = pltpu.VMEM((s_len, HALF_W), F32)
    call = SideCopy(side, n_in=N_HALF + 2, n_out=2, grid=(b, N_KV_HEADS))
    return pl.pallas_call(
        call.wrap(body), name=name, grid=(b, N_KV_HEADS), in_specs=q_specs + [kv_spec, t_spec] + call.in_specs,
        out_specs=[o_spec, o_spec] + call.out_specs,
        out_shape=[jax.ShapeDtypeStruct((b, s_len, ATTN_WIDTH), F32)] * 2 + call.out_shape,
        scratch_shapes=[half_buf] * N_HALF + [pltpu.VMEM((ATTN_PAD + s_len, HALF_W), F32)] + [half_buf] * (2 * n_br * N_HALF)
        + [_ATTN_BIAS_BUF] + call.scratch,
        compiler_params=pltpu.CompilerParams(dimension_semantics=("arbitrary", "arbitrary"), vmem_limit_bytes=ATTN_VMEM_LIMIT),
    )(*([proj3] * (N_HALF + 1)), rope_tab, *call.args)


def attn_bwd(proj3, rope_tab, attn3, lse3, d_attn3, *, name, side=None):
    b, s_len, _ = proj3.shape
    q_specs, kv_spec, t_spec, o_spec, half_specs = _attn_in_specs(s_len)

    def body(*refs):
        q_refs = refs[:N_HALF]
        kv_ref, tab_ref, o_ref = refs[N_HALF:N_HALF + 3]
        lse_refs = refs[N_HALF + 3:2 * N_HALF + 3]
        do_refs = refs[2 * N_HALF + 3:3 * N_HALF + 3]
        dq_ref, dkv_ref = refs[3 * N_HALF + 3:3 * N_HALF + 5]
        scratch = refs[3 * N_HALF + 5:]
        q_s, kv_s = scratch[:N_HALF], scratch[N_HALF]
        dl_s = scratch[N_HALF + 1:2 * N_HALF + 1]
        dq_s = scratch[2 * N_HALF + 1:3 * N_HALF + 1]
        dkv_s = scratch[3 * N_HALF + 1]
        bias_s = scratch[-1]
        _attn_prologue(q_refs, kv_ref, tab_ref, q_s, kv_s, pl.program_id(1), s_len)
        _attn_bias(bias_s)
        dkv_s[...] = jnp.zeros_like(dkv_s)
        for j in range(N_HALF):
            dq_s[j][...] = jnp.zeros_like(dq_s[j])
            for e in range(2):
                cols = slice(e * HEAD_DIM, (e + 1) * HEAD_DIM)
                ocols = slice(j * HALF_W + e * HEAD_DIM, j * HALF_W + (e + 1) * HEAD_DIM)
                delta = jnp.sum(do_refs[j][0, :, cols] * o_ref[0, :, ocols], axis=1, keepdims=True)
                dl_s[j][:, cols] = jnp.broadcast_to(delta, (s_len, HEAD_DIM))
        for i, d, q0, k0, n in _attn_blocks(s_len):
            qrows, krows = _rows(q0, ATTN_BLOCK, d), _rows(k0, 2 * ATTN_BLOCK, d)
            qv = _stack_heads([q_s[j][qrows, :] for j in range(N_HALF)]).astype(BF16)
            kvb = kv_s[krows, :].astype(BF16)
            kk, vv = kvb[:, :HEAD_DIM], kvb[:, HEAD_DIM:]
            do16 = _stack_heads([do_refs[j].at[0][qrows, :] for j in range(N_HALF)]).astype(BF16)
            lse = _stack_stats([lse_refs[j].at[0][qrows, :] for j in range(N_HALF)])
            delta = _stack_stats([dl_s[j][qrows, :] for j in range(N_HALF)])
            sc = lax.dot_general(qv, kk, (((1,), (1,)), ((), ())), preferred_element_type=F32)
            pr = jnp.exp(sc + bias_s[min(n, 1)] - lse)
            dv = lax.dot_general(pr.astype(BF16), do16, (((0,), (0,)), ((), ())), preferred_element_type=F32)
            dp = lax.dot_general(do16, vv, (((1,), (1,)), ((), ())), preferred_element_type=F32)
            ds = (pr * (dp - delta)).astype(BF16)
            dq = jnp.dot(ds, kk, preferred_element_type=F32)
            dk = lax.dot_general(ds, qv, (((0,), (0,)), ((), ())), preferred_element_type=F32)
            for j in range(N_HALF):
                dq_s[j][qrows, :] += _unstack_heads(dq, j)
            dkv_s[krows, :] += jnp.concatenate([dk, dv], axis=1)
        swap = _swap_matrix()
        cos_v, sin_v = tab_ref[0, :, :HEAD_DIM], tab_ref[0, :, HEAD_DIM:]
        for j in range(N_HALF):
            for e in range(2):
                cols = slice(e * HEAD_DIM, (e + 1) * HEAD_DIM)
                ocols = slice(j * HALF_W + e * HEAD_DIM, j * HALF_W + (e + 1) * HEAD_DIM)
                dq_ref[0, :, ocols] = _rope(dq_s[j][:, cols], cos_v, sin_v, swap, HEAD_DIM ** -0.5, True).astype(dq_ref.dtype)
        d_k = _rope(dkv_s[ATTN_PAD:ATTN_PAD + s_len, :HEAD_DIM], cos_v, sin_v, swap, 1.0, True)
        d_v = dkv_s[ATTN_PAD:ATTN_PAD + s_len, HEAD_DIM:]
        for h in range(N_KV_HEADS):
            @pl.when(pl.program_id(1) == h)
            def _():
                dkv_ref[0, :, h * HEAD_DIM:(h + 1) * HEAD_DIM] = d_k.astype(dkv_ref.dtype)
                dkv_ref[0, :, LANE + h * HEAD_DIM:LANE + (h + 1) * HEAD_DIM] = d_v.astype(dkv_ref.dtype)

    kv_out = pl.BlockSpec((1, s_len, 2 * LANE), lambda bi, h: (bi, 0, 0))
    kv_shape = jax.ShapeDtypeStruct((b, s_len, 2 * LANE), BF16)
    half_buf = pltpu.VMEM((s_len, HALF_W), F32)
    pad_buf = pltpu.VMEM((ATTN_PAD + s_len, HALF_W), F32)
    call = SideCopy(side, n_in=3 * N_HALF + 3, n_out=2, grid=(b, N_KV_HEADS))
    return pl.pallas_call(
        call.wrap(body), name=name, grid=(b, N_KV_HEADS),
        in_specs=q_specs + [kv_spec, t_spec, o_spec] + half_specs + half_specs + call.in_specs,
        out_specs=[o_spec, kv_out] + call.out_specs,
        out_shape=[jax.ShapeDtypeStruct((b, s_len, ATTN_WIDTH), BF16), kv_shape] + call.out_shape,
        scratch_shapes=[half_buf] * N_HALF + [pad_buf] + [half_buf] * (2 * N_HALF) + [pad_buf, _ATTN_BIAS_BUF] + call.scratch,
        compiler_params=pltpu.CompilerParams(dimension_semantics=("arbitrary", "arbitrary"), vmem_limit_bytes=ATTN_VMEM_LIMIT),
    )(*([proj3] * (N_HALF + 1)), rope_tab, attn3, *([lse3] * N_HALF), *([d_attn3] * N_HALF), *call.args)


CONV_TC = 256
CONV_COL0 = XBC_COL // CONV_TC


def _shift_down(u, s):
    if s == 0:
        return u
    rows = lax.broadcasted_iota(jnp.int32, u.shape, 0)
    return jnp.where(rows >= s, pltpu.roll(u, s, 0), 0.0)


def _shift_up(u, s):
    if s == 0:
        return u
    n = u.shape[0]
    rows = lax.broadcasted_iota(jnp.int32, u.shape, 0)
    return jnp.where(rows < n - s, pltpu.roll(u, n - s, 0), 0.0)


def conv_silu_fwd(proj3, w, bias, *, name):
    b, s, _ = proj3.shape
    u_spec = pl.BlockSpec((1, s, CONV_TC), lambda j, bi: (bi, 0, CONV_COL0 + j))
    o_spec = pl.BlockSpec((1, s, CONV_TC), lambda j, bi: (bi, 0, j))
    w_spec = pl.BlockSpec((CONV_WIDTH, CONV_TC), lambda j, bi: (0, j))
    b_spec = pl.BlockSpec((1, CONV_TC), lambda j, bi: (0, j))

    def body(u_ref, w_ref, b_ref, o_ref):
        u = u_ref[0]
        y = jnp.broadcast_to(b_ref[...], u.shape)
        for k in range(CONV_WIDTH):
            y = y + w_ref[k:k + 1, :] * _shift_down(u, CONV_WIDTH - 1 - k)
        o_ref[0] = y * jax.nn.sigmoid(y)

    return pl.pallas_call(
        body, name=name, grid=(CONV_CH // CONV_TC, b), in_specs=[u_spec, w_spec, b_spec], out_specs=o_spec,
        out_shape=jax.ShapeDtypeStruct((b, s, CONV_CH), F32),
        compiler_params=_params(("parallel", "arbitrary")),
    )(proj3, w, bias)


def conv_silu_bwd(proj3, w, bias, dact, *, name):
    b, s, _ = proj3.shape
    u_spec = pl.BlockSpec((1, s, CONV_TC), lambda j, bi: (bi, 0, CONV_COL0 + j))
    o_spec = pl.BlockSpec((1, s, CONV_TC), lambda j, bi: (bi, 0, j))
    w_spec = pl.BlockSpec((CONV_WIDTH, CONV_TC), lambda j, bi: (0, j))
    b_spec = pl.BlockSpec((1, CONV_TC), lambda j, bi: (0, j))

    def body(u_ref, w_ref, b_ref, g_ref, du_ref, dw_ref, db_ref):
        bi = pl.program_id(1)

        @pl.when(bi == 0)
        def _():
            dw_ref[...] = jnp.zeros_like(dw_ref)
            db_ref[...] = jnp.zeros_like(db_ref)

        u = u_ref[0]
        y = jnp.broadcast_to(b_ref[...], u.shape)
        shifted = [_shift_down(u, CONV_WIDTH - 1 - k) for k in range(CONV_WIDTH)]
        for k in range(CONV_WIDTH):
            y = y + w_ref[k:k + 1, :] * shifted[k]
        sig = jax.nn.sigmoid(y)
        dy = g_ref[0] * (sig * (1.0 + y * (1.0 - sig)))
        du = jnp.zeros_like(u)
        for k in range(CONV_WIDTH):
            du = du + w_ref[k:k + 1, :] * _shift_up(dy, CONV_WIDTH - 1 - k)
            dw_ref[k:k + 1, :] += jnp.sum(dy * shifted[k], axis=0, keepdims=True)
        du_ref[0] = du.astype(du_ref.dtype)
        db_ref[...] += jnp.sum(dy, axis=0, keepdims=True)

    return pl.pallas_call(
        body, name=name, grid=(CONV_CH // CONV_TC, b), in_specs=[u_spec, w_spec, b_spec, o_spec],
        out_specs=[o_spec, w_spec, b_spec],
        out_shape=[jax.ShapeDtypeStruct((b, s, CONV_CH), BF16), jax.ShapeDtypeStruct((CONV_WIDTH, CONV_CH), F32),
                   jax.ShapeDtypeStruct((1, CONV_CH), F32)],
        compiler_params=_params(("parallel", "arbitrary")),
    )(proj3, w, bias, dact)


SSD_INTERLEAVE = 8
SSD_INTERLEAVE_FWD = 1


def _softplus(z):
    e = jnp.exp(-jnp.abs(z))
    u = 1.0 + e
    log1p = jnp.where(u == 1.0, e, jnp.log(u) * e / jnp.where(u == 1.0, 1.0, u - 1.0))
    return jnp.maximum(z, 0.0) + log1p


def _tri(lower):
    r = lax.broadcasted_iota(jnp.int32, (CHUNK, CHUNK), 0)
    c = lax.broadcasted_iota(jnp.int32, (CHUNK, CHUNK), 1)
    return (r >= c) if lower else (r <= c)


def _ssd_common(dtr_ref, dtb_ref, alog_ref):
    z = dtr_ref[0] + dtb_ref[...]
    dt = _softplus(z)
    aneg = -jnp.exp(alog_ref[...])
    acs = _dot01_left(_tri(True).astype(BF16), dt * aneg)
    return z, dt, aneg, acs


def _ssd_specs(nc, reverse):
    cidx = (lambda c: nc - 1 - c) if reverse else (lambda c: c)
    act_spec = pl.BlockSpec((1, CHUNK, CONV_CH), lambda b, c: (b, cidx(c), 0))
    y_spec = pl.BlockSpec((1, CHUNK, SSM_INNER), lambda b, c: (b, cidx(c), 0))
    dt_in_spec = pl.BlockSpec((1, CHUNK, LANE), lambda b, c: (b, cidx(c), DT_COL // LANE))
    dt_out_spec = pl.BlockSpec((1, CHUNK, LANE), lambda b, c: (b, cidx(c), 0))
    par_spec = pl.BlockSpec((1, LANE), lambda b, c: (0, 0))
    h_spec = pl.BlockSpec((1, SSM_HEADS, 1, SSM_P, D_STATE), lambda b, c: (b, 0, cidx(c), 0, 0))
    return act_spec, y_spec, dt_in_spec, dt_out_spec, par_spec, h_spec


def _head_cols(h):
    return slice(h * SSM_P, (h + 1) * SSM_P)


def _group_cols(g, which):
    start = SSM_INNER + which * SSM_GROUPS * D_STATE + g * D_STATE
    return slice(start, start + D_STATE)


def _each(f, *lists):
    return [f(*a) for a in zip(*lists)]


def _nt(a, b):
    return lax.dot_general(a, b, (((1,), (1,)), ((), ())), preferred_element_type=F32)


def _tn(a, b):
    return lax.dot_general(a, b, (((0,), (0,)), ((), ())), preferred_element_type=F32)


def _nn(a, b):
    return jnp.dot(a, b, preferred_element_type=F32)


def _rowsum(a):
    return jnp.sum(a, axis=1, keepdims=True)


def _colsum(a):
    return jnp.sum(a, axis=0, keepdims=True)


def _bf(a):
    return a.astype(BF16)


def _head_batches(g, width=SSD_INTERLEAVE):
    first = g * HEADS_PER_GROUP
    return [list(range(first + k, first + k + width)) for k in range(0, HEADS_PER_GROUP, width)]


def _decay_matrix(acs_j, acs_row, tri_mask):
    dm = jnp.broadcast_to(acs_j, (CHUNK, CHUNK)) - jnp.broadcast_to(acs_row, (CHUNK, CHUNK))
    return jnp.where(tri_mask, jnp.exp(jnp.where(tri_mask, dm, 0.0)), 0.0)


def ssd_fwd(act3, proj3, dtb, alog, dsk, *, name, side=None):
    b, s, _ = act3.shape
    nc = s // CHUNK
    act_spec, y_spec, dt_in_spec, _, par_spec, h_spec = _ssd_specs(nc, False)

    def body(act_ref, dtr_ref, dtb_ref, alog_ref, dsk_ref, y_ref, hp_ref, state):
        c = pl.program_id(1)

        @pl.when(c == 0)
        def _():
            state[...] = jnp.zeros_like(state)

        _, dt, _, acs = _ssd_common(dtr_ref, dtb_ref, alog_ref)
        acs_t = acs.T
        tri_mask = _tri(True)
        last_row = (lax.broadcasted_iota(jnp.int32, (CHUNK, 1), 0) == CHUNK - 1).astype(F32)
        for g in range(SSM_GROUPS):
            b16 = _bf(act_ref[0, :, _group_cols(g, 0)])
            c16 = _bf(act_ref[0, :, _group_cols(g, 1)])
            cb = _nt(c16, b16)
            for hs in _head_batches(g, SSD_INTERLEAVE_FWD):
                x = [act_ref[0, :, _head_cols(h)] for h in hs]
                dt_j = [dt[:, h:h + 1] for h in hs]
                acs_j = [acs[:, h:h + 1] for h in hs]
                acs_last = [_colsum(a * last_row) for a in acs_j]
                xg = _each(lambda xv, d: xv * d, x, dt_j)
                mm = [cb * _decay_matrix(a, acs_t[h:h + 1, :], tri_mask) for a, h in zip(acs_j, hs)]
                decay_s = _each(lambda al, a: jnp.exp(al - a), acs_last, acs_j)
                y_diag = _each(lambda m_, v: _nn(_bf(m_), _bf(v)), mm, xg)
                st = _each(lambda v, d: _tn(_bf(v * d), b16), xg, decay_s)
                hp = [state[h] for h in hs]
                for h, v in zip(hs, hp):
                    hp_ref[0, h, 0] = v
                y_off = [_nt(c16, _bf(v)) for v in hp]
                for h, yd, yo, a, xv in zip(hs, y_diag, y_off, acs_j, x):
                    y_ref[0, :, _head_cols(h)] = yd + yo * jnp.exp(a) + dsk_ref[:, h:h + 1] * xv
                for h, v, al, sv in zip(hs, hp, acs_last, st):
                    state[h] = v * jnp.exp(al) + sv

    call = SideCopy(side, n_in=5, n_out=2, grid=(b, nc))
    return pl.pallas_call(
        call.wrap(body), name=name, grid=(b, nc),
        in_specs=[act_spec, dt_in_spec, par_spec, par_spec, par_spec] + call.in_specs,
        out_specs=[y_spec, h_spec] + call.out_specs,
        out_shape=[jax.ShapeDtypeStruct((b, s, SSM_INNER), F32),
                   jax.ShapeDtypeStruct((b, SSM_HEADS, nc, SSM_P, D_STATE), F32)] + call.out_shape,
        scratch_shapes=[pltpu.VMEM((SSM_HEADS, SSM_P, D_STATE), F32)] + call.scratch,
        compiler_params=_params(("arbitrary", "arbitrary")),
    )(act3, proj3, dtb, alog, dsk, *call.args)


def ssd_bwd(act3, proj3, dtb, alog, dsk, hprev, dy3, *, name, side=None):
    b, s, _ = act3.shape
    nc = s // CHUNK
    act_spec, y_spec, dt_in_spec, dt_out_spec, par_spec, h_spec = _ssd_specs(nc, True)
    dpar_spec = pl.BlockSpec((8, LANE), lambda bi, c: (0, 0))

    def body(act_ref, dtr_ref, dtb_ref, alog_ref, dsk_ref, hp_ref, dy_ref, dact_ref, ddtr_ref, dpar_ref, dstate):
        bi, c = pl.program_id(0), pl.program_id(1)

        @pl.when(c == 0)
        def _():
            dstate[...] = jnp.zeros_like(dstate)

        @pl.when((bi == 0) & (c == 0))
        def _():
            dpar_ref[...] = jnp.zeros_like(dpar_ref)

        z, dt, aneg, acs = _ssd_common(dtr_ref, dtb_ref, alog_ref)
        acs_t = acs.T
        tri_mask = _tri(True)
        last_row = (lax.broadcasted_iota(jnp.int32, (CHUNK, 1), 0) == CHUNK - 1).astype(F32)
        lanes = lax.broadcasted_iota(jnp.int32, (1, LANE), 1)
        sublanes = lax.broadcasted_iota(jnp.int32, (LANE, 1), 0)
        ddt_mat = jnp.zeros((CHUNK, LANE), F32)
        dacs_mat = jnp.zeros((CHUNK, LANE), F32)
        dacs_rows = jnp.zeros((LANE, CHUNK), F32)
        ddsk_row = jnp.zeros((1, LANE), F32)
        for g in range(SSM_GROUPS):
            b16 = _bf(act_ref[0, :, _group_cols(g, 0)])
            c16 = _bf(act_ref[0, :, _group_cols(g, 1)])
            cb = _nt(c16, b16)
            dcb = jnp.zeros((CHUNK, CHUNK), F32)
            db_acc = jnp.zeros((CHUNK, D_STATE), F32)
            dc_acc = jnp.zeros((CHUNK, D_STATE), F32)
            for hs in _head_batches(g):
                x = [act_ref[0, :, _head_cols(h)] for h in hs]
                g_y = [dy_ref[0, :, _head_cols(h)] for h in hs]
                hp = [hp_ref[0, h, 0] for h in hs]
                g_hn = [dstate[h] for h in hs]
                dt_j = [dt[:, h:h + 1] for h in hs]
                acs_j = [acs[:, h:h + 1] for h in hs]
                acs_last = [_colsum(a * last_row) for a in acs_j]
                xg = _each(lambda xv, d: xv * d, x, dt_j)
                lm = [_decay_matrix(a, acs_t[h:h + 1, :], tri_mask) for a, h in zip(acs_j, hs)]
                mm = [cb * l for l in lm]
                decay_s = _each(lambda al, a: jnp.exp(al - a), acs_last, acs_j)
                ea = [jnp.exp(a) for a in acs_j]
                cd = [jnp.exp(al) for al in acs_last]
                g_y16, xg16, hp16, g_hn16 = [[_bf(v) for v in vs] for vs in (g_y, xg, hp, g_hn)]
                d_mm = _each(_nt, g_y16, xg16)
                d_xg = _each(lambda m_, gy: _tn(_bf(m_), gy), mm, g_y16)
                d_dm = _each(lambda a, m_: a * m_, d_mm, mm)
                d_acs = [_rowsum(v) for v in d_dm]
                t_off = [_nt(c16, v) for v in hp16]
                d_t16 = _each(lambda gy, e: _bf(gy * e), g_y, ea)
                d_acs = _each(lambda da, gy, t, e: da + _rowsum(gy * t) * e, d_acs, g_y, t_off, ea)
                d_hp = _each(lambda dtv, gh, cdv: _tn(dtv, c16) + gh * cdv, d_t16, g_hn, cd)
                d_w = [_nt(b16, v) for v in g_hn16]
                d_xg = _each(lambda dx, dw, ds: dx + dw * ds, d_xg, d_w, decay_s)
                d_ds = _each(lambda dw, v, ds: _rowsum(dw * v) * ds, d_w, xg, decay_s)
                d_last = _each(lambda gh, hv, cdv, dd: _colsum(_rowsum(gh * hv)) * cdv + _colsum(dd), g_hn, hp, cd, d_ds)
                d_acs = _each(lambda da, dd, dl: da - dd + dl * last_row, d_acs, d_ds, d_last)
                for h, gy, dx, d, xv in zip(hs, g_y, d_xg, dt_j, x):
                    dact_ref[0, :, _head_cols(h)] = dsk_ref[:, h:h + 1] * gy + dx * d
                for h, v in zip(hs, d_hp):
                    dstate[h] = v
                for k, h in enumerate(hs):
                    onehot = (lanes == h).astype(F32)
                    dcb = dcb + d_mm[k] * lm[k]
                    dc_acc = dc_acc + _nn(d_t16[k], hp16[k])
                    db_acc = db_acc + _nn(_bf(xg[k] * decay_s[k]), g_hn16[k])
                    ddsk_row = ddsk_row + _colsum(_rowsum(g_y[k] * x[k])) * onehot
                    ddt_mat = ddt_mat + _rowsum(d_xg[k] * x[k]) * onehot
                    dacs_mat = dacs_mat + d_acs[k] * onehot
                    dacs_rows = dacs_rows + (sublanes == h).astype(F32) * _colsum(d_dm[k])
            dcb16 = _bf(dcb)
            dact_ref[0, :, _group_cols(g, 1)] = dc_acc + _nn(dcb16, b16)
            dact_ref[0, :, _group_cols(g, 0)] = db_acc + _tn(dcb16, c16)
        d_a = _dot01_left(_tri(False).astype(BF16), dacs_mat - dacs_rows.T)
        ddt_mat = ddt_mat + d_a * aneg
        d_raw = ddt_mat * jax.nn.sigmoid(z)
        ddtr_ref[0] = d_raw
        dpar_ref[0:1, :] += _colsum(d_raw)
        dpar_ref[1:2, :] += _colsum(d_a * dt) * aneg
        dpar_ref[2:3, :] += ddsk_row

    call = SideCopy(side, n_in=7, n_out=3, grid=(b, nc))
    return pl.pallas_call(
        call.wrap(body), name=name, grid=(b, nc),
        in_specs=[act_spec, dt_in_spec, par_spec, par_spec, par_spec, h_spec, y_spec] + call.in_specs,
        out_specs=[act_spec, dt_out_spec, dpar_spec] + call.out_specs,
        out_shape=[jax.ShapeDtypeStruct(act3.shape, F32), jax.ShapeDtypeStruct((b, s, LANE), F32),
                   jax.ShapeDtypeStruct((8, LANE), F32)] + call.out_shape,
        scratch_shapes=[pltpu.VMEM((SSM_HEADS, SSM_P, D_STATE), F32)] + call.scratch,
        compiler_params=_params(("arbitrary", "arbitrary")),
    )(act3, proj3, dtb, alog, dsk, hprev, dy3, *call.args)


def rotary_tables(positions):
    inv_freq = ROPE_THETA ** (-jnp.arange(0, ROPE_DIM, 2, dtype=F32) / ROPE_DIM)
    ang = positions.astype(F32)[..., None] * inv_freq
    cos, sin = jnp.cos(ang), jnp.sin(ang)
    rest = HEAD_DIM - ROPE_DIM
    cosf = jnp.concatenate([cos, cos, jnp.ones(cos.shape[:2] + (rest,), F32)], axis=-1)
    sinf = jnp.concatenate([-sin, sin, jnp.zeros(sin.shape[:2] + (rest,), F32)], axis=-1)
    return cosf, sinf


def w_in_columns(w):
    pad = jnp.zeros((w.shape[0], IN_PAD - IN_PROJ), w.dtype)
    return jnp.concatenate([w[:, :Q_END], w[:, V_END:XBC_END], w[:, Q_END:V_END], w[:, XBC_END:], pad], axis=1)


def w_in_grad_columns(g):
    return jnp.concatenate([g[:, :Z_COL], g[:, K_COL:DT_COL], g[:, Z_COL:K_COL], g[:, DT_COL:DT_COL + SSM_HEADS]], axis=1)


def lane_pad(v):
    return jnp.pad(v.reshape(1, -1), ((0, 0), (0, LANE - v.shape[-1])))


def layer_fwd(h, wts, small, rope_tab, b, s, tag, attn_side=None, rest_from=None, ssd_side=None):
    w_in = wts[0]
    t = b * s
    sv = {"h": h}
    hn = rowwise_fwd(rms_fn, [h], [small["norm_mix"]], [BF16], name=f"rms_mix_{tag}")[0]
    proj = matmul(hn, w_in, name=f"in_proj_{tag}")
    sv["hn"], sv["proj"] = hn, proj
    proj3 = proj.reshape(b, s, IN_PAD)
    attn3, lse3, *attn_out = attn_fwd(proj3, rope_tab, name=f"attn_{tag}", side=attn_side)
    if rest_from is not None:
        wts = (w_in,) + tuple(rest_from(attn_out))
    _, w_out, w_gate, w_up, w_down = wts
    sv["attn3"], sv["lse3"] = attn3, lse3
    attn = attn3.reshape(t, ATTN_WIDTH)
    act3 = conv_silu_fwd(proj3, small["conv_w"], small["conv_b"], name=f"conv_{tag}")
    y3, hprev, *ssd_out = ssd_fwd(act3, proj3, small["dt_bias"], small["a_log"], small["d_skip"], name=f"ssd_{tag}",
                                  side=ssd_side)
    y = y3.reshape(t, SSM_INNER)
    sv["act3"], sv["hprev"], sv["y"] = act3, hprev, y
    gn = rowwise_fwd(gated_norm_fn, [y, proj], [small["ssm_norm"]], [BF16], name=f"gated_norm_{tag}", groups=SSM_GROUPS,
                     windows=[None, (Z_COL, SSM_INNER)])[0]
    sv["gn"] = gn
    h1 = matmul([attn, gn], w_out, name=f"out_proj_{tag}", residual=h)
    sv["h1"] = h1
    hn2 = rowwise_fwd(rms_fn, [h1], [small["norm_ffn"]], [BF16], name=f"rms_ffn_{tag}")[0]
    gate = matmul(hn2, w_gate, out_dtype=BF16, name=f"ffn_gate_{tag}")
    up = matmul(hn2, w_up, out_dtype=BF16, name=f"ffn_up_{tag}")
    act2 = rowwise_fwd(swiglu_fn, [gate, up], [], [BF16], name=f"swiglu_{tag}")[0]
    sv["hn2"], sv["gate"], sv["up"], sv["act2"] = hn2, gate, up, act2
    h2 = matmul(act2, w_down, name=f"ffn_down_{tag}", residual=h1)
    return h2, sv, wts, (ssd_out or None)


def layer_bwd(dh2, sv, wts, small, rope_tab, b, s, tag, ssd_side=None, attn_side_fn=None):
    w_in, w_out, w_gate, w_up, w_down = wts
    t = b * s
    gr = {}
    d_act2 = matmul(dh2, w_down, tb=True, out_dtype=BF16, name=f"ffn_down_dx_{tag}")
    gr["w_down"] = matmul(sv["act2"], dh2, ta=True, out_dtype=BF16, name=f"ffn_down_dw_{tag}")
    d_gate, d_up = rowwise_bwd(swiglu_fn, [sv["gate"], sv["up"]], [], [d_act2], [BF16, BF16], name=f"swiglu_bwd_{tag}")
    gr["w_gate"] = matmul(sv["hn2"], d_gate, ta=True, out_dtype=BF16, name=f"ffn_gate_dw_{tag}")
    gr["w_up"] = matmul(sv["hn2"], d_up, ta=True, out_dtype=BF16, name=f"ffn_up_dw_{tag}")
    d_hn2 = matmul(d_gate, w_gate, tb=True, name=f"ffn_gate_dx_{tag}")
    d_hn2 = matmul(d_up, w_up, tb=True, residual=d_hn2, name=f"ffn_up_dx_{tag}")
    dh1, gr["norm_ffn"] = rowwise_bwd(rms_fn, [sv["h1"]], [small["norm_ffn"]], [d_hn2], [F32],
                                      name=f"rms_ffn_bwd_{tag}", add_to_first=dh2)
    d_cat = matmul(dh1, w_out, tb=True, name=f"out_proj_dx_{tag}")
    gr["w_out"] = jnp.concatenate([
        matmul(sv["attn3"].reshape(t, ATTN_WIDTH), dh1, ta=True, out_dtype=BF16, name=f"out_proj_dw_attn_{tag}"),
        matmul(sv["gn"], dh1, ta=True, out_dtype=BF16, name=f"out_proj_dw_ssd_{tag}")], axis=0)
    d_y, d_z, gr["ssm_norm"] = rowwise_bwd(gated_norm_fn, [sv["y"], sv["proj"]], [small["ssm_norm"]], [d_cat], [F32, BF16],
                                           name=f"gated_norm_bwd_{tag}", groups=SSM_GROUPS,
                                           windows=[None, (Z_COL, SSM_INNER)], ct_windows=[(ATTN_WIDTH, SSM_INNER)])
    proj3 = sv["proj"].reshape(b, s, IN_PAD)
    d_act3, d_dtr, d_par, *ssd_out = ssd_bwd(sv["act3"], proj3, small["dt_bias"], small["a_log"], small["d_skip"],
                                             sv["hprev"], d_y.reshape(b, s, SSM_INNER), name=f"ssd_bwd_{tag}", side=ssd_side)
    gr["dt_bias"], gr["a_log"], gr["d_skip"] = d_par[0, :SSM_HEADS], d_par[1, :SSM_HEADS], d_par[2, :SSM_HEADS]
    d_xbc, gr["conv_w"], gr["conv_b"] = conv_silu_bwd(proj3, small["conv_w"], small["conv_b"], d_act3,
                                                      name=f"conv_bwd_{tag}")
    attn_side = attn_side_fn(gr) if attn_side_fn is not None else None
    d_q3, d_kv3, *attn_out = attn_bwd(proj3, rope_tab, sv["attn3"], sv["lse3"], d_cat.reshape(b, s, MIX_WIDTH),
                                      name=f"attn_bwd_{tag}", side=attn_side)
    d_proj = [d_q3.reshape(t, ATTN_WIDTH), d_z, d_xbc.reshape(t, CONV_CH), d_kv3.reshape(t, 2 * LANE),
              d_dtr.reshape(t, LANE)]
    d_hn = matmul(d_proj, w_in, tb=True, name=f"in_proj_dx_{tag}")
    gr["w_in"] = w_in_grad_columns(jnp.concatenate(
        [matmul(sv["hn"], part, ta=True, out_dtype=BF16, name=f"in_proj_dw_{k}_{tag}") for k, part in enumerate(d_proj)],
        axis=1))
    dh, gr["norm_mix"] = rowwise_bwd(rms_fn, [sv["h"]], [small["norm_mix"]], [d_hn], [F32],
                                     name=f"rms_mix_bwd_{tag}", add_to_first=dh1)
    return dh, gr, (ssd_out or None), (attn_out or None)


def local_step(x, positions, big, small_all, final_norm, loss_target, *, plan=None):
    b, s, _ = x.shape
    t = b * s
    rope_tab = jnp.concatenate(rotary_tables(positions), axis=-1)
    h = x.reshape(t, D_MODEL)
    saved, big = [], list(big)
    for l in range(DEPTH):
        kw = {}
        if plan is not None and l == 0:
            kw = dict(attn_side=(plan["rest0"], False), rest_from=plan["make_rest0"], ssd_side=(plan["late"], False))
        h, sv, big[l], got = layer_fwd(h, big[l], small_all[l], rope_tab, b, s, f"l{l}", **kw)
        if got is not None:
            big[DEPTH - 1] = plan["make_late"](got)
        saved.append(sv)
    dh, d_final, loss = loss_and_grad(h, loss_target.reshape(t, D_MODEL), final_norm.reshape(1, D_MODEL))
    grads, received = [None] * DEPTH, {}
    for l in reversed(range(DEPTH)):
        kw = {}
        if plan is not None and l == 0:
            kw = dict(ssd_side=(plan["grads_late"](grads[DEPTH - 1]), True),
                      attn_side_fn=lambda gr: (plan["grads_rest0"](gr), True))
        dh, grads[l], got_ssd, got_attn = layer_bwd(dh, saved[l], big[l], small_all[l], rope_tab, b, s, f"l{l}", **kw)
        if got_ssd is not None:
            received["late"] = got_ssd
        if got_attn is not None:
            received["rest0"] = got_attn
    return loss, dh.reshape(b, s, D_MODEL), grads, d_final, received


def _slab_rows(r):
    return r if r <= 512 else _pick(r, (512, 352, 256, 128, 8))


def cast_bf16(x, *, name):
    def fn(v):
        return (v,)
    return rowwise_fwd(fn, [x], [], [BF16], name=name, tr=_slab_rows(x.shape[0]))[0]


def sum_slots(x, *, name):
    n, r, c = x.shape
    tr = _slab_rows(r)

    def body(x_ref, o_ref):
        acc = x_ref[0].astype(F32)
        for i in range(1, n):
            acc = acc + x_ref[i].astype(F32)
        o_ref[...] = acc

    return pl.pallas_call(
        body, name=name, grid=(r // tr,), in_specs=[pl.BlockSpec((n, tr, c), lambda i: (0, i, 0))],
        out_specs=pl.BlockSpec((tr, c), lambda i: (i, 0)), out_shape=jax.ShapeDtypeStruct((r, c), F32),
        compiler_params=_params(("parallel",)),
    )(x)


def adamw(g, w, m, v, *, name):
    r, c = w.shape
    tr = _slab_rows(r)
    bc1 = 1.0 / (1.0 - ADAM_B1 ** ADAM_STEP)
    bc2 = 1.0 / (1.0 - ADAM_B2 ** ADAM_STEP)

    def body(g_ref, w_ref, m_ref, v_ref, g_out, d_out, m_out, v_out):
        gv = g_ref[...]
        m_new = ADAM_B1 * m_ref[...] + (1.0 - ADAM_B1) * gv
        v_new = ADAM_B2 * v_ref[...] + (1.0 - ADAM_B2) * (gv * gv)
        g_out[...] = gv
        m_out[...] = m_new
        v_out[...] = v_new
        d_out[...] = -ADAM_LR * ((m_new * bc1) / (jnp.sqrt(v_new * bc2) + ADAM_EPS) + ADAM_WD * w_ref[...])

    spec = pl.BlockSpec((tr, c), lambda i: (i, 0))
    return pl.pallas_call(
        body, name=name, grid=(r // tr,), in_specs=[spec] * 4, out_specs=[spec] * 4,
        out_shape=[jax.ShapeDtypeStruct((r, c), F32)] * 4, compiler_params=_params(("parallel",)),
    )(g, w, m, v)


def adamw_layers(g_parts, w, m, v, *, name):
    depth, a, b = w.shape
    tr = _pick(a, (256, 352, 192, 128, 8))
    counts = [len(p) for p in g_parts]
    flat_parts = [q for p in g_parts for q in p]
    bc1 = 1.0 / (1.0 - ADAM_B1 ** ADAM_STEP)
    bc2 = 1.0 / (1.0 - ADAM_B2 ** ADAM_STEP)

    def body(*refs):
        layer = pl.program_id(0)
        g, off = None, 0
        for l, cnt in enumerate(counts):
            g_l = refs[off][...]
            for r_ in refs[off + 1:off + cnt]:
                g_l = g_l + r_[...]
            off += cnt
            g = g_l if g is None else jnp.where(layer == l, g_l, g)
        w_ref, m_ref, v_ref, g_out, d_out, m_out, v_out = refs[off:]
        m_new = ADAM_B1 * m_ref[0] + (1.0 - ADAM_B1) * g
        v_new = ADAM_B2 * v_ref[0] + (1.0 - ADAM_B2) * (g * g)
        g_out[0] = g
        m_out[0] = m_new
        v_out[0] = v_new
        d_out[0] = -ADAM_LR * ((m_new * bc1) / (jnp.sqrt(v_new * bc2) + ADAM_EPS) + ADAM_WD * w_ref[0])

    g_spec = pl.BlockSpec((tr, b), lambda l, i: (i, 0))
    spec = pl.BlockSpec((1, tr, b), lambda l, i: (l, i, 0))
    return pl.pallas_call(
        body, name=name, grid=(depth, a // tr), in_specs=[g_spec] * len(flat_parts) + [spec] * 3, out_specs=[spec] * 4,
        out_shape=[jax.ShapeDtypeStruct(w.shape, F32)] * 4, compiler_params=_params(("parallel", "parallel")),
    )(*flat_parts, w, m, v)


SWAP_PIECES = 4


def _other_chips(x, y):
    return [(1 - x, y), (x, 1 - y), (1 - x, 1 - y)]


def allgather_chips(shards):
    n_arr = len(shards)

    def body(*refs):
        in_refs, out_refs = refs[:n_arr], refs[n_arr:2 * n_arr]
        send_sems, recv_sems, local_sems = refs[2 * n_arr:]
        x, y, c = lax.axis_index("x"), lax.axis_index("y"), lax.axis_index("c")
        chip = 2 * x + y
        started = []
        for a, (in_ref, out_ref) in enumerate(zip(in_refs, out_refs)):
            mine = pltpu.make_async_copy(in_ref, out_ref.at[chip], local_sems.at[a])
            mine.start()
            started.append(mine.wait)
            for k, (px, py) in enumerate(_other_chips(x, y)):
                cp = pltpu.make_async_remote_copy(src_ref=in_ref, dst_ref=out_ref.at[chip], send_sem=send_sems.at[3 * a + k],
                                                  recv_sem=recv_sems.at[3 * a + k], device_id=(px, py, c), device_id_type=MESH)
                cp.start()
                started.append(cp.wait_send)
        for a, (in_ref, out_ref) in enumerate(zip(in_refs, out_refs)):
            for k, (px, py) in enumerate(_other_chips(x, y)):
                pltpu.make_async_remote_copy(src_ref=in_ref, dst_ref=out_ref.at[2 * px + py], send_sem=send_sems.at[3 * a + k],
                                             recv_sem=recv_sems.at[3 * a + k], device_id=(px, py, c),
                                             device_id_type=MESH).wait_recv()
        for wait in started:
            wait()

    hbm = pl.BlockSpec(memory_space=pltpu.HBM)
    return pl.pallas_call(
        body, name="allgather_weights", in_specs=[hbm] * n_arr, out_specs=[hbm] * n_arr,
        out_shape=[jax.ShapeDtypeStruct((N_CHIPS,) + s.shape, s.dtype) for s in shards],
        scratch_shapes=[pltpu.SemaphoreType.DMA((3 * n_arr,)), pltpu.SemaphoreType.DMA((3 * n_arr,)),
                        pltpu.SemaphoreType.DMA((n_arr,))],
    )(*shards)


def exchange_grads(big, small):
    def body(big_ref, small_ref, big_out, small_out, send_sems, recv_sems, local_sems):
        x, y, c = lax.axis_index("x"), lax.axis_index("y"), lax.axis_index("c")
        chip = 2 * x + y
        dev = 4 * x + 2 * y + c
        own_big = pltpu.make_async_copy(big_ref.at[chip], big_out.at[chip], local_sems.at[0])
        own_small = pltpu.make_async_copy(small_ref, small_out.at[dev], local_sems.at[1])
        own_big.start()
        own_small.start()
        sends = []
        for k, (px, py) in enumerate(_other_chips(x, y)):
            cp = pltpu.make_async_remote_copy(src_ref=big_ref.at[2 * px + py], dst_ref=big_out.at[chip],
                                              send_sem=send_sems.at[k], recv_sem=recv_sems.at[k],
                                              device_id=(px, py, c), device_id_type=MESH)
            cp.start()
            sends.append(cp)
        peers = []
        for r in range(1, N_DEV):
            fx, fy, fc = (r >> 2) & 1, (r >> 1) & 1, r & 1
            px, py, pc = (x + fx) % 2, (y + fy) % 2, (c + fc) % 2
            peers.append((px, py, pc))
            cp = pltpu.make_async_remote_copy(src_ref=small_ref, dst_ref=small_out.at[dev], send_sem=send_sems.at[2 + r],
                                              recv_sem=recv_sems.at[2 + r], device_id=(px, py, pc), device_id_type=MESH)
            cp.start()
            sends.append(cp)
        for k, (px, py) in enumerate(_other_chips(x, y)):
            pltpu.make_async_remote_copy(src_ref=big_ref.at[chip], dst_ref=big_out.at[2 * px + py],
                                         send_sem=send_sems.at[k], recv_sem=recv_sems.at[k],
                                         device_id=(px, py, c), device_id_type=MESH).wait_recv()
        for r, (px, py, pc) in zip(range(1, N_DEV), peers):
            pltpu.make_async_remote_copy(src_ref=small_ref, dst_ref=small_out.at[4 * px + 2 * py + pc],
                                         send_sem=send_sems.at[2 + r], recv_sem=recv_sems.at[2 + r],
                                         device_id=(px, py, pc), device_id_type=MESH).wait_recv()
        for cp in sends:
            cp.wait_send()
        own_big.wait()
        own_small.wait()

    hbm = pl.BlockSpec(memory_space=pltpu.HBM)
    n_sem = 3 + N_DEV - 1
    return pl.pallas_call(
        body, name="exchange_grads", in_specs=[hbm, hbm], out_specs=[hbm, hbm],
        out_shape=[jax.ShapeDtypeStruct(big.shape, big.dtype), jax.ShapeDtypeStruct((N_DEV,) + small.shape, small.dtype)],
        scratch_shapes=[pltpu.SemaphoreType.DMA((n_sem,)), pltpu.SemaphoreType.DMA((n_sem,)), pltpu.SemaphoreType.DMA((2,))],
    )(big, small)


def swap_cores_list(arrays):
    n = len(arrays)

    def body(*refs):
        ins, outs, (send_sems, recv_sems) = refs[:n], refs[n:2 * n], refs[2 * n:]
        x, y, c = lax.axis_index("x"), lax.axis_index("y"), lax.axis_index("c")
        copies = []
        for k in range(n):
            rows = ins[k].shape[0] // SWAP_PIECES
            for p in range(SWAP_PIECES):
                part = pl.ds(p * rows, rows)
                copies.append(pltpu.make_async_remote_copy(
                    src_ref=ins[k].at[part], dst_ref=outs[k].at[part], send_sem=send_sems.at[k * SWAP_PIECES + p],
                    recv_sem=recv_sems.at[k * SWAP_PIECES + p], device_id=(x, y, 1 - c), device_id_type=MESH))
        for cp in copies:
            cp.start()
        for cp in copies:
            cp.wait_recv()
        for cp in copies:
            cp.wait_send()

    assert all(a.shape[0] % (8 * SWAP_PIECES) == 0 for a in arrays)
    hbm = pl.BlockSpec(memory_space=pltpu.HBM)
    return pl.pallas_call(
        body, name="swap_cores", in_specs=[hbm] * n, out_specs=[hbm] * n,
        out_shape=[jax.ShapeDtypeStruct(a.shape, a.dtype) for a in arrays],
        scratch_shapes=[pltpu.SemaphoreType.DMA((n * SWAP_PIECES,)), pltpu.SemaphoreType.DMA((n * SWAP_PIECES,))],
    )(*arrays)


BIG_NAMES = ("w_in", "w_out", "w_gate", "w_up", "w_down")
BIG_SHARD_AXIS = {"w_in": 1, "w_out": 0, "w_gate": 1, "w_up": 1, "w_down": 0}
SMALL_NAMES = ("norm_mix", "conv_w", "conv_b", "dt_bias", "a_log", "d_skip", "ssm_norm", "norm_ffn")


def pack_small(parts):
    flat = jnp.concatenate([p.reshape(-1).astype(F32) for p in parts])
    rows = -(-flat.size // LANE)
    rows = -(-rows // 8) * 8
    return jnp.pad(flat, (0, rows * LANE - flat.size)).reshape(rows, LANE)


def unpack_small(packed, like):
    out, off = [], 0
    flat = packed.reshape(-1)
    for a in like:
        out.append(flat[off:off + a.size].reshape(a.shape))
        off += a.size
    return out


def kernel(x, positions, norm_mix, w_in, conv_w, conv_b, dt_bias, a_log, d_skip, ssm_norm, w_out, norm_ffn, w_gate, w_up, w_down, final_norm, loss_target, m_norm_mix, m_w_in, m_conv_w, m_conv_b, m_dt_bias, m_a_log, m_d_skip, m_ssm_norm, m_w_out, m_norm_ffn, m_w_gate, m_w_up, m_w_down, m_final_norm, v_norm_mix, v_w_in, v_conv_w, v_conv_b, v_dt_bias, v_a_log, v_d_skip, v_ssm_norm, v_w_out, v_norm_ffn, v_w_gate, v_w_up, v_w_down, v_final_norm):
    chip = 2 * lax.axis_index("x") + lax.axis_index("y")
    w_sh = {"w_in": w_in, "w_out": w_out, "w_gate": w_gate, "w_up": w_up, "w_down": w_down}
    m_sh = {"w_in": m_w_in, "w_out": m_w_out, "w_gate": m_w_gate, "w_up": m_w_up, "w_down": m_w_down}
    v_sh = {"w_in": v_w_in, "w_out": v_w_out, "w_gate": v_w_gate, "w_up": v_w_up, "w_down": v_w_down}
    assert DEPTH == 2
    rest = BIG_NAMES[1:]

    w16 = {n: cast_bf16(w_sh[n].reshape(-1, w_sh[n].shape[-1]), name=f"cast_{n}").reshape(w_sh[n].shape) for n in BIG_NAMES}

    def joined(n, gathered):
        if BIG_SHARD_AXIS[n] == 0:
            full = gathered.reshape(-1, gathered.shape[-1])
        else:
            full = jnp.concatenate([gathered[j] for j in range(N_CHIPS)], axis=1)
        return w_in_columns(full) if n == "w_in" else full

    def per_chip(n, g):
        if BIG_SHARD_AXIS[n] == 0:
            return g.reshape(N_CHIPS, -1, g.shape[-1])
        return jnp.stack(jnp.split(g, N_CHIPS, axis=1))

    conv_cols = CONV_CH // N_CHIPS
    gathered_in0, conv_g = allgather_chips([w16["w_in"][0], conv_w.reshape(-1, LANE)])
    big = [(joined("w_in", gathered_in0),) + (None,) * len(rest), None]
    early = {}

    def make_rest0(gs):
        early["w_in"] = gs[len(rest)]
        return tuple(joined(n, g) for n, g in zip(rest, gs))

    plan = {
        "rest0": [w16[n][0] for n in rest] + [w16["w_in"][DEPTH - 1]],
        "make_rest0": make_rest0,
        "late": [w16[n][DEPTH - 1] for n in rest],
        "make_late": lambda gs: (joined("w_in", early["w_in"]),) + tuple(joined(n, g) for n, g in zip(rest, gs)),
        "grads_late": lambda gr: [per_chip(n, gr[n]) for n in BIG_NAMES],
        "grads_rest0": lambda gr: [per_chip(n, gr[n]) for n in rest],
    }
    conv_w_full = jnp.concatenate([conv_g[j].reshape(DEPTH, CONV_WIDTH, conv_cols) for j in range(N_CHIPS)], axis=2)
    small_all = []
    for l in range(DEPTH):
        small_all.append({
            "norm_mix": norm_mix[l].reshape(1, -1), "conv_w": conv_w_full[l], "conv_b": conv_b[l].reshape(1, -1),
            "dt_bias": lane_pad(dt_bias[l]), "a_log": lane_pad(a_log[l]), "d_skip": lane_pad(d_skip[l]),
            "ssm_norm": ssm_norm[l].reshape(1, -1), "norm_ffn": norm_ffn[l].reshape(1, -1)})

    loss_part, grad_x, grads, d_final, received = local_step(x, positions, big, small_all, final_norm, loss_target, plan=plan)

    small_parts = [jnp.stack([grads[l][n].reshape(-1) for l in range(DEPTH)]) for n in SMALL_NAMES]
    small_parts += [d_final.reshape(-1), loss_part.reshape(-1)]
    recv_in0, recv_small = exchange_grads(per_chip("w_in", grads[0]["w_in"]), pack_small(small_parts))
    recv = [dict(zip(BIG_NAMES, [recv_in0] + list(received["rest0"]))), dict(zip(BIG_NAMES, received["late"]))]
    keys = [(l, n) for l in range(DEPTH) for n in BIG_NAMES]
    mine = {(l, n): sum_slots(recv[l][n], name=f"sum_partials_{n}_l{l}") for l, n in keys}
    other = dict(zip(keys, swap_cores_list([mine[k] for k in keys])))

    g_big, d_big, m_big, v_big = {}, {}, {}, {}
    for n in BIG_NAMES:
        g_big[n], d_big[n], m_big[n], v_big[n] = adamw_layers([[mine[(l, n)], other[(l, n)]] for l in range(DEPTH)],
                                                              w_sh[n], m_sh[n], v_sh[n], name=f"adamw_{n}")

    small_sum = sum_slots(recv_small, name="sum_small")
    like = [norm_mix, conv_w_full, conv_b, dt_bias, a_log, d_skip, ssm_norm, norm_ffn, final_norm, loss_part.reshape(-1)]
    g_small = unpack_small(small_sum, like)
    loss = g_small[-1][0]
    g_small = dict(zip(SMALL_NAMES + ("final_norm",), g_small[:-1]))
    g_small["conv_w"] = lax.dynamic_slice_in_dim(g_small["conv_w"], chip * conv_cols, conv_cols, axis=2)
    w_small = {"norm_mix": norm_mix, "conv_w": conv_w, "conv_b": conv_b, "dt_bias": dt_bias, "a_log": a_log, "d_skip": d_skip,
               "ssm_norm": ssm_norm, "norm_ffn": norm_ffn, "final_norm": final_norm}
    m_small = {"norm_mix": m_norm_mix, "conv_w": m_conv_w, "conv_b": m_conv_b, "dt_bias": m_dt_bias, "a_log": m_a_log,
               "d_skip": m_d_skip, "ssm_norm": m_ssm_norm, "norm_ffn": m_norm_ffn, "final_norm": m_final_norm}
    v_small = {"norm_mix": v_norm_mix, "conv_w": v_conv_w, "conv_b": v_conv_b, "dt_bias": v_dt_bias, "a_log": v_a_log,
               "d_skip": v_d_skip, "ssm_norm": v_ssm_norm, "norm_ffn": v_norm_ffn, "final_norm": v_final_norm}
    names = SMALL_NAMES + ("final_norm",)
    order = [w_small[n] for n in names]
    res = adamw(pack_small([g_small[n] for n in names]), pack_small(order), pack_small([m_small[n] for n in names]),
                pack_small([v_small[n] for n in names]), name="adamw_small")
    g_s, d_s, m_s, v_s = (dict(zip(names, unpack_small(a, order))) for a in res)

    all_names = ("norm_mix", "w_in", "conv_w", "conv_b", "dt_bias", "a_log", "d_skip", "ssm_norm", "w_out", "norm_ffn",
                 "w_gate", "w_up", "w_down", "final_norm")
    outs = [loss, grad_x]
    for src_big, src_small in ((g_big, g_s), (d_big, d_s), (m_big, m_s), (v_big, v_s)):
        outs += [src_big[n] if n in BIG_NAMES else src_small[n] for n in all_names]
    return tuple(outs)
```

```python
import functools

import jax
import jax.numpy as jnp
from jax import lax
from jax.experimental import pallas as pl
from jax.experimental.pallas import tpu as pltpu

F32 = jnp.float32
BF16 = jnp.bfloat16
MESH = pl.DeviceIdType.MESH

D_MODEL = 1024
DEPTH = 2
HEAD_DIM = 64
N_Q_HEADS = 8
N_KV_HEADS = 2
GQA = N_Q_HEADS // N_KV_HEADS
ATTN_WIDTH = N_Q_HEADS * HEAD_DIM
ROPE_DIM = HEAD_DIM // 4
ROPE_HALF = ROPE_DIM // 2
ROPE_THETA = 500000.0
DILATIONS = (1, 4, 16)
ATTN_BLOCK = 128
SSM_P = 64
SSM_HEADS = 16
SSM_INNER = SSM_HEADS * SSM_P
SSM_GROUPS = 2
HEADS_PER_GROUP = SSM_HEADS // SSM_GROUPS
D_STATE = 128
CONV_WIDTH = 4
CHUNK = 128
CONV_CH = SSM_INNER + 2 * SSM_GROUPS * D_STATE
MIX_WIDTH = ATTN_WIDTH + SSM_INNER
Q_END = ATTN_WIDTH
K_END = Q_END + N_KV_HEADS * HEAD_DIM
V_END = K_END + N_KV_HEADS * HEAD_DIM
Z_END = V_END + SSM_INNER
XBC_END = Z_END + CONV_CH
IN_PROJ = XBC_END + SSM_HEADS
LANE = 128
IN_PAD = XBC_END + LANE
Q_COL, Z_COL, XBC_COL, K_COL, V_COL, DT_COL = 0, 512, 1536, 3072, 3200, 3328
EPS = 1e-5
ADAM_LR, ADAM_B1, ADAM_B2, ADAM_EPS, ADAM_WD, ADAM_STEP = 0.001, 0.9, 0.999, 1e-8, 0.01, 10
N_CHIPS = 4
N_DEV = 8
VMEM_LIMIT = 48 * 1024 * 1024
NEG_BIG = -1e30


def _params(sem=None):
    return pltpu.CompilerParams(dimension_semantics=sem, vmem_limit_bytes=VMEM_LIMIT)


def _pick(n, prefs):
    for p in prefs:
        if n % p == 0:
            return p
    return n


def matmul(a, b, *, name, ta=False, tb=False, out_dtype=F32, residual=None):
    if ta:
        assert not tb and residual is None
        return _matmul_over_rows(a, b, name=name, out_dtype=out_dtype)
    return _matmul_full_k(a, b, name=name, tb=tb, out_dtype=out_dtype, residual=residual)


def _matmul_full_k(a, b, *, name, tb, out_dtype, residual):
    a_parts = list(a) if isinstance(a, (list, tuple)) else [a]
    n_a = len(a_parts)
    m = a_parts[0].shape[0]
    kdim = sum(p.shape[1] for p in a_parts)
    wide = kdim > 1536 or any(p.dtype == F32 for p in a_parts)
    n = b.shape[0] if tb else b.shape[1]
    tm = _pick(m, (512, 256)) if wide else _pick(m, (1024, 512, 256))
    tn = _pick(n, (1152, 1408, 1536, 1024, 768, 512, 384, 256, 128))
    b_spec = pl.BlockSpec((tn, kdim), lambda i, j: (j, 0)) if tb else pl.BlockSpec((kdim, tn), lambda i, j: (0, j))
    o_spec = pl.BlockSpec((tm, tn), lambda i, j: (i, j))
    dims = (((1,), (1 if tb else 0,)), ((), ()))
    has_res = residual is not None

    def body(*refs):
        b_ref, o_ref = refs[n_a], refs[-1]
        pieces = [r[...].astype(BF16) for r in refs[:n_a]]
        av = pieces[0] if n_a == 1 else jnp.concatenate(pieces, axis=1)
        r = lax.dot_general(av, b_ref[...].astype(BF16), dims, preferred_element_type=F32)
        if has_res:
            r = r + refs[n_a + 1][...]
        o_ref[...] = r.astype(out_dtype)

    in_specs = ([pl.BlockSpec((tm, p.shape[1]), lambda i, j: (i, 0)) for p in a_parts] + [b_spec]
                + ([o_spec] if has_res else []))
    args = tuple(a_parts) + (b,) + ((residual,) if has_res else ())
    return pl.pallas_call(
        body, name=name, grid=(m // tm, n // tn), in_specs=in_specs, out_specs=o_spec,
        out_shape=jax.ShapeDtypeStruct((m, n), out_dtype),
        compiler_params=_params(("parallel", "parallel")),
    )(*args)


def _matmul_over_rows(a, b, *, name, out_dtype):
    t, m = a.shape
    n = b.shape[1]
    tm = _pick(m, (1024, 1408, 768, 512, 256, 128))
    tn = _pick(n, (1152, 1408, 1024, 768, 512, 384, 256, 128))
    tk = _pick(t, (2048, 1024, 512, 256, 128))
    nk = t // tk

    def body(a_ref, b_ref, o_ref, acc):
        k = pl.program_id(2)
        part = lax.dot_general(a_ref[...].astype(BF16), b_ref[...].astype(BF16), (((0,), (0,)), ((), ())),
                               preferred_element_type=F32)

        @pl.when(k == 0)
        def _():
            acc[...] = part

        @pl.when(k > 0)
        def _():
            acc[...] += part

        @pl.when(k == nk - 1)
        def _():
            o_ref[...] = acc[...].astype(out_dtype)

    return pl.pallas_call(
        body, name=name, grid=(m // tm, n // tn, nk),
        in_specs=[pl.BlockSpec((tk, tm), lambda i, j, k: (k, i)), pl.BlockSpec((tk, tn), lambda i, j, k: (k, j))],
        out_specs=pl.BlockSpec((tm, tn), lambda i, j, k: (i, j)),
        out_shape=jax.ShapeDtypeStruct((m, n), out_dtype),
        scratch_shapes=[pltpu.VMEM((tm, tn), F32)],
        compiler_params=_params(("parallel", "parallel", "arbitrary")),
    )(a, b)


ROW_BLOCK_BYTES = 32 * 1024 * 1024


def _row_tile(t, tr, widths, n_copies):
    lanes = sum(-(-wd // LANE) * LANE for wd in widths) * n_copies
    tr = min(tr, t)
    while tr > 8 and tr * lanes * 4 > ROW_BLOCK_BYTES:
        tr //= 2
    return tr


def _row_widths(rows, groups, windows):
    windows = windows or [None] * len(rows)
    widths = [(w[1] if w else a.shape[1]) // groups for a, w in zip(rows, windows)]
    assert all(w is None or w[0] % wd == 0 for w, wd in zip(windows, widths))
    return widths, [(w[0] // wd if w else 0) for w, wd in zip(windows, widths)]


def _row_specs(tr, widths, offs):
    return [pl.BlockSpec((tr, wd), functools.partial(lambda g, i, off: (i, g + off), off=off)) for wd, off in zip(widths, offs)]


def rowwise_fwd(fn, rows, params, out_dtypes, *, name, tr=512, groups=1, windows=None):
    t = rows[0].shape[0]
    widths, offs = _row_widths(rows, groups, windows)
    tr = _row_tile(t, tr, widths, 2)
    row_specs = _row_specs(tr, widths, offs)
    par_spec = lambda p: pl.BlockSpec((1, p.shape[1] // groups), lambda g, i: (0, g))
    n_in = len(rows) + len(params)
    out_cols = [o.shape[1] for o in jax.eval_shape(
        fn, *[jax.ShapeDtypeStruct((tr, wd), F32) for wd in widths],
        *[jax.ShapeDtypeStruct((1, p.shape[1] // groups), F32) for p in params])]

    def body(*refs):
        vals = [r[...].astype(F32) for r in refs[:n_in]]
        outs = fn(*vals)
        for o_ref, o in zip(refs[n_in:], outs):
            o_ref[...] = o.astype(o_ref.dtype)

    return pl.pallas_call(
        body, name=name, grid=(groups, t // tr),
        in_specs=row_specs + [par_spec(p) for p in params],
        out_specs=[pl.BlockSpec((tr, c), lambda g, i: (i, g)) for c in out_cols],
        out_shape=[jax.ShapeDtypeStruct((t, c * groups), d) for c, d in zip(out_cols, out_dtypes)],
        compiler_params=_params(("arbitrary", "arbitrary")),
    )(*rows, *params)


def rowwise_bwd(fn, rows, params, cts, drow_dtypes, *, name, tr=512, groups=1, add_to_first=None, windows=None,
                ct_windows=None):
    t = rows[0].shape[0]
    widths, offs = _row_widths(rows, groups, windows)
    ct_widths, ct_offs = _row_widths(cts, groups, ct_windows)
    tr = _row_tile(t, tr, widths + ct_widths, 2)
    row_spec = lambda a: pl.BlockSpec((tr, a.shape[1] // groups), lambda g, i: (i, g))
    row_specs = _row_specs(tr, widths, offs)
    par_spec = lambda p: pl.BlockSpec((1, p.shape[1] // groups), lambda g, i: (0, g))
    n_rows, n_par, n_ct = len(rows), len(params), len(cts)
    has_add = add_to_first is not None
    n_in = n_rows + n_par + n_ct + (1 if has_add else 0)

    def body(*refs):
        i = pl.program_id(1)
        vals = [r[...].astype(F32) for r in refs[:n_rows + n_par]]
        ct_vals = tuple(r[...].astype(F32) for r in refs[n_rows + n_par:n_rows + n_par + n_ct])
        _, vjp = jax.vjp(fn, *vals)
        grads = vjp(ct_vals)
        out_refs = refs[n_in:]
        for idx in range(n_rows):
            g = grads[idx]
            if idx == 0 and has_add:
                g = g + refs[n_in - 1][...]
            out_refs[idx][...] = g.astype(out_refs[idx].dtype)
        for idx in range(n_par):
            p_ref = out_refs[n_rows + idx]

            @pl.when(i == 0)
            def _():
                p_ref[...] = jnp.zeros_like(p_ref)

            p_ref[...] += grads[n_rows + idx]

    ins = list(rows) + list(params) + list(cts) + ([add_to_first] if has_add else [])
    in_specs = (row_specs + [par_spec(p) for p in params] + _row_specs(tr, ct_widths, ct_offs)
                + ([row_spec(add_to_first)] if has_add else []))
    return pl.pallas_call(
        body, name=name, grid=(groups, t // tr), in_specs=in_specs,
        out_specs=[pl.BlockSpec((tr, wd), lambda g, i: (i, g)) for wd in widths] + [par_spec(p) for p in params],
        out_shape=[jax.ShapeDtypeStruct((t, wd * groups), d) for wd, d in zip(widths, drow_dtypes)]
        + [jax.ShapeDtypeStruct(p.shape, F32) for p in params],
        compiler_params=_params(("arbitrary", "arbitrary")),
    )(*ins)


def rms_fn(x, w):
    return (x * lax.rsqrt(jnp.mean(x * x, axis=-1, keepdims=True) + EPS) * w,)


def swiglu_fn(g, u):
    return (g * jax.nn.sigmoid(g) * u,)


def gated_norm_fn(y, z, w):
    v = y * (z * jax.nn.sigmoid(z))
    return (v * lax.rsqrt(jnp.mean(v * v, axis=-1, keepdims=True) + EPS) * w,)


def loss_and_grad(h, target, w, *, tr=512):
    t, d = h.shape

    def loss_fn(hv, wv, tv):
        err = rms_fn(hv, wv)[0] - tv
        per_row = jnp.mean(err * err, axis=-1, keepdims=True)
        return 0.5 * jnp.sum(per_row, axis=0, keepdims=True)

    def body(h_ref, t_ref, w_ref, dh_ref, dw_ref, loss_ref):
        i = pl.program_id(0)

        @pl.when(i == 0)
        def _():
            dw_ref[...] = jnp.zeros_like(dw_ref)
            loss_ref[...] = jnp.zeros_like(loss_ref)

        tv = t_ref[...]
        val, vjp = jax.vjp(lambda hv, wv: loss_fn(hv, wv, tv), h_ref[...], w_ref[...])
        dh, dw = vjp(jnp.ones((1, 1), F32))
        dh_ref[...] = dh
        dw_ref[...] += dw
        loss_ref[...] += jnp.broadcast_to(val, loss_ref.shape)

    row = pl.BlockSpec((tr, d), lambda i: (i, 0))
    par = pl.BlockSpec((1, d), lambda i: (0, 0))
    return pl.pallas_call(
        body, name="loss_and_grad", grid=(t // tr,), in_specs=[row, row, par],
        out_specs=[row, par, pl.BlockSpec((1, LANE), lambda i: (0, 0))],
        out_shape=[jax.ShapeDtypeStruct((t, d), F32), jax.ShapeDtypeStruct((1, d), F32),
                   jax.ShapeDtypeStruct((1, LANE), F32)],
        compiler_params=_params(("arbitrary",)),
    )(h, target, w)


def _split3(x):
    hi = x.astype(BF16)
    r1 = x - hi.astype(F32)
    mid = r1.astype(BF16)
    lo = (r1 - mid.astype(F32)).astype(BF16)
    return hi, mid, lo


def _dot01_left(m01, x):
    return sum(jnp.dot(m01, p, preferred_element_type=F32) for p in _split3(x))


def _dot01_right(x, m01):
    return sum(jnp.dot(p, m01, preferred_element_type=F32) for p in _split3(x))


ATTN_PAD = ATTN_BLOCK * DILATIONS[-1]
Q_GROUP_W = GQA * HEAD_DIM
ATTN_VMEM_LIMIT = 56 * 1024 * 1024
HALF_W = 2 * HEAD_DIM
N_HALF = Q_GROUP_W // HALF_W
_ATTN_BIAS_BUF = pltpu.VMEM((2, GQA * ATTN_BLOCK, 2 * ATTN_BLOCK), F32)


def _attn_mask(n):
    rows = GQA * ATTN_BLOCK
    qi = lax.broadcasted_iota(jnp.int32, (rows, 2 * ATTN_BLOCK), 0) % ATTN_BLOCK
    ki = lax.broadcasted_iota(jnp.int32, (rows, 2 * ATTN_BLOCK), 1)
    delta = qi + ATTN_BLOCK - ki
    return (delta >= 0) & (delta <= ATTN_BLOCK) & ((n - 1) * ATTN_BLOCK + ki >= 0)


def _attn_bias(bias_s):
    for first in (0, 1):
        bias_s[first] = jnp.where(_attn_mask(first), 0.0, NEG_BIG)


def _rope(x, cos_v, sin_v, swap, scale, adjoint):
    if adjoint:
        return (x * cos_v + _dot01_right(x * sin_v, swap)) * scale
    return (x * cos_v + _dot01_right(x, swap) * sin_v) * scale


def _swap_matrix():
    c = HEAD_DIM
    ci = lax.broadcasted_iota(jnp.int32, (c, c), 0)
    cj = lax.broadcasted_iota(jnp.int32, (c, c), 1)
    swap = ((cj == ci + ROPE_HALF) & (ci < ROPE_HALF)) | ((cj == ci - ROPE_HALF) & (ci >= ROPE_HALF) & (ci < ROPE_DIM))
    return swap.astype(BF16)


def _attn_blocks(s_len):
    out = []
    for i, d in enumerate(DILATIONS):
        nb = s_len // (ATTN_BLOCK * d)
        for r in range(d):
            for n in range(nb):
                start = r + d * ATTN_BLOCK * n
                out.append((i, d, start, ATTN_PAD + start - d * ATTN_BLOCK, n))
    return out


def _rows(start, size, d):
    return pl.ds(start, size, stride=d) if d > 1 else pl.ds(start, size)


def _attn_prologue(q_refs, kv_ref, tab_ref, q_s, kv_s, hk, s_len):
    swap = _swap_matrix()
    cos_v, sin_v = tab_ref[0, :, :HEAD_DIM], tab_ref[0, :, HEAD_DIM:]
    for j in range(N_HALF):
        for e in range(2):
            cols = slice(e * HEAD_DIM, (e + 1) * HEAD_DIM)
            q_s[j][:, cols] = _rope(q_refs[j][0, :, cols], cos_v, sin_v, swap, HEAD_DIM ** -0.5, False)
    kv_s[0:ATTN_PAD, :] = jnp.zeros((ATTN_PAD, HALF_W), F32)
    for h in range(N_KV_HEADS):
        @pl.when(hk == h)
        def _():
            kv_s[ATTN_PAD:ATTN_PAD + s_len, :HEAD_DIM] = _rope(kv_ref[0, :, h * HEAD_DIM:(h + 1) * HEAD_DIM], cos_v, sin_v,
                                                               swap, 1.0, False)
            kv_s[ATTN_PAD:ATTN_PAD + s_len, HEAD_DIM:] = kv_ref[0, :, LANE + h * HEAD_DIM:LANE + (h + 1) * HEAD_DIM]


def _stack_heads(halves):
    return jnp.concatenate([h[:, e * HEAD_DIM:(e + 1) * HEAD_DIM] for h in halves for e in range(2)], axis=0)


def _unstack_heads(x, j):
    return jnp.concatenate([x[(2 * j + e) * ATTN_BLOCK:(2 * j + e + 1) * ATTN_BLOCK] for e in range(2)], axis=1)


def _stack_stats(halves):
    return jnp.concatenate([jnp.max(h[:, e * HEAD_DIM:(e + 1) * HEAD_DIM], axis=1, keepdims=True)
                            for h in halves for e in range(2)], axis=0)


def _attn_in_specs(s_len):
    assert K_COL % (2 * LANE) == 0 and V_COL == K_COL + LANE

    def halves(first_tile):
        return [pl.BlockSpec((1, s_len, HALF_W), functools.partial(lambda b, h, j: (b, 0, first_tile + N_HALF * h + j), j=j))
                for j in range(N_HALF)]

    kv_spec = pl.BlockSpec((1, s_len, 2 * LANE), lambda b, h: (b, 0, K_COL // (2 * LANE)))
    t_spec = pl.BlockSpec((1, s_len, 2 * HEAD_DIM), lambda b, h: (b, 0, 0))
    o_spec = pl.BlockSpec((1, s_len, Q_GROUP_W), lambda b, h: (b, 0, h))
    return halves(Q_COL // HALF_W), kv_spec, t_spec, o_spec, halves(0)


class SideCopy:
    def __init__(self, side, *, n_in, n_out, grid):
        self.side, self.n_in, self.n_out, self.grid = side, n_in, n_out, grid
        hbm = pl.BlockSpec(memory_space=pltpu.HBM)
        if side is None:
            self.in_specs, self.out_specs, self.out_shape, self.scratch, self.args = [], [], [], [], []
            return
        srcs, per_dest = side
        n = len(srcs)
        self.in_specs, self.out_specs, self.args = [hbm] * n, [hbm] * n, list(srcs)
        self.out_shape = [jax.ShapeDtypeStruct(s.shape if per_dest else (N_CHIPS,) + s.shape, s.dtype) for s in srcs]
        self.scratch = [pltpu.SemaphoreType.DMA(((N_CHIPS - 1) * n,)), pltpu.SemaphoreType.DMA(((N_CHIPS - 1) * n,)),
                        pltpu.SemaphoreType.DMA((n,))]

    def wrap(self, body):
        if self.side is None:
            return body
        n_in, n_out, grid, per_dest, n = self.n_in, self.n_out, self.grid, self.side[1], len(self.side[0])

        def wrapped(*refs):
            ins, srcs = refs[:n_in], refs[n_in:n_in + n]
            outs, dsts = refs[n_in + n:n_in + n + n_out], refs[n_in + n + n_out:n_in + 2 * n + n_out]
            scratch, sems = refs[n_in + 2 * n + n_out:-3], refs[-3:]
            ids = [pl.program_id(a) for a in range(len(grid))]
            first = functools.reduce(lambda p, q: p & q, [i == 0 for i in ids])
            last = functools.reduce(lambda p, q: p & q, [i == g - 1 for i, g in zip(ids, grid)])

            @pl.when(first)
            def _():
                for a in range(n):
                    local, sends, _ = _chip_copies(srcs[a], dsts[a], *sems, per_dest, a)
                    local.start()
                    for cp in sends:
                        cp.start()

            body(*ins, *outs, *scratch)

            @pl.when(last)
            def _():
                for a in range(n):
                    local, sends, recvs = _chip_copies(srcs[a], dsts[a], *sems, per_dest, a)
                    for cp in recvs:
                        cp.wait_recv()
                    for cp in sends:
                        cp.wait_send()
                    local.wait()

        return wrapped


def _chip_copies(src_ref, dst_ref, send_sems, recv_sems, local_sems, per_dest, a=0):
    x, y, c = lax.axis_index("x"), lax.axis_index("y"), lax.axis_index("c")
    chip = 2 * x + y
    own = src_ref.at[chip] if per_dest else src_ref
    local = pltpu.make_async_copy(own, dst_ref.at[chip], local_sems.at[a])
    sends, recvs = [], []
    for k, (px, py) in enumerate([(1 - x, y), (x, 1 - y), (1 - x, 1 - y)]):
        k = (N_CHIPS - 1) * a + k
        peer = dict(send_sem=send_sems.at[k], recv_sem=recv_sems.at[k], device_id=(px, py, c), device_id_type=MESH)
        sends.append(pltpu.make_async_remote_copy(src_ref=src_ref.at[2 * px + py] if per_dest else src_ref,
                                                  dst_ref=dst_ref.at[chip], **peer))
        recvs.append(pltpu.make_async_remote_copy(src_ref=own, dst_ref=dst_ref.at[2 * px + py], **peer))
    return local, sends, recvs


def attn_fwd(proj3, rope_tab, *, name, side=None):
    b, s_len, _ = proj3.shape
    q_specs, kv_spec, t_spec, o_spec, _ = _attn_in_specs(s_len)
    n_br = len(DILATIONS)

    def body(*refs):
        q_refs, (kv_ref, tab_ref, o_ref, lse_ref) = refs[:N_HALF], refs[N_HALF:N_HALF + 4]
        scratch = refs[N_HALF + 4:]
        q_s, kv_s = scratch[:N_HALF], scratch[N_HALF]
        o_s = [scratch[N_HALF + 1 + i * N_HALF:N_HALF + 1 + (i + 1) * N_HALF] for i in range(n_br)]
        l_s = [scratch[N_HALF + 1 + (n_br + i) * N_HALF:N_HALF + 1 + (n_br + i + 1) * N_HALF] for i in range(n_br)]
        bias_s = scratch[-1]
        _attn_prologue(q_refs, kv_ref, tab_ref, q_s, kv_s, pl.program_id(1), s_len)
        _attn_bias(bias_s)
        for i, d, q0, k0, n in _attn_blocks(s_len):
            qrows = _rows(q0, ATTN_BLOCK, d)
            qv = _stack_heads([q_s[j][qrows, :] for j in range(N_HALF)]).astype(BF16)
            kvb = kv_s[_rows(k0, 2 * ATTN_BLOCK, d), :].astype(BF16)
            kk, vv = kvb[:, :HEAD_DIM], kvb[:, HEAD_DIM:]
            sc = lax.dot_general(qv, kk, (((1,), (1,)), ((), ())), preferred_element_type=F32)
            sc = sc + bias_s[min(n, 1)]
            m = jnp.max(sc, axis=-1, keepdims=True)
            pr = jnp.exp(sc - m)
            den = jnp.sum(pr, axis=-1, keepdims=True)
            o = jnp.dot(pr.astype(BF16), vv, preferred_element_type=F32) / den
            lse_b = jnp.broadcast_to(m + jnp.log(den), (GQA * ATTN_BLOCK, HEAD_DIM))
            for j in range(N_HALF):
                o_s[i][j][qrows, :] = _unstack_heads(o, j)
                l_s[i][j][qrows, :] = _unstack_heads(lse_b, j)
        step = 256
        for t0 in range(0, s_len, step):
            rs = pl.ds(t0, step)
            for j in range(N_HALF):
                ls = [l_s[i][j][rs, :] for i in range(n_br)]
                m = functools.reduce(jnp.maximum, ls)
                es = [jnp.exp(l - m) for l in ls]
                tot = functools.reduce(lambda a, c: a + c, es)
                inv = 1.0 / tot
                acc = None
                for i in range(n_br):
                    term = (es[i] * inv) * o_s[i][j][rs, :]
                    acc = term if acc is None else acc + term
                o_ref[0, rs, j * HALF_W:(j + 1) * HALF_W] = acc
                lse_ref[0, rs, j * HALF_W:(j + 1) * HALF_W] = m + jnp.log(tot)

    half_buf = pltpu.VMEM((s_len, HALF_W), F32)
    call = SideCopy(side, n_in=N_HALF + 2, n_out=2, grid=(b, N_KV_HEADS))
    return pl.pallas_call(
        call.wrap(body), name=name, grid=(b, N_KV_HEADS), in_specs=q_specs + [kv_spec, t_spec] + call.in_specs,
        out_specs=[o_spec, o_spec] + call.out_specs,
        out_shape=[jax.ShapeDtypeStruct((b, s_len, ATTN_WIDTH), F32)] * 2 + call.out_shape,
        scratch_shapes=[half_buf] * N_HALF + [pltpu.VMEM((ATTN_PAD + s_len, HALF_W), F32)] + [half_buf] * (2 * n_br * N_HALF)
        + [_ATTN_BIAS_BUF] + call.scratch,
        compiler_params=pltpu.CompilerParams(dimension_semantics=("arbitrary", "arbitrary"), vmem_limit_bytes=ATTN_VMEM_LIMIT),
    )(*([proj3] * (N_HALF + 1)), rope_tab, *call.args)


def attn_bwd(proj3, rope_tab, attn3, lse3, d_attn3, *, name, side=None):
    b, s_len, _ = proj3.shape
    q_specs, kv_spec, t_spec, o_spec, half_specs = _attn_in_specs(s_len)

    def body(*refs):
        q_refs = refs[:N_HALF]
        kv_ref, tab_ref, o_ref = refs[N_HALF:N_HALF + 3]
        lse_refs = refs[N_HALF + 3:2 * N_HALF + 3]
        do_refs = refs[2 * N_HALF + 3:3 * N_HALF + 3]
        dq_ref, dkv_ref = refs[3 * N_HALF + 3:3 * N_HALF + 5]
        scratch = refs[3 * N_HALF + 5:]
        q_s, kv_s = scratch[:N_HALF], scratch[N_HALF]
        dl_s = scratch[N_HALF + 1:2 * N_HALF + 1]
        dq_s = scratch[2 * N_HALF + 1:3 * N_HALF + 1]
        dkv_s = scratch[3 * N_HALF + 1]
        bias_s = scratch[-1]
        _attn_prologue(q_refs, kv_ref, tab_ref, q_s, kv_s, pl.program_id(1), s_len)
        _attn_bias(bias_s)
        dkv_s[...] = jnp.zeros_like(dkv_s)
        for j in range(N_HALF):
            dq_s[j][...] = jnp.zeros_like(dq_s[j])
            for e in range(2):
                cols = slice(e * HEAD_DIM, (e + 1) * HEAD_DIM)
                ocols = slice(j * HALF_W + e * HEAD_DIM, j * HALF_W + (e + 1) * HEAD_DIM)
                delta = jnp.sum(do_refs[j][0, :, cols] * o_ref[0, :, ocols], axis=1, keepdims=True)
                dl_s[j][:, cols] = jnp.broadcast_to(delta, (s_len, HEAD_DIM))
        for i, d, q0, k0, n in _attn_blocks(s_len):
            qrows, krows = _rows(q0, ATTN_BLOCK, d), _rows(k0, 2 * ATTN_BLOCK, d)
            qv = _stack_heads([q_s[j][qrows, :] for j in range(N_HALF)]).astype(BF16)
            kvb = kv_s[krows, :].astype(BF16)
            kk, vv = kvb[:, :HEAD_DIM], kvb[:, HEAD_DIM:]
            do16 = _stack_heads([do_refs[j].at[0][qrows, :] for j in range(N_HALF)]).astype(BF16)
            lse = _stack_stats([lse_refs[j].at[0][qrows, :] for j in range(N_HALF)])
            delta = _stack_stats([dl_s[j][qrows, :] for j in range(N_HALF)])
            sc = lax.dot_general(qv, kk, (((1,), (1,)), ((), ())), preferred_element_type=F32)
            pr = jnp.exp(sc + bias_s[min(n, 1)] - lse)
            dv = lax.dot_general(pr.astype(BF16), do16, (((0,), (0,)), ((), ())), preferred_element_type=F32)
            dp = lax.dot_general(do16, vv, (((1,), (1,)), ((), ())), preferred_element_type=F32)
            ds = (pr * (dp - delta)).astype(BF16)
            dq = jnp.dot(ds, kk, preferred_element_type=F32)
            dk = lax.dot_general(ds, qv, (((0,), (0,)), ((), ())), preferred_element_type=F32)
            for j in range(N_HALF):
                dq_s[j][qrows, :] += _unstack_heads(dq, j)
            dkv_s[krows, :] += jnp.concatenate([dk, dv], axis=1)
        swap = _swap_matrix()
        cos_v, sin_v = tab_ref[0, :, :HEAD_DIM], tab_ref[0, :, HEAD_DIM:]
        for j in range(N_HALF):
            for e in range(2):
                cols = slice(e * HEAD_DIM, (e + 1) * HEAD_DIM)
                ocols = slice(j * HALF_W + e * HEAD_DIM, j * HALF_W + (e + 1) * HEAD_DIM)
                dq_ref[0, :, ocols] = _rope(dq_s[j][:, cols], cos_v, sin_v, swap, HEAD_DIM ** -0.5, True).astype(dq_ref.dtype)
        d_k = _rope(dkv_s[ATTN_PAD:ATTN_PAD + s_len, :HEAD_DIM], cos_v, sin_v, swap, 1.0, True)
        d_v = dkv_s[ATTN_PAD:ATTN_PAD + s_len, HEAD_DIM:]
        for h in range(N_KV_HEADS):
            @pl.when(pl.program_id(1) == h)
            def _():
                dkv_ref[0, :, h * HEAD_DIM:(h + 1) * HEAD_DIM] = d_k.astype(dkv_ref.dtype)
                dkv_ref[0, :, LANE + h * HEAD_DIM:LANE + (h + 1) * HEAD_DIM] = d_v.astype(dkv_ref.dtype)

    kv_out = pl.BlockSpec((1, s_len, 2 * LANE), lambda bi, h: (bi, 0, 0))
    kv_shape = jax.ShapeDtypeStruct((b, s_len, 2 * LANE), BF16)
    half_buf = pltpu.VMEM((s_len, HALF_W), F32)
    pad_buf = pltpu.VMEM((ATTN_PAD + s_len, HALF_W), F32)
    call = SideCopy(side, n_in=3 * N_HALF + 3, n_out=2, grid=(b, N_KV_HEADS))
    return pl.pallas_call(
        call.wrap(body), name=name, grid=(b, N_KV_HEADS),
        in_specs=q_specs + [kv_spec, t_spec, o_spec] + half_specs + half_specs + call.in_specs,
        out_specs=[o_spec, kv_out] + call.out_specs,
        out_shape=[jax.ShapeDtypeStruct((b, s_len, ATTN_WIDTH), BF16), kv_shape] + call.out_shape,
        scratch_shapes=[half_buf] * N_HALF + [pad_buf] + [half_buf] * (2 * N_HALF) + [pad_buf, _ATTN_BIAS_BUF] + call.scratch,
        compiler_params=pltpu.CompilerParams(dimension_semantics=("arbitrary", "arbitrary"), vmem_limit_bytes=ATTN_VMEM_LIMIT),
    )(*([proj3] * (N_HALF + 1)), rope_tab, attn3, *([lse3] * N_HALF), *([d_attn3] * N_HALF), *call.args)


CONV_TC = 256
CONV_COL0 = XBC_COL // CONV_TC


def _shift_down(u, s):
    if s == 0:
        return u
    rows = lax.broadcasted_iota(jnp.int32, u.shape, 0)
    return jnp.where(rows >= s, pltpu.roll(u, s, 0), 0.0)


def _shift_up(u, s):
    if s == 0:
        return u
    n = u.shape[0]
    rows = lax.broadcasted_iota(jnp.int32, u.shape, 0)
    return jnp.where(rows < n - s, pltpu.roll(u, n - s, 0), 0.0)


def conv_silu_fwd(proj3, w, bias, *, name):
    b, s, _ = proj3.shape
    u_spec = pl.BlockSpec((1, s, CONV_TC), lambda j, bi: (bi, 0, CONV_COL0 + j))
    o_spec = pl.BlockSpec((1, s, CONV_TC), lambda j, bi: (bi, 0, j))
    w_spec = pl.BlockSpec((CONV_WIDTH, CONV_TC), lambda j, bi: (0, j))
    b_spec = pl.BlockSpec((1, CONV_TC), lambda j, bi: (0, j))

    def body(u_ref, w_ref, b_ref, o_ref):
        u = u_ref[0]
        y = jnp.broadcast_to(b_ref[...], u.shape)
        for k in range(CONV_WIDTH):
            y = y + w_ref[k:k + 1, :] * _shift_down(u, CONV_WIDTH - 1 - k)
        o_ref[0] = y * jax.nn.sigmoid(y)

    return pl.pallas_call(
        body, name=name, grid=(CONV_CH // CONV_TC, b), in_specs=[u_spec, w_spec, b_spec], out_specs=o_spec,
        out_shape=jax.ShapeDtypeStruct((b, s, CONV_CH), F32),
        compiler_params=_params(("parallel", "arbitrary")),
    )(proj3, w, bias)


def conv_silu_bwd(proj3, w, bias, dact, *, name):
    b, s, _ = proj3.shape
    u_spec = pl.BlockSpec((1, s, CONV_TC), lambda j, bi: (bi, 0, CONV_COL0 + j))
    o_spec = pl.BlockSpec((1, s, CONV_TC), lambda j, bi: (bi, 0, j))
    w_spec = pl.BlockSpec((CONV_WIDTH, CONV_TC), lambda j, bi: (0, j))
    b_spec = pl.BlockSpec((1, CONV_TC), lambda j, bi: (0, j))

    def body(u_ref, w_ref, b_ref, g_ref, du_ref, dw_ref, db_ref):
        bi = pl.program_id(1)

        @pl.when(bi == 0)
        def _():
            dw_ref[...] = jnp.zeros_like(dw_ref)
            db_ref[...] = jnp.zeros_like(db_ref)

        u = u_ref[0]
        y = jnp.broadcast_to(b_ref[...], u.shape)
        shifted = [_shift_down(u, CONV_WIDTH - 1 - k) for k in range(CONV_WIDTH)]
        for k in range(CONV_WIDTH):
            y = y + w_ref[k:k + 1, :] * shifted[k]
        sig = jax.nn.sigmoid(y)
        dy = g_ref[0] * (sig * (1.0 + y * (1.0 - sig)))
        du = jnp.zeros_like(u)
        for k in range(CONV_WIDTH):
            du = du + w_ref[k:k + 1, :] * _shift_up(dy, CONV_WIDTH - 1 - k)
            dw_ref[k:k + 1, :] += jnp.sum(dy * shifted[k], axis=0, keepdims=True)
        du_ref[0] = du.astype(du_ref.dtype)
        db_ref[...] += jnp.sum(dy, axis=0, keepdims=True)

    return pl.pallas_call(
        body, name=name, grid=(CONV_CH // CONV_TC, b), in_specs=[u_spec, w_spec, b_spec, o_spec],
        out_specs=[o_spec, w_spec, b_spec],
        out_shape=[jax.ShapeDtypeStruct((b, s, CONV_CH), BF16), jax.ShapeDtypeStruct((CONV_WIDTH, CONV_CH), F32),
                   jax.ShapeDtypeStruct((1, CONV_CH), F32)],
        compiler_params=_params(("parallel", "arbitrary")),
    )(proj3, w, bias, dact)


SSD_INTERLEAVE = 8
SSD_INTERLEAVE_FWD = 1


def _softplus(z):
    e = jnp.exp(-jnp.abs(z))
    u = 1.0 + e
    log1p = jnp.where(u == 1.0, e, jnp.log(u) * e / jnp.where(u == 1.0, 1.0, u - 1.0))
    return jnp.maximum(z, 0.0) + log1p


def _tri(lower):
    r = lax.broadcasted_iota(jnp.int32, (CHUNK, CHUNK), 0)
    c = lax.broadcasted_iota(jnp.int32, (CHUNK, CHUNK), 1)
    return (r >= c) if lower else (r <= c)


def _ssd_common(dtr_ref, dtb_ref, alog_ref):
    z = dtr_ref[0] + dtb_ref[...]
    dt = _softplus(z)
    aneg = -jnp.exp(alog_ref[...])
    acs = _dot01_left(_tri(True).astype(BF16), dt * aneg)
    return z, dt, aneg, acs


def _ssd_specs(nc, reverse):
    cidx = (lambda c: nc - 1 - c) if reverse else (lambda c: c)
    act_spec = pl.BlockSpec((1, CHUNK, CONV_CH), lambda b, c: (b, cidx(c), 0))
    y_spec = pl.BlockSpec((1, CHUNK, SSM_INNER), lambda b, c: (b, cidx(c), 0))
    dt_in_spec = pl.BlockSpec((1, CHUNK, LANE), lambda b, c: (b, cidx(c), DT_COL // LANE))
    dt_out_spec = pl.BlockSpec((1, CHUNK, LANE), lambda b, c: (b, cidx(c), 0))
    par_spec = pl.BlockSpec((1, LANE), lambda b, c: (0, 0))
    h_spec = pl.BlockSpec((1, SSM_HEADS, 1, SSM_P, D_STATE), lambda b, c: (b, 0, cidx(c), 0, 0))
    return act_spec, y_spec, dt_in_spec, dt_out_spec, par_spec, h_spec


def _head_cols(h):
    return slice(h * SSM_P, (h + 1) * SSM_P)


def _group_cols(g, which):
    start = SSM_INNER + which * SSM_GROUPS * D_STATE + g * D_STATE
    return slice(start, start + D_STATE)


def _each(f, *lists):
    return [f(*a) for a in zip(*lists)]


def _nt(a, b):
    return lax.dot_general(a, b, (((1,), (1,)), ((), ())), preferred_element_type=F32)


def _tn(a, b):
    return lax.dot_general(a, b, (((0,), (0,)), ((), ())), preferred_element_type=F32)


def _nn(a, b):
    return jnp.dot(a, b, preferred_element_type=F32)


def _rowsum(a):
    return jnp.sum(a, axis=1, keepdims=True)


def _colsum(a):
    return jnp.sum(a, axis=0, keepdims=True)


def _bf(a):
    return a.astype(BF16)


def _head_batches(g, width=SSD_INTERLEAVE):
    first = g * HEADS_PER_GROUP
    return [list(range(first + k, first + k + width)) for k in range(0, HEADS_PER_GROUP, width)]


def _decay_matrix(acs_j, acs_row, tri_mask):
    dm = jnp.broadcast_to(acs_j, (CHUNK, CHUNK)) - jnp.broadcast_to(acs_row, (CHUNK, CHUNK))
    return jnp.where(tri_mask, jnp.exp(jnp.where(tri_mask, dm, 0.0)), 0.0)


def ssd_fwd(act3, proj3, dtb, alog, dsk, *, name, side=None):
    b, s, _ = act3.shape
    nc = s // CHUNK
    act_spec, y_spec, dt_in_spec, _, par_spec, h_spec = _ssd_specs(nc, False)

    def body(act_ref, dtr_ref, dtb_ref, alog_ref, dsk_ref, y_ref, hp_ref, state):
        c = pl.program_id(1)

        @pl.when(c == 0)
        def _():
            state[...] = jnp.zeros_like(state)

        _, dt, _, acs = _ssd_common(dtr_ref, dtb_ref, alog_ref)
        acs_t = acs.T
        tri_mask = _tri(True)
        last_row = (lax.broadcasted_iota(jnp.int32, (CHUNK, 1), 0) == CHUNK - 1).astype(F32)
        for g in range(SSM_GROUPS):
            b16 = _bf(act_ref[0, :, _group_cols(g, 0)])
            c16 = _bf(act_ref[0, :, _group_cols(g, 1)])
            cb = _nt(c16, b16)
            for hs in _head_batches(g, SSD_INTERLEAVE_FWD):
                x = [act_ref[0, :, _head_cols(h)] for h in hs]
                dt_j = [dt[:, h:h + 1] for h in hs]
                acs_j = [acs[:, h:h + 1] for h in hs]
                acs_last = [_colsum(a * last_row) for a in acs_j]
                xg = _each(lambda xv, d: xv * d, x, dt_j)
                mm = [cb * _decay_matrix(a, acs_t[h:h + 1, :], tri_mask) for a, h in zip(acs_j, hs)]
                decay_s = _each(lambda al, a: jnp.exp(al - a), acs_last, acs_j)
                y_diag = _each(lambda m_, v: _nn(_bf(m_), _bf(v)), mm, xg)
                st = _each(lambda v, d: _tn(_bf(v * d), b16), xg, decay_s)
                hp = [state[h] for h in hs]
                for h, v in zip(hs, hp):
                    hp_ref[0, h, 0] = v
                y_off = [_nt(c16, _bf(v)) for v in hp]
                for h, yd, yo, a, xv in zip(hs, y_diag, y_off, acs_j, x):
                    y_ref[0, :, _head_cols(h)] = yd + yo * jnp.exp(a) + dsk_ref[:, h:h + 1] * xv
                for h, v, al, sv in zip(hs, hp, acs_last, st):
                    state[h] = v * jnp.exp(al) + sv

    call = SideCopy(side, n_in=5, n_out=2, grid=(b, nc))
    return pl.pallas_call(
        call.wrap(body), name=name, grid=(b, nc),
        in_specs=[act_spec, dt_in_spec, par_spec, par_spec, par_spec] + call.in_specs,
        out_specs=[y_spec, h_spec] + call.out_specs,
        out_shape=[jax.ShapeDtypeStruct((b, s, SSM_INNER), F32),
                   jax.ShapeDtypeStruct((b, SSM_HEADS, nc, SSM_P, D_STATE), F32)] + call.out_shape,
        scratch_shapes=[pltpu.VMEM((SSM_HEADS, SSM_P, D_STATE), F32)] + call.scratch,
        compiler_params=_params(("arbitrary", "arbitrary")),
    )(act3, proj3, dtb, alog, dsk, *call.args)


def ssd_bwd(act3, proj3, dtb, alog, dsk, hprev, dy3, *, name, side=None):
    b, s, _ = act3.shape
    nc = s // CHUNK
    act_spec, y_spec, dt_in_spec, dt_out_spec, par_spec, h_spec = _ssd_specs(nc, True)
    dpar_spec = pl.BlockSpec((8, LANE), lambda bi, c: (0, 0))

    def body(act_ref, dtr_ref, dtb_ref, alog_ref, dsk_ref, hp_ref, dy_ref, dact_ref, ddtr_ref, dpar_ref, dstate):
        bi, c = pl.program_id(0), pl.program_id(1)

        @pl.when(c == 0)
        def _():
            dstate[...] = jnp.zeros_like(dstate)

        @pl.when((bi == 0) & (c == 0))
        def _():
            dpar_ref[...] = jnp.zeros_like(dpar_ref)

        z, dt, aneg, acs = _ssd_common(dtr_ref, dtb_ref, alog_ref)
        acs_t = acs.T
        tri_mask = _tri(True)
        last_row = (lax.broadcasted_iota(jnp.int32, (CHUNK, 1), 0) == CHUNK - 1).astype(F32)
        lanes = lax.broadcasted_iota(jnp.int32, (1, LANE), 1)
        sublanes = lax.broadcasted_iota(jnp.int32, (LANE, 1), 0)
        ddt_mat = jnp.zeros((CHUNK, LANE), F32)
        dacs_mat = jnp.zeros((CHUNK, LANE), F32)
        dacs_rows = jnp.zeros((LANE, CHUNK), F32)
        ddsk_row = jnp.zeros((1, LANE), F32)
        for g in range(SSM_GROUPS):
            b16 = _bf(act_ref[0, :, _group_cols(g, 0)])
            c16 = _bf(act_ref[0, :, _group_cols(g, 1)])
            cb = _nt(c16, b16)
            dcb = jnp.zeros((CHUNK, CHUNK), F32)
            db_acc = jnp.zeros((CHUNK, D_STATE), F32)
            dc_acc = jnp.zeros((CHUNK, D_STATE), F32)
            for hs in _head_batches(g):
                x = [act_ref[0, :, _head_cols(h)] for h in hs]
                g_y = [dy_ref[0, :, _head_cols(h)] for h in hs]
                hp = [hp_ref[0, h, 0] for h in hs]
                g_hn = [dstate[h] for h in hs]
                dt_j = [dt[:, h:h + 1] for h in hs]
                acs_j = [acs[:, h:h + 1] for h in hs]
                acs_last = [_colsum(a * last_row) for a in acs_j]
                xg = _each(lambda xv, d: xv * d, x, dt_j)
                lm = [_decay_matrix(a, acs_t[h:h + 1, :], tri_mask) for a, h in zip(acs_j, hs)]
                mm = [cb * l for l in lm]
                decay_s = _each(lambda al, a: jnp.exp(al - a), acs_last, acs_j)
                ea = [jnp.exp(a) for a in acs_j]
                cd = [jnp.exp(al) for al in acs_last]
                g_y16, xg16, hp16, g_hn16 = [[_bf(v) for v in vs] for vs in (g_y, xg, hp, g_hn)]
                d_mm = _each(_nt, g_y16, xg16)
                d_xg = _each(lambda m_, gy: _tn(_bf(m_), gy), mm, g_y16)
                d_dm = _each(lambda a, m_: a * m_, d_mm, mm)
                d_acs = [_rowsum(v) for v in d_dm]
                t_off = [_nt(c16, v) for v in hp16]
                d_t16 = _each(lambda gy, e: _bf(gy * e), g_y, ea)
                d_acs = _each(lambda da, gy, t, e: da + _rowsum(gy * t) * e, d_acs, g_y, t_off, ea)
                d_hp = _each(lambda dtv, gh, cdv: _tn(dtv, c16) + gh * cdv, d_t16, g_hn, cd)
                d_w = [_nt(b16, v) for v in g_hn16]
                d_xg = _each(lambda dx, dw, ds: dx + dw * ds, d_xg, d_w, decay_s)
                d_ds = _each(lambda dw, v, ds: _rowsum(dw * v) * ds, d_w, xg, decay_s)
                d_last = _each(lambda gh, hv, cdv, dd: _colsum(_rowsum(gh * hv)) * cdv + _colsum(dd), g_hn, hp, cd, d_ds)
                d_acs = _each(lambda da, dd, dl: da - dd + dl * last_row, d_acs, d_ds, d_last)
                for h, gy, dx, d, xv in zip(hs, g_y, d_xg, dt_j, x):
                    dact_ref[0, :, _head_cols(h)] = dsk_ref[:, h:h + 1] * gy + dx * d
                for h, v in zip(hs, d_hp):
                    dstate[h] = v
                for k, h in enumerate(hs):
                    onehot = (lanes == h).astype(F32)
                    dcb = dcb + d_mm[k] * lm[k]
                    dc_acc = dc_acc + _nn(d_t16[k], hp16[k])
                    db_acc = db_acc + _nn(_bf(xg[k] * decay_s[k]), g_hn16[k])
                    ddsk_row = ddsk_row + _colsum(_rowsum(g_y[k] * x[k])) * onehot
                    ddt_mat = ddt_mat + _rowsum(d_xg[k] * x[k]) * onehot
                    dacs_mat = dacs_mat + d_acs[k] * onehot
                    dacs_rows = dacs_rows + (sublanes == h).astype(F32) * _colsum(d_dm[k])
            dcb16 = _bf(dcb)
            dact_ref[0, :, _group_cols(g, 1)] = dc_acc + _nn(dcb16, b16)
            dact_ref[0, :, _group_cols(g, 0)] = db_acc + _tn(dcb16, c16)
        d_a = _dot01_left(_tri(False).astype(BF16), dacs_mat - dacs_rows.T)
        ddt_mat = ddt_mat + d_a * aneg
        d_raw = ddt_mat * jax.nn.sigmoid(z)
        ddtr_ref[0] = d_raw
        dpar_ref[0:1, :] += _colsum(d_raw)
        dpar_ref[1:2, :] += _colsum(d_a * dt) * aneg
        dpar_ref[2:3, :] += ddsk_row

    call = SideCopy(side, n_in=7, n_out=3, grid=(b, nc))
    return pl.pallas_call(
        call.wrap(body), name=name, grid=(b, nc),
        in_specs=[act_spec, dt_in_spec, par_spec, par_spec, par_spec, h_spec, y_spec] + call.in_specs,
        out_specs=[act_spec, dt_out_spec, dpar_spec] + call.out_specs,
        out_shape=[jax.ShapeDtypeStruct(act3.shape, F32), jax.ShapeDtypeStruct((b, s, LANE), F32),
                   jax.ShapeDtypeStruct((8, LANE), F32)] + call.out_shape,
        scratch_shapes=[pltpu.VMEM((SSM_HEADS, SSM_P, D_STATE), F32)] + call.scratch,
        compiler_params=_params(("arbitrary", "arbitrary")),
    )(act3, proj3, dtb, alog, dsk, hprev, dy3, *call.args)


def rotary_tables(positions):
    inv_freq = ROPE_THETA ** (-jnp.arange(0, ROPE_DIM, 2, dtype=F32) / ROPE_DIM)
    ang = positions.astype(F32)[..., None] * inv_freq
    cos, sin = jnp.cos(ang), jnp.sin(ang)
    rest = HEAD_DIM - ROPE_DIM
    cosf = jnp.concatenate([cos, cos, jnp.ones(cos.shape[:2] + (rest,), F32)], axis=-1)
    sinf = jnp.concatenate([-sin, sin, jnp.zeros(sin.shape[:2] + (rest,), F32)], axis=-1)
    return cosf, sinf


def w_in_columns(w):
    pad = jnp.zeros((w.shape[0], IN_PAD - IN_PROJ), w.dtype)
    return jnp.concatenate([w[:, :Q_END], w[:, V_END:XBC_END], w[:, Q_END:V_END], w[:, XBC_END:], pad], axis=1)


def w_in_grad_columns(g):
    return jnp.concatenate([g[:, :Z_COL], g[:, K_COL:DT_COL], g[:, Z_COL:K_COL], g[:, DT_COL:DT_COL + SSM_HEADS]], axis=1)


def lane_pad(v):
    return jnp.pad(v.reshape(1, -1), ((0, 0), (0, LANE - v.shape[-1])))


def layer_fwd(h, wts, small, rope_tab, b, s, tag, attn_side=None, rest_from=None, ssd_side=None):
    w_in = wts[0]
    t = b * s
    sv = {"h": h}
    hn = rowwise_fwd(rms_fn, [h], [small["norm_mix"]], [BF16], name=f"rms_mix_{tag}")[0]
    proj = matmul(hn, w_in, name=f"in_proj_{tag}")
    sv["hn"], sv["proj"] = hn, proj
    proj3 = proj.reshape(b, s, IN_PAD)
    attn3, lse3, *attn_out = attn_fwd(proj3, rope_tab, name=f"attn_{tag}", side=attn_side)
    if rest_from is not None:
        wts = (w_in,) + tuple(rest_from(attn_out))
    _, w_out, w_gate, w_up, w_down = wts
    sv["attn3"], sv["lse3"] = attn3, lse3
    attn = attn3.reshape(t, ATTN_WIDTH)
    act3 = conv_silu_fwd(proj3, small["conv_w"], small["conv_b"], name=f"conv_{tag}")
    y3, hprev, *ssd_out = ssd_fwd(act3, proj3, small["dt_bias"], small["a_log"], small["d_skip"], name=f"ssd_{tag}",
                                  side=ssd_side)
    y = y3.reshape(t, SSM_INNER)
    sv["act3"], sv["hprev"], sv["y"] = act3, hprev, y
    gn = rowwise_fwd(gated_norm_fn, [y, proj], [small["ssm_norm"]], [BF16], name=f"gated_norm_{tag}", groups=SSM_GROUPS,
                     windows=[None, (Z_COL, SSM_INNER)])[0]
    sv["gn"] = gn
    h1 = matmul([attn, gn], w_out, name=f"out_proj_{tag}", residual=h)
    sv["h1"] = h1
    hn2 = rowwise_fwd(rms_fn, [h1], [small["norm_ffn"]], [BF16], name=f"rms_ffn_{tag}")[0]
    gate = matmul(hn2, w_gate, out_dtype=BF16, name=f"ffn_gate_{tag}")
    up = matmul(hn2, w_up, out_dtype=BF16, name=f"ffn_up_{tag}")
    act2 = rowwise_fwd(swiglu_fn, [gate, up], [], [BF16], name=f"swiglu_{tag}")[0]
    sv["hn2"], sv["gate"], sv["up"], sv["act2"] = hn2, gate, up, act2
    h2 = matmul(act2, w_down, name=f"ffn_down_{tag}", residual=h1)
    return h2, sv, wts, (ssd_out or None)


def layer_bwd(dh2, sv, wts, small, rope_tab, b, s, tag, ssd_side=None, attn_side_fn=None):
    w_in, w_out, w_gate, w_up, w_down = wts
    t = b * s
    gr = {}
    d_act2 = matmul(dh2, w_down, tb=True, out_dtype=BF16, name=f"ffn_down_dx_{tag}")
    gr["w_down"] = matmul(sv["act2"], dh2, ta=True, out_dtype=BF16, name=f"ffn_down_dw_{tag}")
    d_gate, d_up = rowwise_bwd(swiglu_fn, [sv["gate"], sv["up"]], [], [d_act2], [BF16, BF16], name=f"swiglu_bwd_{tag}")
    gr["w_gate"] = matmul(sv["hn2"], d_gate, ta=True, out_dtype=BF16, name=f"ffn_gate_dw_{tag}")
    gr["w_up"] = matmul(sv["hn2"], d_up, ta=True, out_dtype=BF16, name=f"ffn_up_dw_{tag}")
    d_hn2 = matmul(d_gate, w_gate, tb=True, name=f"ffn_gate_dx_{tag}")
    d_hn2 = matmul(d_up, w_up, tb=True, residual=d_hn2, name=f"ffn_up_dx_{tag}")
    dh1, gr["norm_ffn"] = rowwise_bwd(rms_fn, [sv["h1"]], [small["norm_ffn"]], [d_hn2], [F32],
                                      name=f"rms_ffn_bwd_{tag}", add_to_first=dh2)
    d_cat = matmul(dh1, w_out, tb=True, name=f"out_proj_dx_{tag}")
    gr["w_out"] = jnp.concatenate([
        matmul(sv["attn3"].reshape(t, ATTN_WIDTH), dh1, ta=True, out_dtype=BF16, name=f"out_proj_dw_attn_{tag}"),
        matmul(sv["gn"], dh1, ta=True, out_dtype=BF16, name=f"out_proj_dw_ssd_{tag}")], axis=0)
    d_y, d_z, gr["ssm_norm"] = rowwise_bwd(gated_norm_fn, [sv["y"], sv["proj"]], [small["ssm_norm"]], [d_cat], [F32, BF16],
                                           name=f"gated_norm_bwd_{tag}", groups=SSM_GROUPS,
                                           windows=[None, (Z_COL, SSM_INNER)], ct_windows=[(ATTN_WIDTH, SSM_INNER)])
    proj3 = sv["proj"].reshape(b, s, IN_PAD)
    d_act3, d_dtr, d_par, *ssd_out = ssd_bwd(sv["act3"], proj3, small["dt_bias"], small["a_log"], small["d_skip"],
                                             sv["hprev"], d_y.reshape(b, s, SSM_INNER), name=f"ssd_bwd_{tag}", side=ssd_side)
    gr["dt_bias"], gr["a_log"], gr["d_skip"] = d_par[0, :SSM_HEADS], d_par[1, :SSM_HEADS], d_par[2, :SSM_HEADS]
    d_xbc, gr["conv_w"], gr["conv_b"] = conv_silu_bwd(proj3, small["conv_w"], small["conv_b"], d_act3,
                                                      name=f"conv_bwd_{tag}")
    attn_side = attn_side_fn(gr) if attn_side_fn is not None else None
    d_q3, d_kv3, *attn_out = attn_bwd(proj3, rope_tab, sv["attn3"], sv["lse3"], d_cat.reshape(b, s, MIX_WIDTH),
                                      name=f"attn_bwd_{tag}", side=attn_side)
    d_proj = [d_q3.reshape(t, ATTN_WIDTH), d_z, d_xbc.reshape(t, CONV_CH), d_kv3.reshape(t, 2 * LANE),
              d_dtr.reshape(t, LANE)]
    d_hn = matmul(d_proj, w_in, tb=True, name=f"in_proj_dx_{tag}")
    gr["w_in"] = w_in_grad_columns(jnp.concatenate(
        [matmul(sv["hn"], part, ta=True, out_dtype=BF16, name=f"in_proj_dw_{k}_{tag}") for k, part in enumerate(d_proj)],
        axis=1))
    dh, gr["norm_mix"] = rowwise_bwd(rms_fn, [sv["h"]], [small["norm_mix"]], [d_hn], [F32],
                                     name=f"rms_mix_bwd_{tag}", add_to_first=dh1)
    return dh, gr, (ssd_out or None), (attn_out or None)


def local_step(x, positions, big, small_all, final_norm, loss_target, *, plan=None):
    b, s, _ = x.shape
    t = b * s
    rope_tab = jnp.concatenate(rotary_tables(positions), axis=-1)
    h = x.reshape(t, D_MODEL)
    saved, big = [], list(big)
    for l in range(DEPTH):
        kw = {}
        if plan is not None and l == 0:
            kw = dict(attn_side=(plan["rest0"], False), rest_from=plan["make_rest0"], ssd_side=(plan["late"], False))
        h, sv, big[l], got = layer_fwd(h, big[l], small_all[l], rope_tab, b, s, f"l{l}", **kw)
        if got is not None:
            big[DEPTH - 1] = plan["make_late"](got)
        saved.append(sv)
    dh, d_final, loss = loss_and_grad(h, loss_target.reshape(t, D_MODEL), final_norm.reshape(1, D_MODEL))
    grads, received = [None] * DEPTH, {}
    for l in reversed(range(DEPTH)):
        kw = {}
        if plan is not None and l == 0:
            kw = dict(ssd_side=(plan["grads_late"](grads[DEPTH - 1]), True),
                      attn_side_fn=lambda gr: (plan["grads_rest0"](gr), True))
        dh, grads[l], got_ssd, got_attn = layer_bwd(dh, saved[l], big[l], small_all[l], rope_tab, b, s, f"l{l}", **kw)
        if got_ssd is not None:
            received["late"] = got_ssd
        if got_attn is not None:
            received["rest0"] = got_attn
    return loss, dh.reshape(b, s, D_MODEL), grads, d_final, received


def _slab_rows(r):
    return r if r <= 512 else _pick(r, (512, 352, 256, 128, 8))


def cast_bf16(x, *, name):
    def fn(v):
        return (v,)
    return rowwise_fwd(fn, [x], [], [BF16], name=name, tr=_slab_rows(x.shape[0]))[0]


def sum_slots(x, *, name):
    n, r, c = x.shape
    tr = _slab_rows(r)

    def body(x_ref, o_ref):
        acc = x_ref[0].astype(F32)
        for i in range(1, n):
            acc = acc + x_ref[i].astype(F32)
        o_ref[...] = acc

    return pl.pallas_call(
        body, name=name, grid=(r // tr,), in_specs=[pl.BlockSpec((n, tr, c), lambda i: (0, i, 0))],
        out_specs=pl.BlockSpec((tr, c), lambda i: (i, 0)), out_shape=jax.ShapeDtypeStruct((r, c), F32),
        compiler_params=_params(("parallel",)),
    )(x)


def adamw(g, w, m, v, *, name):
    r, c = w.shape
    tr = _slab_rows(r)
    bc1 = 1.0 / (1.0 - ADAM_B1 ** ADAM_STEP)
    bc2 = 1.0 / (1.0 - ADAM_B2 ** ADAM_STEP)

    def body(g_ref, w_ref, m_ref, v_ref, g_out, d_out, m_out, v_out):
        gv = g_ref[...]
        m_new = ADAM_B1 * m_ref[...] + (1.0 - ADAM_B1) * gv
        v_new = ADAM_B2 * v_ref[...] + (1.0 - ADAM_B2) * (gv * gv)
        g_out[...] = gv
        m_out[...] = m_new
        v_out[...] = v_new
        d_out[...] = -ADAM_LR * ((m_new * bc1) / (jnp.sqrt(v_new * bc2) + ADAM_EPS) + ADAM_WD * w_ref[...])

    spec = pl.BlockSpec((tr, c), lambda i: (i, 0))
    return pl.pallas_call(
        body, name=name, grid=(r // tr,), in_specs=[spec] * 4, out_specs=[spec] * 4,
        out_shape=[jax.ShapeDtypeStruct((r, c), F32)] * 4, compiler_params=_params(("parallel",)),
    )(g, w, m, v)


def adamw_layers(g_parts, w, m, v, *, name):
    depth, a, b = w.shape
    tr = _pick(a, (256, 352, 192, 128, 8))
    counts = [len(p) for p in g_parts]
    flat_parts = [q for p in g_parts for q in p]
    bc1 = 1.0 / (1.0 - ADAM_B1 ** ADAM_STEP)
    bc2 = 1.0 / (1.0 - ADAM_B2 ** ADAM_STEP)

    def body(*refs):
        layer = pl.program_id(0)
        g, off = None, 0
        for l, cnt in enumerate(counts):
            g_l = refs[off][...]
            for r_ in refs[off + 1:off + cnt]:
                g_l = g_l + r_[...]
            off += cnt
            g = g_l if g is None else jnp.where(layer == l, g_l, g)
        w_ref, m_ref, v_ref, g_out, d_out, m_out, v_out = refs[off:]
        m_new = ADAM_B1 * m_ref[0] + (1.0 - ADAM_B1) * g
        v_new = ADAM_B2 * v_ref[0] + (1.0 - ADAM_B2) * (g * g)
        g_out[0] = g
        m_out[0] = m_new
        v_out[0] = v_new
        d_out[0] = -ADAM_LR * ((m_new * bc1) / (jnp.sqrt(v_new * bc2) + ADAM_EPS) + ADAM_WD * w_ref[0])

    g_spec = pl.BlockSpec((tr, b), lambda l, i: (i, 0))
    spec = pl.BlockSpec((1, tr, b), lambda l, i: (l, i, 0))
    return pl.pallas_call(
        body, name=name, grid=(depth, a // tr), in_specs=[g_spec] * len(flat_parts) + [spec] * 3, out_specs=[spec] * 4,
        out_shape=[jax.ShapeDtypeStruct(w.shape, F32)] * 4, compiler_params=_params(("parallel", "parallel")),
    )(*flat_parts, w, m, v)


SWAP_PIECES = 4


def _other_chips(x, y):
    return [(1 - x, y), (x, 1 - y), (1 - x, 1 - y)]


def allgather_chips(shards):
    n_arr = len(shards)

    def body(*refs):
        in_refs, out_refs = refs[:n_arr], refs[n_arr:2 * n_arr]
        send_sems, recv_sems, local_sems = refs[2 * n_arr:]
        x, y, c = lax.axis_index("x"), lax.axis_index("y"), lax.axis_index("c")
        chip = 2 * x + y
        started = []
        for a, (in_ref, out_ref) in enumerate(zip(in_refs, out_refs)):
            mine = pltpu.make_async_copy(in_ref, out_ref.at[chip], local_sems.at[a])
            mine.start()
            started.append(mine.wait)
            for k, (px, py) in enumerate(_other_chips(x, y)):
                cp = pltpu.make_async_remote_copy(src_ref=in_ref, dst_ref=out_ref.at[chip], send_sem=send_sems.at[3 * a + k],
                                                  recv_sem=recv_sems.at[3 * a + k], device_id=(px, py, c), device_id_type=MESH)
                cp.start()
                started.append(cp.wait_send)
        for a, (in_ref, out_ref) in enumerate(zip(in_refs, out_refs)):
            for k, (px, py) in enumerate(_other_chips(x, y)):
                pltpu.make_async_remote_copy(src_ref=in_ref, dst_ref=out_ref.at[2 * px + py], send_sem=send_sems.at[3 * a + k],
                                             recv_sem=recv_sems.at[3 * a + k], device_id=(px, py, c),
                                             device_id_type=MESH).wait_recv()
        for wait in started:
            wait()

    hbm = pl.BlockSpec(memory_space=pltpu.HBM)
    return pl.pallas_call(
        body, name="allgather_weights", in_specs=[hbm] * n_arr, out_specs=[hbm] * n_arr,
        out_shape=[jax.ShapeDtypeStruct((N_CHIPS,) + s.shape, s.dtype) for s in shards],
        scratch_shapes=[pltpu.SemaphoreType.DMA((3 * n_arr,)), pltpu.SemaphoreType.DMA((3 * n_arr,)),
                        pltpu.SemaphoreType.DMA((n_arr,))],
    )(*shards)


def exchange_grads(big, small):
    def body(big_ref, small_ref, big_out, small_out, send_sems, recv_sems, local_sems):
        x, y, c = lax.axis_index("x"), lax.axis_index("y"), lax.axis_index("c")
        chip = 2 * x + y
        dev = 4 * x + 2 * y + c
        own_big = pltpu.make_async_copy(big_ref.at[chip], big_out.at[chip], local_sems.at[0])
        own_small = pltpu.make_async_copy(small_ref, small_out.at[dev], local_sems.at[1])
        own_big.start()
        own_small.start()
        sends = []
        for k, (px, py) in enumerate(_other_chips(x, y)):
            cp = pltpu.make_async_remote_copy(src_ref=big_ref.at[2 * px + py], dst_ref=big_out.at[chip],
                                              send_sem=send_sems.at[k], recv_sem=recv_sems.at[k],
                                              device_id=(px, py, c), device_id_type=MESH)
            cp.start()
            sends.append(cp)
        peers = []
        for r in range(1, N_DEV):
            fx, fy, fc = (r >> 2) & 1, (r >> 1) & 1, r & 1
            px, py, pc = (x + fx) % 2, (y + fy) % 2, (c + fc) % 2
            peers.append((px, py, pc))
            cp = pltpu.make_async_remote_copy(src_ref=small_ref, dst_ref=small_out.at[dev], send_sem=send_sems.at[2 + r],
                                              recv_sem=recv_sems.at[2 + r], device_id=(px, py, pc), device_id_type=MESH)
            cp.start()
            sends.append(cp)
        for k, (px, py) in enumerate(_other_chips(x, y)):
            pltpu.make_async_remote_copy(src_ref=big_ref.at[chip], dst_ref=big_out.at[2 * px + py],
                                         send_sem=send_sems.at[k], recv_sem=recv_sems.at[k],
                                         device_id=(px, py, c), device_id_type=MESH).wait_recv()
        for r, (px, py, pc) in zip(range(1, N_DEV), peers):
            pltpu.make_async_remote_copy(src_ref=small_ref, dst_ref=small_out.at[4 * px + 2 * py + pc],
                                         send_sem=send_sems.at[2 + r], recv_sem=recv_sems.at[2 + r],
                                         device_id=(px, py, pc), device_id_type=MESH).wait_recv()
        for cp in sends:
            cp.wait_send()
        own_big.wait()
        own_small.wait()

    hbm = pl.BlockSpec(memory_space=pltpu.HBM)
    n_sem = 3 + N_DEV - 1
    return pl.pallas_call(
        body, name="exchange_grads", in_specs=[hbm, hbm], out_specs=[hbm, hbm],
        out_shape=[jax.ShapeDtypeStruct(big.shape, big.dtype), jax.ShapeDtypeStruct((N_DEV,) + small.shape, small.dtype)],
        scratch_shapes=[pltpu.SemaphoreType.DMA((n_sem,)), pltpu.SemaphoreType.DMA((n_sem,)), pltpu.SemaphoreType.DMA((2,))],
    )(big, small)


def swap_cores_list(arrays):
    n = len(arrays)

    def body(*refs):
        ins, outs, (send_sems, recv_sems) = refs[:n], refs[n:2 * n], refs[2 * n:]
        x, y, c = lax.axis_index("x"), lax.axis_index("y"), lax.axis_index("c")
        copies = []
        for k in range(n):
            rows = ins[k].shape[0] // SWAP_PIECES
            for p in range(SWAP_PIECES):
                part = pl.ds(p * rows, rows)
                copies.append(pltpu.make_async_remote_copy(
                    src_ref=ins[k].at[part], dst_ref=outs[k].at[part], send_sem=send_sems.at[k * SWAP_PIECES + p],
                    recv_sem=recv_sems.at[k * SWAP_PIECES + p], device_id=(x, y, 1 - c), device_id_type=MESH))
        for cp in copies:
            cp.start()
        for cp in copies:
            cp.wait_recv()
        for cp in copies:
            cp.wait_send()

    assert all(a.shape[0] % (8 * SWAP_PIECES) == 0 for a in arrays)
    hbm = pl.BlockSpec(memory_space=pltpu.HBM)
    return pl.pallas_call(
        body, name="swap_cores", in_specs=[hbm] * n, out_specs=[hbm] * n,
        out_shape=[jax.ShapeDtypeStruct(a.shape, a.dtype) for a in arrays],
        scratch_shapes=[pltpu.SemaphoreType.DMA((n * SWAP_PIECES,)), pltpu.SemaphoreType.DMA((n * SWAP_PIECES,))],
    )(*arrays)


BIG_NAMES = ("w_in", "w_out", "w_gate", "w_up", "w_down")
BIG_SHARD_AXIS = {"w_in": 1, "w_out": 0, "w_gate": 1, "w_up": 1, "w_down": 0}
SMALL_NAMES = ("norm_mix", "conv_w", "conv_b", "dt_bias", "a_log", "d_skip", "ssm_norm", "norm_ffn")


def pack_small(parts):
    flat = jnp.concatenate([p.reshape(-1).astype(F32) for p in parts])
    rows = -(-flat.size // LANE)
    rows = -(-rows // 8) * 8
    return jnp.pad(flat, (0, rows * LANE - flat.size)).reshape(rows, LANE)


def unpack_small(packed, like):
    out, off = [], 0
    flat = packed.reshape(-1)
    for a in like:
        out.append(flat[off:off + a.size].reshape(a.shape))
        off += a.size
    return out


def kernel(x, positions, norm_mix, w_in, conv_w, conv_b, dt_bias, a_log, d_skip, ssm_norm, w_out, norm_ffn, w_gate, w_up, w_down, final_norm, loss_target, m_norm_mix, m_w_in, m_conv_w, m_conv_b, m_dt_bias, m_a_log, m_d_skip, m_ssm_norm, m_w_out, m_norm_ffn, m_w_gate, m_w_up, m_w_down, m_final_norm, v_norm_mix, v_w_in, v_conv_w, v_conv_b, v_dt_bias, v_a_log, v_d_skip, v_ssm_norm, v_w_out, v_norm_ffn, v_w_gate, v_w_up, v_w_down, v_final_norm):
    chip = 2 * lax.axis_index("x") + lax.axis_index("y")
    w_sh = {"w_in": w_in, "w_out": w_out, "w_gate": w_gate, "w_up": w_up, "w_down": w_down}
    m_sh = {"w_in": m_w_in, "w_out": m_w_out, "w_gate": m_w_gate, "w_up": m_w_up, "w_down": m_w_down}
    v_sh = {"w_in": v_w_in, "w_out": v_w_out, "w_gate": v_w_gate, "w_up": v_w_up, "w_down": v_w_down}
    assert DEPTH == 2
    rest = BIG_NAMES[1:]

    w16 = {n: cast_bf16(w_sh[n].reshape(-1, w_sh[n].shape[-1]), name=f"cast_{n}").reshape(w_sh[n].shape) for n in BIG_NAMES}

    def joined(n, gathered):
        if BIG_SHARD_AXIS[n] == 0:
            full = gathered.reshape(-1, gathered.shape[-1])
        else:
            full = jnp.concatenate([gathered[j] for j in range(N_CHIPS)], axis=1)
        return w_in_columns(full) if n == "w_in" else full

    def per_chip(n, g):
        if BIG_SHARD_AXIS[n] == 0:
            return g.reshape(N_CHIPS, -1, g.shape[-1])
        return jnp.stack(jnp.split(g, N_CHIPS, axis=1))

    conv_cols = CONV_CH // N_CHIPS
    gathered_in0, conv_g = allgather_chips([w16["w_in"][0], conv_w.reshape(-1, LANE)])
    big = [(joined("w_in", gathered_in0),) + (None,) * len(rest), None]
    early = {}

    def make_rest0(gs):
        early["w_in"] = gs[len(rest)]
        return tuple(joined(n, g) for n, g in zip(rest, gs))

    plan = {
        "rest0": [w16[n][0] for n in rest] + [w16["w_in"][DEPTH - 1]],
        "make_rest0": make_rest0,
        "late": [w16[n][DEPTH - 1] for n in rest],
        "make_late": lambda gs: (joined("w_in", early["w_in"]),) + tuple(joined(n, g) for n, g in zip(rest, gs)),
        "grads_late": lambda gr: [per_chip(n, gr[n]) for n in BIG_NAMES],
        "grads_rest0": lambda gr: [per_chip(n, gr[n]) for n in rest],
    }
    conv_w_full = jnp.concatenate([conv_g[j].reshape(DEPTH, CONV_WIDTH, conv_cols) for j in range(N_CHIPS)], axis=2)
    small_all = []
    for l in range(DEPTH):
        small_all.append({
            "norm_mix": norm_mix[l].reshape(1, -1), "conv_w": conv_w_full[l], "conv_b": conv_b[l].reshape(1, -1),
            "dt_bias": lane_pad(dt_bias[l]), "a_log": lane_pad(a_log[l]), "d_skip": lane_pad(d_skip[l]),
            "ssm_norm": ssm_norm[l].reshape(1, -1), "norm_ffn": norm_ffn[l].reshape(1, -1)})

    loss_part, grad_x, grads, d_final, received = local_step(x, positions, big, small_all, final_norm, loss_target, plan=plan)

    small_parts = [jnp.stack([grads[l][n].reshape(-1) for l in range(DEPTH)]) for n in SMALL_NAMES]
    small_parts += [d_final.reshape(-1), loss_part.reshape(-1)]
    recv_in0, recv_small = exchange_grads(per_chip("w_in", grads[0]["w_in"]), pack_small(small_parts))
    recv = [dict(zip(BIG_NAMES, [recv_in0] + list(received["rest0"]))), dict(zip(BIG_NAMES, received["late"]))]
    keys = [(l, n) for l in range(DEPTH) for n in BIG_NAMES]
    mine = {(l, n): sum_slots(recv[l][n], name=f"sum_partials_{n}_l{l}") for l, n in keys}
    other = dict(zip(keys, swap_cores_list([mine[k] for k in keys])))

    g_big, d_big, m_big, v_big = {}, {}, {}, {}
    for n in BIG_NAMES:
        g_big[n], d_big[n], m_big[n], v_big[n] = adamw_layers([[mine[(l, n)], other[(l, n)]] for l in range(DEPTH)],
                                                              w_sh[n], m_sh[n], v_sh[n], name=f"adamw_{n}")

    small_sum = sum_slots(recv_small, name="sum_small")
    like = [norm_mix, conv_w_full, conv_b, dt_bias, a_log, d_skip, ssm_norm, norm_ffn, final_norm, loss_part.reshape(-1)]
    g_small = unpack_small(small_sum, like)
    loss = g_small[-1][0]
    g_small = dict(zip(SMALL_NAMES + ("final_norm",), g_small[:-1]))
    g_small["conv_w"] = lax.dynamic_slice_in_dim(g_small["conv_w"], chip * conv_cols, conv_cols, axis=2)
    w_small = {"norm_mix": norm_mix, "conv_w": conv_w, "conv_b": conv_b, "dt_bias": dt_bias, "a_log": a_log, "d_skip": d_skip,
               "ssm_norm": ssm_norm, "norm_ffn": norm_ffn, "final_norm": final_norm}
    m_small = {"norm_mix": m_norm_mix, "conv_w": m_conv_w, "conv_b": m_conv_b, "dt_bias": m_dt_bias, "a_log": m_a_log,
               "d_skip": m_d_skip, "ssm_norm": m_ssm_norm, "norm_ffn": m_norm_ffn, "final_norm": m_final_norm}
    v_small = {"norm_mix": v_norm_mix, "conv_w": v_conv_w, "conv_b": v_conv_b, "dt_bias": v_dt_bias, "a_log": v_a_log,
               "d_skip": v_d_skip, "ssm_norm": v_ssm_norm, "norm_ffn": v_norm_ffn, "final_norm": v_final_norm}
    names = SMALL_NAMES + ("final_norm",)
    order = [w_small[n] for n in names]
    res = adamw(pack_small([g_small[n] for n in names]), pack_small(order), pack_small([m_small[n] for n in names]),
                pack_small([v_small[n] for n in names]), name="adamw_small")
    g_s, d_s, m_s, v_s = (dict(zip(names, unpack_small(a, order))) for a in res)

    all_names = ("norm_mix", "w_in", "conv_w", "conv_b", "dt_bias", "a_log", "d_skip", "ssm_norm", "w_out", "norm_ffn",
                 "w_gate", "w_up", "w_down", "final_norm")
    outs = [loss, grad_x]
    for src_big, src_small in ((g_big, g_s), (d_big, d_s), (m_big, m_s), (v_big, v_s)):
        outs += [src_big[n] if n in BIG_NAMES else src_small[n] for n in all_names]
    return tuple(outs)
```

```python
import functools

import jax
import jax.numpy as jnp
from jax import lax
from jax.experimental import pallas as pl
from jax.experimental.pallas import tpu as pltpu

F32 = jnp.float32
BF16 = jnp.bfloat16
MESH = pl.DeviceIdType.MESH

D_MODEL = 1024
DEPTH = 2
HEAD_DIM = 64
N_Q_HEADS = 8
N_KV_HEADS = 2
GQA = N_Q_HEADS // N_KV_HEADS
ATTN_WIDTH = N_Q_HEADS * HEAD_DIM
ROPE_DIM = HEAD_DIM // 4
ROPE_HALF = ROPE_DIM // 2
ROPE_THETA = 500000.0
DILATIONS = (1, 4, 16)
ATTN_BLOCK = 128
SSM_P = 64
SSM_HEADS = 16
SSM_INNER = SSM_HEADS * SSM_P
SSM_GROUPS = 2
HEADS_PER_GROUP = SSM_HEADS // SSM_GROUPS
D_STATE = 128
CONV_WIDTH = 4
CHUNK = 128
CONV_CH = SSM_INNER + 2 * SSM_GROUPS * D_STATE
MIX_WIDTH = ATTN_WIDTH + SSM_INNER
Q_END = ATTN_WIDTH
K_END = Q_END + N_KV_HEADS * HEAD_DIM
V_END = K_END + N_KV_HEADS * HEAD_DIM
Z_END = V_END + SSM_INNER
XBC_END = Z_END + CONV_CH
IN_PROJ = XBC_END + SSM_HEADS
LANE = 128
IN_PAD = XBC_END + LANE
Q_COL, Z_COL, XBC_COL, K_COL, V_COL, DT_COL = 0, 512, 1536, 3072, 3200, 3328
EPS = 1e-5
ADAM_LR, ADAM_B1, ADAM_B2, ADAM_EPS, ADAM_WD, ADAM_STEP = 0.001, 0.9, 0.999, 1e-8, 0.01, 10
N_CHIPS = 4
N_DEV = 8
VMEM_LIMIT = 48 * 1024 * 1024
NEG_BIG = -1e30


def _params(sem=None):
    return pltpu.CompilerParams(dimension_semantics=sem, vmem_limit_bytes=VMEM_LIMIT)


def _pick(n, prefs):
    for p in prefs:
        if n % p == 0:
            return p
    return n


def matmul(a, b, *, name, ta=False, tb=False, out_dtype=F32, residual=None):
    if ta:
        assert not tb and residual is None
        return _matmul_over_rows(a, b, name=name, out_dtype=out_dtype)
    return _matmul_full_k(a, b, name=name, tb=tb, out_dtype=out_dtype, residual=residual)


def _matmul_full_k(a, b, *, name, tb, out_dtype, residual):
    a_parts = list(a) if isinstance(a, (list, tuple)) else [a]
    n_a = len(a_parts)
    m = a_parts[0].shape[0]
    kdim = sum(p.shape[1] for p in a_parts)
    n = b.shape[0] if tb else b.shape[1]
    tm = _pick(m, (1024, 512, 256))
    tn = _pick(n, (1152, 1408, 1536, 1024, 768, 512, 384, 256, 128))
    b_spec = pl.BlockSpec((tn, kdim), lambda i, j: (j, 0)) if tb else pl.BlockSpec((kdim, tn), lambda i, j: (0, j))
    o_spec = pl.BlockSpec((tm, tn), lambda i, j: (i, j))
    dims = (((1,), (1 if tb else 0,)), ((), ()))
    has_res = residual is not None

    def body(*refs):
        b_ref, o_ref = refs[n_a], refs[-1]
        pieces = [r[...].astype(BF16) for r in refs[:n_a]]
        av = pieces[0] if n_a == 1 else jnp.concatenate(pieces, axis=1)
        r = lax.dot_general(av, b_ref[...].astype(BF16), dims, preferred_element_type=F32)
        if has_res:
            r = r + refs[n_a + 1][...]
        o_ref[...] = r.astype(out_dtype)

    in_specs = ([pl.BlockSpec((tm, p.shape[1]), lambda i, j: (i, 0)) for p in a_parts] + [b_spec]
                + ([o_spec] if has_res else []))
    args = tuple(a_parts) + (b,) + ((residual,) if has_res else ())
    return pl.pallas_call(
        body, name=name, grid=(m // tm, n // tn), in_specs=in_specs, out_specs=o_spec,
        out_shape=jax.ShapeDtypeStruct((m, n), out_dtype),
        compiler_params=_params(("parallel", "parallel")),
    )(*args)


def _matmul_over_rows(a, b, *, name, out_dtype):
    t, m = a.shape
    n = b.shape[1]
    tm = _pick(m, (1024, 1408, 768, 512, 256, 128))
    tn = _pick(n, (1152, 1408, 1024, 768, 512, 384, 256, 128))
    tk = _pick(t, (2048, 1024, 512, 256, 128))
    nk = t // tk

    def body(a_ref, b_ref, o_ref, acc):
        k = pl.program_id(2)
        part = lax.dot_general(a_ref[...].astype(BF16), b_ref[...].astype(BF16), (((0,), (0,)), ((), ())),
                               preferred_element_type=F32)

        @pl.when(k == 0)
        def _():
            acc[...] = part

        @pl.when(k > 0)
        def _():
            acc[...] += part

        @pl.when(k == nk - 1)
        def _():
            o_ref[...] = acc[...].astype(out_dtype)

    return pl.pallas_call(
        body, name=name, grid=(m // tm, n // tn, nk),
        in_specs=[pl.BlockSpec((tk, tm), lambda i, j, k: (k, i)), pl.BlockSpec((tk, tn), lambda i, j, k: (k, j))],
        out_specs=pl.BlockSpec((tm, tn), lambda i, j, k: (i, j)),
        out_shape=jax.ShapeDtypeStruct((m, n), out_dtype),
        scratch_shapes=[pltpu.VMEM((tm, tn), F32)],
        compiler_params=_params(("parallel", "parallel", "arbitrary")),
    )(a, b)


ROW_BLOCK_BYTES = 32 * 1024 * 1024


def _row_tile(t, tr, widths, n_copies):
    lanes = sum(-(-wd // LANE) * LANE for wd in widths) * n_copies
    tr = min(tr, t)
    while tr > 8 and tr * lanes * 4 > ROW_BLOCK_BYTES:
        tr //= 2
    return tr


def _row_widths(rows, groups, windows):
    windows = windows or [None] * len(rows)
    widths = [(w[1] if w else a.shape[1]) // groups for a, w in zip(rows, windows)]
    assert all(w is None or w[0] % wd == 0 for w, wd in zip(windows, widths))
    return widths, [(w[0] // wd if w else 0) for w, wd in zip(windows, widths)]


def _row_specs(tr, widths, offs):
    return [pl.BlockSpec((tr, wd), functools.partial(lambda g, i, off: (i, g + off), off=off)) for wd, off in zip(widths, offs)]


def rowwise_fwd(fn, rows, params, out_dtypes, *, name, tr=512, groups=1, windows=None):
    t = rows[0].shape[0]
    widths, offs = _row_widths(rows, groups, windows)
    tr = _row_tile(t, tr, widths, 2)
    row_specs = _row_specs(tr, widths, offs)
    par_spec = lambda p: pl.BlockSpec((1, p.shape[1] // groups), lambda g, i: (0, g))
    n_in = len(rows) + len(params)
    out_cols = [o.shape[1] for o in jax.eval_shape(
        fn, *[jax.ShapeDtypeStruct((tr, wd), F32) for wd in widths],
        *[jax.ShapeDtypeStruct((1, p.shape[1] // groups), F32) for p in params])]

    def body(*refs):
        vals = [r[...].astype(F32) for r in refs[:n_in]]
        outs = fn(*vals)
        for o_ref, o in zip(refs[n_in:], outs):
            o_ref[...] = o.astype(o_ref.dtype)

    return pl.pallas_call(
        body, name=name, grid=(groups, t // tr),
        in_specs=row_specs + [par_spec(p) for p in params],
        out_specs=[pl.BlockSpec((tr, c), lambda g, i: (i, g)) for c in out_cols],
        out_shape=[jax.ShapeDtypeStruct((t, c * groups), d) for c, d in zip(out_cols, out_dtypes)],
        compiler_params=_params(("arbitrary", "arbitrary")),
    )(*rows, *params)


def rowwise_bwd(fn, rows, params, cts, drow_dtypes, *, name, tr=512, groups=1, add_to_first=None, windows=None,
                ct_windows=None):
    t = rows[0].shape[0]
    widths, offs = _row_widths(rows, groups, windows)
    ct_widths, ct_offs = _row_widths(cts, groups, ct_windows)
    tr = _row_tile(t, tr, widths + ct_widths, 2)
    row_spec = lambda a: pl.BlockSpec((tr, a.shape[1] // groups), lambda g, i: (i, g))
    row_specs = _row_specs(tr, widths, offs)
    par_spec = lambda p: pl.BlockSpec((1, p.shape[1] // groups), lambda g, i: (0, g))
    n_rows, n_par, n_ct = len(rows), len(params), len(cts)
    has_add = add_to_first is not None
    n_in = n_rows + n_par + n_ct + (1 if has_add else 0)

    def body(*refs):
        i = pl.program_id(1)
        vals = [r[...].astype(F32) for r in refs[:n_rows + n_par]]
        ct_vals = tuple(r[...].astype(F32) for r in refs[n_rows + n_par:n_rows + n_par + n_ct])
        _, vjp = jax.vjp(fn, *vals)
        grads = vjp(ct_vals)
        out_refs = refs[n_in:]
        for idx in range(n_rows):
            g = grads[idx]
            if idx == 0 and has_add:
                g = g + refs[n_in - 1][...]
            out_refs[idx][...] = g.astype(out_refs[idx].dtype)
        for idx in range(n_par):
            p_ref = out_refs[n_rows + idx]

            @pl.when(i == 0)
            def _():
                p_ref[...] = jnp.zeros_like(p_ref)

            p_ref[...] += grads[n_rows + idx]

    ins = list(rows) + list(params) + list(cts) + ([add_to_first] if has_add else [])
    in_specs = (row_specs + [par_spec(p) for p in params] + _row_specs(tr, ct_widths, ct_offs)
                + ([row_spec(add_to_first)] if has_add else []))
    return pl.pallas_call(
        body, name=name, grid=(groups, t // tr), in_specs=in_specs,
        out_specs=[pl.BlockSpec((tr, wd), lambda g, i: (i, g)) for wd in widths] + [par_spec(p) for p in params],
        out_shape=[jax.ShapeDtypeStruct((t, wd * groups), d) for wd, d in zip(widths, drow_dtypes)]
        + [jax.ShapeDtypeStruct(p.shape, F32) for p in params],
        compiler_params=_params(("arbitrary", "arbitrary")),
    )(*ins)


def rms_fn(x, w):
    return (x * lax.rsqrt(jnp.mean(x * x, axis=-1, keepdims=True) + EPS) * w,)


def swiglu_fn(g, u):
    return (g * jax.nn.sigmoid(g) * u,)


def gated_norm_fn(y, z, w):
    v = y * (z * jax.nn.sigmoid(z))
    return (v * lax.rsqrt(jnp.mean(v * v, axis=-1, keepdims=True) + EPS) * w,)


def loss_and_grad(h, target, w, *, tr=512):
    t, d = h.shape

    def loss_fn(hv, wv, tv):
        err = rms_fn(hv, wv)[0] - tv
        per_row = jnp.mean(err * err, axis=-1, keepdims=True)
        return 0.5 * jnp.sum(per_row, axis=0, keepdims=True)

    def body(h_ref, t_ref, w_ref, dh_ref, dw_ref, loss_ref):
        i = pl.program_id(0)

        @pl.when(i == 0)
        def _():
            dw_ref[...] = jnp.zeros_like(dw_ref)
            loss_ref[...] = jnp.zeros_like(loss_ref)

        tv = t_ref[...]
        val, vjp = jax.vjp(lambda hv, wv: loss_fn(hv, wv, tv), h_ref[...], w_ref[...])
        dh, dw = vjp(jnp.ones((1, 1), F32))
        dh_ref[...] = dh
        dw_ref[...] += dw
        loss_ref[...] += jnp.broadcast_to(val, loss_ref.shape)

    row = pl.BlockSpec((tr, d), lambda i: (i, 0))
    par = pl.BlockSpec((1, d), lambda i: (0, 0))
    return pl.pallas_call(
        body, name="loss_and_grad", grid=(t // tr,), in_specs=[row, row, par],
        out_specs=[row, par, pl.BlockSpec((1, LANE), lambda i: (0, 0))],
        out_shape=[jax.ShapeDtypeStruct((t, d), F32), jax.ShapeDtypeStruct((1, d), F32),
                   jax.ShapeDtypeStruct((1, LANE), F32)],
        compiler_params=_params(("arbitrary",)),
    )(h, target, w)


def _split3(x):
    hi = x.astype(BF16)
    r1 = x - hi.astype(F32)
    mid = r1.astype(BF16)
    lo = (r1 - mid.astype(F32)).astype(BF16)
    return hi, mid, lo


def _dot01_left(m01, x):
    return sum(jnp.dot(m01, p, preferred_element_type=F32) for p in _split3(x))


def _dot01_right(x, m01):
    return sum(jnp.dot(p, m01, preferred_element_type=F32) for p in _split3(x))


ATTN_PAD = ATTN_BLOCK * DILATIONS[-1]
Q_GROUP_W = GQA * HEAD_DIM
ATTN_VMEM_LIMIT = 56 * 1024 * 1024
HALF_W = 2 * HEAD_DIM
N_HALF = Q_GROUP_W // HALF_W
_ATTN_BIAS_BUF = pltpu.VMEM((2, GQA * ATTN_BLOCK, 2 * ATTN_BLOCK), F32)


def _attn_mask(n):
    rows = GQA * ATTN_BLOCK
    qi = lax.broadcasted_iota(jnp.int32, (rows, 2 * ATTN_BLOCK), 0) % ATTN_BLOCK
    ki = lax.broadcasted_iota(jnp.int32, (rows, 2 * ATTN_BLOCK), 1)
    delta = qi + ATTN_BLOCK - ki
    return (delta >= 0) & (delta <= ATTN_BLOCK) & ((n - 1) * ATTN_BLOCK + ki >= 0)


def _attn_bias(bias_s):
    for first in (0, 1):
        bias_s[first] = jnp.where(_attn_mask(first), 0.0, NEG_BIG)


def _rope(x, cos_v, sin_v, swap, scale, adjoint):
    if adjoint:
        return (x * cos_v + _dot01_right(x * sin_v, swap)) * scale
    return (x * cos_v + _dot01_right(x, swap) * sin_v) * scale


def _swap_matrix():
    c = HEAD_DIM
    ci = lax.broadcasted_iota(jnp.int32, (c, c), 0)
    cj = lax.broadcasted_iota(jnp.int32, (c, c), 1)
    swap = ((cj == ci + ROPE_HALF) & (ci < ROPE_HALF)) | ((cj == ci - ROPE_HALF) & (ci >= ROPE_HALF) & (ci < ROPE_DIM))
    return swap.astype(BF16)


def _attn_blocks(s_len):
    out = []
    for i, d in enumerate(DILATIONS):
        nb = s_len // (ATTN_BLOCK * d)
        for r in range(d):
            for n in range(nb):
                start = r + d * ATTN_BLOCK * n
                out.append((i, d, start, ATTN_PAD + start - d * ATTN_BLOCK, n))
    return out


def _rows(start, size, d):
    return pl.ds(start, size, stride=d) if d > 1 else pl.ds(start, size)


def _attn_prologue(q_refs, kv_ref, tab_ref, q_s, kv_s, hk, s_len):
    swap = _swap_matrix()
    cos_v, sin_v = tab_ref[0, :, :HEAD_DIM], tab_ref[0, :, HEAD_DIM:]
    for j in range(N_HALF):
        for e in range(2):
            cols = slice(e * HEAD_DIM, (e + 1) * HEAD_DIM)
            q_s[j][:, cols] = _rope(q_refs[j][0, :, cols], cos_v, sin_v, swap, HEAD_DIM ** -0.5, False)
    kv_s[0:ATTN_PAD, :] = jnp.zeros((ATTN_PAD, HALF_W), F32)
    for h in range(N_KV_HEADS):
        @pl.when(hk == h)
        def _():
            kv_s[ATTN_PAD:ATTN_PAD + s_len, :HEAD_DIM] = _rope(kv_ref[0, :, h * HEAD_DIM:(h + 1) * HEAD_DIM], cos_v, sin_v,
                                                               swap, 1.0, False)
            kv_s[ATTN_PAD:ATTN_PAD + s_len, HEAD_DIM:] = kv_ref[0, :, LANE + h * HEAD_DIM:LANE + (h + 1) * HEAD_DIM]


def _stack_heads(halves):
    return jnp.concatenate([h[:, e * HEAD_DIM:(e + 1) * HEAD_DIM] for h in halves for e in range(2)], axis=0)


def _unstack_heads(x, j):
    return jnp.concatenate([x[(2 * j + e) * ATTN_BLOCK:(2 * j + e + 1) * ATTN_BLOCK] for e in range(2)], axis=1)


def _stack_stats(halves):
    return jnp.concatenate([jnp.max(h[:, e * HEAD_DIM:(e + 1) * HEAD_DIM], axis=1, keepdims=True)
                            for h in halves for e in range(2)], axis=0)


def _attn_in_specs(s_len):
    assert K_COL % (2 * LANE) == 0 and V_COL == K_COL + LANE

    def halves(first_tile):
        return [pl.BlockSpec((1, s_len, HALF_W), functools.partial(lambda b, h, j: (b, 0, first_tile + N_HALF * h + j), j=j))
                for j in range(N_HALF)]

    kv_spec = pl.BlockSpec((1, s_len, 2 * LANE), lambda b, h: (b, 0, K_COL // (2 * LANE)))
    t_spec = pl.BlockSpec((1, s_len, 2 * HEAD_DIM), lambda b, h: (b, 0, 0))
    o_spec = pl.BlockSpec((1, s_len, Q_GROUP_W), lambda b, h: (b, 0, h))
    return halves(Q_COL // HALF_W), kv_spec, t_spec, o_spec, halves(0)


class SideCopy:
    def __init__(self, side, *, n_in, n_out, grid):
        self.side, self.n_in, self.n_out, self.grid = side, n_in, n_out, grid
        hbm = pl.BlockSpec(memory_space=pltpu.HBM)
        if side is None:
            self.in_specs, self.out_specs, self.out_shape, self.scratch, self.args = [], [], [], [], []
            return
        srcs, per_dest = side
        n = len(srcs)
        self.in_specs, self.out_specs, self.args = [hbm] * n, [hbm] * n, list(srcs)
        self.out_shape = [jax.ShapeDtypeStruct(s.shape if per_dest else (N_CHIPS,) + s.shape, s.dtype) for s in srcs]
        self.scratch = [pltpu.SemaphoreType.DMA(((N_CHIPS - 1) * n,)), pltpu.SemaphoreType.DMA(((N_CHIPS - 1) * n,)),
                        pltpu.SemaphoreType.DMA((n,))]

    def wrap(self, body):
        if self.side is None:
            return body
        n_in, n_out, grid, per_dest, n = self.n_in, self.n_out, self.grid, self.side[1], len(self.side[0])

        def wrapped(*refs):
            ins, srcs = refs[:n_in], refs[n_in:n_in + n]
            outs, dsts = refs[n_in + n:n_in + n + n_out], refs[n_in + n + n_out:n_in + 2 * n + n_out]
            scratch, sems = refs[n_in + 2 * n + n_out:-3], refs[-3:]
            ids = [pl.program_id(a) for a in range(len(grid))]
            first = functools.reduce(lambda p, q: p & q, [i == 0 for i in ids])
            last = functools.reduce(lambda p, q: p & q, [i == g - 1 for i, g in zip(ids, grid)])

            @pl.when(first)
            def _():
                for a in range(n):
                    local, sends, _ = _chip_copies(srcs[a], dsts[a], *sems, per_dest, a)
                    local.start()
                    for cp in sends:
                        cp.start()

            body(*ins, *outs, *scratch)

            @pl.when(last)
            def _():
                for a in range(n):
                    local, sends, recvs = _chip_copies(srcs[a], dsts[a], *sems, per_dest, a)
                    for cp in recvs:
                        cp.wait_recv()
                    for cp in sends:
                        cp.wait_send()
                    local.wait()

        return wrapped


def _chip_copies(src_ref, dst_ref, send_sems, recv_sems, local_sems, per_dest, a=0):
    x, y, c = lax.axis_index("x"), lax.axis_index("y"), lax.axis_index("c")
    chip = 2 * x + y
    own = src_ref.at[chip] if per_dest else src_ref
    local = pltpu.make_async_copy(own, dst_ref.at[chip], local_sems.at[a])
    sends, recvs = [], []
    for k, (px, py) in enumerate([(1 - x, y), (x, 1 - y), (1 - x, 1 - y)]):
        k = (N_CHIPS - 1) * a + k
        peer = dict(send_sem=send_sems.at[k], recv_sem=recv_sems.at[k], device_id=(px, py, c), device_id_type=MESH)
        sends.append(pltpu.make_async_remote_copy(src_ref=src_ref.at[2 * px + py] if per_dest else src_ref,
                                                  dst_ref=dst_ref.at[chip], **peer))
        recvs.append(pltpu.make_async_remote_copy(src_ref=own, dst_ref=dst_ref.at[2 * px + py], **peer))
    return local, sends, recvs


def attn_fwd(proj3, rope_tab, *, name, side=None):
    b, s_len, _ = proj3.shape
    q_specs, kv_spec, t_spec, o_spec, _ = _attn_in_specs(s_len)
    n_br = len(DILATIONS)

    def body(*refs):
        q_refs, (kv_ref, tab_ref, o_ref, lse_ref) = refs[:N_HALF], refs[N_HALF:N_HALF + 4]
        scratch = refs[N_HALF + 4:]
        q_s, kv_s = scratch[:N_HALF], scratch[N_HALF]
        o_s = [scratch[N_HALF + 1 + i * N_HALF:N_HALF + 1 + (i + 1) * N_HALF] for i in range(n_br)]
        l_s = [scratch[N_HALF + 1 + (n_br + i) * N_HALF:N_HALF + 1 + (n_br + i + 1) * N_HALF] for i in range(n_br)]
        bias_s = scratch[-1]
        _attn_prologue(q_refs, kv_ref, tab_ref, q_s, kv_s, pl.program_id(1), s_len)
        _attn_bias(bias_s)
        for i, d, q0, k0, n in _attn_blocks(s_len):
            qrows = _rows(q0, ATTN_BLOCK, d)
            qv = _stack_heads([q_s[j][qrows, :] for j in range(N_HALF)]).astype(BF16)
            kvb = kv_s[_rows(k0, 2 * ATTN_BLOCK, d), :].astype(BF16)
            kk, vv = kvb[:, :HEAD_DIM], kvb[:, HEAD_DIM:]
            sc = lax.dot_general(qv, kk, (((1,), (1,)), ((), ())), preferred_element_type=F32)
            sc = sc + bias_s[min(n, 1)]
            m = jnp.max(sc, axis=-1, keepdims=True)
            pr = jnp.exp(sc - m)
            den = jnp.sum(pr, axis=-1, keepdims=True)
            o = jnp.dot(pr.astype(BF16), vv, preferred_element_type=F32) / den
            lse_b = jnp.broadcast_to(m + jnp.log(den), (GQA * ATTN_BLOCK, HEAD_DIM))
            for j in range(N_HALF):
                o_s[i][j][qrows, :] = _unstack_heads(o, j)
                l_s[i][j][qrows, :] = _unstack_heads(lse_b, j)
        step = 256
        for t0 in range(0, s_len, step):
            rs = pl.ds(t0, step)
            for j in range(N_HALF):
                ls = [l_s[i][j][rs, :] for i in range(n_br)]
                m = functools.reduce(jnp.maximum, ls)
                es = [jnp.exp(l - m) for l in ls]
                tot = functools.reduce(lambda a, c: a + c, es)
                inv = 1.0 / tot
                acc = None
                for i in range(n_br):
                    term = (es[i] * inv) * o_s[i][j][rs, :]
                    acc = term if acc is None else acc + term
                o_ref[0, rs, j * HALF_W:(j + 1) * HALF_W] = acc
                lse_ref[0, rs, j * HALF_W:(j + 1) * HALF_W] = m + jnp.log(tot)

    half_buf = pltpu.VMEM((s_len, HALF_W), F32)
    call = SideCopy(side, n_in=N_HALF + 2, n_out=2, grid=(b, N_KV_HEADS))
    return pl.pallas_call(
        call.wrap(body), name=name, grid=(b, N_KV_HEADS), in_specs=q_specs + [kv_spec, t_spec] + call.in_specs,
        out_specs=[o_spec, o_spec] + call.out_specs,
        out_shape=[jax.ShapeDtypeStruct((b, s_len, ATTN_WIDTH), F32)] * 2 + call.out_shape,
        scratch_shapes=[half_buf] * N_HALF + [pltpu.VMEM((ATTN_PAD + s_len, HALF_W), F32)] + [half_buf] * (2 * n_br * N_HALF)
        + [_ATTN_BIAS_BUF] + call.scratch,
        compiler_params=pltpu.CompilerParams(dimension_semantics=("arbitrary", "arbitrary"), vmem_limit_bytes=ATTN_VMEM_LIMIT),
    )(*([proj3] * (N_HALF + 1)), rope_tab, *call.args)


def attn_bwd(proj3, rope_tab, attn3, lse3, d_attn3, *, name, side=None):
    b, s_len, _ = proj3.shape
    q_specs, kv_spec, t_spec, o_spec, half_specs = _attn_in_specs(s_len)

    def body(*refs):
        q_refs = refs[:N_HALF]
        kv_ref, tab_ref, o_ref = refs[N_HALF:N_HALF + 3]
        lse_refs = refs[N_HALF + 3:2 * N_HALF + 3]
        do_refs = refs[2 * N_HALF + 3:3 * N_HALF + 3]
        dq_ref, dkv_ref = refs[3 * N_HALF + 3:3 * N_HALF + 5]
        scratch = refs[3 * N_HALF + 5:]
        q_s, kv_s = scratch[:N_HALF], scratch[N_HALF]
        dl_s = scratch[N_HALF + 1:2 * N_HALF + 1]
        dq_s = scratch[2 * N_HALF + 1:3 * N_HALF + 1]
        dkv_s = scratch[3 * N_HALF + 1]
        bias_s = scratch[-1]
        _attn_prologue(q_refs, kv_ref, tab_ref, q_s, kv_s, pl.program_id(1), s_len)
        _attn_bias(bias_s)
        dkv_s[...] = jnp.zeros_like(dkv_s)
        for j in range(N_HALF):
            dq_s[j][...] = jnp.zeros_like(dq_s[j])
            for e in range(2):
                cols = slice(e * HEAD_DIM, (e + 1) * HEAD_DIM)
                ocols = slice(j * HALF_W + e * HEAD_DIM, j * HALF_W + (e + 1) * HEAD_DIM)
                delta = jnp.sum(do_refs[j][0, :, cols] * o_ref[0, :, ocols], axis=1, keepdims=True)
                dl_s[j][:, cols] = jnp.broadcast_to(delta, (s_len, HEAD_DIM))
        for i, d, q0, k0, n in _attn_blocks(s_len):
            qrows, krows = _rows(q0, ATTN_BLOCK, d), _rows(k0, 2 * ATTN_BLOCK, d)
            qv = _stack_heads([q_s[j][qrows, :] for j in range(N_HALF)]).astype(BF16)
            kvb = kv_s[krows, :].astype(BF16)
            kk, vv = kvb[:, :HEAD_DIM], kvb[:, HEAD_DIM:]
            do16 = _stack_heads([do_refs[j].at[0][qrows, :] for j in range(N_HALF)]).astype(BF16)
            lse = _stack_stats([lse_refs[j].at[0][qrows, :] for j in range(N_HALF)])
            delta = _stack_stats([dl_s[j][qrows, :] for j in range(N_HALF)])
            sc = lax.dot_general(qv, kk, (((1,), (1,)), ((), ())), preferred_element_type=F32)
            pr = jnp.exp(sc + bias_s[min(n, 1)] - lse)
            dv = lax.dot_general(pr.astype(BF16), do16, (((0,), (0,)), ((), ())), preferred_element_type=F32)
            dp = lax.dot_general(do16, vv, (((1,), (1,)), ((), ())), preferred_element_type=F32)
            ds = (pr * (dp - delta)).astype(BF16)
            dq = jnp.dot(ds, kk, preferred_element_type=F32)
            dk = lax.dot_general(ds, qv, (((0,), (0,)), ((), ())), preferred_element_type=F32)
            for j in range(N_HALF):
                dq_s[j][qrows, :] += _unstack_heads(dq, j)
            dkv_s[krows, :] += jnp.concatenate([dk, dv], axis=1)
        swap = _swap_matrix()
        cos_v, sin_v = tab_ref[0, :, :HEAD_DIM], tab_ref[0, :, HEAD_DIM:]
        for j in range(N_HALF):
            for e in range(2):
                cols = slice(e * HEAD_DIM, (e + 1) * HEAD_DIM)
                ocols = slice(j * HALF_W + e * HEAD_DIM, j * HALF_W + (e + 1) * HEAD_DIM)
                dq_ref[0, :, ocols] = _rope(dq_s[j][:, cols], cos_v, sin_v, swap, HEAD_DIM ** -0.5, True).astype(dq_ref.dtype)
        d_k = _rope(dkv_s[ATTN_PAD:ATTN_PAD + s_len, :HEAD_DIM], cos_v, sin_v, swap, 1.0, True)
        d_v = dkv_s[ATTN_PAD:ATTN_PAD + s_len, HEAD_DIM:]
        for h in range(N_KV_HEADS):
            @pl.when(pl.program_id(1) == h)
            def _():
                dkv_ref[0, :, h * HEAD_DIM:(h + 1) * HEAD_DIM] = d_k.astype(dkv_ref.dtype)
                dkv_ref[0, :, LANE + h * HEAD_DIM:LANE + (h + 1) * HEAD_DIM] = d_v.astype(dkv_ref.dtype)

    kv_out = pl.BlockSpec((1, s_len, 2 * LANE), lambda bi, h: (bi, 0, 0))
    kv_shape = jax.ShapeDtypeStruct((b, s_len, 2 * LANE), BF16)
    half_buf = pltpu.VMEM((s_len, HALF_W), F32)
    pad_buf = pltpu.VMEM((ATTN_PAD + s_len, HALF_W), F32)
    call = SideCopy(side, n_in=3 * N_HALF + 3, n_out=2, grid=(b, N_KV_HEADS))
    return pl.pallas_call(
        call.wrap(body), name=name, grid=(b, N_KV_HEADS),
        in_specs=q_specs + [kv_spec, t_spec, o_spec] + half_specs + half_specs + call.in_specs,
        out_specs=[o_spec, kv_out] + call.out_specs,
        out_shape=[jax.ShapeDtypeStruct((b, s_len, ATTN_WIDTH), BF16), kv_shape] + call.out_shape,
        scratch_shapes=[half_buf] * N_HALF + [pad_buf] + [half_buf] * (2 * N_HALF) + [pad_buf, _ATTN_BIAS_BUF] + call.scratch,
        compiler_params=pltpu.CompilerParams(dimension_semantics=("arbitrary", "arbitrary"), vmem_limit_bytes=ATTN_VMEM_LIMIT),
    )(*([proj3] * (N_HALF + 1)), rope_tab, attn3, *([lse3] * N_HALF), *([d_attn3] * N_HALF), *call.args)


CONV_TC = 256
CONV_COL0 = XBC_COL // CONV_TC


def _shift_down(u, s):
    if s == 0:
        return u
    rows = lax.broadcasted_iota(jnp.int32, u.shape, 0)
    return jnp.where(rows >= s, pltpu.roll(u, s, 0), 0.0)


def _shift_up(u, s):
    if s == 0:
        return u
    n = u.shape[0]
    rows = lax.broadcasted_iota(jnp.int32, u.shape, 0)
    return jnp.where(rows < n - s, pltpu.roll(u, n - s, 0), 0.0)


def conv_silu_fwd(proj3, w, bias, *, name):
    b, s, _ = proj3.shape
    u_spec = pl.BlockSpec((1, s, CONV_TC), lambda j, bi: (bi, 0, CONV_COL0 + j))
    o_spec = pl.BlockSpec((1, s, CONV_TC), lambda j, bi: (bi, 0, j))
    w_spec = pl.BlockSpec((CONV_WIDTH, CONV_TC), lambda j, bi: (0, j))
    b_spec = pl.BlockSpec((1, CONV_TC), lambda j, bi: (0, j))

    def body(u_ref, w_ref, b_ref, o_ref):
        u = u_ref[0]
        y = jnp.broadcast_to(b_ref[...], u.shape)
        for k in range(CONV_WIDTH):
            y = y + w_ref[k:k + 1, :] * _shift_down(u, CONV_WIDTH - 1 - k)
        o_ref[0] = y * jax.nn.sigmoid(y)

    return pl.pallas_call(
        body, name=name, grid=(CONV_CH // CONV_TC, b), in_specs=[u_spec, w_spec, b_spec], out_specs=o_spec,
        out_shape=jax.ShapeDtypeStruct((b, s, CONV_CH), F32),
        compiler_params=_params(("parallel", "arbitrary")),
    )(proj3, w, bias)


def conv_silu_bwd(proj3, w, bias, dact, *, name):
    b, s, _ = proj3.shape
    u_spec = pl.BlockSpec((1, s, CONV_TC), lambda j, bi: (bi, 0, CONV_COL0 + j))
    o_spec = pl.BlockSpec((1, s, CONV_TC), lambda j, bi: (bi, 0, j))
    w_spec = pl.BlockSpec((CONV_WIDTH, CONV_TC), lambda j, bi: (0, j))
    b_spec = pl.BlockSpec((1, CONV_TC), lambda j, bi: (0, j))

    def body(u_ref, w_ref, b_ref, g_ref, du_ref, dw_ref, db_ref):
        bi = pl.program_id(1)

        @pl.when(bi == 0)
        def _():
            dw_ref[...] = jnp.zeros_like(dw_ref)
            db_ref[...] = jnp.zeros_like(db_ref)

        u = u_ref[0]
        y = jnp.broadcast_to(b_ref[...], u.shape)
        shifted = [_shift_down(u, CONV_WIDTH - 1 - k) for k in range(CONV_WIDTH)]
        for k in range(CONV_WIDTH):
            y = y + w_ref[k:k + 1, :] * shifted[k]
        sig = jax.nn.sigmoid(y)
        dy = g_ref[0] * (sig * (1.0 + y * (1.0 - sig)))
        du = jnp.zeros_like(u)
        for k in range(CONV_WIDTH):
            du = du + w_ref[k:k + 1, :] * _shift_up(dy, CONV_WIDTH - 1 - k)
            dw_ref[k:k + 1, :] += jnp.sum(dy * shifted[k], axis=0, keepdims=True)
        du_ref[0] = du.astype(du_ref.dtype)
        db_ref[...] += jnp.sum(dy, axis=0, keepdims=True)

    return pl.pallas_call(
        body, name=name, grid=(CONV_CH // CONV_TC, b), in_specs=[u_spec, w_spec, b_spec, o_spec],
        out_specs=[o_spec, w_spec, b_spec],
        out_shape=[jax.ShapeDtypeStruct((b, s, CONV_CH), BF16), jax.ShapeDtypeStruct((CONV_WIDTH, CONV_CH), F32),
                   jax.ShapeDtypeStruct((1, CONV_CH), F32)],
        compiler_params=_params(("parallel", "arbitrary")),
    )(proj3, w, bias, dact)


SSD_INTERLEAVE = 8
SSD_INTERLEAVE_FWD = 1


def _softplus(z):
    e = jnp.exp(-jnp.abs(z))
    u = 1.0 + e
    log1p = jnp.where(u == 1.0, e, jnp.log(u) * e / jnp.where(u == 1.0, 1.0, u - 1.0))
    return jnp.maximum(z, 0.0) + log1p


def _tri(lower):
    r = lax.broadcasted_iota(jnp.int32, (CHUNK, CHUNK), 0)
    c = lax.broadcasted_iota(jnp.int32, (CHUNK, CHUNK), 1)
    return (r >= c) if lower else (r <= c)


def _ssd_common(dtr_ref, dtb_ref, alog_ref):
    z = dtr_ref[0] + dtb_ref[...]
    dt = _softplus(z)
    aneg = -jnp.exp(alog_ref[...])
    acs = _dot01_left(_tri(True).astype(BF16), dt * aneg)
    return z, dt, aneg, acs


def _ssd_specs(nc, reverse):
    cidx = (lambda c: nc - 1 - c) if reverse else (lambda c: c)
    act_spec = pl.BlockSpec((1, CHUNK, CONV_CH), lambda b, c: (b, cidx(c), 0))
    y_spec = pl.BlockSpec((1, CHUNK, SSM_INNER), lambda b, c: (b, cidx(c), 0))
    dt_in_spec = pl.BlockSpec((1, CHUNK, LANE), lambda b, c: (b, cidx(c), DT_COL // LANE))
    dt_out_spec = pl.BlockSpec((1, CHUNK, LANE), lambda b, c: (b, cidx(c), 0))
    par_spec = pl.BlockSpec((1, LANE), lambda b, c: (0, 0))
    h_spec = pl.BlockSpec((1, SSM_HEADS, 1, SSM_P, D_STATE), lambda b, c: (b, 0, cidx(c), 0, 0))
    return act_spec, y_spec, dt_in_spec, dt_out_spec, par_spec, h_spec


def _head_cols(h):
    return slice(h * SSM_P, (h + 1) * SSM_P)


def _group_cols(g, which):
    start = SSM_INNER + which * SSM_GROUPS * D_STATE + g * D_STATE
    return slice(start, start + D_STATE)


def _each(f, *lists):
    return [f(*a) for a in zip(*lists)]


def _nt(a, b):
    return lax.dot_general(a, b, (((1,), (1,)), ((), ())), preferred_element_type=F32)


def _tn(a, b):
    return lax.dot_general(a, b, (((0,), (0,)), ((), ())), preferred_element_type=F32)


def _nn(a, b):
    return jnp.dot(a, b, preferred_element_type=F32)


def _rowsum(a):
    return jnp.sum(a, axis=1, keepdims=True)


def _colsum(a):
    return jnp.sum(a, axis=0, keepdims=True)


def _bf(a):
    return a.astype(BF16)


def _head_batches(g, width=SSD_INTERLEAVE):
    first = g * HEADS_PER_GROUP
    return [list(range(first + k, first + k + width)) for k in range(0, HEADS_PER_GROUP, width)]


def _decay_matrix(acs_j, acs_row, tri_mask):
    dm = jnp.broadcast_to(acs_j, (CHUNK, CHUNK)) - jnp.broadcast_to(acs_row, (CHUNK, CHUNK))
    return jnp.where(tri_mask, jnp.exp(jnp.where(tri_mask, dm, 0.0)), 0.0)


def ssd_fwd(act3, proj3, dtb, alog, dsk, *, name, side=None):
    b, s, _ = act3.shape
    nc = s // CHUNK
    act_spec, y_spec, dt_in_spec, _, par_spec, h_spec = _ssd_specs(nc, False)

    def body(act_ref, dtr_ref, dtb_ref, alog_ref, dsk_ref, y_ref, hp_ref, state):
        c = pl.program_id(1)

        @pl.when(c == 0)
        def _():
            state[...] = jnp.zeros_like(state)

        _, dt, _, acs = _ssd_common(dtr_ref, dtb_ref, alog_ref)
        acs_t = acs.T
        tri_mask = _tri(True)
        last_row = (lax.broadcasted_iota(jnp.int32, (CHUNK, 1), 0) == CHUNK - 1).astype(F32)
        for g in range(SSM_GROUPS):
            b16 = _bf(act_ref[0, :, _group_cols(g, 0)])
            c16 = _bf(act_ref[0, :, _group_cols(g, 1)])
            cb = _nt(c16, b16)
            for hs in _head_batches(g, SSD_INTERLEAVE_FWD):
                x = [act_ref[0, :, _head_cols(h)] for h in hs]
                dt_j = [dt[:, h:h + 1] for h in hs]
                acs_j = [acs[:, h:h + 1] for h in hs]
                acs_last = [_colsum(a * last_row) for a in acs_j]
                xg = _each(lambda xv, d: xv * d, x, dt_j)
                mm = [cb * _decay_matrix(a, acs_t[h:h + 1, :], tri_mask) for a, h in zip(acs_j, hs)]
                decay_s = _each(lambda al, a: jnp.exp(al - a), acs_last, acs_j)
                y_diag = _each(lambda m_, v: _nn(_bf(m_), _bf(v)), mm, xg)
                st = _each(lambda v, d: _tn(_bf(v * d), b16), xg, decay_s)
                hp = [state[h] for h in hs]
                for h, v in zip(hs, hp):
                    hp_ref[0, h, 0] = v
                y_off = [_nt(c16, _bf(v)) for v in hp]
                for h, yd, yo, a, xv in zip(hs, y_diag, y_off, acs_j, x):
                    y_ref[0, :, _head_cols(h)] = yd + yo * jnp.exp(a) + dsk_ref[:, h:h + 1] * xv
                for h, v, al, sv in zip(hs, hp, acs_last, st):
                    state[h] = v * jnp.exp(al) + sv

    call = SideCopy(side, n_in=5, n_out=2, grid=(b, nc))
    return pl.pallas_call(
        call.wrap(body), name=name, grid=(b, nc),
        in_specs=[act_spec, dt_in_spec, par_spec, par_spec, par_spec] + call.in_specs,
        out_specs=[y_spec, h_spec] + call.out_specs,
        out_shape=[jax.ShapeDtypeStruct((b, s, SSM_INNER), F32),
                   jax.ShapeDtypeStruct((b, SSM_HEADS, nc, SSM_P, D_STATE), F32)] + call.out_shape,
        scratch_shapes=[pltpu.VMEM((SSM_HEADS, SSM_P, D_STATE), F32)] + call.scratch,
        compiler_params=_params(("arbitrary", "arbitrary")),
    )(act3, proj3, dtb, alog, dsk, *call.args)


def ssd_bwd(act3, proj3, dtb, alog, dsk, hprev, dy3, *, name, side=None):
    b, s, _ = act3.shape
    nc = s // CHUNK
    act_spec, y_spec, dt_in_spec, dt_out_spec, par_spec, h_spec = _ssd_specs(nc, True)
    dpar_spec = pl.BlockSpec((8, LANE), lambda bi, c: (0, 0))

    def body(act_ref, dtr_ref, dtb_ref, alog_ref, dsk_ref, hp_ref, dy_ref, dact_ref, ddtr_ref, dpar_ref, dstate):
        bi, c = pl.program_id(0), pl.program_id(1)

        @pl.when(c == 0)
        def _():
            dstate[...] = jnp.zeros_like(dstate)

        @pl.when((bi == 0) & (c == 0))
        def _():
            dpar_ref[...] = jnp.zeros_like(dpar_ref)

        z, dt, aneg, acs = _ssd_common(dtr_ref, dtb_ref, alog_ref)
        acs_t = acs.T
        tri_mask = _tri(True)
        last_row = (lax.broadcasted_iota(jnp.int32, (CHUNK, 1), 0) == CHUNK - 1).astype(F32)
        lanes = lax.broadcasted_iota(jnp.int32, (1, LANE), 1)
        sublanes = lax.broadcasted_iota(jnp.int32, (LANE, 1), 0)
        ddt_mat = jnp.zeros((CHUNK, LANE), F32)
        dacs_mat = jnp.zeros((CHUNK, LANE), F32)
        dacs_rows = jnp.zeros((LANE, CHUNK), F32)
        ddsk_row = jnp.zeros((1, LANE), F32)
        for g in range(SSM_GROUPS):
            b16 = _bf(act_ref[0, :, _group_cols(g, 0)])
            c16 = _bf(act_ref[0, :, _group_cols(g, 1)])
            cb = _nt(c16, b16)
            dcb = jnp.zeros((CHUNK, CHUNK), F32)
            db_acc = jnp.zeros((CHUNK, D_STATE), F32)
            dc_acc = jnp.zeros((CHUNK, D_STATE), F32)
            for hs in _head_batches(g):
                x = [act_ref[0, :, _head_cols(h)] for h in hs]
                g_y = [dy_ref[0, :, _head_cols(h)] for h in hs]
                hp = [hp_ref[0, h, 0] for h in hs]
                g_hn = [dstate[h] for h in hs]
                dt_j = [dt[:, h:h + 1] for h in hs]
                acs_j = [acs[:, h:h + 1] for h in hs]
                acs_last = [_colsum(a * last_row) for a in acs_j]
                xg = _each(lambda xv, d: xv * d, x, dt_j)
                lm = [_decay_matrix(a, acs_t[h:h + 1, :], tri_mask) for a, h in zip(acs_j, hs)]
                mm = [cb * l for l in lm]
                decay_s = _each(lambda al, a: jnp.exp(al - a), acs_last, acs_j)
                ea = [jnp.exp(a) for a in acs_j]
                cd = [jnp.exp(al) for al in acs_last]
                g_y16, xg16, hp16, g_hn16 = [[_bf(v) for v in vs] for vs in (g_y, xg, hp, g_hn)]
                d_mm = _each(_nt, g_y16, xg16)
                d_xg = _each(lambda m_, gy: _tn(_bf(m_), gy), mm, g_y16)
                d_dm = _each(lambda a, m_: a * m_, d_mm, mm)
                d_acs = [_rowsum(v) for v in d_dm]
                t_off = [_nt(c16, v) for v in hp16]
                d_t16 = _each(lambda gy, e: _bf(gy * e), g_y, ea)
                d_acs = _each(lambda da, gy, t, e: da + _rowsum(gy * t) * e, d_acs, g_y, t_off, ea)
                d_hp = _each(lambda dtv, gh, cdv: _tn(dtv, c16) + gh * cdv, d_t16, g_hn, cd)
                d_w = [_nt(b16, v) for v in g_hn16]
                d_xg = _each(lambda dx, dw, ds: dx + dw * ds, d_xg, d_w, decay_s)
                d_ds = _each(lambda dw, v, ds: _rowsum(dw * v) * ds, d_w, xg, decay_s)
                d_last = _each(lambda gh, hv, cdv, dd: _colsum(_rowsum(gh * hv)) * cdv + _colsum(dd), g_hn, hp, cd, d_ds)
                d_acs = _each(lambda da, dd, dl: da - dd + dl * last_row, d_acs, d_ds, d_last)
                for h, gy, dx, d, xv in zip(hs, g_y, d_xg, dt_j, x):
                    dact_ref[0, :, _head_cols(h)] = dsk_ref[:, h:h + 1] * gy + dx * d
                for h, v in zip(hs, d_hp):
                    dstate[h] = v
                for k, h in enumerate(hs):
                    onehot = (lanes == h).astype(F32)
                    dcb = dcb + d_mm[k] * lm[k]
                    dc_acc = dc_acc + _nn(d_t16[k], hp16[k])
                    db_acc = db_acc + _nn(_bf(xg[k] * decay_s[k]), g_hn16[k])
                    ddsk_row = ddsk_row + _colsum(_rowsum(g_y[k] * x[k])) * onehot
                    ddt_mat = ddt_mat + _rowsum(d_xg[k] * x[k]) * onehot
                    dacs_mat = dacs_mat + d_acs[k] * onehot
                    dacs_rows = dacs_rows + (sublanes == h).astype(F32) * _colsum(d_dm[k])
            dcb16 = _bf(dcb)
            dact_ref[0, :, _group_cols(g, 1)] = dc_acc + _nn(dcb16, b16)
            dact_ref[0, :, _group_cols(g, 0)] = db_acc + _tn(dcb16, c16)
        d_a = _dot01_left(_tri(False).astype(BF16), dacs_mat - dacs_rows.T)
        ddt_mat = ddt_mat + d_a * aneg
        d_raw = ddt_mat * jax.nn.sigmoid(z)
        ddtr_ref[0] = d_raw
        dpar_ref[0:1, :] += _colsum(d_raw)
        dpar_ref[1:2, :] += _colsum(d_a * dt) * aneg
        dpar_ref[2:3, :] += ddsk_row

    call = SideCopy(side, n_in=7, n_out=3, grid=(b, nc))
    return pl.pallas_call(
        call.wrap(body), name=name, grid=(b, nc),
        in_specs=[act_spec, dt_in_spec, par_spec, par_spec, par_spec, h_spec, y_spec] + call.in_specs,
        out_specs=[act_spec, dt_out_spec, dpar_spec] + call.out_specs,
        out_shape=[jax.ShapeDtypeStruct(act3.shape, F32), jax.ShapeDtypeStruct((b, s, LANE), F32),
                   jax.ShapeDtypeStruct((8, LANE), F32)] + call.out_shape,
        scratch_shapes=[pltpu.VMEM((SSM_HEADS, SSM_P, D_STATE), F32)] + call.scratch,
        compiler_params=_params(("arbitrary", "arbitrary")),
    )(act3, proj3, dtb, alog, dsk, hprev, dy3, *call.args)


def rotary_tables(positions):
    inv_freq = ROPE_THETA ** (-jnp.arange(0, ROPE_DIM, 2, dtype=F32) / ROPE_DIM)
    ang = positions.astype(F32)[..., None] * inv_freq
    cos, sin = jnp.cos(ang), jnp.sin(ang)
    rest = HEAD_DIM - ROPE_DIM
    cosf = jnp.concatenate([cos, cos, jnp.ones(cos.shape[:2] + (rest,), F32)], axis=-1)
    sinf = jnp.concatenate([-sin, sin, jnp.zeros(sin.shape[:2] + (rest,), F32)], axis=-1)
    return cosf, sinf


def w_in_columns(w):
    pad = jnp.zeros((w.shape[0], IN_PAD - IN_PROJ), w.dtype)
    return jnp.concatenate([w[:, :Q_END], w[:, V_END:XBC_END], w[:, Q_END:V_END], w[:, XBC_END:], pad], axis=1)


def w_in_grad_columns(g):
    return jnp.concatenate([g[:, :Z_COL], g[:, K_COL:DT_COL], g[:, Z_COL:K_COL], g[:, DT_COL:DT_COL + SSM_HEADS]], axis=1)


def lane_pad(v):
    return jnp.pad(v.reshape(1, -1), ((0, 0), (0, LANE - v.shape[-1])))


def layer_fwd(h, wts, small, rope_tab, b, s, tag, attn_side=None, rest_from=None, ssd_side=None):
    w_in = wts[0]
    t = b * s
    sv = {"h": h}
    hn = rowwise_fwd(rms_fn, [h], [small["norm_mix"]], [BF16], name=f"rms_mix_{tag}")[0]
    proj = matmul(hn, w_in, name=f"in_proj_{tag}")
    sv["hn"], sv["proj"] = hn, proj
    proj3 = proj.reshape(b, s, IN_PAD)
    attn3, lse3, *attn_out = attn_fwd(proj3, rope_tab, name=f"attn_{tag}", side=attn_side)
    if rest_from is not None:
        wts = (w_in,) + tuple(rest_from(attn_out))
    _, w_out, w_gate, w_up, w_down = wts
    sv["attn3"], sv["lse3"] = attn3, lse3
    attn = attn3.reshape(t, ATTN_WIDTH)
    act3 = conv_silu_fwd(proj3, small["conv_w"], small["conv_b"], name=f"conv_{tag}")
    y3, hprev, *ssd_out = ssd_fwd(act3, proj3, small["dt_bias"], small["a_log"], small["d_skip"], name=f"ssd_{tag}",
                                  side=ssd_side)
    y = y3.reshape(t, SSM_INNER)
    sv["act3"], sv["hprev"], sv["y"] = act3, hprev, y
    gn = rowwise_fwd(gated_norm_fn, [y, proj], [small["ssm_norm"]], [BF16], name=f"gated_norm_{tag}", groups=SSM_GROUPS,
                     windows=[None, (Z_COL, SSM_INNER)])[0]
    sv["gn"] = gn
    h1 = matmul([attn, gn], w_out, name=f"out_proj_{tag}", residual=h)
    sv["h1"] = h1
    hn2 = rowwise_fwd(rms_fn, [h1], [small["norm_ffn"]], [BF16], name=f"rms_ffn_{tag}")[0]
    gate = matmul(hn2, w_gate, out_dtype=BF16, name=f"ffn_gate_{tag}")
    up = matmul(hn2, w_up, out_dtype=BF16, name=f"ffn_up_{tag}")
    act2 = rowwise_fwd(swiglu_fn, [gate, up], [], [BF16], name=f"swiglu_{tag}")[0]
    sv["hn2"], sv["gate"], sv["up"], sv["act2"] = hn2, gate, up, act2
    h2 = matmul(act2, w_down, name=f"ffn_down_{tag}", residual=h1)
    return h2, sv, wts, (ssd_out or None)


def layer_bwd(dh2, sv, wts, small, rope_tab, b, s, tag, ssd_side=None, attn_side_fn=None):
    w_in, w_out, w_gate, w_up, w_down = wts
    t = b * s
    gr = {}
    d_act2 = matmul(dh2, w_down, tb=True, out_dtype=BF16, name=f"ffn_down_dx_{tag}")
    gr["w_down"] = matmul(sv["act2"], dh2, ta=True, out_dtype=BF16, name=f"ffn_down_dw_{tag}")
    d_gate, d_up = rowwise_bwd(swiglu_fn, [sv["gate"], sv["up"]], [], [d_act2], [BF16, BF16], name=f"swiglu_bwd_{tag}")
    gr["w_gate"] = matmul(sv["hn2"], d_gate, ta=True, out_dtype=BF16, name=f"ffn_gate_dw_{tag}")
    gr["w_up"] = matmul(sv["hn2"], d_up, ta=True, out_dtype=BF16, name=f"ffn_up_dw_{tag}")
    d_hn2 = matmul(d_gate, w_gate, tb=True, name=f"ffn_gate_dx_{tag}")
    d_hn2 = matmul(d_up, w_up, tb=True, residual=d_hn2, name=f"ffn_up_dx_{tag}")
    dh1, gr["norm_ffn"] = rowwise_bwd(rms_fn, [sv["h1"]], [small["norm_ffn"]], [d_hn2], [F32],
                                      name=f"rms_ffn_bwd_{tag}", add_to_first=dh2)
    d_cat = matmul(dh1, w_out, tb=True, name=f"out_proj_dx_{tag}")
    gr["w_out"] = jnp.concatenate([
        matmul(sv["attn3"].reshape(t, ATTN_WIDTH), dh1, ta=True, out_dtype=BF16, name=f"out_proj_dw_attn_{tag}"),
        matmul(sv["gn"], dh1, ta=True, out_dtype=BF16, name=f"out_proj_dw_ssd_{tag}")], axis=0)
    d_y, d_z, gr["ssm_norm"] = rowwise_bwd(gated_norm_fn, [sv["y"], sv["proj"]], [small["ssm_norm"]], [d_cat], [F32, BF16],
                                           name=f"gated_norm_bwd_{tag}", groups=SSM_GROUPS,
                                           windows=[None, (Z_COL, SSM_INNER)], ct_windows=[(ATTN_WIDTH, SSM_INNER)])
    proj3 = sv["proj"].reshape(b, s, IN_PAD)
    d_act3, d_dtr, d_par, *ssd_out = ssd_bwd(sv["act3"], proj3, small["dt_bias"], small["a_log"], small["d_skip"],
                                             sv["hprev"], d_y.reshape(b, s, SSM_INNER), name=f"ssd_bwd_{tag}", side=ssd_side)
    gr["dt_bias"], gr["a_log"], gr["d_skip"] = d_par[0, :SSM_HEADS], d_par[1, :SSM_HEADS], d_par[2, :SSM_HEADS]
    d_xbc, gr["conv_w"], gr["conv_b"] = conv_silu_bwd(proj3, small["conv_w"], small["conv_b"], d_act3,
                                                      name=f"conv_bwd_{tag}")
    attn_side = attn_side_fn(gr) if attn_side_fn is not None else None
    d_q3, d_kv3, *attn_out = attn_bwd(proj3, rope_tab, sv["attn3"], sv["lse3"], d_cat.reshape(b, s, MIX_WIDTH),
                                      name=f"attn_bwd_{tag}", side=attn_side)
    d_proj = [d_q3.reshape(t, ATTN_WIDTH), d_z, d_xbc.reshape(t, CONV_CH), d_kv3.reshape(t, 2 * LANE),
              d_dtr.reshape(t, LANE)]
    d_hn = matmul(d_proj, w_in, tb=True, name=f"in_proj_dx_{tag}")
    gr["w_in"] = w_in_grad_columns(jnp.concatenate(
        [matmul(sv["hn"], part, ta=True, out_dtype=BF16, name=f"in_proj_dw_{k}_{tag}") for k, part in enumerate(d_proj)],
        axis=1))
    dh, gr["norm_mix"] = rowwise_bwd(rms_fn, [sv["h"]], [small["norm_mix"]], [d_hn], [F32],
                                     name=f"rms_mix_bwd_{tag}", add_to_first=dh1)
    return dh, gr, (ssd_out or None), (attn_out or None)


def local_step(x, positions, big, small_all, final_norm, loss_target, *, plan=None):
    b, s, _ = x.shape
    t = b * s
    rope_tab = jnp.concatenate(rotary_tables(positions), axis=-1)
    h = x.reshape(t, D_MODEL)
    saved, big = [], list(big)
    for l in range(DEPTH):
        kw = {}
        if plan is not None and l == 0:
            kw = dict(attn_side=(plan["rest0"], False), rest_from=plan["make_rest0"], ssd_side=(plan["late"], False))
        h, sv, big[l], got = layer_fwd(h, big[l], small_all[l], rope_tab, b, s, f"l{l}", **kw)
        if got is not None:
            big[DEPTH - 1] = plan["make_late"](got)
        saved.append(sv)
    dh, d_final, loss = loss_and_grad(h, loss_target.reshape(t, D_MODEL), final_norm.reshape(1, D_MODEL))
    grads, received = [None] * DEPTH, {}
    for l in reversed(range(DEPTH)):
        kw = {}
        if plan is not None and l == 0:
            kw = dict(ssd_side=(plan["grads_late"](grads[DEPTH - 1]), True),
                      attn_side_fn=lambda gr: (plan["grads_rest0"](gr), True))
        dh, grads[l], got_ssd, got_attn = layer_bwd(dh, saved[l], big[l], small_all[l], rope_tab, b, s, f"l{l}", **kw)
        if got_ssd is not None:
            received["late"] = got_ssd
        if got_attn is not None:
            received["rest0"] = got_attn
    return loss, dh.reshape(b, s, D_MODEL), grads, d_final, received


def _slab_rows(r):
    return r if r <= 512 else _pick(r, (512, 352, 256, 128, 8))


def cast_bf16(x, *, name):
    def fn(v):
        return (v,)
    return rowwise_fwd(fn, [x], [], [BF16], name=name, tr=_slab_rows(x.shape[0]))[0]


def sum_slots(x, *, name):
    n, r, c = x.shape
    tr = _slab_rows(r)

    def body(x_ref, o_ref):
        acc = x_ref[0].astype(F32)
        for i in range(1, n):
            acc = acc + x_ref[i].astype(F32)
        o_ref[...] = acc

    return pl.pallas_call(
        body, name=name, grid=(r // tr,), in_specs=[pl.BlockSpec((n, tr, c), lambda i: (0, i, 0))],
        out_specs=pl.BlockSpec((tr, c), lambda i: (i, 0)), out_shape=jax.ShapeDtypeStruct((r, c), F32),
        compiler_params=_params(("parallel",)),
    )(x)


def adamw(g, w, m, v, *, name):
    r, c = w.shape
    tr = _slab_rows(r)
    bc1 = 1.0 / (1.0 - ADAM_B1 ** ADAM_STEP)
    bc2 = 1.0 / (1.0 - ADAM_B2 ** ADAM_STEP)

    def body(g_ref, w_ref, m_ref, v_ref, g_out, d_out, m_out, v_out):
        gv = g_ref[...]
        m_new = ADAM_B1 * m_ref[...] + (1.0 - ADAM_B1) * gv
        v_new = ADAM_B2 * v_ref[...] + (1.0 - ADAM_B2) * (gv * gv)
        g_out[...] = gv
        m_out[...] = m_new
        v_out[...] = v_new
        d_out[...] = -ADAM_LR * ((m_new * bc1) / (jnp.sqrt(v_new * bc2) + ADAM_EPS) + ADAM_WD * w_ref[...])

    spec = pl.BlockSpec((tr, c), lambda i: (i, 0))
    return pl.pallas_call(
        body, name=name, grid=(r // tr,), in_specs=[spec] * 4, out_specs=[spec] * 4,
        out_shape=[jax.ShapeDtypeStruct((r, c), F32)] * 4, compiler_params=_params(("parallel",)),
    )(g, w, m, v)


def adamw_layers(g_parts, w, m, v, *, name):
    depth, a, b = w.shape
    tr = _pick(a, (256, 352, 192, 128, 8))
    counts = [len(p) for p in g_parts]
    flat_parts = [q for p in g_parts for q in p]
    bc1 = 1.0 / (1.0 - ADAM_B1 ** ADAM_STEP)
    bc2 = 1.0 / (1.0 - ADAM_B2 ** ADAM_STEP)

    def body(*refs):
        layer = pl.program_id(0)
        g, off = None, 0
        for l, cnt in enumerate(counts):
            g_l = refs[off][...]
            for r_ in refs[off + 1:off + cnt]:
                g_l = g_l + r_[...]
            off += cnt
            g = g_l if g is None else jnp.where(layer == l, g_l, g)
        w_ref, m_ref, v_ref, g_out, d_out, m_out, v_out = refs[off:]
        m_new = ADAM_B1 * m_ref[0] + (1.0 - ADAM_B1) * g
        v_new = ADAM_B2 * v_ref[0] + (1.0 - ADAM_B2) * (g * g)
        g_out[0] = g
        m_out[0] = m_new
        v_out[0] = v_new
        d_out[0] = -ADAM_LR * ((m_new * bc1) / (jnp.sqrt(v_new * bc2) + ADAM_EPS) + ADAM_WD * w_ref[0])

    g_spec = pl.BlockSpec((tr, b), lambda l, i: (i, 0))
    spec = pl.BlockSpec((1, tr, b), lambda l, i: (l, i, 0))
    return pl.pallas_call(
        body, name=name, grid=(depth, a // tr), in_specs=[g_spec] * len(flat_parts) + [spec] * 3, out_specs=[spec] * 4,
        out_shape=[jax.ShapeDtypeStruct(w.shape, F32)] * 4, compiler_params=_params(("parallel", "parallel")),
    )(*flat_parts, w, m, v)


SWAP_PIECES = 4


def _other_chips(x, y):
    return [(1 - x, y), (x, 1 - y), (1 - x, 1 - y)]


def allgather_chips(shards):
    n_arr = len(shards)

    def body(*refs):
        in_refs, out_refs = refs[:n_arr], refs[n_arr:2 * n_arr]
        send_sems, recv_sems, local_sems = refs[2 * n_arr:]
        x, y, c = lax.axis_index("x"), lax.axis_index("y"), lax.axis_index("c")
        chip = 2 * x + y
        started = []
        for a, (in_ref, out_ref) in enumerate(zip(in_refs, out_refs)):
            mine = pltpu.make_async_copy(in_ref, out_ref.at[chip], local_sems.at[a])
            mine.start()
            started.append(mine.wait)
            for k, (px, py) in enumerate(_other_chips(x, y)):
                cp = pltpu.make_async_remote_copy(src_ref=in_ref, dst_ref=out_ref.at[chip], send_sem=send_sems.at[3 * a + k],
                                                  recv_sem=recv_sems.at[3 * a + k], device_id=(px, py, c), device_id_type=MESH)
                cp.start()
                started.append(cp.wait_send)
        for a, (in_ref, out_ref) in enumerate(zip(in_refs, out_refs)):
            for k, (px, py) in enumerate(_other_chips(x, y)):
                pltpu.make_async_remote_copy(src_ref=in_ref, dst_ref=out_ref.at[2 * px + py], send_sem=send_sems.at[3 * a + k],
                                             recv_sem=recv_sems.at[3 * a + k], device_id=(px, py, c),
                                             device_id_type=MESH).wait_recv()
        for wait in started:
            wait()

    hbm = pl.BlockSpec(memory_space=pltpu.HBM)
    return pl.pallas_call(
        body, name="allgather_weights", in_specs=[hbm] * n_arr, out_specs=[hbm] * n_arr,
        out_shape=[jax.ShapeDtypeStruct((N_CHIPS,) + s.shape, s.dtype) for s in shards],
        scratch_shapes=[pltpu.SemaphoreType.DMA((3 * n_arr,)), pltpu.SemaphoreType.DMA((3 * n_arr,)),
                        pltpu.SemaphoreType.DMA((n_arr,))],
    )(*shards)


def exchange_grads(big, small):
    def body(big_ref, small_ref, big_out, small_out, send_sems, recv_sems, local_sems):
        x, y, c = lax.axis_index("x"), lax.axis_index("y"), lax.axis_index("c")
        chip = 2 * x + y
        dev = 4 * x + 2 * y + c
        own_big = pltpu.make_async_copy(big_ref.at[chip], big_out.at[chip], local_sems.at[0])
        own_small = pltpu.make_async_copy(small_ref, small_out.at[dev], local_sems.at[1])
        own_big.start()
        own_small.start()
        sends = []
        for k, (px, py) in enumerate(_other_chips(x, y)):
            cp = pltpu.make_async_remote_copy(src_ref=big_ref.at[2 * px + py], dst_ref=big_out.at[chip],
                                              send_sem=send_sems.at[k], recv_sem=recv_sems.at[k],
                                              device_id=(px, py, c), device_id_type=MESH)
            cp.start()
            sends.append(cp)
        peers = []
        for r in range(1, N_DEV):
            fx, fy, fc = (r >> 2) & 1, (r >> 1) & 1, r & 1
            px, py, pc = (x + fx) % 2, (y + fy) % 2, (c + fc) % 2
            peers.append((px, py, pc))
            cp = pltpu.make_async_remote_copy(src_ref=small_ref, dst_ref=small_out.at[dev], send_sem=send_sems.at[2 + r],
                                              recv_sem=recv_sems.at[2 + r], device_id=(px, py, pc), device_id_type=MESH)
            cp.start()
            sends.append(cp)
        for k, (px, py) in enumerate(_other_chips(x, y)):
            pltpu.make_async_remote_copy(src_ref=big_ref.at[chip], dst_ref=big_out.at[2 * px + py],
                                         send_sem=send_sems.at[k], recv_sem=recv_sems.at[k],
                                         device_id=(px, py, c), device_id_type=MESH).wait_recv()
        for r, (px, py, pc) in zip(range(1, N_DEV), peers):
            pltpu.make_async_remote_copy(src_ref=small_ref, dst_ref=small_out.at[4 * px + 2 * py + pc],
                                         send_sem=send_sems.at[2 + r], recv_sem=recv_sems.at[2 + r],
                                         device_id=(px, py, pc), device_id_type=MESH).wait_recv()
        for cp in sends:
            cp.wait_send()
        own_big.wait()
        own_small.wait()

    hbm = pl.BlockSpec(memory_space=pltpu.HBM)
    n_sem = 3 + N_DEV - 1
    return pl.pallas_call(
        body, name="exchange_grads", in_specs=[hbm, hbm], out_specs=[hbm, hbm],
        out_shape=[jax.ShapeDtypeStruct(big.shape, big.dtype), jax.ShapeDtypeStruct((N_DEV,) + small.shape, small.dtype)],
        scratch_shapes=[pltpu.SemaphoreType.DMA((n_sem,)), pltpu.SemaphoreType.DMA((n_sem,)), pltpu.SemaphoreType.DMA((2,))],
    )(big, small)


def swap_cores_list(arrays):
    n = len(arrays)

    def body(*refs):
        ins, outs, (send_sems, recv_sems) = refs[:n], refs[n:2 * n], refs[2 * n:]
        x, y, c = lax.axis_index("x"), lax.axis_index("y"), lax.axis_index("c")
        copies = []
        for k in range(n):
            rows = ins[k].shape[0] // SWAP_PIECES
            for p in range(SWAP_PIECES):
                part = pl.ds(p * rows, rows)
                copies.append(pltpu.make_async_remote_copy(
                    src_ref=ins[k].at[part], dst_ref=outs[k].at[part], send_sem=send_sems.at[k * SWAP_PIECES + p],
                    recv_sem=recv_sems.at[k * SWAP_PIECES + p], device_id=(x, y, 1 - c), device_id_type=MESH))
        for cp in copies:
            cp.start()
        for cp in copies:
            cp.wait_recv()
        for cp in copies:
            cp.wait_send()

    assert all(a.shape[0] % (8 * SWAP_PIECES) == 0 for a in arrays)
    hbm = pl.BlockSpec(memory_space=pltpu.HBM)
    return pl.pallas_call(
        body, name="swap_cores", in_specs=[hbm] * n, out_specs=[hbm] * n,
        out_shape=[jax.ShapeDtypeStruct(a.shape, a.dtype) for a in arrays],
        scratch_shapes=[pltpu.SemaphoreType.DMA((n * SWAP_PIECES,)), pltpu.SemaphoreType.DMA((n * SWAP_PIECES,))],
    )(*arrays)


BIG_NAMES = ("w_in", "w_out", "w_gate", "w_up", "w_down")
BIG_SHARD_AXIS = {"w_in": 1, "w_out": 0, "w_gate": 1, "w_up": 1, "w_down": 0}
SMALL_NAMES = ("norm_mix", "conv_w", "conv_b", "dt_bias", "a_log", "d_skip", "ssm_norm", "norm_ffn")


def pack_small(parts):
    flat = jnp.concatenate([p.reshape(-1).astype(F32) for p in parts])
    rows = -(-flat.size // LANE)
    rows = -(-rows // 8) * 8
    return jnp.pad(flat, (0, rows * LANE - flat.size)).reshape(rows, LANE)


def unpack_small(packed, like):
    out, off = [], 0
    flat = packed.reshape(-1)
    for a in like:
        out.append(flat[off:off + a.size].reshape(a.shape))
        off += a.size
    return out


def kernel(x, positions, norm_mix, w_in, conv_w, conv_b, dt_bias, a_log, d_skip, ssm_norm, w_out, norm_ffn, w_gate, w_up, w_down, final_norm, loss_target, m_norm_mix, m_w_in, m_conv_w, m_conv_b, m_dt_bias, m_a_log, m_d_skip, m_ssm_norm, m_w_out, m_norm_ffn, m_w_gate, m_w_up, m_w_down, m_final_norm, v_norm_mix, v_w_in, v_conv_w, v_conv_b, v_dt_bias, v_a_log, v_d_skip, v_ssm_norm, v_w_out, v_norm_ffn, v_w_gate, v_w_up, v_w_down, v_final_norm):
    chip = 2 * lax.axis_index("x") + lax.axis_index("y")
    w_sh = {"w_in": w_in, "w_out": w_out, "w_gate": w_gate, "w_up": w_up, "w_down": w_down}
    m_sh = {"w_in": m_w_in, "w_out": m_w_out, "w_gate": m_w_gate, "w_up": m_w_up, "w_down": m_w_down}
    v_sh = {"w_in": v_w_in, "w_out": v_w_out, "w_gate": v_w_gate, "w_up": v_w_up, "w_down": v_w_down}
    assert DEPTH == 2
    rest = BIG_NAMES[1:]

    w16 = {n: cast_bf16(w_sh[n].reshape(-1, w_sh[n].shape[-1]), name=f"cast_{n}").reshape(w_sh[n].shape) for n in BIG_NAMES}

    def joined(n, gathered):
        if BIG_SHARD_AXIS[n] == 0:
            full = gathered.reshape(-1, gathered.shape[-1])
        else:
            full = jnp.concatenate([gathered[j] for j in range(N_CHIPS)], axis=1)
        return w_in_columns(full) if n == "w_in" else full

    def per_chip(n, g):
        if BIG_SHARD_AXIS[n] == 0:
            return g.reshape(N_CHIPS, -1, g.shape[-1])
        return jnp.stack(jnp.split(g, N_CHIPS, axis=1))

    conv_cols = CONV_CH // N_CHIPS
    gathered_in0, conv_g = allgather_chips([w16["w_in"][0], conv_w.reshape(-1, LANE)])
    big = [(joined("w_in", gathered_in0),) + (None,) * len(rest), None]
    early = {}

    def make_rest0(gs):
        early["w_in"] = gs[len(rest)]
        return tuple(joined(n, g) for n, g in zip(rest, gs))

    plan = {
        "rest0": [w16[n][0] for n in rest] + [w16["w_in"][DEPTH - 1]],
        "make_rest0": make_rest0,
        "late": [w16[n][DEPTH - 1] for n in rest],
        "make_late": lambda gs: (joined("w_in", early["w_in"]),) + tuple(joined(n, g) for n, g in zip(rest, gs)),
        "grads_late": lambda gr: [per_chip(n, gr[n]) for n in BIG_NAMES],
        "grads_rest0": lambda gr: [per_chip(n, gr[n]) for n in rest],
    }
    conv_w_full = jnp.concatenate([conv_g[j].reshape(DEPTH, CONV_WIDTH, conv_cols) for j in range(N_CHIPS)], axis=2)
    small_all = []
    for l in range(DEPTH):
        small_all.append({
            "norm_mix": norm_mix[l].reshape(1, -1), "conv_w": conv_w_full[l], "conv_b": conv_b[l].reshape(1, -1),
            "dt_bias": lane_pad(dt_bias[l]), "a_log": lane_pad(a_log[l]), "d_skip": lane_pad(d_skip[l]),
            "ssm_norm": ssm_norm[l].reshape(1, -1), "norm_ffn": norm_ffn[l].reshape(1, -1)})

    loss_part, grad_x, grads, d_final, received = local_step(x, positions, big, small_all, final_norm, loss_target, plan=plan)

    small_parts = [jnp.stack([grads[l][n].reshape(-1) for l in range(DEPTH)]) for n in SMALL_NAMES]
    small_parts += [d_final.reshape(-1), loss_part.reshape(-1)]
    recv_in0, recv_small = exchange_grads(per_chip("w_in", grads[0]["w_in"]), pack_small(small_parts))
    recv = [dict(zip(BIG_NAMES, [recv_in0] + list(received["rest0"]))), dict(zip(BIG_NAMES, received["late"]))]
    keys = [(l, n) for l in range(DEPTH) for n in BIG_NAMES]
    mine = {(l, n): sum_slots(recv[l][n], name=f"sum_partials_{n}_l{l}") for l, n in keys}
    other = dict(zip(keys, swap_cores_list([mine[k] for k in keys])))

    g_big, d_big, m_big, v_big = {}, {}, {}, {}
    for n in BIG_NAMES:
        g_big[n], d_big[n], m_big[n], v_big[n] = adamw_layers([[mine[(l, n)], other[(l, n)]] for l in range(DEPTH)],
                                                              w_sh[n], m_sh[n], v_sh[n], name=f"adamw_{n}")

    small_sum = sum_slots(recv_small, name="sum_small")
    like = [norm_mix, conv_w_full, conv_b, dt_bias, a_log, d_skip, ssm_norm, norm_ffn, final_norm, loss_part.reshape(-1)]
    g_small = unpack_small(small_sum, like)
    loss = g_small[-1][0]
    g_small = dict(zip(SMALL_NAMES + ("final_norm",), g_small[:-1]))
    g_small["conv_w"] = lax.dynamic_slice_in_dim(g_small["conv_w"], chip * conv_cols, conv_cols, axis=2)
    w_small = {"norm_mix": norm_mix, "conv_w": conv_w, "conv_b": conv_b, "dt_bias": dt_bias, "a_log": a_log, "d_skip": d_skip,
               "ssm_norm": ssm_norm, "norm_ffn": norm_ffn, "final_norm": final_norm}
    m_small = {"norm_mix": m_norm_mix, "conv_w": m_conv_w, "conv_b": m_conv_b, "dt_bias": m_dt_bias, "a_log": m_a_log,
               "d_skip": m_d_skip, "ssm_norm": m_ssm_norm, "norm_ffn": m_norm_ffn, "final_norm": m_final_norm}
    v_small = {"norm_mix": v_norm_mix, "conv_w": v_conv_w, "conv_b": v_conv_b, "dt_bias": v_dt_bias, "a_log": v_a_log,
               "d_skip": v_d_skip, "ssm_norm": v_ssm_norm, "norm_ffn": v_norm_ffn, "final_norm": v_final_norm}
    names = SMALL_NAMES + ("final_norm",)
    order = [w_small[n] for n in names]
    res = adamw(pack_small([g_small[n] for n in names]), pack_small(order), pack_small([m_small[n] for n in names]),
                pack_small([v_small[n] for n in names]), name="adamw_small")
    g_s, d_s, m_s, v_s = (dict(zip(names, unpack_small(a, order))) for a in res)

    all_names = ("norm_mix", "w_in", "conv_w", "conv_b", "dt_bias", "a_log", "d_skip", "ssm_norm", "w_out", "norm_ffn",
                 "w_gate", "w_up", "w_down", "final_norm")
    outs = [loss, grad_x]
    for src_big, src_small in ((g_big, g_s), (d_big, d_s), (m_big, m_s), (v_big, v_s)):
        outs += [src_big[n] if n in BIG_NAMES else src_small[n] for n in all_names]
    return tuple(outs)
```
